```python
import jax, jax.numpy as jnp
from jax import lax
import numpy as np

D_MODEL = 1024
BATCH = 8
SEQ = 4096
DEPTH = 2

GRID_W = 64
CTX_LEN = 256
CONV_W = 1024
CONV_K = 3
N_HEADS = 8
N_KV_HEADS = 2
HEAD_DIM = 128
GROUP = N_HEADS // N_KV_HEADS
Q_BLOCK = 128
ROPE_THETA = 10000.0
ROPE_AXIS_DIM = HEAD_DIM // 2
ROPE_PAIRS = ROPE_AXIS_DIM // 2
ATTN_SCALE = HEAD_DIM ** -0.5
GLA_HEADS = 4
GLA_DK = D_MODEL // 2
GLA_DV = D_MODEL
GLA_DKH = GLA_DK // GLA_HEADS
GLA_DVH = GLA_DV // GLA_HEADS
GLA_RANK = 16
GLA_TAU = 16.0
GLA_CHUNK = 64
N_BRANCH = 3
EPS = 1e-6

IN_SPLITS = (CONV_W, CONV_W, CONV_W, CONV_W,
             N_HEADS * HEAD_DIM, N_KV_HEADS * HEAD_DIM, N_KV_HEADS * HEAD_DIM,
             N_HEADS * HEAD_DIM,
             GLA_DK, GLA_DK, GLA_DV, GLA_RANK, GLA_RANK, GLA_DV,
             N_BRANCH * D_MODEL)
IN_WIDTH = sum(IN_SPLITS)

kernel_name = "hybrid_conv_gqa_gla_prefix_dit"


def _rmsnorm(x, g):
    xf = x.astype(jnp.float32)
    y = xf * lax.rsqrt(jnp.mean(xf * xf, axis=-1, keepdims=True) + EPS)
    return (y * g.astype(jnp.float32)).astype(x.dtype)


def _split_proj(p):
    offsets = np.cumsum(IN_SPLITS)[:-1].tolist()
    return jnp.split(p, offsets, axis=-1)


def _short_conv(u, w):
    up = jnp.pad(u, ((0, 0), (1, 1), (0, 0)))
    return w[0] * up[:, :-2] + w[1] * up[:, 1:-1] + w[2] * up[:, 2:]


def _axial_rope_tables(n_tokens):
    n_rows = n_tokens // GRID_W
    row = jnp.repeat(jnp.arange(n_rows, dtype=jnp.float32), GRID_W)
    col = jnp.tile(jnp.arange(GRID_W, dtype=jnp.float32), n_rows)
    freqs = ROPE_THETA ** (-jnp.arange(ROPE_PAIRS, dtype=jnp.float32) * 2.0 / ROPE_AXIS_DIM)
    ang = jnp.stack([row[:, None] * freqs, col[:, None] * freqs], axis=1)
    return jnp.cos(ang), jnp.sin(ang)


def _apply_rope(x, cos, sin):
    b_, t_, h_, _ = x.shape
    xr = x.reshape(b_, t_, h_, 2, 2, ROPE_PAIRS)
    x1, x2 = xr[..., 0, :], xr[..., 1, :]
    c_, s_ = cos[None, :, None], sin[None, :, None]
    out = jnp.stack([x1 * c_ - x2 * s_, x2 * c_ + x1 * s_], axis=-2)
    return out.reshape(b_, t_, h_, HEAD_DIM).astype(x.dtype)


def _attn_heads(q, k, v, q_g, k_g, rope):
    b_, t_ = q.shape[:2]
    q = _rmsnorm(q.reshape(b_, t_, N_HEADS, HEAD_DIM), q_g)
    k = _rmsnorm(k.reshape(b_, t_, N_KV_HEADS, HEAD_DIM), k_g)
    v = v.reshape(b_, t_, N_KV_HEADS, HEAD_DIM)
    if rope is not None:
        q = _apply_rope(q, *rope)
        k = _apply_rope(k, *rope)
    return q, k, v


def _sdpa_blocks(q, keys, vals):
    b_, t_ = q.shape[:2]
    nb = t_ // Q_BLOCK
    qb = q.reshape(b_, nb, Q_BLOCK, N_KV_HEADS, GROUP, HEAD_DIM).transpose(1, 0, 2, 3, 4, 5)

    def one_block(qblk):
        s = jnp.einsum('bqkgd,bskd->bkgqs', qblk, keys).astype(jnp.float32) * ATTN_SCALE
        p = jax.nn.softmax(s, axis=-1).astype(vals.dtype)
        return jnp.einsum('bkgqs,bskd->bqkgd', p, vals)

    o = lax.map(one_block, qb)
    return o.transpose(1, 0, 2, 3, 4, 5).reshape(b_, t_, N_HEADS * HEAD_DIM)


def _gla_inputs(q, k, v, r_f, r_b, w_df, b_df, w_db, b_db):
    b_, t_ = q.shape[:2]
    q = q.reshape(b_, t_, GLA_HEADS, GLA_DKH) * (GLA_DKH ** -0.5)
    k = k.reshape(b_, t_, GLA_HEADS, GLA_DKH)
    v = v.reshape(b_, t_, GLA_HEADS, GLA_DVH)
    la_f = (jax.nn.log_sigmoid((r_f @ w_df + b_df).astype(jnp.float32)) / GLA_TAU).reshape(b_, t_, GLA_HEADS, GLA_DKH)
    la_b = (jax.nn.log_sigmoid((r_b @ w_db + b_db).astype(jnp.float32)) / GLA_TAU).reshape(b_, t_, GLA_HEADS, GLA_DKH)
    return q, k, v, la_f, la_b


def _gla_scan(q, k, v, log_a, s0):
    b_, t_, h_, _ = q.shape
    dv = v.shape[-1]
    nc = t_ // GLA_CHUNK

    def chunks(a):
        return a.astype(jnp.float32).reshape(b_, nc, GLA_CHUNK, h_, a.shape[-1]).transpose(1, 0, 3, 2, 4)

    mask = jnp.tril(jnp.ones((GLA_CHUNK, GLA_CHUNK), dtype=bool))

    def step(s, inp):
        qc, kc, vc, ac = inp
        bcum = jnp.cumsum(ac, axis=2)
        o_inter = jnp.einsum('bhtd,bhde->bhte', qc * jnp.exp(bcum), s)
        diff = bcum[:, :, :, None, :] - bcum[:, :, None, :, :]
        decay = jnp.exp(jnp.where(mask[:, :, None], diff, -jnp.inf))
        att = jnp.einsum('bhtd,bhsd,bhtsd->bhts', qc, kc, decay)
        o_intra = jnp.einsum('bhts,bhse->bhte', att, vc)
        b_last = bcum[:, :, -1:, :]
        s_new = jnp.exp(b_last[:, :, 0, :])[..., None] * s + jnp.einsum('bhsd,bhse->bhde', kc * jnp.exp(b_last - bcum), vc)
        return s_new, o_inter + o_intra

    s_f, o = lax.scan(step, s0, (chunks(q), chunks(k), chunks(v), chunks(log_a)))
    o = o.transpose(1, 0, 3, 2, 4).reshape(b_, t_, h_, dv)
    return o.astype(v.dtype), s_f


def _flip(a):
    return jnp.flip(a, axis=1)


def _gla_bidirectional(ctx_in, lat_in):
    qc, kc, vc, lfc, lbc = ctx_in
    ql, kl, vl, lfl, lbl = lat_in
    s0 = jnp.zeros((qc.shape[0], GLA_HEADS, GLA_DKH, GLA_DVH), jnp.float32)
    o_cf, s_f = _gla_scan(qc, kc, vc, lfc, s0)
    o_cb, s_b = _gla_scan(_flip(qc), _flip(kc), _flip(vc), _flip(lbc), s0)
    o_lf, _ = _gla_scan(ql, kl, vl, lfl, s_f)
    o_lb, _ = _gla_scan(_flip(ql), _flip(kl), _flip(vl), _flip(lbl), s_b)
    return o_lf + _flip(o_lb), o_cf + _flip(o_cb)


def _mixer_output(parts, att_o, gla_o, conv_w_l, gla_g_l, w_br_conv_l, w_br_attn_l, w_br_gla_l, b_gate_l, w_out_l):
    a_b, a_c, a_x, a_z = parts[0], parts[1], parts[2], parts[3]
    z_attn, z_gla, mg = parts[7], parts[13], parts[14]
    b_, t_ = a_b.shape[:2]
    br_a = ((a_b * _short_conv(a_c * a_x, conv_w_l)) * jax.nn.silu(a_z)) @ w_br_conv_l
    br_b = (att_o * jax.nn.silu(z_attn)) @ w_br_attn_l
    br_c = (_rmsnorm(gla_o, gla_g_l).reshape(b_, t_, GLA_DV) * jax.nn.silu(z_gla)) @ w_br_gla_l
    g_a, g_b, g_c = jnp.split(jax.nn.sigmoid(mg + b_gate_l), N_BRANCH, axis=-1)
    return (g_a * br_a + g_b * br_b + g_c * br_c) @ w_out_l


def _fwd_setup_inputs(seed: int = 0) -> dict:
    key = jax.random.key(seed)
    ks = jax.random.split(key, 24)
    n = jax.random.normal
    f32 = jnp.float32
    return {
        "x": n(ks[0], (BATCH, SEQ, D_MODEL), f32),
        "c": n(ks[1], (BATCH, D_MODEL), f32),
        "ctx": n(ks[2], (BATCH, CTX_LEN, D_MODEL), f32),
        "c_ctx": n(ks[3], (D_MODEL,), f32),
        "w_ada": n(ks[4], (DEPTH, D_MODEL, 3 * D_MODEL), f32) * D_MODEL ** -0.5,
        "b_ada": 0.02 * n(ks[5], (DEPTH, 3 * D_MODEL), f32),
        "g_pre": 1.0 + 0.02 * n(ks[6], (DEPTH, D_MODEL), f32),
        "g_post": 1.0 + 0.02 * n(ks[7], (DEPTH, D_MODEL), f32),
        "w_in": n(ks[8], (DEPTH, D_MODEL, IN_WIDTH), f32) * D_MODEL ** -0.5,
        "conv_w": n(ks[9], (DEPTH, CONV_K, CONV_W), f32) * CONV_K ** -0.5,
        "q_norm_g": 1.0 + 0.02 * n(ks[10], (DEPTH, HEAD_DIM), f32),
        "k_norm_g": 1.0 + 0.02 * n(ks[11], (DEPTH, HEAD_DIM), f32),
        "w_decay_fwd": n(ks[12], (DEPTH, GLA_RANK, GLA_DK), f32) * GLA_RANK ** -0.5,
        "b_decay_fwd": 0.01 * n(ks[13], (DEPTH, GLA_DK), f32),
        "w_decay_bwd": n(ks[14], (DEPTH, GLA_RANK, GLA_DK), f32) * GLA_RANK ** -0.5,
        "b_decay_bwd": 0.01 * n(ks[15], (DEPTH, GLA_DK), f32),
        "gla_norm_g": 1.0 + 0.02 * n(ks[16], (DEPTH, GLA_DVH), f32),
        "w_br_conv": n(ks[17], (DEPTH, CONV_W, D_MODEL), f32) * CONV_W ** -0.5,
        "w_br_attn": n(ks[18], (DEPTH, N_HEADS * HEAD_DIM, D_MODEL), f32) * (N_HEADS * HEAD_DIM) ** -0.5,
        "w_br_gla": n(ks[19], (DEPTH, GLA_DV, D_MODEL), f32) * GLA_DV ** -0.5,
        "b_gate": 0.02 * n(ks[20], (DEPTH, N_BRANCH * D_MODEL), f32),
        "w_out": n(ks[21], (DEPTH, D_MODEL, D_MODEL), f32) * D_MODEL ** -0.5,
    }


def _fwd_reference(x, c, ctx, c_ctx, w_ada, b_ada, g_pre, g_post, w_in, conv_w, q_norm_g, k_norm_g,
              w_decay_fwd, b_decay_fwd, w_decay_bwd, b_decay_bwd, gla_norm_g,
              w_br_conv, w_br_attn, w_br_gla, b_gate, w_out):
    n_tok = x.shape[1]
    rope = _axial_rope_tables(n_tok)
    xc = ctx
    for l in range(DEPTH):
        last = l == DEPTH - 1
        mod_l = jax.nn.silu(c) @ w_ada[l] + b_ada[l]
        sh_l, sc_l, gt_l = jnp.split(mod_l[:, None, :], 3, axis=-1)
        mod_c = jax.nn.silu(c_ctx) @ w_ada[l] + b_ada[l]
        sh_c, sc_c, gt_c = jnp.split(mod_c, 3, axis=-1)
        h_l = _rmsnorm(x, g_pre[l]) * (1.0 + sc_l) + sh_l
        h_c = _rmsnorm(xc, g_pre[l]) * (1.0 + sc_c) + sh_c
        pl = _split_proj(h_l @ w_in[l])
        pc = _split_proj(h_c @ w_in[l])
        q_l, k_l, v_l = _attn_heads(pl[4], pl[5], pl[6], q_norm_g[l], k_norm_g[l], rope)
        q_c, k_c, v_c = _attn_heads(pc[4], pc[5], pc[6], q_norm_g[l], k_norm_g[l], None)
        keys = jnp.concatenate([k_l, k_c], axis=1)
        vals = jnp.concatenate([v_l, v_c], axis=1)
        att_l = _sdpa_blocks(q_l, keys, vals)
        dec = (w_decay_fwd[l], b_decay_fwd[l], w_decay_bwd[l], b_decay_bwd[l])
        gla_c_in = _gla_inputs(pc[8], pc[9], pc[10], pc[11], pc[12], *dec)
        gla_l_in = _gla_inputs(pl[8], pl[9], pl[10], pl[11], pl[12], *dec)
        gla_l, gla_c = _gla_bidirectional(gla_c_in, gla_l_in)
        shared = (conv_w[l], gla_norm_g[l], w_br_conv[l], w_br_attn[l], w_br_gla[l], b_gate[l], w_out[l])
        out_l = _mixer_output(pl, att_l, gla_l, *shared)
        if not last:
            att_c = _sdpa_blocks(q_c, k_c, v_c)
            out_c = _mixer_output(pc, att_c, gla_c, *shared)
            xc = xc + gt_c * _rmsnorm(out_c, g_post[l])
        x = x + gt_l * _rmsnorm(out_l, g_post[l])
    return x


import jax as _jax
import jax.numpy as _jnp

TWIN_FORMAT = 'train_step'
FWD_PARAMS = ['x', 'c', 'ctx', 'c_ctx', 'w_ada', 'b_ada', 'g_pre', 'g_post', 'w_in', 'conv_w', 'q_norm_g', 'k_norm_g', 'w_decay_fwd', 'b_decay_fwd', 'w_decay_bwd', 'b_decay_bwd', 'gla_norm_g', 'w_br_conv', 'w_br_attn', 'w_br_gla', 'b_gate', 'w_out']
TWIN_WEIGHTS = ['c_ctx', 'w_ada', 'b_ada', 'g_pre', 'g_post', 'w_in', 'conv_w', 'q_norm_g', 'k_norm_g', 'w_decay_fwd', 'b_decay_fwd', 'w_decay_bwd', 'b_decay_bwd', 'gla_norm_g', 'w_br_conv', 'w_br_attn', 'w_br_gla', 'b_gate', 'w_out']
TWIN_DIFF_INPUT = 'x'
TWIN_INPUTS = ['x', 'c', 'ctx', 'c_ctx', 'w_ada', 'b_ada', 'g_pre', 'g_post', 'w_in', 'conv_w', 'q_norm_g', 'k_norm_g', 'w_decay_fwd', 'b_decay_fwd', 'w_decay_bwd', 'b_decay_bwd', 'gla_norm_g', 'w_br_conv', 'w_br_attn', 'w_br_gla', 'b_gate', 'w_out', 'loss_target', 'm_c_ctx', 'm_w_ada', 'm_b_ada', 'm_g_pre', 'm_g_post', 'm_w_in', 'm_conv_w', 'm_q_norm_g', 'm_k_norm_g', 'm_w_decay_fwd', 'm_b_decay_fwd', 'm_w_decay_bwd', 'm_b_decay_bwd', 'm_gla_norm_g', 'm_w_br_conv', 'm_w_br_attn', 'm_w_br_gla', 'm_b_gate', 'm_w_out', 'v_c_ctx', 'v_w_ada', 'v_b_ada', 'v_g_pre', 'v_g_post', 'v_w_in', 'v_conv_w', 'v_q_norm_g', 'v_k_norm_g', 'v_w_decay_fwd', 'v_b_decay_fwd', 'v_w_decay_bwd', 'v_b_decay_bwd', 'v_gla_norm_g', 'v_w_br_conv', 'v_w_br_attn', 'v_w_br_gla', 'v_b_gate', 'v_w_out']
TWIN_OUTPUTS = ['loss', 'grad_x', 'grad_c_ctx', 'grad_w_ada', 'grad_b_ada', 'grad_g_pre', 'grad_g_post', 'grad_w_in', 'grad_conv_w', 'grad_q_norm_g', 'grad_k_norm_g', 'grad_w_decay_fwd', 'grad_b_decay_fwd', 'grad_w_decay_bwd', 'grad_b_decay_bwd', 'grad_gla_norm_g', 'grad_w_br_conv', 'grad_w_br_attn', 'grad_w_br_gla', 'grad_b_gate', 'grad_w_out', 'delta_c_ctx', 'delta_w_ada', 'delta_b_ada', 'delta_g_pre', 'delta_g_post', 'delta_w_in', 'delta_conv_w', 'delta_q_norm_g', 'delta_k_norm_g', 'delta_w_decay_fwd', 'delta_b_decay_fwd', 'delta_w_decay_bwd', 'delta_b_decay_bwd', 'delta_gla_norm_g', 'delta_w_br_conv', 'delta_w_br_attn', 'delta_w_br_gla', 'delta_b_gate', 'delta_w_out', 'new_m_c_ctx', 'new_m_w_ada', 'new_m_b_ada', 'new_m_g_pre', 'new_m_g_post', 'new_m_w_in', 'new_m_conv_w', 'new_m_q_norm_g', 'new_m_k_norm_g', 'new_m_w_decay_fwd', 'new_m_b_decay_fwd', 'new_m_w_decay_bwd', 'new_m_b_decay_bwd', 'new_m_gla_norm_g', 'new_m_w_br_conv', 'new_m_w_br_attn', 'new_m_w_br_gla', 'new_m_b_gate', 'new_m_w_out', 'new_v_c_ctx', 'new_v_w_ada', 'new_v_b_ada', 'new_v_g_pre', 'new_v_g_post', 'new_v_w_in', 'new_v_conv_w', 'new_v_q_norm_g', 'new_v_k_norm_g', 'new_v_w_decay_fwd', 'new_v_b_decay_fwd', 'new_v_w_decay_bwd', 'new_v_b_decay_bwd', 'new_v_gla_norm_g', 'new_v_w_br_conv', 'new_v_w_br_attn', 'new_v_w_br_gla', 'new_v_b_gate', 'new_v_w_out']
TWIN_LEAF_KINDS = {'loss': 'loss', 'grad_x': 'grad_x', 'grad_c_ctx': 'grad_w', 'grad_w_ada': 'grad_w', 'grad_b_ada': 'grad_w', 'grad_g_pre': 'grad_w', 'grad_g_post': 'grad_w', 'grad_w_in': 'grad_w', 'grad_conv_w': 'grad_w', 'grad_q_norm_g': 'grad_w', 'grad_k_norm_g': 'grad_w', 'grad_w_decay_fwd': 'grad_w', 'grad_b_decay_fwd': 'grad_w', 'grad_w_decay_bwd': 'grad_w', 'grad_b_decay_bwd': 'grad_w', 'grad_gla_norm_g': 'grad_w', 'grad_w_br_conv': 'grad_w', 'grad_w_br_attn': 'grad_w', 'grad_w_br_gla': 'grad_w', 'grad_b_gate': 'grad_w', 'grad_w_out': 'grad_w', 'delta_c_ctx': 'delta_w', 'delta_w_ada': 'delta_w', 'delta_b_ada': 'delta_w', 'delta_g_pre': 'delta_w', 'delta_g_post': 'delta_w', 'delta_w_in': 'delta_w', 'delta_conv_w': 'delta_w', 'delta_q_norm_g': 'delta_w', 'delta_k_norm_g': 'delta_w', 'delta_w_decay_fwd': 'delta_w', 'delta_b_decay_fwd': 'delta_w', 'delta_w_decay_bwd': 'delta_w', 'delta_b_decay_bwd': 'delta_w', 'delta_gla_norm_g': 'delta_w', 'delta_w_br_conv': 'delta_w', 'delta_w_br_attn': 'delta_w', 'delta_w_br_gla': 'delta_w', 'delta_b_gate': 'delta_w', 'delta_w_out': 'delta_w', 'new_m_c_ctx': 'new_m', 'new_m_w_ada': 'new_m', 'new_m_b_ada': 'new_m', 'new_m_g_pre': 'new_m', 'new_m_g_post': 'new_m', 'new_m_w_in': 'new_m', 'new_m_conv_w': 'new_m', 'new_m_q_norm_g': 'new_m', 'new_m_k_norm_g': 'new_m', 'new_m_w_decay_fwd': 'new_m', 'new_m_b_decay_fwd': 'new_m', 'new_m_w_decay_bwd': 'new_m', 'new_m_b_decay_bwd': 'new_m', 'new_m_gla_norm_g': 'new_m', 'new_m_w_br_conv': 'new_m', 'new_m_w_br_attn': 'new_m', 'new_m_w_br_gla': 'new_m', 'new_m_b_gate': 'new_m', 'new_m_w_out': 'new_m', 'new_v_c_ctx': 'new_v', 'new_v_w_ada': 'new_v', 'new_v_b_ada': 'new_v', 'new_v_g_pre': 'new_v', 'new_v_g_post': 'new_v', 'new_v_w_in': 'new_v', 'new_v_conv_w': 'new_v', 'new_v_q_norm_g': 'new_v', 'new_v_k_norm_g': 'new_v', 'new_v_w_decay_fwd': 'new_v', 'new_v_b_decay_fwd': 'new_v', 'new_v_w_decay_bwd': 'new_v', 'new_v_b_decay_bwd': 'new_v', 'new_v_gla_norm_g': 'new_v', 'new_v_w_br_conv': 'new_v', 'new_v_w_br_attn': 'new_v', 'new_v_w_br_gla': 'new_v', 'new_v_b_gate': 'new_v', 'new_v_w_out': 'new_v'}


def _forward(args):
    return _fwd_reference(*[args[k] for k in FWD_PARAMS])


def _output_shape():
    out = _jax.eval_shape(lambda: _forward(_fwd_setup_inputs(0)))
    return out.shape, out.dtype

N_MICROBATCH = 1
ADAM_LR = 0.001
ADAM_B1 = 0.9
ADAM_B2 = 0.999
ADAM_EPS = 1e-08
ADAM_WD = 0.01
ADAM_STEP = 10
PER_EXAMPLE_BATCH_AXIS = {'x': 0, 'c': 0, 'ctx': 0, 'loss_target': 0}
SHARED_INPUTS = []
_WEIGHT_DTYPES = {'c_ctx': _jnp.float32, 'w_ada': _jnp.float32, 'b_ada': _jnp.float32, 'g_pre': _jnp.float32, 'g_post': _jnp.float32, 'w_in': _jnp.float32, 'conv_w': _jnp.float32, 'q_norm_g': _jnp.float32, 'k_norm_g': _jnp.float32, 'w_decay_fwd': _jnp.float32, 'b_decay_fwd': _jnp.float32, 'w_decay_bwd': _jnp.float32, 'b_decay_bwd': _jnp.float32, 'gla_norm_g': _jnp.float32, 'w_br_conv': _jnp.float32, 'w_br_attn': _jnp.float32, 'w_br_gla': _jnp.float32, 'b_gate': _jnp.float32, 'w_out': _jnp.float32}
MOMENT_SCALE = {'c_ctx': 6.756457e-02, 'w_ada': 3.434767e+00, 'b_ada': 6.377071e+00, 'g_pre': 4.561244e-01, 'g_post': 1.491328e+01, 'w_in': 1.584055e-01, 'conv_w': 2.416158e-01, 'q_norm_g': 3.722110e-02, 'k_norm_g': 3.733078e-02, 'w_decay_fwd': 5.780008e-02, 'b_decay_fwd': 8.762730e-02, 'w_decay_bwd': 6.076694e-02, 'b_decay_bwd': 8.535431e-02, 'gla_norm_g': 2.736672e-01, 'w_br_conv': 2.856601e-01, 'w_br_attn': 1.550113e-01, 'w_br_gla': 1.305591e-01, 'b_gate': 7.988956e-02, 'w_out': 3.801841e-01}


def _to_microbatches(a, axis):
    t = _jnp.moveaxis(a, axis, 0)
    t = t.reshape((N_MICROBATCH, t.shape[0] // N_MICROBATCH) + t.shape[1:])
    return _jnp.moveaxis(t, 1, axis + 1)


def setup_inputs(seed: int = 0) -> dict:
    inp = _fwd_setup_inputs(seed)
    key = _jax.random.fold_in(_jax.random.key(seed), 7919)
    shape, _ = _output_shape()
    out = dict(inp)
    out["loss_target"] = _jax.random.normal(_jax.random.fold_in(key, 0), shape, _jnp.float32)
    for i, name in enumerate(TWIN_WEIGHTS):
        w = inp[name].astype(_jnp.float32)
        if MOMENT_SCALE is None:
            s = _jnp.sqrt(_jnp.mean(_jnp.square(w)) + 1e-30)
        else:
            s = MOMENT_SCALE[name]
        km, kv = _jax.random.split(_jax.random.fold_in(key, i + 1))
        out[name] = w
        out["m_" + name] = s * _jax.random.normal(km, w.shape, _jnp.float32)
        out["v_" + name] = (s * s) * _jax.random.uniform(kv, w.shape, _jnp.float32, 0.5, 1.5)
    if N_MICROBATCH > 1:
        for name, axis in PER_EXAMPLE_BATCH_AXIS.items():
            out[name] = _to_microbatches(out[name], axis)
    return {'x': out['x'], 'c': out['c'], 'ctx': out['ctx'], 'c_ctx': out['c_ctx'], 'w_ada': out['w_ada'], 'b_ada': out['b_ada'], 'g_pre': out['g_pre'], 'g_post': out['g_post'], 'w_in': out['w_in'], 'conv_w': out['conv_w'], 'q_norm_g': out['q_norm_g'], 'k_norm_g': out['k_norm_g'], 'w_decay_fwd': out['w_decay_fwd'], 'b_decay_fwd': out['b_decay_fwd'], 'w_decay_bwd': out['w_decay_bwd'], 'b_decay_bwd': out['b_decay_bwd'], 'gla_norm_g': out['gla_norm_g'], 'w_br_conv': out['w_br_conv'], 'w_br_attn': out['w_br_attn'], 'w_br_gla': out['w_br_gla'], 'b_gate': out['b_gate'], 'w_out': out['w_out'], 'loss_target': out['loss_target'], 'm_c_ctx': out['m_c_ctx'], 'm_w_ada': out['m_w_ada'], 'm_b_ada': out['m_b_ada'], 'm_g_pre': out['m_g_pre'], 'm_g_post': out['m_g_post'], 'm_w_in': out['m_w_in'], 'm_conv_w': out['m_conv_w'], 'm_q_norm_g': out['m_q_norm_g'], 'm_k_norm_g': out['m_k_norm_g'], 'm_w_decay_fwd': out['m_w_decay_fwd'], 'm_b_decay_fwd': out['m_b_decay_fwd'], 'm_w_decay_bwd': out['m_w_decay_bwd'], 'm_b_decay_bwd': out['m_b_decay_bwd'], 'm_gla_norm_g': out['m_gla_norm_g'], 'm_w_br_conv': out['m_w_br_conv'], 'm_w_br_attn': out['m_w_br_attn'], 'm_w_br_gla': out['m_w_br_gla'], 'm_b_gate': out['m_b_gate'], 'm_w_out': out['m_w_out'], 'v_c_ctx': out['v_c_ctx'], 'v_w_ada': out['v_w_ada'], 'v_b_ada': out['v_b_ada'], 'v_g_pre': out['v_g_pre'], 'v_g_post': out['v_g_post'], 'v_w_in': out['v_w_in'], 'v_conv_w': out['v_conv_w'], 'v_q_norm_g': out['v_q_norm_g'], 'v_k_norm_g': out['v_k_norm_g'], 'v_w_decay_fwd': out['v_w_decay_fwd'], 'v_b_decay_fwd': out['v_b_decay_fwd'], 'v_w_decay_bwd': out['v_w_decay_bwd'], 'v_b_decay_bwd': out['v_b_decay_bwd'], 'v_gla_norm_g': out['v_gla_norm_g'], 'v_w_br_conv': out['v_w_br_conv'], 'v_w_br_attn': out['v_w_br_attn'], 'v_w_br_gla': out['v_w_br_gla'], 'v_b_gate': out['v_b_gate'], 'v_w_out': out['v_w_out']}


def _loss(weights, diff, rest, loss_target):
    with _jax.named_scope("forward"):
        args = {**rest, TWIN_DIFF_INPUT: diff, **{k: w.astype(_WEIGHT_DTYPES[k]) for k, w in weights.items()}}
        y = _forward(args)
    with _jax.named_scope("loss_head"):
        err = _jnp.square(y.astype(_jnp.float32) - loss_target)
        return 0.5 * _jnp.sum(_jnp.mean(err, axis=-1)) if err.ndim else 0.5 * err


def _adamw(w, g, m, v):
    m = ADAM_B1 * m + (1.0 - ADAM_B1) * g
    v = ADAM_B2 * v + (1.0 - ADAM_B2) * _jnp.square(g)
    m_hat = m / (1.0 - ADAM_B1 ** ADAM_STEP)
    v_hat = v / (1.0 - ADAM_B2 ** ADAM_STEP)
    delta = -ADAM_LR * (m_hat / (_jnp.sqrt(v_hat) + ADAM_EPS) + ADAM_WD * w)
    return delta, m, v


def reference(x, c, ctx, c_ctx, w_ada, b_ada, g_pre, g_post, w_in, conv_w, q_norm_g, k_norm_g, w_decay_fwd, b_decay_fwd, w_decay_bwd, b_decay_bwd, gla_norm_g, w_br_conv, w_br_attn, w_br_gla, b_gate, w_out, loss_target, m_c_ctx, m_w_ada, m_b_ada, m_g_pre, m_g_post, m_w_in, m_conv_w, m_q_norm_g, m_k_norm_g, m_w_decay_fwd, m_b_decay_fwd, m_w_decay_bwd, m_b_decay_bwd, m_gla_norm_g, m_w_br_conv, m_w_br_attn, m_w_br_gla, m_b_gate, m_w_out, v_c_ctx, v_w_ada, v_b_ada, v_g_pre, v_g_post, v_w_in, v_conv_w, v_q_norm_g, v_k_norm_g, v_w_decay_fwd, v_b_decay_fwd, v_w_decay_bwd, v_b_decay_bwd, v_gla_norm_g, v_w_br_conv, v_w_br_attn, v_w_br_gla, v_b_gate, v_w_out):
    given = dict(x=x, c=c, ctx=ctx, c_ctx=c_ctx, w_ada=w_ada, b_ada=b_ada, g_pre=g_pre, g_post=g_post, w_in=w_in, conv_w=conv_w, q_norm_g=q_norm_g, k_norm_g=k_norm_g, w_decay_fwd=w_decay_fwd, b_decay_fwd=b_decay_fwd, w_decay_bwd=w_decay_bwd, b_decay_bwd=b_decay_bwd, gla_norm_g=gla_norm_g, w_br_conv=w_br_conv, w_br_attn=w_br_attn, w_br_gla=w_br_gla, b_gate=b_gate, w_out=w_out, loss_target=loss_target, m_c_ctx=m_c_ctx, m_w_ada=m_w_ada, m_b_ada=m_b_ada, m_g_pre=m_g_pre, m_g_post=m_g_post, m_w_in=m_w_in, m_conv_w=m_conv_w, m_q_norm_g=m_q_norm_g, m_k_norm_g=m_k_norm_g, m_w_decay_fwd=m_w_decay_fwd, m_b_decay_fwd=m_b_decay_fwd, m_w_decay_bwd=m_w_decay_bwd, m_b_decay_bwd=m_b_decay_bwd, m_gla_norm_g=m_gla_norm_g, m_w_br_conv=m_w_br_conv, m_w_br_attn=m_w_br_attn, m_w_br_gla=m_w_br_gla, m_b_gate=m_b_gate, m_w_out=m_w_out, v_c_ctx=v_c_ctx, v_w_ada=v_w_ada, v_b_ada=v_b_ada, v_g_pre=v_g_pre, v_g_post=v_g_post, v_w_in=v_w_in, v_conv_w=v_conv_w, v_q_norm_g=v_q_norm_g, v_k_norm_g=v_k_norm_g, v_w_decay_fwd=v_w_decay_fwd, v_b_decay_fwd=v_b_decay_fwd, v_w_decay_bwd=v_w_decay_bwd, v_b_decay_bwd=v_b_decay_bwd, v_gla_norm_g=v_gla_norm_g, v_w_br_conv=v_w_br_conv, v_w_br_attn=v_w_br_attn, v_w_br_gla=v_w_br_gla, v_b_gate=v_b_gate, v_w_out=v_w_out)
    weights = {n: given[n] for n in TWIN_WEIGHTS}
    shared = {n: given[n] for n in SHARED_INPUTS}
    per_example = {n: given[n] for n in ['x', 'c', 'ctx']}
    grad_fn = _jax.value_and_grad(_loss, argnums=(0, 1))

    def one_microbatch(ex, loss_target):
        ex = dict(ex)
        diff = ex.pop(TWIN_DIFF_INPUT)
        return grad_fn(weights, diff, {**shared, **ex}, loss_target)

    if N_MICROBATCH == 1:
        loss, (grad_w, grad_x) = one_microbatch(per_example, given["loss_target"])
    else:
        def body(carry, xs):
            loss_sum, grad_sum = carry
            l_k, (gw_k, gx_k) = one_microbatch(xs[0], xs[1])
            with _jax.named_scope("update"):
                return (loss_sum + l_k, _jax.tree.map(_jnp.add, grad_sum, gw_k)), gx_k

        init = (_jnp.zeros((), _jnp.float32), _jax.tree.map(_jnp.zeros_like, weights))
        (loss, grad_w), grad_x = _jax.lax.scan(body, init, (per_example, given["loss_target"]))
    with _jax.named_scope("update"):
        delta_w, new_m, new_v = {}, {}, {}
        for n in TWIN_WEIGHTS:
            delta_w[n], new_m[n], new_v[n] = _adamw(weights[n], grad_w[n], given["m_" + n], given["v_" + n])
    return (loss, grad_x, *[grad_w[n] for n in TWIN_WEIGHTS], *[delta_w[n] for n in TWIN_WEIGHTS],
            *[new_m[n] for n in TWIN_WEIGHTS], *[new_v[n] for n in TWIN_WEIGHTS])
```

```python
import functools

import numpy as np
import jax
import jax.numpy as jnp
from jax import lax
from jax.experimental import pallas as pl
from jax.experimental.pallas import tpu as pltpu

F32, BF16 = jnp.float32, jnp.bfloat16
HIGHEST = lax.Precision.HIGHEST

D = 1024
DEPTH = 2
GRID_W = 64
NH, NKV, HD = 8, 2, 128
GROUP = NH // NKV
ROPE_THETA = 10000.0
ATTN_SCALE = HD ** -0.5
GH, GDK, GDV = 4, 128, 256
GLA_RANK = 16
GLA_TAU = 16.0
CH = 64
GLA_SCALE = GDK ** -0.5
EPS = 1e-6
NDEV = 8
LANE = 128
TM = 256
NEG = -1e30

ADAM_LR, ADAM_B1, ADAM_B2, ADAM_EPS, ADAM_WD, ADAM_STEP = 0.001, 0.9, 0.999, 1e-08, 0.01, 10

_SEGS = (("a_b", 0, 1024), ("a_c", 1024, 1024), ("a_x", 2048, 1024), ("a_z", 3072, 1024),
         ("q", 4096, 1024), ("z_attn", 5632, 1024), ("gv", 7680, 1024), ("zg", 8736, 1024),
         ("mg", 9760, 3072), ("gq", 6656, 512), ("gk", 7168, 512), ("k", 5120, 256), ("v", 5376, 256),
         ("r", 8704, 32))
IN_WIDTH = 12832
NP = 13312
OFF = {}
_o = 0
for _n, _s, _w in _SEGS:
    OFF[_n] = _o
    _o += _w
R_PAD = 128


def _cparams(ngrid, vmem_mb):
    return pltpu.CompilerParams(dimension_semantics=("arbitrary",) * ngrid, vmem_limit_bytes=vmem_mb << 20)


def _pick(n, cands):
    for c in cands:
        if n % c == 0:
            return c
    return n


def _sigmoid(x):
    return 1.0 / (1.0 + jnp.exp(-x))


def _row_tile(r, cap=2048):
    best = 8
    for t in range(8, min(r, cap) + 1, 8):
        if r % t == 0:
            best = t
    return best if r % best == 0 else r


def _all_gather(x, name):
    r, w = x.shape

    def body(x_ref, out_ref, send_sems, recv_sems, local_sem):
        mx, my, mc = lax.axis_index("x"), lax.axis_index("y"), lax.axis_index("c")
        me, sibling = (mx, my, mc), (mx, my, 1 - mc)
        chips = [(1 - mx, my), (mx, 1 - my), (1 - mx, 1 - my)]

        def slot(px, py, pc):
            return out_ref.at[4 * px + 2 * py + pc]

        def copy(k, block, to, src=None):
            return pltpu.make_async_remote_copy(
                src_ref=slot(*block) if src is None else src, dst_ref=slot(*block),
                send_sem=send_sems.at[k], recv_sem=recv_sems.at[k],
                device_id=to, device_id_type=pl.DeviceIdType.MESH)

        mine = pltpu.make_async_copy(x_ref, slot(*me), local_sem)
        mine.start()
        first = [copy(0, me, sibling, src=x_ref)]
        first += [copy(1 + j, me, (*chip, mc), src=x_ref) for j, chip in enumerate(chips)]
        for cp in first:
            cp.start()
        passed = [copy(4 + j, (*chip, mc), sibling) for j, chip in enumerate(chips)]
        for j, chip in enumerate(chips):
            copy(1 + j, (*chip, mc), me).wait_recv()
            passed[j].start()
        copy(0, sibling, me).wait_recv()
        for j, chip in enumerate(chips):
            copy(4 + j, (*chip, 1 - mc), me).wait_recv()
        for cp in first + passed:
            cp.wait_send()
        mine.wait()

    return pl.pallas_call(
        body, name=name,
        out_shape=jax.ShapeDtypeStruct((NDEV, r, w), x.dtype),
        in_specs=[pl.BlockSpec(memory_space=pl.ANY)],
        out_specs=pl.BlockSpec(memory_space=pl.ANY),
        scratch_shapes=[pltpu.SemaphoreType.DMA((7,)), pltpu.SemaphoreType.DMA((7,)), pltpu.SemaphoreType.DMA],
    )(x)


def _all_to_all(x, name):
    _, r, w = x.shape

    def body(x_ref, out_ref, send_sems, recv_sems, local_sem):
        mx, my, mc = lax.axis_index("x"), lax.axis_index("y"), lax.axis_index("c")
        me = 4 * mx + 2 * my + mc
        mine = pltpu.make_async_copy(x_ref.at[me], out_ref.at[me], local_sem)
        mine.start()
        copies = []
        for rel in range(1, NDEV):
            px = (1 - mx) if rel & 4 else mx
            py = (1 - my) if rel & 2 else my
            pc = (1 - mc) if rel & 1 else mc
            peer = 4 * px + 2 * py + pc
            copies.append(pltpu.make_async_remote_copy(
                src_ref=x_ref.at[peer], dst_ref=out_ref.at[me],
                send_sem=send_sems.at[rel - 1], recv_sem=recv_sems.at[rel - 1],
                device_id=(px, py, pc), device_id_type=pl.DeviceIdType.MESH))
        for cp in copies:
            cp.start()
        for cp in copies:
            cp.wait_recv()
        for cp in copies:
            cp.wait_send()
        mine.wait()

    return pl.pallas_call(
        body, name=name,
        out_shape=jax.ShapeDtypeStruct((NDEV, r, w), x.dtype),
        in_specs=[pl.BlockSpec(memory_space=pl.ANY)],
        out_specs=pl.BlockSpec(memory_space=pl.ANY),
        scratch_shapes=[pltpu.SemaphoreType.DMA((7,)), pltpu.SemaphoreType.DMA((7,)), pltpu.SemaphoreType.DMA],
    )(x)


def _mm(a, b, name, ta=False, tb=False, out_dtype=F32, bias=None, precise=False, tm=None, tn=None, tk=None):
    m, k = (a.shape[1], a.shape[0]) if ta else a.shape
    n = b.shape[0] if tb else b.shape[1]
    assert k == (b.shape[1] if tb else b.shape[0])
    tm = tm or _pick(m, (1088, 1024, 512, 256, 128))
    tn = tn or _pick(n, (1024, 512, 384, 256, 128))
    tk = tk or _pick(k, (1024, 1088, 512, 256, 128))
    nk = k // tk
    dn = (((0 if ta else 1,), (1 if tb else 0,)), ((), ()))

    def body(*refs):
        if bias is None:
            a_ref, b_ref, o_ref = refs[:3]
            bias_ref = None
        else:
            a_ref, b_ref, bias_ref, o_ref = refs[:4]
        x, y = a_ref[...], b_ref[...]
        if precise:
            p = lax.dot_general(x.astype(F32), y.astype(F32), dn, preferred_element_type=F32, precision=HIGHEST)
        else:
            p = lax.dot_general(x.astype(BF16), y.astype(BF16), dn, preferred_element_type=F32)

        def finish(acc):
            if bias_ref is not None:
                acc = acc + bias_ref[...]
            o_ref[...] = acc.astype(out_dtype)

        if nk == 1:
            finish(p)
        else:
            acc_ref = refs[-1]
            kk = pl.program_id(2)

            @pl.when(kk == 0)
            def _():
                acc_ref[...] = p

            @pl.when(kk > 0)
            def _():
                acc_ref[...] += p

            @pl.when(kk == nk - 1)
            def _():
                finish(acc_ref[...])

    a_spec = pl.BlockSpec((tk, tm), lambda i, j, kk: (kk, i)) if ta else pl.BlockSpec((tm, tk), lambda i, j, kk: (i, kk))
    b_spec = pl.BlockSpec((tn, tk), lambda i, j, kk: (j, kk)) if tb else pl.BlockSpec((tk, tn), lambda i, j, kk: (kk, j))
    in_specs = [a_spec, b_spec]
    args = [a, b]
    if bias is not None:
        in_specs.append(pl.BlockSpec((1, tn), lambda i, j, kk: (0, j)))
        args.append(bias)
    return pl.pallas_call(
        body, name=name, grid=(m // tm, n // tn, nk),
        in_specs=in_specs, out_specs=pl.BlockSpec((tm, tn), lambda i, j, kk: (i, j)),
        out_shape=jax.ShapeDtypeStruct((m, n), out_dtype),
        scratch_shapes=[pltpu.VMEM((tm, tn), F32)] if nk > 1 else [],
        compiler_params=_cparams(3, 56),
    )(*args)


def _ada_in(cc):
    def body(c_ref, s_ref, d_ref):
        x = c_ref[...]
        sg = _sigmoid(x)
        s_ref[...] = x * sg
        d_ref[...] = sg * (1.0 + x * (1.0 - sg))

    return pl.pallas_call(body, name="ada_in", out_shape=(jax.ShapeDtypeStruct(cc.shape, F32),) * 2)(cc)


def _cctx_grad(t0, t1, dsilu):
    def body(a_ref, b_ref, d_ref, o_ref):
        o_ref[...] = (a_ref[...] + b_ref[...]) * d_ref[...]

    return pl.pallas_call(body, name="cctx_grad", out_shape=jax.ShapeDtypeStruct(t0.shape, F32))(t0, t1, dsilu)


def _seg_spec(nct, rows=3):
    return pl.BlockSpec((None, rows, D), lambda i: (jnp.where(i >= nct, 1, 0), 0, 0))


def _prenorm_fwd(x, g_pre, mod3, nct, name):
    t = x.shape[0]

    def body(x_ref, g_ref, mod_ref, h_ref):
        xv = x_ref[...]
        r = lax.rsqrt(jnp.mean(xv * xv, axis=-1, keepdims=True) + EPS)
        y = xv * r * g_ref[...]
        h_ref[...] = (y * (1.0 + mod_ref[1:2, :]) + mod_ref[0:1, :]).astype(BF16)

    return pl.pallas_call(
        body, name=name, grid=(t // TM,),
        in_specs=[pl.BlockSpec((TM, D), lambda i: (i, 0)), pl.BlockSpec((1, D), lambda i: (0, 0)), _seg_spec(nct)],
        out_specs=pl.BlockSpec((TM, D), lambda i: (i, 0)),
        out_shape=jax.ShapeDtypeStruct((t, D), BF16), compiler_params=_cparams(1, 32),
    )(x, g_pre, mod3)


def _prenorm_bwd(dh, x, dxo, g_pre, mod3, nct, name):
    t = x.shape[0]

    def body(dh_ref, x_ref, dxo_ref, g_ref, mod_ref, dx_ref, dsh_ref, dsc_ref, dg_ref):
        i = pl.program_id(0)
        xv, dhv, g = x_ref[...], dh_ref[...], g_ref[...]
        r = lax.rsqrt(jnp.mean(xv * xv, axis=-1, keepdims=True) + EPS)
        xh = xv * r
        dy = dhv * (1.0 + mod_ref[1:2, :])
        dxh = dy * g
        dx_ref[...] = dxo_ref[...] + r * (dxh - xh * jnp.mean(dxh * xh, axis=-1, keepdims=True))

        @pl.when((i == 0) | (i == nct))
        def _():
            dsh_ref[...] = jnp.zeros_like(dsh_ref)
            dsc_ref[...] = jnp.zeros_like(dsc_ref)

        @pl.when(i == 0)
        def _():
            dg_ref[...] = jnp.zeros_like(dg_ref)

        dsh_ref[...] += jnp.sum(dhv, axis=0, keepdims=True)
        dsc_ref[...] += jnp.sum(dhv * (xh * g), axis=0, keepdims=True)
        dg_ref[...] += jnp.sum(dy * xh, axis=0, keepdims=True)

    row = pl.BlockSpec((TM, D), lambda i: (i, 0))
    seg8 = pl.BlockSpec((None, 8, D), lambda i: (jnp.where(i >= nct, 1, 0), 0, 0))
    return pl.pallas_call(
        body, name=name, grid=(t // TM,),
        in_specs=[row, row, row, pl.BlockSpec((1, D), lambda i: (0, 0)), _seg_spec(nct)],
        out_specs=(row, seg8, seg8, pl.BlockSpec((8, D), lambda i: (0, 0))),
        out_shape=(jax.ShapeDtypeStruct((t, D), F32), jax.ShapeDtypeStruct((2, 8, D), F32),
                   jax.ShapeDtypeStruct((2, 8, D), F32), jax.ShapeDtypeStruct((8, D), F32)),
        compiler_params=_cparams(1, 32),
    )(dh, x, dxo, g_pre, mod3)


def _post_fwd(x, out, g_post, mod3, nct, name):
    t = x.shape[0]

    def body(x_ref, o_ref, g_ref, mod_ref, y_ref):
        ov = o_ref[...]
        r = lax.rsqrt(jnp.mean(ov * ov, axis=-1, keepdims=True) + EPS)
        y_ref[...] = x_ref[...] + mod_ref[2:3, :] * (ov * r * g_ref[...])

    row = pl.BlockSpec((TM, D), lambda i: (i, 0))
    return pl.pallas_call(
        body, name=name, grid=(t // TM,),
        in_specs=[row, row, pl.BlockSpec((1, D), lambda i: (0, 0)), _seg_spec(nct)],
        out_specs=row, out_shape=jax.ShapeDtypeStruct((t, D), F32), compiler_params=_cparams(1, 32),
    )(x, out, g_post, mod3)


def _post_bwd(dxo, out, g_post, mod3, nct, name):
    t = out.shape[0]

    def body(dx_ref, o_ref, g_ref, mod_ref, do_ref, dgt_ref, dg_ref):
        i = pl.program_id(0)
        ov, dxv, g = o_ref[...], dx_ref[...], g_ref[...]
        r = lax.rsqrt(jnp.mean(ov * ov, axis=-1, keepdims=True) + EPS)
        nh = ov * r
        dn = dxv * mod_ref[2:3, :]
        dnh = dn * g
        do_ref[...] = (r * (dnh - nh * jnp.mean(dnh * nh, axis=-1, keepdims=True))).astype(BF16)

        @pl.when((i == 0) | (i == nct))
        def _():
            dgt_ref[...] = jnp.zeros_like(dgt_ref)

        @pl.when(i == 0)
        def _():
            dg_ref[...] = jnp.zeros_like(dg_ref)

        dgt_ref[...] += jnp.sum(dxv * (nh * g), axis=0, keepdims=True)
        dg_ref[...] += jnp.sum(dn * nh, axis=0, keepdims=True)

    row = pl.BlockSpec((TM, D), lambda i: (i, 0))
    seg8 = pl.BlockSpec((None, 8, D), lambda i: (jnp.where(i >= nct, 1, 0), 0, 0))
    return pl.pallas_call(
        body, name=name, grid=(t // TM,),
        in_specs=[row, row, pl.BlockSpec((1, D), lambda i: (0, 0)), _seg_spec(nct)],
        out_specs=(row, seg8, pl.BlockSpec((8, D), lambda i: (0, 0))),
        out_shape=(jax.ShapeDtypeStruct((t, D), BF16), jax.ShapeDtypeStruct((2, 8, D), F32),
                   jax.ShapeDtypeStruct((8, D), F32)),
        compiler_params=_cparams(1, 32),
    )(dxo, out, g_post, mod3)


def _loss_grad(y, target, nct, name):
    t = y.shape[0]

    def body(y_ref, t_ref, dy_ref, l_ref):
        i = pl.program_id(0)

        @pl.when(i == 0)
        def _():
            l_ref[...] = jnp.zeros_like(l_ref)

        @pl.when(i < nct)
        def _():
            dy_ref[...] = jnp.zeros_like(dy_ref)

        @pl.when(i >= nct)
        def _():
            err = y_ref[...] - t_ref[...]
            dy_ref[...] = err / D
            l_ref[...] += jnp.sum(jnp.sum(err * err, axis=1, keepdims=True), axis=0, keepdims=True)

    row = pl.BlockSpec((TM, D), lambda i: (i, 0))
    return pl.pallas_call(
        body, name=name, grid=(t // TM,),
        in_specs=[row, pl.BlockSpec((TM, D), lambda i: (jnp.maximum(i - nct, 0), 0))],
        out_specs=(row, pl.BlockSpec((8, LANE), lambda i: (0, 0))),
        out_shape=(jax.ShapeDtypeStruct((t, D), F32), jax.ShapeDtypeStruct((8, LANE), F32)),
        compiler_params=_cparams(1, 32),
    )(y, target)


def _pcol(name, width):
    assert OFF[name] % width == 0
    blk = OFF[name] // width
    return pl.BlockSpec((TM, width), lambda i: (i, blk))


def _shift_rows(u, prev_row, next_row):
    n = u.shape[0]
    row = lax.broadcasted_iota(jnp.int32, u.shape, 0)
    prev = jnp.where(row == 0, prev_row, pltpu.roll(u, 1, 0))
    nxt = jnp.where(row == n - 1, next_row, pltpu.roll(u, n - 1, 0))
    return prev, nxt


def _halo_specs(width, nt, blk=0):
    r8 = TM // 8
    prev = pl.BlockSpec((8, width), lambda i: (jnp.maximum(i * r8 - 1, 0), blk))
    nxt = pl.BlockSpec((8, width), lambda i: (jnp.minimum((i + 1) * r8, nt * r8 - 1), blk))
    return prev, nxt


def _conv_fwd(p, conv_w8, nct, name):
    t = p.shape[0]
    nt = t // TM

    def body(ab_ref, ac_ref, ax_ref, az_ref, acp_ref, axp_ref, acn_ref, axn_ref, w_ref, cv_ref, ya_ref):
        i = pl.program_id(0)
        u = ac_ref[...] * ax_ref[...]
        mp = jnp.where((i == 0) | (i == nct), 0.0, 1.0)
        mn = jnp.where((i == nct - 1) | (i == nt - 1), 0.0, 1.0)
        prev, nxt = _shift_rows(u, acp_ref[7:8, :] * axp_ref[7:8, :] * mp, acn_ref[0:1, :] * axn_ref[0:1, :] * mn)
        cv = w_ref[0:1, :] * prev + w_ref[1:2, :] * u + w_ref[2:3, :] * nxt
        az = az_ref[...]
        cv_ref[...] = cv
        ya_ref[...] = (ab_ref[...] * cv * (az * _sigmoid(az))).astype(BF16)

    acp, acn = _halo_specs(D, nt, OFF["a_c"] // D)
    axp, axn = _halo_specs(D, nt, OFF["a_x"] // D)
    row = pl.BlockSpec((TM, D), lambda i: (i, 0))
    return pl.pallas_call(
        body, name=name, grid=(nt,),
        in_specs=[_pcol("a_b", D), _pcol("a_c", D), _pcol("a_x", D), _pcol("a_z", D), acp, axp, acn, axn,
                  pl.BlockSpec((8, D), lambda i: (0, 0))],
        out_specs=(row, row),
        out_shape=(jax.ShapeDtypeStruct((t, D), F32), jax.ShapeDtypeStruct((t, D), BF16)),
        compiler_params=_cparams(1, 40),
    )(p, p, p, p, p, p, p, p, conv_w8)


def _conv_bwd_a(dya, p, cv, name):
    t = p.shape[0]

    def body(dy_ref, ab_ref, az_ref, cv_ref, dcv_ref, dab_ref, daz_ref):
        dy, ab, az, c = dy_ref[...], ab_ref[...], az_ref[...], cv_ref[...]
        sg = _sigmoid(az)
        sz = az * sg
        dcv_ref[...] = dy * ab * sz
        dab_ref[...] = (dy * c * sz).astype(BF16)
        daz_ref[...] = (dy * ab * c * (sg * (1.0 + az * (1.0 - sg)))).astype(BF16)

    row = pl.BlockSpec((TM, D), lambda i: (i, 0))
    return pl.pallas_call(
        body, name=name, grid=(t // TM,),
        in_specs=[row, _pcol("a_b", D), _pcol("a_z", D), row], out_specs=(row, row, row),
        out_shape=(jax.ShapeDtypeStruct((t, D), F32), jax.ShapeDtypeStruct((t, D), BF16),
                   jax.ShapeDtypeStruct((t, D), BF16)),
        compiler_params=_cparams(1, 40),
    )(dya, p, p, cv)


def _conv_bwd_b(dcv, p, conv_w8, nct, name):
    t = p.shape[0]
    nt = t // TM

    def body(dcv_ref, dp_ref, dn_ref, ac_ref, ax_ref, w_ref, dac_ref, dax_ref, dw_ref):
        i = pl.program_id(0)
        d, ac, ax = dcv_ref[...], ac_ref[...], ax_ref[...]
        u = ac * ax
        mp = jnp.where((i == 0) | (i == nct), 0.0, 1.0)
        mn = jnp.where((i == nct - 1) | (i == nt - 1), 0.0, 1.0)
        dprev, dnxt = _shift_rows(d, dp_ref[7:8, :] * mp, dn_ref[0:1, :] * mn)
        du = w_ref[0:1, :] * dnxt + w_ref[1:2, :] * d + w_ref[2:3, :] * dprev
        dac_ref[...] = (du * ax).astype(BF16)
        dax_ref[...] = (du * ac).astype(BF16)

        @pl.when(i == 0)
        def _():
            dw_ref[...] = jnp.zeros_like(dw_ref)

        dw0 = jnp.sum(u * dnxt, axis=0, keepdims=True)
        dw1 = jnp.sum(u * d, axis=0, keepdims=True)
        dw2 = jnp.sum(u * dprev, axis=0, keepdims=True)
        r8 = lax.broadcasted_iota(jnp.int32, (8, D), 0)
        dw_ref[...] += jnp.where(r8 == 0, dw0, jnp.where(r8 == 1, dw1, jnp.where(r8 == 2, dw2, 0.0)))

    dp, dn = _halo_specs(D, nt)
    row = pl.BlockSpec((TM, D), lambda i: (i, 0))
    return pl.pallas_call(
        body, name=name, grid=(nt,),
        in_specs=[row, dp, dn, _pcol("a_c", D), _pcol("a_x", D), pl.BlockSpec((8, D), lambda i: (0, 0))],
        out_specs=(row, row, pl.BlockSpec((8, D), lambda i: (0, 0))),
        out_shape=(jax.ShapeDtypeStruct((t, D), BF16), jax.ShapeDtypeStruct((t, D), BF16),
                   jax.ShapeDtypeStruct((8, D), F32)),
        compiler_params=_cparams(1, 40),
    )(dcv, dcv, dcv, p, p, conv_w8)


def _rot_half(x):
    lane = lax.broadcasted_iota(jnp.int32, x.shape, 1)
    return jnp.where((lane % 64) < 32, pltpu.roll(x, 96, 1), pltpu.roll(x, 32, 1))


def _qk_prep_fwd(p, qg, kg, cos_t, sin_t, name):
    t = p.shape[0]

    def body(q_ref, k_ref, qg_ref, kg_ref, c_ref, s_ref, qo_ref, ko_ref):
        c, s = c_ref[...], s_ref[...]

        def one(xv, g):
            y = xv * lax.rsqrt(jnp.mean(xv * xv, axis=-1, keepdims=True) + EPS) * g
            return (y * c + _rot_half(y) * s).astype(BF16)

        for h in range(NH):
            qo_ref[:, h * HD:(h + 1) * HD] = one(q_ref[:, h * HD:(h + 1) * HD], qg_ref[...])
        for h in range(NKV):
            ko_ref[:, h * HD:(h + 1) * HD] = one(k_ref[:, h * HD:(h + 1) * HD], kg_ref[...])

    vec = pl.BlockSpec((1, HD), lambda i: (0, 0))
    tab = pl.BlockSpec((TM, HD), lambda i: (i, 0))
    return pl.pallas_call(
        body, name=name, grid=(t // TM,),
        in_specs=[_pcol("q", NH * HD), _pcol("k", NKV * HD), vec, vec, tab, tab],
        out_specs=(pl.BlockSpec((TM, NH * HD), lambda i: (i, 0)), pl.BlockSpec((TM, NKV * HD), lambda i: (i, 0))),
        out_shape=(jax.ShapeDtypeStruct((t, NH * HD), BF16), jax.ShapeDtypeStruct((t, NKV * HD), BF16)),
        compiler_params=_cparams(1, 32),
    )(p, p, qg, kg, cos_t, sin_t)


def _qk_prep_bwd(dqr, dkr, p, qg, kg, cos_t, sin_t, name):
    t = p.shape[0]

    def body(dq_ref, dk_ref, q_ref, k_ref, qg_ref, kg_ref, c_ref, s_ref, dqo_ref, dko_ref, dqg_ref, dkg_ref):
        i = pl.program_id(0)
        c, s = c_ref[...], s_ref[...]

        @pl.when(i == 0)
        def _():
            dqg_ref[...] = jnp.zeros_like(dqg_ref)
            dkg_ref[...] = jnp.zeros_like(dkg_ref)

        def one(dyr, xv, g):
            dy = dyr * c + _rot_half(dyr * s)
            r = lax.rsqrt(jnp.mean(xv * xv, axis=-1, keepdims=True) + EPS)
            xh = xv * r
            dxh = dy * g
            dx = r * (dxh - xh * jnp.mean(dxh * xh, axis=-1, keepdims=True))
            return dx.astype(BF16), jnp.sum(dy * xh, axis=0, keepdims=True)

        for h in range(NH):
            sl = slice(h * HD, (h + 1) * HD)
            dx, dg = one(dq_ref[:, sl], q_ref[:, sl], qg_ref[...])
            dqo_ref[:, sl] = dx
            dqg_ref[...] += dg
        for h in range(NKV):
            sl = slice(h * HD, (h + 1) * HD)
            dx, dg = one(dk_ref[:, sl], k_ref[:, sl], kg_ref[...])
            dko_ref[:, sl] = dx
            dkg_ref[...] += dg

    vec = pl.BlockSpec((1, HD), lambda i: (0, 0))
    tab = pl.BlockSpec((TM, HD), lambda i: (i, 0))
    acc = pl.BlockSpec((8, HD), lambda i: (0, 0))
    qrow = pl.BlockSpec((TM, NH * HD), lambda i: (i, 0))
    krow = pl.BlockSpec((TM, NKV * HD), lambda i: (i, 0))
    return pl.pallas_call(
        body, name=name, grid=(t // TM,),
        in_specs=[qrow, krow, _pcol("q", NH * HD), _pcol("k", NKV * HD), vec, vec, tab, tab],
        out_specs=(qrow, krow, acc, acc),
        out_shape=(jax.ShapeDtypeStruct((t, NH * HD), BF16), jax.ShapeDtypeStruct((t, NKV * HD), BF16),
                   jax.ShapeDtypeStruct((8, HD), F32), jax.ShapeDtypeStruct((8, HD), F32)),
        compiler_params=_cparams(1, 32),
    )(dqr, dkr, p, p, qg, kg, cos_t, sin_t)


def _attn_probs(q, k, i, nct, ctx):
    s = lax.dot_general(q, k, (((1,), (1,)), ((), ())), preferred_element_type=F32) * ATTN_SCALE
    col = lax.broadcasted_iota(jnp.int32, s.shape, 1)
    s = jnp.where((i < nct) & (col >= ctx), NEG, s)
    e = jnp.exp(s - jnp.max(s, axis=-1, keepdims=True))
    return e / jnp.sum(e, axis=-1, keepdims=True)


def _attn_fwd(qr, kr, p, nct, name):
    t = qr.shape[0]
    ctx = nct * TM
    vblk = OFF["v"] // HD

    def body(q_ref, k_ref, v_ref, o_ref):
        pr = _attn_probs(q_ref[...], k_ref[...], pl.program_id(1), nct, ctx)
        o_ref[...] = jnp.dot(pr.astype(BF16), v_ref[...].astype(BF16), preferred_element_type=F32)

    return pl.pallas_call(
        body, name=name, grid=(NH, t // TM),
        in_specs=[pl.BlockSpec((TM, HD), lambda h, i: (i, h)),
                  pl.BlockSpec((t, HD), lambda h, i: (0, h // GROUP)),
                  pl.BlockSpec((t, HD), lambda h, i: (0, vblk + h // GROUP))],
        out_specs=pl.BlockSpec((TM, HD), lambda h, i: (i, h)),
        out_shape=jax.ShapeDtypeStruct((t, NH * HD), F32), compiler_params=_cparams(2, 48),
    )(qr, kr, p)


def _attn_bwd(qr, kr, p, o, do, nct, name):
    t = qr.shape[0]
    ctx = nct * TM
    vblk = OFF["v"] // HD

    def body(q_ref, k_ref, v_ref, o_ref, do_ref, dq_ref, dk_ref, dv_ref):
        g, i = pl.program_id(1), pl.program_id(2)
        q, k = q_ref[...], k_ref[...]
        vb = v_ref[...].astype(BF16)
        dov = do_ref[...]
        dob = dov.astype(BF16)
        pr = _attn_probs(q, k, i, nct, ctx)
        dp = lax.dot_general(dob, vb, (((1,), (1,)), ((), ())), preferred_element_type=F32)
        drow = jnp.sum(dov * o_ref[...], axis=-1, keepdims=True)
        ds = (pr * (dp - drow) * ATTN_SCALE).astype(BF16)
        dq_ref[...] = jnp.dot(ds, k, preferred_element_type=F32)

        @pl.when((g == 0) & (i == 0))
        def _():
            dk_ref[...] = jnp.zeros_like(dk_ref)
            dv_ref[...] = jnp.zeros_like(dv_ref)

        dk_ref[...] += lax.dot_general(ds, q, (((0,), (0,)), ((), ())), preferred_element_type=F32)
        dv_ref[...] += lax.dot_general(pr.astype(BF16), dob, (((0,), (0,)), ((), ())), preferred_element_type=F32)

    qspec = pl.BlockSpec((TM, HD), lambda kv, g, i: (i, kv * GROUP + g))
    return pl.pallas_call(
        body, name=name, grid=(NKV, GROUP, t // TM),
        in_specs=[qspec, pl.BlockSpec((t, HD), lambda kv, g, i: (0, kv)),
                  pl.BlockSpec((t, HD), lambda kv, g, i: (0, vblk + kv)), qspec, qspec],
        out_specs=(qspec, pl.BlockSpec((t, HD), lambda kv, g, i: (0, kv)), pl.BlockSpec((t, HD), lambda kv, g, i: (0, kv))),
        out_shape=(jax.ShapeDtypeStruct((t, NH * HD), F32), jax.ShapeDtypeStruct((t, NKV * HD), F32),
                   jax.ShapeDtypeStruct((t, NKV * HD), F32)),
        compiler_params=_cparams(3, 48),
    )(qr, kr, p, o, do)


def _decay_fwd(p, wd, bd, name):
    t = p.shape[0]

    def body(r_ref, w_ref, b_ref, z_ref, la_ref):
        z = jnp.dot(r_ref[...].astype(BF16), w_ref[...].astype(BF16), preferred_element_type=F32) + b_ref[...]
        z_ref[...] = z
        la_ref[...] = (jnp.minimum(z, 0.0) - jnp.log(1.0 + jnp.exp(-jnp.abs(z)))) / GLA_TAU

    row = pl.BlockSpec((TM, D), lambda i: (i, 0))
    return pl.pallas_call(
        body, name=name, grid=(t // TM,),
        in_specs=[_pcol("r", R_PAD), pl.BlockSpec((R_PAD, D), lambda i: (0, 0)), pl.BlockSpec((1, D), lambda i: (0, 0))],
        out_specs=(row, row),
        out_shape=(jax.ShapeDtypeStruct((t, D), F32), jax.ShapeDtypeStruct((t, D), F32)),
        compiler_params=_cparams(1, 32),
    )(p, wd, bd)


def _chunk_order(s, ncc, nc, rev):
    if not rev:
        return s
    return jnp.where(s < ncc, ncc - 1 - s, nc - 1 - (s - ncc))


def _gla_chunk(q_ref, k_ref, la_ref, rev):
    r = lax.broadcasted_iota(jnp.int32, (CH, CH), 0)
    c = lax.broadcasted_iota(jnp.int32, (CH, CH), 1)
    keep = (c >= r) if rev else (c <= r)
    tri = keep.astype(F32)
    bc = jnp.dot(tri, la_ref[...], preferred_element_type=F32, precision=HIGHEST)
    bl = jnp.sum(la_ref[...], axis=0, keepdims=True)
    qt = q_ref[...] * GLA_SCALE * jnp.exp(bc)
    kv = k_ref[...]
    kt = kv * jnp.exp(-bc)
    kh = kv * jnp.exp(bl - bc)
    return qt, kt, jnp.exp(bl), kh, keep, bc


_NT = (((1,), (1,)), ((), ()))
_TN = (((0,), (0,)), ((), ()))


def _gla_specs(ncc, nc, rev, backward):
    def idx(s):
        return _chunk_order((nc - 1 - s) if backward else s, ncc, nc, rev)

    qb, kb, vb = OFF["gq"] // GDK, OFF["gk"] // GDK, OFF["gv"] // GDV
    lab = (GH if rev else 0)
    q = pl.BlockSpec((CH, GDK), lambda h, s: (idx(s), qb + h))
    k = pl.BlockSpec((CH, GDK), lambda h, s: (idx(s), kb + h))
    v = pl.BlockSpec((CH, GDV), lambda h, s: (idx(s), vb + h))
    la = pl.BlockSpec((CH, GDK), lambda h, s: (idx(s), lab + h))
    o = pl.BlockSpec((CH, GDV), lambda h, s: (idx(s), h))
    dk = pl.BlockSpec((CH, GDK), lambda h, s: (idx(s), h))
    st = pl.BlockSpec((None, None, GDV, GDK), lambda h, s: (idx(s), h, 0, 0))
    return q, k, v, la, o, dk, st


def _gla_fwd(p, la, ncc, rev, name):
    t = p.shape[0]
    nc = t // CH
    q_s, k_s, v_s, la_s, o_s, _, st_s = _gla_specs(ncc, nc, rev, False)

    def body(q_ref, k_ref, v_ref, la_ref, o_ref, st_ref, s_scr):
        @pl.when(pl.program_id(1) == 0)
        def _():
            s_scr[...] = jnp.zeros_like(s_scr)

        qt, kt, gl, kh, keep, _ = _gla_chunk(q_ref, k_ref, la_ref, rev)
        st = s_scr[...]
        st_ref[...] = st
        vb = v_ref[...].astype(BF16)
        qb = qt.astype(BF16)
        a = jnp.where(keep, lax.dot_general(qb, kt.astype(BF16), _NT, preferred_element_type=F32), 0.0)
        o_ref[...] = (lax.dot_general(qb, st.astype(BF16), _NT, preferred_element_type=F32)
                      + jnp.dot(a.astype(BF16), vb, preferred_element_type=F32))
        s_scr[...] = st * gl + lax.dot_general(vb, kh.astype(BF16), _TN, preferred_element_type=F32)

    return pl.pallas_call(
        body, name=name, grid=(GH, nc),
        in_specs=[q_s, k_s, v_s, la_s], out_specs=(o_s, st_s),
        out_shape=(jax.ShapeDtypeStruct((t, GH * GDV), F32), jax.ShapeDtypeStruct((nc, GH, GDV, GDK), F32)),
        scratch_shapes=[pltpu.VMEM((GDV, GDK), F32)], compiler_params=_cparams(2, 32),
    )(p, p, p, la)


def _gla_bwd(p, la, do, st, ncc, rev, name):
    t = p.shape[0]
    nc = t // CH
    q_s, k_s, v_s, la_s, o_s, dk_s, st_s = _gla_specs(ncc, nc, rev, True)

    def body(q_ref, k_ref, v_ref, la_ref, do_ref, st_ref, dq_ref, dk_ref, dv_ref, dla_ref, ds_scr):
        @pl.when(pl.program_id(1) == 0)
        def _():
            ds_scr[...] = jnp.zeros_like(ds_scr)

        qt, kt, gl, kh, keep, bc = _gla_chunk(q_ref, k_ref, la_ref, rev)
        stv = st_ref[...]
        dsn = ds_scr[...]
        dsb = dsn.astype(BF16)
        vb, dob = v_ref[...].astype(BF16), do_ref[...].astype(BF16)
        qb, kb = qt.astype(BF16), kt.astype(BF16)
        a = jnp.where(keep, lax.dot_general(qb, kb, _NT, preferred_element_type=F32), 0.0).astype(BF16)
        da = jnp.where(keep, lax.dot_general(dob, vb, _NT, preferred_element_type=F32), 0.0).astype(BF16)
        dqt = jnp.dot(dob, stv.astype(BF16), preferred_element_type=F32) + jnp.dot(da, kb, preferred_element_type=F32)
        dkh = jnp.dot(vb, dsb, preferred_element_type=F32)
        dkt = lax.dot_general(da, qb, _TN, preferred_element_type=F32) + dkh * gl
        dv_ref[...] = (lax.dot_general(a, dob, _TN, preferred_element_type=F32)
                       + lax.dot_general(kh.astype(BF16), dsb, _NT, preferred_element_type=F32))
        ds_scr[...] = lax.dot_general(dob, qb, _TN, preferred_element_type=F32) + dsn * gl
        dgl = jnp.sum(stv * dsn, axis=0, keepdims=True) + jnp.sum(dkh * kt, axis=0, keepdims=True)
        row = lax.broadcasted_iota(jnp.int32, (CH, GDK), 0)
        last = 0 if rev else CH - 1
        dbc = dqt * qt - dkt * kt + jnp.where(row == last, dgl * gl, 0.0)
        r = lax.broadcasted_iota(jnp.int32, (CH, CH), 0)
        c = lax.broadcasted_iota(jnp.int32, (CH, CH), 1)
        trit = ((c <= r) if rev else (c >= r)).astype(F32)
        dla_ref[...] = jnp.dot(trit, dbc, preferred_element_type=F32, precision=HIGHEST)
        dq_ref[...] = dqt * (GLA_SCALE * jnp.exp(bc))
        dk_ref[...] = dkt * jnp.exp(-bc)

    return pl.pallas_call(
        body, name=name, grid=(GH, nc),
        in_specs=[q_s, k_s, v_s, la_s, o_s, st_s], out_specs=(dk_s, dk_s, o_s, dk_s),
        out_shape=(jax.ShapeDtypeStruct((t, GH * GDK), F32), jax.ShapeDtypeStruct((t, GH * GDK), F32),
                   jax.ShapeDtypeStruct((t, GH * GDV), F32), jax.ShapeDtypeStruct((t, GH * GDK), F32)),
        scratch_shapes=[pltpu.VMEM((GDV, GDK), F32)], compiler_params=_cparams(2, 32),
    )(p, p, p, la, do, st)


def _gla_merge_bwd(gf, gb, z, p, wd, name):
    t = p.shape[0]
    w2 = GH * GDK

    def body(dqf, dkf, dvf, dlf, dqb, dkb, dvb, dlb, z_ref, r_ref, w_ref, dq_ref, dk_ref, dv_ref, dr_ref, db_ref, dw_ref):
        i = pl.program_id(0)
        dq_ref[...] = (dqf[...] + dqb[...]).astype(BF16)
        dk_ref[...] = (dkf[...] + dkb[...]).astype(BF16)
        dv_ref[...] = (dvf[...] + dvb[...]).astype(BF16)
        zv = z_ref[...]
        dz = jnp.concatenate([dlf[...], dlb[...]], axis=1) * (_sigmoid(-zv) / GLA_TAU)
        dzb = dz.astype(BF16)
        dr_ref[...] = lax.dot_general(dzb, w_ref[...].astype(BF16), _NT, preferred_element_type=F32).astype(BF16)

        @pl.when(i == 0)
        def _():
            db_ref[...] = jnp.zeros_like(db_ref)
            dw_ref[...] = jnp.zeros_like(dw_ref)

        db_ref[...] += jnp.sum(dz, axis=0, keepdims=True)
        dw_ref[...] += lax.dot_general(r_ref[...].astype(BF16), dzb, _TN, preferred_element_type=F32)

    half = pl.BlockSpec((TM, w2), lambda i: (i, 0))
    row = pl.BlockSpec((TM, D), lambda i: (i, 0))
    wspec = pl.BlockSpec((R_PAD, D), lambda i: (0, 0))
    return pl.pallas_call(
        body, name=name, grid=(t // TM,),
        in_specs=[half, half, row, half, half, half, row, half, row, _pcol("r", R_PAD), wspec],
        out_specs=(half, half, row, pl.BlockSpec((TM, R_PAD), lambda i: (i, 0)),
                   pl.BlockSpec((8, D), lambda i: (0, 0)), wspec),
        out_shape=(jax.ShapeDtypeStruct((t, w2), BF16), jax.ShapeDtypeStruct((t, w2), BF16),
                   jax.ShapeDtypeStruct((t, D), BF16), jax.ShapeDtypeStruct((t, R_PAD), BF16),
                   jax.ShapeDtypeStruct((8, D), F32), jax.ShapeDtypeStruct((R_PAD, D), F32)),
        compiler_params=_cparams(1, 40),
    )(*gf, *gb, z, p, wd)


def _branch_fwd(att, of, ob, p, gla_g, name):
    t = p.shape[0]

    def body(att_ref, of_ref, ob_ref, za_ref, zg_ref, g_ref, yb_ref, yc_ref):
        za = za_ref[...]
        yb_ref[...] = (att_ref[...] * (za * _sigmoid(za))).astype(BF16)
        for h in range(GH):
            sl = slice(h * GDV, (h + 1) * GDV)
            o = of_ref[:, sl] + ob_ref[:, sl]
            n = o * lax.rsqrt(jnp.mean(o * o, axis=-1, keepdims=True) + EPS) * g_ref[...]
            zh = zg_ref[:, sl]
            yc_ref[:, sl] = (n * (zh * _sigmoid(zh))).astype(BF16)

    row = pl.BlockSpec((TM, D), lambda i: (i, 0))
    return pl.pallas_call(
        body, name=name, grid=(t // TM,),
        in_specs=[row, row, row, _pcol("z_attn", D), _pcol("zg", D), pl.BlockSpec((1, GDV), lambda i: (0, 0))],
        out_specs=(row, row),
        out_shape=(jax.ShapeDtypeStruct((t, D), BF16), jax.ShapeDtypeStruct((t, D), BF16)),
        compiler_params=_cparams(1, 40),
    )(att, of, ob, p, p, gla_g)


def _branch_bwd(dyb, dyc, att, of, ob, p, gla_g, name):
    t = p.shape[0]

    def body(dyb_ref, dyc_ref, att_ref, of_ref, ob_ref, za_ref, zg_ref, g_ref, datt_ref, dza_ref, do_ref, dzg_ref, dg_ref):
        i = pl.program_id(0)

        @pl.when(i == 0)
        def _():
            dg_ref[...] = jnp.zeros_like(dg_ref)

        za, dyb = za_ref[...], dyb_ref[...]
        sa = _sigmoid(za)
        datt_ref[...] = dyb * (za * sa)
        dza_ref[...] = (dyb * att_ref[...] * (sa * (1.0 + za * (1.0 - sa)))).astype(BF16)
        g = g_ref[...]
        for h in range(GH):
            sl = slice(h * GDV, (h + 1) * GDV)
            o = of_ref[:, sl] + ob_ref[:, sl]
            r = lax.rsqrt(jnp.mean(o * o, axis=-1, keepdims=True) + EPS)
            oh = o * r
            zh, dyc = zg_ref[:, sl], dyc_ref[:, sl]
            sg = _sigmoid(zh)
            dn = dyc * (zh * sg)
            dzg_ref[:, sl] = (dyc * (oh * g) * (sg * (1.0 + zh * (1.0 - sg)))).astype(BF16)
            doh = dn * g
            do_ref[:, sl] = r * (doh - oh * jnp.mean(doh * oh, axis=-1, keepdims=True))
            dg_ref[...] += jnp.sum(dn * oh, axis=0, keepdims=True)

    row = pl.BlockSpec((TM, D), lambda i: (i, 0))
    return pl.pallas_call(
        body, name=name, grid=(t // TM,),
        in_specs=[row, row, row, row, row, _pcol("z_attn", D), _pcol("zg", D), pl.BlockSpec((1, GDV), lambda i: (0, 0))],
        out_specs=(row, row, row, row, pl.BlockSpec((8, GDV), lambda i: (0, 0))),
        out_shape=(jax.ShapeDtypeStruct((t, D), F32), jax.ShapeDtypeStruct((t, D), BF16),
                   jax.ShapeDtypeStruct((t, D), F32), jax.ShapeDtypeStruct((t, D), BF16),
                   jax.ShapeDtypeStruct((8, GDV), F32)),
        compiler_params=_cparams(1, 48),
    )(dyb, dyc, att, of, ob, p, p, gla_g)


def _merge_fwd(bra, brb, brc, p, b_gate, name):
    t = p.shape[0]
    mgb = OFF["mg"] // D

    def body(a_ref, b_ref, c_ref, ga_ref, gb_ref, gc_ref, bg_ref, m_ref):
        m_ref[...] = (_sigmoid(ga_ref[...] + bg_ref[:, 0:D]) * a_ref[...]
                      + _sigmoid(gb_ref[...] + bg_ref[:, D:2 * D]) * b_ref[...]
                      + _sigmoid(gc_ref[...] + bg_ref[:, 2 * D:3 * D]) * c_ref[...]).astype(BF16)

    row = pl.BlockSpec((TM, D), lambda i: (i, 0))
    gates = [pl.BlockSpec((TM, D), functools.partial(lambda i, b: (i, b), b=mgb + j)) for j in range(3)]
    return pl.pallas_call(
        body, name=name, grid=(t // TM,),
        in_specs=[row, row, row, *gates, pl.BlockSpec((1, 3 * D), lambda i: (0, 0))],
        out_specs=row, out_shape=jax.ShapeDtypeStruct((t, D), BF16), compiler_params=_cparams(1, 40),
    )(bra, brb, brc, p, p, p, b_gate)


def _merge_bwd(dm, bra, brb, brc, p, b_gate, name):
    t = p.shape[0]
    mgb = OFF["mg"] // D

    def body(dm_ref, a_ref, b_ref, c_ref, ga_ref, gb_ref, gc_ref, bg_ref, da_ref, db_ref, dc_ref, dmg_ref, dbg_ref):
        i = pl.program_id(0)

        @pl.when(i == 0)
        def _():
            dbg_ref[...] = jnp.zeros_like(dbg_ref)

        dm = dm_ref[...]
        for j, (br_ref, g_ref, d_ref) in enumerate(((a_ref, ga_ref, da_ref), (b_ref, gb_ref, db_ref), (c_ref, gc_ref, dc_ref))):
            sl = slice(j * D, (j + 1) * D)
            g = _sigmoid(g_ref[...] + bg_ref[:, sl])
            d_ref[...] = (dm * g).astype(BF16)
            dmg = dm * br_ref[...] * (g * (1.0 - g))
            dmg_ref[:, sl] = dmg.astype(BF16)
            dbg_ref[:, sl] += jnp.sum(dmg, axis=0, keepdims=True)

    row = pl.BlockSpec((TM, D), lambda i: (i, 0))
    gates = [pl.BlockSpec((TM, D), functools.partial(lambda i, b: (i, b), b=mgb + j)) for j in range(3)]
    return pl.pallas_call(
        body, name=name, grid=(t // TM,),
        in_specs=[row, row, row, row, *gates, pl.BlockSpec((1, 3 * D), lambda i: (0, 0))],
        out_specs=(row, row, row, pl.BlockSpec((TM, 3 * D), lambda i: (i, 0)), pl.BlockSpec((8, 3 * D), lambda i: (0, 0))),
        out_shape=(jax.ShapeDtypeStruct((t, D), BF16),) * 3 + (jax.ShapeDtypeStruct((t, 3 * D), BF16),
                                                                jax.ShapeDtypeStruct((8, 3 * D), F32)),
        compiler_params=_cparams(1, 48),
    )(dm, bra, brb, brc, p, p, p, b_gate)


def _adamw(gsrc, w, m, v, name):
    ns, r, _ = gsrc.shape
    tr = _row_tile(r)

    def body(g_ref, w_ref, m_ref, v_ref, go_ref, d_ref, mo_ref, vo_ref):
        g = g_ref[0]
        for s in range(1, ns):
            g = g + g_ref[s]
        mn = ADAM_B1 * m_ref[...] + (1.0 - ADAM_B1) * g
        vn = ADAM_B2 * v_ref[...] + (1.0 - ADAM_B2) * jnp.square(g)
        m_hat = mn / (1.0 - ADAM_B1 ** ADAM_STEP)
        v_hat = vn / (1.0 - ADAM_B2 ** ADAM_STEP)
        go_ref[...] = g
        d_ref[...] = -ADAM_LR * (m_hat / (jnp.sqrt(v_hat) + ADAM_EPS) + ADAM_WD * w_ref[...])
        mo_ref[...] = mn
        vo_ref[...] = vn

    row = pl.BlockSpec((tr, LANE), lambda i: (i, 0))
    return pl.pallas_call(
        body, name=name, grid=(r // tr,),
        in_specs=[pl.BlockSpec((ns, tr, LANE), lambda i: (0, i, 0)), row, row, row],
        out_specs=(row,) * 4, out_shape=(jax.ShapeDtypeStruct((r, LANE), F32),) * 4,
        compiler_params=_cparams(1, 48),
    )(gsrc, w, m, v)


def _slot_sum(gsrc, name):
    ns, r, _ = gsrc.shape

    def body(g_ref, o_ref):
        g = g_ref[0]
        for s in range(1, ns):
            g = g + g_ref[s]
        o_ref[...] = g

    return pl.pallas_call(body, name=name, out_shape=jax.ShapeDtypeStruct((r, LANE), F32))(gsrc)


def _pack(parts, dtype, row_align):
    chunk = row_align * LANE
    out, spans, o = [], [], 0
    for a in parts:
        f = a.reshape(-1).astype(dtype)
        n = f.shape[0]
        pad = (-n) % chunk
        if pad:
            f = jnp.concatenate([f, jnp.zeros((pad,), dtype)])
        out.append(f.reshape(-1, LANE))
        spans.append((o, n))
        o += (n + pad) // LANE
    return jnp.concatenate(out, axis=0), spans


def _unpack(packed, spans, shapes):
    res = []
    for (o, n), shp in zip(spans, shapes):
        rows = -(-n // LANE)
        res.append(packed[o:o + rows].reshape(-1)[:n].reshape(shp))
    return res


def _rope_tables(ctx, seq):
    n_rows = seq // GRID_W
    pairs = HD // 4
    row = jnp.repeat(jnp.arange(n_rows, dtype=F32), GRID_W)
    col = jnp.tile(jnp.arange(GRID_W, dtype=F32), n_rows)
    freqs = ROPE_THETA ** (-jnp.arange(pairs, dtype=F32) * 2.0 / (HD // 2))
    ar, ac = row[:, None] * freqs, col[:, None] * freqs
    cos_l = jnp.concatenate([jnp.cos(ar), jnp.cos(ar), jnp.cos(ac), jnp.cos(ac)], axis=1)
    sin_l = jnp.concatenate([-jnp.sin(ar), jnp.sin(ar), -jnp.sin(ac), jnp.sin(ac)], axis=1)
    cos_t = jnp.concatenate([jnp.ones((ctx, HD), F32), cos_l], axis=0)
    sin_t = jnp.concatenate([jnp.zeros((ctx, HD), F32), sin_l], axis=0)
    return cos_t, sin_t


def _to_proj_layout(w):
    parts = [w[:, s:s + wd] for _, s, wd in _SEGS]
    used = sum(wd for _, _, wd in _SEGS)
    parts.append(jnp.zeros((w.shape[0], NP - used), w.dtype))
    return jnp.concatenate(parts, axis=1)


def _from_proj_layout(g):
    order = sorted(_SEGS, key=lambda sg: sg[1])
    return jnp.concatenate([g[:, OFF[n]:OFF[n] + wd] for n, _, wd in order], axis=1)


def _row0(a):
    return a[..., 0, :]


def kernel(x, c, ctx, c_ctx, w_ada, b_ada, g_pre, g_post, w_in, conv_w, q_norm_g, k_norm_g, w_decay_fwd, b_decay_fwd, w_decay_bwd, b_decay_bwd, gla_norm_g, w_br_conv, w_br_attn, w_br_gla, b_gate, w_out, loss_target, m_c_ctx, m_w_ada, m_b_ada, m_g_pre, m_g_post, m_w_in, m_conv_w, m_q_norm_g, m_k_norm_g, m_w_decay_fwd, m_b_decay_fwd, m_w_decay_bwd, m_b_decay_bwd, m_gla_norm_g, m_w_br_conv, m_w_br_attn, m_w_br_gla, m_b_gate, m_w_out, v_c_ctx, v_w_ada, v_b_ada, v_g_pre, v_g_post, v_w_in, v_conv_w, v_q_norm_g, v_k_norm_g, v_w_decay_fwd, v_b_decay_fwd, v_w_decay_bwd, v_b_decay_bwd, v_gla_norm_g, v_w_br_conv, v_w_br_attn, v_w_br_gla, v_b_gate, v_w_out):
    seq, n_ctx = x.shape[1], ctx.shape[1]
    assert n_ctx % TM == 0 and seq % TM == 0 and seq % GRID_W == 0
    t = n_ctx + seq
    nct, ncc = n_ctx // TM, n_ctx // CH
    dev = 4 * lax.axis_index("x") + 2 * lax.axis_index("y") + lax.axis_index("c")
    ada_w = w_ada.shape[2]
    in_w = w_in.shape[2]
    br_r = w_br_conv.shape[1]

    big_parts = [w_ada, w_in, w_br_conv, w_br_attn, w_br_gla, w_out]
    wpack, wspans = _pack(big_parts, BF16, 16)
    wall = _all_gather(wpack, "gather_weights")
    small_parts = [conv_w, w_decay_fwd, w_decay_bwd]
    spack, sspans = _pack(small_parts, F32, 8)
    sall = _all_gather(spack, "gather_small")

    def gathered(all_, spans, k, shp):
        o, n = spans[k]
        rows = n // LANE
        return all_[:, o:o + rows].reshape((NDEV,) + shp)

    w_ada_f = gathered(wall, wspans, 0, (DEPTH, D, ada_w)).transpose(1, 2, 0, 3).reshape(DEPTH, D, 3 * D)
    w_in_f = gathered(wall, wspans, 1, (DEPTH, D, in_w)).transpose(1, 2, 0, 3).reshape(DEPTH, D, IN_WIDTH)
    w_brs_f = [gathered(wall, wspans, 2 + j, (DEPTH, br_r, D)).transpose(1, 0, 2, 3).reshape(DEPTH, D, D) for j in range(4)]
    conv_f = gathered(sall, sspans, 0, (DEPTH, 3, D // NDEV)).transpose(1, 2, 0, 3).reshape(DEPTH, 3, D)
    wdf_f = gathered(sall, sspans, 1, (DEPTH, GLA_RANK, GH * GDK // NDEV)).transpose(1, 2, 0, 3).reshape(DEPTH, GLA_RANK, GH * GDK)
    wdb_f = gathered(sall, sspans, 2, (DEPTH, GLA_RANK, GH * GDK // NDEV)).transpose(1, 2, 0, 3).reshape(DEPTH, GLA_RANK, GH * GDK)

    cos_t, sin_t = _rope_tables(n_ctx, seq)
    cc = jnp.concatenate([c_ctx[None, :], c.reshape(1, D), jnp.zeros((6, D), F32)], axis=0)
    silu_cc, dsilu_cc = _ada_in(cc)

    wp, conv8, wd_pad, bd = [], [], [], []
    for l in range(DEPTH):
        wp.append(_to_proj_layout(w_in_f[l]))
        conv8.append(jnp.concatenate([conv_f[l], jnp.zeros((5, D), F32)], axis=0))
        wd = jnp.zeros((R_PAD, D), F32)
        wd = wd.at[0:GLA_RANK, 0:GH * GDK].set(wdf_f[l]).at[GLA_RANK:2 * GLA_RANK, GH * GDK:].set(wdb_f[l])
        wd_pad.append(wd)
        bd.append(jnp.concatenate([b_decay_fwd[l], b_decay_bwd[l]])[None, :])

    xs = jnp.concatenate([ctx[0], x[0]], axis=0)
    saved = []
    for l in range(DEPTH):
        n = f"l{l}_"
        mod = _mm(silu_cc, w_ada_f[l], n + "mod", bias=b_ada[l][None, :])
        mod3 = mod[0:2].reshape(2, 3, D)
        h = _prenorm_fwd(xs, g_pre[l][None, :], mod3, nct, n + "prenorm")
        p = _mm(h, wp[l], n + "proj")
        cv, ya = _conv_fwd(p, conv8[l], nct, n + "conv")
        qr, kr = _qk_prep_fwd(p, q_norm_g[l][None, :], k_norm_g[l][None, :], cos_t, sin_t, n + "qk_prep")
        att = _attn_fwd(qr, kr, p, nct, n + "attn")
        z, la = _decay_fwd(p, wd_pad[l], bd[l], n + "decay")
        of, stf = _gla_fwd(p, la, ncc, False, n + "gla_f")
        ob, stb = _gla_fwd(p, la, ncc, True, n + "gla_b")
        yb, yc = _branch_fwd(att, of, ob, p, gla_norm_g[l][None, :], n + "branch")
        bra = _mm(ya, w_brs_f[0][l], n + "br_conv")
        brb = _mm(yb, w_brs_f[1][l], n + "br_attn")
        brc = _mm(yc, w_brs_f[2][l], n + "br_gla")
        mm_ = _merge_fwd(bra, brb, brc, p, b_gate[l][None, :], n + "merge")
        out = _mm(mm_, w_brs_f[3][l], n + "out")
        x_new = _post_fwd(xs, out, g_post[l][None, :], mod3, nct, n + "post")
        saved.append(dict(x=xs, mod3=mod3, h=h, p=p, cv=cv, ya=ya, qr=qr, kr=kr, att=att, z=z, la=la, of=of, ob=ob,
                          stf=stf, stb=stb, yb=yb, yc=yc, bra=bra, brb=brb, brc=brc, m=mm_, out=out))
        xs = x_new

    dx, sq = _loss_grad(xs, loss_target[0], nct, "loss")
    loss = lax.psum(0.5 * sq[0, 0] / D, ("x", "y", "c"))

    gw = {k: [None] * DEPTH for k in ("w_in", "br_conv", "br_attn", "br_gla", "out", "b_gate", "g_pre", "g_post",
                                      "conv_w", "qg", "kg", "wd", "bdec", "gla_g", "dmod")}
    dctx = []
    for l in reversed(range(DEPTH)):
        n = f"l{l}_b_"
        s = saved[l]
        p = s["p"]
        d_out, dgt, gw["g_post"][l] = _post_bwd(dx, s["out"], g_post[l][None, :], s["mod3"], nct, n + "post")
        dm = _mm(d_out, w_brs_f[3][l], n + "dm", tb=True)
        gw["out"][l] = _mm(s["m"], d_out, n + "dw_out", ta=True)
        dbra, dbrb, dbrc, dmg, gw["b_gate"][l] = _merge_bwd(dm, s["bra"], s["brb"], s["brc"], p, b_gate[l][None, :], n + "merge")
        dya = _mm(dbra, w_brs_f[0][l], n + "dya", tb=True)
        dyb = _mm(dbrb, w_brs_f[1][l], n + "dyb", tb=True)
        dyc = _mm(dbrc, w_brs_f[2][l], n + "dyc", tb=True)
        gw["br_conv"][l] = _mm(s["ya"], dbra, n + "dw_conv", ta=True)
        gw["br_attn"][l] = _mm(s["yb"], dbrb, n + "dw_attn", ta=True)
        gw["br_gla"][l] = _mm(s["yc"], dbrc, n + "dw_gla", ta=True)
        dcv, dab, daz = _conv_bwd_a(dya, p, s["cv"], n + "conv_a")
        dac, dax, gw["conv_w"][l] = _conv_bwd_b(dcv, p, conv8[l], nct, n + "conv_b")
        datt, dza, dgo, dzg, gw["gla_g"][l] = _branch_bwd(dyb, dyc, s["att"], s["of"], s["ob"], p, gla_norm_g[l][None, :], n + "branch")
        dqr, dkr, dv = _attn_bwd(s["qr"], s["kr"], p, s["att"], datt, nct, n + "attn")
        dq, dk, gw["qg"][l], gw["kg"][l] = _qk_prep_bwd(dqr, dkr, p, q_norm_g[l][None, :], k_norm_g[l][None, :], cos_t, sin_t, n + "qk_prep")
        gf = _gla_bwd(p, s["la"], dgo, s["stf"], ncc, False, n + "gla_f")
        gb = _gla_bwd(p, s["la"], dgo, s["stb"], ncc, True, n + "gla_b")
        dgq, dgk, dgv, dr, gw["bdec"][l], gw["wd"][l] = _gla_merge_bwd(gf, gb, s["z"], p, wd_pad[l], n + "gla_merge")
        pieces = dict(a_b=dab, a_c=dac, a_x=dax, a_z=daz, q=dq, z_attn=dza, gv=dgv, zg=dzg, mg=dmg, gq=dgq, gk=dgk,
                      k=dk, v=dv.astype(BF16), r=dr)
        used = sum(wd_ for _, _, wd_ in _SEGS) - 32 + R_PAD
        dp = jnp.concatenate([pieces[nm] for nm, _, _ in _SEGS] + [jnp.zeros((t, NP - used), BF16)], axis=1)
        dh = _mm(dp, wp[l], n + "dh", tb=True)
        gw["w_in"][l] = _mm(s["h"], dp, n + "dw_in", ta=True)
        dx, dsh, dsc, gw["g_pre"][l] = _prenorm_bwd(dh, s["x"], dx, g_pre[l][None, :], s["mod3"], nct, n + "prenorm")
        dmod = jnp.stack([_row0(dsh), _row0(dsc), _row0(dgt)], axis=1).reshape(2, 3 * D)
        gw["dmod"][l] = dmod
        dmod8 = jnp.concatenate([dmod, jnp.zeros((6, 3 * D), F32)], axis=0)
        dctx.append(_mm(dmod8, w_ada_f[l], n + "dsilu", tb=True))
    grad_x = dx[n_ctx:][None]
    g_cctx = _cctx_grad(dctx[0], dctx[1], dsilu_cc)[0]

    def st2(name):
        return jnp.stack(gw[name])

    g_b_ada = jnp.stack([gw["dmod"][l][0] + gw["dmod"][l][1] for l in range(DEPTH)])
    g_bdf = jnp.stack([gw["bdec"][l][0, :GH * GDK] for l in range(DEPTH)])
    g_bdb = jnp.stack([gw["bdec"][l][0, GH * GDK:] for l in range(DEPTH)])
    g_wdf = jnp.stack([gw["wd"][l][0:GLA_RANK, :GH * GDK] for l in range(DEPTH)])
    g_wdb = jnp.stack([gw["wd"][l][GLA_RANK:2 * GLA_RANK, GH * GDK:] for l in range(DEPTH)])
    rep_names = ["c_ctx", "b_ada", "g_pre", "g_post", "q_norm_g", "k_norm_g", "b_decay_fwd", "b_decay_bwd", "gla_norm_g", "b_gate"]
    rep_grads = [g_cctx, g_b_ada, st2("g_pre")[:, 0], st2("g_post")[:, 0], st2("qg")[:, 0], st2("kg")[:, 0], g_bdf, g_bdb,
                 st2("gla_g")[:, 0], st2("b_gate")[:, 0]]
    rep_w = [c_ctx, b_ada, g_pre, g_post, q_norm_g, k_norm_g, b_decay_fwd, b_decay_bwd, gla_norm_g, b_gate]
    rep_m = [m_c_ctx, m_b_ada, m_g_pre, m_g_post, m_q_norm_g, m_k_norm_g, m_b_decay_fwd, m_b_decay_bwd, m_gla_norm_g, m_b_gate]
    rep_v = [v_c_ctx, v_b_ada, v_g_pre, v_g_post, v_q_norm_g, v_k_norm_g, v_b_decay_fwd, v_b_decay_bwd, v_gla_norm_g, v_b_gate]
    shard_grads = [st2("conv_w")[:, 0:3], g_wdf, g_wdb]
    extra = [silu_cc[0:2], jnp.stack(gw["dmod"])]
    gpack, gspans = _pack(rep_grads + shard_grads + extra, F32, 8)
    gall = _all_gather(gpack, "gather_small_grads")
    n_rep = len(rep_grads)
    rep_rows = gspans[n_rep][0]
    shard_rows = gspans[n_rep + len(shard_grads)][0]

    wpk, rspans = _pack(rep_w, F32, 8)
    mpk, _ = _pack(rep_m, F32, 8)
    vpk, _ = _pack(rep_v, F32, 8)
    rep_out = _adamw(gall[:, :rep_rows], wpk, mpk, vpk, "adam_rep")
    rep_shapes = [a.shape for a in rep_w]
    rep_g, rep_d, rep_nm, rep_nv = [_unpack(o, rspans, rep_shapes) for o in rep_out]

    ssum = _slot_sum(gall[:, rep_rows:shard_rows], "sum_small_sharded")
    sh_spans = [(o - rep_rows, n_) for o, n_ in gspans[n_rep:n_rep + len(shard_grads)]]
    g_conv_full, g_wdf_full, g_wdb_full = _unpack(ssum, sh_spans, [(DEPTH, 3, D), (DEPTH, GLA_RANK, GH * GDK), (DEPTH, GLA_RANK, GH * GDK)])
    cw, dw = D // NDEV, GH * GDK // NDEV
    sh_g = [lax.dynamic_slice_in_dim(g_conv_full, dev * cw, cw, axis=2),
            lax.dynamic_slice_in_dim(g_wdf_full, dev * dw, dw, axis=2),
            lax.dynamic_slice_in_dim(g_wdb_full, dev * dw, dw, axis=2)]
    sgp, shs = _pack(sh_g, F32, 8)
    swp, _ = _pack([conv_w, w_decay_fwd, w_decay_bwd], F32, 8)
    smp, _ = _pack([m_conv_w, m_w_decay_fwd, m_w_decay_bwd], F32, 8)
    svp, _ = _pack([v_conv_w, v_w_decay_fwd, v_w_decay_bwd], F32, 8)
    sh_out = _adamw(sgp[None], swp, smp, svp, "adam_small_sharded")
    sh_shapes = [conv_w.shape, w_decay_fwd.shape, w_decay_bwd.shape]
    sh_gr, sh_d, sh_nm, sh_nv = [_unpack(o, shs, sh_shapes) for o in sh_out]

    eo, en = gspans[n_rep + len(shard_grads)]
    a_all = gall[:, eo:eo + en // LANE].reshape(NDEV * 2, D)
    eo2, en2 = gspans[n_rep + len(shard_grads) + 1]
    d_all = gall[:, eo2:eo2 + en2 // LANE].reshape(NDEV, DEPTH, 2, 3 * D).transpose(1, 0, 2, 3).reshape(DEPTH, NDEV * 2, 3 * D)
    g_ada = jnp.stack([_mm(a_all, lax.dynamic_slice_in_dim(d_all[l], dev * ada_w, ada_w, axis=1), f"dw_ada{l}",
                           ta=True, precise=True, tk=NDEV * 2) for l in range(DEPTH)])
    agp, aspans = _pack([g_ada], F32, 8)
    ada_out = _adamw(agp[None], _pack([w_ada], F32, 8)[0], _pack([m_w_ada], F32, 8)[0], _pack([v_w_ada], F32, 8)[0], "adam_ada")
    ada_g, ada_d, ada_nm, ada_nv = [_unpack(o, aspans, [w_ada.shape])[0] for o in ada_out]

    gin = jnp.stack([_from_proj_layout(gw["w_in"][l]) for l in range(DEPTH)])
    slots = [gin.reshape(DEPTH, D, NDEV, in_w).transpose(2, 0, 1, 3).reshape(NDEV, -1)]
    for name in ("br_conv", "br_attn", "br_gla", "out"):
        slots.append(st2(name).reshape(DEPTH, NDEV, br_r, D).transpose(1, 0, 2, 3).reshape(NDEV, -1))
    gslots = jnp.concatenate(slots, axis=1).reshape(NDEV, -1, LANE)
    grecv = _all_to_all(gslots, "exchange_grads")
    big_w = [w_in, w_br_conv, w_br_attn, w_br_gla, w_out]
    big_m = [m_w_in, m_w_br_conv, m_w_br_attn, m_w_br_gla, m_w_out]
    big_v = [v_w_in, v_w_br_conv, v_w_br_attn, v_w_br_gla, v_w_out]
    bwp, bspans = _pack(big_w, F32, 8)
    big_out = _adamw(grecv, bwp, _pack(big_m, F32, 8)[0], _pack(big_v, F32, 8)[0], "adam_big")
    big_shapes = [a.shape for a in big_w]
    big_g, big_d, big_nm, big_nv = [_unpack(o, bspans, big_shapes) for o in big_out]

    def ordered(rep, ada, big, sh):
        c_ctx_, b_ada_, g_pre_, g_post_, qg_, kg_, bdf_, bdb_, glag_, bgate_ = rep
        w_in_, brc_, bra_, brg_, wout_ = big
        conv_, wdf_, wdb_ = sh
        return [c_ctx_, ada, b_ada_, g_pre_, g_post_, w_in_, conv_, qg_, kg_, wdf_, bdf_, wdb_, bdb_, glag_,
                brc_, bra_, brg_, bgate_, wout_]

    return (loss, grad_x,
            *ordered(rep_g, ada_g, big_g, sh_gr), *ordered(rep_d, ada_d, big_d, sh_d),
            *ordered(rep_nm, ada_nm, big_nm, sh_nm), *ordered(rep_nv, ada_nv, big_nv, sh_nv))
```

```python
import functools

import numpy as np
import jax
import jax.numpy as jnp
from jax import lax
from jax.experimental import pallas as pl
from jax.experimental.pallas import tpu as pltpu

F32, BF16 = jnp.float32, jnp.bfloat16
HIGHEST = lax.Precision.HIGHEST

D = 1024
DEPTH = 2
GRID_W = 64
NH, NKV, HD = 8, 2, 128
GROUP = NH // NKV
ROPE_THETA = 10000.0
ATTN_SCALE = HD ** -0.5
GH, GDK, GDV = 4, 128, 256
GLA_RANK = 16
GLA_TAU = 16.0
CH = 64
GLA_SCALE = GDK ** -0.5
EPS = 1e-6
NDEV = 8
LANE = 128
TM = 256
NEG = -1e30

ADAM_LR, ADAM_B1, ADAM_B2, ADAM_EPS, ADAM_WD, ADAM_STEP = 0.001, 0.9, 0.999, 1e-08, 0.01, 10

_SEGS = (("a_b", 0, 1024), ("a_c", 1024, 1024), ("a_x", 2048, 1024), ("a_z", 3072, 1024),
         ("q", 4096, 1024), ("z_attn", 5632, 1024), ("gv", 7680, 1024), ("zg", 8736, 1024),
         ("mg", 9760, 3072), ("gq", 6656, 512), ("gk", 7168, 512), ("k", 5120, 256), ("v", 5376, 256),
         ("r", 8704, 32))
IN_WIDTH = 12832
NP = 13312
OFF = {}
_o = 0
for _n, _s, _w in _SEGS:
    OFF[_n] = _o
    _o += _w
R_PAD = 128


def _cparams(ngrid, vmem_mb):
    return pltpu.CompilerParams(dimension_semantics=("arbitrary",) * ngrid, vmem_limit_bytes=vmem_mb << 20)


def _pick(n, cands):
    for c in cands:
        if n % c == 0:
            return c
    return n


def _sigmoid(x):
    return 1.0 / (1.0 + jnp.exp(-x))


ADAM_SRC_BYTES = 8 << 20
ADAM_ROW_BYTES = 1 << 20


def _all_gather(xs, name):
    n = len(xs)

    def body(*refs):
        x_refs, out_refs = refs[:n], refs[n:2 * n]
        send_sems, recv_sems, local_sems = refs[2 * n:]
        mx, my, mc = lax.axis_index("x"), lax.axis_index("y"), lax.axis_index("c")
        me, sibling = (mx, my, mc), (mx, my, 1 - mc)
        chips = [(1 - mx, my), (mx, 1 - my), (1 - mx, 1 - my)]

        def slot(a, px, py, pc):
            return out_refs[a].at[4 * px + 2 * py + pc]

        def copy(k, a, block, to, own=False):
            return pltpu.make_async_remote_copy(
                src_ref=x_refs[a] if own else slot(a, *block), dst_ref=slot(a, *block),
                send_sem=send_sems.at[k * n + a], recv_sem=recv_sems.at[k * n + a],
                device_id=to, device_id_type=pl.DeviceIdType.MESH)

        mine = [pltpu.make_async_copy(x_refs[a], slot(a, *me), local_sems.at[a]) for a in range(n)]
        for cp in mine:
            cp.start()
        first = [copy(0, a, me, sibling, own=True) for a in range(n)]
        first += [copy(1 + j, a, me, (*chip, mc), own=True) for a in range(n) for j, chip in enumerate(chips)]
        for cp in first:
            cp.start()
        passed = []
        for a in range(n):
            for j, chip in enumerate(chips):
                copy(1 + j, a, (*chip, mc), me).wait_recv()
                passed.append(copy(4 + j, a, (*chip, mc), sibling))
                passed[-1].start()
        for a in range(n):
            copy(0, a, sibling, me).wait_recv()
            for j, chip in enumerate(chips):
                copy(4 + j, a, (*chip, 1 - mc), me).wait_recv()
        for cp in first + passed:
            cp.wait_send()
        for cp in mine:
            cp.wait()

    hbm = pl.BlockSpec(memory_space=pl.ANY)
    return pl.pallas_call(
        body, name=name,
        out_shape=tuple(jax.ShapeDtypeStruct((NDEV,) + x.shape, x.dtype) for x in xs),
        in_specs=[hbm] * n, out_specs=(hbm,) * n,
        scratch_shapes=[pltpu.SemaphoreType.DMA((7 * n,)), pltpu.SemaphoreType.DMA((7 * n,)),
                        pltpu.SemaphoreType.DMA((n,))],
    )(*xs)


def _all_to_all(xs, name):
    n = len(xs)

    def body(*refs):
        x_refs, out_refs = refs[:n], refs[n:2 * n]
        send_sems, recv_sems, local_sems = refs[2 * n:]
        mx, my, mc = lax.axis_index("x"), lax.axis_index("y"), lax.axis_index("c")
        me = 4 * mx + 2 * my + mc
        mine = [pltpu.make_async_copy(x_refs[a].at[me], out_refs[a].at[me], local_sems.at[a]) for a in range(n)]
        for cp in mine:
            cp.start()
        copies = []
        for a in range(n):
            for rel in range(1, NDEV):
                px = (1 - mx) if rel & 4 else mx
                py = (1 - my) if rel & 2 else my
                pc = (1 - mc) if rel & 1 else mc
                peer = 4 * px + 2 * py + pc
                k = (rel - 1) * n + a
                copies.append(pltpu.make_async_remote_copy(
                    src_ref=x_refs[a].at[peer], dst_ref=out_refs[a].at[me],
                    send_sem=send_sems.at[k], recv_sem=recv_sems.at[k],
                    device_id=(px, py, pc), device_id_type=pl.DeviceIdType.MESH))
        for cp in copies:
            cp.start()
        for cp in copies:
            cp.wait_recv()
        for cp in copies:
            cp.wait_send()
        for cp in mine:
            cp.wait()

    hbm = pl.BlockSpec(memory_space=pl.ANY)
    return pl.pallas_call(
        body, name=name,
        out_shape=tuple(jax.ShapeDtypeStruct(x.shape, x.dtype) for x in xs),
        in_specs=[hbm] * n, out_specs=(hbm,) * n,
        scratch_shapes=[pltpu.SemaphoreType.DMA((7 * n,)), pltpu.SemaphoreType.DMA((7 * n,)),
                        pltpu.SemaphoreType.DMA((n,))],
    )(*xs)


def _mm(a, b, name, ta=False, tb=False, out_dtype=F32, bias=None, precise=False, tm=None, tn=None, tk=None):
    m, k = (a.shape[1], a.shape[0]) if ta else a.shape
    n = b.shape[0] if tb else b.shape[1]
    assert k == (b.shape[1] if tb else b.shape[0])
    tm = tm or _pick(m, (1088, 1024, 512, 256, 128))
    tn = tn or _pick(n, (1024, 512, 384, 256, 128))
    tk = tk or _pick(k, (1024, 1088, 512, 256, 128))
    nk = k // tk
    dn = (((0 if ta else 1,), (1 if tb else 0,)), ((), ()))

    def body(*refs):
        if bias is None:
            a_ref, b_ref, o_ref = refs[:3]
            bias_ref = None
        else:
            a_ref, b_ref, bias_ref, o_ref = refs[:4]
        x, y = a_ref[...], b_ref[...]
        if precise:
            p = lax.dot_general(x.astype(F32), y.astype(F32), dn, preferred_element_type=F32, precision=HIGHEST)
        else:
            p = lax.dot_general(x.astype(BF16), y.astype(BF16), dn, preferred_element_type=F32)

        def finish(acc):
            if bias_ref is not None:
                acc = acc + bias_ref[...]
            o_ref[...] = acc.astype(out_dtype)

        if nk == 1:
            finish(p)
        else:
            acc_ref = refs[-1]
            kk = pl.program_id(2)

            @pl.when(kk == 0)
            def _():
                acc_ref[...] = p

            @pl.when(kk > 0)
            def _():
                acc_ref[...] += p

            @pl.when(kk == nk - 1)
            def _():
                finish(acc_ref[...])

    a_spec = pl.BlockSpec((tk, tm), lambda i, j, kk: (kk, i)) if ta else pl.BlockSpec((tm, tk), lambda i, j, kk: (i, kk))
    b_spec = pl.BlockSpec((tn, tk), lambda i, j, kk: (j, kk)) if tb else pl.BlockSpec((tk, tn), lambda i, j, kk: (kk, j))
    in_specs = [a_spec, b_spec]
    args = [a, b]
    if bias is not None:
        in_specs.append(pl.BlockSpec((1, tn), lambda i, j, kk: (0, j)))
        args.append(bias)
    return pl.pallas_call(
        body, name=name, grid=(m // tm, n // tn, nk),
        in_specs=in_specs, out_specs=pl.BlockSpec((tm, tn), lambda i, j, kk: (i, j)),
        out_shape=jax.ShapeDtypeStruct((m, n), out_dtype),
        scratch_shapes=[pltpu.VMEM((tm, tn), F32)] if nk > 1 else [],
        compiler_params=_cparams(3, 56),
    )(*args)


def _ada_in(cc):
    def body(c_ref, s_ref, d_ref):
        x = c_ref[...]
        sg = _sigmoid(x)
        s_ref[...] = x * sg
        d_ref[...] = sg * (1.0 + x * (1.0 - sg))

    return pl.pallas_call(body, name="ada_in", out_shape=(jax.ShapeDtypeStruct(cc.shape, F32),) * 2)(cc)


def _cctx_grad(t0, t1, dsilu):
    def body(a_ref, b_ref, d_ref, o_ref):
        o_ref[...] = (a_ref[...] + b_ref[...]) * d_ref[...]

    return pl.pallas_call(body, name="cctx_grad", out_shape=jax.ShapeDtypeStruct(t0.shape, F32))(t0, t1, dsilu)


def _seg_spec(nct, rows=3):
    return pl.BlockSpec((None, rows, D), lambda i: (jnp.where(i >= nct, 1, 0), 0, 0))


def _prenorm_fwd(x, g_pre, mod3, nct, name):
    t = x.shape[0]

    def body(x_ref, g_ref, mod_ref, h_ref):
        xv = x_ref[...]
        r = lax.rsqrt(jnp.mean(xv * xv, axis=-1, keepdims=True) + EPS)
        y = xv * r * g_ref[...]
        h_ref[...] = (y * (1.0 + mod_ref[1:2, :]) + mod_ref[0:1, :]).astype(BF16)

    return pl.pallas_call(
        body, name=name, grid=(t // TM,),
        in_specs=[pl.BlockSpec((TM, D), lambda i: (i, 0)), pl.BlockSpec((1, D), lambda i: (0, 0)), _seg_spec(nct)],
        out_specs=pl.BlockSpec((TM, D), lambda i: (i, 0)),
        out_shape=jax.ShapeDtypeStruct((t, D), BF16), compiler_params=_cparams(1, 32),
    )(x, g_pre, mod3)


def _prenorm_bwd(dh, x, dxo, g_pre, mod3, nct, name):
    t = x.shape[0]

    def body(dh_ref, x_ref, dxo_ref, g_ref, mod_ref, dx_ref, dsh_ref, dsc_ref, dg_ref):
        i = pl.program_id(0)
        xv, dhv, g = x_ref[...], dh_ref[...], g_ref[...]
        r = lax.rsqrt(jnp.mean(xv * xv, axis=-1, keepdims=True) + EPS)
        xh = xv * r
        dy = dhv * (1.0 + mod_ref[1:2, :])
        dxh = dy * g
        dx_ref[...] = dxo_ref[...] + r * (dxh - xh * jnp.mean(dxh * xh, axis=-1, keepdims=True))

        @pl.when((i == 0) | (i == nct))
        def _():
            dsh_ref[...] = jnp.zeros_like(dsh_ref)
            dsc_ref[...] = jnp.zeros_like(dsc_ref)

        @pl.when(i == 0)
        def _():
            dg_ref[...] = jnp.zeros_like(dg_ref)

        dsh_ref[...] += jnp.sum(dhv, axis=0, keepdims=True)
        dsc_ref[...] += jnp.sum(dhv * (xh * g), axis=0, keepdims=True)
        dg_ref[...] += jnp.sum(dy * xh, axis=0, keepdims=True)

    row = pl.BlockSpec((TM, D), lambda i: (i, 0))
    seg8 = pl.BlockSpec((None, 8, D), lambda i: (jnp.where(i >= nct, 1, 0), 0, 0))
    return pl.pallas_call(
        body, name=name, grid=(t // TM,),
        in_specs=[row, row, row, pl.BlockSpec((1, D), lambda i: (0, 0)), _seg_spec(nct)],
        out_specs=(row, seg8, seg8, pl.BlockSpec((8, D), lambda i: (0, 0))),
        out_shape=(jax.ShapeDtypeStruct((t, D), F32), jax.ShapeDtypeStruct((2, 8, D), F32),
                   jax.ShapeDtypeStruct((2, 8, D), F32), jax.ShapeDtypeStruct((8, D), F32)),
        compiler_params=_cparams(1, 32),
    )(dh, x, dxo, g_pre, mod3)


def _post_fwd(x, out, g_post, mod3, nct, name):
    t = x.shape[0]

    def body(x_ref, o_ref, g_ref, mod_ref, y_ref):
        ov = o_ref[...]
        r = lax.rsqrt(jnp.mean(ov * ov, axis=-1, keepdims=True) + EPS)
        y_ref[...] = x_ref[...] + mod_ref[2:3, :] * (ov * r * g_ref[...])

    row = pl.BlockSpec((TM, D), lambda i: (i, 0))
    return pl.pallas_call(
        body, name=name, grid=(t // TM,),
        in_specs=[row, row, pl.BlockSpec((1, D), lambda i: (0, 0)), _seg_spec(nct)],
        out_specs=row, out_shape=jax.ShapeDtypeStruct((t, D), F32), compiler_params=_cparams(1, 32),
    )(x, out, g_post, mod3)


def _post_bwd(dxo, out, g_post, mod3, nct, name):
    t = out.shape[0]

    def body(dx_ref, o_ref, g_ref, mod_ref, do_ref, dgt_ref, dg_ref):
        i = pl.program_id(0)
        ov, dxv, g = o_ref[...], dx_ref[...], g_ref[...]
        r = lax.rsqrt(jnp.mean(ov * ov, axis=-1, keepdims=True) + EPS)
        nh = ov * r
        dn = dxv * mod_ref[2:3, :]
        dnh = dn * g
        do_ref[...] = (r * (dnh - nh * jnp.mean(dnh * nh, axis=-1, keepdims=True))).astype(BF16)

        @pl.when((i == 0) | (i == nct))
        def _():
            dgt_ref[...] = jnp.zeros_like(dgt_ref)

        @pl.when(i == 0)
        def _():
            dg_ref[...] = jnp.zeros_like(dg_ref)

        dgt_ref[...] += jnp.sum(dxv * (nh * g), axis=0, keepdims=True)
        dg_ref[...] += jnp.sum(dn * nh, axis=0, keepdims=True)

    row = pl.BlockSpec((TM, D), lambda i: (i, 0))
    seg8 = pl.BlockSpec((None, 8, D), lambda i: (jnp.where(i >= nct, 1, 0), 0, 0))
    return pl.pallas_call(
        body, name=name, grid=(t // TM,),
        in_specs=[row, row, pl.BlockSpec((1, D), lambda i: (0, 0)), _seg_spec(nct)],
        out_specs=(row, seg8, pl.BlockSpec((8, D), lambda i: (0, 0))),
        out_shape=(jax.ShapeDtypeStruct((t, D), BF16), jax.ShapeDtypeStruct((2, 8, D), F32),
                   jax.ShapeDtypeStruct((8, D), F32)),
        compiler_params=_cparams(1, 32),
    )(dxo, out, g_post, mod3)


def _loss_grad(y, target, nct, name):
    t = y.shape[0]

    def body(y_ref, t_ref, dy_ref, l_ref):
        i = pl.program_id(0)

        @pl.when(i == 0)
        def _():
            l_ref[...] = jnp.zeros_like(l_ref)

        @pl.when(i < nct)
        def _():
            dy_ref[...] = jnp.zeros_like(dy_ref)

        @pl.when(i >= nct)
        def _():
            err = y_ref[...] - t_ref[...]
            dy_ref[...] = err / D
            l_ref[...] += jnp.sum(jnp.sum(err * err, axis=1, keepdims=True), axis=0, keepdims=True)

    row = pl.BlockSpec((TM, D), lambda i: (i, 0))
    return pl.pallas_call(
        body, name=name, grid=(t // TM,),
        in_specs=[row, pl.BlockSpec((TM, D), lambda i: (jnp.maximum(i - nct, 0), 0))],
        out_specs=(row, pl.BlockSpec((8, LANE), lambda i: (0, 0))),
        out_shape=(jax.ShapeDtypeStruct((t, D), F32), jax.ShapeDtypeStruct((8, LANE), F32)),
        compiler_params=_cparams(1, 32),
    )(y, target)


def _pcol(name, width):
    assert OFF[name] % width == 0
    blk = OFF[name] // width
    return pl.BlockSpec((TM, width), lambda i: (i, blk))


def _shift_rows(u, prev_row, next_row):
    n = u.shape[0]
    row = lax.broadcasted_iota(jnp.int32, u.shape, 0)
    prev = jnp.where(row == 0, prev_row, pltpu.roll(u, 1, 0))
    nxt = jnp.where(row == n - 1, next_row, pltpu.roll(u, n - 1, 0))
    return prev, nxt


def _halo_specs(width, nt, blk=0):
    r8 = TM // 8
    prev = pl.BlockSpec((8, width), lambda i: (jnp.maximum(i * r8 - 1, 0), blk))
    nxt = pl.BlockSpec((8, width), lambda i: (jnp.minimum((i + 1) * r8, nt * r8 - 1), blk))
    return prev, nxt


def _conv_fwd(p, conv_w8, nct, name):
    t = p.shape[0]
    nt = t // TM

    def body(ab_ref, ac_ref, ax_ref, az_ref, acp_ref, axp_ref, acn_ref, axn_ref, w_ref, cv_ref, ya_ref):
        i = pl.program_id(0)
        u = ac_ref[...] * ax_ref[...]
        mp = jnp.where((i == 0) | (i == nct), 0.0, 1.0)
        mn = jnp.where((i == nct - 1) | (i == nt - 1), 0.0, 1.0)
        prev, nxt = _shift_rows(u, acp_ref[7:8, :] * axp_ref[7:8, :] * mp, acn_ref[0:1, :] * axn_ref[0:1, :] * mn)
        cv = w_ref[0:1, :] * prev + w_ref[1:2, :] * u + w_ref[2:3, :] * nxt
        az = az_ref[...]
        cv_ref[...] = cv
        ya_ref[...] = (ab_ref[...] * cv * (az * _sigmoid(az))).astype(BF16)

    acp, acn = _halo_specs(D, nt, OFF["a_c"] // D)
    axp, axn = _halo_specs(D, nt, OFF["a_x"] // D)
    row = pl.BlockSpec((TM, D), lambda i: (i, 0))
    return pl.pallas_call(
        body, name=name, grid=(nt,),
        in_specs=[_pcol("a_b", D), _pcol("a_c", D), _pcol("a_x", D), _pcol("a_z", D), acp, axp, acn, axn,
                  pl.BlockSpec((8, D), lambda i: (0, 0))],
        out_specs=(row, row),
        out_shape=(jax.ShapeDtypeStruct((t, D), F32), jax.ShapeDtypeStruct((t, D), BF16)),
        compiler_params=_cparams(1, 40),
    )(p, p, p, p, p, p, p, p, conv_w8)


def _conv_bwd_a(dya, p, cv, name):
    t = p.shape[0]

    def body(dy_ref, ab_ref, az_ref, cv_ref, dcv_ref, dab_ref, daz_ref):
        dy, ab, az, c = dy_ref[...], ab_ref[...], az_ref[...], cv_ref[...]
        sg = _sigmoid(az)
        sz = az * sg
        dcv_ref[...] = dy * ab * sz
        dab_ref[...] = (dy * c * sz).astype(BF16)
        daz_ref[...] = (dy * ab * c * (sg * (1.0 + az * (1.0 - sg)))).astype(BF16)

    row = pl.BlockSpec((TM, D), lambda i: (i, 0))
    return pl.pallas_call(
        body, name=name, grid=(t // TM,),
        in_specs=[row, _pcol("a_b", D), _pcol("a_z", D), row], out_specs=(row, row, row),
        out_shape=(jax.ShapeDtypeStruct((t, D), F32), jax.ShapeDtypeStruct((t, D), BF16),
                   jax.ShapeDtypeStruct((t, D), BF16)),
        compiler_params=_cparams(1, 40),
    )(dya, p, p, cv)


def _conv_bwd_b(dcv, p, conv_w8, nct, name):
    t = p.shape[0]
    nt = t // TM

    def body(dcv_ref, dp_ref, dn_ref, ac_ref, ax_ref, w_ref, dac_ref, dax_ref, dw_ref):
        i = pl.program_id(0)
        d, ac, ax = dcv_ref[...], ac_ref[...], ax_ref[...]
        u = ac * ax
        mp = jnp.where((i == 0) | (i == nct), 0.0, 1.0)
        mn = jnp.where((i == nct - 1) | (i == nt - 1), 0.0, 1.0)
        dprev, dnxt = _shift_rows(d, dp_ref[7:8, :] * mp, dn_ref[0:1, :] * mn)
        du = w_ref[0:1, :] * dnxt + w_ref[1:2, :] * d + w_ref[2:3, :] * dprev
        dac_ref[...] = (du * ax).astype(BF16)
        dax_ref[...] = (du * ac).astype(BF16)

        @pl.when(i == 0)
        def _():
            dw_ref[...] = jnp.zeros_like(dw_ref)

        dw0 = jnp.sum(u * dnxt, axis=0, keepdims=True)
        dw1 = jnp.sum(u * d, axis=0, keepdims=True)
        dw2 = jnp.sum(u * dprev, axis=0, keepdims=True)
        r8 = lax.broadcasted_iota(jnp.int32, (8, D), 0)
        dw_ref[...] += jnp.where(r8 == 0, dw0, jnp.where(r8 == 1, dw1, jnp.where(r8 == 2, dw2, 0.0)))

    dp, dn = _halo_specs(D, nt)
    row = pl.BlockSpec((TM, D), lambda i: (i, 0))
    return pl.pallas_call(
        body, name=name, grid=(nt,),
        in_specs=[row, dp, dn, _pcol("a_c", D), _pcol("a_x", D), pl.BlockSpec((8, D), lambda i: (0, 0))],
        out_specs=(row, row, pl.BlockSpec((8, D), lambda i: (0, 0))),
        out_shape=(jax.ShapeDtypeStruct((t, D), BF16), jax.ShapeDtypeStruct((t, D), BF16),
                   jax.ShapeDtypeStruct((8, D), F32)),
        compiler_params=_cparams(1, 40),
    )(dcv, dcv, dcv, p, p, conv_w8)


def _rot_half(x):
    lane = lax.broadcasted_iota(jnp.int32, x.shape, 1)
    return jnp.where((lane % 64) < 32, pltpu.roll(x, 96, 1), pltpu.roll(x, 32, 1))


def _qk_prep_fwd(p, qg, kg, cos_t, sin_t, name):
    t = p.shape[0]

    def body(q_ref, k_ref, qg_ref, kg_ref, c_ref, s_ref, qo_ref, ko_ref):
        c, s = c_ref[...], s_ref[...]

        def one(xv, g):
            y = xv * lax.rsqrt(jnp.mean(xv * xv, axis=-1, keepdims=True) + EPS) * g
            return (y * c + _rot_half(y) * s).astype(BF16)

        for h in range(NH):
            qo_ref[:, h * HD:(h + 1) * HD] = one(q_ref[:, h * HD:(h + 1) * HD], qg_ref[...])
        for h in range(NKV):
            ko_ref[:, h * HD:(h + 1) * HD] = one(k_ref[:, h * HD:(h + 1) * HD], kg_ref[...])

    vec = pl.BlockSpec((1, HD), lambda i: (0, 0))
    tab = pl.BlockSpec((TM, HD), lambda i: (i, 0))
    return pl.pallas_call(
        body, name=name, grid=(t // TM,),
        in_specs=[_pcol("q", NH * HD), _pcol("k", NKV * HD), vec, vec, tab, tab],
        out_specs=(pl.BlockSpec((TM, NH * HD), lambda i: (i, 0)), pl.BlockSpec((TM, NKV * HD), lambda i: (i, 0))),
        out_shape=(jax.ShapeDtypeStruct((t, NH * HD), BF16), jax.ShapeDtypeStruct((t, NKV * HD), BF16)),
        compiler_params=_cparams(1, 32),
    )(p, p, qg, kg, cos_t, sin_t)


def _qk_prep_bwd(dqr, dkr, p, qg, kg, cos_t, sin_t, name):
    t = p.shape[0]

    def body(dq_ref, dk_ref, q_ref, k_ref, qg_ref, kg_ref, c_ref, s_ref, dqo_ref, dko_ref, dqg_ref, dkg_ref):
        i = pl.program_id(0)
        c, s = c_ref[...], s_ref[...]

        @pl.when(i == 0)
        def _():
            dqg_ref[...] = jnp.zeros_like(dqg_ref)
            dkg_ref[...] = jnp.zeros_like(dkg_ref)

        def one(dyr, xv, g):
            dy = dyr * c + _rot_half(dyr * s)
            r = lax.rsqrt(jnp.mean(xv * xv, axis=-1, keepdims=True) + EPS)
            xh = xv * r
            dxh = dy * g
            dx = r * (dxh - xh * jnp.mean(dxh * xh, axis=-1, keepdims=True))
            return dx.astype(BF16), jnp.sum(dy * xh, axis=0, keepdims=True)

        for h in range(NH):
            sl = slice(h * HD, (h + 1) * HD)
            dx, dg = one(dq_ref[:, sl], q_ref[:, sl], qg_ref[...])
            dqo_ref[:, sl] = dx
            dqg_ref[...] += dg
        for h in range(NKV):
            sl = slice(h * HD, (h + 1) * HD)
            dx, dg = one(dk_ref[:, sl], k_ref[:, sl], kg_ref[...])
            dko_ref[:, sl] = dx
            dkg_ref[...] += dg

    vec = pl.BlockSpec((1, HD), lambda i: (0, 0))
    tab = pl.BlockSpec((TM, HD), lambda i: (i, 0))
    acc = pl.BlockSpec((8, HD), lambda i: (0, 0))
    qrow = pl.BlockSpec((TM, NH * HD), lambda i: (i, 0))
    krow = pl.BlockSpec((TM, NKV * HD), lambda i: (i, 0))
    return pl.pallas_call(
        body, name=name, grid=(t // TM,),
        in_specs=[qrow, krow, _pcol("q", NH * HD), _pcol("k", NKV * HD), vec, vec, tab, tab],
        out_specs=(qrow, krow, acc, acc),
        out_shape=(jax.ShapeDtypeStruct((t, NH * HD), BF16), jax.ShapeDtypeStruct((t, NKV * HD), BF16),
                   jax.ShapeDtypeStruct((8, HD), F32), jax.ShapeDtypeStruct((8, HD), F32)),
        compiler_params=_cparams(1, 32),
    )(dqr, dkr, p, p, qg, kg, cos_t, sin_t)


def _attn_probs(q, k, i, nct, ctx):
    s = lax.dot_general(q, k, (((1,), (1,)), ((), ())), preferred_element_type=F32) * ATTN_SCALE
    col = lax.broadcasted_iota(jnp.int32, s.shape, 1)
    s = jnp.where((i < nct) & (col >= ctx), NEG, s)
    e = jnp.exp(s - jnp.max(s, axis=-1, keepdims=True))
    return e / jnp.sum(e, axis=-1, keepdims=True)


def _attn_fwd(qr, kr, p, nct, name):
    t = qr.shape[0]
    ctx = nct * TM
    vblk = OFF["v"] // HD

    def body(q_ref, k_ref, v_ref, o_ref):
        pr = _attn_probs(q_ref[...], k_ref[...], pl.program_id(1), nct, ctx)
        o_ref[...] = jnp.dot(pr.astype(BF16), v_ref[...].astype(BF16), preferred_element_type=F32)

    return pl.pallas_call(
        body, name=name, grid=(NH, t // TM),
        in_specs=[pl.BlockSpec((TM, HD), lambda h, i: (i, h)),
                  pl.BlockSpec((t, HD), lambda h, i: (0, h // GROUP)),
                  pl.BlockSpec((t, HD), lambda h, i: (0, vblk + h // GROUP))],
        out_specs=pl.BlockSpec((TM, HD), lambda h, i: (i, h)),
        out_shape=jax.ShapeDtypeStruct((t, NH * HD), F32), compiler_params=_cparams(2, 48),
    )(qr, kr, p)


def _attn_bwd(qr, kr, p, o, do, nct, name):
    t = qr.shape[0]
    ctx = nct * TM
    vblk = OFF["v"] // HD

    def body(q_ref, k_ref, v_ref, o_ref, do_ref, dq_ref, dk_ref, dv_ref):
        g, i = pl.program_id(1), pl.program_id(2)
        q, k = q_ref[...], k_ref[...]
        vb = v_ref[...].astype(BF16)
        dov = do_ref[...]
        dob = dov.astype(BF16)
        pr = _attn_probs(q, k, i, nct, ctx)
        dp = lax.dot_general(dob, vb, (((1,), (1,)), ((), ())), preferred_element_type=F32)
        drow = jnp.sum(dov * o_ref[...], axis=-1, keepdims=True)
        ds = (pr * (dp - drow) * ATTN_SCALE).astype(BF16)
        dq_ref[...] = jnp.dot(ds, k, preferred_element_type=F32)

        @pl.when((g == 0) & (i == 0))
        def _():
            dk_ref[...] = jnp.zeros_like(dk_ref)
            dv_ref[...] = jnp.zeros_like(dv_ref)

        dk_ref[...] += lax.dot_general(ds, q, (((0,), (0,)), ((), ())), preferred_element_type=F32)
        dv_ref[...] += lax.dot_general(pr.astype(BF16), dob, (((0,), (0,)), ((), ())), preferred_element_type=F32)

    qspec = pl.BlockSpec((TM, HD), lambda kv, g, i: (i, kv * GROUP + g))
    return pl.pallas_call(
        body, name=name, grid=(NKV, GROUP, t // TM),
        in_specs=[qspec, pl.BlockSpec((t, HD), lambda kv, g, i: (0, kv)),
                  pl.BlockSpec((t, HD), lambda kv, g, i: (0, vblk + kv)), qspec, qspec],
        out_specs=(qspec, pl.BlockSpec((t, HD), lambda kv, g, i: (0, kv)), pl.BlockSpec((t, HD), lambda kv, g, i: (0, kv))),
        out_shape=(jax.ShapeDtypeStruct((t, NH * HD), F32), jax.ShapeDtypeStruct((t, NKV * HD), F32),
                   jax.ShapeDtypeStruct((t, NKV * HD), F32)),
        compiler_params=_cparams(3, 48),
    )(qr, kr, p, o, do)


def _decay_fwd(p, wd, bd, name):
    t = p.shape[0]

    def body(r_ref, w_ref, b_ref, z_ref, la_ref):
        z = jnp.dot(r_ref[...].astype(BF16), w_ref[...].astype(BF16), preferred_element_type=F32) + b_ref[...]
        z_ref[...] = z
        la_ref[...] = (jnp.minimum(z, 0.0) - jnp.log(1.0 + jnp.exp(-jnp.abs(z)))) / GLA_TAU

    row = pl.BlockSpec((TM, D), lambda i: (i, 0))
    return pl.pallas_call(
        body, name=name, grid=(t // TM,),
        in_specs=[_pcol("r", R_PAD), pl.BlockSpec((R_PAD, D), lambda i: (0, 0)), pl.BlockSpec((1, D), lambda i: (0, 0))],
        out_specs=(row, row),
        out_shape=(jax.ShapeDtypeStruct((t, D), F32), jax.ShapeDtypeStruct((t, D), F32)),
        compiler_params=_cparams(1, 32),
    )(p, wd, bd)


def _chunk_order(s, ncc, nc, rev):
    if not rev:
        return s
    return jnp.where(s < ncc, ncc - 1 - s, nc - 1 - (s - ncc))


def _gla_chunk(qv, kv, lav, rev):
    r = lax.broadcasted_iota(jnp.int32, (CH, CH), 0)
    c = lax.broadcasted_iota(jnp.int32, (CH, CH), 1)
    keep = (c >= r) if rev else (c <= r)
    tri = keep.astype(F32)
    bc = jnp.dot(tri, lav, preferred_element_type=F32, precision=HIGHEST)
    bl = jnp.sum(lav, axis=0, keepdims=True)
    qt = qv * GLA_SCALE * jnp.exp(bc)
    kt = kv * jnp.exp(-bc)
    kh = kv * jnp.exp(bl - bc)
    return qt, kt, jnp.exp(bl), kh, keep, bc


_NT = (((1,), (1,)), ((), ()))
_TN = (((0,), (0,)), ((), ()))


def _gla_specs(ncc, nc, rev, backward):
    def idx(s):
        return _chunk_order((nc - 1 - s) if backward else s, ncc, nc, rev)

    wk, wv = GH * GDK, GH * GDV
    qb, kb, vb = OFF["gq"] // wk, OFF["gk"] // wk, OFF["gv"] // wv
    lab = 1 if rev else 0
    q = pl.BlockSpec((CH, wk), lambda s: (idx(s), qb))
    k = pl.BlockSpec((CH, wk), lambda s: (idx(s), kb))
    v = pl.BlockSpec((CH, wv), lambda s: (idx(s), vb))
    la = pl.BlockSpec((CH, wk), lambda s: (idx(s), lab))
    o = pl.BlockSpec((CH, wv), lambda s: (idx(s), 0))
    dk = pl.BlockSpec((CH, wk), lambda s: (idx(s), 0))
    st = pl.BlockSpec((None, GH, GDV, GDK), lambda s: (idx(s), 0, 0, 0))
    return q, k, v, la, o, dk, st


def _gla_fwd(p, la, ncc, rev, name):
    t = p.shape[0]
    nc = t // CH
    q_s, k_s, v_s, la_s, o_s, _, st_s = _gla_specs(ncc, nc, rev, False)

    def body(q_ref, k_ref, v_ref, la_ref, o_ref, st_ref, s_scr):
        @pl.when(pl.program_id(0) == 0)
        def _():
            s_scr[...] = jnp.zeros_like(s_scr)

        for h in range(GH):
            sk, sv = slice(h * GDK, (h + 1) * GDK), slice(h * GDV, (h + 1) * GDV)
            qt, kt, gl, kh, keep, _ = _gla_chunk(q_ref[:, sk], k_ref[:, sk], la_ref[:, sk], rev)
            st = s_scr[h]
            st_ref[h] = st
            vb = v_ref[:, sv].astype(BF16)
            qb = qt.astype(BF16)
            a = jnp.where(keep, lax.dot_general(qb, kt.astype(BF16), _NT, preferred_element_type=F32), 0.0)
            o_ref[:, sv] = (lax.dot_general(qb, st.astype(BF16), _NT, preferred_element_type=F32)
                            + jnp.dot(a.astype(BF16), vb, preferred_element_type=F32))
            s_scr[h] = st * gl + lax.dot_general(vb, kh.astype(BF16), _TN, preferred_element_type=F32)

    return pl.pallas_call(
        body, name=name, grid=(nc,),
        in_specs=[q_s, k_s, v_s, la_s], out_specs=(o_s, st_s),
        out_shape=(jax.ShapeDtypeStruct((t, GH * GDV), F32), jax.ShapeDtypeStruct((nc, GH, GDV, GDK), F32)),
        scratch_shapes=[pltpu.VMEM((GH, GDV, GDK), F32)], compiler_params=_cparams(1, 32),
    )(p, p, p, la)


def _gla_bwd(p, la, do, st, ncc, rev, name):
    t = p.shape[0]
    nc = t // CH
    q_s, k_s, v_s, la_s, o_s, dk_s, st_s = _gla_specs(ncc, nc, rev, True)

    def body(q_ref, k_ref, v_ref, la_ref, do_ref, st_ref, dq_ref, dk_ref, dv_ref, dla_ref, ds_scr):
        @pl.when(pl.program_id(0) == 0)
        def _():
            ds_scr[...] = jnp.zeros_like(ds_scr)

        row = lax.broadcasted_iota(jnp.int32, (CH, GDK), 0)
        r = lax.broadcasted_iota(jnp.int32, (CH, CH), 0)
        c = lax.broadcasted_iota(jnp.int32, (CH, CH), 1)
        trit = ((c <= r) if rev else (c >= r)).astype(F32)
        last = 0 if rev else CH - 1
        for h in range(GH):
            sk, sv = slice(h * GDK, (h + 1) * GDK), slice(h * GDV, (h + 1) * GDV)
            qt, kt, gl, kh, keep, bc = _gla_chunk(q_ref[:, sk], k_ref[:, sk], la_ref[:, sk], rev)
            stv = st_ref[h]
            dsn = ds_scr[h]
            dsb = dsn.astype(BF16)
            vb, dob = v_ref[:, sv].astype(BF16), do_ref[:, sv].astype(BF16)
            qb, kb = qt.astype(BF16), kt.astype(BF16)
            a = jnp.where(keep, lax.dot_general(qb, kb, _NT, preferred_element_type=F32), 0.0).astype(BF16)
            da = jnp.where(keep, lax.dot_general(dob, vb, _NT, preferred_element_type=F32), 0.0).astype(BF16)
            dqt = (jnp.dot(dob, stv.astype(BF16), preferred_element_type=F32)
                   + jnp.dot(da, kb, preferred_element_type=F32))
            dkh = jnp.dot(vb, dsb, preferred_element_type=F32)
            dkt = lax.dot_general(da, qb, _TN, preferred_element_type=F32) + dkh * gl
            dv_ref[:, sv] = (lax.dot_general(a, dob, _TN, preferred_element_type=F32)
                             + lax.dot_general(kh.astype(BF16), dsb, _NT, preferred_element_type=F32))
            ds_scr[h] = lax.dot_general(dob, qb, _TN, preferred_element_type=F32) + dsn * gl
            dgl = jnp.sum(stv * dsn, axis=0, keepdims=True) + jnp.sum(dkh * kt, axis=0, keepdims=True)
            dbc = dqt * qt - dkt * kt + jnp.where(row == last, dgl * gl, 0.0)
            dla_ref[:, sk] = jnp.dot(trit, dbc, preferred_element_type=F32, precision=HIGHEST)
            dq_ref[:, sk] = dqt * (GLA_SCALE * jnp.exp(bc))
            dk_ref[:, sk] = dkt * jnp.exp(-bc)

    return pl.pallas_call(
        body, name=name, grid=(nc,),
        in_specs=[q_s, k_s, v_s, la_s, o_s, st_s], out_specs=(dk_s, dk_s, o_s, dk_s),
        out_shape=(jax.ShapeDtypeStruct((t, GH * GDK), F32), jax.ShapeDtypeStruct((t, GH * GDK), F32),
                   jax.ShapeDtypeStruct((t, GH * GDV), F32), jax.ShapeDtypeStruct((t, GH * GDK), F32)),
        scratch_shapes=[pltpu.VMEM((GH, GDV, GDK), F32)], compiler_params=_cparams(1, 32),
    )(p, p, p, la, do, st)


def _gla_merge_bwd(gf, gb, z, p, wd, name):
    t = p.shape[0]
    w2 = GH * GDK

    def body(dqf, dkf, dvf, dlf, dqb, dkb, dvb, dlb, z_ref, r_ref, w_ref, dq_ref, dk_ref, dv_ref, dr_ref, db_ref, dw_ref):
        i = pl.program_id(0)
        dq_ref[...] = (dqf[...] + dqb[...]).astype(BF16)
        dk_ref[...] = (dkf[...] + dkb[...]).astype(BF16)
        dv_ref[...] = (dvf[...] + dvb[...]).astype(BF16)
        zv = z_ref[...]
        dz = jnp.concatenate([dlf[...], dlb[...]], axis=1) * (_sigmoid(-zv) / GLA_TAU)
        dzb = dz.astype(BF16)
        dr_ref[...] = lax.dot_general(dzb, w_ref[...].astype(BF16), _NT, preferred_element_type=F32).astype(BF16)

        @pl.when(i == 0)
        def _():
            db_ref[...] = jnp.zeros_like(db_ref)
            dw_ref[...] = jnp.zeros_like(dw_ref)

        db_ref[...] += jnp.sum(dz, axis=0, keepdims=True)
        dw_ref[...] += lax.dot_general(r_ref[...].astype(BF16), dzb, _TN, preferred_element_type=F32)

    half = pl.BlockSpec((TM, w2), lambda i: (i, 0))
    row = pl.BlockSpec((TM, D), lambda i: (i, 0))
    wspec = pl.BlockSpec((R_PAD, D), lambda i: (0, 0))
    return pl.pallas_call(
        body, name=name, grid=(t // TM,),
        in_specs=[half, half, row, half, half, half, row, half, row, _pcol("r", R_PAD), wspec],
        out_specs=(half, half, row, pl.BlockSpec((TM, R_PAD), lambda i: (i, 0)),
                   pl.BlockSpec((8, D), lambda i: (0, 0)), wspec),
        out_shape=(jax.ShapeDtypeStruct((t, w2), BF16), jax.ShapeDtypeStruct((t, w2), BF16),
                   jax.ShapeDtypeStruct((t, D), BF16), jax.ShapeDtypeStruct((t, R_PAD), BF16),
                   jax.ShapeDtypeStruct((8, D), F32), jax.ShapeDtypeStruct((R_PAD, D), F32)),
        compiler_params=_cparams(1, 40),
    )(*gf, *gb, z, p, wd)


def _branch_fwd(att, of, ob, p, gla_g, name):
    t = p.shape[0]

    def body(att_ref, of_ref, ob_ref, za_ref, zg_ref, g_ref, yb_ref, yc_ref):
        za = za_ref[...]
        yb_ref[...] = (att_ref[...] * (za * _sigmoid(za))).astype(BF16)
        for h in range(GH):
            sl = slice(h * GDV, (h + 1) * GDV)
            o = of_ref[:, sl] + ob_ref[:, sl]
            n = o * lax.rsqrt(jnp.mean(o * o, axis=-1, keepdims=True) + EPS) * g_ref[...]
            zh = zg_ref[:, sl]
            yc_ref[:, sl] = (n * (zh * _sigmoid(zh))).astype(BF16)

    row = pl.BlockSpec((TM, D), lambda i: (i, 0))
    return pl.pallas_call(
        body, name=name, grid=(t // TM,),
        in_specs=[row, row, row, _pcol("z_attn", D), _pcol("zg", D), pl.BlockSpec((1, GDV), lambda i: (0, 0))],
        out_specs=(row, row),
        out_shape=(jax.ShapeDtypeStruct((t, D), BF16), jax.ShapeDtypeStruct((t, D), BF16)),
        compiler_params=_cparams(1, 40),
    )(att, of, ob, p, p, gla_g)


def _branch_bwd(dyb, dyc, att, of, ob, p, gla_g, name):
    t = p.shape[0]

    def body(dyb_ref, dyc_ref, att_ref, of_ref, ob_ref, za_ref, zg_ref, g_ref, datt_ref, dza_ref, do_ref, dzg_ref, dg_ref):
        i = pl.program_id(0)

        @pl.when(i == 0)
        def _():
            dg_ref[...] = jnp.zeros_like(dg_ref)

        za, dyb = za_ref[...], dyb_ref[...]
        sa = _sigmoid(za)
        datt_ref[...] = dyb * (za * sa)
        dza_ref[...] = (dyb * att_ref[...] * (sa * (1.0 + za * (1.0 - sa)))).astype(BF16)
        g = g_ref[...]
        for h in range(GH):
            sl = slice(h * GDV, (h + 1) * GDV)
            o = of_ref[:, sl] + ob_ref[:, sl]
            r = lax.rsqrt(jnp.mean(o * o, axis=-1, keepdims=True) + EPS)
            oh = o * r
            zh, dyc = zg_ref[:, sl], dyc_ref[:, sl]
            sg = _sigmoid(zh)
            dn = dyc * (zh * sg)
            dzg_ref[:, sl] = (dyc * (oh * g) * (sg * (1.0 + zh * (1.0 - sg)))).astype(BF16)
            doh = dn * g
            do_ref[:, sl] = r * (doh - oh * jnp.mean(doh * oh, axis=-1, keepdims=True))
            dg_ref[...] += jnp.sum(dn * oh, axis=0, keepdims=True)

    row = pl.BlockSpec((TM, D), lambda i: (i, 0))
    return pl.pallas_call(
        body, name=name, grid=(t // TM,),
        in_specs=[row, row, row, row, row, _pcol("z_attn", D), _pcol("zg", D), pl.BlockSpec((1, GDV), lambda i: (0, 0))],
        out_specs=(row, row, row, row, pl.BlockSpec((8, GDV), lambda i: (0, 0))),
        out_shape=(jax.ShapeDtypeStruct((t, D), F32), jax.ShapeDtypeStruct((t, D), BF16),
                   jax.ShapeDtypeStruct((t, D), F32), jax.ShapeDtypeStruct((t, D), BF16),
                   jax.ShapeDtypeStruct((8, GDV), F32)),
        compiler_params=_cparams(1, 48),
    )(dyb, dyc, att, of, ob, p, p, gla_g)


def _merge_fwd(bra, brb, brc, p, b_gate, name):
    t = p.shape[0]
    mgb = OFF["mg"] // D

    def body(a_ref, b_ref, c_ref, ga_ref, gb_ref, gc_ref, bg_ref, m_ref):
        m_ref[...] = (_sigmoid(ga_ref[...] + bg_ref[:, 0:D]) * a_ref[...]
                      + _sigmoid(gb_ref[...] + bg_ref[:, D:2 * D]) * b_ref[...]
                      + _sigmoid(gc_ref[...] + bg_ref[:, 2 * D:3 * D]) * c_ref[...]).astype(BF16)

    row = pl.BlockSpec((TM, D), lambda i: (i, 0))
    gates = [pl.BlockSpec((TM, D), functools.partial(lambda i, b: (i, b), b=mgb + j)) for j in range(3)]
    return pl.pallas_call(
        body, name=name, grid=(t // TM,),
        in_specs=[row, row, row, *gates, pl.BlockSpec((1, 3 * D), lambda i: (0, 0))],
        out_specs=row, out_shape=jax.ShapeDtypeStruct((t, D), BF16), compiler_params=_cparams(1, 40),
    )(bra, brb, brc, p, p, p, b_gate)


def _merge_bwd(dm, bra, brb, brc, p, b_gate, name):
    t = p.shape[0]
    mgb = OFF["mg"] // D

    def body(dm_ref, a_ref, b_ref, c_ref, ga_ref, gb_ref, gc_ref, bg_ref, da_ref, db_ref, dc_ref, dmg_ref, dbg_ref):
        i = pl.program_id(0)

        @pl.when(i == 0)
        def _():
            dbg_ref[...] = jnp.zeros_like(dbg_ref)

        dm = dm_ref[...]
        for j, (br_ref, g_ref, d_ref) in enumerate(((a_ref, ga_ref, da_ref), (b_ref, gb_ref, db_ref), (c_ref, gc_ref, dc_ref))):
            sl = slice(j * D, (j + 1) * D)
            g = _sigmoid(g_ref[...] + bg_ref[:, sl])
            d_ref[...] = (dm * g).astype(BF16)
            dmg = dm * br_ref[...] * (g * (1.0 - g))
            dmg_ref[:, sl] = dmg.astype(BF16)
            dbg_ref[:, sl] += jnp.sum(dmg, axis=0, keepdims=True)

    row = pl.BlockSpec((TM, D), lambda i: (i, 0))
    gates = [pl.BlockSpec((TM, D), functools.partial(lambda i, b: (i, b), b=mgb + j)) for j in range(3)]
    return pl.pallas_call(
        body, name=name, grid=(t // TM,),
        in_specs=[row, row, row, row, *gates, pl.BlockSpec((1, 3 * D), lambda i: (0, 0))],
        out_specs=(row, row, row, pl.BlockSpec((TM, 3 * D), lambda i: (i, 0)), pl.BlockSpec((8, 3 * D), lambda i: (0, 0))),
        out_shape=(jax.ShapeDtypeStruct((t, D), BF16),) * 3 + (jax.ShapeDtypeStruct((t, 3 * D), BF16),
                                                                jax.ShapeDtypeStruct((8, 3 * D), F32)),
        compiler_params=_cparams(1, 48),
    )(dm, bra, brb, brc, p, p, p, b_gate)


def _adamw(gsrc, w, m, v, name):
    ns, nl, r, c = gsrc.shape
    lanes = -(-c // LANE) * LANE
    tr = r
    for cand in range(16, r, 16):
        if r % cand == 0 and ns * cand * lanes * gsrc.dtype.itemsize <= ADAM_SRC_BYTES and cand * lanes * 4 <= ADAM_ROW_BYTES:
            tr = cand
    if r * lanes * 4 <= ADAM_ROW_BYTES and ns * r * lanes * gsrc.dtype.itemsize <= ADAM_SRC_BYTES:
        tr = r

    def body(g_ref, w_ref, m_ref, v_ref, go_ref, d_ref, mo_ref, vo_ref):
        g = g_ref[0].astype(F32)
        for s in range(1, ns):
            g = g + g_ref[s].astype(F32)
        mn = ADAM_B1 * m_ref[...] + (1.0 - ADAM_B1) * g
        vn = ADAM_B2 * v_ref[...] + (1.0 - ADAM_B2) * jnp.square(g)
        m_hat = mn / (1.0 - ADAM_B1 ** ADAM_STEP)
        v_hat = vn / (1.0 - ADAM_B2 ** ADAM_STEP)
        go_ref[...] = g
        d_ref[...] = -ADAM_LR * (m_hat / (jnp.sqrt(v_hat) + ADAM_EPS) + ADAM_WD * w_ref[...])
        mo_ref[...] = mn
        vo_ref[...] = vn

    row = pl.BlockSpec((None, tr, c), lambda l, i: (l, i, 0))
    return pl.pallas_call(
        body, name=name, grid=(nl, r // tr),
        in_specs=[pl.BlockSpec((ns, None, tr, c), lambda l, i: (0, l, i, 0)), row, row, row],
        out_specs=(row,) * 4, out_shape=(jax.ShapeDtypeStruct((nl, r, c), F32),) * 4,
        compiler_params=_cparams(2, 48),
    )(gsrc, w, m, v)


def _slot_sum(gsrc, name):
    ns, r, _ = gsrc.shape

    def body(g_ref, o_ref):
        g = g_ref[0]
        for s in range(1, ns):
            g = g + g_ref[s]
        o_ref[...] = g

    return pl.pallas_call(body, name=name, out_shape=jax.ShapeDtypeStruct((r, LANE), F32))(gsrc)


def _pack(parts, dtype, row_align):
    chunk = row_align * LANE
    out, spans, o = [], [], 0
    for a in parts:
        f = a.reshape(-1).astype(dtype)
        n = f.shape[0]
        pad = (-n) % chunk
        if pad:
            f = jnp.concatenate([f, jnp.zeros((pad,), dtype)])
        out.append(f.reshape(-1, LANE))
        spans.append((o, n))
        o += (n + pad) // LANE
    return jnp.concatenate(out, axis=0), spans


def _unpack(packed, spans, shapes):
    res = []
    for (o, n), shp in zip(spans, shapes):
        rows = -(-n // LANE)
        res.append(packed[o:o + rows].reshape(-1)[:n].reshape(shp))
    return res


def _rope_tables(ctx, seq):
    n_rows = seq // GRID_W
    pairs = HD // 4
    row = jnp.repeat(jnp.arange(n_rows, dtype=F32), GRID_W)
    col = jnp.tile(jnp.arange(GRID_W, dtype=F32), n_rows)
    freqs = ROPE_THETA ** (-jnp.arange(pairs, dtype=F32) * 2.0 / (HD // 2))
    ar, ac = row[:, None] * freqs, col[:, None] * freqs
    cos_l = jnp.concatenate([jnp.cos(ar), jnp.cos(ar), jnp.cos(ac), jnp.cos(ac)], axis=1)
    sin_l = jnp.concatenate([-jnp.sin(ar), jnp.sin(ar), -jnp.sin(ac), jnp.sin(ac)], axis=1)
    cos_t = jnp.concatenate([jnp.ones((ctx, HD), F32), cos_l], axis=0)
    sin_t = jnp.concatenate([jnp.zeros((ctx, HD), F32), sin_l], axis=0)
    return cos_t, sin_t


def _to_proj_layout(w):
    parts = [w[:, s:s + wd] for _, s, wd in _SEGS]
    used = sum(wd for _, _, wd in _SEGS)
    parts.append(jnp.zeros((w.shape[0], NP - used), w.dtype))
    return jnp.concatenate(parts, axis=1)


def _from_proj_layout(g):
    order = sorted(_SEGS, key=lambda sg: sg[1])
    return jnp.concatenate([g[:, OFF[n]:OFF[n] + wd] for n, _, wd in order], axis=1)


def _row0(a):
    return a[..., 0, :]


def kernel(x, c, ctx, c_ctx, w_ada, b_ada, g_pre, g_post, w_in, conv_w, q_norm_g, k_norm_g, w_decay_fwd, b_decay_fwd, w_decay_bwd, b_decay_bwd, gla_norm_g, w_br_conv, w_br_attn, w_br_gla, b_gate, w_out, loss_target, m_c_ctx, m_w_ada, m_b_ada, m_g_pre, m_g_post, m_w_in, m_conv_w, m_q_norm_g, m_k_norm_g, m_w_decay_fwd, m_b_decay_fwd, m_w_decay_bwd, m_b_decay_bwd, m_gla_norm_g, m_w_br_conv, m_w_br_attn, m_w_br_gla, m_b_gate, m_w_out, v_c_ctx, v_w_ada, v_b_ada, v_g_pre, v_g_post, v_w_in, v_conv_w, v_q_norm_g, v_k_norm_g, v_w_decay_fwd, v_b_decay_fwd, v_w_decay_bwd, v_b_decay_bwd, v_gla_norm_g, v_w_br_conv, v_w_br_attn, v_w_br_gla, v_b_gate, v_w_out):
    seq, n_ctx = x.shape[1], ctx.shape[1]
    assert n_ctx % TM == 0 and seq % TM == 0 and seq % GRID_W == 0
    t = n_ctx + seq
    nct, ncc = n_ctx // TM, n_ctx // CH
    dev = 4 * lax.axis_index("x") + 2 * lax.axis_index("y") + lax.axis_index("c")
    ada_w = w_ada.shape[2]
    in_w = w_in.shape[2]
    br_r = w_br_conv.shape[1]

    big_parts = [w_ada, w_in, w_br_conv, w_br_attn, w_br_gla, w_out]
    wall = _all_gather([w.astype(BF16) for w in big_parts], "gather_weights")
    small_parts = [conv_w, w_decay_fwd, w_decay_bwd]
    spack, sspans = _pack(small_parts, F32, 8)
    sall, = _all_gather([spack], "gather_small")

    def gathered(all_, spans, k, shp):
        o, n = spans[k]
        rows = n // LANE
        return all_[:, o:o + rows].reshape((NDEV,) + shp)

    w_ada_f = wall[0].transpose(1, 2, 0, 3).reshape(DEPTH, D, 3 * D)
    w_in_f = wall[1].transpose(1, 2, 0, 3).reshape(DEPTH, D, IN_WIDTH)
    w_brs_f = [wall[2 + j].transpose(1, 0, 2, 3).reshape(DEPTH, D, D) for j in range(4)]
    conv_f = gathered(sall, sspans, 0, (DEPTH, 3, D // NDEV)).transpose(1, 2, 0, 3).reshape(DEPTH, 3, D)
    wdf_f = gathered(sall, sspans, 1, (DEPTH, GLA_RANK, GH * GDK // NDEV)).transpose(1, 2, 0, 3).reshape(DEPTH, GLA_RANK, GH * GDK)
    wdb_f = gathered(sall, sspans, 2, (DEPTH, GLA_RANK, GH * GDK // NDEV)).transpose(1, 2, 0, 3).reshape(DEPTH, GLA_RANK, GH * GDK)

    cos_t, sin_t = _rope_tables(n_ctx, seq)
    cc = jnp.concatenate([c_ctx[None, :], c.reshape(1, D), jnp.zeros((6, D), F32)], axis=0)
    silu_cc, dsilu_cc = _ada_in(cc)

    wp, conv8, wd_pad, bd = [], [], [], []
    for l in range(DEPTH):
        wp.append(_to_proj_layout(w_in_f[l]))
        conv8.append(jnp.concatenate([conv_f[l], jnp.zeros((5, D), F32)], axis=0))
        zr = jnp.zeros((GLA_RANK, GH * GDK), F32)
        wd_pad.append(jnp.concatenate([jnp.concatenate([wdf_f[l], zr], axis=1), jnp.concatenate([zr, wdb_f[l]], axis=1),
                                       jnp.zeros((R_PAD - 2 * GLA_RANK, D), F32)], axis=0))
        bd.append(jnp.concatenate([b_decay_fwd[l], b_decay_bwd[l]])[None, :])

    xs = jnp.concatenate([ctx[0], x[0]], axis=0)
    saved = []
    for l in range(DEPTH):
        n = f"l{l}_"
        mod = _mm(silu_cc, w_ada_f[l], n + "mod", bias=b_ada[l][None, :])
        mod3 = mod[0:2].reshape(2, 3, D)
        h = _prenorm_fwd(xs, g_pre[l][None, :], mod3, nct, n + "prenorm")
        p = _mm(h, wp[l], n + "proj")
        cv, ya = _conv_fwd(p, conv8[l], nct, n + "conv")
        qr, kr = _qk_prep_fwd(p, q_norm_g[l][None, :], k_norm_g[l][None, :], cos_t, sin_t, n + "qk_prep")
        att = _attn_fwd(qr, kr, p, nct, n + "attn")
        z, la = _decay_fwd(p, wd_pad[l], bd[l], n + "decay")
        of, stf = _gla_fwd(p, la, ncc, False, n + "gla_f")
        ob, stb = _gla_fwd(p, la, ncc, True, n + "gla_b")
        yb, yc = _branch_fwd(att, of, ob, p, gla_norm_g[l][None, :], n + "branch")
        bra = _mm(ya, w_brs_f[0][l], n + "br_conv")
        brb = _mm(yb, w_brs_f[1][l], n + "br_attn")
        brc = _mm(yc, w_brs_f[2][l], n + "br_gla")
        mm_ = _merge_fwd(bra, brb, brc, p, b_gate[l][None, :], n + "merge")
        out = _mm(mm_, w_brs_f[3][l], n + "out")
        x_new = _post_fwd(xs, out, g_post[l][None, :], mod3, nct, n + "post")
        saved.append(dict(x=xs, mod3=mod3, h=h, p=p, cv=cv, ya=ya, qr=qr, kr=kr, att=att, z=z, la=la, of=of, ob=ob,
                          stf=stf, stb=stb, yb=yb, yc=yc, bra=bra, brb=brb, brc=brc, m=mm_, out=out))
        xs = x_new

    dx, sq = _loss_grad(xs, loss_target[0], nct, "loss")
    loss = lax.psum(0.5 * sq[0, 0] / D, ("x", "y", "c"))

    gw = {k: [None] * DEPTH for k in ("w_in", "br_conv", "br_attn", "br_gla", "out", "b_gate", "g_pre", "g_post",
                                      "conv_w", "qg", "kg", "wd", "bdec", "gla_g", "dmod")}
    dctx = []
    for l in reversed(range(DEPTH)):
        n = f"l{l}_b_"
        s = saved[l]
        p = s["p"]
        d_out, dgt, gw["g_post"][l] = _post_bwd(dx, s["out"], g_post[l][None, :], s["mod3"], nct, n + "post")
        dm = _mm(d_out, w_brs_f[3][l], n + "dm", tb=True)
        gw["out"][l] = _mm(s["m"], d_out, n + "dw_out", ta=True, out_dtype=BF16)
        dbra, dbrb, dbrc, dmg, gw["b_gate"][l] = _merge_bwd(dm, s["bra"], s["brb"], s["brc"], p, b_gate[l][None, :], n + "merge")
        dya = _mm(dbra, w_brs_f[0][l], n + "dya", tb=True)
        dyb = _mm(dbrb, w_brs_f[1][l], n + "dyb", tb=True)
        dyc = _mm(dbrc, w_brs_f[2][l], n + "dyc", tb=True)
        gw["br_conv"][l] = _mm(s["ya"], dbra, n + "dw_conv", ta=True, out_dtype=BF16)
        gw["br_attn"][l] = _mm(s["yb"], dbrb, n + "dw_attn", ta=True, out_dtype=BF16)
        gw["br_gla"][l] = _mm(s["yc"], dbrc, n + "dw_gla", ta=True, out_dtype=BF16)
        dcv, dab, daz = _conv_bwd_a(dya, p, s["cv"], n + "conv_a")
        dac, dax, gw["conv_w"][l] = _conv_bwd_b(dcv, p, conv8[l], nct, n + "conv_b")
        datt, dza, dgo, dzg, gw["gla_g"][l] = _branch_bwd(dyb, dyc, s["att"], s["of"], s["ob"], p, gla_norm_g[l][None, :], n + "branch")
        dqr, dkr, dv = _attn_bwd(s["qr"], s["kr"], p, s["att"], datt, nct, n + "attn")
        dq, dk, gw["qg"][l], gw["kg"][l] = _qk_prep_bwd(dqr, dkr, p, q_norm_g[l][None, :], k_norm_g[l][None, :], cos_t, sin_t, n + "qk_prep")
        gf = _gla_bwd(p, s["la"], dgo, s["stf"], ncc, False, n + "gla_f")
        gb = _gla_bwd(p, s["la"], dgo, s["stb"], ncc, True, n + "gla_b")
        dgq, dgk, dgv, dr, gw["bdec"][l], gw["wd"][l] = _gla_merge_bwd(gf, gb, s["z"], p, wd_pad[l], n + "gla_merge")
        pieces = dict(a_b=dab, a_c=dac, a_x=dax, a_z=daz, q=dq, z_attn=dza, gv=dgv, zg=dzg, mg=dmg, gq=dgq, gk=dgk,
                      k=dk, v=dv.astype(BF16), r=dr)
        used = sum(wd_ for _, _, wd_ in _SEGS) - 32 + R_PAD
        dp = jnp.concatenate([pieces[nm] for nm, _, _ in _SEGS] + [jnp.zeros((t, NP - used), BF16)], axis=1)
        dh = _mm(dp, wp[l], n + "dh", tb=True)
        gw["w_in"][l] = _mm(s["h"], dp, n + "dw_in", ta=True, out_dtype=BF16)
        dx, dsh, dsc, gw["g_pre"][l] = _prenorm_bwd(dh, s["x"], dx, g_pre[l][None, :], s["mod3"], nct, n + "prenorm")
        dmod = jnp.stack([_row0(dsh), _row0(dsc), _row0(dgt)], axis=1).reshape(2, 3 * D)
        gw["dmod"][l] = dmod
        dmod8 = jnp.concatenate([dmod, jnp.zeros((6, 3 * D), F32)], axis=0)
        dctx.append(_mm(dmod8, w_ada_f[l], n + "dsilu", tb=True))
    grad_x = dx[n_ctx:][None]
    g_cctx = _cctx_grad(dctx[0], dctx[1], dsilu_cc)[0]

    def st2(name):
        return jnp.stack(gw[name])

    g_b_ada = jnp.stack([gw["dmod"][l][0] + gw["dmod"][l][1] for l in range(DEPTH)])
    g_bdf = jnp.stack([gw["bdec"][l][0, :GH * GDK] for l in range(DEPTH)])
    g_bdb = jnp.stack([gw["bdec"][l][0, GH * GDK:] for l in range(DEPTH)])
    g_wdf = jnp.stack([gw["wd"][l][0:GLA_RANK, :GH * GDK] for l in range(DEPTH)])
    g_wdb = jnp.stack([gw["wd"][l][GLA_RANK:2 * GLA_RANK, GH * GDK:] for l in range(DEPTH)])
    rep_names = ["c_ctx", "b_ada", "g_pre", "g_post", "q_norm_g", "k_norm_g", "b_decay_fwd", "b_decay_bwd", "gla_norm_g", "b_gate"]
    rep_grads = [g_cctx, g_b_ada, st2("g_pre")[:, 0], st2("g_post")[:, 0], st2("qg")[:, 0], st2("kg")[:, 0], g_bdf, g_bdb,
                 st2("gla_g")[:, 0], st2("b_gate")[:, 0]]
    rep_w = [c_ctx, b_ada, g_pre, g_post, q_norm_g, k_norm_g, b_decay_fwd, b_decay_bwd, gla_norm_g, b_gate]
    rep_m = [m_c_ctx, m_b_ada, m_g_pre, m_g_post, m_q_norm_g, m_k_norm_g, m_b_decay_fwd, m_b_decay_bwd, m_gla_norm_g, m_b_gate]
    rep_v = [v_c_ctx, v_b_ada, v_g_pre, v_g_post, v_q_norm_g, v_k_norm_g, v_b_decay_fwd, v_b_decay_bwd, v_gla_norm_g, v_b_gate]
    shard_grads = [st2("conv_w")[:, 0:3], g_wdf, g_wdb]
    extra = [silu_cc[0:2], jnp.stack(gw["dmod"])]
    gpack, gspans = _pack(rep_grads + shard_grads + extra, F32, 8)
    gall, = _all_gather([gpack], "gather_small_grads")
    n_rep = len(rep_grads)
    rep_rows = gspans[n_rep][0]
    shard_rows = gspans[n_rep + len(shard_grads)][0]

    wpk, rspans = _pack(rep_w, F32, 8)
    mpk, _ = _pack(rep_m, F32, 8)
    vpk, _ = _pack(rep_v, F32, 8)
    rep_out = _adamw(gall[:, None, :rep_rows], wpk[None], mpk[None], vpk[None], "adam_rep")
    rep_shapes = [a.shape for a in rep_w]
    rep_g, rep_d, rep_nm, rep_nv = [_unpack(o[0], rspans, rep_shapes) for o in rep_out]

    ssum = _slot_sum(gall[:, rep_rows:shard_rows], "sum_small_sharded")
    sh_spans = [(o - rep_rows, n_) for o, n_ in gspans[n_rep:n_rep + len(shard_grads)]]
    g_conv_full, g_wdf_full, g_wdb_full = _unpack(ssum, sh_spans, [(DEPTH, 3, D), (DEPTH, GLA_RANK, GH * GDK), (DEPTH, GLA_RANK, GH * GDK)])
    cw, dw = D // NDEV, GH * GDK // NDEV
    sh_g = [lax.dynamic_slice_in_dim(g_conv_full, dev * cw, cw, axis=2),
            lax.dynamic_slice_in_dim(g_wdf_full, dev * dw, dw, axis=2),
            lax.dynamic_slice_in_dim(g_wdb_full, dev * dw, dw, axis=2)]
    sgp, shs = _pack(sh_g, F32, 8)
    swp, _ = _pack([conv_w, w_decay_fwd, w_decay_bwd], F32, 8)
    smp, _ = _pack([m_conv_w, m_w_decay_fwd, m_w_decay_bwd], F32, 8)
    svp, _ = _pack([v_conv_w, v_w_decay_fwd, v_w_decay_bwd], F32, 8)
    sh_out = _adamw(sgp[None, None], swp[None], smp[None], svp[None], "adam_small_sharded")
    sh_shapes = [conv_w.shape, w_decay_fwd.shape, w_decay_bwd.shape]
    sh_gr, sh_d, sh_nm, sh_nv = [_unpack(o[0], shs, sh_shapes) for o in sh_out]

    eo, en = gspans[n_rep + len(shard_grads)]
    a_all = gall[:, eo:eo + en // LANE].reshape(NDEV * 2, D)
    eo2, en2 = gspans[n_rep + len(shard_grads) + 1]
    d_all = gall[:, eo2:eo2 + en2 // LANE].reshape(NDEV, DEPTH, 2, 3 * D).transpose(1, 0, 2, 3).reshape(DEPTH, NDEV * 2, 3 * D)
    g_ada = jnp.stack([_mm(a_all, lax.dynamic_slice_in_dim(d_all[l], dev * ada_w, ada_w, axis=1), f"dw_ada{l}",
                           ta=True, precise=True, tk=NDEV * 2) for l in range(DEPTH)])
    ada_g, ada_d, ada_nm, ada_nv = _adamw(g_ada[None], w_ada, m_w_ada, v_w_ada, "adam_ada")

    gin = jnp.stack([_from_proj_layout(gw["w_in"][l]) for l in range(DEPTH)])
    slots = [gin.reshape(DEPTH, D, NDEV, in_w).transpose(2, 0, 1, 3)]
    for name in ("br_conv", "br_attn", "br_gla", "out"):
        slots.append(st2(name).reshape(DEPTH, NDEV, br_r, D).transpose(1, 0, 2, 3))
    grecv = _all_to_all(slots, "exchange_grads")
    big_w = [w_in, w_br_conv, w_br_attn, w_br_gla, w_out]
    big_m = [m_w_in, m_w_br_conv, m_w_br_attn, m_w_br_gla, m_w_out]
    big_v = [v_w_in, v_w_br_conv, v_w_br_attn, v_w_br_gla, v_w_out]
    big_out = [_adamw(grecv[j], big_w[j], big_m[j], big_v[j], f"adam_big{j}") for j in range(len(big_w))]
    big_g, big_d, big_nm, big_nv = [[o[k] for o in big_out] for k in range(4)]

    def ordered(rep, ada, big, sh):
        c_ctx_, b_ada_, g_pre_, g_post_, qg_, kg_, bdf_, bdb_, glag_, bgate_ = rep
        w_in_, brc_, bra_, brg_, wout_ = big
        conv_, wdf_, wdb_ = sh
        return [c_ctx_, ada, b_ada_, g_pre_, g_post_, w_in_, conv_, qg_, kg_, wdf_, bdf_, wdb_, bdb_, glag_,
                brc_, bra_, brg_, bgate_, wout_]

    return (loss, grad_x,
            *ordered(rep_g, ada_g, big_g, sh_gr), *ordered(rep_d, ada_d, big_d, sh_d),
            *ordered(rep_nm, ada_nm, big_nm, sh_nm), *ordered(rep_nv, ada_nv, big_nv, sh_nv))
```

```python
import functools

import numpy as np
import jax
import jax.numpy as jnp
from jax import lax
from jax.experimental import pallas as pl
from jax.experimental.pallas import tpu as pltpu

F32, BF16 = jnp.float32, jnp.bfloat16
HIGHEST = lax.Precision.HIGHEST

D = 1024
DEPTH = 2
GRID_W = 64
NH, NKV, HD = 8, 2, 128
GROUP = NH // NKV
ROPE_THETA = 10000.0
ATTN_SCALE = HD ** -0.5
GH, GDK, GDV = 4, 128, 256
GLA_RANK = 16
GLA_TAU = 16.0
CH = 64
GLA_SCALE = GDK ** -0.5
EPS = 1e-6
NDEV = 8
LANE = 128
TM = 256
NEG = -1e30

ADAM_LR, ADAM_B1, ADAM_B2, ADAM_EPS, ADAM_WD, ADAM_STEP = 0.001, 0.9, 0.999, 1e-08, 0.01, 10

_SEGS = (("a_b", 0, 1024), ("a_c", 1024, 1024), ("a_x", 2048, 1024), ("a_z", 3072, 1024),
         ("q", 4096, 1024), ("z_attn", 5632, 1024), ("gv", 7680, 1024), ("zg", 8736, 1024),
         ("mg", 9760, 3072), ("gq", 6656, 512), ("gk", 7168, 512), ("k", 5120, 256), ("v", 5376, 256),
         ("r", 8704, 32))
IN_WIDTH = 12832
NP = 13312
OFF = {}
_o = 0
for _n, _s, _w in _SEGS:
    OFF[_n] = _o
    _o += _w
R_PAD = 128


def _cparams(ngrid, vmem_mb):
    return pltpu.CompilerParams(dimension_semantics=("arbitrary",) * ngrid, vmem_limit_bytes=vmem_mb << 20)


def _pick(n, cands):
    for c in cands:
        if n % c == 0:
            return c
    return n


def _sigmoid(x):
    return 1.0 / (1.0 + jnp.exp(-x))


ADAM_SRC_BYTES = 8 << 20
ADAM_ROW_BYTES = 1 << 20


def _all_gather(xs, name):
    return _comm_alone(_GatherRider(xs), name)


_HBM = pl.BlockSpec(memory_space=pl.ANY)


class _Rider:
    def __init__(self, xs, out_shapes):
        self.xs, self.n = list(xs), len(xs)
        self.out_shape = [jax.ShapeDtypeStruct(s, x.dtype) for s, x in zip(out_shapes, xs)]
        self.scratch = [pltpu.SemaphoreType.DMA((7 * self.n,)), pltpu.SemaphoreType.DMA((7 * self.n,)),
                        pltpu.SemaphoreType.DMA((self.n,))]


class _GatherRider(_Rider):
    def __init__(self, xs):
        super().__init__(xs, [(NDEV,) + x.shape for x in xs])

    def _parts(self, x_refs, out_refs, sems):
        n = self.n
        send_sems, recv_sems, local_sems = sems
        mx, my, mc = lax.axis_index("x"), lax.axis_index("y"), lax.axis_index("c")
        me, sibling = (mx, my, mc), (mx, my, 1 - mc)
        chips = [(1 - mx, my), (mx, 1 - my), (1 - mx, 1 - my)]

        def slot(a, px, py, pc):
            return out_refs[a].at[4 * px + 2 * py + pc]

        def copy(k, a, block, to, own=False):
            return pltpu.make_async_remote_copy(
                src_ref=x_refs[a] if own else slot(a, *block), dst_ref=slot(a, *block),
                send_sem=send_sems.at[k * n + a], recv_sem=recv_sems.at[k * n + a],
                device_id=to, device_id_type=pl.DeviceIdType.MESH)

        mine = [pltpu.make_async_copy(x_refs[a], slot(a, *me), local_sems.at[a]) for a in range(n)]
        first = [copy(0, a, me, sibling, own=True) for a in range(n)]
        first += [copy(1 + j, a, me, (*chip, mc), own=True) for a in range(n) for j, chip in enumerate(chips)]
        landed = [copy(1 + j, a, (*chip, mc), me) for a in range(n) for j, chip in enumerate(chips)]
        passed = [copy(4 + j, a, (*chip, mc), sibling) for a in range(n) for j, chip in enumerate(chips)]
        from_sibling = [copy(0, a, sibling, me) for a in range(n)]
        from_sibling += [copy(4 + j, a, (*chip, 1 - mc), me) for a in range(n) for j, chip in enumerate(chips)]
        return mine, first, landed, passed, from_sibling

    def start(self, x_refs, out_refs, sems):
        mine, first, _, _, _ = self._parts(x_refs, out_refs, sems)
        for cp in mine + first:
            cp.start()

    def middle(self, x_refs, out_refs, sems):
        _, _, landed, passed, _ = self._parts(x_refs, out_refs, sems)
        for got, fwd in zip(landed, passed):
            got.wait_recv()
            fwd.start()

    def finish(self, x_refs, out_refs, sems):
        mine, first, _, passed, from_sibling = self._parts(x_refs, out_refs, sems)
        for cp in from_sibling:
            cp.wait_recv()
        for cp in first + passed:
            cp.wait_send()
        for cp in mine:
            cp.wait()


class _ExchangeRider(_Rider):
    def __init__(self, xs):
        super().__init__(xs, [x.shape for x in xs])

    def _parts(self, x_refs, out_refs, sems):
        n = self.n
        send_sems, recv_sems, local_sems = sems
        mx, my, mc = lax.axis_index("x"), lax.axis_index("y"), lax.axis_index("c")
        me = 4 * mx + 2 * my + mc
        mine = [pltpu.make_async_copy(x_refs[a].at[me], out_refs[a].at[me], local_sems.at[a]) for a in range(n)]
        copies = []
        for a in range(n):
            for rel in range(1, NDEV):
                px = (1 - mx) if rel & 4 else mx
                py = (1 - my) if rel & 2 else my
                pc = (1 - mc) if rel & 1 else mc
                peer = 4 * px + 2 * py + pc
                k = (rel - 1) * n + a
                copies.append(pltpu.make_async_remote_copy(
                    src_ref=x_refs[a].at[peer], dst_ref=out_refs[a].at[me],
                    send_sem=send_sems.at[k], recv_sem=recv_sems.at[k],
                    device_id=(px, py, pc), device_id_type=pl.DeviceIdType.MESH))
        return mine, copies

    def start(self, x_refs, out_refs, sems):
        mine, copies = self._parts(x_refs, out_refs, sems)
        for cp in mine + copies:
            cp.start()

    def middle(self, x_refs, out_refs, sems):
        pass

    def finish(self, x_refs, out_refs, sems):
        mine, copies = self._parts(x_refs, out_refs, sems)
        for cp in copies:
            cp.wait_recv()
        for cp in copies:
            cp.wait_send()
        for cp in mine:
            cp.wait()


def _comm_alone(rider, name):
    n = rider.n

    def body(*refs):
        x_refs, out_refs, sems = refs[:n], refs[n:2 * n], refs[2 * n:]
        rider.start(x_refs, out_refs, sems)
        rider.middle(x_refs, out_refs, sems)
        rider.finish(x_refs, out_refs, sems)

    return pl.pallas_call(
        body, name=name, out_shape=tuple(rider.out_shape), in_specs=[_HBM] * n, out_specs=(_HBM,) * n,
        scratch_shapes=rider.scratch,
    )(*rider.xs)


def _with_rider(body, nin, nout, rider, first, mid, last):
    if rider is None:
        return body
    n = rider.n

    def wrapped(*refs):
        ins, x_refs = refs[:nin], refs[nin:nin + n]
        outs, out_refs = refs[nin + n:nin + n + nout], refs[nin + n + nout:nin + 2 * n + nout]
        scratch, sems = refs[nin + 2 * n + nout:-3], refs[-3:]

        @pl.when(first())
        def _():
            rider.start(x_refs, out_refs, sems)

        body(*ins, *outs, *scratch)

        @pl.when(mid())
        def _():
            rider.middle(x_refs, out_refs, sems)

        @pl.when(last())
        def _():
            rider.finish(x_refs, out_refs, sems)

    return wrapped


def _all_to_all(xs, name):
    return _comm_alone(_ExchangeRider(xs), name)


def _mm(a, b, name, ta=False, tb=False, out_dtype=F32, bias=None, precise=False, tm=None, tn=None, tk=None):
    m, k = (a.shape[1], a.shape[0]) if ta else a.shape
    n = b.shape[0] if tb else b.shape[1]
    assert k == (b.shape[1] if tb else b.shape[0])
    tm = tm or _pick(m, (1088, 1024, 512, 256, 128))
    tn = tn or _pick(n, (1024, 512, 384, 256, 128))
    tk = tk or _pick(k, (1024, 1088, 512, 256, 128))
    nk = k // tk
    dn = (((0 if ta else 1,), (1 if tb else 0,)), ((), ()))

    def body(*refs):
        if bias is None:
            a_ref, b_ref, o_ref = refs[:3]
            bias_ref = None
        else:
            a_ref, b_ref, bias_ref, o_ref = refs[:4]
        x, y = a_ref[...], b_ref[...]
        if precise:
            p = lax.dot_general(x.astype(F32), y.astype(F32), dn, preferred_element_type=F32, precision=HIGHEST)
        else:
            p = lax.dot_general(x.astype(BF16), y.astype(BF16), dn, preferred_element_type=F32)

        def finish(acc):
            if bias_ref is not None:
                acc = acc + bias_ref[...]
            o_ref[...] = acc.astype(out_dtype)

        if nk == 1:
            finish(p)
        else:
            acc_ref = refs[-1]
            kk = pl.program_id(2)

            @pl.when(kk == 0)
            def _():
                acc_ref[...] = p

            @pl.when(kk > 0)
            def _():
                acc_ref[...] += p

            @pl.when(kk == nk - 1)
            def _():
                finish(acc_ref[...])

    a_spec = pl.BlockSpec((tk, tm), lambda i, j, kk: (kk, i)) if ta else pl.BlockSpec((tm, tk), lambda i, j, kk: (i, kk))
    b_spec = pl.BlockSpec((tn, tk), lambda i, j, kk: (j, kk)) if tb else pl.BlockSpec((tk, tn), lambda i, j, kk: (kk, j))
    in_specs = [a_spec, b_spec]
    args = [a, b]
    if bias is not None:
        in_specs.append(pl.BlockSpec((1, tn), lambda i, j, kk: (0, j)))
        args.append(bias)
    return pl.pallas_call(
        body, name=name, grid=(m // tm, n // tn, nk),
        in_specs=in_specs, out_specs=pl.BlockSpec((tm, tn), lambda i, j, kk: (i, j)),
        out_shape=jax.ShapeDtypeStruct((m, n), out_dtype),
        scratch_shapes=[pltpu.VMEM((tm, tn), F32)] if nk > 1 else [],
        compiler_params=_cparams(3, 56),
    )(*args)


def _ada_in(cc):
    def body(c_ref, s_ref, d_ref):
        x = c_ref[...]
        sg = _sigmoid(x)
        s_ref[...] = x * sg
        d_ref[...] = sg * (1.0 + x * (1.0 - sg))

    return pl.pallas_call(body, name="ada_in", out_shape=(jax.ShapeDtypeStruct(cc.shape, F32),) * 2)(cc)


def _cctx_grad(t0, t1, dsilu):
    def body(a_ref, b_ref, d_ref, o_ref):
        o_ref[...] = (a_ref[...] + b_ref[...]) * d_ref[...]

    return pl.pallas_call(body, name="cctx_grad", out_shape=jax.ShapeDtypeStruct(t0.shape, F32))(t0, t1, dsilu)


def _seg_spec(nct, rows=3):
    return pl.BlockSpec((None, rows, D), lambda i: (jnp.where(i >= nct, 1, 0), 0, 0))


def _prenorm_fwd(x, g_pre, mod3, nct, name):
    t = x.shape[0]

    def body(x_ref, g_ref, mod_ref, h_ref):
        xv = x_ref[...]
        r = lax.rsqrt(jnp.mean(xv * xv, axis=-1, keepdims=True) + EPS)
        y = xv * r * g_ref[...]
        h_ref[...] = (y * (1.0 + mod_ref[1:2, :]) + mod_ref[0:1, :]).astype(BF16)

    return pl.pallas_call(
        body, name=name, grid=(t // TM,),
        in_specs=[pl.BlockSpec((TM, D), lambda i: (i, 0)), pl.BlockSpec((1, D), lambda i: (0, 0)), _seg_spec(nct)],
        out_specs=pl.BlockSpec((TM, D), lambda i: (i, 0)),
        out_shape=jax.ShapeDtypeStruct((t, D), BF16), compiler_params=_cparams(1, 32),
    )(x, g_pre, mod3)


def _prenorm_bwd(dh, x, dxo, g_pre, mod3, nct, name):
    t = x.shape[0]

    def body(dh_ref, x_ref, dxo_ref, g_ref, mod_ref, dx_ref, dsh_ref, dsc_ref, dg_ref):
        i = pl.program_id(0)
        xv, dhv, g = x_ref[...], dh_ref[...], g_ref[...]
        r = lax.rsqrt(jnp.mean(xv * xv, axis=-1, keepdims=True) + EPS)
        xh = xv * r
        dy = dhv * (1.0 + mod_ref[1:2, :])
        dxh = dy * g
        dx_ref[...] = dxo_ref[...] + r * (dxh - xh * jnp.mean(dxh * xh, axis=-1, keepdims=True))

        @pl.when((i == 0) | (i == nct))
        def _():
            dsh_ref[...] = jnp.zeros_like(dsh_ref)
            dsc_ref[...] = jnp.zeros_like(dsc_ref)

        @pl.when(i == 0)
        def _():
            dg_ref[...] = jnp.zeros_like(dg_ref)

        dsh_ref[...] += jnp.sum(dhv, axis=0, keepdims=True)
        dsc_ref[...] += jnp.sum(dhv * (xh * g), axis=0, keepdims=True)
        dg_ref[...] += jnp.sum(dy * xh, axis=0, keepdims=True)

    row = pl.BlockSpec((TM, D), lambda i: (i, 0))
    seg8 = pl.BlockSpec((None, 8, D), lambda i: (jnp.where(i >= nct, 1, 0), 0, 0))
    return pl.pallas_call(
        body, name=name, grid=(t // TM,),
        in_specs=[row, row, row, pl.BlockSpec((1, D), lambda i: (0, 0)), _seg_spec(nct)],
        out_specs=(row, seg8, seg8, pl.BlockSpec((8, D), lambda i: (0, 0))),
        out_shape=(jax.ShapeDtypeStruct((t, D), F32), jax.ShapeDtypeStruct((2, 8, D), F32),
                   jax.ShapeDtypeStruct((2, 8, D), F32), jax.ShapeDtypeStruct((8, D), F32)),
        compiler_params=_cparams(1, 32),
    )(dh, x, dxo, g_pre, mod3)


def _post_fwd(x, out, g_post, mod3, nct, name):
    t = x.shape[0]

    def body(x_ref, o_ref, g_ref, mod_ref, y_ref):
        ov = o_ref[...]
        r = lax.rsqrt(jnp.mean(ov * ov, axis=-1, keepdims=True) + EPS)
        y_ref[...] = x_ref[...] + mod_ref[2:3, :] * (ov * r * g_ref[...])

    row = pl.BlockSpec((TM, D), lambda i: (i, 0))
    return pl.pallas_call(
        body, name=name, grid=(t // TM,),
        in_specs=[row, row, pl.BlockSpec((1, D), lambda i: (0, 0)), _seg_spec(nct)],
        out_specs=row, out_shape=jax.ShapeDtypeStruct((t, D), F32), compiler_params=_cparams(1, 32),
    )(x, out, g_post, mod3)


def _post_bwd(dxo, out, g_post, mod3, nct, name):
    t = out.shape[0]

    def body(dx_ref, o_ref, g_ref, mod_ref, do_ref, dgt_ref, dg_ref):
        i = pl.program_id(0)
        ov, dxv, g = o_ref[...], dx_ref[...], g_ref[...]
        r = lax.rsqrt(jnp.mean(ov * ov, axis=-1, keepdims=True) + EPS)
        nh = ov * r
        dn = dxv * mod_ref[2:3, :]
        dnh = dn * g
        do_ref[...] = (r * (dnh - nh * jnp.mean(dnh * nh, axis=-1, keepdims=True))).astype(BF16)

        @pl.when((i == 0) | (i == nct))
        def _():
            dgt_ref[...] = jnp.zeros_like(dgt_ref)

        @pl.when(i == 0)
        def _():
            dg_ref[...] = jnp.zeros_like(dg_ref)

        dgt_ref[...] += jnp.sum(dxv * (nh * g), axis=0, keepdims=True)
        dg_ref[...] += jnp.sum(dn * nh, axis=0, keepdims=True)

    row = pl.BlockSpec((TM, D), lambda i: (i, 0))
    seg8 = pl.BlockSpec((None, 8, D), lambda i: (jnp.where(i >= nct, 1, 0), 0, 0))
    return pl.pallas_call(
        body, name=name, grid=(t // TM,),
        in_specs=[row, row, pl.BlockSpec((1, D), lambda i: (0, 0)), _seg_spec(nct)],
        out_specs=(row, seg8, pl.BlockSpec((8, D), lambda i: (0, 0))),
        out_shape=(jax.ShapeDtypeStruct((t, D), BF16), jax.ShapeDtypeStruct((2, 8, D), F32),
                   jax.ShapeDtypeStruct((8, D), F32)),
        compiler_params=_cparams(1, 32),
    )(dxo, out, g_post, mod3)


def _loss_grad(y, target, nct, name):
    t = y.shape[0]

    def body(y_ref, t_ref, dy_ref, l_ref):
        i = pl.program_id(0)

        @pl.when(i == 0)
        def _():
            l_ref[...] = jnp.zeros_like(l_ref)

        @pl.when(i < nct)
        def _():
            dy_ref[...] = jnp.zeros_like(dy_ref)

        @pl.when(i >= nct)
        def _():
            err = y_ref[...] - t_ref[...]
            dy_ref[...] = err / D
            l_ref[...] += jnp.sum(jnp.sum(err * err, axis=1, keepdims=True), axis=0, keepdims=True)

    row = pl.BlockSpec((TM, D), lambda i: (i, 0))
    return pl.pallas_call(
        body, name=name, grid=(t // TM,),
        in_specs=[row, pl.BlockSpec((TM, D), lambda i: (jnp.maximum(i - nct, 0), 0))],
        out_specs=(row, pl.BlockSpec((8, LANE), lambda i: (0, 0))),
        out_shape=(jax.ShapeDtypeStruct((t, D), F32), jax.ShapeDtypeStruct((8, LANE), F32)),
        compiler_params=_cparams(1, 32),
    )(y, target)


def _pcol(name, width):
    assert OFF[name] % width == 0
    blk = OFF[name] // width
    return pl.BlockSpec((TM, width), lambda i: (i, blk))


def _shift_rows(u, prev_row, next_row):
    n = u.shape[0]
    row = lax.broadcasted_iota(jnp.int32, u.shape, 0)
    prev = jnp.where(row == 0, prev_row, pltpu.roll(u, 1, 0))
    nxt = jnp.where(row == n - 1, next_row, pltpu.roll(u, n - 1, 0))
    return prev, nxt


def _halo_specs(width, nt, blk=0):
    r8 = TM // 8
    prev = pl.BlockSpec((8, width), lambda i: (jnp.maximum(i * r8 - 1, 0), blk))
    nxt = pl.BlockSpec((8, width), lambda i: (jnp.minimum((i + 1) * r8, nt * r8 - 1), blk))
    return prev, nxt


def _conv_fwd(p, conv_w8, nct, name):
    t = p.shape[0]
    nt = t // TM

    def body(ab_ref, ac_ref, ax_ref, az_ref, acp_ref, axp_ref, acn_ref, axn_ref, w_ref, cv_ref, ya_ref):
        i = pl.program_id(0)
        u = ac_ref[...] * ax_ref[...]
        mp = jnp.where((i == 0) | (i == nct), 0.0, 1.0)
        mn = jnp.where((i == nct - 1) | (i == nt - 1), 0.0, 1.0)
        prev, nxt = _shift_rows(u, acp_ref[7:8, :] * axp_ref[7:8, :] * mp, acn_ref[0:1, :] * axn_ref[0:1, :] * mn)
        cv = w_ref[0:1, :] * prev + w_ref[1:2, :] * u + w_ref[2:3, :] * nxt
        az = az_ref[...]
        cv_ref[...] = cv
        ya_ref[...] = (ab_ref[...] * cv * (az * _sigmoid(az))).astype(BF16)

    acp, acn = _halo_specs(D, nt, OFF["a_c"] // D)
    axp, axn = _halo_specs(D, nt, OFF["a_x"] // D)
    row = pl.BlockSpec((TM, D), lambda i: (i, 0))
    return pl.pallas_call(
        body, name=name, grid=(nt,),
        in_specs=[_pcol("a_b", D), _pcol("a_c", D), _pcol("a_x", D), _pcol("a_z", D), acp, axp, acn, axn,
                  pl.BlockSpec((8, D), lambda i: (0, 0))],
        out_specs=(row, row),
        out_shape=(jax.ShapeDtypeStruct((t, D), F32), jax.ShapeDtypeStruct((t, D), BF16)),
        compiler_params=_cparams(1, 40),
    )(p, p, p, p, p, p, p, p, conv_w8)


def _conv_bwd_a(dya, p, cv, name):
    t = p.shape[0]

    def body(dy_ref, ab_ref, az_ref, cv_ref, dcv_ref, dab_ref, daz_ref):
        dy, ab, az, c = dy_ref[...], ab_ref[...], az_ref[...], cv_ref[...]
        sg = _sigmoid(az)
        sz = az * sg
        dcv_ref[...] = dy * ab * sz
        dab_ref[...] = (dy * c * sz).astype(BF16)
        daz_ref[...] = (dy * ab * c * (sg * (1.0 + az * (1.0 - sg)))).astype(BF16)

    row = pl.BlockSpec((TM, D), lambda i: (i, 0))
    return pl.pallas_call(
        body, name=name, grid=(t // TM,),
        in_specs=[row, _pcol("a_b", D), _pcol("a_z", D), row], out_specs=(row, row, row),
        out_shape=(jax.ShapeDtypeStruct((t, D), F32), jax.ShapeDtypeStruct((t, D), BF16),
                   jax.ShapeDtypeStruct((t, D), BF16)),
        compiler_params=_cparams(1, 40),
    )(dya, p, p, cv)


def _conv_bwd_b(dcv, p, conv_w8, nct, name):
    t = p.shape[0]
    nt = t // TM

    def body(dcv_ref, dp_ref, dn_ref, ac_ref, ax_ref, w_ref, dac_ref, dax_ref, dw_ref):
        i = pl.program_id(0)
        d, ac, ax = dcv_ref[...], ac_ref[...], ax_ref[...]
        u = ac * ax
        mp = jnp.where((i == 0) | (i == nct), 0.0, 1.0)
        mn = jnp.where((i == nct - 1) | (i == nt - 1), 0.0, 1.0)
        dprev, dnxt = _shift_rows(d, dp_ref[7:8, :] * mp, dn_ref[0:1, :] * mn)
        du = w_ref[0:1, :] * dnxt + w_ref[1:2, :] * d + w_ref[2:3, :] * dprev
        dac_ref[...] = (du * ax).astype(BF16)
        dax_ref[...] = (du * ac).astype(BF16)

        @pl.when(i == 0)
        def _():
            dw_ref[...] = jnp.zeros_like(dw_ref)

        dw0 = jnp.sum(u * dnxt, axis=0, keepdims=True)
        dw1 = jnp.sum(u * d, axis=0, keepdims=True)
        dw2 = jnp.sum(u * dprev, axis=0, keepdims=True)
        r8 = lax.broadcasted_iota(jnp.int32, (8, D), 0)
        dw_ref[...] += jnp.where(r8 == 0, dw0, jnp.where(r8 == 1, dw1, jnp.where(r8 == 2, dw2, 0.0)))

    dp, dn = _halo_specs(D, nt)
    row = pl.BlockSpec((TM, D), lambda i: (i, 0))
    return pl.pallas_call(
        body, name=name, grid=(nt,),
        in_specs=[row, dp, dn, _pcol("a_c", D), _pcol("a_x", D), pl.BlockSpec((8, D), lambda i: (0, 0))],
        out_specs=(row, row, pl.BlockSpec((8, D), lambda i: (0, 0))),
        out_shape=(jax.ShapeDtypeStruct((t, D), BF16), jax.ShapeDtypeStruct((t, D), BF16),
                   jax.ShapeDtypeStruct((8, D), F32)),
        compiler_params=_cparams(1, 40),
    )(dcv, dcv, dcv, p, p, conv_w8)


def _rot_half(x):
    lane = lax.broadcasted_iota(jnp.int32, x.shape, 1)
    return jnp.where((lane % 64) < 32, pltpu.roll(x, 96, 1), pltpu.roll(x, 32, 1))


def _qk_prep_fwd(p, qg, kg, cos_t, sin_t, name):
    t = p.shape[0]

    def body(q_ref, k_ref, qg_ref, kg_ref, c_ref, s_ref, qo_ref, ko_ref):
        c, s = c_ref[...], s_ref[...]

        def one(xv, g):
            y = xv * lax.rsqrt(jnp.mean(xv * xv, axis=-1, keepdims=True) + EPS) * g
            return (y * c + _rot_half(y) * s).astype(BF16)

        for h in range(NH):
            qo_ref[:, h * HD:(h + 1) * HD] = one(q_ref[:, h * HD:(h + 1) * HD], qg_ref[...])
        for h in range(NKV):
            ko_ref[:, h * HD:(h + 1) * HD] = one(k_ref[:, h * HD:(h + 1) * HD], kg_ref[...])

    vec = pl.BlockSpec((1, HD), lambda i: (0, 0))
    tab = pl.BlockSpec((TM, HD), lambda i: (i, 0))
    return pl.pallas_call(
        body, name=name, grid=(t // TM,),
        in_specs=[_pcol("q", NH * HD), _pcol("k", NKV * HD), vec, vec, tab, tab],
        out_specs=(pl.BlockSpec((TM, NH * HD), lambda i: (i, 0)), pl.BlockSpec((TM, NKV * HD), lambda i: (i, 0))),
        out_shape=(jax.ShapeDtypeStruct((t, NH * HD), BF16), jax.ShapeDtypeStruct((t, NKV * HD), BF16)),
        compiler_params=_cparams(1, 32),
    )(p, p, qg, kg, cos_t, sin_t)


def _qk_prep_bwd(dqr, dkr, p, qg, kg, cos_t, sin_t, name):
    t = p.shape[0]

    def body(dq_ref, dk_ref, q_ref, k_ref, qg_ref, kg_ref, c_ref, s_ref, dqo_ref, dko_ref, dqg_ref, dkg_ref):
        i = pl.program_id(0)
        c, s = c_ref[...], s_ref[...]

        @pl.when(i == 0)
        def _():
            dqg_ref[...] = jnp.zeros_like(dqg_ref)
            dkg_ref[...] = jnp.zeros_like(dkg_ref)

        def one(dyr, xv, g):
            dy = dyr * c + _rot_half(dyr * s)
            r = lax.rsqrt(jnp.mean(xv * xv, axis=-1, keepdims=True) + EPS)
            xh = xv * r
            dxh = dy * g
            dx = r * (dxh - xh * jnp.mean(dxh * xh, axis=-1, keepdims=True))
            return dx.astype(BF16), jnp.sum(dy * xh, axis=0, keepdims=True)

        for h in range(NH):
            sl = slice(h * HD, (h + 1) * HD)
            dx, dg = one(dq_ref[:, sl], q_ref[:, sl], qg_ref[...])
            dqo_ref[:, sl] = dx
            dqg_ref[...] += dg
        for h in range(NKV):
            sl = slice(h * HD, (h + 1) * HD)
            dx, dg = one(dk_ref[:, sl], k_ref[:, sl], kg_ref[...])
            dko_ref[:, sl] = dx
            dkg_ref[...] += dg

    vec = pl.BlockSpec((1, HD), lambda i: (0, 0))
    tab = pl.BlockSpec((TM, HD), lambda i: (i, 0))
    acc = pl.BlockSpec((8, HD), lambda i: (0, 0))
    qrow = pl.BlockSpec((TM, NH * HD), lambda i: (i, 0))
    krow = pl.BlockSpec((TM, NKV * HD), lambda i: (i, 0))
    return pl.pallas_call(
        body, name=name, grid=(t // TM,),
        in_specs=[qrow, krow, _pcol("q", NH * HD), _pcol("k", NKV * HD), vec, vec, tab, tab],
        out_specs=(qrow, krow, acc, acc),
        out_shape=(jax.ShapeDtypeStruct((t, NH * HD), BF16), jax.ShapeDtypeStruct((t, NKV * HD), BF16),
                   jax.ShapeDtypeStruct((8, HD), F32), jax.ShapeDtypeStruct((8, HD), F32)),
        compiler_params=_cparams(1, 32),
    )(dqr, dkr, p, p, qg, kg, cos_t, sin_t)


def _attn_exp(q, k):
    s = lax.dot_general(q, k, _NT, preferred_element_type=F32) * ATTN_SCALE
    e = jnp.exp(s - jnp.max(s, axis=-1, keepdims=True))
    return e, jnp.sum(e, axis=-1, keepdims=True)


def _attn_fwd(qr, kr, p, nct, name, rider=None):
    t = qr.shape[0]
    nt = t // TM
    ctx = nct * TM
    vblk = OFF["v"] // HD

    def body(q_ref, k_ref, v_ref, o_ref):
        def tile(keys):
            e, l = _attn_exp(q_ref[...], k_ref[keys, :])
            o_ref[...] = jnp.dot(e.astype(BF16), v_ref[keys, :].astype(BF16), preferred_element_type=F32) / l

        pl.when(pl.program_id(1) < nct)(lambda: tile(slice(0, ctx)))
        pl.when(pl.program_id(1) >= nct)(lambda: tile(slice(None)))

    def at(h, i):
        return lambda: (pl.program_id(0) == h) & (pl.program_id(1) == i)

    rn = 0 if rider is None else rider.n
    return pl.pallas_call(
        _with_rider(body, 3, 1, rider, at(0, 0), at(NH * 3 // 4, 0), at(NH - 1, nt - 1)),
        name=name, grid=(NH, nt),
        in_specs=[pl.BlockSpec((TM, HD), lambda h, i: (i, h)),
                  pl.BlockSpec((t, HD), lambda h, i: (0, h // GROUP)),
                  pl.BlockSpec((t, HD), lambda h, i: (0, vblk + h // GROUP))] + [_HBM] * rn,
        out_specs=(pl.BlockSpec((TM, HD), lambda h, i: (i, h)),) + (_HBM,) * rn,
        out_shape=(jax.ShapeDtypeStruct((t, NH * HD), F32),) + (() if rider is None else tuple(rider.out_shape)),
        scratch_shapes=[] if rider is None else rider.scratch,
        compiler_params=_cparams(2, 48),
    )(qr, kr, p, *(() if rider is None else rider.xs))


def _attn_bwd(qr, kr, p, o, do, nct, name, rider=None):
    t = qr.shape[0]
    nt = t // TM
    ctx = nct * TM
    vblk = OFF["v"] // HD

    def body(q_ref, k_ref, v_ref, o_ref, do_ref, dq_ref, dk_ref, dv_ref):
        g, i = pl.program_id(1), pl.program_id(2)

        @pl.when((g == 0) & (i == 0))
        def _():
            dk_ref[...] = jnp.zeros_like(dk_ref)
            dv_ref[...] = jnp.zeros_like(dv_ref)

        def tile(keys):
            q, k = q_ref[...], k_ref[keys, :]
            vb = v_ref[keys, :].astype(BF16)
            dov = do_ref[...]
            dob = dov.astype(BF16)
            e, l = _attn_exp(q, k)
            pr = e * (1.0 / l)
            dp = lax.dot_general(dob, vb, _NT, preferred_element_type=F32)
            drow = jnp.sum(dov * o_ref[...], axis=-1, keepdims=True)
            ds = (pr * (dp - drow) * ATTN_SCALE).astype(BF16)
            dq_ref[...] = jnp.dot(ds, k, preferred_element_type=F32)
            dk_ref[keys, :] += lax.dot_general(ds, q, _TN, preferred_element_type=F32)
            dv_ref[keys, :] += lax.dot_general(pr.astype(BF16), dob, _TN, preferred_element_type=F32)

        pl.when(i < nct)(lambda: tile(slice(0, ctx)))
        pl.when(i >= nct)(lambda: tile(slice(None)))

    def at(kv, g, i):
        return lambda: (pl.program_id(0) == kv) & (pl.program_id(1) == g) & (pl.program_id(2) == i)

    rn = 0 if rider is None else rider.n
    qspec = pl.BlockSpec((TM, HD), lambda kv, g, i: (i, kv * GROUP + g))
    kvspec = pl.BlockSpec((t, HD), lambda kv, g, i: (0, kv))
    return pl.pallas_call(
        _with_rider(body, 5, 3, rider, at(0, 0, 0), at(NKV - 1, 0, 0), at(NKV - 1, GROUP - 1, nt - 1)),
        name=name, grid=(NKV, GROUP, nt),
        in_specs=[qspec, kvspec, pl.BlockSpec((t, HD), lambda kv, g, i: (0, vblk + kv)), qspec, qspec] + [_HBM] * rn,
        out_specs=(qspec, kvspec, kvspec) + (_HBM,) * rn,
        out_shape=(jax.ShapeDtypeStruct((t, NH * HD), F32), jax.ShapeDtypeStruct((t, NKV * HD), F32),
                   jax.ShapeDtypeStruct((t, NKV * HD), F32)) + (() if rider is None else tuple(rider.out_shape)),
        scratch_shapes=[] if rider is None else rider.scratch,
        compiler_params=_cparams(3, 48),
    )(qr, kr, p, o, do, *(() if rider is None else rider.xs))


def _decay_fwd(p, wd, bd, name):
    t = p.shape[0]

    def body(r_ref, w_ref, b_ref, z_ref, la_ref):
        z = jnp.dot(r_ref[...].astype(BF16), w_ref[...].astype(BF16), preferred_element_type=F32) + b_ref[...]
        z_ref[...] = z
        la_ref[...] = (jnp.minimum(z, 0.0) - jnp.log(1.0 + jnp.exp(-jnp.abs(z)))) / GLA_TAU

    row = pl.BlockSpec((TM, D), lambda i: (i, 0))
    return pl.pallas_call(
        body, name=name, grid=(t // TM,),
        in_specs=[_pcol("r", R_PAD), pl.BlockSpec((R_PAD, D), lambda i: (0, 0)), pl.BlockSpec((1, D), lambda i: (0, 0))],
        out_specs=(row, row),
        out_shape=(jax.ShapeDtypeStruct((t, D), F32), jax.ShapeDtypeStruct((t, D), F32)),
        compiler_params=_cparams(1, 32),
    )(p, wd, bd)


def _chunk_order(s, ncc, nc, rev):
    if not rev:
        return s
    return jnp.where(s < ncc, ncc - 1 - s, nc - 1 - (s - ncc))


def _gla_chunk(qv, kv, lav, rev):
    r = lax.broadcasted_iota(jnp.int32, (CH, CH), 0)
    c = lax.broadcasted_iota(jnp.int32, (CH, CH), 1)
    keep = (c >= r) if rev else (c <= r)
    tri = keep.astype(F32)
    bc = jnp.dot(tri, lav, preferred_element_type=F32, precision=HIGHEST)
    bl = jnp.sum(lav, axis=0, keepdims=True)
    qt = qv * GLA_SCALE * jnp.exp(bc)
    kt = kv * jnp.exp(-bc)
    kh = kv * jnp.exp(bl - bc)
    return qt, kt, jnp.exp(bl), kh, keep, bc


_NT = (((1,), (1,)), ((), ()))
_TN = (((0,), (0,)), ((), ()))


def _gla_specs(ncc, nc, rev, backward):
    def idx(s):
        return _chunk_order((nc - 1 - s) if backward else s, ncc, nc, rev)

    wk, wv = GH * GDK, GH * GDV
    qb, kb, vb = OFF["gq"] // wk, OFF["gk"] // wk, OFF["gv"] // wv
    lab = 1 if rev else 0
    q = pl.BlockSpec((CH, wk), lambda s: (idx(s), qb))
    k = pl.BlockSpec((CH, wk), lambda s: (idx(s), kb))
    v = pl.BlockSpec((CH, wv), lambda s: (idx(s), vb))
    la = pl.BlockSpec((CH, wk), lambda s: (idx(s), lab))
    o = pl.BlockSpec((CH, wv), lambda s: (idx(s), 0))
    dk = pl.BlockSpec((CH, wk), lambda s: (idx(s), 0))
    st = pl.BlockSpec((None, GH, GDV, GDK), lambda s: (idx(s), 0, 0, 0))
    return q, k, v, la, o, dk, st


def _gla_fwd(p, la, ncc, rev, name):
    t = p.shape[0]
    nc = t // CH
    q_s, k_s, v_s, la_s, o_s, _, st_s = _gla_specs(ncc, nc, rev, False)

    def body(q_ref, k_ref, v_ref, la_ref, o_ref, st_ref, s_scr):
        @pl.when(pl.program_id(0) == 0)
        def _():
            s_scr[...] = jnp.zeros_like(s_scr)

        for h in range(GH):
            sk, sv = slice(h * GDK, (h + 1) * GDK), slice(h * GDV, (h + 1) * GDV)
            qt, kt, gl, kh, keep, _ = _gla_chunk(q_ref[:, sk], k_ref[:, sk], la_ref[:, sk], rev)
            st = s_scr[h]
            st_ref[h] = st
            vb = v_ref[:, sv].astype(BF16)
            qb = qt.astype(BF16)
            a = jnp.where(keep, lax.dot_general(qb, kt.astype(BF16), _NT, preferred_element_type=F32), 0.0)
            o_ref[:, sv] = (lax.dot_general(qb, st.astype(BF16), _NT, preferred_element_type=F32)
                            + jnp.dot(a.astype(BF16), vb, preferred_element_type=F32))
            s_scr[h] = st * gl + lax.dot_general(vb, kh.astype(BF16), _TN, preferred_element_type=F32)

    return pl.pallas_call(
        body, name=name, grid=(nc,),
        in_specs=[q_s, k_s, v_s, la_s], out_specs=(o_s, st_s),
        out_shape=(jax.ShapeDtypeStruct((t, GH * GDV), F32), jax.ShapeDtypeStruct((nc, GH, GDV, GDK), F32)),
        scratch_shapes=[pltpu.VMEM((GH, GDV, GDK), F32)], compiler_params=_cparams(1, 32),
    )(p, p, p, la)


def _gla_bwd(p, la, do, st, ncc, rev, name):
    t = p.shape[0]
    nc = t // CH
    q_s, k_s, v_s, la_s, o_s, dk_s, st_s = _gla_specs(ncc, nc, rev, True)

    def body(q_ref, k_ref, v_ref, la_ref, do_ref, st_ref, dq_ref, dk_ref, dv_ref, dla_ref, ds_scr):
        @pl.when(pl.program_id(0) == 0)
        def _():
            ds_scr[...] = jnp.zeros_like(ds_scr)

        row = lax.broadcasted_iota(jnp.int32, (CH, GDK), 0)
        r = lax.broadcasted_iota(jnp.int32, (CH, CH), 0)
        c = lax.broadcasted_iota(jnp.int32, (CH, CH), 1)
        trit = ((c <= r) if rev else (c >= r)).astype(F32)
        last = 0 if rev else CH - 1
        for h in range(GH):
            sk, sv = slice(h * GDK, (h + 1) * GDK), slice(h * GDV, (h + 1) * GDV)
            qt, kt, gl, kh, keep, bc = _gla_chunk(q_ref[:, sk], k_ref[:, sk], la_ref[:, sk], rev)
            stv = st_ref[h]
            dsn = ds_scr[h]
            dsb = dsn.astype(BF16)
            vb, dob = v_ref[:, sv].astype(BF16), do_ref[:, sv].astype(BF16)
            qb, kb = qt.astype(BF16), kt.astype(BF16)
            a = jnp.where(keep, lax.dot_general(qb, kb, _NT, preferred_element_type=F32), 0.0).astype(BF16)
            da = jnp.where(keep, lax.dot_general(dob, vb, _NT, preferred_element_type=F32), 0.0).astype(BF16)
            dqt = (jnp.dot(dob, stv.astype(BF16), preferred_element_type=F32)
                   + jnp.dot(da, kb, preferred_element_type=F32))
            dkh = jnp.dot(vb, dsb, preferred_element_type=F32)
            dkt = lax.dot_general(da, qb, _TN, preferred_element_type=F32) + dkh * gl
            dv_ref[:, sv] = (lax.dot_general(a, dob, _TN, preferred_element_type=F32)
                             + lax.dot_general(kh.astype(BF16), dsb, _NT, preferred_element_type=F32))
            ds_scr[h] = lax.dot_general(dob, qb, _TN, preferred_element_type=F32) + dsn * gl
            dgl = jnp.sum(stv * dsn, axis=0, keepdims=True) + jnp.sum(dkh * kt, axis=0, keepdims=True)
            dbc = dqt * qt - dkt * kt + jnp.where(row == last, dgl * gl, 0.0)
            dla_ref[:, sk] = jnp.dot(trit, dbc, preferred_element_type=F32, precision=HIGHEST)
            dq_ref[:, sk] = dqt * (GLA_SCALE * jnp.exp(bc))
            dk_ref[:, sk] = dkt * jnp.exp(-bc)

    return pl.pallas_call(
        body, name=name, grid=(nc,),
        in_specs=[q_s, k_s, v_s, la_s, o_s, st_s], out_specs=(dk_s, dk_s, o_s, dk_s),
        out_shape=(jax.ShapeDtypeStruct((t, GH * GDK), F32), jax.ShapeDtypeStruct((t, GH * GDK), F32),
                   jax.ShapeDtypeStruct((t, GH * GDV), F32), jax.ShapeDtypeStruct((t, GH * GDK), F32)),
        scratch_shapes=[pltpu.VMEM((GH, GDV, GDK), F32)], compiler_params=_cparams(1, 32),
    )(p, p, p, la, do, st)


def _gla_merge_bwd(gf, gb, z, p, wd, name):
    t = p.shape[0]
    w2 = GH * GDK

    def body(dqf, dkf, dvf, dlf, dqb, dkb, dvb, dlb, z_ref, r_ref, w_ref, dq_ref, dk_ref, dv_ref, dr_ref, db_ref, dw_ref):
        i = pl.program_id(0)
        dq_ref[...] = (dqf[...] + dqb[...]).astype(BF16)
        dk_ref[...] = (dkf[...] + dkb[...]).astype(BF16)
        dv_ref[...] = (dvf[...] + dvb[...]).astype(BF16)
        zv = z_ref[...]
        dz = jnp.concatenate([dlf[...], dlb[...]], axis=1) * (_sigmoid(-zv) / GLA_TAU)
        dzb = dz.astype(BF16)
        dr_ref[...] = lax.dot_general(dzb, w_ref[...].astype(BF16), _NT, preferred_element_type=F32).astype(BF16)

        @pl.when(i == 0)
        def _():
            db_ref[...] = jnp.zeros_like(db_ref)
            dw_ref[...] = jnp.zeros_like(dw_ref)

        db_ref[...] += jnp.sum(dz, axis=0, keepdims=True)
        dw_ref[...] += lax.dot_general(r_ref[...].astype(BF16), dzb, _TN, preferred_element_type=F32)

    half = pl.BlockSpec((TM, w2), lambda i: (i, 0))
    row = pl.BlockSpec((TM, D), lambda i: (i, 0))
    wspec = pl.BlockSpec((R_PAD, D), lambda i: (0, 0))
    return pl.pallas_call(
        body, name=name, grid=(t // TM,),
        in_specs=[half, half, row, half, half, half, row, half, row, _pcol("r", R_PAD), wspec],
        out_specs=(half, half, row, pl.BlockSpec((TM, R_PAD), lambda i: (i, 0)),
                   pl.BlockSpec((8, D), lambda i: (0, 0)), wspec),
        out_shape=(jax.ShapeDtypeStruct((t, w2), BF16), jax.ShapeDtypeStruct((t, w2), BF16),
                   jax.ShapeDtypeStruct((t, D), BF16), jax.ShapeDtypeStruct((t, R_PAD), BF16),
                   jax.ShapeDtypeStruct((8, D), F32), jax.ShapeDtypeStruct((R_PAD, D), F32)),
        compiler_params=_cparams(1, 40),
    )(*gf, *gb, z, p, wd)


def _branch_fwd(att, of, ob, p, gla_g, name):
    t = p.shape[0]

    def body(att_ref, of_ref, ob_ref, za_ref, zg_ref, g_ref, yb_ref, yc_ref):
        za = za_ref[...]
        yb_ref[...] = (att_ref[...] * (za * _sigmoid(za))).astype(BF16)
        for h in range(GH):
            sl = slice(h * GDV, (h + 1) * GDV)
            o = of_ref[:, sl] + ob_ref[:, sl]
            n = o * lax.rsqrt(jnp.mean(o * o, axis=-1, keepdims=True) + EPS) * g_ref[...]
            zh = zg_ref[:, sl]
            yc_ref[:, sl] = (n * (zh * _sigmoid(zh))).astype(BF16)

    row = pl.BlockSpec((TM, D), lambda i: (i, 0))
    return pl.pallas_call(
        body, name=name, grid=(t // TM,),
        in_specs=[row, row, row, _pcol("z_attn", D), _pcol("zg", D), pl.BlockSpec((1, GDV), lambda i: (0, 0))],
        out_specs=(row, row),
        out_shape=(jax.ShapeDtypeStruct((t, D), BF16), jax.ShapeDtypeStruct((t, D), BF16)),
        compiler_params=_cparams(1, 40),
    )(att, of, ob, p, p, gla_g)


def _branch_bwd(dyb, dyc, att, of, ob, p, gla_g, name):
    t = p.shape[0]

    def body(dyb_ref, dyc_ref, att_ref, of_ref, ob_ref, za_ref, zg_ref, g_ref, datt_ref, dza_ref, do_ref, dzg_ref, dg_ref):
        i = pl.program_id(0)

        @pl.when(i == 0)
        def _():
            dg_ref[...] = jnp.zeros_like(dg_ref)

        za, dyb = za_ref[...], dyb_ref[...]
        sa = _sigmoid(za)
        datt_ref[...] = dyb * (za * sa)
        dza_ref[...] = (dyb * att_ref[...] * (sa * (1.0 + za * (1.0 - sa)))).astype(BF16)
        g = g_ref[...]
        for h in range(GH):
            sl = slice(h * GDV, (h + 1) * GDV)
            o = of_ref[:, sl] + ob_ref[:, sl]
            r = lax.rsqrt(jnp.mean(o * o, axis=-1, keepdims=True) + EPS)
            oh = o * r
            zh, dyc = zg_ref[:, sl], dyc_ref[:, sl]
            sg = _sigmoid(zh)
            dn = dyc * (zh * sg)
            dzg_ref[:, sl] = (dyc * (oh * g) * (sg * (1.0 + zh * (1.0 - sg)))).astype(BF16)
            doh = dn * g
            do_ref[:, sl] = r * (doh - oh * jnp.mean(doh * oh, axis=-1, keepdims=True))
            dg_ref[...] += jnp.sum(dn * oh, axis=0, keepdims=True)

    row = pl.BlockSpec((TM, D), lambda i: (i, 0))
    return pl.pallas_call(
        body, name=name, grid=(t // TM,),
        in_specs=[row, row, row, row, row, _pcol("z_attn", D), _pcol("zg", D), pl.BlockSpec((1, GDV), lambda i: (0, 0))],
        out_specs=(row, row, row, row, pl.BlockSpec((8, GDV), lambda i: (0, 0))),
        out_shape=(jax.ShapeDtypeStruct((t, D), F32), jax.ShapeDtypeStruct((t, D), BF16),
                   jax.ShapeDtypeStruct((t, D), F32), jax.ShapeDtypeStruct((t, D), BF16),
                   jax.ShapeDtypeStruct((8, GDV), F32)),
        compiler_params=_cparams(1, 48),
    )(dyb, dyc, att, of, ob, p, p, gla_g)


def _merge_fwd(bra, brb, brc, p, b_gate, name):
    t = p.shape[0]
    mgb = OFF["mg"] // D

    def body(a_ref, b_ref, c_ref, ga_ref, gb_ref, gc_ref, bg_ref, m_ref):
        m_ref[...] = (_sigmoid(ga_ref[...] + bg_ref[:, 0:D]) * a_ref[...]
                      + _sigmoid(gb_ref[...] + bg_ref[:, D:2 * D]) * b_ref[...]
                      + _sigmoid(gc_ref[...] + bg_ref[:, 2 * D:3 * D]) * c_ref[...]).astype(BF16)

    row = pl.BlockSpec((TM, D), lambda i: (i, 0))
    gates = [pl.BlockSpec((TM, D), functools.partial(lambda i, b: (i, b), b=mgb + j)) for j in range(3)]
    return pl.pallas_call(
        body, name=name, grid=(t // TM,),
        in_specs=[row, row, row, *gates, pl.BlockSpec((1, 3 * D), lambda i: (0, 0))],
        out_specs=row, out_shape=jax.ShapeDtypeStruct((t, D), BF16), compiler_params=_cparams(1, 40),
    )(bra, brb, brc, p, p, p, b_gate)


def _merge_bwd(dm, bra, brb, brc, p, b_gate, name):
    t = p.shape[0]
    mgb = OFF["mg"] // D

    def body(dm_ref, a_ref, b_ref, c_ref, ga_ref, gb_ref, gc_ref, bg_ref, da_ref, db_ref, dc_ref, dmg_ref, dbg_ref):
        i = pl.program_id(0)

        @pl.when(i == 0)
        def _():
            dbg_ref[...] = jnp.zeros_like(dbg_ref)

        dm = dm_ref[...]
        for j, (br_ref, g_ref, d_ref) in enumerate(((a_ref, ga_ref, da_ref), (b_ref, gb_ref, db_ref), (c_ref, gc_ref, dc_ref))):
            sl = slice(j * D, (j + 1) * D)
            g = _sigmoid(g_ref[...] + bg_ref[:, sl])
            d_ref[...] = (dm * g).astype(BF16)
            dmg = dm * br_ref[...] * (g * (1.0 - g))
            dmg_ref[:, sl] = dmg.astype(BF16)
            dbg_ref[:, sl] += jnp.sum(dmg, axis=0, keepdims=True)

    row = pl.BlockSpec((TM, D), lambda i: (i, 0))
    gates = [pl.BlockSpec((TM, D), functools.partial(lambda i, b: (i, b), b=mgb + j)) for j in range(3)]
    return pl.pallas_call(
        body, name=name, grid=(t // TM,),
        in_specs=[row, row, row, row, *gates, pl.BlockSpec((1, 3 * D), lambda i: (0, 0))],
        out_specs=(row, row, row, pl.BlockSpec((TM, 3 * D), lambda i: (i, 0)), pl.BlockSpec((8, 3 * D), lambda i: (0, 0))),
        out_shape=(jax.ShapeDtypeStruct((t, D), BF16),) * 3 + (jax.ShapeDtypeStruct((t, 3 * D), BF16),
                                                                jax.ShapeDtypeStruct((8, 3 * D), F32)),
        compiler_params=_cparams(1, 48),
    )(dm, bra, brb, brc, p, p, p, b_gate)


def _adamw(gsrc, w, m, v, name):
    ns, nl, r, c = gsrc.shape
    lanes = -(-c // LANE) * LANE
    tr = r
    for cand in range(16, r, 16):
        if r % cand == 0 and ns * cand * lanes * gsrc.dtype.itemsize <= ADAM_SRC_BYTES and cand * lanes * 4 <= ADAM_ROW_BYTES:
            tr = cand
    if r * lanes * 4 <= ADAM_ROW_BYTES and ns * r * lanes * gsrc.dtype.itemsize <= ADAM_SRC_BYTES:
        tr = r

    def body(g_ref, w_ref, m_ref, v_ref, go_ref, d_ref, mo_ref, vo_ref):
        g = g_ref[0].astype(F32)
        for s in range(1, ns):
            g = g + g_ref[s].astype(F32)
        mn = ADAM_B1 * m_ref[...] + (1.0 - ADAM_B1) * g
        vn = ADAM_B2 * v_ref[...] + (1.0 - ADAM_B2) * jnp.square(g)
        m_hat = mn / (1.0 - ADAM_B1 ** ADAM_STEP)
        v_hat = vn / (1.0 - ADAM_B2 ** ADAM_STEP)
        go_ref[...] = g
        d_ref[...] = -ADAM_LR * (m_hat / (jnp.sqrt(v_hat) + ADAM_EPS) + ADAM_WD * w_ref[...])
        mo_ref[...] = mn
        vo_ref[...] = vn

    row = pl.BlockSpec((None, tr, c), lambda l, i: (l, i, 0))
    return pl.pallas_call(
        body, name=name, grid=(nl, r // tr),
        in_specs=[pl.BlockSpec((ns, None, tr, c), lambda l, i: (0, l, i, 0)), row, row, row],
        out_specs=(row,) * 4, out_shape=(jax.ShapeDtypeStruct((nl, r, c), F32),) * 4,
        compiler_params=_cparams(2, 48),
    )(gsrc, w, m, v)


def _slot_sum(gsrc, name):
    ns, r, _ = gsrc.shape

    def body(g_ref, o_ref):
        g = g_ref[0]
        for s in range(1, ns):
            g = g + g_ref[s]
        o_ref[...] = g

    return pl.pallas_call(body, name=name, out_shape=jax.ShapeDtypeStruct((r, LANE), F32))(gsrc)


def _pack(parts, dtype, row_align):
    chunk = row_align * LANE
    out, spans, o = [], [], 0
    for a in parts:
        f = a.reshape(-1).astype(dtype)
        n = f.shape[0]
        pad = (-n) % chunk
        if pad:
            f = jnp.concatenate([f, jnp.zeros((pad,), dtype)])
        out.append(f.reshape(-1, LANE))
        spans.append((o, n))
        o += (n + pad) // LANE
    return jnp.concatenate(out, axis=0), spans


def _unpack(packed, spans, shapes):
    res = []
    for (o, n), shp in zip(spans, shapes):
        rows = -(-n // LANE)
        res.append(packed[o:o + rows].reshape(-1)[:n].reshape(shp))
    return res


def _rope_tables(ctx, seq):
    n_rows = seq // GRID_W
    pairs = HD // 4
    row = jnp.repeat(jnp.arange(n_rows, dtype=F32), GRID_W)
    col = jnp.tile(jnp.arange(GRID_W, dtype=F32), n_rows)
    freqs = ROPE_THETA ** (-jnp.arange(pairs, dtype=F32) * 2.0 / (HD // 2))
    ar, ac = row[:, None] * freqs, col[:, None] * freqs
    cos_l = jnp.concatenate([jnp.cos(ar), jnp.cos(ar), jnp.cos(ac), jnp.cos(ac)], axis=1)
    sin_l = jnp.concatenate([-jnp.sin(ar), jnp.sin(ar), -jnp.sin(ac), jnp.sin(ac)], axis=1)
    cos_t = jnp.concatenate([jnp.ones((ctx, HD), F32), cos_l], axis=0)
    sin_t = jnp.concatenate([jnp.zeros((ctx, HD), F32), sin_l], axis=0)
    return cos_t, sin_t


def _to_proj_layout(w):
    parts = [w[:, s:s + wd] for _, s, wd in _SEGS]
    used = sum(wd for _, _, wd in _SEGS)
    parts.append(jnp.zeros((w.shape[0], NP - used), w.dtype))
    return jnp.concatenate(parts, axis=1)


def _from_proj_layout(g):
    order = sorted(_SEGS, key=lambda sg: sg[1])
    return jnp.concatenate([g[:, OFF[n]:OFF[n] + wd] for n, _, wd in order], axis=1)


def _row0(a):
    return a[..., 0, :]


def kernel(x, c, ctx, c_ctx, w_ada, b_ada, g_pre, g_post, w_in, conv_w, q_norm_g, k_norm_g, w_decay_fwd, b_decay_fwd, w_decay_bwd, b_decay_bwd, gla_norm_g, w_br_conv, w_br_attn, w_br_gla, b_gate, w_out, loss_target, m_c_ctx, m_w_ada, m_b_ada, m_g_pre, m_g_post, m_w_in, m_conv_w, m_q_norm_g, m_k_norm_g, m_w_decay_fwd, m_b_decay_fwd, m_w_decay_bwd, m_b_decay_bwd, m_gla_norm_g, m_w_br_conv, m_w_br_attn, m_w_br_gla, m_b_gate, m_w_out, v_c_ctx, v_w_ada, v_b_ada, v_g_pre, v_g_post, v_w_in, v_conv_w, v_q_norm_g, v_k_norm_g, v_w_decay_fwd, v_b_decay_fwd, v_w_decay_bwd, v_b_decay_bwd, v_gla_norm_g, v_w_br_conv, v_w_br_attn, v_w_br_gla, v_b_gate, v_w_out):
    seq, n_ctx = x.shape[1], ctx.shape[1]
    assert n_ctx % TM == 0 and seq % TM == 0 and seq % GRID_W == 0
    t = n_ctx + seq
    nct, ncc = n_ctx // TM, n_ctx // CH
    dev = 4 * lax.axis_index("x") + 2 * lax.axis_index("y") + lax.axis_index("c")
    ada_w = w_ada.shape[2]
    in_w = w_in.shape[2]
    br_r = w_br_conv.shape[1]

    wb = [w.astype(BF16) for w in (w_ada, w_in, w_br_conv, w_br_attn, w_br_gla, w_out)]
    wall = _all_gather([wb[0], wb[1][0]], "gather_first")
    later = _GatherRider([wb[1][1], wb[2], wb[3], wb[4], wb[5]])
    small_parts = [conv_w, w_decay_fwd, w_decay_bwd]
    spack, sspans = _pack(small_parts, F32, 8)
    sall, = _all_gather([spack], "gather_small")

    def gathered(all_, spans, k, shp):
        o, n = spans[k]
        rows = n // LANE
        return all_[:, o:o + rows].reshape((NDEV,) + shp)

    def full_in(g):
        return _to_proj_layout(g.transpose(1, 0, 2).reshape(D, IN_WIDTH))

    w_ada_f = wall[0].transpose(1, 2, 0, 3).reshape(DEPTH, D, 3 * D)
    wp = [full_in(wall[1]), None]
    conv_f = gathered(sall, sspans, 0, (DEPTH, 3, D // NDEV)).transpose(1, 2, 0, 3).reshape(DEPTH, 3, D)
    wdf_f = gathered(sall, sspans, 1, (DEPTH, GLA_RANK, GH * GDK // NDEV)).transpose(1, 2, 0, 3).reshape(DEPTH, GLA_RANK, GH * GDK)
    wdb_f = gathered(sall, sspans, 2, (DEPTH, GLA_RANK, GH * GDK // NDEV)).transpose(1, 2, 0, 3).reshape(DEPTH, GLA_RANK, GH * GDK)

    cos_t, sin_t = _rope_tables(n_ctx, seq)
    cc = jnp.concatenate([c_ctx[None, :], c.reshape(1, D), jnp.zeros((6, D), F32)], axis=0)
    silu_cc, dsilu_cc = _ada_in(cc)

    conv8, wd_pad, bd = [], [], []
    for l in range(DEPTH):
        conv8.append(jnp.concatenate([conv_f[l], jnp.zeros((5, D), F32)], axis=0))
        zr = jnp.zeros((GLA_RANK, GH * GDK), F32)
        wd_pad.append(jnp.concatenate([jnp.concatenate([wdf_f[l], zr], axis=1), jnp.concatenate([zr, wdb_f[l]], axis=1),
                                       jnp.zeros((R_PAD - 2 * GLA_RANK, D), F32)], axis=0))
        bd.append(jnp.concatenate([b_decay_fwd[l], b_decay_bwd[l]])[None, :])

    xs = jnp.concatenate([ctx[0], x[0]], axis=0)
    saved = []
    for l in range(DEPTH):
        n = f"l{l}_"
        mod = _mm(silu_cc, w_ada_f[l], n + "mod", bias=b_ada[l][None, :])
        mod3 = mod[0:2].reshape(2, 3, D)
        h = _prenorm_fwd(xs, g_pre[l][None, :], mod3, nct, n + "prenorm")
        p = _mm(h, wp[l], n + "proj")
        cv, ya = _conv_fwd(p, conv8[l], nct, n + "conv")
        qr, kr = _qk_prep_fwd(p, q_norm_g[l][None, :], k_norm_g[l][None, :], cos_t, sin_t, n + "qk_prep")
        att, *got = _attn_fwd(qr, kr, p, nct, n + "attn", rider=later if l == 0 else None)
        if l == 0:
            wp[1] = full_in(got[0])
            w_brs_f = [g.transpose(1, 0, 2, 3).reshape(DEPTH, D, D) for g in got[1:]]
        z, la = _decay_fwd(p, wd_pad[l], bd[l], n + "decay")
        of, stf = _gla_fwd(p, la, ncc, False, n + "gla_f")
        ob, stb = _gla_fwd(p, la, ncc, True, n + "gla_b")
        yb, yc = _branch_fwd(att, of, ob, p, gla_norm_g[l][None, :], n + "branch")
        bra = _mm(ya, w_brs_f[0][l], n + "br_conv")
        brb = _mm(yb, w_brs_f[1][l], n + "br_attn")
        brc = _mm(yc, w_brs_f[2][l], n + "br_gla")
        mm_ = _merge_fwd(bra, brb, brc, p, b_gate[l][None, :], n + "merge")
        out = _mm(mm_, w_brs_f[3][l], n + "out")
        x_new = _post_fwd(xs, out, g_post[l][None, :], mod3, nct, n + "post")
        saved.append(dict(x=xs, mod3=mod3, h=h, p=p, cv=cv, ya=ya, qr=qr, kr=kr, att=att, z=z, la=la, of=of, ob=ob,
                          stf=stf, stb=stb, yb=yb, yc=yc, bra=bra, brb=brb, brc=brc, m=mm_, out=out))
        xs = x_new

    dx, sq = _loss_grad(xs, loss_target[0], nct, "loss")
    loss = lax.psum(0.5 * sq[0, 0] / D, ("x", "y", "c"))

    gw = {k: [None] * DEPTH for k in ("w_in", "br_conv", "br_attn", "br_gla", "out", "b_gate", "g_pre", "g_post",
                                      "conv_w", "qg", "kg", "wd", "bdec", "gla_g", "dmod")}
    dctx = []

    def grad_slots(l):
        g_in = _from_proj_layout(gw["w_in"][l]).reshape(D, NDEV, in_w).transpose(1, 0, 2)
        return [g_in] + [gw[k][l].reshape(NDEV, br_r, D) for k in ("br_conv", "br_attn", "br_gla", "out")]

    for l in reversed(range(DEPTH)):
        n = f"l{l}_b_"
        s = saved[l]
        p = s["p"]
        d_out, dgt, gw["g_post"][l] = _post_bwd(dx, s["out"], g_post[l][None, :], s["mod3"], nct, n + "post")
        dm = _mm(d_out, w_brs_f[3][l], n + "dm", tb=True)
        gw["out"][l] = _mm(s["m"], d_out, n + "dw_out", ta=True, out_dtype=BF16)
        dbra, dbrb, dbrc, dmg, gw["b_gate"][l] = _merge_bwd(dm, s["bra"], s["brb"], s["brc"], p, b_gate[l][None, :], n + "merge")
        dya = _mm(dbra, w_brs_f[0][l], n + "dya", tb=True)
        dyb = _mm(dbrb, w_brs_f[1][l], n + "dyb", tb=True)
        dyc = _mm(dbrc, w_brs_f[2][l], n + "dyc", tb=True)
        gw["br_conv"][l] = _mm(s["ya"], dbra, n + "dw_conv", ta=True, out_dtype=BF16)
        gw["br_attn"][l] = _mm(s["yb"], dbrb, n + "dw_attn", ta=True, out_dtype=BF16)
        gw["br_gla"][l] = _mm(s["yc"], dbrc, n + "dw_gla", ta=True, out_dtype=BF16)
        dcv, dab, daz = _conv_bwd_a(dya, p, s["cv"], n + "conv_a")
        dac, dax, gw["conv_w"][l] = _conv_bwd_b(dcv, p, conv8[l], nct, n + "conv_b")
        datt, dza, dgo, dzg, gw["gla_g"][l] = _branch_bwd(dyb, dyc, s["att"], s["of"], s["ob"], p, gla_norm_g[l][None, :], n + "branch")
        ex1 = _ExchangeRider(grad_slots(DEPTH - 1)) if l == 0 else None
        dqr, dkr, dv, *got = _attn_bwd(s["qr"], s["kr"], p, s["att"], datt, nct, n + "attn", rider=ex1)
        if l == 0:
            recv1 = got
        dq, dk, gw["qg"][l], gw["kg"][l] = _qk_prep_bwd(dqr, dkr, p, q_norm_g[l][None, :], k_norm_g[l][None, :], cos_t, sin_t, n + "qk_prep")
        gf = _gla_bwd(p, s["la"], dgo, s["stf"], ncc, False, n + "gla_f")
        gb = _gla_bwd(p, s["la"], dgo, s["stb"], ncc, True, n + "gla_b")
        dgq, dgk, dgv, dr, gw["bdec"][l], gw["wd"][l] = _gla_merge_bwd(gf, gb, s["z"], p, wd_pad[l], n + "gla_merge")
        pieces = dict(a_b=dab, a_c=dac, a_x=dax, a_z=daz, q=dq, z_attn=dza, gv=dgv, zg=dzg, mg=dmg, gq=dgq, gk=dgk,
                      k=dk, v=dv.astype(BF16), r=dr)
        used = sum(wd_ for _, _, wd_ in _SEGS) - 32 + R_PAD
        dp = jnp.concatenate([pieces[nm] for nm, _, _ in _SEGS] + [jnp.zeros((t, NP - used), BF16)], axis=1)
        dh = _mm(dp, wp[l], n + "dh", tb=True)
        gw["w_in"][l] = _mm(s["h"], dp, n + "dw_in", ta=True, out_dtype=BF16)
        dx, dsh, dsc, gw["g_pre"][l] = _prenorm_bwd(dh, s["x"], dx, g_pre[l][None, :], s["mod3"], nct, n + "prenorm")
        dmod = jnp.stack([_row0(dsh), _row0(dsc), _row0(dgt)], axis=1).reshape(2, 3 * D)
        gw["dmod"][l] = dmod
        dmod8 = jnp.concatenate([dmod, jnp.zeros((6, 3 * D), F32)], axis=0)
        dctx.append(_mm(dmod8, w_ada_f[l], n + "dsilu", tb=True))
    grad_x = dx[n_ctx:][None]
    g_cctx = _cctx_grad(dctx[0], dctx[1], dsilu_cc)[0]

    def st2(name):
        return jnp.stack(gw[name])

    g_b_ada = jnp.stack([gw["dmod"][l][0] + gw["dmod"][l][1] for l in range(DEPTH)])
    g_bdf = jnp.stack([gw["bdec"][l][0, :GH * GDK] for l in range(DEPTH)])
    g_bdb = jnp.stack([gw["bdec"][l][0, GH * GDK:] for l in range(DEPTH)])
    g_wdf = jnp.stack([gw["wd"][l][0:GLA_RANK, :GH * GDK] for l in range(DEPTH)])
    g_wdb = jnp.stack([gw["wd"][l][GLA_RANK:2 * GLA_RANK, GH * GDK:] for l in range(DEPTH)])
    rep_names = ["c_ctx", "b_ada", "g_pre", "g_post", "q_norm_g", "k_norm_g", "b_decay_fwd", "b_decay_bwd", "gla_norm_g", "b_gate"]
    rep_grads = [g_cctx, g_b_ada, st2("g_pre")[:, 0], st2("g_post")[:, 0], st2("qg")[:, 0], st2("kg")[:, 0], g_bdf, g_bdb,
                 st2("gla_g")[:, 0], st2("b_gate")[:, 0]]
    rep_w = [c_ctx, b_ada, g_pre, g_post, q_norm_g, k_norm_g, b_decay_fwd, b_decay_bwd, gla_norm_g, b_gate]
    rep_m = [m_c_ctx, m_b_ada, m_g_pre, m_g_post, m_q_norm_g, m_k_norm_g, m_b_decay_fwd, m_b_decay_bwd, m_gla_norm_g, m_b_gate]
    rep_v = [v_c_ctx, v_b_ada, v_g_pre, v_g_post, v_q_norm_g, v_k_norm_g, v_b_decay_fwd, v_b_decay_bwd, v_gla_norm_g, v_b_gate]
    shard_grads = [st2("conv_w")[:, 0:3], g_wdf, g_wdb]
    extra = [silu_cc[0:2], jnp.stack(gw["dmod"])]
    gpack, gspans = _pack(rep_grads + shard_grads + extra, F32, 8)
    gall, = _all_gather([gpack], "gather_small_grads")
    n_rep = len(rep_grads)
    rep_rows = gspans[n_rep][0]
    shard_rows = gspans[n_rep + len(shard_grads)][0]

    wpk, rspans = _pack(rep_w, F32, 8)
    mpk, _ = _pack(rep_m, F32, 8)
    vpk, _ = _pack(rep_v, F32, 8)
    rep_out = _adamw(gall[:, None, :rep_rows], wpk[None], mpk[None], vpk[None], "adam_rep")
    rep_shapes = [a.shape for a in rep_w]
    rep_g, rep_d, rep_nm, rep_nv = [_unpack(o[0], rspans, rep_shapes) for o in rep_out]

    ssum = _slot_sum(gall[:, rep_rows:shard_rows], "sum_small_sharded")
    sh_spans = [(o - rep_rows, n_) for o, n_ in gspans[n_rep:n_rep + len(shard_grads)]]
    g_conv_full, g_wdf_full, g_wdb_full = _unpack(ssum, sh_spans, [(DEPTH, 3, D), (DEPTH, GLA_RANK, GH * GDK), (DEPTH, GLA_RANK, GH * GDK)])
    cw, dw = D // NDEV, GH * GDK // NDEV
    sh_g = [lax.dynamic_slice_in_dim(g_conv_full, dev * cw, cw, axis=2),
            lax.dynamic_slice_in_dim(g_wdf_full, dev * dw, dw, axis=2),
            lax.dynamic_slice_in_dim(g_wdb_full, dev * dw, dw, axis=2)]
    sgp, shs = _pack(sh_g, F32, 8)
    swp, _ = _pack([conv_w, w_decay_fwd, w_decay_bwd], F32, 8)
    smp, _ = _pack([m_conv_w, m_w_decay_fwd, m_w_decay_bwd], F32, 8)
    svp, _ = _pack([v_conv_w, v_w_decay_fwd, v_w_decay_bwd], F32, 8)
    sh_out = _adamw(sgp[None, None], swp[None], smp[None], svp[None], "adam_small_sharded")
    sh_shapes = [conv_w.shape, w_decay_fwd.shape, w_decay_bwd.shape]
    sh_gr, sh_d, sh_nm, sh_nv = [_unpack(o[0], shs, sh_shapes) for o in sh_out]

    eo, en = gspans[n_rep + len(shard_grads)]
    a_all = gall[:, eo:eo + en // LANE].reshape(NDEV * 2, D)
    eo2, en2 = gspans[n_rep + len(shard_grads) + 1]
    d_all = gall[:, eo2:eo2 + en2 // LANE].reshape(NDEV, DEPTH, 2, 3 * D).transpose(1, 0, 2, 3).reshape(DEPTH, NDEV * 2, 3 * D)
    g_ada = jnp.stack([_mm(a_all, lax.dynamic_slice_in_dim(d_all[l], dev * ada_w, ada_w, axis=1), f"dw_ada{l}",
                           ta=True, precise=True, tk=NDEV * 2) for l in range(DEPTH)])
    ada_g, ada_d, ada_nm, ada_nv = _adamw(g_ada[None], w_ada, m_w_ada, v_w_ada, "adam_ada")

    recv0 = _all_to_all(grad_slots(0), "exchange_grads")
    grecv = [jnp.stack([a0, a1], axis=1) for a0, a1 in zip(recv0, recv1)]
    big_w = [w_in, w_br_conv, w_br_attn, w_br_gla, w_out]
    big_m = [m_w_in, m_w_br_conv, m_w_br_attn, m_w_br_gla, m_w_out]
    big_v = [v_w_in, v_w_br_conv, v_w_br_attn, v_w_br_gla, v_w_out]
    big_out = [_adamw(grecv[j], big_w[j], big_m[j], big_v[j], f"adam_big{j}") for j in range(len(big_w))]
    big_g, big_d, big_nm, big_nv = [[o[k] for o in big_out] for k in range(4)]

    def ordered(rep, ada, big, sh):
        c_ctx_, b_ada_, g_pre_, g_post_, qg_, kg_, bdf_, bdb_, glag_, bgate_ = rep
        w_in_, brc_, bra_, brg_, wout_ = big
        conv_, wdf_, wdb_ = sh
        return [c_ctx_, ada, b_ada_, g_pre_, g_post_, w_in_, conv_, qg_, kg_, wdf_, bdf_, wdb_, bdb_, glag_,
                brc_, bra_, brg_, bgate_, wout_]

    return (loss, grad_x,
            *ordered(rep_g, ada_g, big_g, sh_gr), *ordered(rep_d, ada_d, big_d, sh_d),
            *ordered(rep_nm, ada_nm, big_nm, sh_nm), *ordered(rep_nv, ada_nv, big_nv, sh_nv))
```

```python
import functools

import numpy as np
import jax
import jax.numpy as jnp
from jax import lax
from jax.experimental import pallas as pl
from jax.experimental.pallas import tpu as pltpu

F32, BF16 = jnp.float32, jnp.bfloat16
HIGHEST = lax.Precision.HIGHEST

D = 1024
DEPTH = 2
GRID_W = 64
NH, NKV, HD = 8, 2, 128
GROUP = NH // NKV
ROPE_THETA = 10000.0
ATTN_SCALE = HD ** -0.5
GH, GDK, GDV = 4, 128, 256
GLA_RANK = 16
GLA_TAU = 16.0
CH = 64
GLA_SCALE = GDK ** -0.5
EPS = 1e-6
NDEV = 8
LANE = 128
TM = 256

ADAM_LR, ADAM_B1, ADAM_B2, ADAM_EPS, ADAM_WD, ADAM_STEP = 0.001, 0.9, 0.999, 1e-08, 0.01, 10

_SEGS = (("a_b", 0, 1024), ("a_c", 1024, 1024), ("a_x", 2048, 1024), ("a_z", 3072, 1024),
         ("q", 4096, 1024), ("z_attn", 5632, 1024), ("gv", 7680, 1024), ("zg", 8736, 1024),
         ("mg", 9760, 3072), ("gq", 6656, 512), ("gk", 7168, 512), ("k", 5120, 256), ("v", 5376, 256),
         ("r", 8704, 32))
IN_WIDTH = 12832
NP = 13312
OFF = {}
_o = 0
for _n, _s, _w in _SEGS:
    OFF[_n] = _o
    _o += _w
R_PAD = 128


def _cparams(ngrid, vmem_mb):
    return pltpu.CompilerParams(dimension_semantics=("arbitrary",) * ngrid, vmem_limit_bytes=vmem_mb << 20)


def _pick(n, cands):
    for c in cands:
        if n % c == 0:
            return c
    return n


def _sigmoid(x):
    return 1.0 / (1.0 + jnp.exp(-x))


ADAM_SRC_BYTES = 8 << 20
ADAM_ROW_BYTES = 1 << 20


def _all_gather(xs, name):
    return _comm_alone(_GatherRider(xs), name)


_HBM = pl.BlockSpec(memory_space=pl.ANY)


class _Rider:
    def __init__(self, xs, out_shapes):
        self.xs, self.n = list(xs), len(xs)
        self.out_shape = [jax.ShapeDtypeStruct(s, x.dtype) for s, x in zip(out_shapes, xs)]
        self.scratch = [pltpu.SemaphoreType.DMA((7 * self.n,)), pltpu.SemaphoreType.DMA((7 * self.n,)),
                        pltpu.SemaphoreType.DMA((self.n,))]


class _GatherRider(_Rider):
    def __init__(self, xs):
        super().__init__(xs, [(NDEV,) + x.shape for x in xs])

    def _parts(self, x_refs, out_refs, sems):
        n = self.n
        send_sems, recv_sems, local_sems = sems
        mx, my, mc = lax.axis_index("x"), lax.axis_index("y"), lax.axis_index("c")
        me, sibling = (mx, my, mc), (mx, my, 1 - mc)
        chips = [(1 - mx, my), (mx, 1 - my), (1 - mx, 1 - my)]

        def slot(a, px, py, pc):
            return out_refs[a].at[4 * px + 2 * py + pc]

        def copy(k, a, block, to, own=False):
            return pltpu.make_async_remote_copy(
                src_ref=x_refs[a] if own else slot(a, *block), dst_ref=slot(a, *block),
                send_sem=send_sems.at[k * n + a], recv_sem=recv_sems.at[k * n + a],
                device_id=to, device_id_type=pl.DeviceIdType.MESH)

        mine = [pltpu.make_async_copy(x_refs[a], slot(a, *me), local_sems.at[a]) for a in range(n)]
        first = [copy(0, a, me, sibling, own=True) for a in range(n)]
        first += [copy(1 + j, a, me, (*chip, mc), own=True) for a in range(n) for j, chip in enumerate(chips)]
        landed = [copy(1 + j, a, (*chip, mc), me) for a in range(n) for j, chip in enumerate(chips)]
        passed = [copy(4 + j, a, (*chip, mc), sibling) for a in range(n) for j, chip in enumerate(chips)]
        from_sibling = [copy(0, a, sibling, me) for a in range(n)]
        from_sibling += [copy(4 + j, a, (*chip, 1 - mc), me) for a in range(n) for j, chip in enumerate(chips)]
        return mine, first, landed, passed, from_sibling

    def start(self, x_refs, out_refs, sems):
        mine, first, _, _, _ = self._parts(x_refs, out_refs, sems)
        for cp in mine + first:
            cp.start()

    def middle(self, x_refs, out_refs, sems):
        _, _, landed, passed, _ = self._parts(x_refs, out_refs, sems)
        for got, fwd in zip(landed, passed):
            got.wait_recv()
            fwd.start()

    def finish(self, x_refs, out_refs, sems):
        mine, first, _, passed, from_sibling = self._parts(x_refs, out_refs, sems)
        for cp in from_sibling:
            cp.wait_recv()
        for cp in first + passed:
            cp.wait_send()
        for cp in mine:
            cp.wait()


class _ExchangeRider(_Rider):
    def __init__(self, xs):
        super().__init__(xs, [x.shape for x in xs])

    def _parts(self, x_refs, out_refs, sems):
        n = self.n
        send_sems, recv_sems, local_sems = sems
        mx, my, mc = lax.axis_index("x"), lax.axis_index("y"), lax.axis_index("c")
        me = 4 * mx + 2 * my + mc
        mine = [pltpu.make_async_copy(x_refs[a].at[me], out_refs[a].at[me], local_sems.at[a]) for a in range(n)]
        copies = []
        for a in range(n):
            for rel in range(1, NDEV):
                px = (1 - mx) if rel & 4 else mx
                py = (1 - my) if rel & 2 else my
                pc = (1 - mc) if rel & 1 else mc
                peer = 4 * px + 2 * py + pc
                k = (rel - 1) * n + a
                copies.append(pltpu.make_async_remote_copy(
                    src_ref=x_refs[a].at[peer], dst_ref=out_refs[a].at[me],
                    send_sem=send_sems.at[k], recv_sem=recv_sems.at[k],
                    device_id=(px, py, pc), device_id_type=pl.DeviceIdType.MESH))
        return mine, copies

    def start(self, x_refs, out_refs, sems):
        mine, copies = self._parts(x_refs, out_refs, sems)
        for cp in mine + copies:
            cp.start()

    def middle(self, x_refs, out_refs, sems):
        pass

    def finish(self, x_refs, out_refs, sems):
        mine, copies = self._parts(x_refs, out_refs, sems)
        for cp in copies:
            cp.wait_recv()
        for cp in copies:
            cp.wait_send()
        for cp in mine:
            cp.wait()


def _comm_alone(rider, name):
    n = rider.n

    def body(*refs):
        x_refs, out_refs, sems = refs[:n], refs[n:2 * n], refs[2 * n:]
        rider.start(x_refs, out_refs, sems)
        rider.middle(x_refs, out_refs, sems)
        rider.finish(x_refs, out_refs, sems)

    return pl.pallas_call(
        body, name=name, out_shape=tuple(rider.out_shape), in_specs=[_HBM] * n, out_specs=(_HBM,) * n,
        scratch_shapes=rider.scratch,
    )(*rider.xs)


def _with_rider(body, nin, nout, rider, first, mid, last):
    if rider is None:
        return body
    n = rider.n

    def wrapped(*refs):
        ins, x_refs = refs[:nin], refs[nin:nin + n]
        outs, out_refs = refs[nin + n:nin + n + nout], refs[nin + n + nout:nin + 2 * n + nout]
        scratch, sems = refs[nin + 2 * n + nout:-3], refs[-3:]

        @pl.when(first())
        def _():
            rider.start(x_refs, out_refs, sems)

        body(*ins, *outs, *scratch)

        @pl.when(mid())
        def _():
            rider.middle(x_refs, out_refs, sems)

        @pl.when(last())
        def _():
            rider.finish(x_refs, out_refs, sems)

    return wrapped


def _mm(a, b, name, ta=False, tb=False, out_dtype=F32, bias=None, precise=False, tm=None, tn=None, tk=None, rider=None):
    m, k = (a.shape[1], a.shape[0]) if ta else a.shape
    n = b.shape[0] if tb else b.shape[1]
    assert k == (b.shape[1] if tb else b.shape[0])
    tm = tm or _pick(m, (1088, 1024, 512, 256, 128))
    tn = tn or _pick(n, (1024, 512, 384, 256, 128))
    tk = tk or _pick(k, (1024, 1088, 512, 256, 128))
    nk = k // tk
    dn = (((0 if ta else 1,), (1 if tb else 0,)), ((), ()))

    def body(*refs):
        if bias is None:
            a_ref, b_ref, o_ref = refs[:3]
            bias_ref = None
        else:
            a_ref, b_ref, bias_ref, o_ref = refs[:4]
        x, y = a_ref[...], b_ref[...]
        if precise:
            p = lax.dot_general(x.astype(F32), y.astype(F32), dn, preferred_element_type=F32, precision=HIGHEST)
        else:
            p = lax.dot_general(x.astype(BF16), y.astype(BF16), dn, preferred_element_type=F32)

        def finish(acc):
            if bias_ref is not None:
                acc = acc + bias_ref[...]
            o_ref[...] = acc.astype(out_dtype)

        if nk == 1:
            finish(p)
        else:
            acc_ref = refs[-1]
            kk = pl.program_id(2)

            @pl.when(kk == 0)
            def _():
                acc_ref[...] = p

            @pl.when(kk > 0)
            def _():
                acc_ref[...] += p

            @pl.when(kk == nk - 1)
            def _():
                finish(acc_ref[...])

    a_spec = pl.BlockSpec((tk, tm), lambda i, j, kk: (kk, i)) if ta else pl.BlockSpec((tm, tk), lambda i, j, kk: (i, kk))
    b_spec = pl.BlockSpec((tn, tk), lambda i, j, kk: (j, kk)) if tb else pl.BlockSpec((tk, tn), lambda i, j, kk: (kk, j))
    in_specs = [a_spec, b_spec]
    args = [a, b]
    if bias is not None:
        in_specs.append(pl.BlockSpec((1, tn), lambda i, j, kk: (0, j)))
        args.append(bias)
    grid = (m // tm, n // tn, nk)
    out_spec = pl.BlockSpec((tm, tn), lambda i, j, kk: (i, j))
    scratch = [pltpu.VMEM((tm, tn), F32)] if nk > 1 else []
    if rider is None:
        return pl.pallas_call(
            body, name=name, grid=grid, in_specs=in_specs, out_specs=out_spec,
            out_shape=jax.ShapeDtypeStruct((m, n), out_dtype), scratch_shapes=scratch, compiler_params=_cparams(3, 56),
        )(*args)

    def at(step):
        return lambda: ((pl.program_id(0) == step[0]) & (pl.program_id(1) == step[1]) & (pl.program_id(2) == step[2]))

    end = tuple(g - 1 for g in grid)
    return pl.pallas_call(
        _with_rider(body, len(args), 1, rider, at((0, 0, 0)), at((grid[0] // 2, 0, 0)), at(end)),
        name=name, grid=grid, in_specs=in_specs + [_HBM] * rider.n, out_specs=(out_spec,) + (_HBM,) * rider.n,
        out_shape=(jax.ShapeDtypeStruct((m, n), out_dtype),) + tuple(rider.out_shape),
        scratch_shapes=scratch + rider.scratch, compiler_params=_cparams(3, 56),
    )(*args, *rider.xs)


def _ada_in(cc):
    def body(c_ref, s_ref, d_ref):
        x = c_ref[...]
        sg = _sigmoid(x)
        s_ref[...] = x * sg
        d_ref[...] = sg * (1.0 + x * (1.0 - sg))

    return pl.pallas_call(body, name="ada_in", out_shape=(jax.ShapeDtypeStruct(cc.shape, F32),) * 2)(cc)


def _cctx_grad(t0, t1, dsilu):
    def body(a_ref, b_ref, d_ref, o_ref):
        o_ref[...] = (a_ref[...] + b_ref[...]) * d_ref[...]

    return pl.pallas_call(body, name="cctx_grad", out_shape=jax.ShapeDtypeStruct(t0.shape, F32))(t0, t1, dsilu)


def _seg_spec(nct, rows=3):
    return pl.BlockSpec((None, rows, D), lambda i: (jnp.where(i >= nct, 1, 0), 0, 0))


def _prenorm_fwd(x, g_pre, mod3, nct, name):
    t = x.shape[0]

    def body(x_ref, g_ref, mod_ref, h_ref):
        xv = x_ref[...]
        r = lax.rsqrt(jnp.mean(xv * xv, axis=-1, keepdims=True) + EPS)
        y = xv * r * g_ref[...]
        h_ref[...] = (y * (1.0 + mod_ref[1:2, :]) + mod_ref[0:1, :]).astype(BF16)

    return pl.pallas_call(
        body, name=name, grid=(t // TM,),
        in_specs=[pl.BlockSpec((TM, D), lambda i: (i, 0)), pl.BlockSpec((1, D), lambda i: (0, 0)), _seg_spec(nct)],
        out_specs=pl.BlockSpec((TM, D), lambda i: (i, 0)),
        out_shape=jax.ShapeDtypeStruct((t, D), BF16), compiler_params=_cparams(1, 32),
    )(x, g_pre, mod3)


def _prenorm_bwd(dh, x, dxo, g_pre, mod3, nct, name):
    t = x.shape[0]

    def body(dh_ref, x_ref, dxo_ref, g_ref, mod_ref, dx_ref, dsh_ref, dsc_ref, dg_ref):
        i = pl.program_id(0)
        xv, dhv, g = x_ref[...], dh_ref[...], g_ref[...]
        r = lax.rsqrt(jnp.mean(xv * xv, axis=-1, keepdims=True) + EPS)
        xh = xv * r
        dy = dhv * (1.0 + mod_ref[1:2, :])
        dxh = dy * g
        dx_ref[...] = dxo_ref[...] + r * (dxh - xh * jnp.mean(dxh * xh, axis=-1, keepdims=True))

        @pl.when((i == 0) | (i == nct))
        def _():
            dsh_ref[...] = jnp.zeros_like(dsh_ref)
            dsc_ref[...] = jnp.zeros_like(dsc_ref)

        @pl.when(i == 0)
        def _():
            dg_ref[...] = jnp.zeros_like(dg_ref)

        dsh_ref[...] += jnp.sum(dhv, axis=0, keepdims=True)
        dsc_ref[...] += jnp.sum(dhv * (xh * g), axis=0, keepdims=True)
        dg_ref[...] += jnp.sum(dy * xh, axis=0, keepdims=True)

    row = pl.BlockSpec((TM, D), lambda i: (i, 0))
    seg8 = pl.BlockSpec((None, 8, D), lambda i: (jnp.where(i >= nct, 1, 0), 0, 0))
    return pl.pallas_call(
        body, name=name, grid=(t // TM,),
        in_specs=[row, row, row, pl.BlockSpec((1, D), lambda i: (0, 0)), _seg_spec(nct)],
        out_specs=(row, seg8, seg8, pl.BlockSpec((8, D), lambda i: (0, 0))),
        out_shape=(jax.ShapeDtypeStruct((t, D), F32), jax.ShapeDtypeStruct((2, 8, D), F32),
                   jax.ShapeDtypeStruct((2, 8, D), F32), jax.ShapeDtypeStruct((8, D), F32)),
        compiler_params=_cparams(1, 32),
    )(dh, x, dxo, g_pre, mod3)


def _post_fwd(x, out, g_post, mod3, nct, name):
    t = x.shape[0]

    def body(x_ref, o_ref, g_ref, mod_ref, y_ref):
        ov = o_ref[...]
        r = lax.rsqrt(jnp.mean(ov * ov, axis=-1, keepdims=True) + EPS)
        y_ref[...] = x_ref[...] + mod_ref[2:3, :] * (ov * r * g_ref[...])

    row = pl.BlockSpec((TM, D), lambda i: (i, 0))
    return pl.pallas_call(
        body, name=name, grid=(t // TM,),
        in_specs=[row, row, pl.BlockSpec((1, D), lambda i: (0, 0)), _seg_spec(nct)],
        out_specs=row, out_shape=jax.ShapeDtypeStruct((t, D), F32), compiler_params=_cparams(1, 32),
    )(x, out, g_post, mod3)


def _post_bwd(dxo, out, g_post, mod3, nct, name):
    t = out.shape[0]

    def body(dx_ref, o_ref, g_ref, mod_ref, do_ref, dgt_ref, dg_ref):
        i = pl.program_id(0)
        ov, dxv, g = o_ref[...], dx_ref[...], g_ref[...]
        r = lax.rsqrt(jnp.mean(ov * ov, axis=-1, keepdims=True) + EPS)
        nh = ov * r
        dn = dxv * mod_ref[2:3, :]
        dnh = dn * g
        do_ref[...] = (r * (dnh - nh * jnp.mean(dnh * nh, axis=-1, keepdims=True))).astype(BF16)

        @pl.when((i == 0) | (i == nct))
        def _():
            dgt_ref[...] = jnp.zeros_like(dgt_ref)

        @pl.when(i == 0)
        def _():
            dg_ref[...] = jnp.zeros_like(dg_ref)

        dgt_ref[...] += jnp.sum(dxv * (nh * g), axis=0, keepdims=True)
        dg_ref[...] += jnp.sum(dn * nh, axis=0, keepdims=True)

    row = pl.BlockSpec((TM, D), lambda i: (i, 0))
    seg8 = pl.BlockSpec((None, 8, D), lambda i: (jnp.where(i >= nct, 1, 0), 0, 0))
    return pl.pallas_call(
        body, name=name, grid=(t // TM,),
        in_specs=[row, row, pl.BlockSpec((1, D), lambda i: (0, 0)), _seg_spec(nct)],
        out_specs=(row, seg8, pl.BlockSpec((8, D), lambda i: (0, 0))),
        out_shape=(jax.ShapeDtypeStruct((t, D), BF16), jax.ShapeDtypeStruct((2, 8, D), F32),
                   jax.ShapeDtypeStruct((8, D), F32)),
        compiler_params=_cparams(1, 32),
    )(dxo, out, g_post, mod3)


def _loss_grad(y, target, nct, name):
    t = y.shape[0]

    def body(y_ref, t_ref, dy_ref, l_ref):
        i = pl.program_id(0)

        @pl.when(i == 0)
        def _():
            l_ref[...] = jnp.zeros_like(l_ref)

        @pl.when(i < nct)
        def _():
            dy_ref[...] = jnp.zeros_like(dy_ref)

        @pl.when(i >= nct)
        def _():
            err = y_ref[...] - t_ref[...]
            dy_ref[...] = err / D
            l_ref[...] += jnp.sum(jnp.sum(err * err, axis=1, keepdims=True), axis=0, keepdims=True)

    row = pl.BlockSpec((TM, D), lambda i: (i, 0))
    return pl.pallas_call(
        body, name=name, grid=(t // TM,),
        in_specs=[row, pl.BlockSpec((TM, D), lambda i: (jnp.maximum(i - nct, 0), 0))],
        out_specs=(row, pl.BlockSpec((8, LANE), lambda i: (0, 0))),
        out_shape=(jax.ShapeDtypeStruct((t, D), F32), jax.ShapeDtypeStruct((8, LANE), F32)),
        compiler_params=_cparams(1, 32),
    )(y, target)


def _pcol(name, width):
    assert OFF[name] % width == 0
    blk = OFF[name] // width
    return pl.BlockSpec((TM, width), lambda i: (i, blk))


def _shift_rows(u, prev_row, next_row):
    n = u.shape[0]
    row = lax.broadcasted_iota(jnp.int32, u.shape, 0)
    prev = jnp.where(row == 0, prev_row, pltpu.roll(u, 1, 0))
    nxt = jnp.where(row == n - 1, next_row, pltpu.roll(u, n - 1, 0))
    return prev, nxt


def _halo_specs(width, nt, blk=0):
    r8 = TM // 8
    prev = pl.BlockSpec((8, width), lambda i: (jnp.maximum(i * r8 - 1, 0), blk))
    nxt = pl.BlockSpec((8, width), lambda i: (jnp.minimum((i + 1) * r8, nt * r8 - 1), blk))
    return prev, nxt


def _conv_fwd(p, conv_w8, nct, name):
    t = p.shape[0]
    nt = t // TM

    def body(ab_ref, ac_ref, ax_ref, az_ref, acp_ref, axp_ref, acn_ref, axn_ref, w_ref, cv_ref, ya_ref):
        i = pl.program_id(0)
        u = ac_ref[...] * ax_ref[...]
        mp = jnp.where((i == 0) | (i == nct), 0.0, 1.0)
        mn = jnp.where((i == nct - 1) | (i == nt - 1), 0.0, 1.0)
        prev, nxt = _shift_rows(u, acp_ref[7:8, :] * axp_ref[7:8, :] * mp, acn_ref[0:1, :] * axn_ref[0:1, :] * mn)
        cv = w_ref[0:1, :] * prev + w_ref[1:2, :] * u + w_ref[2:3, :] * nxt
        az = az_ref[...]
        cv_ref[...] = cv
        ya_ref[...] = (ab_ref[...] * cv * (az * _sigmoid(az))).astype(BF16)

    acp, acn = _halo_specs(D, nt, OFF["a_c"] // D)
    axp, axn = _halo_specs(D, nt, OFF["a_x"] // D)
    row = pl.BlockSpec((TM, D), lambda i: (i, 0))
    return pl.pallas_call(
        body, name=name, grid=(nt,),
        in_specs=[_pcol("a_b", D), _pcol("a_c", D), _pcol("a_x", D), _pcol("a_z", D), acp, axp, acn, axn,
                  pl.BlockSpec((8, D), lambda i: (0, 0))],
        out_specs=(row, row),
        out_shape=(jax.ShapeDtypeStruct((t, D), F32), jax.ShapeDtypeStruct((t, D), BF16)),
        compiler_params=_cparams(1, 40),
    )(p, p, p, p, p, p, p, p, conv_w8)


def _conv_bwd_a(dya, p, cv, name):
    t = p.shape[0]

    def body(dy_ref, ab_ref, az_ref, cv_ref, dcv_ref, dab_ref, daz_ref):
        dy, ab, az, c = dy_ref[...], ab_ref[...], az_ref[...], cv_ref[...]
        sg = _sigmoid(az)
        sz = az * sg
        dcv_ref[...] = dy * ab * sz
        dab_ref[...] = (dy * c * sz).astype(BF16)
        daz_ref[...] = (dy * ab * c * (sg * (1.0 + az * (1.0 - sg)))).astype(BF16)

    row = pl.BlockSpec((TM, D), lambda i: (i, 0))
    return pl.pallas_call(
        body, name=name, grid=(t // TM,),
        in_specs=[row, _pcol("a_b", D), _pcol("a_z", D), row], out_specs=(row, row, row),
        out_shape=(jax.ShapeDtypeStruct((t, D), F32), jax.ShapeDtypeStruct((t, D), BF16),
                   jax.ShapeDtypeStruct((t, D), BF16)),
        compiler_params=_cparams(1, 40),
    )(dya, p, p, cv)


def _conv_bwd_b(dcv, p, conv_w8, nct, name):
    t = p.shape[0]
    nt = t // TM

    def body(dcv_ref, dp_ref, dn_ref, ac_ref, ax_ref, w_ref, dac_ref, dax_ref, dw_ref):
        i = pl.program_id(0)
        d, ac, ax = dcv_ref[...], ac_ref[...], ax_ref[...]
        u = ac * ax
        mp = jnp.where((i == 0) | (i == nct), 0.0, 1.0)
        mn = jnp.where((i == nct - 1) | (i == nt - 1), 0.0, 1.0)
        dprev, dnxt = _shift_rows(d, dp_ref[7:8, :] * mp, dn_ref[0:1, :] * mn)
        du = w_ref[0:1, :] * dnxt + w_ref[1:2, :] * d + w_ref[2:3, :] * dprev
        dac_ref[...] = (du * ax).astype(BF16)
        dax_ref[...] = (du * ac).astype(BF16)

        @pl.when(i == 0)
        def _():
            dw_ref[...] = jnp.zeros_like(dw_ref)

        dw0 = jnp.sum(u * dnxt, axis=0, keepdims=True)
        dw1 = jnp.sum(u * d, axis=0, keepdims=True)
        dw2 = jnp.sum(u * dprev, axis=0, keepdims=True)
        r8 = lax.broadcasted_iota(jnp.int32, (8, D), 0)
        dw_ref[...] += jnp.where(r8 == 0, dw0, jnp.where(r8 == 1, dw1, jnp.where(r8 == 2, dw2, 0.0)))

    dp, dn = _halo_specs(D, nt)
    row = pl.BlockSpec((TM, D), lambda i: (i, 0))
    return pl.pallas_call(
        body, name=name, grid=(nt,),
        in_specs=[row, dp, dn, _pcol("a_c", D), _pcol("a_x", D), pl.BlockSpec((8, D), lambda i: (0, 0))],
        out_specs=(row, row, pl.BlockSpec((8, D), lambda i: (0, 0))),
        out_shape=(jax.ShapeDtypeStruct((t, D), BF16), jax.ShapeDtypeStruct((t, D), BF16),
                   jax.ShapeDtypeStruct((8, D), F32)),
        compiler_params=_cparams(1, 40),
    )(dcv, dcv, dcv, p, p, conv_w8)


def _rot_half(x):
    lane = lax.broadcasted_iota(jnp.int32, x.shape, 1)
    return jnp.where((lane % 64) < 32, pltpu.roll(x, 96, 1), pltpu.roll(x, 32, 1))


def _qk_prep_fwd(p, qg, kg, cos_t, sin_t, name):
    t = p.shape[0]

    def body(q_ref, k_ref, qg_ref, kg_ref, c_ref, s_ref, qo_ref, ko_ref):
        c, s = c_ref[...], s_ref[...]

        def one(xv, g):
            y = xv * lax.rsqrt(jnp.mean(xv * xv, axis=-1, keepdims=True) + EPS) * g
            return (y * c + _rot_half(y) * s).astype(BF16)

        for h in range(NH):
            qo_ref[:, h * HD:(h + 1) * HD] = one(q_ref[:, h * HD:(h + 1) * HD], qg_ref[...])
        for h in range(NKV):
            ko_ref[:, h * HD:(h + 1) * HD] = one(k_ref[:, h * HD:(h + 1) * HD], kg_ref[...])

    vec = pl.BlockSpec((1, HD), lambda i: (0, 0))
    tab = pl.BlockSpec((TM, HD), lambda i: (i, 0))
    return pl.pallas_call(
        body, name=name, grid=(t // TM,),
        in_specs=[_pcol("q", NH * HD), _pcol("k", NKV * HD), vec, vec, tab, tab],
        out_specs=(pl.BlockSpec((TM, NH * HD), lambda i: (i, 0)), pl.BlockSpec((TM, NKV * HD), lambda i: (i, 0))),
        out_shape=(jax.ShapeDtypeStruct((t, NH * HD), BF16), jax.ShapeDtypeStruct((t, NKV * HD), BF16)),
        compiler_params=_cparams(1, 32),
    )(p, p, qg, kg, cos_t, sin_t)


def _qk_prep_bwd(dqr, dkr, p, qg, kg, cos_t, sin_t, name):
    t = p.shape[0]

    def body(dq_ref, dk_ref, q_ref, k_ref, qg_ref, kg_ref, c_ref, s_ref, dqo_ref, dko_ref, dqg_ref, dkg_ref):
        i = pl.program_id(0)
        c, s = c_ref[...], s_ref[...]

        @pl.when(i == 0)
        def _():
            dqg_ref[...] = jnp.zeros_like(dqg_ref)
            dkg_ref[...] = jnp.zeros_like(dkg_ref)

        def one(dyr, xv, g):
            dy = dyr * c + _rot_half(dyr * s)
            r = lax.rsqrt(jnp.mean(xv * xv, axis=-1, keepdims=True) + EPS)
            xh = xv * r
            dxh = dy * g
            dx = r * (dxh - xh * jnp.mean(dxh * xh, axis=-1, keepdims=True))
            return dx.astype(BF16), jnp.sum(dy * xh, axis=0, keepdims=True)

        for h in range(NH):
            sl = slice(h * HD, (h + 1) * HD)
            dx, dg = one(dq_ref[:, sl], q_ref[:, sl], qg_ref[...])
            dqo_ref[:, sl] = dx
            dqg_ref[...] += dg
        for h in range(NKV):
            sl = slice(h * HD, (h + 1) * HD)
            dx, dg = one(dk_ref[:, sl], k_ref[:, sl], kg_ref[...])
            dko_ref[:, sl] = dx
            dkg_ref[...] += dg

    vec = pl.BlockSpec((1, HD), lambda i: (0, 0))
    tab = pl.BlockSpec((TM, HD), lambda i: (i, 0))
    acc = pl.BlockSpec((8, HD), lambda i: (0, 0))
    qrow = pl.BlockSpec((TM, NH * HD), lambda i: (i, 0))
    krow = pl.BlockSpec((TM, NKV * HD), lambda i: (i, 0))
    return pl.pallas_call(
        body, name=name, grid=(t // TM,),
        in_specs=[qrow, krow, _pcol("q", NH * HD), _pcol("k", NKV * HD), vec, vec, tab, tab],
        out_specs=(qrow, krow, acc, acc),
        out_shape=(jax.ShapeDtypeStruct((t, NH * HD), BF16), jax.ShapeDtypeStruct((t, NKV * HD), BF16),
                   jax.ShapeDtypeStruct((8, HD), F32), jax.ShapeDtypeStruct((8, HD), F32)),
        compiler_params=_cparams(1, 32),
    )(dqr, dkr, p, p, qg, kg, cos_t, sin_t)


def _attn_exp(q, k):
    s = lax.dot_general(q, k, _NT, preferred_element_type=F32) * ATTN_SCALE
    e = jnp.exp(s - jnp.max(s, axis=-1, keepdims=True))
    return e, jnp.sum(e, axis=-1, keepdims=True)


def _attn_fwd(qr, kr, p, nct, name, rider=None):
    t = qr.shape[0]
    nt = t // TM
    ctx = nct * TM
    vblk = OFF["v"] // HD

    def body(q_ref, k_ref, v_ref, o_ref):
        def tile(keys):
            e, l = _attn_exp(q_ref[...], k_ref[keys, :])
            o_ref[...] = jnp.dot(e.astype(BF16), v_ref[keys, :].astype(BF16), preferred_element_type=F32) / l

        pl.when(pl.program_id(1) < nct)(lambda: tile(slice(0, ctx)))
        pl.when(pl.program_id(1) >= nct)(lambda: tile(slice(None)))

    def at(h, i):
        return lambda: (pl.program_id(0) == h) & (pl.program_id(1) == i)

    rn = 0 if rider is None else rider.n
    return pl.pallas_call(
        _with_rider(body, 3, 1, rider, at(0, 0), at(NH * 3 // 4, 0), at(NH - 1, nt - 1)),
        name=name, grid=(NH, nt),
        in_specs=[pl.BlockSpec((TM, HD), lambda h, i: (i, h)),
                  pl.BlockSpec((t, HD), lambda h, i: (0, h // GROUP)),
                  pl.BlockSpec((t, HD), lambda h, i: (0, vblk + h // GROUP))] + [_HBM] * rn,
        out_specs=(pl.BlockSpec((TM, HD), lambda h, i: (i, h)),) + (_HBM,) * rn,
        out_shape=(jax.ShapeDtypeStruct((t, NH * HD), F32),) + (() if rider is None else tuple(rider.out_shape)),
        scratch_shapes=[] if rider is None else rider.scratch,
        compiler_params=_cparams(2, 48),
    )(qr, kr, p, *(() if rider is None else rider.xs))


def _attn_bwd(qr, kr, p, o, do, nct, name, rider=None):
    t = qr.shape[0]
    nt = t // TM
    ctx = nct * TM
    vblk = OFF["v"] // HD

    def body(q_ref, k_ref, v_ref, o_ref, do_ref, dq_ref, dk_ref, dv_ref):
        g, i = pl.program_id(1), pl.program_id(2)

        @pl.when((g == 0) & (i == 0))
        def _():
            dk_ref[...] = jnp.zeros_like(dk_ref)
            dv_ref[...] = jnp.zeros_like(dv_ref)

        def tile(keys):
            q, k = q_ref[...], k_ref[keys, :]
            vb = v_ref[keys, :].astype(BF16)
            dov = do_ref[...]
            dob = dov.astype(BF16)
            e, l = _attn_exp(q, k)
            pr = e * (1.0 / l)
            dp = lax.dot_general(dob, vb, _NT, preferred_element_type=F32)
            drow = jnp.sum(dov * o_ref[...], axis=-1, keepdims=True)
            ds = (pr * (dp - drow) * ATTN_SCALE).astype(BF16)
            dq_ref[...] = jnp.dot(ds, k, preferred_element_type=F32)
            dk_ref[keys, :] += lax.dot_general(ds, q, _TN, preferred_element_type=F32)
            dv_ref[keys, :] += lax.dot_general(pr.astype(BF16), dob, _TN, preferred_element_type=F32)

        pl.when(i < nct)(lambda: tile(slice(0, ctx)))
        pl.when(i >= nct)(lambda: tile(slice(None)))

    def at(kv, g, i):
        return lambda: (pl.program_id(0) == kv) & (pl.program_id(1) == g) & (pl.program_id(2) == i)

    rn = 0 if rider is None else rider.n
    qspec = pl.BlockSpec((TM, HD), lambda kv, g, i: (i, kv * GROUP + g))
    kvspec = pl.BlockSpec((t, HD), lambda kv, g, i: (0, kv))
    return pl.pallas_call(
        _with_rider(body, 5, 3, rider, at(0, 0, 0), at(NKV - 1, 0, 0), at(NKV - 1, GROUP - 1, nt - 1)),
        name=name, grid=(NKV, GROUP, nt),
        in_specs=[qspec, kvspec, pl.BlockSpec((t, HD), lambda kv, g, i: (0, vblk + kv)), qspec, qspec] + [_HBM] * rn,
        out_specs=(qspec, kvspec, kvspec) + (_HBM,) * rn,
        out_shape=(jax.ShapeDtypeStruct((t, NH * HD), F32), jax.ShapeDtypeStruct((t, NKV * HD), F32),
                   jax.ShapeDtypeStruct((t, NKV * HD), F32)) + (() if rider is None else tuple(rider.out_shape)),
        scratch_shapes=[] if rider is None else rider.scratch,
        compiler_params=_cparams(3, 48),
    )(qr, kr, p, o, do, *(() if rider is None else rider.xs))


def _decay_fwd(p, wd, bd, name):
    t = p.shape[0]

    def body(r_ref, w_ref, b_ref, z_ref, la_ref):
        z = jnp.dot(r_ref[...].astype(BF16), w_ref[...].astype(BF16), preferred_element_type=F32) + b_ref[...]
        z_ref[...] = z
        la_ref[...] = (jnp.minimum(z, 0.0) - jnp.log(1.0 + jnp.exp(-jnp.abs(z)))) / GLA_TAU

    row = pl.BlockSpec((TM, D), lambda i: (i, 0))
    return pl.pallas_call(
        body, name=name, grid=(t // TM,),
        in_specs=[_pcol("r", R_PAD), pl.BlockSpec((R_PAD, D), lambda i: (0, 0)), pl.BlockSpec((1, D), lambda i: (0, 0))],
        out_specs=(row, row),
        out_shape=(jax.ShapeDtypeStruct((t, D), F32), jax.ShapeDtypeStruct((t, D), F32)),
        compiler_params=_cparams(1, 32),
    )(p, wd, bd)


def _chunk_order(s, ncc, nc, rev):
    if not rev:
        return s
    return jnp.where(s < ncc, ncc - 1 - s, nc - 1 - (s - ncc))


def _gla_chunk(qv, kv, lav, rev):
    r = lax.broadcasted_iota(jnp.int32, (CH, CH), 0)
    c = lax.broadcasted_iota(jnp.int32, (CH, CH), 1)
    keep = (c >= r) if rev else (c <= r)
    tri = keep.astype(F32)
    bc = jnp.dot(tri, lav, preferred_element_type=F32, precision=HIGHEST)
    bl = jnp.sum(lav, axis=0, keepdims=True)
    qt = qv * GLA_SCALE * jnp.exp(bc)
    kt = kv * jnp.exp(-bc)
    kh = kv * jnp.exp(bl - bc)
    return qt, kt, jnp.exp(bl), kh, keep, bc


_NT = (((1,), (1,)), ((), ()))
_TN = (((0,), (0,)), ((), ()))


def _gla_specs(ncc, nc, rev, backward):
    def idx(s):
        return _chunk_order((nc - 1 - s) if backward else s, ncc, nc, rev)

    wk, wv = GH * GDK, GH * GDV
    qb, kb, vb = OFF["gq"] // wk, OFF["gk"] // wk, OFF["gv"] // wv
    lab = 1 if rev else 0
    q = pl.BlockSpec((CH, wk), lambda s: (idx(s), qb))
    k = pl.BlockSpec((CH, wk), lambda s: (idx(s), kb))
    v = pl.BlockSpec((CH, wv), lambda s: (idx(s), vb))
    la = pl.BlockSpec((CH, wk), lambda s: (idx(s), lab))
    o = pl.BlockSpec((CH, wv), lambda s: (idx(s), 0))
    dk = pl.BlockSpec((CH, wk), lambda s: (idx(s), 0))
    st = pl.BlockSpec((None, GH, GDV, GDK), lambda s: (idx(s), 0, 0, 0))
    return q, k, v, la, o, dk, st


def _gla_fwd(p, la, ncc, rev, name):
    t = p.shape[0]
    nc = t // CH
    q_s, k_s, v_s, la_s, o_s, _, st_s = _gla_specs(ncc, nc, rev, False)

    def body(q_ref, k_ref, v_ref, la_ref, o_ref, st_ref, s_scr):
        @pl.when(pl.program_id(0) == 0)
        def _():
            s_scr[...] = jnp.zeros_like(s_scr)

        for h in range(GH):
            sk, sv = slice(h * GDK, (h + 1) * GDK), slice(h * GDV, (h + 1) * GDV)
            qt, kt, gl, kh, keep, _ = _gla_chunk(q_ref[:, sk], k_ref[:, sk], la_ref[:, sk], rev)
            st = s_scr[h]
            st_ref[h] = st
            vb = v_ref[:, sv].astype(BF16)
            qb = qt.astype(BF16)
            a = jnp.where(keep, lax.dot_general(qb, kt.astype(BF16), _NT, preferred_element_type=F32), 0.0)
            o_ref[:, sv] = (lax.dot_general(qb, st.astype(BF16), _NT, preferred_element_type=F32)
                            + jnp.dot(a.astype(BF16), vb, preferred_element_type=F32))
            s_scr[h] = st * gl + lax.dot_general(vb, kh.astype(BF16), _TN, preferred_element_type=F32)

    return pl.pallas_call(
        body, name=name, grid=(nc,),
        in_specs=[q_s, k_s, v_s, la_s], out_specs=(o_s, st_s),
        out_shape=(jax.ShapeDtypeStruct((t, GH * GDV), F32), jax.ShapeDtypeStruct((nc, GH, GDV, GDK), F32)),
        scratch_shapes=[pltpu.VMEM((GH, GDV, GDK), F32)], compiler_params=_cparams(1, 32),
    )(p, p, p, la)


def _gla_bwd(p, la, do, st, ncc, rev, name):
    t = p.shape[0]
    nc = t // CH
    q_s, k_s, v_s, la_s, o_s, dk_s, st_s = _gla_specs(ncc, nc, rev, True)

    def body(q_ref, k_ref, v_ref, la_ref, do_ref, st_ref, dq_ref, dk_ref, dv_ref, dla_ref, ds_scr):
        @pl.when(pl.program_id(0) == 0)
        def _():
            ds_scr[...] = jnp.zeros_like(ds_scr)

        row = lax.broadcasted_iota(jnp.int32, (CH, GDK), 0)
        r = lax.broadcasted_iota(jnp.int32, (CH, CH), 0)
        c = lax.broadcasted_iota(jnp.int32, (CH, CH), 1)
        trit = ((c <= r) if rev else (c >= r)).astype(F32)
        last = 0 if rev else CH - 1
        for h in range(GH):
            sk, sv = slice(h * GDK, (h + 1) * GDK), slice(h * GDV, (h + 1) * GDV)
            qt, kt, gl, kh, keep, bc = _gla_chunk(q_ref[:, sk], k_ref[:, sk], la_ref[:, sk], rev)
            stv = st_ref[h]
            dsn = ds_scr[h]
            dsb = dsn.astype(BF16)
            vb, dob = v_ref[:, sv].astype(BF16), do_ref[:, sv].astype(BF16)
            qb, kb = qt.astype(BF16), kt.astype(BF16)
            a = jnp.where(keep, lax.dot_general(qb, kb, _NT, preferred_element_type=F32), 0.0).astype(BF16)
            da = jnp.where(keep, lax.dot_general(dob, vb, _NT, preferred_element_type=F32), 0.0).astype(BF16)
            dqt = (jnp.dot(dob, stv.astype(BF16), preferred_element_type=F32)
                   + jnp.dot(da, kb, preferred_element_type=F32))
            dkh = jnp.dot(vb, dsb, preferred_element_type=F32)
            dkt = lax.dot_general(da, qb, _TN, preferred_element_type=F32) + dkh * gl
            dv_ref[:, sv] = (lax.dot_general(a, dob, _TN, preferred_element_type=F32)
                             + lax.dot_general(kh.astype(BF16), dsb, _NT, preferred_element_type=F32))
            ds_scr[h] = lax.dot_general(dob, qb, _TN, preferred_element_type=F32) + dsn * gl
            dgl = jnp.sum(stv * dsn, axis=0, keepdims=True) + jnp.sum(dkh * kt, axis=0, keepdims=True)
            dbc = dqt * qt - dkt * kt + jnp.where(row == last, dgl * gl, 0.0)
            dla_ref[:, sk] = jnp.dot(trit, dbc, preferred_element_type=F32, precision=HIGHEST)
            dq_ref[:, sk] = dqt * (GLA_SCALE * jnp.exp(bc))
            dk_ref[:, sk] = dkt * jnp.exp(-bc)

    return pl.pallas_call(
        body, name=name, grid=(nc,),
        in_specs=[q_s, k_s, v_s, la_s, o_s, st_s], out_specs=(dk_s, dk_s, o_s, dk_s),
        out_shape=(jax.ShapeDtypeStruct((t, GH * GDK), F32), jax.ShapeDtypeStruct((t, GH * GDK), F32),
                   jax.ShapeDtypeStruct((t, GH * GDV), F32), jax.ShapeDtypeStruct((t, GH * GDK), F32)),
        scratch_shapes=[pltpu.VMEM((GH, GDV, GDK), F32)], compiler_params=_cparams(1, 32),
    )(p, p, p, la, do, st)


def _gla_merge_bwd(gf, gb, z, p, wd, name):
    t = p.shape[0]
    w2 = GH * GDK

    def body(dqf, dkf, dvf, dlf, dqb, dkb, dvb, dlb, z_ref, r_ref, w_ref, dq_ref, dk_ref, dv_ref, dr_ref, db_ref, dw_ref):
        i = pl.program_id(0)
        dq_ref[...] = (dqf[...] + dqb[...]).astype(BF16)
        dk_ref[...] = (dkf[...] + dkb[...]).astype(BF16)
        dv_ref[...] = (dvf[...] + dvb[...]).astype(BF16)
        zv = z_ref[...]
        dz = jnp.concatenate([dlf[...], dlb[...]], axis=1) * (_sigmoid(-zv) / GLA_TAU)
        dzb = dz.astype(BF16)
        dr_ref[...] = lax.dot_general(dzb, w_ref[...].astype(BF16), _NT, preferred_element_type=F32).astype(BF16)

        @pl.when(i == 0)
        def _():
            db_ref[...] = jnp.zeros_like(db_ref)
            dw_ref[...] = jnp.zeros_like(dw_ref)

        db_ref[...] += jnp.sum(dz, axis=0, keepdims=True)
        dw_ref[...] += lax.dot_general(r_ref[...].astype(BF16), dzb, _TN, preferred_element_type=F32)

    half = pl.BlockSpec((TM, w2), lambda i: (i, 0))
    row = pl.BlockSpec((TM, D), lambda i: (i, 0))
    wspec = pl.BlockSpec((R_PAD, D), lambda i: (0, 0))
    return pl.pallas_call(
        body, name=name, grid=(t // TM,),
        in_specs=[half, half, row, half, half, half, row, half, row, _pcol("r", R_PAD), wspec],
        out_specs=(half, half, row, pl.BlockSpec((TM, R_PAD), lambda i: (i, 0)),
                   pl.BlockSpec((8, D), lambda i: (0, 0)), wspec),
        out_shape=(jax.ShapeDtypeStruct((t, w2), BF16), jax.ShapeDtypeStruct((t, w2), BF16),
                   jax.ShapeDtypeStruct((t, D), BF16), jax.ShapeDtypeStruct((t, R_PAD), BF16),
                   jax.ShapeDtypeStruct((8, D), F32), jax.ShapeDtypeStruct((R_PAD, D), F32)),
        compiler_params=_cparams(1, 40),
    )(*gf, *gb, z, p, wd)


def _branch_fwd(att, of, ob, p, gla_g, name):
    t = p.shape[0]

    def body(att_ref, of_ref, ob_ref, za_ref, zg_ref, g_ref, yb_ref, yc_ref):
        za = za_ref[...]
        yb_ref[...] = (att_ref[...] * (za * _sigmoid(za))).astype(BF16)
        for h in range(GH):
            sl = slice(h * GDV, (h + 1) * GDV)
            o = of_ref[:, sl] + ob_ref[:, sl]
            n = o * lax.rsqrt(jnp.mean(o * o, axis=-1, keepdims=True) + EPS) * g_ref[...]
            zh = zg_ref[:, sl]
            yc_ref[:, sl] = (n * (zh * _sigmoid(zh))).astype(BF16)

    row = pl.BlockSpec((TM, D), lambda i: (i, 0))
    return pl.pallas_call(
        body, name=name, grid=(t // TM,),
        in_specs=[row, row, row, _pcol("z_attn", D), _pcol("zg", D), pl.BlockSpec((1, GDV), lambda i: (0, 0))],
        out_specs=(row, row),
        out_shape=(jax.ShapeDtypeStruct((t, D), BF16), jax.ShapeDtypeStruct((t, D), BF16)),
        compiler_params=_cparams(1, 40),
    )(att, of, ob, p, p, gla_g)


def _branch_bwd(dyb, dyc, att, of, ob, p, gla_g, name):
    t = p.shape[0]

    def body(dyb_ref, dyc_ref, att_ref, of_ref, ob_ref, za_ref, zg_ref, g_ref, datt_ref, dza_ref, do_ref, dzg_ref, dg_ref):
        i = pl.program_id(0)

        @pl.when(i == 0)
        def _():
            dg_ref[...] = jnp.zeros_like(dg_ref)

        za, dyb = za_ref[...], dyb_ref[...]
        sa = _sigmoid(za)
        datt_ref[...] = dyb * (za * sa)
        dza_ref[...] = (dyb * att_ref[...] * (sa * (1.0 + za * (1.0 - sa)))).astype(BF16)
        g = g_ref[...]
        for h in range(GH):
            sl = slice(h * GDV, (h + 1) * GDV)
            o = of_ref[:, sl] + ob_ref[:, sl]
            r = lax.rsqrt(jnp.mean(o * o, axis=-1, keepdims=True) + EPS)
            oh = o * r
            zh, dyc = zg_ref[:, sl], dyc_ref[:, sl]
            sg = _sigmoid(zh)
            dn = dyc * (zh * sg)
            dzg_ref[:, sl] = (dyc * (oh * g) * (sg * (1.0 + zh * (1.0 - sg)))).astype(BF16)
            doh = dn * g
            do_ref[:, sl] = r * (doh - oh * jnp.mean(doh * oh, axis=-1, keepdims=True))
            dg_ref[...] += jnp.sum(dn * oh, axis=0, keepdims=True)

    row = pl.BlockSpec((TM, D), lambda i: (i, 0))
    return pl.pallas_call(
        body, name=name, grid=(t // TM,),
        in_specs=[row, row, row, row, row, _pcol("z_attn", D), _pcol("zg", D), pl.BlockSpec((1, GDV), lambda i: (0, 0))],
        out_specs=(row, row, row, row, pl.BlockSpec((8, GDV), lambda i: (0, 0))),
        out_shape=(jax.ShapeDtypeStruct((t, D), F32), jax.ShapeDtypeStruct((t, D), BF16),
                   jax.ShapeDtypeStruct((t, D), F32), jax.ShapeDtypeStruct((t, D), BF16),
                   jax.ShapeDtypeStruct((8, GDV), F32)),
        compiler_params=_cparams(1, 48),
    )(dyb, dyc, att, of, ob, p, p, gla_g)


def _merge_fwd(bra, brb, brc, p, b_gate, name):
    t = p.shape[0]
    mgb = OFF["mg"] // D

    def body(a_ref, b_ref, c_ref, ga_ref, gb_ref, gc_ref, bg_ref, m_ref):
        m_ref[...] = (_sigmoid(ga_ref[...] + bg_ref[:, 0:D]) * a_ref[...]
                      + _sigmoid(gb_ref[...] + bg_ref[:, D:2 * D]) * b_ref[...]
                      + _sigmoid(gc_ref[...] + bg_ref[:, 2 * D:3 * D]) * c_ref[...]).astype(BF16)

    row = pl.BlockSpec((TM, D), lambda i: (i, 0))
    gates = [pl.BlockSpec((TM, D), functools.partial(lambda i, b: (i, b), b=mgb + j)) for j in range(3)]
    return pl.pallas_call(
        body, name=name, grid=(t // TM,),
        in_specs=[row, row, row, *gates, pl.BlockSpec((1, 3 * D), lambda i: (0, 0))],
        out_specs=row, out_shape=jax.ShapeDtypeStruct((t, D), BF16), compiler_params=_cparams(1, 40),
    )(bra, brb, brc, p, p, p, b_gate)


def _merge_bwd(dm, bra, brb, brc, p, b_gate, name):
    t = p.shape[0]
    mgb = OFF["mg"] // D

    def body(dm_ref, a_ref, b_ref, c_ref, ga_ref, gb_ref, gc_ref, bg_ref, da_ref, db_ref, dc_ref, dmg_ref, dbg_ref):
        i = pl.program_id(0)

        @pl.when(i == 0)
        def _():
            dbg_ref[...] = jnp.zeros_like(dbg_ref)

        dm = dm_ref[...]
        for j, (br_ref, g_ref, d_ref) in enumerate(((a_ref, ga_ref, da_ref), (b_ref, gb_ref, db_ref), (c_ref, gc_ref, dc_ref))):
            sl = slice(j * D, (j + 1) * D)
            g = _sigmoid(g_ref[...] + bg_ref[:, sl])
            d_ref[...] = (dm * g).astype(BF16)
            dmg = dm * br_ref[...] * (g * (1.0 - g))
            dmg_ref[:, sl] = dmg.astype(BF16)
            dbg_ref[:, sl] += jnp.sum(dmg, axis=0, keepdims=True)

    row = pl.BlockSpec((TM, D), lambda i: (i, 0))
    gates = [pl.BlockSpec((TM, D), functools.partial(lambda i, b: (i, b), b=mgb + j)) for j in range(3)]
    return pl.pallas_call(
        body, name=name, grid=(t // TM,),
        in_specs=[row, row, row, row, *gates, pl.BlockSpec((1, 3 * D), lambda i: (0, 0))],
        out_specs=(row, row, row, pl.BlockSpec((TM, 3 * D), lambda i: (i, 0)), pl.BlockSpec((8, 3 * D), lambda i: (0, 0))),
        out_shape=(jax.ShapeDtypeStruct((t, D), BF16),) * 3 + (jax.ShapeDtypeStruct((t, 3 * D), BF16),
                                                                jax.ShapeDtypeStruct((8, 3 * D), F32)),
        compiler_params=_cparams(1, 48),
    )(dm, bra, brb, brc, p, p, p, b_gate)


def _adamw(gsrc, w, m, v, name):
    ns, nl, r, c = gsrc.shape
    gb = gsrc.dtype.itemsize

    def fits(rows, cols):
        lanes = -(-cols // LANE) * LANE
        return ns * rows * lanes * gb <= ADAM_SRC_BYTES and rows * lanes * 4 <= ADAM_ROW_BYTES

    tr, tc = r, c
    if not fits(r, c):
        rows = [cand for cand in range(16, r, 16) if r % cand == 0 and fits(cand, c)]
        cols = [cand for cand in range(LANE, c, LANE) if c % cand == 0 and fits(r, cand)]
        if rows:
            tr = rows[-1]
        else:
            tc = cols[-1]

    def body(g_ref, w_ref, m_ref, v_ref, go_ref, d_ref, mo_ref, vo_ref):
        g = g_ref[0].astype(F32)
        for s in range(1, ns):
            g = g + g_ref[s].astype(F32)
        mn = ADAM_B1 * m_ref[...] + (1.0 - ADAM_B1) * g
        vn = ADAM_B2 * v_ref[...] + (1.0 - ADAM_B2) * jnp.square(g)
        m_hat = mn / (1.0 - ADAM_B1 ** ADAM_STEP)
        v_hat = vn / (1.0 - ADAM_B2 ** ADAM_STEP)
        go_ref[...] = g
        d_ref[...] = -ADAM_LR * (m_hat / (jnp.sqrt(v_hat) + ADAM_EPS) + ADAM_WD * w_ref[...])
        mo_ref[...] = mn
        vo_ref[...] = vn

    row = pl.BlockSpec((None, tr, tc), lambda l, i, j: (l, i, j))
    return pl.pallas_call(
        body, name=name, grid=(nl, r // tr, c // tc),
        in_specs=[pl.BlockSpec((ns, None, tr, tc), lambda l, i, j: (0, l, i, j)), row, row, row],
        out_specs=(row,) * 4, out_shape=(jax.ShapeDtypeStruct((nl, r, c), F32),) * 4,
        compiler_params=_cparams(3, 48),
    )(gsrc, w, m, v)


def _slot_sum(gsrc, name):
    ns, r, _ = gsrc.shape

    def body(g_ref, o_ref):
        g = g_ref[0]
        for s in range(1, ns):
            g = g + g_ref[s]
        o_ref[...] = g

    return pl.pallas_call(body, name=name, out_shape=jax.ShapeDtypeStruct((r, LANE), F32))(gsrc)


def _pack(parts, dtype, row_align):
    chunk = row_align * LANE
    out, spans, o = [], [], 0
    for a in parts:
        f = a.reshape(-1).astype(dtype)
        n = f.shape[0]
        pad = (-n) % chunk
        if pad:
            f = jnp.concatenate([f, jnp.zeros((pad,), dtype)])
        out.append(f.reshape(-1, LANE))
        spans.append((o, n))
        o += (n + pad) // LANE
    return jnp.concatenate(out, axis=0), spans


def _unpack(packed, spans, shapes):
    res = []
    for (o, n), shp in zip(spans, shapes):
        rows = -(-n // LANE)
        res.append(packed[o:o + rows].reshape(-1)[:n].reshape(shp))
    return res


def _rope_tables(ctx, seq):
    n_rows = seq // GRID_W
    pairs = HD // 4
    row = jnp.repeat(jnp.arange(n_rows, dtype=F32), GRID_W)
    col = jnp.tile(jnp.arange(GRID_W, dtype=F32), n_rows)
    freqs = ROPE_THETA ** (-jnp.arange(pairs, dtype=F32) * 2.0 / (HD // 2))
    ar, ac = row[:, None] * freqs, col[:, None] * freqs
    cos_l = jnp.concatenate([jnp.cos(ar), jnp.cos(ar), jnp.cos(ac), jnp.cos(ac)], axis=1)
    sin_l = jnp.concatenate([-jnp.sin(ar), jnp.sin(ar), -jnp.sin(ac), jnp.sin(ac)], axis=1)
    cos_t = jnp.concatenate([jnp.ones((ctx, HD), F32), cos_l], axis=0)
    sin_t = jnp.concatenate([jnp.zeros((ctx, HD), F32), sin_l], axis=0)
    return cos_t, sin_t


def _to_proj_layout(wt):
    parts = [wt[s:s + wd] for _, s, wd in _SEGS]
    used = sum(wd for _, _, wd in _SEGS)
    parts.append(jnp.zeros((NP - used, wt.shape[1]), wt.dtype))
    return jnp.concatenate(parts, axis=0)


def _from_proj_layout(g):
    order = sorted(_SEGS, key=lambda sg: sg[1])
    return jnp.concatenate([g[OFF[n]:OFF[n] + wd] for n, _, wd in order], axis=0)


def _row0(a):
    return a[..., 0, :]


def kernel(x, c, ctx, c_ctx, w_ada, b_ada, g_pre, g_post, w_in, conv_w, q_norm_g, k_norm_g, w_decay_fwd, b_decay_fwd, w_decay_bwd, b_decay_bwd, gla_norm_g, w_br_conv, w_br_attn, w_br_gla, b_gate, w_out, loss_target, m_c_ctx, m_w_ada, m_b_ada, m_g_pre, m_g_post, m_w_in, m_conv_w, m_q_norm_g, m_k_norm_g, m_w_decay_fwd, m_b_decay_fwd, m_w_decay_bwd, m_b_decay_bwd, m_gla_norm_g, m_w_br_conv, m_w_br_attn, m_w_br_gla, m_b_gate, m_w_out, v_c_ctx, v_w_ada, v_b_ada, v_g_pre, v_g_post, v_w_in, v_conv_w, v_q_norm_g, v_k_norm_g, v_w_decay_fwd, v_b_decay_fwd, v_w_decay_bwd, v_b_decay_bwd, v_gla_norm_g, v_w_br_conv, v_w_br_attn, v_w_br_gla, v_b_gate, v_w_out):
    seq, n_ctx = x.shape[1], ctx.shape[1]
    assert n_ctx % TM == 0 and seq % TM == 0 and seq % GRID_W == 0
    t = n_ctx + seq
    nct, ncc = n_ctx // TM, n_ctx // CH
    dev = 4 * lax.axis_index("x") + 2 * lax.axis_index("y") + lax.axis_index("c")
    ada_w = w_ada.shape[2]
    in_w = w_in.shape[2]
    br_r = w_br_conv.shape[1]

    def in_t(a, l):
        return a.transpose(2, 0, 1)[:, l, :]

    wb = [w.astype(BF16) for w in (w_ada, w_br_conv, w_br_attn, w_br_gla, w_out)]
    wall = _all_gather([wb[0][0], in_t(w_in, 0).astype(BF16)], "gather_first")
    later = _GatherRider([in_t(w_in, 1).astype(BF16), wb[0][1], wb[1], wb[2], wb[3], wb[4]])
    small_parts = [conv_w, w_decay_fwd, w_decay_bwd]
    spack, sspans = _pack(small_parts, F32, 8)
    sall, = _all_gather([spack], "gather_small")

    def gathered(all_, spans, k, shp):
        o, n = spans[k]
        rows = n // LANE
        return all_[:, o:o + rows].reshape((NDEV,) + shp)

    def full_in(g):
        return _to_proj_layout(g.reshape(IN_WIDTH, D))

    def full_ada(g):
        return g.transpose(1, 0, 2).reshape(D, 3 * D)

    w_ada_f = [full_ada(wall[0]), None]
    wp = [full_in(wall[1]), None]
    conv_f = gathered(sall, sspans, 0, (DEPTH, 3, D // NDEV)).transpose(1, 2, 0, 3).reshape(DEPTH, 3, D)
    wdf_f = gathered(sall, sspans, 1, (DEPTH, GLA_RANK, GH * GDK // NDEV)).transpose(1, 2, 0, 3).reshape(DEPTH, GLA_RANK, GH * GDK)
    wdb_f = gathered(sall, sspans, 2, (DEPTH, GLA_RANK, GH * GDK // NDEV)).transpose(1, 2, 0, 3).reshape(DEPTH, GLA_RANK, GH * GDK)

    cos_t, sin_t = _rope_tables(n_ctx, seq)
    cc = jnp.concatenate([c_ctx[None, :], c.reshape(1, D), jnp.zeros((6, D), F32)], axis=0)
    silu_cc, dsilu_cc = _ada_in(cc)

    conv8, wd_pad, bd = [], [], []
    for l in range(DEPTH):
        conv8.append(jnp.concatenate([conv_f[l], jnp.zeros((5, D), F32)], axis=0))
        zr = jnp.zeros((GLA_RANK, GH * GDK), F32)
        wd_pad.append(jnp.concatenate([jnp.concatenate([wdf_f[l], zr], axis=1), jnp.concatenate([zr, wdb_f[l]], axis=1),
                                       jnp.zeros((R_PAD - 2 * GLA_RANK, D), F32)], axis=0))
        bd.append(jnp.concatenate([b_decay_fwd[l], b_decay_bwd[l]])[None, :])

    xs = jnp.concatenate([ctx[0], x[0]], axis=0)
    saved = []
    for l in range(DEPTH):
        n = f"l{l}_"
        mod = _mm(silu_cc, w_ada_f[l], n + "mod", bias=b_ada[l][None, :])
        mod3 = mod[0:2].reshape(2, 3, D)
        h = _prenorm_fwd(xs, g_pre[l][None, :], mod3, nct, n + "prenorm")
        p = _mm(h, wp[l], n + "proj", tb=True)
        cv, ya = _conv_fwd(p, conv8[l], nct, n + "conv")
        qr, kr = _qk_prep_fwd(p, q_norm_g[l][None, :], k_norm_g[l][None, :], cos_t, sin_t, n + "qk_prep")
        att, *got = _attn_fwd(qr, kr, p, nct, n + "attn", rider=later if l == 0 else None)
        if l == 0:
            wp[1], w_ada_f[1] = full_in(got[0]), full_ada(got[1])
            w_brs_f = [g.transpose(1, 0, 2, 3).reshape(DEPTH, D, D) for g in got[2:]]
        z, la = _decay_fwd(p, wd_pad[l], bd[l], n + "decay")
        of, stf = _gla_fwd(p, la, ncc, False, n + "gla_f")
        ob, stb = _gla_fwd(p, la, ncc, True, n + "gla_b")
        yb, yc = _branch_fwd(att, of, ob, p, gla_norm_g[l][None, :], n + "branch")
        bra = _mm(ya, w_brs_f[0][l], n + "br_conv")
        brb = _mm(yb, w_brs_f[1][l], n + "br_attn")
        brc = _mm(yc, w_brs_f[2][l], n + "br_gla")
        mm_ = _merge_fwd(bra, brb, brc, p, b_gate[l][None, :], n + "merge")
        out = _mm(mm_, w_brs_f[3][l], n + "out")
        x_new = _post_fwd(xs, out, g_post[l][None, :], mod3, nct, n + "post")
        saved.append(dict(x=xs, mod3=mod3, h=h, p=p, cv=cv, ya=ya, qr=qr, kr=kr, att=att, z=z, la=la, of=of, ob=ob,
                          stf=stf, stb=stb, yb=yb, yc=yc, bra=bra, brb=brb, brc=brc, m=mm_, out=out))
        xs = x_new

    dx, sq = _loss_grad(xs, loss_target[0], nct, "loss")
    loss = lax.psum(0.5 * sq[0, 0] / D, ("x", "y", "c"))

    gw = {k: [None] * DEPTH for k in ("w_in", "br_conv", "br_attn", "br_gla", "out", "b_gate", "g_pre", "g_post",
                                      "conv_w", "qg", "kg", "wd", "bdec", "gla_g", "dmod")}
    dctx = []

    def in_slots(l):
        return _from_proj_layout(gw["w_in"][l]).reshape(NDEV, in_w, D)

    def br_slots():
        return [jnp.stack([gw[k][l].reshape(NDEV, br_r, D) for l in range(DEPTH)], axis=1)
                for k in ("br_conv", "br_attn", "br_gla", "out")]

    for l in reversed(range(DEPTH)):
        n = f"l{l}_b_"
        s = saved[l]
        p = s["p"]
        d_out, dgt, gw["g_post"][l] = _post_bwd(dx, s["out"], g_post[l][None, :], s["mod3"], nct, n + "post")
        dm = _mm(d_out, w_brs_f[3][l], n + "dm", tb=True)
        gw["out"][l] = _mm(s["m"], d_out, n + "dw_out", ta=True, out_dtype=BF16)
        dbra, dbrb, dbrc, dmg, gw["b_gate"][l] = _merge_bwd(dm, s["bra"], s["brb"], s["brc"], p, b_gate[l][None, :], n + "merge")
        dya = _mm(dbra, w_brs_f[0][l], n + "dya", tb=True)
        dyb = _mm(dbrb, w_brs_f[1][l], n + "dyb", tb=True)
        dyc = _mm(dbrc, w_brs_f[2][l], n + "dyc", tb=True)
        gw["br_conv"][l] = _mm(s["ya"], dbra, n + "dw_conv", ta=True, out_dtype=BF16)
        gw["br_attn"][l] = _mm(s["yb"], dbrb, n + "dw_attn", ta=True, out_dtype=BF16)
        gw["br_gla"][l] = _mm(s["yc"], dbrc, n + "dw_gla", ta=True, out_dtype=BF16)
        dcv, dab, daz = _conv_bwd_a(dya, p, s["cv"], n + "conv_a")
        dac, dax, gw["conv_w"][l] = _conv_bwd_b(dcv, p, conv8[l], nct, n + "conv_b")
        datt, dza, dgo, dzg, gw["gla_g"][l] = _branch_bwd(dyb, dyc, s["att"], s["of"], s["ob"], p, gla_norm_g[l][None, :], n + "branch")
        ex1 = _ExchangeRider([in_slots(DEPTH - 1)] + br_slots()) if l == 0 else None
        dqr, dkr, dv, *got = _attn_bwd(s["qr"], s["kr"], p, s["att"], datt, nct, n + "attn", rider=ex1)
        if l == 0:
            recv_in1, recv_br = got[0], got[1:]
        dq, dk, gw["qg"][l], gw["kg"][l] = _qk_prep_bwd(dqr, dkr, p, q_norm_g[l][None, :], k_norm_g[l][None, :], cos_t, sin_t, n + "qk_prep")
        gf = _gla_bwd(p, s["la"], dgo, s["stf"], ncc, False, n + "gla_f")
        gb = _gla_bwd(p, s["la"], dgo, s["stb"], ncc, True, n + "gla_b")
        dgq, dgk, dgv, dr, gw["bdec"][l], gw["wd"][l] = _gla_merge_bwd(gf, gb, s["z"], p, wd_pad[l], n + "gla_merge")
        pieces = dict(a_b=dab, a_c=dac, a_x=dax, a_z=daz, q=dq, z_attn=dza, gv=dgv, zg=dzg, mg=dmg, gq=dgq, gk=dgk,
                      k=dk, v=dv.astype(BF16), r=dr)
        used = sum(wd_ for _, _, wd_ in _SEGS) - 32 + R_PAD
        dp = jnp.concatenate([pieces[nm] for nm, _, _ in _SEGS] + [jnp.zeros((t, NP - used), BF16)], axis=1)
        gw["w_in"][l] = _mm(dp, s["h"], n + "dw_in", ta=True, out_dtype=BF16)
        if l == 0:
            dh, recv_in0 = _mm(dp, wp[l], n + "dh", rider=_ExchangeRider([in_slots(0)]))
        else:
            dh = _mm(dp, wp[l], n + "dh")
        dx, dsh, dsc, gw["g_pre"][l] = _prenorm_bwd(dh, s["x"], dx, g_pre[l][None, :], s["mod3"], nct, n + "prenorm")
        dmod = jnp.stack([_row0(dsh), _row0(dsc), _row0(dgt)], axis=1).reshape(2, 3 * D)
        gw["dmod"][l] = dmod
        dmod8 = jnp.concatenate([dmod, jnp.zeros((6, 3 * D), F32)], axis=0)
        dctx.append(_mm(dmod8, w_ada_f[l], n + "dsilu", tb=True))
    grad_x = dx[n_ctx:][None]
    g_cctx = _cctx_grad(dctx[0], dctx[1], dsilu_cc)[0]

    def st2(name):
        return jnp.stack(gw[name])

    g_b_ada = jnp.stack([gw["dmod"][l][0] + gw["dmod"][l][1] for l in range(DEPTH)])
    g_bdf = jnp.stack([gw["bdec"][l][0, :GH * GDK] for l in range(DEPTH)])
    g_bdb = jnp.stack([gw["bdec"][l][0, GH * GDK:] for l in range(DEPTH)])
    g_wdf = jnp.stack([gw["wd"][l][0:GLA_RANK, :GH * GDK] for l in range(DEPTH)])
    g_wdb = jnp.stack([gw["wd"][l][GLA_RANK:2 * GLA_RANK, GH * GDK:] for l in range(DEPTH)])
    rep_names = ["c_ctx", "b_ada", "g_pre", "g_post", "q_norm_g", "k_norm_g", "b_decay_fwd", "b_decay_bwd", "gla_norm_g", "b_gate"]
    rep_grads = [g_cctx, g_b_ada, st2("g_pre")[:, 0], st2("g_post")[:, 0], st2("qg")[:, 0], st2("kg")[:, 0], g_bdf, g_bdb,
                 st2("gla_g")[:, 0], st2("b_gate")[:, 0]]
    rep_w = [c_ctx, b_ada, g_pre, g_post, q_norm_g, k_norm_g, b_decay_fwd, b_decay_bwd, gla_norm_g, b_gate]
    rep_m = [m_c_ctx, m_b_ada, m_g_pre, m_g_post, m_q_norm_g, m_k_norm_g, m_b_decay_fwd, m_b_decay_bwd, m_gla_norm_g, m_b_gate]
    rep_v = [v_c_ctx, v_b_ada, v_g_pre, v_g_post, v_q_norm_g, v_k_norm_g, v_b_decay_fwd, v_b_decay_bwd, v_gla_norm_g, v_b_gate]
    shard_grads = [st2("conv_w")[:, 0:3], g_wdf, g_wdb]
    extra = [silu_cc[0:2], jnp.stack(gw["dmod"])]
    gpack, gspans = _pack(rep_grads + shard_grads + extra, F32, 8)
    gall, = _all_gather([gpack], "gather_small_grads")
    n_rep = len(rep_grads)
    rep_rows = gspans[n_rep][0]
    shard_rows = gspans[n_rep + len(shard_grads)][0]

    wpk, rspans = _pack(rep_w, F32, 8)
    mpk, _ = _pack(rep_m, F32, 8)
    vpk, _ = _pack(rep_v, F32, 8)
    rep_out = _adamw(gall[:, None, :rep_rows], wpk[None], mpk[None], vpk[None], "adam_rep")
    rep_shapes = [a.shape for a in rep_w]
    rep_g, rep_d, rep_nm, rep_nv = [_unpack(o[0], rspans, rep_shapes) for o in rep_out]

    ssum = _slot_sum(gall[:, rep_rows:shard_rows], "sum_small_sharded")
    sh_spans = [(o - rep_rows, n_) for o, n_ in gspans[n_rep:n_rep + len(shard_grads)]]
    g_conv_full, g_wdf_full, g_wdb_full = _unpack(ssum, sh_spans, [(DEPTH, 3, D), (DEPTH, GLA_RANK, GH * GDK), (DEPTH, GLA_RANK, GH * GDK)])
    cw, dw = D // NDEV, GH * GDK // NDEV
    sh_g = [lax.dynamic_slice_in_dim(g_conv_full, dev * cw, cw, axis=2),
            lax.dynamic_slice_in_dim(g_wdf_full, dev * dw, dw, axis=2),
            lax.dynamic_slice_in_dim(g_wdb_full, dev * dw, dw, axis=2)]
    sgp, shs = _pack(sh_g, F32, 8)
    swp, _ = _pack([conv_w, w_decay_fwd, w_decay_bwd], F32, 8)
    smp, _ = _pack([m_conv_w, m_w_decay_fwd, m_w_decay_bwd], F32, 8)
    svp, _ = _pack([v_conv_w, v_w_decay_fwd, v_w_decay_bwd], F32, 8)
    sh_out = _adamw(sgp[None, None], swp[None], smp[None], svp[None], "adam_small_sharded")
    sh_shapes = [conv_w.shape, w_decay_fwd.shape, w_decay_bwd.shape]
    sh_gr, sh_d, sh_nm, sh_nv = [_unpack(o[0], shs, sh_shapes) for o in sh_out]

    eo, en = gspans[n_rep + len(shard_grads)]
    a_all = gall[:, eo:eo + en // LANE].reshape(NDEV * 2, D)
    eo2, en2 = gspans[n_rep + len(shard_grads) + 1]
    d_all = gall[:, eo2:eo2 + en2 // LANE].reshape(NDEV, DEPTH, 2, 3 * D).transpose(1, 0, 2, 3).reshape(DEPTH, NDEV * 2, 3 * D)
    g_ada = jnp.stack([_mm(a_all, lax.dynamic_slice_in_dim(d_all[l], dev * ada_w, ada_w, axis=1), f"dw_ada{l}",
                           ta=True, precise=True, tk=NDEV * 2) for l in range(DEPTH)])
    ada_g, ada_d, ada_nm, ada_nv = _adamw(g_ada[None], w_ada, m_w_ada, v_w_ada, "adam_ada")

    big_w = [w_br_conv, w_br_attn, w_br_gla, w_out]
    big_m = [m_w_br_conv, m_w_br_attn, m_w_br_gla, m_w_out]
    big_v = [v_w_br_conv, v_w_br_attn, v_w_br_gla, v_w_out]
    big_out = [_adamw(recv_br[j], big_w[j], big_m[j], big_v[j], f"adam_big{j}") for j in range(len(big_w))]
    in_out = [_adamw(r_[:, None], in_t(w_in, l)[None], in_t(m_w_in, l)[None], in_t(v_w_in, l)[None], f"adam_in{l}")
              for l, r_ in enumerate((recv_in0, recv_in1))]
    in_res = [jnp.stack([in_out[l][k][0] for l in range(DEPTH)], axis=1).transpose(1, 2, 0) for k in range(4)]
    big_g, big_d, big_nm, big_nv = [[in_res[k]] + [o[k] for o in big_out] for k in range(4)]

    def ordered(rep, ada, big, sh):
        c_ctx_, b_ada_, g_pre_, g_post_, qg_, kg_, bdf_, bdb_, glag_, bgate_ = rep
        w_in_, brc_, bra_, brg_, wout_ = big
        conv_, wdf_, wdb_ = sh
        return [c_ctx_, ada, b_ada_, g_pre_, g_post_, w_in_, conv_, qg_, kg_, wdf_, bdf_, wdb_, bdb_, glag_,
                brc_, bra_, brg_, bgate_, wout_]

    return (loss, grad_x,
            *ordered(rep_g, ada_g, big_g, sh_gr), *ordered(rep_d, ada_d, big_d, sh_d),
            *ordered(rep_nm, ada_nm, big_nm, sh_nm), *ordered(rep_nv, ada_nv, big_nv, sh_nv))
```

```python
import functools

import numpy as np
import jax
import jax.numpy as jnp
from jax import lax
from jax.experimental import pallas as pl
from jax.experimental.pallas import tpu as pltpu

F32, BF16 = jnp.float32, jnp.bfloat16
HIGHEST = lax.Precision.HIGHEST

D = 1024
DEPTH = 2
GRID_W = 64
NH, NKV, HD = 8, 2, 128
GROUP = NH // NKV
ROPE_THETA = 10000.0
ATTN_SCALE = HD ** -0.5
GH, GDK, GDV = 4, 128, 256
GLA_RANK = 16
GLA_TAU = 16.0
CH = 64
GLA_SCALE = GDK ** -0.5
EPS = 1e-6
NDEV = 8
LANE = 128
TM = 256
ATTN_HEADS_PER_STEP = 1
ATTN_KEY_CHUNK = 8192

ADAM_LR, ADAM_B1, ADAM_B2, ADAM_EPS, ADAM_WD, ADAM_STEP = 0.001, 0.9, 0.999, 1e-08, 0.01, 10

_SEGS = (("a_b", 0, 1024), ("a_c", 1024, 1024), ("a_x", 2048, 1024), ("a_z", 3072, 1024),
         ("q", 4096, 1024), ("z_attn", 5632, 1024), ("gv", 7680, 1024), ("zg", 8736, 1024),
         ("mg", 9760, 3072), ("gq", 6656, 512), ("gk", 7168, 512), ("k", 5120, 256), ("v", 5376, 256),
         ("r", 8704, 32))
IN_WIDTH = 12832
NP = 13312
OFF = {}
_o = 0
for _n, _s, _w in _SEGS:
    OFF[_n] = _o
    _o += _w
R_PAD = 128


def _cparams(ngrid, vmem_mb):
    return pltpu.CompilerParams(dimension_semantics=("arbitrary",) * ngrid, vmem_limit_bytes=vmem_mb << 20)


def _pick(n, cands):
    for c in cands:
        if n % c == 0:
            return c
    return n


def _sigmoid(x):
    return 1.0 / (1.0 + jnp.exp(-x))


ADAM_SRC_BYTES = 8 << 20
ADAM_ROW_BYTES = 1 << 20


def _all_gather(xs, name):
    return _comm_alone(_GatherRider(xs), name)


_HBM = pl.BlockSpec(memory_space=pl.ANY)


class _Rider:
    def __init__(self, xs, out_shapes):
        self.xs, self.n = list(xs), len(xs)
        self.out_shape = [jax.ShapeDtypeStruct(s, x.dtype) for s, x in zip(out_shapes, xs)]
        self.scratch = [pltpu.SemaphoreType.DMA((7 * self.n,)), pltpu.SemaphoreType.DMA((7 * self.n,)),
                        pltpu.SemaphoreType.DMA((self.n,))]


class _GatherRider(_Rider):
    def __init__(self, xs):
        super().__init__(xs, [(NDEV,) + x.shape for x in xs])

    def _parts(self, x_refs, out_refs, sems):
        n = self.n
        send_sems, recv_sems, local_sems = sems
        mx, my, mc = lax.axis_index("x"), lax.axis_index("y"), lax.axis_index("c")
        me, sibling = (mx, my, mc), (mx, my, 1 - mc)
        chips = [(1 - mx, my), (mx, 1 - my), (1 - mx, 1 - my)]

        def slot(a, px, py, pc):
            return out_refs[a].at[4 * px + 2 * py + pc]

        def copy(k, a, block, to, own=False):
            return pltpu.make_async_remote_copy(
                src_ref=x_refs[a] if own else slot(a, *block), dst_ref=slot(a, *block),
                send_sem=send_sems.at[k * n + a], recv_sem=recv_sems.at[k * n + a],
                device_id=to, device_id_type=pl.DeviceIdType.MESH)

        mine = [pltpu.make_async_copy(x_refs[a], slot(a, *me), local_sems.at[a]) for a in range(n)]
        first = [copy(0, a, me, sibling, own=True) for a in range(n)]
        first += [copy(1 + j, a, me, (*chip, mc), own=True) for a in range(n) for j, chip in enumerate(chips)]
        landed = [copy(1 + j, a, (*chip, mc), me) for a in range(n) for j, chip in enumerate(chips)]
        passed = [copy(4 + j, a, (*chip, mc), sibling) for a in range(n) for j, chip in enumerate(chips)]
        from_sibling = [copy(0, a, sibling, me) for a in range(n)]
        from_sibling += [copy(4 + j, a, (*chip, 1 - mc), me) for a in range(n) for j, chip in enumerate(chips)]
        return mine, first, landed, passed, from_sibling

    def start(self, x_refs, out_refs, sems):
        mine, first, _, _, _ = self._parts(x_refs, out_refs, sems)
        for cp in mine + first:
            cp.start()

    def middle(self, x_refs, out_refs, sems):
        _, _, landed, passed, _ = self._parts(x_refs, out_refs, sems)
        for got, fwd in zip(landed, passed):
            got.wait_recv()
            fwd.start()

    def finish(self, x_refs, out_refs, sems):
        mine, first, _, passed, from_sibling = self._parts(x_refs, out_refs, sems)
        for cp in from_sibling:
            cp.wait_recv()
        for cp in first + passed:
            cp.wait_send()
        for cp in mine:
            cp.wait()


class _ExchangeRider(_Rider):
    def __init__(self, xs):
        super().__init__(xs, [x.shape for x in xs])

    def _parts(self, x_refs, out_refs, sems):
        n = self.n
        send_sems, recv_sems, local_sems = sems
        mx, my, mc = lax.axis_index("x"), lax.axis_index("y"), lax.axis_index("c")
        me = 4 * mx + 2 * my + mc
        mine = [pltpu.make_async_copy(x_refs[a].at[me], out_refs[a].at[me], local_sems.at[a]) for a in range(n)]
        copies = []
        for a in range(n):
            for rel in range(1, NDEV):
                px = (1 - mx) if rel & 4 else mx
                py = (1 - my) if rel & 2 else my
                pc = (1 - mc) if rel & 1 else mc
                peer = 4 * px + 2 * py + pc
                k = (rel - 1) * n + a
                copies.append(pltpu.make_async_remote_copy(
                    src_ref=x_refs[a].at[peer], dst_ref=out_refs[a].at[me],
                    send_sem=send_sems.at[k], recv_sem=recv_sems.at[k],
                    device_id=(px, py, pc), device_id_type=pl.DeviceIdType.MESH))
        return mine, copies

    def start(self, x_refs, out_refs, sems):
        mine, copies = self._parts(x_refs, out_refs, sems)
        for cp in mine + copies:
            cp.start()

    def middle(self, x_refs, out_refs, sems):
        pass

    def finish(self, x_refs, out_refs, sems):
        mine, copies = self._parts(x_refs, out_refs, sems)
        for cp in copies:
            cp.wait_recv()
        for cp in copies:
            cp.wait_send()
        for cp in mine:
            cp.wait()


def _comm_alone(rider, name):
    n = rider.n

    def body(*refs):
        x_refs, out_refs, sems = refs[:n], refs[n:2 * n], refs[2 * n:]
        rider.start(x_refs, out_refs, sems)
        rider.middle(x_refs, out_refs, sems)
        rider.finish(x_refs, out_refs, sems)

    return pl.pallas_call(
        body, name=name, out_shape=tuple(rider.out_shape), in_specs=[_HBM] * n, out_specs=(_HBM,) * n,
        scratch_shapes=rider.scratch,
    )(*rider.xs)


def _with_rider(body, nin, nout, rider, first, mid, last):
    if rider is None:
        return body
    n = rider.n

    def wrapped(*refs):
        ins, x_refs = refs[:nin], refs[nin:nin + n]
        outs, out_refs = refs[nin + n:nin + n + nout], refs[nin + n + nout:nin + 2 * n + nout]
        scratch, sems = refs[nin + 2 * n + nout:-3], refs[-3:]

        @pl.when(first())
        def _():
            rider.start(x_refs, out_refs, sems)

        body(*ins, *outs, *scratch)

        @pl.when(mid())
        def _():
            rider.middle(x_refs, out_refs, sems)

        @pl.when(last())
        def _():
            rider.finish(x_refs, out_refs, sems)

    return wrapped


def _mm(a, b, name, ta=False, tb=False, out_dtype=F32, bias=None, precise=False, tm=None, tn=None, tk=None, rider=None):
    m, k = (a.shape[1], a.shape[0]) if ta else a.shape
    n = b.shape[0] if tb else b.shape[1]
    assert k == (b.shape[1] if tb else b.shape[0])
    tm = tm or _pick(m, (1088, 1024, 512, 256, 128))
    tn = tn or _pick(n, (1024, 512, 384, 256, 128))
    tk = tk or _pick(k, (1024, 1088, 512, 256, 128))
    nk = k // tk
    dn = (((0 if ta else 1,), (1 if tb else 0,)), ((), ()))

    def body(*refs):
        if bias is None:
            a_ref, b_ref, o_ref = refs[:3]
            bias_ref = None
        else:
            a_ref, b_ref, bias_ref, o_ref = refs[:4]
        x, y = a_ref[...], b_ref[...]
        if precise:
            p = lax.dot_general(x.astype(F32), y.astype(F32), dn, preferred_element_type=F32, precision=HIGHEST)
        else:
            p = lax.dot_general(x.astype(BF16), y.astype(BF16), dn, preferred_element_type=F32)

        def finish(acc):
            if bias_ref is not None:
                acc = acc + bias_ref[...]
            o_ref[...] = acc.astype(out_dtype)

        if nk == 1:
            finish(p)
        else:
            acc_ref = refs[-1]
            kk = pl.program_id(2)

            @pl.when(kk == 0)
            def _():
                acc_ref[...] = p

            @pl.when(kk > 0)
            def _():
                acc_ref[...] += p

            @pl.when(kk == nk - 1)
            def _():
                finish(acc_ref[...])

    a_spec = pl.BlockSpec((tk, tm), lambda i, j, kk: (kk, i)) if ta else pl.BlockSpec((tm, tk), lambda i, j, kk: (i, kk))
    b_spec = pl.BlockSpec((tn, tk), lambda i, j, kk: (j, kk)) if tb else pl.BlockSpec((tk, tn), lambda i, j, kk: (kk, j))
    in_specs = [a_spec, b_spec]
    args = [a, b]
    if bias is not None:
        in_specs.append(pl.BlockSpec((1, tn), lambda i, j, kk: (0, j)))
        args.append(bias)
    grid = (m // tm, n // tn, nk)
    out_spec = pl.BlockSpec((tm, tn), lambda i, j, kk: (i, j))
    scratch = [pltpu.VMEM((tm, tn), F32)] if nk > 1 else []
    if rider is None:
        return pl.pallas_call(
            body, name=name, grid=grid, in_specs=in_specs, out_specs=out_spec,
            out_shape=jax.ShapeDtypeStruct((m, n), out_dtype), scratch_shapes=scratch, compiler_params=_cparams(3, 56),
        )(*args)

    def at(step):
        return lambda: ((pl.program_id(0) == step[0]) & (pl.program_id(1) == step[1]) & (pl.program_id(2) == step[2]))

    end = tuple(g - 1 for g in grid)
    return pl.pallas_call(
        _with_rider(body, len(args), 1, rider, at((0, 0, 0)), at((grid[0] // 2, 0, 0)), at(end)),
        name=name, grid=grid, in_specs=in_specs + [_HBM] * rider.n, out_specs=(out_spec,) + (_HBM,) * rider.n,
        out_shape=(jax.ShapeDtypeStruct((m, n), out_dtype),) + tuple(rider.out_shape),
        scratch_shapes=scratch + rider.scratch, compiler_params=_cparams(3, 56),
    )(*args, *rider.xs)


def _ada_in(cc):
    def body(c_ref, s_ref, d_ref):
        x = c_ref[...]
        sg = _sigmoid(x)
        s_ref[...] = x * sg
        d_ref[...] = sg * (1.0 + x * (1.0 - sg))

    return pl.pallas_call(body, name="ada_in", out_shape=(jax.ShapeDtypeStruct(cc.shape, F32),) * 2)(cc)


def _cctx_grad(t0, t1, dsilu):
    def body(a_ref, b_ref, d_ref, o_ref):
        o_ref[...] = (a_ref[...] + b_ref[...]) * d_ref[...]

    return pl.pallas_call(body, name="cctx_grad", out_shape=jax.ShapeDtypeStruct(t0.shape, F32))(t0, t1, dsilu)


def _seg_spec(nct, rows=3):
    return pl.BlockSpec((None, rows, D), lambda i: (jnp.where(i >= nct, 1, 0), 0, 0))


def _prenorm_fwd(x, g_pre, mod3, nct, name):
    t = x.shape[0]

    def body(x_ref, g_ref, mod_ref, h_ref):
        xv = x_ref[...]
        r = lax.rsqrt(jnp.mean(xv * xv, axis=-1, keepdims=True) + EPS)
        y = xv * r * g_ref[...]
        h_ref[...] = (y * (1.0 + mod_ref[1:2, :]) + mod_ref[0:1, :]).astype(BF16)

    return pl.pallas_call(
        body, name=name, grid=(t // TM,),
        in_specs=[pl.BlockSpec((TM, D), lambda i: (i, 0)), pl.BlockSpec((1, D), lambda i: (0, 0)), _seg_spec(nct)],
        out_specs=pl.BlockSpec((TM, D), lambda i: (i, 0)),
        out_shape=jax.ShapeDtypeStruct((t, D), BF16), compiler_params=_cparams(1, 32),
    )(x, g_pre, mod3)


def _prenorm_bwd(dh, x, dxo, g_pre, mod3, nct, name):
    t = x.shape[0]

    def body(dh_ref, x_ref, dxo_ref, g_ref, mod_ref, dx_ref, dsh_ref, dsc_ref, dg_ref):
        i = pl.program_id(0)
        xv, dhv, g = x_ref[...], dh_ref[...], g_ref[...]
        r = lax.rsqrt(jnp.mean(xv * xv, axis=-1, keepdims=True) + EPS)
        xh = xv * r
        dy = dhv * (1.0 + mod_ref[1:2, :])
        dxh = dy * g
        dx_ref[...] = dxo_ref[...] + r * (dxh - xh * jnp.mean(dxh * xh, axis=-1, keepdims=True))

        @pl.when((i == 0) | (i == nct))
        def _():
            dsh_ref[...] = jnp.zeros_like(dsh_ref)
            dsc_ref[...] = jnp.zeros_like(dsc_ref)

        @pl.when(i == 0)
        def _():
            dg_ref[...] = jnp.zeros_like(dg_ref)

        dsh_ref[...] += jnp.sum(dhv, axis=0, keepdims=True)
        dsc_ref[...] += jnp.sum(dhv * (xh * g), axis=0, keepdims=True)
        dg_ref[...] += jnp.sum(dy * xh, axis=0, keepdims=True)

    row = pl.BlockSpec((TM, D), lambda i: (i, 0))
    seg8 = pl.BlockSpec((None, 8, D), lambda i: (jnp.where(i >= nct, 1, 0), 0, 0))
    return pl.pallas_call(
        body, name=name, grid=(t // TM,),
        in_specs=[row, row, row, pl.BlockSpec((1, D), lambda i: (0, 0)), _seg_spec(nct)],
        out_specs=(row, seg8, seg8, pl.BlockSpec((8, D), lambda i: (0, 0))),
        out_shape=(jax.ShapeDtypeStruct((t, D), F32), jax.ShapeDtypeStruct((2, 8, D), F32),
                   jax.ShapeDtypeStruct((2, 8, D), F32), jax.ShapeDtypeStruct((8, D), F32)),
        compiler_params=_cparams(1, 32),
    )(dh, x, dxo, g_pre, mod3)


def _post_fwd(x, out, g_post, mod3, nct, name):
    t = x.shape[0]

    def body(x_ref, o_ref, g_ref, mod_ref, y_ref):
        ov = o_ref[...]
        r = lax.rsqrt(jnp.mean(ov * ov, axis=-1, keepdims=True) + EPS)
        y_ref[...] = x_ref[...] + mod_ref[2:3, :] * (ov * r * g_ref[...])

    row = pl.BlockSpec((TM, D), lambda i: (i, 0))
    return pl.pallas_call(
        body, name=name, grid=(t // TM,),
        in_specs=[row, row, pl.BlockSpec((1, D), lambda i: (0, 0)), _seg_spec(nct)],
        out_specs=row, out_shape=jax.ShapeDtypeStruct((t, D), F32), compiler_params=_cparams(1, 32),
    )(x, out, g_post, mod3)


def _post_bwd(dxo, out, g_post, mod3, nct, name):
    t = out.shape[0]

    def body(dx_ref, o_ref, g_ref, mod_ref, do_ref, dgt_ref, dg_ref):
        i = pl.program_id(0)
        ov, dxv, g = o_ref[...], dx_ref[...], g_ref[...]
        r = lax.rsqrt(jnp.mean(ov * ov, axis=-1, keepdims=True) + EPS)
        nh = ov * r
        dn = dxv * mod_ref[2:3, :]
        dnh = dn * g
        do_ref[...] = (r * (dnh - nh * jnp.mean(dnh * nh, axis=-1, keepdims=True))).astype(BF16)

        @pl.when((i == 0) | (i == nct))
        def _():
            dgt_ref[...] = jnp.zeros_like(dgt_ref)

        @pl.when(i == 0)
        def _():
            dg_ref[...] = jnp.zeros_like(dg_ref)

        dgt_ref[...] += jnp.sum(dxv * (nh * g), axis=0, keepdims=True)
        dg_ref[...] += jnp.sum(dn * nh, axis=0, keepdims=True)

    row = pl.BlockSpec((TM, D), lambda i: (i, 0))
    seg8 = pl.BlockSpec((None, 8, D), lambda i: (jnp.where(i >= nct, 1, 0), 0, 0))
    return pl.pallas_call(
        body, name=name, grid=(t // TM,),
        in_specs=[row, row, pl.BlockSpec((1, D), lambda i: (0, 0)), _seg_spec(nct)],
        out_specs=(row, seg8, pl.BlockSpec((8, D), lambda i: (0, 0))),
        out_shape=(jax.ShapeDtypeStruct((t, D), BF16), jax.ShapeDtypeStruct((2, 8, D), F32),
                   jax.ShapeDtypeStruct((8, D), F32)),
        compiler_params=_cparams(1, 32),
    )(dxo, out, g_post, mod3)


def _loss_grad(y, target, nct, name):
    t = y.shape[0]

    def body(y_ref, t_ref, dy_ref, l_ref):
        i = pl.program_id(0)

        @pl.when(i == 0)
        def _():
            l_ref[...] = jnp.zeros_like(l_ref)

        @pl.when(i < nct)
        def _():
            dy_ref[...] = jnp.zeros_like(dy_ref)

        @pl.when(i >= nct)
        def _():
            err = y_ref[...] - t_ref[...]
            dy_ref[...] = err / D
            l_ref[...] += jnp.sum(jnp.sum(err * err, axis=1, keepdims=True), axis=0, keepdims=True)

    row = pl.BlockSpec((TM, D), lambda i: (i, 0))
    return pl.pallas_call(
        body, name=name, grid=(t // TM,),
        in_specs=[row, pl.BlockSpec((TM, D), lambda i: (jnp.maximum(i - nct, 0), 0))],
        out_specs=(row, pl.BlockSpec((8, LANE), lambda i: (0, 0))),
        out_shape=(jax.ShapeDtypeStruct((t, D), F32), jax.ShapeDtypeStruct((8, LANE), F32)),
        compiler_params=_cparams(1, 32),
    )(y, target)


def _pcol(name, width):
    assert OFF[name] % width == 0
    blk = OFF[name] // width
    return pl.BlockSpec((TM, width), lambda i: (i, blk))


def _shift_rows(u, prev_row, next_row):
    n = u.shape[0]
    row = lax.broadcasted_iota(jnp.int32, u.shape, 0)
    prev = jnp.where(row == 0, prev_row, pltpu.roll(u, 1, 0))
    nxt = jnp.where(row == n - 1, next_row, pltpu.roll(u, n - 1, 0))
    return prev, nxt


def _halo_specs(width, nt, blk=0):
    r8 = TM // 8
    prev = pl.BlockSpec((8, width), lambda i: (jnp.maximum(i * r8 - 1, 0), blk))
    nxt = pl.BlockSpec((8, width), lambda i: (jnp.minimum((i + 1) * r8, nt * r8 - 1), blk))
    return prev, nxt


def _conv_fwd(p, conv_w8, nct, name):
    t = p.shape[0]
    nt = t // TM

    def body(ab_ref, ac_ref, ax_ref, az_ref, acp_ref, axp_ref, acn_ref, axn_ref, w_ref, cv_ref, ya_ref):
        i = pl.program_id(0)
        u = ac_ref[...] * ax_ref[...]
        mp = jnp.where((i == 0) | (i == nct), 0.0, 1.0)
        mn = jnp.where((i == nct - 1) | (i == nt - 1), 0.0, 1.0)
        prev, nxt = _shift_rows(u, acp_ref[7:8, :] * axp_ref[7:8, :] * mp, acn_ref[0:1, :] * axn_ref[0:1, :] * mn)
        cv = w_ref[0:1, :] * prev + w_ref[1:2, :] * u + w_ref[2:3, :] * nxt
        az = az_ref[...]
        cv_ref[...] = cv
        ya_ref[...] = (ab_ref[...] * cv * (az * _sigmoid(az))).astype(BF16)

    acp, acn = _halo_specs(D, nt, OFF["a_c"] // D)
    axp, axn = _halo_specs(D, nt, OFF["a_x"] // D)
    row = pl.BlockSpec((TM, D), lambda i: (i, 0))
    return pl.pallas_call(
        body, name=name, grid=(nt,),
        in_specs=[_pcol("a_b", D), _pcol("a_c", D), _pcol("a_x", D), _pcol("a_z", D), acp, axp, acn, axn,
                  pl.BlockSpec((8, D), lambda i: (0, 0))],
        out_specs=(row, row),
        out_shape=(jax.ShapeDtypeStruct((t, D), F32), jax.ShapeDtypeStruct((t, D), BF16)),
        compiler_params=_cparams(1, 40),
    )(p, p, p, p, p, p, p, p, conv_w8)


def _conv_bwd_a(dya, p, cv, name):
    t = p.shape[0]

    def body(dy_ref, ab_ref, az_ref, cv_ref, dcv_ref, dab_ref, daz_ref):
        dy, ab, az, c = dy_ref[...], ab_ref[...], az_ref[...], cv_ref[...]
        sg = _sigmoid(az)
        sz = az * sg
        dcv_ref[...] = dy * ab * sz
        dab_ref[...] = (dy * c * sz).astype(BF16)
        daz_ref[...] = (dy * ab * c * (sg * (1.0 + az * (1.0 - sg)))).astype(BF16)

    row = pl.BlockSpec((TM, D), lambda i: (i, 0))
    return pl.pallas_call(
        body, name=name, grid=(t // TM,),
        in_specs=[row, _pcol("a_b", D), _pcol("a_z", D), row], out_specs=(row, row, row),
        out_shape=(jax.ShapeDtypeStruct((t, D), F32), jax.ShapeDtypeStruct((t, D), BF16),
                   jax.ShapeDtypeStruct((t, D), BF16)),
        compiler_params=_cparams(1, 40),
    )(dya, p, p, cv)


def _conv_bwd_b(dcv, p, conv_w8, nct, name):
    t = p.shape[0]
    nt = t // TM

    def body(dcv_ref, dp_ref, dn_ref, ac_ref, ax_ref, w_ref, dac_ref, dax_ref, dw_ref):
        i = pl.program_id(0)
        d, ac, ax = dcv_ref[...], ac_ref[...], ax_ref[...]
        u = ac * ax
        mp = jnp.where((i == 0) | (i == nct), 0.0, 1.0)
        mn = jnp.where((i == nct - 1) | (i == nt - 1), 0.0, 1.0)
        dprev, dnxt = _shift_rows(d, dp_ref[7:8, :] * mp, dn_ref[0:1, :] * mn)
        du = w_ref[0:1, :] * dnxt + w_ref[1:2, :] * d + w_ref[2:3, :] * dprev
        dac_ref[...] = (du * ax).astype(BF16)
        dax_ref[...] = (du * ac).astype(BF16)

        @pl.when(i == 0)
        def _():
            dw_ref[...] = jnp.zeros_like(dw_ref)

        dw0 = jnp.sum(u * dnxt, axis=0, keepdims=True)
        dw1 = jnp.sum(u * d, axis=0, keepdims=True)
        dw2 = jnp.sum(u * dprev, axis=0, keepdims=True)
        r8 = lax.broadcasted_iota(jnp.int32, (8, D), 0)
        dw_ref[...] += jnp.where(r8 == 0, dw0, jnp.where(r8 == 1, dw1, jnp.where(r8 == 2, dw2, 0.0)))

    dp, dn = _halo_specs(D, nt)
    row = pl.BlockSpec((TM, D), lambda i: (i, 0))
    return pl.pallas_call(
        body, name=name, grid=(nt,),
        in_specs=[row, dp, dn, _pcol("a_c", D), _pcol("a_x", D), pl.BlockSpec((8, D), lambda i: (0, 0))],
        out_specs=(row, row, pl.BlockSpec((8, D), lambda i: (0, 0))),
        out_shape=(jax.ShapeDtypeStruct((t, D), BF16), jax.ShapeDtypeStruct((t, D), BF16),
                   jax.ShapeDtypeStruct((8, D), F32)),
        compiler_params=_cparams(1, 40),
    )(dcv, dcv, dcv, p, p, conv_w8)


def _rot_half(x):
    lane = lax.broadcasted_iota(jnp.int32, x.shape, 1)
    return jnp.where((lane % 64) < 32, pltpu.roll(x, 96, 1), pltpu.roll(x, 32, 1))


def _qk_prep_fwd(p, qg, kg, cos_t, sin_t, name):
    t = p.shape[0]

    def body(q_ref, k_ref, qg_ref, kg_ref, c_ref, s_ref, qo_ref, ko_ref):
        c, s = c_ref[...], s_ref[...]

        def one(xv, g):
            y = xv * lax.rsqrt(jnp.mean(xv * xv, axis=-1, keepdims=True) + EPS) * g
            return (y * c + _rot_half(y) * s).astype(BF16)

        for h in range(NH):
            qo_ref[:, h * HD:(h + 1) * HD] = one(q_ref[:, h * HD:(h + 1) * HD], qg_ref[...])
        for h in range(NKV):
            ko_ref[:, h * HD:(h + 1) * HD] = one(k_ref[:, h * HD:(h + 1) * HD], kg_ref[...])

    vec = pl.BlockSpec((1, HD), lambda i: (0, 0))
    tab = pl.BlockSpec((TM, HD), lambda i: (i, 0))
    return pl.pallas_call(
        body, name=name, grid=(t // TM,),
        in_specs=[_pcol("q", NH * HD), _pcol("k", NKV * HD), vec, vec, tab, tab],
        out_specs=(pl.BlockSpec((TM, NH * HD), lambda i: (i, 0)), pl.BlockSpec((TM, NKV * HD), lambda i: (i, 0))),
        out_shape=(jax.ShapeDtypeStruct((t, NH * HD), BF16), jax.ShapeDtypeStruct((t, NKV * HD), BF16)),
        compiler_params=_cparams(1, 32),
    )(p, p, qg, kg, cos_t, sin_t)


def _qk_prep_bwd(dqr, dkr, p, qg, kg, cos_t, sin_t, name):
    t = p.shape[0]

    def body(dq_ref, dk_ref, q_ref, k_ref, qg_ref, kg_ref, c_ref, s_ref, dqo_ref, dko_ref, dqg_ref, dkg_ref):
        i = pl.program_id(0)
        c, s = c_ref[...], s_ref[...]

        @pl.when(i == 0)
        def _():
            dqg_ref[...] = jnp.zeros_like(dqg_ref)
            dkg_ref[...] = jnp.zeros_like(dkg_ref)

        def one(dyr, xv, g):
            dy = dyr * c + _rot_half(dyr * s)
            r = lax.rsqrt(jnp.mean(xv * xv, axis=-1, keepdims=True) + EPS)
            xh = xv * r
            dxh = dy * g
            dx = r * (dxh - xh * jnp.mean(dxh * xh, axis=-1, keepdims=True))
            return dx.astype(BF16), jnp.sum(dy * xh, axis=0, keepdims=True)

        for h in range(NH):
            sl = slice(h * HD, (h + 1) * HD)
            dx, dg = one(dq_ref[:, sl], q_ref[:, sl], qg_ref[...])
            dqo_ref[:, sl] = dx
            dqg_ref[...] += dg
        for h in range(NKV):
            sl = slice(h * HD, (h + 1) * HD)
            dx, dg = one(dk_ref[:, sl], k_ref[:, sl], kg_ref[...])
            dko_ref[:, sl] = dx
            dkg_ref[...] += dg

    vec = pl.BlockSpec((1, HD), lambda i: (0, 0))
    tab = pl.BlockSpec((TM, HD), lambda i: (i, 0))
    acc = pl.BlockSpec((8, HD), lambda i: (0, 0))
    qrow = pl.BlockSpec((TM, NH * HD), lambda i: (i, 0))
    krow = pl.BlockSpec((TM, NKV * HD), lambda i: (i, 0))
    return pl.pallas_call(
        body, name=name, grid=(t // TM,),
        in_specs=[qrow, krow, _pcol("q", NH * HD), _pcol("k", NKV * HD), vec, vec, tab, tab],
        out_specs=(qrow, krow, acc, acc),
        out_shape=(jax.ShapeDtypeStruct((t, NH * HD), BF16), jax.ShapeDtypeStruct((t, NKV * HD), BF16),
                   jax.ShapeDtypeStruct((8, HD), F32), jax.ShapeDtypeStruct((8, HD), F32)),
        compiler_params=_cparams(1, 32),
    )(dqr, dkr, p, p, qg, kg, cos_t, sin_t)


def _key_chunks(n):
    c = max(c for c in range(LANE, min(n, ATTN_KEY_CHUNK) + 1, LANE) if n % c == 0)
    return [(lo, lo + c) for lo in range(0, n, c)]


def _attn_fwd(qr, kr, p, nct, name, rider=None):
    t = qr.shape[0]
    nt = t // TM
    ctx = nct * TM
    vblk = OFF["v"] // HD
    hps = ATTN_HEADS_PER_STEP
    nhp, per_kv = NH // hps, GROUP // hps

    def body(q_ref, k_ref, v_ref, o_ref, lse_ref):
        def tile(nkeys):
            for j in range(hps):
                sl = slice(j * HD, (j + 1) * HD)
                q = q_ref[:, sl]
                m = l = acc = None
                for lo, hi in _key_chunks(nkeys):
                    s = lax.dot_general(q, k_ref[lo:hi, :], _NT, preferred_element_type=F32) * ATTN_SCALE
                    mc = jnp.max(s, axis=-1, keepdims=True)
                    m_new = mc if m is None else jnp.maximum(m, mc)
                    e = jnp.exp(s - m_new)
                    lc = jnp.sum(e, axis=-1, keepdims=True)
                    pv = jnp.dot(e.astype(BF16), v_ref[lo:hi, :].astype(BF16), preferred_element_type=F32)
                    if m is None:
                        l, acc = lc, pv
                    else:
                        alpha = jnp.exp(m - m_new)
                        l, acc = l * alpha + lc, acc * alpha + pv
                    m = m_new
                o_ref[:, sl] = acc / l
                lse_ref[:, j:j + 1] = m + jnp.log(l)

        pl.when(pl.program_id(1) < nct)(lambda: tile(ctx))
        pl.when(pl.program_id(1) >= nct)(lambda: tile(t))

    def at(h, i):
        return lambda: (pl.program_id(0) == h) & (pl.program_id(1) == i)

    rn = 0 if rider is None else rider.n
    qspec = pl.BlockSpec((TM, hps * HD), lambda h, i: (i, h))
    return pl.pallas_call(
        _with_rider(body, 3, 2, rider, at(0, 0), at(nhp * 3 // 4, 0), at(nhp - 1, nt - 1)),
        name=name, grid=(nhp, nt),
        in_specs=[qspec, pl.BlockSpec((t, HD), lambda h, i: (0, h // per_kv)),
                  pl.BlockSpec((t, HD), lambda h, i: (0, vblk + h // per_kv))] + [_HBM] * rn,
        out_specs=(qspec, pl.BlockSpec((None, TM, hps), lambda h, i: (h, i, 0))) + (_HBM,) * rn,
        out_shape=(jax.ShapeDtypeStruct((t, NH * HD), F32), jax.ShapeDtypeStruct((nhp, t, hps), F32))
        + (() if rider is None else tuple(rider.out_shape)),
        scratch_shapes=[] if rider is None else rider.scratch,
        compiler_params=_cparams(2, 48),
    )(qr, kr, p, *(() if rider is None else rider.xs))


def _attn_bwd(qr, kr, p, o, lse, do, nct, name, rider=None):
    t = qr.shape[0]
    nt = t // TM
    ctx = nct * TM
    vblk = OFF["v"] // HD
    hps = ATTN_HEADS_PER_STEP

    def body(q_ref, k_ref, v_ref, o_ref, lse_ref, do_ref, dq_ref, dk_ref, dv_ref):
        g, i = pl.program_id(1), pl.program_id(2)

        @pl.when((g == 0) & (i == 0))
        def _():
            dk_ref[...] = jnp.zeros_like(dk_ref)
            dv_ref[...] = jnp.zeros_like(dv_ref)

        def tile(nkeys):
            heads = []
            for j in range(hps):
                sl = slice(j * HD, (j + 1) * HD)
                dov = do_ref[:, sl]
                drow = jnp.sum(dov * o_ref[:, sl], axis=-1, keepdims=True)
                heads.append((sl, q_ref[:, sl], dov.astype(BF16), drow, lse_ref[:, j:j + 1]))
            dq = [None] * hps
            for lo, hi in _key_chunks(nkeys):
                k = k_ref[lo:hi, :]
                vb = v_ref[lo:hi, :].astype(BF16)
                dk_c = dv_c = None
                for j, (sl, q, dob, drow, lse_j) in enumerate(heads):
                    s = lax.dot_general(q, k, _NT, preferred_element_type=F32) * ATTN_SCALE
                    pr = jnp.exp(s - lse_j)
                    dp = lax.dot_general(dob, vb, _NT, preferred_element_type=F32)
                    ds = (pr * (dp - drow) * ATTN_SCALE).astype(BF16)
                    dq_c = jnp.dot(ds, k, preferred_element_type=F32)
                    dq[j] = dq_c if dq[j] is None else dq[j] + dq_c
                    dk_j = lax.dot_general(ds, q, _TN, preferred_element_type=F32)
                    dv_j = lax.dot_general(pr.astype(BF16), dob, _TN, preferred_element_type=F32)
                    dk_c = dk_j if dk_c is None else dk_c + dk_j
                    dv_c = dv_j if dv_c is None else dv_c + dv_j
                dk_ref[lo:hi, :] += dk_c
                dv_ref[lo:hi, :] += dv_c
            for j, (sl, *_) in enumerate(heads):
                dq_ref[:, sl] = dq[j]

        pl.when(i < nct)(lambda: tile(ctx))
        pl.when(i >= nct)(lambda: tile(t))

    def at(kv, g, i):
        return lambda: (pl.program_id(0) == kv) & (pl.program_id(1) == g) & (pl.program_id(2) == i)

    rn = 0 if rider is None else rider.n
    per_kv = GROUP // hps
    qspec = pl.BlockSpec((TM, hps * HD), lambda kv, g, i: (i, kv * per_kv + g))
    kvspec = pl.BlockSpec((t, HD), lambda kv, g, i: (0, kv))
    lspec = pl.BlockSpec((None, TM, hps), lambda kv, g, i: (kv * per_kv + g, i, 0))
    return pl.pallas_call(
        _with_rider(body, 6, 3, rider, at(0, 0, 0), at(NKV - 1, 0, 0), at(NKV - 1, per_kv - 1, nt - 1)),
        name=name, grid=(NKV, per_kv, nt),
        in_specs=[qspec, kvspec, pl.BlockSpec((t, HD), lambda kv, g, i: (0, vblk + kv)), qspec, lspec, qspec]
        + [_HBM] * rn,
        out_specs=(qspec, kvspec, kvspec) + (_HBM,) * rn,
        out_shape=(jax.ShapeDtypeStruct((t, NH * HD), F32), jax.ShapeDtypeStruct((t, NKV * HD), F32),
                   jax.ShapeDtypeStruct((t, NKV * HD), F32)) + (() if rider is None else tuple(rider.out_shape)),
        scratch_shapes=[] if rider is None else rider.scratch,
        compiler_params=_cparams(3, 48),
    )(qr, kr, p, o, lse, do, *(() if rider is None else rider.xs))


def _decay_fwd(p, wd, bd, name):
    t = p.shape[0]

    def body(r_ref, w_ref, b_ref, z_ref, la_ref):
        z = jnp.dot(r_ref[...].astype(BF16), w_ref[...].astype(BF16), preferred_element_type=F32) + b_ref[...]
        z_ref[...] = z
        la_ref[...] = (jnp.minimum(z, 0.0) - jnp.log(1.0 + jnp.exp(-jnp.abs(z)))) / GLA_TAU

    row = pl.BlockSpec((TM, D), lambda i: (i, 0))
    return pl.pallas_call(
        body, name=name, grid=(t // TM,),
        in_specs=[_pcol("r", R_PAD), pl.BlockSpec((R_PAD, D), lambda i: (0, 0)), pl.BlockSpec((1, D), lambda i: (0, 0))],
        out_specs=(row, row),
        out_shape=(jax.ShapeDtypeStruct((t, D), F32), jax.ShapeDtypeStruct((t, D), F32)),
        compiler_params=_cparams(1, 32),
    )(p, wd, bd)


def _chunk_order(s, ncc, nc, rev):
    if not rev:
        return s
    return jnp.where(s < ncc, ncc - 1 - s, nc - 1 - (s - ncc))


def _gla_chains(dirs):
    return [(rev, d, h) + tuple(refs) for d, (rev, *refs) in enumerate(dirs) for h in range(GH)]


def _hk(h):
    return slice(h * GDK, (h + 1) * GDK)


def _hv(h):
    return slice(h * GDV, (h + 1) * GDV)


def _gla_factors(qs, ks, las, revs):
    r = lax.broadcasted_iota(jnp.int32, (CH, CH), 0)
    c = lax.broadcasted_iota(jnp.int32, (CH, CH), 1)
    keeps = [(c >= r) if rev else (c <= r) for rev in revs]
    bcs = [jnp.dot(keep.astype(F32), la, preferred_element_type=F32, precision=HIGHEST) for keep, la in zip(keeps, las)]
    bls = [jnp.sum(la, axis=0, keepdims=True) for la in las]
    qts = [q * GLA_SCALE * jnp.exp(bc) for q, bc in zip(qs, bcs)]
    kts = [k * jnp.exp(-bc) for k, bc in zip(ks, bcs)]
    khs = [k * jnp.exp(bl - bc) for k, bl, bc in zip(ks, bls, bcs)]
    gls = [jnp.exp(bl) for bl in bls]
    return qts, kts, gls, khs, keeps, bcs


_NT = (((1,), (1,)), ((), ()))
_TN = (((0,), (0,)), ((), ()))


def _gla_specs(ncc, nc, rev, backward):
    def idx(s):
        return _chunk_order((nc - 1 - s) if backward else s, ncc, nc, rev)

    wk, wv = GH * GDK, GH * GDV
    qb, kb, vb = OFF["gq"] // wk, OFF["gk"] // wk, OFF["gv"] // wv
    lab = 1 if rev else 0
    q = pl.BlockSpec((CH, wk), lambda s: (idx(s), qb))
    k = pl.BlockSpec((CH, wk), lambda s: (idx(s), kb))
    v = pl.BlockSpec((CH, wv), lambda s: (idx(s), vb))
    la = pl.BlockSpec((CH, wk), lambda s: (idx(s), lab))
    o = pl.BlockSpec((CH, wv), lambda s: (idx(s), 0))
    dk = pl.BlockSpec((CH, wk), lambda s: (idx(s), 0))
    st = pl.BlockSpec((None, GH, GDV, GDK), lambda s: (idx(s), 0, 0, 0))
    return q, k, v, la, o, dk, st


def _gla_fwd(p, la, ncc, name):
    t = p.shape[0]
    nc = t // CH
    specs = [_gla_specs(ncc, nc, rev, False) for rev in (False, True)]

    def body(qf, kf, vf, laf, qb_, kb_, vb_, lab, of, stf, ob, stb, s_scr):
        @pl.when(pl.program_id(0) == 0)
        def _():
            s_scr[...] = jnp.zeros_like(s_scr)

        ch = _gla_chains(((False, qf, kf, vf, laf, of, stf), (True, qb_, kb_, vb_, lab, ob, stb)))
        qts, kts, gls, khs, keeps, _ = _gla_factors([c[3][:, _hk(c[2])] for c in ch], [c[4][:, _hk(c[2])] for c in ch],
                                                    [c[6][:, _hk(c[2])] for c in ch], [c[0] for c in ch])
        sts = [s_scr[c[1], c[2]] for c in ch]
        for c, st in zip(ch, sts):
            c[8][c[2]] = st
        vbs = [c[5][:, _hv(c[2])].astype(BF16) for c in ch]
        qbs = [qt.astype(BF16) for qt in qts]
        a_s = [jnp.where(keep, lax.dot_general(qb, kt.astype(BF16), _NT, preferred_element_type=F32), 0.0)
               for keep, qb, kt in zip(keeps, qbs, kts)]
        inter = [lax.dot_general(qb, st.astype(BF16), _NT, preferred_element_type=F32) for qb, st in zip(qbs, sts)]
        intra = [jnp.dot(a.astype(BF16), vb, preferred_element_type=F32) for a, vb in zip(a_s, vbs)]
        for c, x, y in zip(ch, inter, intra):
            c[7][:, _hv(c[2])] = x + y
        upd = [lax.dot_general(vb, kh.astype(BF16), _TN, preferred_element_type=F32) for vb, kh in zip(vbs, khs)]
        for c, st, gl, u in zip(ch, sts, gls, upd):
            s_scr[c[1], c[2]] = st * gl + u

    o_shape = jax.ShapeDtypeStruct((t, GH * GDV), F32)
    st_shape = jax.ShapeDtypeStruct((nc, GH, GDV, GDK), F32)
    return pl.pallas_call(
        body, name=name, grid=(nc,),
        in_specs=[sp for s_ in specs for sp in s_[:4]],
        out_specs=tuple(sp for s_ in specs for sp in (s_[4], s_[6])),
        out_shape=(o_shape, st_shape, o_shape, st_shape),
        scratch_shapes=[pltpu.VMEM((2, GH, GDV, GDK), F32)], compiler_params=_cparams(1, 32),
    )(p, p, p, la, p, p, p, la)


def _gla_bwd(p, la, do, stf, stb, ncc, name):
    t = p.shape[0]
    nc = t // CH
    specs = [_gla_specs(ncc, nc, rev, True) for rev in (False, True)]

    def mm(xs, ys, dims=None):
        if dims is None:
            return [jnp.dot(x, y, preferred_element_type=F32) for x, y in zip(xs, ys)]
        return [lax.dot_general(x, y, dims, preferred_element_type=F32) for x, y in zip(xs, ys)]

    def body(*refs):
        ins_f, ins_b, outs_f, outs_b, ds_scr = refs[0:6], refs[6:12], refs[12:16], refs[16:20], refs[20]

        @pl.when(pl.program_id(0) == 0)
        def _():
            ds_scr[...] = jnp.zeros_like(ds_scr)

        ch = _gla_chains(((False, *ins_f, *outs_f), (True, *ins_b, *outs_b)))
        revs = [c[0] for c in ch]
        qts, kts, gls, khs, keeps, bcs = _gla_factors([c[3][:, _hk(c[2])] for c in ch], [c[4][:, _hk(c[2])] for c in ch],
                                                      [c[6][:, _hk(c[2])] for c in ch], revs)
        stvs = [c[8][c[2]].astype(BF16) for c in ch]
        dsns = [ds_scr[c[1], c[2]] for c in ch]
        dsbs = [x.astype(BF16) for x in dsns]
        vbs = [c[5][:, _hv(c[2])].astype(BF16) for c in ch]
        dobs = [c[7][:, _hv(c[2])].astype(BF16) for c in ch]
        qbs, kbs = [x.astype(BF16) for x in qts], [x.astype(BF16) for x in kts]
        a_s = [jnp.where(keep, x, 0.0).astype(BF16) for keep, x in zip(keeps, mm(qbs, kbs, _NT))]
        das = [jnp.where(keep, x, 0.0).astype(BF16) for keep, x in zip(keeps, mm(dobs, vbs, _NT))]
        dqts = [x + y for x, y in zip(mm(dobs, stvs), mm(das, kbs))]
        dkhs = mm(vbs, dsbs)
        dkts = [x + dkh * gl for x, dkh, gl in zip(mm(das, qbs, _TN), dkhs, gls)]
        for c, x, y in zip(ch, mm(a_s, dobs, _TN), mm([kh.astype(BF16) for kh in khs], dsbs, _NT)):
            c[11][:, _hv(c[2])] = x + y
        for c, x, dsn, gl in zip(ch, mm(dobs, qbs, _TN), dsns, gls):
            ds_scr[c[1], c[2]] = x + dsn * gl
        dgls = [jnp.sum(c[8][c[2]] * dsn, axis=0, keepdims=True) + jnp.sum(dkh * kt, axis=0, keepdims=True)
                for c, dsn, dkh, kt in zip(ch, dsns, dkhs, kts)]
        row = lax.broadcasted_iota(jnp.int32, (CH, GDK), 0)
        dbcs = [dqt * qt - dkt * kt + jnp.where(row == (0 if rev else CH - 1), dgl * gl, 0.0)
                for rev, dqt, qt, dkt, kt, dgl, gl in zip(revs, dqts, qts, dkts, kts, dgls, gls)]
        r = lax.broadcasted_iota(jnp.int32, (CH, CH), 0)
        c_ = lax.broadcasted_iota(jnp.int32, (CH, CH), 1)
        dlas = [jnp.dot(((c_ <= r) if rev else (c_ >= r)).astype(F32), dbc, preferred_element_type=F32, precision=HIGHEST)
                for rev, dbc in zip(revs, dbcs)]
        for c, dla, dqt, dkt, bc in zip(ch, dlas, dqts, dkts, bcs):
            c[12][:, _hk(c[2])] = dla
            c[9][:, _hk(c[2])] = dqt * (GLA_SCALE * jnp.exp(bc))
            c[10][:, _hk(c[2])] = dkt * jnp.exp(-bc)

    k_shape = jax.ShapeDtypeStruct((t, GH * GDK), F32)
    v_shape = jax.ShapeDtypeStruct((t, GH * GDV), F32)
    res = pl.pallas_call(
        body, name=name, grid=(nc,),
        in_specs=[sp for q_s, k_s, v_s, la_s, o_s, _, st_s in specs for sp in (q_s, k_s, v_s, la_s, o_s, st_s)],
        out_specs=tuple(sp for _, _, _, _, o_s, dk_s, _ in specs for sp in (dk_s, dk_s, o_s, dk_s)),
        out_shape=(k_shape, k_shape, v_shape, k_shape) * 2,
        scratch_shapes=[pltpu.VMEM((2, GH, GDV, GDK), F32)], compiler_params=_cparams(1, 32),
    )(p, p, p, la, do, stf, p, p, p, la, do, stb)
    return res[:4], res[4:]


def _gla_merge_bwd(gf, gb, z, p, wd, name):
    t = p.shape[0]
    w2 = GH * GDK

    def body(dqf, dkf, dvf, dlf, dqb, dkb, dvb, dlb, z_ref, r_ref, w_ref, dq_ref, dk_ref, dv_ref, dr_ref, db_ref, dw_ref):
        i = pl.program_id(0)
        dq_ref[...] = (dqf[...] + dqb[...]).astype(BF16)
        dk_ref[...] = (dkf[...] + dkb[...]).astype(BF16)
        dv_ref[...] = (dvf[...] + dvb[...]).astype(BF16)
        zv = z_ref[...]
        dz = jnp.concatenate([dlf[...], dlb[...]], axis=1) * (_sigmoid(-zv) / GLA_TAU)
        dzb = dz.astype(BF16)
        dr_ref[...] = lax.dot_general(dzb, w_ref[...].astype(BF16), _NT, preferred_element_type=F32).astype(BF16)

        @pl.when(i == 0)
        def _():
            db_ref[...] = jnp.zeros_like(db_ref)
            dw_ref[...] = jnp.zeros_like(dw_ref)

        db_ref[...] += jnp.sum(dz, axis=0, keepdims=True)
        dw_ref[...] += lax.dot_general(r_ref[...].astype(BF16), dzb, _TN, preferred_element_type=F32)

    half = pl.BlockSpec((TM, w2), lambda i: (i, 0))
    row = pl.BlockSpec((TM, D), lambda i: (i, 0))
    wspec = pl.BlockSpec((R_PAD, D), lambda i: (0, 0))
    return pl.pallas_call(
        body, name=name, grid=(t // TM,),
        in_specs=[half, half, row, half, half, half, row, half, row, _pcol("r", R_PAD), wspec],
        out_specs=(half, half, row, pl.BlockSpec((TM, R_PAD), lambda i: (i, 0)),
                   pl.BlockSpec((8, D), lambda i: (0, 0)), wspec),
        out_shape=(jax.ShapeDtypeStruct((t, w2), BF16), jax.ShapeDtypeStruct((t, w2), BF16),
                   jax.ShapeDtypeStruct((t, D), BF16), jax.ShapeDtypeStruct((t, R_PAD), BF16),
                   jax.ShapeDtypeStruct((8, D), F32), jax.ShapeDtypeStruct((R_PAD, D), F32)),
        compiler_params=_cparams(1, 40),
    )(*gf, *gb, z, p, wd)


def _branch_fwd(att, of, ob, p, gla_g, name):
    t = p.shape[0]

    def body(att_ref, of_ref, ob_ref, za_ref, zg_ref, g_ref, yb_ref, yc_ref):
        za = za_ref[...]
        yb_ref[...] = (att_ref[...] * (za * _sigmoid(za))).astype(BF16)
        for h in range(GH):
            sl = slice(h * GDV, (h + 1) * GDV)
            o = of_ref[:, sl] + ob_ref[:, sl]
            n = o * lax.rsqrt(jnp.mean(o * o, axis=-1, keepdims=True) + EPS) * g_ref[...]
            zh = zg_ref[:, sl]
            yc_ref[:, sl] = (n * (zh * _sigmoid(zh))).astype(BF16)

    row = pl.BlockSpec((TM, D), lambda i: (i, 0))
    return pl.pallas_call(
        body, name=name, grid=(t // TM,),
        in_specs=[row, row, row, _pcol("z_attn", D), _pcol("zg", D), pl.BlockSpec((1, GDV), lambda i: (0, 0))],
        out_specs=(row, row),
        out_shape=(jax.ShapeDtypeStruct((t, D), BF16), jax.ShapeDtypeStruct((t, D), BF16)),
        compiler_params=_cparams(1, 40),
    )(att, of, ob, p, p, gla_g)


def _branch_bwd(dyb, dyc, att, of, ob, p, gla_g, name):
    t = p.shape[0]

    def body(dyb_ref, dyc_ref, att_ref, of_ref, ob_ref, za_ref, zg_ref, g_ref, datt_ref, dza_ref, do_ref, dzg_ref, dg_ref):
        i = pl.program_id(0)

        @pl.when(i == 0)
        def _():
            dg_ref[...] = jnp.zeros_like(dg_ref)

        za, dyb = za_ref[...], dyb_ref[...]
        sa = _sigmoid(za)
        datt_ref[...] = dyb * (za * sa)
        dza_ref[...] = (dyb * att_ref[...] * (sa * (1.0 + za * (1.0 - sa)))).astype(BF16)
        g = g_ref[...]
        for h in range(GH):
            sl = slice(h * GDV, (h + 1) * GDV)
            o = of_ref[:, sl] + ob_ref[:, sl]
            r = lax.rsqrt(jnp.mean(o * o, axis=-1, keepdims=True) + EPS)
            oh = o * r
            zh, dyc = zg_ref[:, sl], dyc_ref[:, sl]
            sg = _sigmoid(zh)
            dn = dyc * (zh * sg)
            dzg_ref[:, sl] = (dyc * (oh * g) * (sg * (1.0 + zh * (1.0 - sg)))).astype(BF16)
            doh = dn * g
            do_ref[:, sl] = r * (doh - oh * jnp.mean(doh * oh, axis=-1, keepdims=True))
            dg_ref[...] += jnp.sum(dn * oh, axis=0, keepdims=True)

    row = pl.BlockSpec((TM, D), lambda i: (i, 0))
    return pl.pallas_call(
        body, name=name, grid=(t // TM,),
        in_specs=[row, row, row, row, row, _pcol("z_attn", D), _pcol("zg", D), pl.BlockSpec((1, GDV), lambda i: (0, 0))],
        out_specs=(row, row, row, row, pl.BlockSpec((8, GDV), lambda i: (0, 0))),
        out_shape=(jax.ShapeDtypeStruct((t, D), F32), jax.ShapeDtypeStruct((t, D), BF16),
                   jax.ShapeDtypeStruct((t, D), F32), jax.ShapeDtypeStruct((t, D), BF16),
                   jax.ShapeDtypeStruct((8, GDV), F32)),
        compiler_params=_cparams(1, 48),
    )(dyb, dyc, att, of, ob, p, p, gla_g)


def _merge_fwd(bra, brb, brc, p, b_gate, name):
    t = p.shape[0]
    mgb = OFF["mg"] // D

    def body(a_ref, b_ref, c_ref, ga_ref, gb_ref, gc_ref, bg_ref, m_ref):
        m_ref[...] = (_sigmoid(ga_ref[...] + bg_ref[:, 0:D]) * a_ref[...]
                      + _sigmoid(gb_ref[...] + bg_ref[:, D:2 * D]) * b_ref[...]
                      + _sigmoid(gc_ref[...] + bg_ref[:, 2 * D:3 * D]) * c_ref[...]).astype(BF16)

    row = pl.BlockSpec((TM, D), lambda i: (i, 0))
    gates = [pl.BlockSpec((TM, D), functools.partial(lambda i, b: (i, b), b=mgb + j)) for j in range(3)]
    return pl.pallas_call(
        body, name=name, grid=(t // TM,),
        in_specs=[row, row, row, *gates, pl.BlockSpec((1, 3 * D), lambda i: (0, 0))],
        out_specs=row, out_shape=jax.ShapeDtypeStruct((t, D), BF16), compiler_params=_cparams(1, 40),
    )(bra, brb, brc, p, p, p, b_gate)


def _merge_bwd(dm, bra, brb, brc, p, b_gate, name):
    t = p.shape[0]
    mgb = OFF["mg"] // D

    def body(dm_ref, a_ref, b_ref, c_ref, ga_ref, gb_ref, gc_ref, bg_ref, da_ref, db_ref, dc_ref, dmg_ref, dbg_ref):
        i = pl.program_id(0)

        @pl.when(i == 0)
        def _():
            dbg_ref[...] = jnp.zeros_like(dbg_ref)

        dm = dm_ref[...]
        for j, (br_ref, g_ref, d_ref) in enumerate(((a_ref, ga_ref, da_ref), (b_ref, gb_ref, db_ref), (c_ref, gc_ref, dc_ref))):
            sl = slice(j * D, (j + 1) * D)
            g = _sigmoid(g_ref[...] + bg_ref[:, sl])
            d_ref[...] = (dm * g).astype(BF16)
            dmg = dm * br_ref[...] * (g * (1.0 - g))
            dmg_ref[:, sl] = dmg.astype(BF16)
            dbg_ref[:, sl] += jnp.sum(dmg, axis=0, keepdims=True)

    row = pl.BlockSpec((TM, D), lambda i: (i, 0))
    gates = [pl.BlockSpec((TM, D), functools.partial(lambda i, b: (i, b), b=mgb + j)) for j in range(3)]
    return pl.pallas_call(
        body, name=name, grid=(t // TM,),
        in_specs=[row, row, row, row, *gates, pl.BlockSpec((1, 3 * D), lambda i: (0, 0))],
        out_specs=(row, row, row, pl.BlockSpec((TM, 3 * D), lambda i: (i, 0)), pl.BlockSpec((8, 3 * D), lambda i: (0, 0))),
        out_shape=(jax.ShapeDtypeStruct((t, D), BF16),) * 3 + (jax.ShapeDtypeStruct((t, 3 * D), BF16),
                                                                jax.ShapeDtypeStruct((8, 3 * D), F32)),
        compiler_params=_cparams(1, 48),
    )(dm, bra, brb, brc, p, p, p, b_gate)


def _adamw(gsrc, w, m, v, name):
    ns, nl, r, c = gsrc.shape
    gb = gsrc.dtype.itemsize

    def fits(rows, cols):
        lanes = -(-cols // LANE) * LANE
        return ns * rows * lanes * gb <= ADAM_SRC_BYTES and rows * lanes * 4 <= ADAM_ROW_BYTES

    tr, tc = r, c
    if not fits(r, c):
        rows = [cand for cand in range(16, r, 16) if r % cand == 0 and fits(cand, c)]
        cols = [cand for cand in range(LANE, c, LANE) if c % cand == 0 and fits(r, cand)]
        if rows:
            tr = rows[-1]
        else:
            tc = cols[-1]

    def body(g_ref, w_ref, m_ref, v_ref, go_ref, d_ref, mo_ref, vo_ref):
        g = g_ref[0].astype(F32)
        for s in range(1, ns):
            g = g + g_ref[s].astype(F32)
        mn = ADAM_B1 * m_ref[...] + (1.0 - ADAM_B1) * g
        vn = ADAM_B2 * v_ref[...] + (1.0 - ADAM_B2) * jnp.square(g)
        m_hat = mn / (1.0 - ADAM_B1 ** ADAM_STEP)
        v_hat = vn / (1.0 - ADAM_B2 ** ADAM_STEP)
        go_ref[...] = g
        d_ref[...] = -ADAM_LR * (m_hat / (jnp.sqrt(v_hat) + ADAM_EPS) + ADAM_WD * w_ref[...])
        mo_ref[...] = mn
        vo_ref[...] = vn

    row = pl.BlockSpec((None, tr, tc), lambda l, i, j: (l, i, j))
    return pl.pallas_call(
        body, name=name, grid=(nl, r // tr, c // tc),
        in_specs=[pl.BlockSpec((ns, None, tr, tc), lambda l, i, j: (0, l, i, j)), row, row, row],
        out_specs=(row,) * 4, out_shape=(jax.ShapeDtypeStruct((nl, r, c), F32),) * 4,
        compiler_params=_cparams(3, 48),
    )(gsrc, w, m, v)


def _slot_sum(gsrc, name):
    ns, r, _ = gsrc.shape

    def body(g_ref, o_ref):
        g = g_ref[0]
        for s in range(1, ns):
            g = g + g_ref[s]
        o_ref[...] = g

    return pl.pallas_call(body, name=name, out_shape=jax.ShapeDtypeStruct((r, LANE), F32))(gsrc)


def _pack(parts, dtype, row_align):
    chunk = row_align * LANE
    out, spans, o = [], [], 0
    for a in parts:
        f = a.reshape(-1).astype(dtype)
        n = f.shape[0]
        pad = (-n) % chunk
        if pad:
            f = jnp.concatenate([f, jnp.zeros((pad,), dtype)])
        out.append(f.reshape(-1, LANE))
        spans.append((o, n))
        o += (n + pad) // LANE
    return jnp.concatenate(out, axis=0), spans


def _unpack(packed, spans, shapes):
    res = []
    for (o, n), shp in zip(spans, shapes):
        rows = -(-n // LANE)
        res.append(packed[o:o + rows].reshape(-1)[:n].reshape(shp))
    return res


def _rope_tables(ctx, seq):
    n_rows = seq // GRID_W
    pairs = HD // 4
    row = jnp.repeat(jnp.arange(n_rows, dtype=F32), GRID_W)
    col = jnp.tile(jnp.arange(GRID_W, dtype=F32), n_rows)
    freqs = ROPE_THETA ** (-jnp.arange(pairs, dtype=F32) * 2.0 / (HD // 2))
    ar, ac = row[:, None] * freqs, col[:, None] * freqs
    cos_l = jnp.concatenate([jnp.cos(ar), jnp.cos(ar), jnp.cos(ac), jnp.cos(ac)], axis=1)
    sin_l = jnp.concatenate([-jnp.sin(ar), jnp.sin(ar), -jnp.sin(ac), jnp.sin(ac)], axis=1)
    cos_t = jnp.concatenate([jnp.ones((ctx, HD), F32), cos_l], axis=0)
    sin_t = jnp.concatenate([jnp.zeros((ctx, HD), F32), sin_l], axis=0)
    return cos_t, sin_t


def _to_proj_layout(wt):
    parts = [wt[s:s + wd] for _, s, wd in _SEGS]
    used = sum(wd for _, _, wd in _SEGS)
    parts.append(jnp.zeros((NP - used, wt.shape[1]), wt.dtype))
    return jnp.concatenate(parts, axis=0)


def _from_proj_layout(g):
    order = sorted(_SEGS, key=lambda sg: sg[1])
    return jnp.concatenate([g[OFF[n]:OFF[n] + wd] for n, _, wd in order], axis=0)


def _row0(a):
    return a[..., 0, :]


def kernel(x, c, ctx, c_ctx, w_ada, b_ada, g_pre, g_post, w_in, conv_w, q_norm_g, k_norm_g, w_decay_fwd, b_decay_fwd, w_decay_bwd, b_decay_bwd, gla_norm_g, w_br_conv, w_br_attn, w_br_gla, b_gate, w_out, loss_target, m_c_ctx, m_w_ada, m_b_ada, m_g_pre, m_g_post, m_w_in, m_conv_w, m_q_norm_g, m_k_norm_g, m_w_decay_fwd, m_b_decay_fwd, m_w_decay_bwd, m_b_decay_bwd, m_gla_norm_g, m_w_br_conv, m_w_br_attn, m_w_br_gla, m_b_gate, m_w_out, v_c_ctx, v_w_ada, v_b_ada, v_g_pre, v_g_post, v_w_in, v_conv_w, v_q_norm_g, v_k_norm_g, v_w_decay_fwd, v_b_decay_fwd, v_w_decay_bwd, v_b_decay_bwd, v_gla_norm_g, v_w_br_conv, v_w_br_attn, v_w_br_gla, v_b_gate, v_w_out):
    seq, n_ctx = x.shape[1], ctx.shape[1]
    assert n_ctx % TM == 0 and seq % TM == 0 and seq % GRID_W == 0
    t = n_ctx + seq
    nct, ncc = n_ctx // TM, n_ctx // CH
    dev = 4 * lax.axis_index("x") + 2 * lax.axis_index("y") + lax.axis_index("c")
    ada_w = w_ada.shape[2]
    in_w = w_in.shape[2]
    br_r = w_br_conv.shape[1]

    def in_t(a, l):
        return a.transpose(2, 0, 1)[:, l, :]

    wb = [w.astype(BF16) for w in (w_ada, w_br_conv, w_br_attn, w_br_gla, w_out)]
    wall = _all_gather([wb[0][0], in_t(w_in, 0).astype(BF16)], "gather_first")
    later = _GatherRider([in_t(w_in, 1).astype(BF16), wb[0][1], wb[1], wb[2], wb[3], wb[4]])
    small_parts = [conv_w, w_decay_fwd, w_decay_bwd]
    spack, sspans = _pack(small_parts, F32, 8)
    sall, = _all_gather([spack], "gather_small")

    def gathered(all_, spans, k, shp):
        o, n = spans[k]
        rows = n // LANE
        return all_[:, o:o + rows].reshape((NDEV,) + shp)

    def full_in(g):
        return _to_proj_layout(g.reshape(IN_WIDTH, D))

    def full_ada(g):
        return g.transpose(1, 0, 2).reshape(D, 3 * D)

    w_ada_f = [full_ada(wall[0]), None]
    wp = [full_in(wall[1]), None]
    conv_f = gathered(sall, sspans, 0, (DEPTH, 3, D // NDEV)).transpose(1, 2, 0, 3).reshape(DEPTH, 3, D)
    wdf_f = gathered(sall, sspans, 1, (DEPTH, GLA_RANK, GH * GDK // NDEV)).transpose(1, 2, 0, 3).reshape(DEPTH, GLA_RANK, GH * GDK)
    wdb_f = gathered(sall, sspans, 2, (DEPTH, GLA_RANK, GH * GDK // NDEV)).transpose(1, 2, 0, 3).reshape(DEPTH, GLA_RANK, GH * GDK)

    cos_t, sin_t = _rope_tables(n_ctx, seq)
    cc = jnp.concatenate([c_ctx[None, :], c.reshape(1, D), jnp.zeros((6, D), F32)], axis=0)
    silu_cc, dsilu_cc = _ada_in(cc)

    conv8, wd_pad, bd = [], [], []
    for l in range(DEPTH):
        conv8.append(jnp.concatenate([conv_f[l], jnp.zeros((5, D), F32)], axis=0))
        zr = jnp.zeros((GLA_RANK, GH * GDK), F32)
        wd_pad.append(jnp.concatenate([jnp.concatenate([wdf_f[l], zr], axis=1), jnp.concatenate([zr, wdb_f[l]], axis=1),
                                       jnp.zeros((R_PAD - 2 * GLA_RANK, D), F32)], axis=0))
        bd.append(jnp.concatenate([b_decay_fwd[l], b_decay_bwd[l]])[None, :])

    xs = jnp.concatenate([ctx[0], x[0]], axis=0)
    saved = []
    for l in range(DEPTH):
        n = f"l{l}_"
        mod = _mm(silu_cc, w_ada_f[l], n + "mod", bias=b_ada[l][None, :])
        mod3 = mod[0:2].reshape(2, 3, D)
        h = _prenorm_fwd(xs, g_pre[l][None, :], mod3, nct, n + "prenorm")
        p = _mm(h, wp[l], n + "proj", tb=True)
        cv, ya = _conv_fwd(p, conv8[l], nct, n + "conv")
        qr, kr = _qk_prep_fwd(p, q_norm_g[l][None, :], k_norm_g[l][None, :], cos_t, sin_t, n + "qk_prep")
        att, lse, *got = _attn_fwd(qr, kr, p, nct, n + "attn", rider=later if l == 0 else None)
        if l == 0:
            wp[1], w_ada_f[1] = full_in(got[0]), full_ada(got[1])
            w_brs_f = [g.transpose(1, 0, 2, 3).reshape(DEPTH, D, D) for g in got[2:]]
        z, la = _decay_fwd(p, wd_pad[l], bd[l], n + "decay")
        of, stf, ob, stb = _gla_fwd(p, la, ncc, n + "gla")
        yb, yc = _branch_fwd(att, of, ob, p, gla_norm_g[l][None, :], n + "branch")
        bra = _mm(ya, w_brs_f[0][l], n + "br_conv")
        brb = _mm(yb, w_brs_f[1][l], n + "br_attn")
        brc = _mm(yc, w_brs_f[2][l], n + "br_gla")
        mm_ = _merge_fwd(bra, brb, brc, p, b_gate[l][None, :], n + "merge")
        out = _mm(mm_, w_brs_f[3][l], n + "out")
        x_new = _post_fwd(xs, out, g_post[l][None, :], mod3, nct, n + "post")
        saved.append(dict(x=xs, mod3=mod3, h=h, p=p, cv=cv, ya=ya, qr=qr, kr=kr, att=att, lse=lse, z=z, la=la, of=of, ob=ob,
                          stf=stf, stb=stb, yb=yb, yc=yc, bra=bra, brb=brb, brc=brc, m=mm_, out=out))
        xs = x_new

    dx, sq = _loss_grad(xs, loss_target[0], nct, "loss")
    loss = lax.psum(0.5 * sq[0, 0] / D, ("x", "y", "c"))

    gw = {k: [None] * DEPTH for k in ("w_in", "br_conv", "br_attn", "br_gla", "out", "b_gate", "g_pre", "g_post",
                                      "conv_w", "qg", "kg", "wd", "bdec", "gla_g", "dmod")}
    dctx = []

    def in_slots(l):
        return _from_proj_layout(gw["w_in"][l]).reshape(NDEV, in_w, D)

    def br_slots():
        return [jnp.stack([gw[k][l].reshape(NDEV, br_r, D) for l in range(DEPTH)], axis=1)
                for k in ("br_conv", "br_attn", "br_gla", "out")]

    for l in reversed(range(DEPTH)):
        n = f"l{l}_b_"
        s = saved[l]
        p = s["p"]
        d_out, dgt, gw["g_post"][l] = _post_bwd(dx, s["out"], g_post[l][None, :], s["mod3"], nct, n + "post")
        dm = _mm(d_out, w_brs_f[3][l], n + "dm", tb=True)
        gw["out"][l] = _mm(s["m"], d_out, n + "dw_out", ta=True, out_dtype=BF16)
        dbra, dbrb, dbrc, dmg, gw["b_gate"][l] = _merge_bwd(dm, s["bra"], s["brb"], s["brc"], p, b_gate[l][None, :], n + "merge")
        dya = _mm(dbra, w_brs_f[0][l], n + "dya", tb=True)
        dyb = _mm(dbrb, w_brs_f[1][l], n + "dyb", tb=True)
        dyc = _mm(dbrc, w_brs_f[2][l], n + "dyc", tb=True)
        gw["br_conv"][l] = _mm(s["ya"], dbra, n + "dw_conv", ta=True, out_dtype=BF16)
        gw["br_attn"][l] = _mm(s["yb"], dbrb, n + "dw_attn", ta=True, out_dtype=BF16)
        gw["br_gla"][l] = _mm(s["yc"], dbrc, n + "dw_gla", ta=True, out_dtype=BF16)
        dcv, dab, daz = _conv_bwd_a(dya, p, s["cv"], n + "conv_a")
        dac, dax, gw["conv_w"][l] = _conv_bwd_b(dcv, p, conv8[l], nct, n + "conv_b")
        datt, dza, dgo, dzg, gw["gla_g"][l] = _branch_bwd(dyb, dyc, s["att"], s["of"], s["ob"], p, gla_norm_g[l][None, :], n + "branch")
        ex1 = _ExchangeRider([in_slots(DEPTH - 1)] + br_slots()) if l == 0 else None
        dqr, dkr, dv, *got = _attn_bwd(s["qr"], s["kr"], p, s["att"], s["lse"], datt, nct, n + "attn", rider=ex1)
        if l == 0:
            recv_in1, recv_br = got[0], got[1:]
        dq, dk, gw["qg"][l], gw["kg"][l] = _qk_prep_bwd(dqr, dkr, p, q_norm_g[l][None, :], k_norm_g[l][None, :], cos_t, sin_t, n + "qk_prep")
        gf, gb = _gla_bwd(p, s["la"], dgo, s["stf"], s["stb"], ncc, n + "gla")
        dgq, dgk, dgv, dr, gw["bdec"][l], gw["wd"][l] = _gla_merge_bwd(gf, gb, s["z"], p, wd_pad[l], n + "gla_merge")
        pieces = dict(a_b=dab, a_c=dac, a_x=dax, a_z=daz, q=dq, z_attn=dza, gv=dgv, zg=dzg, mg=dmg, gq=dgq, gk=dgk,
                      k=dk, v=dv.astype(BF16), r=dr)
        used = sum(wd_ for _, _, wd_ in _SEGS) - 32 + R_PAD
        dp = jnp.concatenate([pieces[nm] for nm, _, _ in _SEGS] + [jnp.zeros((t, NP - used), BF16)], axis=1)
        gw["w_in"][l] = _mm(dp, s["h"], n + "dw_in", ta=True, out_dtype=BF16)
        if l == 0:
            dh, recv_in0 = _mm(dp, wp[l], n + "dh", rider=_ExchangeRider([in_slots(0)]))
        else:
            dh = _mm(dp, wp[l], n + "dh")
        dx, dsh, dsc, gw["g_pre"][l] = _prenorm_bwd(dh, s["x"], dx, g_pre[l][None, :], s["mod3"], nct, n + "prenorm")
        dmod = jnp.stack([_row0(dsh), _row0(dsc), _row0(dgt)], axis=1).reshape(2, 3 * D)
        gw["dmod"][l] = dmod
        dmod8 = jnp.concatenate([dmod, jnp.zeros((6, 3 * D), F32)], axis=0)
        dctx.append(_mm(dmod8, w_ada_f[l], n + "dsilu", tb=True))
    grad_x = dx[n_ctx:][None]
    g_cctx = _cctx_grad(dctx[0], dctx[1], dsilu_cc)[0]

    def st2(name):
        return jnp.stack(gw[name])

    g_b_ada = jnp.stack([gw["dmod"][l][0] + gw["dmod"][l][1] for l in range(DEPTH)])
    g_bdf = jnp.stack([gw["bdec"][l][0, :GH * GDK] for l in range(DEPTH)])
    g_bdb = jnp.stack([gw["bdec"][l][0, GH * GDK:] for l in range(DEPTH)])
    g_wdf = jnp.stack([gw["wd"][l][0:GLA_RANK, :GH * GDK] for l in range(DEPTH)])
    g_wdb = jnp.stack([gw["wd"][l][GLA_RANK:2 * GLA_RANK, GH * GDK:] for l in range(DEPTH)])
    rep_names = ["c_ctx", "b_ada", "g_pre", "g_post", "q_norm_g", "k_norm_g", "b_decay_fwd", "b_decay_bwd", "gla_norm_g", "b_gate"]
    rep_grads = [g_cctx, g_b_ada, st2("g_pre")[:, 0], st2("g_post")[:, 0], st2("qg")[:, 0], st2("kg")[:, 0], g_bdf, g_bdb,
                 st2("gla_g")[:, 0], st2("b_gate")[:, 0]]
    rep_w = [c_ctx, b_ada, g_pre, g_post, q_norm_g, k_norm_g, b_decay_fwd, b_decay_bwd, gla_norm_g, b_gate]
    rep_m = [m_c_ctx, m_b_ada, m_g_pre, m_g_post, m_q_norm_g, m_k_norm_g, m_b_decay_fwd, m_b_decay_bwd, m_gla_norm_g, m_b_gate]
    rep_v = [v_c_ctx, v_b_ada, v_g_pre, v_g_post, v_q_norm_g, v_k_norm_g, v_b_decay_fwd, v_b_decay_bwd, v_gla_norm_g, v_b_gate]
    shard_grads = [st2("conv_w")[:, 0:3], g_wdf, g_wdb]
    extra = [silu_cc[0:2], jnp.stack(gw["dmod"])]
    gpack, gspans = _pack(rep_grads + shard_grads + extra, F32, 8)
    gall, = _all_gather([gpack], "gather_small_grads")
    n_rep = len(rep_grads)
    rep_rows = gspans[n_rep][0]
    shard_rows = gspans[n_rep + len(shard_grads)][0]

    wpk, rspans = _pack(rep_w, F32, 8)
    mpk, _ = _pack(rep_m, F32, 8)
    vpk, _ = _pack(rep_v, F32, 8)
    rep_out = _adamw(gall[:, None, :rep_rows], wpk[None], mpk[None], vpk[None], "adam_rep")
    rep_shapes = [a.shape for a in rep_w]
    rep_g, rep_d, rep_nm, rep_nv = [_unpack(o[0], rspans, rep_shapes) for o in rep_out]

    ssum = _slot_sum(gall[:, rep_rows:shard_rows], "sum_small_sharded")
    sh_spans = [(o - rep_rows, n_) for o, n_ in gspans[n_rep:n_rep + len(shard_grads)]]
    g_conv_full, g_wdf_full, g_wdb_full = _unpack(ssum, sh_spans, [(DEPTH, 3, D), (DEPTH, GLA_RANK, GH * GDK), (DEPTH, GLA_RANK, GH * GDK)])
    cw, dw = D // NDEV, GH * GDK // NDEV
    sh_g = [lax.dynamic_slice_in_dim(g_conv_full, dev * cw, cw, axis=2),
            lax.dynamic_slice_in_dim(g_wdf_full, dev * dw, dw, axis=2),
            lax.dynamic_slice_in_dim(g_wdb_full, dev * dw, dw, axis=2)]
    sgp, shs = _pack(sh_g, F32, 8)
    swp, _ = _pack([conv_w, w_decay_fwd, w_decay_bwd], F32, 8)
    smp, _ = _pack([m_conv_w, m_w_decay_fwd, m_w_decay_bwd], F32, 8)
    svp, _ = _pack([v_conv_w, v_w_decay_fwd, v_w_decay_bwd], F32, 8)
    sh_out = _adamw(sgp[None, None], swp[None], smp[None], svp[None], "adam_small_sharded")
    sh_shapes = [conv_w.shape, w_decay_fwd.shape, w_decay_bwd.shape]
    sh_gr, sh_d, sh_nm, sh_nv = [_unpack(o[0], shs, sh_shapes) for o in sh_out]

    eo, en = gspans[n_rep + len(shard_grads)]
    a_all = gall[:, eo:eo + en // LANE].reshape(NDEV * 2, D)
    eo2, en2 = gspans[n_rep + len(shard_grads) + 1]
    d_all = gall[:, eo2:eo2 + en2 // LANE].reshape(NDEV, DEPTH, 2, 3 * D).transpose(1, 0, 2, 3).reshape(DEPTH, NDEV * 2, 3 * D)
    g_ada = jnp.stack([_mm(a_all, lax.dynamic_slice_in_dim(d_all[l], dev * ada_w, ada_w, axis=1), f"dw_ada{l}",
                           ta=True, precise=True, tk=NDEV * 2) for l in range(DEPTH)])
    ada_g, ada_d, ada_nm, ada_nv = _adamw(g_ada[None], w_ada, m_w_ada, v_w_ada, "adam_ada")

    big_w = [w_br_conv, w_br_attn, w_br_gla, w_out]
    big_m = [m_w_br_conv, m_w_br_attn, m_w_br_gla, m_w_out]
    big_v = [v_w_br_conv, v_w_br_attn, v_w_br_gla, v_w_out]
    big_out = [_adamw(recv_br[j], big_w[j], big_m[j], big_v[j], f"adam_big{j}") for j in range(len(big_w))]
    in_out = [_adamw(r_[:, None], in_t(w_in, l)[None], in_t(m_w_in, l)[None], in_t(v_w_in, l)[None], f"adam_in{l}")
              for l, r_ in enumerate((recv_in0, recv_in1))]
    in_res = [jnp.stack([in_out[l][k][0] for l in range(DEPTH)], axis=1).transpose(1, 2, 0) for k in range(4)]
    big_g, big_d, big_nm, big_nv = [[in_res[k]] + [o[k] for o in big_out] for k in range(4)]

    def ordered(rep, ada, big, sh):
        c_ctx_, b_ada_, g_pre_, g_post_, qg_, kg_, bdf_, bdb_, glag_, bgate_ = rep
        w_in_, brc_, bra_, brg_, wout_ = big
        conv_, wdf_, wdb_ = sh
        return [c_ctx_, ada, b_ada_, g_pre_, g_post_, w_in_, conv_, qg_, kg_, wdf_, bdf_, wdb_, bdb_, glag_,
                brc_, bra_, brg_, bgate_, wout_]

    return (loss, grad_x,
            *ordered(rep_g, ada_g, big_g, sh_gr), *ordered(rep_d, ada_d, big_d, sh_d),
            *ordered(rep_nm, ada_nm, big_nm, sh_nm), *ordered(rep_nv, ada_nv, big_nv, sh_nv))
```

```python
import functools

import numpy as np
import jax
import jax.numpy as jnp
from jax import lax
from jax.experimental import pallas as pl
from jax.experimental.pallas import tpu as pltpu

F32, BF16 = jnp.float32, jnp.bfloat16
HIGHEST = lax.Precision.HIGHEST

D = 1024
DEPTH = 2
GRID_W = 64
NH, NKV, HD = 8, 2, 128
GROUP = NH // NKV
ROPE_THETA = 10000.0
ATTN_SCALE = HD ** -0.5
Q_FOLD = ATTN_SCALE * 1.4426950408889634
P_HALO = 16
GH, GDK, GDV = 4, 128, 256
GLA_RANK = 16
GLA_TAU = 16.0
CH = 64
GLA_SCALE = GDK ** -0.5
EPS = 1e-6
NDEV = 8
LANE = 128
TM = 256
ATTN_HEADS_PER_STEP = 1
ATTN_KEY_CHUNK = 8192

ADAM_LR, ADAM_B1, ADAM_B2, ADAM_EPS, ADAM_WD, ADAM_STEP = 0.001, 0.9, 0.999, 1e-08, 0.01, 10

_SEGS = (("a_b", 0, 1024), ("a_c", 1024, 1024), ("a_x", 2048, 1024), ("a_z", 3072, 1024),
         ("q", 4096, 1024), ("z_attn", 5632, 1024), ("gv", 7680, 1024), ("zg", 8736, 1024),
         ("mg", 9760, 3072), ("gq", 6656, 512), ("gk", 7168, 512), ("k", 5120, 256), ("v", 5376, 256),
         ("r", 8704, 32))
IN_WIDTH = 12832
NP = 13312
OFF = {}
_o = 0
for _n, _s, _w in _SEGS:
    OFF[_n] = _o
    _o += _w
R_PAD = 128


def _cparams(ngrid, vmem_mb):
    return pltpu.CompilerParams(dimension_semantics=("arbitrary",) * ngrid, vmem_limit_bytes=vmem_mb << 20)


def _pick(n, cands):
    for c in cands:
        if n % c == 0:
            return c
    return n


def _sigmoid(x):
    return 1.0 / (1.0 + jnp.exp(-x))


ADAM_SRC_BYTES = 8 << 20
ADAM_ROW_BYTES = 1 << 20


def _all_gather(xs, name):
    return _comm_alone(_GatherRider(xs), name)


_HBM = pl.BlockSpec(memory_space=pl.ANY)


class _Rider:
    def __init__(self, xs, out_shapes):
        self.xs, self.n = list(xs), len(xs)
        self.out_shape = [jax.ShapeDtypeStruct(s, x.dtype) for s, x in zip(out_shapes, xs)]
        self.scratch = [pltpu.SemaphoreType.DMA((7 * self.n,)), pltpu.SemaphoreType.DMA((7 * self.n,)),
                        pltpu.SemaphoreType.DMA((self.n,))]


class _GatherRider(_Rider):
    def __init__(self, xs):
        super().__init__(xs, [(NDEV,) + x.shape for x in xs])

    def _parts(self, x_refs, out_refs, sems):
        n = self.n
        send_sems, recv_sems, local_sems = sems
        mx, my, mc = lax.axis_index("x"), lax.axis_index("y"), lax.axis_index("c")
        me, sibling = (mx, my, mc), (mx, my, 1 - mc)
        chips = [(1 - mx, my), (mx, 1 - my), (1 - mx, 1 - my)]

        def slot(a, px, py, pc):
            return out_refs[a].at[4 * px + 2 * py + pc]

        def copy(k, a, block, to, own=False):
            return pltpu.make_async_remote_copy(
                src_ref=x_refs[a] if own else slot(a, *block), dst_ref=slot(a, *block),
                send_sem=send_sems.at[k * n + a], recv_sem=recv_sems.at[k * n + a],
                device_id=to, device_id_type=pl.DeviceIdType.MESH)

        mine = [pltpu.make_async_copy(x_refs[a], slot(a, *me), local_sems.at[a]) for a in range(n)]
        first = [copy(0, a, me, sibling, own=True) for a in range(n)]
        first += [copy(1 + j, a, me, (*chip, mc), own=True) for a in range(n) for j, chip in enumerate(chips)]
        landed = [copy(1 + j, a, (*chip, mc), me) for a in range(n) for j, chip in enumerate(chips)]
        passed = [copy(4 + j, a, (*chip, mc), sibling) for a in range(n) for j, chip in enumerate(chips)]
        from_sibling = [copy(0, a, sibling, me) for a in range(n)]
        from_sibling += [copy(4 + j, a, (*chip, 1 - mc), me) for a in range(n) for j, chip in enumerate(chips)]
        return mine, first, landed, passed, from_sibling

    def start(self, x_refs, out_refs, sems):
        mine, first, _, _, _ = self._parts(x_refs, out_refs, sems)
        for cp in mine + first:
            cp.start()

    def middle(self, x_refs, out_refs, sems):
        _, _, landed, passed, _ = self._parts(x_refs, out_refs, sems)
        for got, fwd in zip(landed, passed):
            got.wait_recv()
            fwd.start()

    def finish(self, x_refs, out_refs, sems):
        mine, first, _, passed, from_sibling = self._parts(x_refs, out_refs, sems)
        for cp in from_sibling:
            cp.wait_recv()
        for cp in first + passed:
            cp.wait_send()
        for cp in mine:
            cp.wait()


class _ExchangeRider(_Rider):
    def __init__(self, xs):
        super().__init__(xs, [x.shape for x in xs])

    def _parts(self, x_refs, out_refs, sems):
        n = self.n
        send_sems, recv_sems, local_sems = sems
        mx, my, mc = lax.axis_index("x"), lax.axis_index("y"), lax.axis_index("c")
        me = 4 * mx + 2 * my + mc
        mine = [pltpu.make_async_copy(x_refs[a].at[me], out_refs[a].at[me], local_sems.at[a]) for a in range(n)]
        copies = []
        for a in range(n):
            for rel in range(1, NDEV):
                px = (1 - mx) if rel & 4 else mx
                py = (1 - my) if rel & 2 else my
                pc = (1 - mc) if rel & 1 else mc
                peer = 4 * px + 2 * py + pc
                k = (rel - 1) * n + a
                copies.append(pltpu.make_async_remote_copy(
                    src_ref=x_refs[a].at[peer], dst_ref=out_refs[a].at[me],
                    send_sem=send_sems.at[k], recv_sem=recv_sems.at[k],
                    device_id=(px, py, pc), device_id_type=pl.DeviceIdType.MESH))
        return mine, copies

    def start(self, x_refs, out_refs, sems):
        mine, copies = self._parts(x_refs, out_refs, sems)
        for cp in mine + copies:
            cp.start()

    def middle(self, x_refs, out_refs, sems):
        pass

    def finish(self, x_refs, out_refs, sems):
        mine, copies = self._parts(x_refs, out_refs, sems)
        for cp in copies:
            cp.wait_recv()
        for cp in copies:
            cp.wait_send()
        for cp in mine:
            cp.wait()


class _Riders:
    def __init__(self, riders):
        self.riders = list(riders)
        self.xs = [x for r in self.riders for x in r.xs]
        self.n = len(self.xs)
        self.out_shape = [s for r in self.riders for s in r.out_shape]
        self.scratch = [s for r in self.riders for s in r.scratch]

    def _each(self, method, x_refs, out_refs, sems):
        a = b = 0
        for r in self.riders:
            getattr(r, method)(x_refs[a:a + r.n], out_refs[a:a + r.n], sems[b:b + len(r.scratch)])
            a, b = a + r.n, b + len(r.scratch)

    def start(self, *refs):
        self._each("start", *refs)

    def middle(self, *refs):
        self._each("middle", *refs)

    def finish(self, *refs):
        self._each("finish", *refs)


def _comm_alone(rider, name):
    n = rider.n

    def body(*refs):
        x_refs, out_refs, sems = refs[:n], refs[n:2 * n], refs[2 * n:]
        rider.start(x_refs, out_refs, sems)
        rider.middle(x_refs, out_refs, sems)
        rider.finish(x_refs, out_refs, sems)

    return pl.pallas_call(
        body, name=name, out_shape=tuple(rider.out_shape), in_specs=[_HBM] * n, out_specs=(_HBM,) * n,
        scratch_shapes=rider.scratch,
    )(*rider.xs)


def _with_rider(body, nin, nout, rider, first, mid, last):
    if rider is None:
        return body
    n = rider.n

    def wrapped(*refs):
        ins, x_refs = refs[:nin], refs[nin:nin + n]
        outs, out_refs = refs[nin + n:nin + n + nout], refs[nin + n + nout:nin + 2 * n + nout]
        ns = len(rider.scratch)
        scratch, sems = refs[nin + 2 * n + nout:len(refs) - ns], refs[len(refs) - ns:]

        @pl.when(first())
        def _():
            rider.start(x_refs, out_refs, sems)

        body(*ins, *outs, *scratch)

        @pl.when(mid())
        def _():
            rider.middle(x_refs, out_refs, sems)

        @pl.when(last())
        def _():
            rider.finish(x_refs, out_refs, sems)

    return wrapped


def _mm(a, b, name, ta=False, tb=False, out_dtype=F32, bias=None, precise=False, tm=None, tn=None, tk=None, rider=None):
    m, k = (a.shape[1], a.shape[0]) if ta else a.shape
    n = b.shape[0] if tb else b.shape[1]
    assert k == (b.shape[1] if tb else b.shape[0])
    tm = tm or _pick(m, (1088, 1024, 512, 256, 128))
    tn = tn or _pick(n, (1024, 512, 384, 256, 128))
    tk = tk or _pick(k, (1024, 1088, 512, 256, 128))
    nk = k // tk
    dn = (((0 if ta else 1,), (1 if tb else 0,)), ((), ()))

    def body(*refs):
        if bias is None:
            a_ref, b_ref, o_ref = refs[:3]
            bias_ref = None
        else:
            a_ref, b_ref, bias_ref, o_ref = refs[:4]
        x, y = a_ref[...], b_ref[...]
        if precise:
            p = lax.dot_general(x.astype(F32), y.astype(F32), dn, preferred_element_type=F32, precision=HIGHEST)
        else:
            p = lax.dot_general(x.astype(BF16), y.astype(BF16), dn, preferred_element_type=F32)

        def finish(acc):
            if bias_ref is not None:
                acc = acc + bias_ref[...]
            o_ref[...] = acc.astype(out_dtype)

        if nk == 1:
            finish(p)
        else:
            acc_ref = refs[-1]
            kk = pl.program_id(2)

            @pl.when(kk == 0)
            def _():
                acc_ref[...] = p

            @pl.when(kk > 0)
            def _():
                acc_ref[...] += p

            @pl.when(kk == nk - 1)
            def _():
                finish(acc_ref[...])

    a_spec = pl.BlockSpec((tk, tm), lambda i, j, kk: (kk, i)) if ta else pl.BlockSpec((tm, tk), lambda i, j, kk: (i, kk))
    b_spec = pl.BlockSpec((tn, tk), lambda i, j, kk: (j, kk)) if tb else pl.BlockSpec((tk, tn), lambda i, j, kk: (kk, j))
    in_specs = [a_spec, b_spec]
    args = [a, b]
    if bias is not None:
        in_specs.append(pl.BlockSpec((1, tn), lambda i, j, kk: (0, j)))
        args.append(bias)
    grid = (m // tm, n // tn, nk)
    out_spec = pl.BlockSpec((tm, tn), lambda i, j, kk: (i, j))
    scratch = [pltpu.VMEM((tm, tn), F32)] if nk > 1 else []
    if rider is None:
        return pl.pallas_call(
            body, name=name, grid=grid, in_specs=in_specs, out_specs=out_spec,
            out_shape=jax.ShapeDtypeStruct((m, n), out_dtype), scratch_shapes=scratch, compiler_params=_cparams(3, 56),
        )(*args)

    def at(step):
        return lambda: ((pl.program_id(0) == step[0]) & (pl.program_id(1) == step[1]) & (pl.program_id(2) == step[2]))

    end = tuple(g - 1 for g in grid)
    return pl.pallas_call(
        _with_rider(body, len(args), 1, rider, at((0, 0, 0)), at((grid[0] // 2, 0, 0)), at(end)),
        name=name, grid=grid, in_specs=in_specs + [_HBM] * rider.n, out_specs=(out_spec,) + (_HBM,) * rider.n,
        out_shape=(jax.ShapeDtypeStruct((m, n), out_dtype),) + tuple(rider.out_shape),
        scratch_shapes=scratch + rider.scratch, compiler_params=_cparams(3, 56),
    )(*args, *rider.xs)


def _ada_in(cc):
    def body(c_ref, s_ref, d_ref):
        x = c_ref[...]
        sg = _sigmoid(x)
        s_ref[...] = x * sg
        d_ref[...] = sg * (1.0 + x * (1.0 - sg))

    return pl.pallas_call(body, name="ada_in", out_shape=(jax.ShapeDtypeStruct(cc.shape, F32),) * 2)(cc)


def _cctx_grad(t0, t1, dsilu):
    def body(a_ref, b_ref, d_ref, o_ref):
        o_ref[...] = (a_ref[...] + b_ref[...]) * d_ref[...]

    return pl.pallas_call(body, name="cctx_grad", out_shape=jax.ShapeDtypeStruct(t0.shape, F32))(t0, t1, dsilu)


def _seg_spec(nct, rows=3):
    return pl.BlockSpec((None, rows, D), lambda i: (jnp.where(i >= nct, 1, 0), 0, 0))


def _prenorm_fwd(x, g_pre, mod3, nct, name):
    t = x.shape[0]

    def body(x_ref, g_ref, mod_ref, h_ref):
        xv = x_ref[...]
        r = lax.rsqrt(jnp.mean(xv * xv, axis=-1, keepdims=True) + EPS)
        y = xv * r * g_ref[...]
        h_ref[...] = (y * (1.0 + mod_ref[1:2, :]) + mod_ref[0:1, :]).astype(BF16)

    return pl.pallas_call(
        body, name=name, grid=(t // TM,),
        in_specs=[pl.BlockSpec((TM, D), lambda i: (i, 0)), pl.BlockSpec((1, D), lambda i: (0, 0)), _seg_spec(nct)],
        out_specs=pl.BlockSpec((TM, D), lambda i: (i, 0)),
        out_shape=jax.ShapeDtypeStruct((t, D), BF16), compiler_params=_cparams(1, 32),
    )(x, g_pre, mod3)


def _prenorm_bwd(dh, x, dxo, g_pre, mod3, nct, name):
    t = x.shape[0]

    def body(dh_ref, x_ref, dxo_ref, g_ref, mod_ref, dx_ref, dsh_ref, dsc_ref, dg_ref):
        i = pl.program_id(0)
        xv, dhv, g = x_ref[...], dh_ref[...], g_ref[...]
        r = lax.rsqrt(jnp.mean(xv * xv, axis=-1, keepdims=True) + EPS)
        xh = xv * r
        dy = dhv * (1.0 + mod_ref[1:2, :])
        dxh = dy * g
        dx_ref[...] = dxo_ref[...] + r * (dxh - xh * jnp.mean(dxh * xh, axis=-1, keepdims=True))

        @pl.when((i == 0) | (i == nct))
        def _():
            dsh_ref[...] = jnp.zeros_like(dsh_ref)
            dsc_ref[...] = jnp.zeros_like(dsc_ref)

        @pl.when(i == 0)
        def _():
            dg_ref[...] = jnp.zeros_like(dg_ref)

        dsh_ref[...] += jnp.sum(dhv, axis=0, keepdims=True)
        dsc_ref[...] += jnp.sum(dhv * (xh * g), axis=0, keepdims=True)
        dg_ref[...] += jnp.sum(dy * xh, axis=0, keepdims=True)

    row = pl.BlockSpec((TM, D), lambda i: (i, 0))
    seg8 = pl.BlockSpec((None, 8, D), lambda i: (jnp.where(i >= nct, 1, 0), 0, 0))
    return pl.pallas_call(
        body, name=name, grid=(t // TM,),
        in_specs=[row, row, row, pl.BlockSpec((1, D), lambda i: (0, 0)), _seg_spec(nct)],
        out_specs=(row, seg8, seg8, pl.BlockSpec((8, D), lambda i: (0, 0))),
        out_shape=(jax.ShapeDtypeStruct((t, D), F32), jax.ShapeDtypeStruct((2, 8, D), F32),
                   jax.ShapeDtypeStruct((2, 8, D), F32), jax.ShapeDtypeStruct((8, D), F32)),
        compiler_params=_cparams(1, 32),
    )(dh, x, dxo, g_pre, mod3)


def _post_fwd(x, out, g_post, mod3, nct, name):
    t = x.shape[0]

    def body(x_ref, o_ref, g_ref, mod_ref, y_ref):
        ov = o_ref[...]
        r = lax.rsqrt(jnp.mean(ov * ov, axis=-1, keepdims=True) + EPS)
        y_ref[...] = x_ref[...] + mod_ref[2:3, :] * (ov * r * g_ref[...])

    row = pl.BlockSpec((TM, D), lambda i: (i, 0))
    return pl.pallas_call(
        body, name=name, grid=(t // TM,),
        in_specs=[row, row, pl.BlockSpec((1, D), lambda i: (0, 0)), _seg_spec(nct)],
        out_specs=row, out_shape=jax.ShapeDtypeStruct((t, D), F32), compiler_params=_cparams(1, 32),
    )(x, out, g_post, mod3)


def _post_bwd(dxo, out, g_post, mod3, nct, name):
    t = out.shape[0]

    def body(dx_ref, o_ref, g_ref, mod_ref, do_ref, dgt_ref, dg_ref):
        i = pl.program_id(0)
        ov, dxv, g = o_ref[...], dx_ref[...], g_ref[...]
        r = lax.rsqrt(jnp.mean(ov * ov, axis=-1, keepdims=True) + EPS)
        nh = ov * r
        dn = dxv * mod_ref[2:3, :]
        dnh = dn * g
        do_ref[...] = (r * (dnh - nh * jnp.mean(dnh * nh, axis=-1, keepdims=True))).astype(BF16)

        @pl.when((i == 0) | (i == nct))
        def _():
            dgt_ref[...] = jnp.zeros_like(dgt_ref)

        @pl.when(i == 0)
        def _():
            dg_ref[...] = jnp.zeros_like(dg_ref)

        dgt_ref[...] += jnp.sum(dxv * (nh * g), axis=0, keepdims=True)
        dg_ref[...] += jnp.sum(dn * nh, axis=0, keepdims=True)

    row = pl.BlockSpec((TM, D), lambda i: (i, 0))
    seg8 = pl.BlockSpec((None, 8, D), lambda i: (jnp.where(i >= nct, 1, 0), 0, 0))
    return pl.pallas_call(
        body, name=name, grid=(t // TM,),
        in_specs=[row, row, pl.BlockSpec((1, D), lambda i: (0, 0)), _seg_spec(nct)],
        out_specs=(row, seg8, pl.BlockSpec((8, D), lambda i: (0, 0))),
        out_shape=(jax.ShapeDtypeStruct((t, D), BF16), jax.ShapeDtypeStruct((2, 8, D), F32),
                   jax.ShapeDtypeStruct((8, D), F32)),
        compiler_params=_cparams(1, 32),
    )(dxo, out, g_post, mod3)


def _loss_grad(y, target, nct, name):
    t = y.shape[0]

    def body(y_ref, t_ref, dy_ref, l_ref):
        i = pl.program_id(0)

        @pl.when(i == 0)
        def _():
            l_ref[...] = jnp.zeros_like(l_ref)

        @pl.when(i < nct)
        def _():
            dy_ref[...] = jnp.zeros_like(dy_ref)

        @pl.when(i >= nct)
        def _():
            err = y_ref[...] - t_ref[...]
            dy_ref[...] = err / D
            l_ref[...] += jnp.sum(jnp.sum(err * err, axis=1, keepdims=True), axis=0, keepdims=True)

    row = pl.BlockSpec((TM, D), lambda i: (i, 0))
    return pl.pallas_call(
        body, name=name, grid=(t // TM,),
        in_specs=[row, pl.BlockSpec((TM, D), lambda i: (jnp.maximum(i - nct, 0), 0))],
        out_specs=(row, pl.BlockSpec((8, LANE), lambda i: (0, 0))),
        out_shape=(jax.ShapeDtypeStruct((t, D), F32), jax.ShapeDtypeStruct((8, LANE), F32)),
        compiler_params=_cparams(1, 32),
    )(y, target)


def _pcol(name, width):
    assert OFF[name] % width == 0
    blk = OFF[name] // width
    return pl.BlockSpec((TM, width), lambda i: (i, blk))


def _shift_rows(u, prev_row, next_row):
    n = u.shape[0]
    row = lax.broadcasted_iota(jnp.int32, u.shape, 0)
    prev = jnp.where(row == 0, prev_row, pltpu.roll(u, 1, 0))
    nxt = jnp.where(row == n - 1, next_row, pltpu.roll(u, n - 1, 0))
    return prev, nxt


def _halo_specs(width, nt, blk=0, rows=8):
    per = TM // rows
    prev = pl.BlockSpec((rows, width), lambda i: (jnp.maximum(i * per - 1, 0), blk))
    nxt = pl.BlockSpec((rows, width), lambda i: (jnp.minimum((i + 1) * per, nt * per - 1), blk))
    return prev, nxt


def _conv_fwd(p, conv_w8, nct, name):
    t = p.shape[0]
    nt = t // TM

    def body(ab_ref, ac_ref, ax_ref, az_ref, acp_ref, axp_ref, acn_ref, axn_ref, w_ref, cv_ref, ya_ref):
        i = pl.program_id(0)
        def f(ref, rows=slice(None)):
            return ref[rows, :].astype(F32)

        u = f(ac_ref) * f(ax_ref)
        mp = jnp.where((i == 0) | (i == nct), 0.0, 1.0)
        mn = jnp.where((i == nct - 1) | (i == nt - 1), 0.0, 1.0)
        last, first = slice(P_HALO - 1, P_HALO), slice(0, 1)
        prev, nxt = _shift_rows(u, f(acp_ref, last) * f(axp_ref, last) * mp, f(acn_ref, first) * f(axn_ref, first) * mn)
        cv = w_ref[0:1, :] * prev + w_ref[1:2, :] * u + w_ref[2:3, :] * nxt
        az = f(az_ref)
        cv_ref[...] = cv
        ya_ref[...] = (f(ab_ref) * cv * (az * _sigmoid(az))).astype(BF16)

    acp, acn = _halo_specs(D, nt, OFF["a_c"] // D, P_HALO)
    axp, axn = _halo_specs(D, nt, OFF["a_x"] // D, P_HALO)
    row = pl.BlockSpec((TM, D), lambda i: (i, 0))
    return pl.pallas_call(
        body, name=name, grid=(nt,),
        in_specs=[_pcol("a_b", D), _pcol("a_c", D), _pcol("a_x", D), _pcol("a_z", D), acp, axp, acn, axn,
                  pl.BlockSpec((8, D), lambda i: (0, 0))],
        out_specs=(row, row),
        out_shape=(jax.ShapeDtypeStruct((t, D), F32), jax.ShapeDtypeStruct((t, D), BF16)),
        compiler_params=_cparams(1, 40),
    )(p, p, p, p, p, p, p, p, conv_w8)


def _conv_bwd_a(dya, p, cv, name):
    t = p.shape[0]

    def body(dy_ref, ab_ref, az_ref, cv_ref, dcv_ref, dab_ref, daz_ref):
        dy, ab, az, c = dy_ref[...], ab_ref[...].astype(F32), az_ref[...].astype(F32), cv_ref[...]
        sg = _sigmoid(az)
        sz = az * sg
        dcv_ref[...] = dy * ab * sz
        dab_ref[...] = (dy * c * sz).astype(BF16)
        daz_ref[...] = (dy * ab * c * (sg * (1.0 + az * (1.0 - sg)))).astype(BF16)

    row = pl.BlockSpec((TM, D), lambda i: (i, 0))
    return pl.pallas_call(
        body, name=name, grid=(t // TM,),
        in_specs=[row, _pcol("a_b", D), _pcol("a_z", D), row], out_specs=(row, row, row),
        out_shape=(jax.ShapeDtypeStruct((t, D), F32), jax.ShapeDtypeStruct((t, D), BF16),
                   jax.ShapeDtypeStruct((t, D), BF16)),
        compiler_params=_cparams(1, 40),
    )(dya, p, p, cv)


def _conv_bwd_b(dcv, p, conv_w8, nct, name):
    t = p.shape[0]
    nt = t // TM

    def body(dcv_ref, dp_ref, dn_ref, ac_ref, ax_ref, w_ref, dac_ref, dax_ref, dw_ref):
        i = pl.program_id(0)
        d, ac, ax = dcv_ref[...], ac_ref[...].astype(F32), ax_ref[...].astype(F32)
        u = ac * ax
        mp = jnp.where((i == 0) | (i == nct), 0.0, 1.0)
        mn = jnp.where((i == nct - 1) | (i == nt - 1), 0.0, 1.0)
        dprev, dnxt = _shift_rows(d, dp_ref[7:8, :] * mp, dn_ref[0:1, :] * mn)
        du = w_ref[0:1, :] * dnxt + w_ref[1:2, :] * d + w_ref[2:3, :] * dprev
        dac_ref[...] = (du * ax).astype(BF16)
        dax_ref[...] = (du * ac).astype(BF16)

        @pl.when(i == 0)
        def _():
            dw_ref[...] = jnp.zeros_like(dw_ref)

        dw0 = jnp.sum(u * dnxt, axis=0, keepdims=True)
        dw1 = jnp.sum(u * d, axis=0, keepdims=True)
        dw2 = jnp.sum(u * dprev, axis=0, keepdims=True)
        r8 = lax.broadcasted_iota(jnp.int32, (8, D), 0)
        dw_ref[...] += jnp.where(r8 == 0, dw0, jnp.where(r8 == 1, dw1, jnp.where(r8 == 2, dw2, 0.0)))

    dp, dn = _halo_specs(D, nt)
    row = pl.BlockSpec((TM, D), lambda i: (i, 0))
    return pl.pallas_call(
        body, name=name, grid=(nt,),
        in_specs=[row, dp, dn, _pcol("a_c", D), _pcol("a_x", D), pl.BlockSpec((8, D), lambda i: (0, 0))],
        out_specs=(row, row, pl.BlockSpec((8, D), lambda i: (0, 0))),
        out_shape=(jax.ShapeDtypeStruct((t, D), BF16), jax.ShapeDtypeStruct((t, D), BF16),
                   jax.ShapeDtypeStruct((8, D), F32)),
        compiler_params=_cparams(1, 40),
    )(dcv, dcv, dcv, p, p, conv_w8)


def _rot_half(x):
    lane = lax.broadcasted_iota(jnp.int32, x.shape, 1)
    return jnp.where((lane % 64) < 32, pltpu.roll(x, 96, 1), pltpu.roll(x, 32, 1))


def _qk_prep_fwd(p, qg, kg, cos_t, sin_t, name):
    t = p.shape[0]

    def body(q_ref, k_ref, qg_ref, kg_ref, c_ref, s_ref, qo_ref, ko_ref):
        c, s = c_ref[...], s_ref[...]

        def one(xv, g, scale):
            y = xv * lax.rsqrt(jnp.mean(xv * xv, axis=-1, keepdims=True) + EPS) * g
            return ((y * c + _rot_half(y) * s) * scale).astype(BF16)

        for h in range(NH):
            qo_ref[:, h * HD:(h + 1) * HD] = one(q_ref[:, h * HD:(h + 1) * HD].astype(F32), qg_ref[...], Q_FOLD)
        for h in range(NKV):
            ko_ref[:, h * HD:(h + 1) * HD] = one(k_ref[:, h * HD:(h + 1) * HD].astype(F32), kg_ref[...], 1.0)

    vec = pl.BlockSpec((1, HD), lambda i: (0, 0))
    tab = pl.BlockSpec((TM, HD), lambda i: (i, 0))
    return pl.pallas_call(
        body, name=name, grid=(t // TM,),
        in_specs=[_pcol("q", NH * HD), _pcol("k", NKV * HD), vec, vec, tab, tab],
        out_specs=(pl.BlockSpec((TM, NH * HD), lambda i: (i, 0)), pl.BlockSpec((TM, NKV * HD), lambda i: (i, 0))),
        out_shape=(jax.ShapeDtypeStruct((t, NH * HD), BF16), jax.ShapeDtypeStruct((t, NKV * HD), BF16)),
        compiler_params=_cparams(1, 32),
    )(p, p, qg, kg, cos_t, sin_t)


def _qk_prep_bwd(dqr, dkr, p, qg, kg, cos_t, sin_t, name):
    t = p.shape[0]

    def body(dq_ref, dk_ref, q_ref, k_ref, qg_ref, kg_ref, c_ref, s_ref, dqo_ref, dko_ref, dqg_ref, dkg_ref):
        i = pl.program_id(0)
        c, s = c_ref[...], s_ref[...]

        @pl.when(i == 0)
        def _():
            dqg_ref[...] = jnp.zeros_like(dqg_ref)
            dkg_ref[...] = jnp.zeros_like(dkg_ref)

        def one(dyr, xv, g):
            dy = dyr * c + _rot_half(dyr * s)
            r = lax.rsqrt(jnp.mean(xv * xv, axis=-1, keepdims=True) + EPS)
            xh = xv * r
            dxh = dy * g
            dx = r * (dxh - xh * jnp.mean(dxh * xh, axis=-1, keepdims=True))
            return dx.astype(BF16), jnp.sum(dy * xh, axis=0, keepdims=True)

        for h in range(NH):
            sl = slice(h * HD, (h + 1) * HD)
            dx, dg = one(dq_ref[:, sl] * ATTN_SCALE, q_ref[:, sl].astype(F32), qg_ref[...])
            dqo_ref[:, sl] = dx
            dqg_ref[...] += dg
        for h in range(NKV):
            sl = slice(h * HD, (h + 1) * HD)
            dx, dg = one(dk_ref[:, sl] * (ATTN_SCALE / Q_FOLD), k_ref[:, sl].astype(F32), kg_ref[...])
            dko_ref[:, sl] = dx
            dkg_ref[...] += dg

    vec = pl.BlockSpec((1, HD), lambda i: (0, 0))
    tab = pl.BlockSpec((TM, HD), lambda i: (i, 0))
    acc = pl.BlockSpec((8, HD), lambda i: (0, 0))
    qrow = pl.BlockSpec((TM, NH * HD), lambda i: (i, 0))
    krow = pl.BlockSpec((TM, NKV * HD), lambda i: (i, 0))
    return pl.pallas_call(
        body, name=name, grid=(t // TM,),
        in_specs=[qrow, krow, _pcol("q", NH * HD), _pcol("k", NKV * HD), vec, vec, tab, tab],
        out_specs=(qrow, krow, acc, acc),
        out_shape=(jax.ShapeDtypeStruct((t, NH * HD), BF16), jax.ShapeDtypeStruct((t, NKV * HD), BF16),
                   jax.ShapeDtypeStruct((8, HD), F32), jax.ShapeDtypeStruct((8, HD), F32)),
        compiler_params=_cparams(1, 32),
    )(dqr, dkr, p, p, qg, kg, cos_t, sin_t)


def _key_chunks(n):
    c = max(c for c in range(LANE, min(n, ATTN_KEY_CHUNK) + 1, LANE) if n % c == 0)
    return [(lo, lo + c) for lo in range(0, n, c)]


def _attn_fwd(qr, kr, p, nct, name, rider=None):
    t = qr.shape[0]
    nt = t // TM
    ctx = nct * TM
    vblk = OFF["v"] // HD
    hps = ATTN_HEADS_PER_STEP
    nhp, per_kv = NH // hps, GROUP // hps

    def body(q_ref, k_ref, v_ref, o_ref, lse_ref):
        def tile(nkeys):
            for j in range(hps):
                sl = slice(j * HD, (j + 1) * HD)
                q = q_ref[:, sl]
                m = l = acc = None
                for lo, hi in _key_chunks(nkeys):
                    s = lax.dot_general(q, k_ref[lo:hi, :], _NT, preferred_element_type=F32)
                    mc = jnp.max(s, axis=-1, keepdims=True)
                    m_new = mc if m is None else jnp.maximum(m, mc)
                    e = jnp.exp2(s - m_new)
                    lc = jnp.sum(e, axis=-1, keepdims=True)
                    pv = jnp.dot(e.astype(BF16), v_ref[lo:hi, :].astype(BF16), preferred_element_type=F32)
                    if m is None:
                        l, acc = lc, pv
                    else:
                        alpha = jnp.exp2(m - m_new)
                        l, acc = l * alpha + lc, acc * alpha + pv
                    m = m_new
                o_ref[:, sl] = acc / l
                lse_ref[:, j:j + 1] = m + jnp.log2(l)

        pl.when(pl.program_id(1) < nct)(lambda: tile(ctx))
        pl.when(pl.program_id(1) >= nct)(lambda: tile(t))

    def at(h, i):
        return lambda: (pl.program_id(0) == h) & (pl.program_id(1) == i)

    rn = 0 if rider is None else rider.n
    qspec = pl.BlockSpec((TM, hps * HD), lambda h, i: (i, h))
    return pl.pallas_call(
        _with_rider(body, 3, 2, rider, at(0, 0), at(nhp * 3 // 4, 0), at(nhp - 1, nt - 1)),
        name=name, grid=(nhp, nt),
        in_specs=[qspec, pl.BlockSpec((t, HD), lambda h, i: (0, h // per_kv)),
                  pl.BlockSpec((t, HD), lambda h, i: (0, vblk + h // per_kv))] + [_HBM] * rn,
        out_specs=(qspec, pl.BlockSpec((None, TM, hps), lambda h, i: (h, i, 0))) + (_HBM,) * rn,
        out_shape=(jax.ShapeDtypeStruct((t, NH * HD), F32), jax.ShapeDtypeStruct((nhp, t, hps), F32))
        + (() if rider is None else tuple(rider.out_shape)),
        scratch_shapes=[] if rider is None else rider.scratch,
        compiler_params=_cparams(2, 48),
    )(qr, kr, p, *(() if rider is None else rider.xs))


def _attn_bwd(qr, kr, p, o, lse, do, nct, name, rider=None):
    t = qr.shape[0]
    nt = t // TM
    ctx = nct * TM
    vblk = OFF["v"] // HD
    hps = ATTN_HEADS_PER_STEP

    def body(q_ref, k_ref, v_ref, o_ref, lse_ref, do_ref, dq_ref, dk_ref, dv_ref):
        g, i = pl.program_id(1), pl.program_id(2)

        @pl.when((g == 0) & (i == 0))
        def _():
            dk_ref[...] = jnp.zeros_like(dk_ref)
            dv_ref[...] = jnp.zeros_like(dv_ref)

        def tile(nkeys):
            heads = []
            for j in range(hps):
                sl = slice(j * HD, (j + 1) * HD)
                dov = do_ref[:, sl]
                drow = jnp.sum(dov * o_ref[:, sl], axis=-1, keepdims=True)
                heads.append((sl, q_ref[:, sl], dov.astype(BF16), drow, lse_ref[:, j:j + 1]))
            dq = [None] * hps
            for lo, hi in _key_chunks(nkeys):
                k = k_ref[lo:hi, :]
                vb = v_ref[lo:hi, :].astype(BF16)
                dk_c = dv_c = None
                for j, (sl, q, dob, drow, lse_j) in enumerate(heads):
                    s = lax.dot_general(q, k, _NT, preferred_element_type=F32)
                    pr = jnp.exp2(s - lse_j)
                    dp = lax.dot_general(dob, vb, _NT, preferred_element_type=F32)
                    ds = (pr * (dp - drow)).astype(BF16)
                    dq_c = jnp.dot(ds, k, preferred_element_type=F32)
                    dq[j] = dq_c if dq[j] is None else dq[j] + dq_c
                    dk_j = lax.dot_general(ds, q, _TN, preferred_element_type=F32)
                    dv_j = lax.dot_general(pr.astype(BF16), dob, _TN, preferred_element_type=F32)
                    dk_c = dk_j if dk_c is None else dk_c + dk_j
                    dv_c = dv_j if dv_c is None else dv_c + dv_j
                dk_ref[lo:hi, :] += dk_c
                dv_ref[lo:hi, :] += dv_c
            for j, (sl, *_) in enumerate(heads):
                dq_ref[:, sl] = dq[j]

        pl.when(i < nct)(lambda: tile(ctx))
        pl.when(i >= nct)(lambda: tile(t))

    def at(kv, g, i):
        return lambda: (pl.program_id(0) == kv) & (pl.program_id(1) == g) & (pl.program_id(2) == i)

    rn = 0 if rider is None else rider.n
    per_kv = GROUP // hps
    qspec = pl.BlockSpec((TM, hps * HD), lambda kv, g, i: (i, kv * per_kv + g))
    kvspec = pl.BlockSpec((t, HD), lambda kv, g, i: (0, kv))
    lspec = pl.BlockSpec((None, TM, hps), lambda kv, g, i: (kv * per_kv + g, i, 0))
    return pl.pallas_call(
        _with_rider(body, 6, 3, rider, at(0, 0, 0), at(NKV - 1, 0, 0), at(NKV - 1, per_kv - 1, nt - 1)),
        name=name, grid=(NKV, per_kv, nt),
        in_specs=[qspec, kvspec, pl.BlockSpec((t, HD), lambda kv, g, i: (0, vblk + kv)), qspec, lspec, qspec]
        + [_HBM] * rn,
        out_specs=(qspec, kvspec, kvspec) + (_HBM,) * rn,
        out_shape=(jax.ShapeDtypeStruct((t, NH * HD), F32), jax.ShapeDtypeStruct((t, NKV * HD), F32),
                   jax.ShapeDtypeStruct((t, NKV * HD), F32)) + (() if rider is None else tuple(rider.out_shape)),
        scratch_shapes=[] if rider is None else rider.scratch,
        compiler_params=_cparams(3, 48),
    )(qr, kr, p, o, lse, do, *(() if rider is None else rider.xs))


def _decay_fwd(p, wd, bd, name):
    t = p.shape[0]

    def body(r_ref, w_ref, b_ref, z_ref, la_ref):
        z = jnp.dot(r_ref[...].astype(BF16), w_ref[...].astype(BF16), preferred_element_type=F32) + b_ref[...]
        z_ref[...] = z
        la_ref[...] = (jnp.minimum(z, 0.0) - jnp.log(1.0 + jnp.exp(-jnp.abs(z)))) / GLA_TAU

    row = pl.BlockSpec((TM, D), lambda i: (i, 0))
    return pl.pallas_call(
        body, name=name, grid=(t // TM,),
        in_specs=[_pcol("r", R_PAD), pl.BlockSpec((R_PAD, D), lambda i: (0, 0)), pl.BlockSpec((1, D), lambda i: (0, 0))],
        out_specs=(row, row),
        out_shape=(jax.ShapeDtypeStruct((t, D), F32), jax.ShapeDtypeStruct((t, D), F32)),
        compiler_params=_cparams(1, 32),
    )(p, wd, bd)


def _chunk_order(s, ncc, nc, rev):
    if not rev:
        return s
    return jnp.where(s < ncc, ncc - 1 - s, nc - 1 - (s - ncc))


def _gla_chains(dirs):
    return [(rev, d, h) + tuple(refs) for d, (rev, *refs) in enumerate(dirs) for h in range(GH)]


def _hk(h):
    return slice(h * GDK, (h + 1) * GDK)


def _hv(h):
    return slice(h * GDV, (h + 1) * GDV)


def _gla_factors(qs, ks, las, revs):
    r = lax.broadcasted_iota(jnp.int32, (CH, CH), 0)
    c = lax.broadcasted_iota(jnp.int32, (CH, CH), 1)
    keeps = [(c >= r) if rev else (c <= r) for rev in revs]
    bcs = [jnp.dot(keep.astype(F32), la, preferred_element_type=F32, precision=HIGHEST) for keep, la in zip(keeps, las)]
    bls = [jnp.sum(la, axis=0, keepdims=True) for la in las]
    qs, ks = [q.astype(F32) for q in qs], [k.astype(F32) for k in ks]
    qts = [q * GLA_SCALE * jnp.exp(bc) for q, bc in zip(qs, bcs)]
    kts = [k * jnp.exp(-bc) for k, bc in zip(ks, bcs)]
    khs = [k * jnp.exp(bl - bc) for k, bl, bc in zip(ks, bls, bcs)]
    gls = [jnp.exp(bl) for bl in bls]
    return qts, kts, gls, khs, keeps, bcs


_NT = (((1,), (1,)), ((), ()))
_TN = (((0,), (0,)), ((), ()))


def _gla_specs(ncc, nc, rev, backward):
    def idx(s):
        return _chunk_order((nc - 1 - s) if backward else s, ncc, nc, rev)

    wk, wv = GH * GDK, GH * GDV
    qb, kb, vb = OFF["gq"] // wk, OFF["gk"] // wk, OFF["gv"] // wv
    lab = 1 if rev else 0
    q = pl.BlockSpec((CH, wk), lambda s: (idx(s), qb))
    k = pl.BlockSpec((CH, wk), lambda s: (idx(s), kb))
    v = pl.BlockSpec((CH, wv), lambda s: (idx(s), vb))
    la = pl.BlockSpec((CH, wk), lambda s: (idx(s), lab))
    o = pl.BlockSpec((CH, wv), lambda s: (idx(s), 0))
    dk = pl.BlockSpec((CH, wk), lambda s: (idx(s), 0))
    st = pl.BlockSpec((None, GH, GDV, GDK), lambda s: (idx(s), 0, 0, 0))
    return q, k, v, la, o, dk, st


def _gla_fwd(p, la, ncc, name):
    t = p.shape[0]
    nc = t // CH
    specs = [_gla_specs(ncc, nc, rev, False) for rev in (False, True)]

    def body(qf, kf, vf, laf, qb_, kb_, vb_, lab, of, stf, ob, stb, s_scr):
        @pl.when(pl.program_id(0) == 0)
        def _():
            s_scr[...] = jnp.zeros_like(s_scr)

        ch = _gla_chains(((False, qf, kf, vf, laf, of, stf), (True, qb_, kb_, vb_, lab, ob, stb)))
        qts, kts, gls, khs, keeps, _ = _gla_factors([c[3][:, _hk(c[2])] for c in ch], [c[4][:, _hk(c[2])] for c in ch],
                                                    [c[6][:, _hk(c[2])] for c in ch], [c[0] for c in ch])
        sts = [s_scr[c[1], c[2]] for c in ch]
        for c, st in zip(ch, sts):
            c[8][c[2]] = st
        vbs = [c[5][:, _hv(c[2])].astype(BF16) for c in ch]
        qbs = [qt.astype(BF16) for qt in qts]
        a_s = [jnp.where(keep, lax.dot_general(qb, kt.astype(BF16), _NT, preferred_element_type=F32), 0.0)
               for keep, qb, kt in zip(keeps, qbs, kts)]
        inter = [lax.dot_general(qb, st.astype(BF16), _NT, preferred_element_type=F32) for qb, st in zip(qbs, sts)]
        intra = [jnp.dot(a.astype(BF16), vb, preferred_element_type=F32) for a, vb in zip(a_s, vbs)]
        for c, x, y in zip(ch, inter, intra):
            c[7][:, _hv(c[2])] = x + y
        upd = [lax.dot_general(vb, kh.astype(BF16), _TN, preferred_element_type=F32) for vb, kh in zip(vbs, khs)]
        for c, st, gl, u in zip(ch, sts, gls, upd):
            s_scr[c[1], c[2]] = st * gl + u

    o_shape = jax.ShapeDtypeStruct((t, GH * GDV), F32)
    st_shape = jax.ShapeDtypeStruct((nc, GH, GDV, GDK), F32)
    return pl.pallas_call(
        body, name=name, grid=(nc,),
        in_specs=[sp for s_ in specs for sp in s_[:4]],
        out_specs=tuple(sp for s_ in specs for sp in (s_[4], s_[6])),
        out_shape=(o_shape, st_shape, o_shape, st_shape),
        scratch_shapes=[pltpu.VMEM((2, GH, GDV, GDK), F32)], compiler_params=_cparams(1, 32),
    )(p, p, p, la, p, p, p, la)


def _gla_bwd(p, la, do, stf, stb, ncc, name):
    t = p.shape[0]
    nc = t // CH
    specs = [_gla_specs(ncc, nc, rev, True) for rev in (False, True)]

    def mm(xs, ys, dims=None):
        if dims is None:
            return [jnp.dot(x, y, preferred_element_type=F32) for x, y in zip(xs, ys)]
        return [lax.dot_general(x, y, dims, preferred_element_type=F32) for x, y in zip(xs, ys)]

    def body(*refs):
        ins_f, ins_b, outs_f, outs_b, ds_scr = refs[0:6], refs[6:12], refs[12:16], refs[16:20], refs[20]

        @pl.when(pl.program_id(0) == 0)
        def _():
            ds_scr[...] = jnp.zeros_like(ds_scr)

        ch = _gla_chains(((False, *ins_f, *outs_f), (True, *ins_b, *outs_b)))
        revs = [c[0] for c in ch]
        qts, kts, gls, khs, keeps, bcs = _gla_factors([c[3][:, _hk(c[2])] for c in ch], [c[4][:, _hk(c[2])] for c in ch],
                                                      [c[6][:, _hk(c[2])] for c in ch], revs)
        stvs = [c[8][c[2]].astype(BF16) for c in ch]
        dsns = [ds_scr[c[1], c[2]] for c in ch]
        dsbs = [x.astype(BF16) for x in dsns]
        vbs = [c[5][:, _hv(c[2])].astype(BF16) for c in ch]
        dobs = [c[7][:, _hv(c[2])].astype(BF16) for c in ch]
        qbs, kbs = [x.astype(BF16) for x in qts], [x.astype(BF16) for x in kts]
        a_s = [jnp.where(keep, x, 0.0).astype(BF16) for keep, x in zip(keeps, mm(qbs, kbs, _NT))]
        das = [jnp.where(keep, x, 0.0).astype(BF16) for keep, x in zip(keeps, mm(dobs, vbs, _NT))]
        dqts = [x + y for x, y in zip(mm(dobs, stvs), mm(das, kbs))]
        dkhs = mm(vbs, dsbs)
        dkts = [x + dkh * gl for x, dkh, gl in zip(mm(das, qbs, _TN), dkhs, gls)]
        for c, x, y in zip(ch, mm(a_s, dobs, _TN), mm([kh.astype(BF16) for kh in khs], dsbs, _NT)):
            c[11][:, _hv(c[2])] = x + y
        for c, x, dsn, gl in zip(ch, mm(dobs, qbs, _TN), dsns, gls):
            ds_scr[c[1], c[2]] = x + dsn * gl
        dgls = [jnp.sum(c[8][c[2]] * dsn, axis=0, keepdims=True) + jnp.sum(dkh * kt, axis=0, keepdims=True)
                for c, dsn, dkh, kt in zip(ch, dsns, dkhs, kts)]
        row = lax.broadcasted_iota(jnp.int32, (CH, GDK), 0)
        dbcs = [dqt * qt - dkt * kt + jnp.where(row == (0 if rev else CH - 1), dgl * gl, 0.0)
                for rev, dqt, qt, dkt, kt, dgl, gl in zip(revs, dqts, qts, dkts, kts, dgls, gls)]
        r = lax.broadcasted_iota(jnp.int32, (CH, CH), 0)
        c_ = lax.broadcasted_iota(jnp.int32, (CH, CH), 1)
        dlas = [jnp.dot(((c_ <= r) if rev else (c_ >= r)).astype(F32), dbc, preferred_element_type=F32, precision=HIGHEST)
                for rev, dbc in zip(revs, dbcs)]
        for c, dla, dqt, dkt, bc in zip(ch, dlas, dqts, dkts, bcs):
            c[12][:, _hk(c[2])] = dla
            c[9][:, _hk(c[2])] = dqt * (GLA_SCALE * jnp.exp(bc))
            c[10][:, _hk(c[2])] = dkt * jnp.exp(-bc)

    k_shape = jax.ShapeDtypeStruct((t, GH * GDK), F32)
    v_shape = jax.ShapeDtypeStruct((t, GH * GDV), F32)
    res = pl.pallas_call(
        body, name=name, grid=(nc,),
        in_specs=[sp for q_s, k_s, v_s, la_s, o_s, _, st_s in specs for sp in (q_s, k_s, v_s, la_s, o_s, st_s)],
        out_specs=tuple(sp for _, _, _, _, o_s, dk_s, _ in specs for sp in (dk_s, dk_s, o_s, dk_s)),
        out_shape=(k_shape, k_shape, v_shape, k_shape) * 2,
        scratch_shapes=[pltpu.VMEM((2, GH, GDV, GDK), F32)], compiler_params=_cparams(1, 32),
    )(p, p, p, la, do, stf, p, p, p, la, do, stb)
    return res[:4], res[4:]


def _gla_merge_bwd(gf, gb, z, p, wd, name):
    t = p.shape[0]
    w2 = GH * GDK

    def body(dqf, dkf, dvf, dlf, dqb, dkb, dvb, dlb, z_ref, r_ref, w_ref, dq_ref, dk_ref, dv_ref, dr_ref, db_ref, dw_ref):
        i = pl.program_id(0)
        dq_ref[...] = (dqf[...] + dqb[...]).astype(BF16)
        dk_ref[...] = (dkf[...] + dkb[...]).astype(BF16)
        dv_ref[...] = (dvf[...] + dvb[...]).astype(BF16)
        zv = z_ref[...]
        dz = jnp.concatenate([dlf[...], dlb[...]], axis=1) * (_sigmoid(-zv) / GLA_TAU)
        dzb = dz.astype(BF16)
        dr_ref[...] = lax.dot_general(dzb, w_ref[...].astype(BF16), _NT, preferred_element_type=F32).astype(BF16)

        @pl.when(i == 0)
        def _():
            db_ref[...] = jnp.zeros_like(db_ref)
            dw_ref[...] = jnp.zeros_like(dw_ref)

        db_ref[...] += jnp.sum(dz, axis=0, keepdims=True)
        dw_ref[...] += lax.dot_general(r_ref[...].astype(BF16), dzb, _TN, preferred_element_type=F32)

    half = pl.BlockSpec((TM, w2), lambda i: (i, 0))
    row = pl.BlockSpec((TM, D), lambda i: (i, 0))
    wspec = pl.BlockSpec((R_PAD, D), lambda i: (0, 0))
    return pl.pallas_call(
        body, name=name, grid=(t // TM,),
        in_specs=[half, half, row, half, half, half, row, half, row, _pcol("r", R_PAD), wspec],
        out_specs=(half, half, row, pl.BlockSpec((TM, R_PAD), lambda i: (i, 0)),
                   pl.BlockSpec((8, D), lambda i: (0, 0)), wspec),
        out_shape=(jax.ShapeDtypeStruct((t, w2), BF16), jax.ShapeDtypeStruct((t, w2), BF16),
                   jax.ShapeDtypeStruct((t, D), BF16), jax.ShapeDtypeStruct((t, R_PAD), BF16),
                   jax.ShapeDtypeStruct((8, D), F32), jax.ShapeDtypeStruct((R_PAD, D), F32)),
        compiler_params=_cparams(1, 40),
    )(*gf, *gb, z, p, wd)


def _branch_fwd(att, of, ob, p, gla_g, name):
    t = p.shape[0]

    def body(att_ref, of_ref, ob_ref, za_ref, zg_ref, g_ref, yb_ref, yc_ref):
        za = za_ref[...].astype(F32)
        yb_ref[...] = (att_ref[...] * (za * _sigmoid(za))).astype(BF16)
        for h in range(GH):
            sl = slice(h * GDV, (h + 1) * GDV)
            o = of_ref[:, sl] + ob_ref[:, sl]
            n = o * lax.rsqrt(jnp.mean(o * o, axis=-1, keepdims=True) + EPS) * g_ref[...]
            zh = zg_ref[:, sl].astype(F32)
            yc_ref[:, sl] = (n * (zh * _sigmoid(zh))).astype(BF16)

    row = pl.BlockSpec((TM, D), lambda i: (i, 0))
    return pl.pallas_call(
        body, name=name, grid=(t // TM,),
        in_specs=[row, row, row, _pcol("z_attn", D), _pcol("zg", D), pl.BlockSpec((1, GDV), lambda i: (0, 0))],
        out_specs=(row, row),
        out_shape=(jax.ShapeDtypeStruct((t, D), BF16), jax.ShapeDtypeStruct((t, D), BF16)),
        compiler_params=_cparams(1, 40),
    )(att, of, ob, p, p, gla_g)


def _branch_bwd(dyb, dyc, att, of, ob, p, gla_g, name):
    t = p.shape[0]

    def body(dyb_ref, dyc_ref, att_ref, of_ref, ob_ref, za_ref, zg_ref, g_ref, datt_ref, dza_ref, do_ref, dzg_ref, dg_ref):
        i = pl.program_id(0)

        @pl.when(i == 0)
        def _():
            dg_ref[...] = jnp.zeros_like(dg_ref)

        za, dyb = za_ref[...].astype(F32), dyb_ref[...]
        sa = _sigmoid(za)
        datt_ref[...] = dyb * (za * sa)
        dza_ref[...] = (dyb * att_ref[...] * (sa * (1.0 + za * (1.0 - sa)))).astype(BF16)
        g = g_ref[...]
        for h in range(GH):
            sl = slice(h * GDV, (h + 1) * GDV)
            o = of_ref[:, sl] + ob_ref[:, sl]
            r = lax.rsqrt(jnp.mean(o * o, axis=-1, keepdims=True) + EPS)
            oh = o * r
            zh, dyc = zg_ref[:, sl].astype(F32), dyc_ref[:, sl]
            sg = _sigmoid(zh)
            dn = dyc * (zh * sg)
            dzg_ref[:, sl] = (dyc * (oh * g) * (sg * (1.0 + zh * (1.0 - sg)))).astype(BF16)
            doh = dn * g
            do_ref[:, sl] = r * (doh - oh * jnp.mean(doh * oh, axis=-1, keepdims=True))
            dg_ref[...] += jnp.sum(dn * oh, axis=0, keepdims=True)

    row = pl.BlockSpec((TM, D), lambda i: (i, 0))
    return pl.pallas_call(
        body, name=name, grid=(t // TM,),
        in_specs=[row, row, row, row, row, _pcol("z_attn", D), _pcol("zg", D), pl.BlockSpec((1, GDV), lambda i: (0, 0))],
        out_specs=(row, row, row, row, pl.BlockSpec((8, GDV), lambda i: (0, 0))),
        out_shape=(jax.ShapeDtypeStruct((t, D), F32), jax.ShapeDtypeStruct((t, D), BF16),
                   jax.ShapeDtypeStruct((t, D), F32), jax.ShapeDtypeStruct((t, D), BF16),
                   jax.ShapeDtypeStruct((8, GDV), F32)),
        compiler_params=_cparams(1, 48),
    )(dyb, dyc, att, of, ob, p, p, gla_g)


def _merge_fwd(bra, brb, brc, p, b_gate, name):
    t = p.shape[0]
    mgb = OFF["mg"] // D

    def body(a_ref, b_ref, c_ref, ga_ref, gb_ref, gc_ref, bg_ref, m_ref):
        m_ref[...] = (_sigmoid(ga_ref[...].astype(F32) + bg_ref[:, 0:D]) * a_ref[...]
                      + _sigmoid(gb_ref[...].astype(F32) + bg_ref[:, D:2 * D]) * b_ref[...]
                      + _sigmoid(gc_ref[...].astype(F32) + bg_ref[:, 2 * D:3 * D]) * c_ref[...]).astype(BF16)

    row = pl.BlockSpec((TM, D), lambda i: (i, 0))
    gates = [pl.BlockSpec((TM, D), functools.partial(lambda i, b: (i, b), b=mgb + j)) for j in range(3)]
    return pl.pallas_call(
        body, name=name, grid=(t // TM,),
        in_specs=[row, row, row, *gates, pl.BlockSpec((1, 3 * D), lambda i: (0, 0))],
        out_specs=row, out_shape=jax.ShapeDtypeStruct((t, D), BF16), compiler_params=_cparams(1, 40),
    )(bra, brb, brc, p, p, p, b_gate)


def _merge_bwd(dm, bra, brb, brc, p, b_gate, name):
    t = p.shape[0]
    mgb = OFF["mg"] // D

    def body(dm_ref, a_ref, b_ref, c_ref, ga_ref, gb_ref, gc_ref, bg_ref, da_ref, db_ref, dc_ref, dmg_ref, dbg_ref):
        i = pl.program_id(0)

        @pl.when(i == 0)
        def _():
            dbg_ref[...] = jnp.zeros_like(dbg_ref)

        dm = dm_ref[...]
        for j, (br_ref, g_ref, d_ref) in enumerate(((a_ref, ga_ref, da_ref), (b_ref, gb_ref, db_ref), (c_ref, gc_ref, dc_ref))):
            sl = slice(j * D, (j + 1) * D)
            g = _sigmoid(g_ref[...].astype(F32) + bg_ref[:, sl])
            d_ref[...] = (dm * g).astype(BF16)
            dmg = dm * br_ref[...] * (g * (1.0 - g))
            dmg_ref[:, sl] = dmg.astype(BF16)
            dbg_ref[:, sl] += jnp.sum(dmg, axis=0, keepdims=True)

    row = pl.BlockSpec((TM, D), lambda i: (i, 0))
    gates = [pl.BlockSpec((TM, D), functools.partial(lambda i, b: (i, b), b=mgb + j)) for j in range(3)]
    return pl.pallas_call(
        body, name=name, grid=(t // TM,),
        in_specs=[row, row, row, row, *gates, pl.BlockSpec((1, 3 * D), lambda i: (0, 0))],
        out_specs=(row, row, row, pl.BlockSpec((TM, 3 * D), lambda i: (i, 0)), pl.BlockSpec((8, 3 * D), lambda i: (0, 0))),
        out_shape=(jax.ShapeDtypeStruct((t, D), BF16),) * 3 + (jax.ShapeDtypeStruct((t, 3 * D), BF16),
                                                                jax.ShapeDtypeStruct((8, 3 * D), F32)),
        compiler_params=_cparams(1, 48),
    )(dm, bra, brb, brc, p, p, p, b_gate)


def _adam_update(ns, g_ref, w_ref, m_ref, v_ref, go_ref, d_ref, mo_ref, vo_ref):
    g = g_ref[0].astype(F32)
    for s in range(1, ns):
        g = g + g_ref[s].astype(F32)
    mn = ADAM_B1 * m_ref[...] + (1.0 - ADAM_B1) * g
    vn = ADAM_B2 * v_ref[...] + (1.0 - ADAM_B2) * jnp.square(g)
    m_hat = mn / (1.0 - ADAM_B1 ** ADAM_STEP)
    v_hat = vn / (1.0 - ADAM_B2 ** ADAM_STEP)
    go_ref[...] = g
    d_ref[...] = -ADAM_LR * (m_hat / (jnp.sqrt(v_hat) + ADAM_EPS) + ADAM_WD * w_ref[...])
    mo_ref[...] = mn
    vo_ref[...] = vn


def _adamw(gsrc, w, m, v, name):
    ns, nl, r, c = gsrc.shape
    gb = gsrc.dtype.itemsize

    def fits(rows, cols):
        lanes = -(-cols // LANE) * LANE
        return ns * rows * lanes * gb <= ADAM_SRC_BYTES and rows * lanes * 4 <= ADAM_ROW_BYTES

    tr, tc = r, c
    if not fits(r, c):
        rows = [cand for cand in range(16, r, 16) if r % cand == 0 and fits(cand, c)]
        cols = [cand for cand in range(LANE, c, LANE) if c % cand == 0 and fits(r, cand)]
        if rows:
            tr = rows[-1]
        else:
            tc = cols[-1]

    def body(*refs):
        _adam_update(ns, *refs)

    row = pl.BlockSpec((None, tr, tc), lambda l, i, j: (l, i, j))
    return pl.pallas_call(
        body, name=name, grid=(nl, r // tr, c // tc),
        in_specs=[pl.BlockSpec((ns, None, tr, tc), lambda l, i, j: (0, l, i, j)), row, row, row],
        out_specs=(row,) * 4, out_shape=(jax.ShapeDtypeStruct((nl, r, c), F32),) * 4,
        compiler_params=_cparams(3, 48),
    )(gsrc, w, m, v)


def _adamw_small(items, name):
    k = len(items)

    def body(*refs):
        for j in range(k):
            _adam_update(items[j][0].shape[0], *refs[4 * j:4 * j + 4], *refs[4 * k + 4 * j:4 * k + 4 * j + 4])

    out = pl.pallas_call(
        body, name=name,
        out_shape=tuple(jax.ShapeDtypeStruct(w.shape, F32) for _, w, _, _ in items for _ in range(4)),
    )(*[a for item in items for a in item])
    return [out[4 * j:4 * j + 4] for j in range(k)]


def _rope_tables(ctx, seq):
    n_rows = seq // GRID_W
    pairs = HD // 4
    row = jnp.repeat(jnp.arange(n_rows, dtype=F32), GRID_W)
    col = jnp.tile(jnp.arange(GRID_W, dtype=F32), n_rows)
    freqs = ROPE_THETA ** (-jnp.arange(pairs, dtype=F32) * 2.0 / (HD // 2))
    ar, ac = row[:, None] * freqs, col[:, None] * freqs
    cos_l = jnp.concatenate([jnp.cos(ar), jnp.cos(ar), jnp.cos(ac), jnp.cos(ac)], axis=1)
    sin_l = jnp.concatenate([-jnp.sin(ar), jnp.sin(ar), -jnp.sin(ac), jnp.sin(ac)], axis=1)
    cos_t = jnp.concatenate([jnp.ones((ctx, HD), F32), cos_l], axis=0)
    sin_t = jnp.concatenate([jnp.zeros((ctx, HD), F32), sin_l], axis=0)
    return cos_t, sin_t


def _to_proj_layout(wt):
    parts = [wt[s:s + wd] for _, s, wd in _SEGS]
    used = sum(wd for _, _, wd in _SEGS)
    parts.append(jnp.zeros((NP - used, wt.shape[1]), wt.dtype))
    return jnp.concatenate(parts, axis=0)


def _from_proj_layout(g):
    order = sorted(_SEGS, key=lambda sg: sg[1])
    return jnp.concatenate([g[OFF[n]:OFF[n] + wd] for n, _, wd in order], axis=0)


def _row0(a):
    return a[..., 0, :]


def kernel(x, c, ctx, c_ctx, w_ada, b_ada, g_pre, g_post, w_in, conv_w, q_norm_g, k_norm_g, w_decay_fwd, b_decay_fwd, w_decay_bwd, b_decay_bwd, gla_norm_g, w_br_conv, w_br_attn, w_br_gla, b_gate, w_out, loss_target, m_c_ctx, m_w_ada, m_b_ada, m_g_pre, m_g_post, m_w_in, m_conv_w, m_q_norm_g, m_k_norm_g, m_w_decay_fwd, m_b_decay_fwd, m_w_decay_bwd, m_b_decay_bwd, m_gla_norm_g, m_w_br_conv, m_w_br_attn, m_w_br_gla, m_b_gate, m_w_out, v_c_ctx, v_w_ada, v_b_ada, v_g_pre, v_g_post, v_w_in, v_conv_w, v_q_norm_g, v_k_norm_g, v_w_decay_fwd, v_b_decay_fwd, v_w_decay_bwd, v_b_decay_bwd, v_gla_norm_g, v_w_br_conv, v_w_br_attn, v_w_br_gla, v_b_gate, v_w_out):
    seq, n_ctx = x.shape[1], ctx.shape[1]
    assert n_ctx % TM == 0 and seq % TM == 0 and seq % GRID_W == 0
    t = n_ctx + seq
    nct, ncc = n_ctx // TM, n_ctx // CH
    dev = 4 * lax.axis_index("x") + 2 * lax.axis_index("y") + lax.axis_index("c")
    ada_w = w_ada.shape[2]
    in_w = w_in.shape[2]
    br_r = w_br_conv.shape[1]

    def in_t(a, l):
        return a.transpose(2, 0, 1)[:, l, :]

    wb = [w.astype(BF16) for w in (w_ada, w_br_conv, w_br_attn, w_br_gla, w_out)]
    wall = _all_gather([wb[0][0], in_t(w_in, 0).astype(BF16), conv_w, w_decay_fwd, w_decay_bwd],
                       "gather_first")
    later = _GatherRider([in_t(w_in, 1).astype(BF16), wb[0][1], wb[1], wb[2], wb[3], wb[4]])

    def full_small(g):
        return g.transpose(1, 2, 0, 3).reshape(DEPTH, g.shape[2], NDEV * g.shape[3])

    def full_in(g):
        return _to_proj_layout(g.reshape(IN_WIDTH, D))

    def full_ada(g):
        return g.transpose(1, 0, 2).reshape(D, 3 * D)

    w_ada_f = [full_ada(wall[0]), None]
    wp = [full_in(wall[1]), None]
    conv_f, wdf_f, wdb_f = full_small(wall[2]), full_small(wall[3]), full_small(wall[4])

    cos_t, sin_t = _rope_tables(n_ctx, seq)
    cc = jnp.concatenate([c_ctx[None, :], c.reshape(1, D), jnp.zeros((6, D), F32)], axis=0)
    silu_cc, dsilu_cc = _ada_in(cc)

    conv8, wd_pad, bd = [], [], []
    for l in range(DEPTH):
        conv8.append(jnp.concatenate([conv_f[l], jnp.zeros((5, D), F32)], axis=0))
        zr = jnp.zeros((GLA_RANK, GH * GDK), F32)
        wd_pad.append(jnp.concatenate([jnp.concatenate([wdf_f[l], zr], axis=1), jnp.concatenate([zr, wdb_f[l]], axis=1),
                                       jnp.zeros((R_PAD - 2 * GLA_RANK, D), F32)], axis=0))
        bd.append(jnp.concatenate([b_decay_fwd[l], b_decay_bwd[l]])[None, :])

    xs = jnp.concatenate([ctx[0], x[0]], axis=0)
    saved = []
    for l in range(DEPTH):
        n = f"l{l}_"
        mod = _mm(silu_cc, w_ada_f[l], n + "mod", bias=b_ada[l][None, :])
        mod3 = mod[0:2].reshape(2, 3, D)
        h = _prenorm_fwd(xs, g_pre[l][None, :], mod3, nct, n + "prenorm")
        p = _mm(h, wp[l], n + "proj", tb=True, out_dtype=BF16)
        cv, ya = _conv_fwd(p, conv8[l], nct, n + "conv")
        qr, kr = _qk_prep_fwd(p, q_norm_g[l][None, :], k_norm_g[l][None, :], cos_t, sin_t, n + "qk_prep")
        att, lse, *got = _attn_fwd(qr, kr, p, nct, n + "attn", rider=later if l == 0 else None)
        if l == 0:
            wp[1], w_ada_f[1] = full_in(got[0]), full_ada(got[1])
            w_brs_f = [g.transpose(1, 0, 2, 3).reshape(DEPTH, D, D) for g in got[2:]]
        z, la = _decay_fwd(p, wd_pad[l], bd[l], n + "decay")
        of, stf, ob, stb = _gla_fwd(p, la, ncc, n + "gla")
        yb, yc = _branch_fwd(att, of, ob, p, gla_norm_g[l][None, :], n + "branch")
        bra = _mm(ya, w_brs_f[0][l], n + "br_conv")
        brb = _mm(yb, w_brs_f[1][l], n + "br_attn")
        brc = _mm(yc, w_brs_f[2][l], n + "br_gla")
        mm_ = _merge_fwd(bra, brb, brc, p, b_gate[l][None, :], n + "merge")
        out = _mm(mm_, w_brs_f[3][l], n + "out")
        x_new = _post_fwd(xs, out, g_post[l][None, :], mod3, nct, n + "post")
        saved.append(dict(x=xs, mod3=mod3, h=h, p=p, cv=cv, ya=ya, qr=qr, kr=kr, att=att, lse=lse, z=z, la=la, of=of, ob=ob,
                          stf=stf, stb=stb, yb=yb, yc=yc, bra=bra, brb=brb, brc=brc, m=mm_, out=out))
        xs = x_new

    dx, sq = _loss_grad(xs, loss_target[0], nct, "loss")
    loss = lax.psum(0.5 * sq[0, 0] / D, ("x", "y", "c"))

    gw = {k: [None] * DEPTH for k in ("w_in", "br_conv", "br_attn", "br_gla", "out", "b_gate", "g_pre", "g_post",
                                      "conv_w", "qg", "kg", "wd", "bdec", "gla_g", "dmod")}
    dctx = []

    def in_slots(l):
        return _from_proj_layout(gw["w_in"][l]).reshape(NDEV, in_w, D)

    def br_slots():
        return [jnp.stack([gw[k][l].reshape(NDEV, br_r, D) for l in range(DEPTH)], axis=1)
                for k in ("br_conv", "br_attn", "br_gla", "out")]

    for l in reversed(range(DEPTH)):
        n = f"l{l}_b_"
        s = saved[l]
        p = s["p"]
        d_out, dgt, gw["g_post"][l] = _post_bwd(dx, s["out"], g_post[l][None, :], s["mod3"], nct, n + "post")
        dm = _mm(d_out, w_brs_f[3][l], n + "dm", tb=True)
        gw["out"][l] = _mm(s["m"], d_out, n + "dw_out", ta=True, out_dtype=BF16)
        dbra, dbrb, dbrc, dmg, gw["b_gate"][l] = _merge_bwd(dm, s["bra"], s["brb"], s["brc"], p, b_gate[l][None, :], n + "merge")
        dya = _mm(dbra, w_brs_f[0][l], n + "dya", tb=True)
        dyb = _mm(dbrb, w_brs_f[1][l], n + "dyb", tb=True)
        dyc = _mm(dbrc, w_brs_f[2][l], n + "dyc", tb=True)
        gw["br_conv"][l] = _mm(s["ya"], dbra, n + "dw_conv", ta=True, out_dtype=BF16)
        gw["br_attn"][l] = _mm(s["yb"], dbrb, n + "dw_attn", ta=True, out_dtype=BF16)
        gw["br_gla"][l] = _mm(s["yc"], dbrc, n + "dw_gla", ta=True, out_dtype=BF16)
        dcv, dab, daz = _conv_bwd_a(dya, p, s["cv"], n + "conv_a")
        dac, dax, gw["conv_w"][l] = _conv_bwd_b(dcv, p, conv8[l], nct, n + "conv_b")
        datt, dza, dgo, dzg, gw["gla_g"][l] = _branch_bwd(dyb, dyc, s["att"], s["of"], s["ob"], p, gla_norm_g[l][None, :], n + "branch")
        ex1 = _ExchangeRider([in_slots(DEPTH - 1)] + br_slots()) if l == 0 else None
        dqr, dkr, dv, *got = _attn_bwd(s["qr"], s["kr"], p, s["att"], s["lse"], datt, nct, n + "attn", rider=ex1)
        if l == 0:
            recv_in1, recv_br = got[0], got[1:]
        dq, dk, gw["qg"][l], gw["kg"][l] = _qk_prep_bwd(dqr, dkr, p, q_norm_g[l][None, :], k_norm_g[l][None, :], cos_t, sin_t, n + "qk_prep")
        gf, gb = _gla_bwd(p, s["la"], dgo, s["stf"], s["stb"], ncc, n + "gla")
        dgq, dgk, dgv, dr, gw["bdec"][l], gw["wd"][l] = _gla_merge_bwd(gf, gb, s["z"], p, wd_pad[l], n + "gla_merge")
        pieces = dict(a_b=dab, a_c=dac, a_x=dax, a_z=daz, q=dq, z_attn=dza, gv=dgv, zg=dzg, mg=dmg, gq=dgq, gk=dgk,
                      k=dk, v=dv.astype(BF16), r=dr)
        used = sum(wd_ for _, _, wd_ in _SEGS) - 32 + R_PAD
        dp = jnp.concatenate([pieces[nm] for nm, _, _ in _SEGS] + [jnp.zeros((t, NP - used), BF16)], axis=1)
        gw["w_in"][l] = _mm(dp, s["h"], n + "dw_in", ta=True, out_dtype=BF16)
        if l == 0:
            dh, recv_in0 = _mm(dp, wp[l], n + "dh", rider=_ExchangeRider([in_slots(0)]))
        else:
            dh = _mm(dp, wp[l], n + "dh")
        dx, dsh, dsc, gw["g_pre"][l] = _prenorm_bwd(dh, s["x"], dx, g_pre[l][None, :], s["mod3"], nct, n + "prenorm")
        dmod = jnp.stack([_row0(dsh), _row0(dsc), _row0(dgt)], axis=1).reshape(2, 3 * D)
        gw["dmod"][l] = dmod
        dmod8 = jnp.concatenate([dmod, jnp.zeros((6, 3 * D), F32)], axis=0)
        dctx.append(_mm(dmod8, w_ada_f[l], n + "dsilu", tb=True))
    grad_x = dx[n_ctx:][None]
    g_cctx = _cctx_grad(dctx[0], dctx[1], dsilu_cc)[0]

    def st2(name):
        return jnp.stack(gw[name])

    g_b_ada = jnp.stack([gw["dmod"][l][0] + gw["dmod"][l][1] for l in range(DEPTH)])
    g_bdf = jnp.stack([gw["bdec"][l][0, :GH * GDK] for l in range(DEPTH)])
    g_bdb = jnp.stack([gw["bdec"][l][0, GH * GDK:] for l in range(DEPTH)])
    g_wdf = jnp.stack([gw["wd"][l][0:GLA_RANK, :GH * GDK] for l in range(DEPTH)])
    g_wdb = jnp.stack([gw["wd"][l][GLA_RANK:2 * GLA_RANK, GH * GDK:] for l in range(DEPTH)])
    rep_grads = [g_cctx, g_b_ada, st2("g_pre")[:, 0], st2("g_post")[:, 0], st2("qg")[:, 0], st2("kg")[:, 0], g_bdf, g_bdb,
                 st2("gla_g")[:, 0], st2("b_gate")[:, 0]]
    rep_w = [c_ctx, b_ada, g_pre, g_post, q_norm_g, k_norm_g, b_decay_fwd, b_decay_bwd, gla_norm_g, b_gate]
    rep_m = [m_c_ctx, m_b_ada, m_g_pre, m_g_post, m_q_norm_g, m_k_norm_g, m_b_decay_fwd, m_b_decay_bwd, m_gla_norm_g, m_b_gate]
    rep_v = [v_c_ctx, v_b_ada, v_g_pre, v_g_post, v_q_norm_g, v_k_norm_g, v_b_decay_fwd, v_b_decay_bwd, v_gla_norm_g, v_b_gate]
    def two_d(a):
        return a.reshape(1, -1) if a.ndim == 1 else a

    def owner_slots(g):
        return g.reshape(DEPTH, g.shape[1], NDEV, g.shape[2] // NDEV).transpose(2, 0, 1, 3)

    n_rep = len(rep_grads)
    small = _comm_alone(_Riders([
        _GatherRider([two_d(g) for g in rep_grads] + [silu_cc[0:2], jnp.stack(gw["dmod"])]),
        _ExchangeRider([owner_slots(st2("conv_w")[:, 0:3]), owner_slots(g_wdf), owner_slots(g_wdb)])]),
        "exchange_small_grads")
    rep_src, (a_all, d_all), sh_src = small[:n_rep], small[n_rep:n_rep + 2], small[n_rep + 2:]
    sh_w = [conv_w, w_decay_fwd, w_decay_bwd]
    sh_m = [m_conv_w, m_w_decay_fwd, m_w_decay_bwd]
    sh_v = [v_conv_w, v_w_decay_fwd, v_w_decay_bwd]
    small_out = _adamw_small(
        [(g, two_d(w), two_d(m), two_d(v)) for g, w, m, v in zip(rep_src, rep_w, rep_m, rep_v)]
        + list(zip(sh_src, sh_w, sh_m, sh_v)), "adam_small")
    rep_g, rep_d, rep_nm, rep_nv = [[small_out[j][k].reshape(rep_w[j].shape) for j in range(n_rep)] for k in range(4)]
    sh_gr, sh_d, sh_nm, sh_nv = [[small_out[n_rep + j][k] for j in range(len(sh_w))] for k in range(4)]

    a_all = a_all.reshape(NDEV * 2, D)
    d_all = d_all.transpose(1, 0, 2, 3).reshape(DEPTH, NDEV * 2, 3 * D)
    g_ada = jnp.stack([_mm(a_all, lax.dynamic_slice_in_dim(d_all[l], dev * ada_w, ada_w, axis=1), f"dw_ada{l}",
                           ta=True, precise=True, tk=NDEV * 2) for l in range(DEPTH)])
    ada_g, ada_d, ada_nm, ada_nv = _adamw(g_ada[None], w_ada, m_w_ada, v_w_ada, "adam_ada")

    big_w = [w_br_conv, w_br_attn, w_br_gla, w_out]
    big_m = [m_w_br_conv, m_w_br_attn, m_w_br_gla, m_w_out]
    big_v = [v_w_br_conv, v_w_br_attn, v_w_br_gla, v_w_out]
    big_out = [_adamw(recv_br[j], big_w[j], big_m[j], big_v[j], f"adam_big{j}") for j in range(len(big_w))]
    in_out = [_adamw(r_[:, None], in_t(w_in, l)[None], in_t(m_w_in, l)[None], in_t(v_w_in, l)[None], f"adam_in{l}")
              for l, r_ in enumerate((recv_in0, recv_in1))]
    in_res = [jnp.stack([in_out[l][k][0] for l in range(DEPTH)], axis=1).transpose(1, 2, 0) for k in range(4)]
    big_g, big_d, big_nm, big_nv = [[in_res[k]] + [o[k] for o in big_out] for k in range(4)]

    def ordered(rep, ada, big, sh):
        c_ctx_, b_ada_, g_pre_, g_post_, qg_, kg_, bdf_, bdb_, glag_, bgate_ = rep
        w_in_, brc_, bra_, brg_, wout_ = big
        conv_, wdf_, wdb_ = sh
        return [c_ctx_, ada, b_ada_, g_pre_, g_post_, w_in_, conv_, qg_, kg_, wdf_, bdf_, wdb_, bdb_, glag_,
                brc_, bra_, brg_, bgate_, wout_]

    return (loss, grad_x,
            *ordered(rep_g, ada_g, big_g, sh_gr), *ordered(rep_d, ada_d, big_d, sh_d),
            *ordered(rep_nm, ada_nm, big_nm, sh_nm), *ordered(rep_nv, ada_nv, big_nv, sh_nv))
```

```python
import functools

import numpy as np
import jax
import jax.numpy as jnp
from jax import lax
from jax.experimental import pallas as pl
from jax.experimental.pallas import tpu as pltpu

F32, BF16 = jnp.float32, jnp.bfloat16
HIGHEST = lax.Precision.HIGHEST

D = 1024
DEPTH = 2
GRID_W = 64
NH, NKV, HD = 8, 2, 128
GROUP = NH // NKV
ROPE_THETA = 10000.0
ATTN_SCALE = HD ** -0.5
Q_FOLD = ATTN_SCALE * 1.4426950408889634
P_HALO = 16
GH, GDK, GDV = 4, 128, 256
GLA_RANK = 16
GLA_TAU = 16.0
CH = 64
GLA_SCALE = GDK ** -0.5
EPS = 1e-6
NDEV = 8
LANE = 128
TM = 256
ATTN_HEADS_PER_STEP = 1
ATTN_KEY_CHUNK = 8192

ADAM_LR, ADAM_B1, ADAM_B2, ADAM_EPS, ADAM_WD, ADAM_STEP = 0.001, 0.9, 0.999, 1e-08, 0.01, 10

_SEGS = (("a_b", 0, 1024), ("a_z", 3072, 1024), ("a_c", 1024, 1024), ("a_x", 2048, 1024),
         ("z_attn", 5632, 1024), ("zg", 8736, 1024), ("gv", 7680, 1024), ("gq", 6656, 512), ("gk", 7168, 512),
         ("q", 4096, 1024), ("mg", 9760, 3072), ("k", 5120, 256), ("v", 5376, 256), ("r", 8704, 32))
DP_BLOCKS = {"conv_a": ("a_b", 2048), "conv_b": ("a_c", 2048), "branch": ("z_attn", 2048), "gla": ("gv", 2048),
             "q": ("q", 1024), "merge": ("mg", 3072), "tail": ("k", 1024)}
IN_WIDTH = 12832
NP = 13312
OFF = {}
_o = 0
for _n, _s, _w in _SEGS:
    OFF[_n] = _o
    _o += _w
R_PAD = 128


def _cparams(ngrid, vmem_mb):
    return pltpu.CompilerParams(dimension_semantics=("arbitrary",) * ngrid, vmem_limit_bytes=vmem_mb << 20)


def _pick(n, cands):
    for c in cands:
        if n % c == 0:
            return c
    return n


def _sigmoid(x):
    return 1.0 / (1.0 + jnp.exp(-x))


ADAM_SRC_BYTES = 8 << 20
ADAM_ROW_BYTES = 1 << 20


def _all_gather(xs, name):
    return _comm_alone(_GatherRider(xs), name)


_HBM = pl.BlockSpec(memory_space=pl.ANY)


class _Rider:
    def __init__(self, xs, out_shapes):
        self.xs, self.n = list(xs), len(xs)
        self.out_shape = [jax.ShapeDtypeStruct(s, x.dtype) for s, x in zip(out_shapes, xs)]
        self.scratch = [pltpu.SemaphoreType.DMA((7 * self.n,)), pltpu.SemaphoreType.DMA((7 * self.n,)),
                        pltpu.SemaphoreType.DMA((self.n,))]


class _GatherRider(_Rider):
    def __init__(self, xs):
        super().__init__(xs, [(NDEV,) + x.shape for x in xs])

    def _parts(self, x_refs, out_refs, sems):
        n = self.n
        send_sems, recv_sems, local_sems = sems
        mx, my, mc = lax.axis_index("x"), lax.axis_index("y"), lax.axis_index("c")
        me, sibling = (mx, my, mc), (mx, my, 1 - mc)
        chips = [(1 - mx, my), (mx, 1 - my), (1 - mx, 1 - my)]

        def slot(a, px, py, pc):
            return out_refs[a].at[4 * px + 2 * py + pc]

        def copy(k, a, block, to, own=False):
            return pltpu.make_async_remote_copy(
                src_ref=x_refs[a] if own else slot(a, *block), dst_ref=slot(a, *block),
                send_sem=send_sems.at[k * n + a], recv_sem=recv_sems.at[k * n + a],
                device_id=to, device_id_type=pl.DeviceIdType.MESH)

        mine = [pltpu.make_async_copy(x_refs[a], slot(a, *me), local_sems.at[a]) for a in range(n)]
        first = [copy(0, a, me, sibling, own=True) for a in range(n)]
        first += [copy(1 + j, a, me, (*chip, mc), own=True) for a in range(n) for j, chip in enumerate(chips)]
        landed = [copy(1 + j, a, (*chip, mc), me) for a in range(n) for j, chip in enumerate(chips)]
        passed = [copy(4 + j, a, (*chip, mc), sibling) for a in range(n) for j, chip in enumerate(chips)]
        from_sibling = [copy(0, a, sibling, me) for a in range(n)]
        from_sibling += [copy(4 + j, a, (*chip, 1 - mc), me) for a in range(n) for j, chip in enumerate(chips)]
        return mine, first, landed, passed, from_sibling

    def start(self, x_refs, out_refs, sems):
        mine, first, _, _, _ = self._parts(x_refs, out_refs, sems)
        for cp in mine + first:
            cp.start()

    def middle(self, x_refs, out_refs, sems):
        _, _, landed, passed, _ = self._parts(x_refs, out_refs, sems)
        for got, fwd in zip(landed, passed):
            got.wait_recv()
            fwd.start()

    def finish(self, x_refs, out_refs, sems):
        mine, first, _, passed, from_sibling = self._parts(x_refs, out_refs, sems)
        for cp in from_sibling:
            cp.wait_recv()
        for cp in first + passed:
            cp.wait_send()
        for cp in mine:
            cp.wait()


class _ExchangeRider(_Rider):
    def __init__(self, xs):
        super().__init__(xs, [x.shape for x in xs])

    def _parts(self, x_refs, out_refs, sems):
        n = self.n
        send_sems, recv_sems, local_sems = sems
        mx, my, mc = lax.axis_index("x"), lax.axis_index("y"), lax.axis_index("c")
        me = 4 * mx + 2 * my + mc
        mine = [pltpu.make_async_copy(x_refs[a].at[me], out_refs[a].at[me], local_sems.at[a]) for a in range(n)]
        copies = []
        for a in range(n):
            for rel in range(1, NDEV):
                px = (1 - mx) if rel & 4 else mx
                py = (1 - my) if rel & 2 else my
                pc = (1 - mc) if rel & 1 else mc
                peer = 4 * px + 2 * py + pc
                k = (rel - 1) * n + a
                copies.append(pltpu.make_async_remote_copy(
                    src_ref=x_refs[a].at[peer], dst_ref=out_refs[a].at[me],
                    send_sem=send_sems.at[k], recv_sem=recv_sems.at[k],
                    device_id=(px, py, pc), device_id_type=pl.DeviceIdType.MESH))
        return mine, copies

    def start(self, x_refs, out_refs, sems):
        mine, copies = self._parts(x_refs, out_refs, sems)
        for cp in mine + copies:
            cp.start()

    def middle(self, x_refs, out_refs, sems):
        pass

    def finish(self, x_refs, out_refs, sems):
        mine, copies = self._parts(x_refs, out_refs, sems)
        for cp in copies:
            cp.wait_recv()
        for cp in copies:
            cp.wait_send()
        for cp in mine:
            cp.wait()


class _Riders:
    def __init__(self, riders):
        self.riders = list(riders)
        self.xs = [x for r in self.riders for x in r.xs]
        self.n = len(self.xs)
        self.out_shape = [s for r in self.riders for s in r.out_shape]
        self.scratch = [s for r in self.riders for s in r.scratch]

    def _each(self, method, x_refs, out_refs, sems):
        a = b = 0
        for r in self.riders:
            getattr(r, method)(x_refs[a:a + r.n], out_refs[a:a + r.n], sems[b:b + len(r.scratch)])
            a, b = a + r.n, b + len(r.scratch)

    def start(self, *refs):
        self._each("start", *refs)

    def middle(self, *refs):
        self._each("middle", *refs)

    def finish(self, *refs):
        self._each("finish", *refs)


def _comm_alone(rider, name):
    n = rider.n

    def body(*refs):
        x_refs, out_refs, sems = refs[:n], refs[n:2 * n], refs[2 * n:]
        rider.start(x_refs, out_refs, sems)
        rider.middle(x_refs, out_refs, sems)
        rider.finish(x_refs, out_refs, sems)

    return pl.pallas_call(
        body, name=name, out_shape=tuple(rider.out_shape), in_specs=[_HBM] * n, out_specs=(_HBM,) * n,
        scratch_shapes=rider.scratch,
    )(*rider.xs)


def _with_rider(body, nin, nout, rider, first, mid, last):
    if rider is None:
        return body
    n = rider.n

    def wrapped(*refs):
        ins, x_refs = refs[:nin], refs[nin:nin + n]
        outs, out_refs = refs[nin + n:nin + n + nout], refs[nin + n + nout:nin + 2 * n + nout]
        ns = len(rider.scratch)
        scratch, sems = refs[nin + 2 * n + nout:len(refs) - ns], refs[len(refs) - ns:]

        @pl.when(first())
        def _():
            rider.start(x_refs, out_refs, sems)

        body(*ins, *outs, *scratch)

        @pl.when(mid())
        def _():
            rider.middle(x_refs, out_refs, sems)

        @pl.when(last())
        def _():
            rider.finish(x_refs, out_refs, sems)

    return wrapped


def _mm(a, b, name, ta=False, tb=False, out_dtype=F32, bias=None, precise=False, tm=None, tn=None, tk=None, rider=None):
    m, k = (a.shape[1], a.shape[0]) if ta else a.shape
    n = b.shape[0] if tb else b.shape[1]
    assert k == (b.shape[1] if tb else b.shape[0])
    tm = tm or _pick(m, (1088, 1024, 512, 256, 128))
    tn = tn or _pick(n, (1024, 512, 384, 256, 128))
    tk = tk or _pick(k, (1024, 1088, 512, 256, 128))
    nk = k // tk
    dn = (((0 if ta else 1,), (1 if tb else 0,)), ((), ()))

    def body(*refs):
        if bias is None:
            a_ref, b_ref, o_ref = refs[:3]
            bias_ref = None
        else:
            a_ref, b_ref, bias_ref, o_ref = refs[:4]
        x, y = a_ref[...], b_ref[...]
        if precise:
            p = lax.dot_general(x.astype(F32), y.astype(F32), dn, preferred_element_type=F32, precision=HIGHEST)
        else:
            p = lax.dot_general(x.astype(BF16), y.astype(BF16), dn, preferred_element_type=F32)

        def finish(acc):
            if bias_ref is not None:
                acc = acc + bias_ref[...]
            o_ref[...] = acc.astype(out_dtype)

        if nk == 1:
            finish(p)
        else:
            acc_ref = refs[-1]
            kk = pl.program_id(2)

            @pl.when(kk == 0)
            def _():
                acc_ref[...] = p

            @pl.when(kk > 0)
            def _():
                acc_ref[...] += p

            @pl.when(kk == nk - 1)
            def _():
                finish(acc_ref[...])

    a_spec = pl.BlockSpec((tk, tm), lambda i, j, kk: (kk, i)) if ta else pl.BlockSpec((tm, tk), lambda i, j, kk: (i, kk))
    b_spec = pl.BlockSpec((tn, tk), lambda i, j, kk: (j, kk)) if tb else pl.BlockSpec((tk, tn), lambda i, j, kk: (kk, j))
    in_specs = [a_spec, b_spec]
    args = [a, b]
    if bias is not None:
        in_specs.append(pl.BlockSpec((1, tn), lambda i, j, kk: (0, j)))
        args.append(bias)
    grid = (m // tm, n // tn, nk)
    out_spec = pl.BlockSpec((tm, tn), lambda i, j, kk: (i, j))
    scratch = [pltpu.VMEM((tm, tn), F32)] if nk > 1 else []
    if rider is None:
        return pl.pallas_call(
            body, name=name, grid=grid, in_specs=in_specs, out_specs=out_spec,
            out_shape=jax.ShapeDtypeStruct((m, n), out_dtype), scratch_shapes=scratch, compiler_params=_cparams(3, 56),
        )(*args)

    def at(step):
        return lambda: ((pl.program_id(0) == step[0]) & (pl.program_id(1) == step[1]) & (pl.program_id(2) == step[2]))

    end = tuple(g - 1 for g in grid)
    return pl.pallas_call(
        _with_rider(body, len(args), 1, rider, at((0, 0, 0)), at((grid[0] // 2, 0, 0)), at(end)),
        name=name, grid=grid, in_specs=in_specs + [_HBM] * rider.n, out_specs=(out_spec,) + (_HBM,) * rider.n,
        out_shape=(jax.ShapeDtypeStruct((m, n), out_dtype),) + tuple(rider.out_shape),
        scratch_shapes=scratch + rider.scratch, compiler_params=_cparams(3, 56),
    )(*args, *rider.xs)


def _ada_in(cc):
    def body(c_ref, s_ref, d_ref):
        x = c_ref[...]
        sg = _sigmoid(x)
        s_ref[...] = x * sg
        d_ref[...] = sg * (1.0 + x * (1.0 - sg))

    return pl.pallas_call(body, name="ada_in", out_shape=(jax.ShapeDtypeStruct(cc.shape, F32),) * 2)(cc)


def _cctx_grad(t0, t1, dsilu):
    def body(a_ref, b_ref, d_ref, o_ref):
        o_ref[...] = (a_ref[...] + b_ref[...]) * d_ref[...]

    return pl.pallas_call(body, name="cctx_grad", out_shape=jax.ShapeDtypeStruct(t0.shape, F32))(t0, t1, dsilu)


def _seg_spec(nct, rows=3):
    return pl.BlockSpec((None, rows, D), lambda i: (jnp.where(i >= nct, 1, 0), 0, 0))


def _prenorm_fwd(x, g_pre, mod3, nct, name):
    t = x.shape[0]

    def body(x_ref, g_ref, mod_ref, h_ref):
        xv = x_ref[...]
        r = lax.rsqrt(jnp.mean(xv * xv, axis=-1, keepdims=True) + EPS)
        y = xv * r * g_ref[...]
        h_ref[...] = (y * (1.0 + mod_ref[1:2, :]) + mod_ref[0:1, :]).astype(BF16)

    return pl.pallas_call(
        body, name=name, grid=(t // TM,),
        in_specs=[pl.BlockSpec((TM, D), lambda i: (i, 0)), pl.BlockSpec((1, D), lambda i: (0, 0)), _seg_spec(nct)],
        out_specs=pl.BlockSpec((TM, D), lambda i: (i, 0)),
        out_shape=jax.ShapeDtypeStruct((t, D), BF16), compiler_params=_cparams(1, 32),
    )(x, g_pre, mod3)


def _prenorm_bwd(dh, x, dxo, g_pre, mod3, nct, name):
    t = x.shape[0]

    def body(dh_ref, x_ref, dxo_ref, g_ref, mod_ref, dx_ref, dsh_ref, dsc_ref, dg_ref):
        i = pl.program_id(0)
        xv, dhv, g = x_ref[...], dh_ref[...], g_ref[...]
        r = lax.rsqrt(jnp.mean(xv * xv, axis=-1, keepdims=True) + EPS)
        xh = xv * r
        dy = dhv * (1.0 + mod_ref[1:2, :])
        dxh = dy * g
        dx_ref[...] = dxo_ref[...] + r * (dxh - xh * jnp.mean(dxh * xh, axis=-1, keepdims=True))

        @pl.when((i == 0) | (i == nct))
        def _():
            dsh_ref[...] = jnp.zeros_like(dsh_ref)
            dsc_ref[...] = jnp.zeros_like(dsc_ref)

        @pl.when(i == 0)
        def _():
            dg_ref[...] = jnp.zeros_like(dg_ref)

        dsh_ref[...] += jnp.sum(dhv, axis=0, keepdims=True)
        dsc_ref[...] += jnp.sum(dhv * (xh * g), axis=0, keepdims=True)
        dg_ref[...] += jnp.sum(dy * xh, axis=0, keepdims=True)

    row = pl.BlockSpec((TM, D), lambda i: (i, 0))
    seg8 = pl.BlockSpec((None, 8, D), lambda i: (jnp.where(i >= nct, 1, 0), 0, 0))
    return pl.pallas_call(
        body, name=name, grid=(t // TM,),
        in_specs=[row, row, row, pl.BlockSpec((1, D), lambda i: (0, 0)), _seg_spec(nct)],
        out_specs=(row, seg8, seg8, pl.BlockSpec((8, D), lambda i: (0, 0))),
        out_shape=(jax.ShapeDtypeStruct((t, D), F32), jax.ShapeDtypeStruct((2, 8, D), F32),
                   jax.ShapeDtypeStruct((2, 8, D), F32), jax.ShapeDtypeStruct((8, D), F32)),
        compiler_params=_cparams(1, 32),
    )(dh, x, dxo, g_pre, mod3)


def _post_fwd(x, out, g_post, mod3, nct, name):
    t = x.shape[0]

    def body(x_ref, o_ref, g_ref, mod_ref, y_ref):
        ov = o_ref[...]
        r = lax.rsqrt(jnp.mean(ov * ov, axis=-1, keepdims=True) + EPS)
        y_ref[...] = x_ref[...] + mod_ref[2:3, :] * (ov * r * g_ref[...])

    row = pl.BlockSpec((TM, D), lambda i: (i, 0))
    return pl.pallas_call(
        body, name=name, grid=(t // TM,),
        in_specs=[row, row, pl.BlockSpec((1, D), lambda i: (0, 0)), _seg_spec(nct)],
        out_specs=row, out_shape=jax.ShapeDtypeStruct((t, D), F32), compiler_params=_cparams(1, 32),
    )(x, out, g_post, mod3)


def _post_bwd(dxo, out, g_post, mod3, nct, name):
    t = out.shape[0]

    def body(dx_ref, o_ref, g_ref, mod_ref, do_ref, dgt_ref, dg_ref):
        i = pl.program_id(0)
        ov, dxv, g = o_ref[...], dx_ref[...], g_ref[...]
        r = lax.rsqrt(jnp.mean(ov * ov, axis=-1, keepdims=True) + EPS)
        nh = ov * r
        dn = dxv * mod_ref[2:3, :]
        dnh = dn * g
        do_ref[...] = (r * (dnh - nh * jnp.mean(dnh * nh, axis=-1, keepdims=True))).astype(BF16)

        @pl.when((i == 0) | (i == nct))
        def _():
            dgt_ref[...] = jnp.zeros_like(dgt_ref)

        @pl.when(i == 0)
        def _():
            dg_ref[...] = jnp.zeros_like(dg_ref)

        dgt_ref[...] += jnp.sum(dxv * (nh * g), axis=0, keepdims=True)
        dg_ref[...] += jnp.sum(dn * nh, axis=0, keepdims=True)

    row = pl.BlockSpec((TM, D), lambda i: (i, 0))
    seg8 = pl.BlockSpec((None, 8, D), lambda i: (jnp.where(i >= nct, 1, 0), 0, 0))
    return pl.pallas_call(
        body, name=name, grid=(t // TM,),
        in_specs=[row, row, pl.BlockSpec((1, D), lambda i: (0, 0)), _seg_spec(nct)],
        out_specs=(row, seg8, pl.BlockSpec((8, D), lambda i: (0, 0))),
        out_shape=(jax.ShapeDtypeStruct((t, D), BF16), jax.ShapeDtypeStruct((2, 8, D), F32),
                   jax.ShapeDtypeStruct((8, D), F32)),
        compiler_params=_cparams(1, 32),
    )(dxo, out, g_post, mod3)


def _loss_grad(y, target, nct, name):
    t = y.shape[0]

    def body(y_ref, t_ref, dy_ref, l_ref):
        i = pl.program_id(0)

        @pl.when(i == 0)
        def _():
            l_ref[...] = jnp.zeros_like(l_ref)

        @pl.when(i < nct)
        def _():
            dy_ref[...] = jnp.zeros_like(dy_ref)

        @pl.when(i >= nct)
        def _():
            err = y_ref[...] - t_ref[...]
            dy_ref[...] = err / D
            l_ref[...] += jnp.sum(jnp.sum(err * err, axis=1, keepdims=True), axis=0, keepdims=True)

    row = pl.BlockSpec((TM, D), lambda i: (i, 0))
    return pl.pallas_call(
        body, name=name, grid=(t // TM,),
        in_specs=[row, pl.BlockSpec((TM, D), lambda i: (jnp.maximum(i - nct, 0), 0))],
        out_specs=(row, pl.BlockSpec((8, LANE), lambda i: (0, 0))),
        out_shape=(jax.ShapeDtypeStruct((t, D), F32), jax.ShapeDtypeStruct((8, LANE), F32)),
        compiler_params=_cparams(1, 32),
    )(y, target)


def _pcol(name, width):
    assert OFF[name] % width == 0
    blk = OFF[name] // width
    return pl.BlockSpec((TM, width), lambda i: (i, blk))


def _shift_rows(u, prev_row, next_row):
    n = u.shape[0]
    row = lax.broadcasted_iota(jnp.int32, u.shape, 0)
    prev = jnp.where(row == 0, prev_row, pltpu.roll(u, 1, 0))
    nxt = jnp.where(row == n - 1, next_row, pltpu.roll(u, n - 1, 0))
    return prev, nxt


def _halo_specs(width, nt, blk=0, rows=8):
    per = TM // rows
    prev = pl.BlockSpec((rows, width), lambda i: (jnp.maximum(i * per - 1, 0), blk))
    nxt = pl.BlockSpec((rows, width), lambda i: (jnp.minimum((i + 1) * per, nt * per - 1), blk))
    return prev, nxt


def _conv_fwd(p, conv_w8, nct, name):
    t = p.shape[0]
    nt = t // TM

    def body(ab_ref, ac_ref, ax_ref, az_ref, acp_ref, axp_ref, acn_ref, axn_ref, w_ref, cv_ref, ya_ref):
        i = pl.program_id(0)
        def f(ref, rows=slice(None)):
            return ref[rows, :].astype(F32)

        u = f(ac_ref) * f(ax_ref)
        mp = jnp.where((i == 0) | (i == nct), 0.0, 1.0)
        mn = jnp.where((i == nct - 1) | (i == nt - 1), 0.0, 1.0)
        last, first = slice(P_HALO - 1, P_HALO), slice(0, 1)
        prev, nxt = _shift_rows(u, f(acp_ref, last) * f(axp_ref, last) * mp, f(acn_ref, first) * f(axn_ref, first) * mn)
        cv = w_ref[0:1, :] * prev + w_ref[1:2, :] * u + w_ref[2:3, :] * nxt
        az = f(az_ref)
        cv_ref[...] = cv
        ya_ref[...] = (f(ab_ref) * cv * (az * _sigmoid(az))).astype(BF16)

    acp, acn = _halo_specs(D, nt, OFF["a_c"] // D, P_HALO)
    axp, axn = _halo_specs(D, nt, OFF["a_x"] // D, P_HALO)
    row = pl.BlockSpec((TM, D), lambda i: (i, 0))
    return pl.pallas_call(
        body, name=name, grid=(nt,),
        in_specs=[_pcol("a_b", D), _pcol("a_c", D), _pcol("a_x", D), _pcol("a_z", D), acp, axp, acn, axn,
                  pl.BlockSpec((8, D), lambda i: (0, 0))],
        out_specs=(row, row),
        out_shape=(jax.ShapeDtypeStruct((t, D), F32), jax.ShapeDtypeStruct((t, D), BF16)),
        compiler_params=_cparams(1, 40),
    )(p, p, p, p, p, p, p, p, conv_w8)


def _dp_spec(key):
    seg, width = DP_BLOCKS[key]
    assert OFF[seg] % width == 0
    blk = OFF[seg] // width
    return pl.BlockSpec((TM, width), lambda i: (i, blk))


def _conv_bwd_a(dya, p, cv, dp, name):
    t = p.shape[0]

    def body(dy_ref, ab_ref, az_ref, cv_ref, _, dcv_ref, dp_ref):
        dy, ab, az, c = dy_ref[...], ab_ref[...].astype(F32), az_ref[...].astype(F32), cv_ref[...]
        sg = _sigmoid(az)
        sz = az * sg
        dcv_ref[...] = dy * ab * sz
        dp_ref[:, 0:D] = (dy * c * sz).astype(BF16)
        dp_ref[:, D:2 * D] = (dy * ab * c * (sg * (1.0 + az * (1.0 - sg)))).astype(BF16)

    row = pl.BlockSpec((TM, D), lambda i: (i, 0))
    return pl.pallas_call(
        body, name=name, grid=(t // TM,),
        in_specs=[row, _pcol("a_b", D), _pcol("a_z", D), row, _HBM], out_specs=(row, _dp_spec("conv_a")),
        out_shape=(jax.ShapeDtypeStruct((t, D), F32), jax.ShapeDtypeStruct(dp.shape, dp.dtype)),
        input_output_aliases={4: 1}, compiler_params=_cparams(1, 40),
    )(dya, p, p, cv, dp)


def _conv_bwd_b(dcv, p, conv_w8, nct, dp, name):
    t = p.shape[0]
    nt = t // TM

    def body(dcv_ref, hp_ref, hn_ref, ac_ref, ax_ref, w_ref, _, dp_ref, dw_ref):
        i = pl.program_id(0)
        d, ac, ax = dcv_ref[...], ac_ref[...].astype(F32), ax_ref[...].astype(F32)
        u = ac * ax
        mp = jnp.where((i == 0) | (i == nct), 0.0, 1.0)
        mn = jnp.where((i == nct - 1) | (i == nt - 1), 0.0, 1.0)
        dprev, dnxt = _shift_rows(d, hp_ref[7:8, :] * mp, hn_ref[0:1, :] * mn)
        du = w_ref[0:1, :] * dnxt + w_ref[1:2, :] * d + w_ref[2:3, :] * dprev
        dp_ref[:, 0:D] = (du * ax).astype(BF16)
        dp_ref[:, D:2 * D] = (du * ac).astype(BF16)

        @pl.when(i == 0)
        def _():
            dw_ref[...] = jnp.zeros_like(dw_ref)

        dw0 = jnp.sum(u * dnxt, axis=0, keepdims=True)
        dw1 = jnp.sum(u * d, axis=0, keepdims=True)
        dw2 = jnp.sum(u * dprev, axis=0, keepdims=True)
        r8 = lax.broadcasted_iota(jnp.int32, (8, D), 0)
        dw_ref[...] += jnp.where(r8 == 0, dw0, jnp.where(r8 == 1, dw1, jnp.where(r8 == 2, dw2, 0.0)))

    hp, hn = _halo_specs(D, nt)
    row = pl.BlockSpec((TM, D), lambda i: (i, 0))
    return pl.pallas_call(
        body, name=name, grid=(nt,),
        in_specs=[row, hp, hn, _pcol("a_c", D), _pcol("a_x", D), pl.BlockSpec((8, D), lambda i: (0, 0)), _HBM],
        out_specs=(_dp_spec("conv_b"), pl.BlockSpec((8, D), lambda i: (0, 0))),
        out_shape=(jax.ShapeDtypeStruct(dp.shape, dp.dtype), jax.ShapeDtypeStruct((8, D), F32)),
        input_output_aliases={6: 0}, compiler_params=_cparams(1, 40),
    )(dcv, dcv, dcv, p, p, conv_w8, dp)


def _rot_half(x):
    lane = lax.broadcasted_iota(jnp.int32, x.shape, 1)
    return jnp.where((lane % 64) < 32, pltpu.roll(x, 96, 1), pltpu.roll(x, 32, 1))


def _qk_prep_fwd(p, qg, kg, cos_t, sin_t, name):
    t = p.shape[0]

    def body(q_ref, k_ref, qg_ref, kg_ref, c_ref, s_ref, qo_ref, ko_ref):
        c, s = c_ref[...], s_ref[...]

        def one(xv, g, scale):
            y = xv * lax.rsqrt(jnp.mean(xv * xv, axis=-1, keepdims=True) + EPS) * g
            return ((y * c + _rot_half(y) * s) * scale).astype(BF16)

        for h in range(NH):
            qo_ref[:, h * HD:(h + 1) * HD] = one(q_ref[:, h * HD:(h + 1) * HD].astype(F32), qg_ref[...], Q_FOLD)
        for h in range(NKV):
            ko_ref[:, h * HD:(h + 1) * HD] = one(k_ref[:, h * HD:(h + 1) * HD].astype(F32), kg_ref[...], 1.0)

    vec = pl.BlockSpec((1, HD), lambda i: (0, 0))
    tab = pl.BlockSpec((TM, HD), lambda i: (i, 0))
    return pl.pallas_call(
        body, name=name, grid=(t // TM,),
        in_specs=[_pcol("q", NH * HD), _pcol("k", NKV * HD), vec, vec, tab, tab],
        out_specs=(pl.BlockSpec((TM, NH * HD), lambda i: (i, 0)), pl.BlockSpec((TM, NKV * HD), lambda i: (i, 0))),
        out_shape=(jax.ShapeDtypeStruct((t, NH * HD), BF16), jax.ShapeDtypeStruct((t, NKV * HD), BF16)),
        compiler_params=_cparams(1, 32),
    )(p, p, qg, kg, cos_t, sin_t)


def _qk_prep_bwd(dqr, dkr, p, qg, kg, cos_t, sin_t, dp, name):
    t = p.shape[0]

    def body(dq_ref, dk_ref, q_ref, k_ref, qg_ref, kg_ref, c_ref, s_ref, _, dqo_ref, dko_ref, dqg_ref, dkg_ref):
        i = pl.program_id(0)
        c, s = c_ref[...], s_ref[...]

        @pl.when(i == 0)
        def _():
            dqg_ref[...] = jnp.zeros_like(dqg_ref)
            dkg_ref[...] = jnp.zeros_like(dkg_ref)

        def one(dyr, xv, g):
            dy = dyr * c + _rot_half(dyr * s)
            r = lax.rsqrt(jnp.mean(xv * xv, axis=-1, keepdims=True) + EPS)
            xh = xv * r
            dxh = dy * g
            dx = r * (dxh - xh * jnp.mean(dxh * xh, axis=-1, keepdims=True))
            return dx.astype(BF16), jnp.sum(dy * xh, axis=0, keepdims=True)

        for h in range(NH):
            sl = slice(h * HD, (h + 1) * HD)
            dx, dg = one(dq_ref[:, sl] * ATTN_SCALE, q_ref[:, sl].astype(F32), qg_ref[...])
            dqo_ref[:, sl] = dx
            dqg_ref[...] += dg
        for h in range(NKV):
            sl = slice(h * HD, (h + 1) * HD)
            dx, dg = one(dk_ref[:, sl] * (ATTN_SCALE / Q_FOLD), k_ref[:, sl].astype(F32), kg_ref[...])
            dko_ref[:, sl] = dx
            dkg_ref[...] += dg

    vec = pl.BlockSpec((1, HD), lambda i: (0, 0))
    tab = pl.BlockSpec((TM, HD), lambda i: (i, 0))
    acc = pl.BlockSpec((8, HD), lambda i: (0, 0))
    qrow = pl.BlockSpec((TM, NH * HD), lambda i: (i, 0))
    krow = pl.BlockSpec((TM, NKV * HD), lambda i: (i, 0))
    return pl.pallas_call(
        body, name=name, grid=(t // TM,),
        in_specs=[qrow, krow, _pcol("q", NH * HD), _pcol("k", NKV * HD), vec, vec, tab, tab, _HBM],
        out_specs=(_dp_spec("q"), krow, acc, acc),
        out_shape=(jax.ShapeDtypeStruct(dp.shape, dp.dtype), jax.ShapeDtypeStruct((t, NKV * HD), BF16),
                   jax.ShapeDtypeStruct((8, HD), F32), jax.ShapeDtypeStruct((8, HD), F32)),
        input_output_aliases={8: 0}, compiler_params=_cparams(1, 32),
    )(dqr, dkr, p, p, qg, kg, cos_t, sin_t, dp)


def _key_chunks(n):
    c = max(c for c in range(LANE, min(n, ATTN_KEY_CHUNK) + 1, LANE) if n % c == 0)
    return [(lo, lo + c) for lo in range(0, n, c)]


def _attn_fwd(qr, kr, p, nct, name, rider=None):
    t = qr.shape[0]
    nt = t // TM
    ctx = nct * TM
    vblk = OFF["v"] // HD
    hps = ATTN_HEADS_PER_STEP
    nhp, per_kv = NH // hps, GROUP // hps

    def body(q_ref, k_ref, v_ref, o_ref, lse_ref):
        def tile(nkeys):
            for j in range(hps):
                sl = slice(j * HD, (j + 1) * HD)
                q = q_ref[:, sl]
                m = l = acc = None
                for lo, hi in _key_chunks(nkeys):
                    s = lax.dot_general(q, k_ref[lo:hi, :], _NT, preferred_element_type=F32)
                    mc = jnp.max(s, axis=-1, keepdims=True)
                    m_new = mc if m is None else jnp.maximum(m, mc)
                    e = jnp.exp2(s - m_new)
                    lc = jnp.sum(e, axis=-1, keepdims=True)
                    pv = jnp.dot(e.astype(BF16), v_ref[lo:hi, :].astype(BF16), preferred_element_type=F32)
                    if m is None:
                        l, acc = lc, pv
                    else:
                        alpha = jnp.exp2(m - m_new)
                        l, acc = l * alpha + lc, acc * alpha + pv
                    m = m_new
                o_ref[:, sl] = acc / l
                lse_ref[:, j:j + 1] = m + jnp.log2(l)

        pl.when(pl.program_id(1) < nct)(lambda: tile(ctx))
        pl.when(pl.program_id(1) >= nct)(lambda: tile(t))

    def at(h, i):
        return lambda: (pl.program_id(0) == h) & (pl.program_id(1) == i)

    rn = 0 if rider is None else rider.n
    qspec = pl.BlockSpec((TM, hps * HD), lambda h, i: (i, h))
    return pl.pallas_call(
        _with_rider(body, 3, 2, rider, at(0, 0), at(nhp * 7 // 8, 0), at(nhp - 1, nt - 1)),
        name=name, grid=(nhp, nt),
        in_specs=[qspec, pl.BlockSpec((t, HD), lambda h, i: (0, h // per_kv)),
                  pl.BlockSpec((t, HD), lambda h, i: (0, vblk + h // per_kv))] + [_HBM] * rn,
        out_specs=(qspec, pl.BlockSpec((None, TM, hps), lambda h, i: (h, i, 0))) + (_HBM,) * rn,
        out_shape=(jax.ShapeDtypeStruct((t, NH * HD), F32), jax.ShapeDtypeStruct((nhp, t, hps), F32))
        + (() if rider is None else tuple(rider.out_shape)),
        scratch_shapes=[] if rider is None else rider.scratch,
        compiler_params=_cparams(2, 48),
    )(qr, kr, p, *(() if rider is None else rider.xs))


def _attn_bwd(qr, kr, p, o, lse, do, nct, name, rider=None):
    t = qr.shape[0]
    nt = t // TM
    ctx = nct * TM
    vblk = OFF["v"] // HD
    hps = ATTN_HEADS_PER_STEP

    def body(q_ref, k_ref, v_ref, o_ref, lse_ref, do_ref, dq_ref, dk_ref, dv_ref):
        g, i = pl.program_id(1), pl.program_id(2)

        @pl.when((g == 0) & (i == 0))
        def _():
            dk_ref[...] = jnp.zeros_like(dk_ref)
            dv_ref[...] = jnp.zeros_like(dv_ref)

        def tile(nkeys):
            heads = []
            for j in range(hps):
                sl = slice(j * HD, (j + 1) * HD)
                dov = do_ref[:, sl]
                drow = jnp.sum(dov * o_ref[:, sl], axis=-1, keepdims=True)
                heads.append((sl, q_ref[:, sl], dov.astype(BF16), drow, lse_ref[:, j:j + 1]))
            dq = [None] * hps
            for lo, hi in _key_chunks(nkeys):
                k = k_ref[lo:hi, :]
                vb = v_ref[lo:hi, :].astype(BF16)
                dk_c = dv_c = None
                for j, (sl, q, dob, drow, lse_j) in enumerate(heads):
                    s = lax.dot_general(q, k, _NT, preferred_element_type=F32)
                    pr = jnp.exp2(s - lse_j)
                    dp = lax.dot_general(dob, vb, _NT, preferred_element_type=F32)
                    ds = (pr * (dp - drow)).astype(BF16)
                    dq_c = jnp.dot(ds, k, preferred_element_type=F32)
                    dq[j] = dq_c if dq[j] is None else dq[j] + dq_c
                    dk_j = lax.dot_general(ds, q, _TN, preferred_element_type=F32)
                    dv_j = lax.dot_general(pr.astype(BF16), dob, _TN, preferred_element_type=F32)
                    dk_c = dk_j if dk_c is None else dk_c + dk_j
                    dv_c = dv_j if dv_c is None else dv_c + dv_j
                dk_ref[lo:hi, :] += dk_c
                dv_ref[lo:hi, :] += dv_c
            for j, (sl, *_) in enumerate(heads):
                dq_ref[:, sl] = dq[j]

        pl.when(i < nct)(lambda: tile(ctx))
        pl.when(i >= nct)(lambda: tile(t))

    def at(kv, g, i):
        return lambda: (pl.program_id(0) == kv) & (pl.program_id(1) == g) & (pl.program_id(2) == i)

    rn = 0 if rider is None else rider.n
    per_kv = GROUP // hps
    qspec = pl.BlockSpec((TM, hps * HD), lambda kv, g, i: (i, kv * per_kv + g))
    kvspec = pl.BlockSpec((t, HD), lambda kv, g, i: (0, kv))
    lspec = pl.BlockSpec((None, TM, hps), lambda kv, g, i: (kv * per_kv + g, i, 0))
    return pl.pallas_call(
        _with_rider(body, 6, 3, rider, at(0, 0, 0), at(NKV - 1, 0, 0), at(NKV - 1, per_kv - 1, nt - 1)),
        name=name, grid=(NKV, per_kv, nt),
        in_specs=[qspec, kvspec, pl.BlockSpec((t, HD), lambda kv, g, i: (0, vblk + kv)), qspec, lspec, qspec]
        + [_HBM] * rn,
        out_specs=(qspec, kvspec, kvspec) + (_HBM,) * rn,
        out_shape=(jax.ShapeDtypeStruct((t, NH * HD), F32), jax.ShapeDtypeStruct((t, NKV * HD), F32),
                   jax.ShapeDtypeStruct((t, NKV * HD), F32)) + (() if rider is None else tuple(rider.out_shape)),
        scratch_shapes=[] if rider is None else rider.scratch,
        compiler_params=_cparams(3, 48),
    )(qr, kr, p, o, lse, do, *(() if rider is None else rider.xs))


def _decay_fwd(p, wd, bd, name):
    t = p.shape[0]

    def body(r_ref, w_ref, b_ref, z_ref, la_ref):
        z = jnp.dot(r_ref[...].astype(BF16), w_ref[...].astype(BF16), preferred_element_type=F32) + b_ref[...]
        z_ref[...] = z
        la_ref[...] = (jnp.minimum(z, 0.0) - jnp.log(1.0 + jnp.exp(-jnp.abs(z)))) / GLA_TAU

    row = pl.BlockSpec((TM, D), lambda i: (i, 0))
    return pl.pallas_call(
        body, name=name, grid=(t // TM,),
        in_specs=[_pcol("r", R_PAD), pl.BlockSpec((R_PAD, D), lambda i: (0, 0)), pl.BlockSpec((1, D), lambda i: (0, 0))],
        out_specs=(row, row),
        out_shape=(jax.ShapeDtypeStruct((t, D), F32), jax.ShapeDtypeStruct((t, D), F32)),
        compiler_params=_cparams(1, 32),
    )(p, wd, bd)


def _chunk_order(s, ncc, nc, rev):
    if not rev:
        return s
    return jnp.where(s < ncc, ncc - 1 - s, nc - 1 - (s - ncc))


def _gla_chains(dirs):
    return [(rev, d, h) + tuple(refs) for d, (rev, *refs) in enumerate(dirs) for h in range(GH)]


def _hk(h):
    return slice(h * GDK, (h + 1) * GDK)


def _hv(h):
    return slice(h * GDV, (h + 1) * GDV)


def _gla_factors(qs, ks, las, revs):
    r = lax.broadcasted_iota(jnp.int32, (CH, CH), 0)
    c = lax.broadcasted_iota(jnp.int32, (CH, CH), 1)
    keeps = [(c >= r) if rev else (c <= r) for rev in revs]
    bcs = [jnp.dot(keep.astype(F32), la, preferred_element_type=F32, precision=HIGHEST) for keep, la in zip(keeps, las)]
    bls = [jnp.sum(la, axis=0, keepdims=True) for la in las]
    qs, ks = [q.astype(F32) for q in qs], [k.astype(F32) for k in ks]
    qts = [q * GLA_SCALE * jnp.exp(bc) for q, bc in zip(qs, bcs)]
    kts = [k * jnp.exp(-bc) for k, bc in zip(ks, bcs)]
    khs = [k * jnp.exp(bl - bc) for k, bl, bc in zip(ks, bls, bcs)]
    gls = [jnp.exp(bl) for bl in bls]
    return qts, kts, gls, khs, keeps, bcs


_NT = (((1,), (1,)), ((), ()))
_TN = (((0,), (0,)), ((), ()))


def _gla_specs(ncc, nc, rev, backward):
    def idx(s):
        return _chunk_order((nc - 1 - s) if backward else s, ncc, nc, rev)

    wk, wv = GH * GDK, GH * GDV
    qb, kb, vb = OFF["gq"] // wk, OFF["gk"] // wk, OFF["gv"] // wv
    lab = 1 if rev else 0
    q = pl.BlockSpec((CH, wk), lambda s: (idx(s), qb))
    k = pl.BlockSpec((CH, wk), lambda s: (idx(s), kb))
    v = pl.BlockSpec((CH, wv), lambda s: (idx(s), vb))
    la = pl.BlockSpec((CH, wk), lambda s: (idx(s), lab))
    o = pl.BlockSpec((CH, wv), lambda s: (idx(s), 0))
    dk = pl.BlockSpec((CH, wk), lambda s: (idx(s), 0))
    st = pl.BlockSpec((None, GH, GDV, GDK), lambda s: (idx(s), 0, 0, 0))
    return q, k, v, la, o, dk, st


def _gla_fwd(p, la, ncc, name):
    t = p.shape[0]
    nc = t // CH
    specs = [_gla_specs(ncc, nc, rev, False) for rev in (False, True)]

    def body(qf, kf, vf, laf, qb_, kb_, vb_, lab, of, stf, ob, stb, s_scr):
        @pl.when(pl.program_id(0) == 0)
        def _():
            s_scr[...] = jnp.zeros_like(s_scr)

        ch = _gla_chains(((False, qf, kf, vf, laf, of, stf), (True, qb_, kb_, vb_, lab, ob, stb)))
        qts, kts, gls, khs, keeps, _ = _gla_factors([c[3][:, _hk(c[2])] for c in ch], [c[4][:, _hk(c[2])] for c in ch],
                                                    [c[6][:, _hk(c[2])] for c in ch], [c[0] for c in ch])
        sts = [s_scr[c[1], c[2]] for c in ch]
        for c, st in zip(ch, sts):
            c[8][c[2]] = st
        vbs = [c[5][:, _hv(c[2])].astype(BF16) for c in ch]
        qbs = [qt.astype(BF16) for qt in qts]
        a_s = [jnp.where(keep, lax.dot_general(qb, kt.astype(BF16), _NT, preferred_element_type=F32), 0.0)
               for keep, qb, kt in zip(keeps, qbs, kts)]
        inter = [lax.dot_general(qb, st.astype(BF16), _NT, preferred_element_type=F32) for qb, st in zip(qbs, sts)]
        intra = [jnp.dot(a.astype(BF16), vb, preferred_element_type=F32) for a, vb in zip(a_s, vbs)]
        for c, x, y in zip(ch, inter, intra):
            c[7][:, _hv(c[2])] = x + y
        upd = [lax.dot_general(vb, kh.astype(BF16), _TN, preferred_element_type=F32) for vb, kh in zip(vbs, khs)]
        for c, st, gl, u in zip(ch, sts, gls, upd):
            s_scr[c[1], c[2]] = st * gl + u

    o_shape = jax.ShapeDtypeStruct((t, GH * GDV), F32)
    st_shape = jax.ShapeDtypeStruct((nc, GH, GDV, GDK), F32)
    return pl.pallas_call(
        body, name=name, grid=(nc,),
        in_specs=[sp for s_ in specs for sp in s_[:4]],
        out_specs=tuple(sp for s_ in specs for sp in (s_[4], s_[6])),
        out_shape=(o_shape, st_shape, o_shape, st_shape),
        scratch_shapes=[pltpu.VMEM((2, GH, GDV, GDK), F32)], compiler_params=_cparams(1, 32),
    )(p, p, p, la, p, p, p, la)


def _gla_bwd(p, la, do, stf, stb, ncc, name):
    t = p.shape[0]
    nc = t // CH
    specs = [_gla_specs(ncc, nc, rev, True) for rev in (False, True)]

    def mm(xs, ys, dims=None):
        if dims is None:
            return [jnp.dot(x, y, preferred_element_type=F32) for x, y in zip(xs, ys)]
        return [lax.dot_general(x, y, dims, preferred_element_type=F32) for x, y in zip(xs, ys)]

    def body(*refs):
        ins_f, ins_b, outs_f, outs_b, ds_scr = refs[0:6], refs[6:12], refs[12:16], refs[16:20], refs[20]

        @pl.when(pl.program_id(0) == 0)
        def _():
            ds_scr[...] = jnp.zeros_like(ds_scr)

        ch = _gla_chains(((False, *ins_f, *outs_f), (True, *ins_b, *outs_b)))
        revs = [c[0] for c in ch]
        qts, kts, gls, khs, keeps, bcs = _gla_factors([c[3][:, _hk(c[2])] for c in ch], [c[4][:, _hk(c[2])] for c in ch],
                                                      [c[6][:, _hk(c[2])] for c in ch], revs)
        stvs = [c[8][c[2]].astype(BF16) for c in ch]
        dsns = [ds_scr[c[1], c[2]] for c in ch]
        dsbs = [x.astype(BF16) for x in dsns]
        vbs = [c[5][:, _hv(c[2])].astype(BF16) for c in ch]
        dobs = [c[7][:, _hv(c[2])].astype(BF16) for c in ch]
        qbs, kbs = [x.astype(BF16) for x in qts], [x.astype(BF16) for x in kts]
        a_s = [jnp.where(keep, x, 0.0).astype(BF16) for keep, x in zip(keeps, mm(qbs, kbs, _NT))]
        das = [jnp.where(keep, x, 0.0).astype(BF16) for keep, x in zip(keeps, mm(dobs, vbs, _NT))]
        dqts = [x + y for x, y in zip(mm(dobs, stvs), mm(das, kbs))]
        dkhs = mm(vbs, dsbs)
        dkts = [x + dkh * gl for x, dkh, gl in zip(mm(das, qbs, _TN), dkhs, gls)]
        for c, x, y in zip(ch, mm(a_s, dobs, _TN), mm([kh.astype(BF16) for kh in khs], dsbs, _NT)):
            c[11][:, _hv(c[2])] = x + y
        for c, x, dsn, gl in zip(ch, mm(dobs, qbs, _TN), dsns, gls):
            ds_scr[c[1], c[2]] = x + dsn * gl
        dgls = [jnp.sum(c[8][c[2]] * dsn, axis=0, keepdims=True) + jnp.sum(dkh * kt, axis=0, keepdims=True)
                for c, dsn, dkh, kt in zip(ch, dsns, dkhs, kts)]
        row = lax.broadcasted_iota(jnp.int32, (CH, GDK), 0)
        dbcs = [dqt * qt - dkt * kt + jnp.where(row == (0 if rev else CH - 1), dgl * gl, 0.0)
                for rev, dqt, qt, dkt, kt, dgl, gl in zip(revs, dqts, qts, dkts, kts, dgls, gls)]
        r = lax.broadcasted_iota(jnp.int32, (CH, CH), 0)
        c_ = lax.broadcasted_iota(jnp.int32, (CH, CH), 1)
        dlas = [jnp.dot(((c_ <= r) if rev else (c_ >= r)).astype(F32), dbc, preferred_element_type=F32, precision=HIGHEST)
                for rev, dbc in zip(revs, dbcs)]
        for c, dla, dqt, dkt, bc in zip(ch, dlas, dqts, dkts, bcs):
            c[12][:, _hk(c[2])] = dla
            c[9][:, _hk(c[2])] = dqt * (GLA_SCALE * jnp.exp(bc))
            c[10][:, _hk(c[2])] = dkt * jnp.exp(-bc)

    k_shape = jax.ShapeDtypeStruct((t, GH * GDK), F32)
    v_shape = jax.ShapeDtypeStruct((t, GH * GDV), F32)
    res = pl.pallas_call(
        body, name=name, grid=(nc,),
        in_specs=[sp for q_s, k_s, v_s, la_s, o_s, _, st_s in specs for sp in (q_s, k_s, v_s, la_s, o_s, st_s)],
        out_specs=tuple(sp for _, _, _, _, o_s, dk_s, _ in specs for sp in (dk_s, dk_s, o_s, dk_s)),
        out_shape=(k_shape, k_shape, v_shape, k_shape) * 2,
        scratch_shapes=[pltpu.VMEM((2, GH, GDV, GDK), F32)], compiler_params=_cparams(1, 32),
    )(p, p, p, la, do, stf, p, p, p, la, do, stb)
    return res[:4], res[4:]


def _gla_merge_bwd(gf, gb, z, p, wd, dp, name):
    t = p.shape[0]
    w2 = GH * GDK

    def body(dqf, dkf, dvf, dlf, dqb, dkb, dvb, dlb, z_ref, r_ref, w_ref, _, dp_ref, dr_ref, db_ref, dw_ref):
        i = pl.program_id(0)
        dp_ref[:, 0:D] = (dvf[...] + dvb[...]).astype(BF16)
        dp_ref[:, D:D + w2] = (dqf[...] + dqb[...]).astype(BF16)
        dp_ref[:, D + w2:D + 2 * w2] = (dkf[...] + dkb[...]).astype(BF16)
        zv = z_ref[...]
        dz = jnp.concatenate([dlf[...], dlb[...]], axis=1) * (_sigmoid(-zv) / GLA_TAU)
        dzb = dz.astype(BF16)
        dr_ref[...] = lax.dot_general(dzb, w_ref[...].astype(BF16), _NT, preferred_element_type=F32).astype(BF16)

        @pl.when(i == 0)
        def _():
            db_ref[...] = jnp.zeros_like(db_ref)
            dw_ref[...] = jnp.zeros_like(dw_ref)

        db_ref[...] += jnp.sum(dz, axis=0, keepdims=True)
        dw_ref[...] += lax.dot_general(r_ref[...].astype(BF16), dzb, _TN, preferred_element_type=F32)

    half = pl.BlockSpec((TM, w2), lambda i: (i, 0))
    row = pl.BlockSpec((TM, D), lambda i: (i, 0))
    wspec = pl.BlockSpec((R_PAD, D), lambda i: (0, 0))
    return pl.pallas_call(
        body, name=name, grid=(t // TM,),
        in_specs=[half, half, row, half, half, half, row, half, row, _pcol("r", R_PAD), wspec, _HBM],
        out_specs=(_dp_spec("gla"), pl.BlockSpec((TM, R_PAD), lambda i: (i, 0)),
                   pl.BlockSpec((8, D), lambda i: (0, 0)), wspec),
        out_shape=(jax.ShapeDtypeStruct(dp.shape, dp.dtype), jax.ShapeDtypeStruct((t, R_PAD), BF16),
                   jax.ShapeDtypeStruct((8, D), F32), jax.ShapeDtypeStruct((R_PAD, D), F32)),
        input_output_aliases={11: 0}, compiler_params=_cparams(1, 40),
    )(*gf, *gb, z, p, wd, dp)


def _dp_tail(dk, dv, dr, dp, name):
    t = dk.shape[0]
    wk = NKV * HD

    def body(dk_ref, dv_ref, dr_ref, _, dp_ref):
        dp_ref[:, 0:wk] = dk_ref[...]
        dp_ref[:, wk:2 * wk] = dv_ref[...].astype(BF16)
        dp_ref[:, 2 * wk:2 * wk + R_PAD] = dr_ref[...]
        dp_ref[:, 2 * wk + R_PAD:] = jnp.zeros((TM, DP_BLOCKS["tail"][1] - 2 * wk - R_PAD), BF16)

    kv = pl.BlockSpec((TM, wk), lambda i: (i, 0))
    return pl.pallas_call(
        body, name=name, grid=(t // TM,),
        in_specs=[kv, kv, pl.BlockSpec((TM, R_PAD), lambda i: (i, 0)), _HBM], out_specs=_dp_spec("tail"),
        out_shape=jax.ShapeDtypeStruct(dp.shape, dp.dtype), input_output_aliases={3: 0},
        compiler_params=_cparams(1, 32),
    )(dk, dv, dr, dp)


def _branch_fwd(att, of, ob, p, gla_g, name):
    t = p.shape[0]

    def body(att_ref, of_ref, ob_ref, za_ref, zg_ref, g_ref, yb_ref, yc_ref):
        za = za_ref[...].astype(F32)
        yb_ref[...] = (att_ref[...] * (za * _sigmoid(za))).astype(BF16)
        for h in range(GH):
            sl = slice(h * GDV, (h + 1) * GDV)
            o = of_ref[:, sl] + ob_ref[:, sl]
            n = o * lax.rsqrt(jnp.mean(o * o, axis=-1, keepdims=True) + EPS) * g_ref[...]
            zh = zg_ref[:, sl].astype(F32)
            yc_ref[:, sl] = (n * (zh * _sigmoid(zh))).astype(BF16)

    row = pl.BlockSpec((TM, D), lambda i: (i, 0))
    return pl.pallas_call(
        body, name=name, grid=(t // TM,),
        in_specs=[row, row, row, _pcol("z_attn", D), _pcol("zg", D), pl.BlockSpec((1, GDV), lambda i: (0, 0))],
        out_specs=(row, row),
        out_shape=(jax.ShapeDtypeStruct((t, D), BF16), jax.ShapeDtypeStruct((t, D), BF16)),
        compiler_params=_cparams(1, 40),
    )(att, of, ob, p, p, gla_g)


def _branch_bwd(dyb, dyc, att, of, ob, p, gla_g, dp, name):
    t = p.shape[0]

    def body(dyb_ref, dyc_ref, att_ref, of_ref, ob_ref, za_ref, zg_ref, g_ref, _, datt_ref, do_ref, dp_ref, dg_ref):
        i = pl.program_id(0)

        @pl.when(i == 0)
        def _():
            dg_ref[...] = jnp.zeros_like(dg_ref)

        za, dyb = za_ref[...].astype(F32), dyb_ref[...]
        sa = _sigmoid(za)
        datt_ref[...] = dyb * (za * sa)
        dp_ref[:, 0:D] = (dyb * att_ref[...] * (sa * (1.0 + za * (1.0 - sa)))).astype(BF16)
        g = g_ref[...]
        for h in range(GH):
            sl = slice(h * GDV, (h + 1) * GDV)
            o = of_ref[:, sl] + ob_ref[:, sl]
            r = lax.rsqrt(jnp.mean(o * o, axis=-1, keepdims=True) + EPS)
            oh = o * r
            zh, dyc = zg_ref[:, sl].astype(F32), dyc_ref[:, sl]
            sg = _sigmoid(zh)
            dn = dyc * (zh * sg)
            dp_ref[:, D + h * GDV:D + (h + 1) * GDV] = (dyc * (oh * g) * (sg * (1.0 + zh * (1.0 - sg)))).astype(BF16)
            doh = dn * g
            do_ref[:, sl] = r * (doh - oh * jnp.mean(doh * oh, axis=-1, keepdims=True))
            dg_ref[...] += jnp.sum(dn * oh, axis=0, keepdims=True)

    row = pl.BlockSpec((TM, D), lambda i: (i, 0))
    return pl.pallas_call(
        body, name=name, grid=(t // TM,),
        in_specs=[row, row, row, row, row, _pcol("z_attn", D), _pcol("zg", D), pl.BlockSpec((1, GDV), lambda i: (0, 0)),
                  _HBM],
        out_specs=(row, row, _dp_spec("branch"), pl.BlockSpec((8, GDV), lambda i: (0, 0))),
        out_shape=(jax.ShapeDtypeStruct((t, D), F32), jax.ShapeDtypeStruct((t, D), F32),
                   jax.ShapeDtypeStruct(dp.shape, dp.dtype), jax.ShapeDtypeStruct((8, GDV), F32)),
        input_output_aliases={8: 2}, compiler_params=_cparams(1, 48),
    )(dyb, dyc, att, of, ob, p, p, gla_g, dp)


def _merge_fwd(bra, brb, brc, p, b_gate, name):
    t = p.shape[0]
    mgb = OFF["mg"] // D

    def body(a_ref, b_ref, c_ref, ga_ref, gb_ref, gc_ref, bg_ref, m_ref):
        m_ref[...] = (_sigmoid(ga_ref[...].astype(F32) + bg_ref[:, 0:D]) * a_ref[...]
                      + _sigmoid(gb_ref[...].astype(F32) + bg_ref[:, D:2 * D]) * b_ref[...]
                      + _sigmoid(gc_ref[...].astype(F32) + bg_ref[:, 2 * D:3 * D]) * c_ref[...]).astype(BF16)

    row = pl.BlockSpec((TM, D), lambda i: (i, 0))
    gates = [pl.BlockSpec((TM, D), functools.partial(lambda i, b: (i, b), b=mgb + j)) for j in range(3)]
    return pl.pallas_call(
        body, name=name, grid=(t // TM,),
        in_specs=[row, row, row, *gates, pl.BlockSpec((1, 3 * D), lambda i: (0, 0))],
        out_specs=row, out_shape=jax.ShapeDtypeStruct((t, D), BF16), compiler_params=_cparams(1, 40),
    )(bra, brb, brc, p, p, p, b_gate)


def _merge_bwd(dm, bra, brb, brc, p, b_gate, name):
    t = p.shape[0]
    mgb = OFF["mg"] // D

    def body(dm_ref, a_ref, b_ref, c_ref, ga_ref, gb_ref, gc_ref, bg_ref, da_ref, db_ref, dc_ref, dmg_ref, dbg_ref):
        i = pl.program_id(0)

        @pl.when(i == 0)
        def _():
            dbg_ref[...] = jnp.zeros_like(dbg_ref)

        dm = dm_ref[...]
        for j, (br_ref, g_ref, d_ref) in enumerate(((a_ref, ga_ref, da_ref), (b_ref, gb_ref, db_ref), (c_ref, gc_ref, dc_ref))):
            sl = slice(j * D, (j + 1) * D)
            g = _sigmoid(g_ref[...].astype(F32) + bg_ref[:, sl])
            d_ref[...] = (dm * g).astype(BF16)
            dmg = dm * br_ref[...] * (g * (1.0 - g))
            dmg_ref[:, sl] = dmg.astype(BF16)
            dbg_ref[:, sl] += jnp.sum(dmg, axis=0, keepdims=True)

    row = pl.BlockSpec((TM, D), lambda i: (i, 0))
    gates = [pl.BlockSpec((TM, D), functools.partial(lambda i, b: (i, b), b=mgb + j)) for j in range(3)]
    return pl.pallas_call(
        body, name=name, grid=(t // TM,),
        in_specs=[row, row, row, row, *gates, pl.BlockSpec((1, 3 * D), lambda i: (0, 0))],
        out_specs=(row, row, row, _dp_spec("merge"), pl.BlockSpec((8, 3 * D), lambda i: (0, 0))),
        out_shape=(jax.ShapeDtypeStruct((t, D), BF16),) * 3 + (jax.ShapeDtypeStruct((t, NP), BF16),
                                                                jax.ShapeDtypeStruct((8, 3 * D), F32)),
        compiler_params=_cparams(1, 48),
    )(dm, bra, brb, brc, p, p, p, b_gate)


def _adam_update(ns, g_ref, w_ref, m_ref, v_ref, go_ref, d_ref, mo_ref, vo_ref):
    g = g_ref[0].astype(F32)
    for s in range(1, ns):
        g = g + g_ref[s].astype(F32)
    mn = ADAM_B1 * m_ref[...] + (1.0 - ADAM_B1) * g
    vn = ADAM_B2 * v_ref[...] + (1.0 - ADAM_B2) * jnp.square(g)
    m_hat = mn / (1.0 - ADAM_B1 ** ADAM_STEP)
    v_hat = vn / (1.0 - ADAM_B2 ** ADAM_STEP)
    go_ref[...] = g
    d_ref[...] = -ADAM_LR * (m_hat / (jnp.sqrt(v_hat) + ADAM_EPS) + ADAM_WD * w_ref[...])
    mo_ref[...] = mn
    vo_ref[...] = vn


def _adamw(gsrc, w, m, v, name):
    ns, nl, r, c = gsrc.shape
    gb = gsrc.dtype.itemsize

    def fits(rows, cols):
        lanes = -(-cols // LANE) * LANE
        return ns * rows * lanes * gb <= ADAM_SRC_BYTES and rows * lanes * 4 <= ADAM_ROW_BYTES

    tr, tc = r, c
    if not fits(r, c):
        rows = [cand for cand in range(16, r, 16) if r % cand == 0 and fits(cand, c)]
        cols = [cand for cand in range(LANE, c, LANE) if c % cand == 0 and fits(r, cand)]
        if rows:
            tr = rows[-1]
        else:
            tc = cols[-1]

    def body(*refs):
        _adam_update(ns, *refs)

    row = pl.BlockSpec((None, tr, tc), lambda l, i, j: (l, i, j))
    return pl.pallas_call(
        body, name=name, grid=(nl, r // tr, c // tc),
        in_specs=[pl.BlockSpec((ns, None, tr, tc), lambda l, i, j: (0, l, i, j)), row, row, row],
        out_specs=(row,) * 4, out_shape=(jax.ShapeDtypeStruct((nl, r, c), F32),) * 4,
        compiler_params=_cparams(3, 48),
    )(gsrc, w, m, v)


def _adamw_small(items, name):
    k = len(items)

    def body(*refs):
        for j in range(k):
            _adam_update(items[j][0].shape[0], *refs[4 * j:4 * j + 4], *refs[4 * k + 4 * j:4 * k + 4 * j + 4])

    out = pl.pallas_call(
        body, name=name,
        out_shape=tuple(jax.ShapeDtypeStruct(w.shape, F32) for _, w, _, _ in items for _ in range(4)),
    )(*[a for item in items for a in item])
    return [out[4 * j:4 * j + 4] for j in range(k)]


def _rope_tables(ctx, seq):
    n_rows = seq // GRID_W
    pairs = HD // 4
    row = jnp.repeat(jnp.arange(n_rows, dtype=F32), GRID_W)
    col = jnp.tile(jnp.arange(GRID_W, dtype=F32), n_rows)
    freqs = ROPE_THETA ** (-jnp.arange(pairs, dtype=F32) * 2.0 / (HD // 2))
    ar, ac = row[:, None] * freqs, col[:, None] * freqs
    cos_l = jnp.concatenate([jnp.cos(ar), jnp.cos(ar), jnp.cos(ac), jnp.cos(ac)], axis=1)
    sin_l = jnp.concatenate([-jnp.sin(ar), jnp.sin(ar), -jnp.sin(ac), jnp.sin(ac)], axis=1)
    cos_t = jnp.concatenate([jnp.ones((ctx, HD), F32), cos_l], axis=0)
    sin_t = jnp.concatenate([jnp.zeros((ctx, HD), F32), sin_l], axis=0)
    return cos_t, sin_t


def _to_proj_layout(wt):
    parts = [wt[s:s + wd] for _, s, wd in _SEGS]
    used = sum(wd for _, _, wd in _SEGS)
    parts.append(jnp.zeros((NP - used, wt.shape[1]), wt.dtype))
    return jnp.concatenate(parts, axis=0)


def _from_proj_layout(g):
    order = sorted(_SEGS, key=lambda sg: sg[1])
    return jnp.concatenate([g[OFF[n]:OFF[n] + wd] for n, _, wd in order], axis=0)


def _row0(a):
    return a[..., 0, :]


def kernel(x, c, ctx, c_ctx, w_ada, b_ada, g_pre, g_post, w_in, conv_w, q_norm_g, k_norm_g, w_decay_fwd, b_decay_fwd, w_decay_bwd, b_decay_bwd, gla_norm_g, w_br_conv, w_br_attn, w_br_gla, b_gate, w_out, loss_target, m_c_ctx, m_w_ada, m_b_ada, m_g_pre, m_g_post, m_w_in, m_conv_w, m_q_norm_g, m_k_norm_g, m_w_decay_fwd, m_b_decay_fwd, m_w_decay_bwd, m_b_decay_bwd, m_gla_norm_g, m_w_br_conv, m_w_br_attn, m_w_br_gla, m_b_gate, m_w_out, v_c_ctx, v_w_ada, v_b_ada, v_g_pre, v_g_post, v_w_in, v_conv_w, v_q_norm_g, v_k_norm_g, v_w_decay_fwd, v_b_decay_fwd, v_w_decay_bwd, v_b_decay_bwd, v_gla_norm_g, v_w_br_conv, v_w_br_attn, v_w_br_gla, v_b_gate, v_w_out):
    seq, n_ctx = x.shape[1], ctx.shape[1]
    assert n_ctx % TM == 0 and seq % TM == 0 and seq % GRID_W == 0
    t = n_ctx + seq
    nct, ncc = n_ctx // TM, n_ctx // CH
    dev = 4 * lax.axis_index("x") + 2 * lax.axis_index("y") + lax.axis_index("c")
    ada_w = w_ada.shape[2]
    in_w = w_in.shape[2]
    br_r = w_br_conv.shape[1]

    def in_t(a, l):
        return a.transpose(2, 0, 1)[:, l, :]

    wb = [w.astype(BF16) for w in (w_ada, w_br_conv, w_br_attn, w_br_gla, w_out)]
    wall = _all_gather([wb[0][0], in_t(w_in, 0).astype(BF16), conv_w, w_decay_fwd, w_decay_bwd],
                       "gather_first")
    later = _GatherRider([in_t(w_in, 1).astype(BF16), wb[0][1], wb[1], wb[2], wb[3], wb[4]])

    def full_small(g):
        return g.transpose(1, 2, 0, 3).reshape(DEPTH, g.shape[2], NDEV * g.shape[3])

    def full_in(g):
        return _to_proj_layout(g.reshape(IN_WIDTH, D))

    def full_ada(g):
        return g.transpose(1, 0, 2).reshape(D, 3 * D)

    w_ada_f = [full_ada(wall[0]), None]
    wp = [full_in(wall[1]), None]
    conv_f, wdf_f, wdb_f = full_small(wall[2]), full_small(wall[3]), full_small(wall[4])

    cos_t, sin_t = _rope_tables(n_ctx, seq)
    cc = jnp.concatenate([c_ctx[None, :], c.reshape(1, D), jnp.zeros((6, D), F32)], axis=0)
    silu_cc, dsilu_cc = _ada_in(cc)

    conv8, wd_pad, bd = [], [], []
    for l in range(DEPTH):
        conv8.append(jnp.concatenate([conv_f[l], jnp.zeros((5, D), F32)], axis=0))
        zr = jnp.zeros((GLA_RANK, GH * GDK), F32)
        wd_pad.append(jnp.concatenate([jnp.concatenate([wdf_f[l], zr], axis=1), jnp.concatenate([zr, wdb_f[l]], axis=1),
                                       jnp.zeros((R_PAD - 2 * GLA_RANK, D), F32)], axis=0))
        bd.append(jnp.concatenate([b_decay_fwd[l], b_decay_bwd[l]])[None, :])

    xs = jnp.concatenate([ctx[0], x[0]], axis=0)
    saved = []
    for l in range(DEPTH):
        n = f"l{l}_"
        mod = _mm(silu_cc, w_ada_f[l], n + "mod", bias=b_ada[l][None, :])
        mod3 = mod[0:2].reshape(2, 3, D)
        h = _prenorm_fwd(xs, g_pre[l][None, :], mod3, nct, n + "prenorm")
        p = _mm(h, wp[l], n + "proj", tb=True, out_dtype=BF16)
        cv, ya = _conv_fwd(p, conv8[l], nct, n + "conv")
        qr, kr = _qk_prep_fwd(p, q_norm_g[l][None, :], k_norm_g[l][None, :], cos_t, sin_t, n + "qk_prep")
        att, lse, *got = _attn_fwd(qr, kr, p, nct, n + "attn", rider=later if l == 0 else None)
        if l == 0:
            wp[1], w_ada_f[1] = full_in(got[0]), full_ada(got[1])
            w_brs_f = [g.transpose(1, 0, 2, 3).reshape(DEPTH, D, D) for g in got[2:]]
        z, la = _decay_fwd(p, wd_pad[l], bd[l], n + "decay")
        of, stf, ob, stb = _gla_fwd(p, la, ncc, n + "gla")
        yb, yc = _branch_fwd(att, of, ob, p, gla_norm_g[l][None, :], n + "branch")
        bra = _mm(ya, w_brs_f[0][l], n + "br_conv")
        brb = _mm(yb, w_brs_f[1][l], n + "br_attn")
        brc = _mm(yc, w_brs_f[2][l], n + "br_gla")
        mm_ = _merge_fwd(bra, brb, brc, p, b_gate[l][None, :], n + "merge")
        out = _mm(mm_, w_brs_f[3][l], n + "out")
        x_new = _post_fwd(xs, out, g_post[l][None, :], mod3, nct, n + "post")
        saved.append(dict(x=xs, mod3=mod3, h=h, p=p, cv=cv, ya=ya, qr=qr, kr=kr, att=att, lse=lse, z=z, la=la, of=of, ob=ob,
                          stf=stf, stb=stb, yb=yb, yc=yc, bra=bra, brb=brb, brc=brc, m=mm_, out=out))
        xs = x_new

    dx, sq = _loss_grad(xs, loss_target[0], nct, "loss")
    loss = lax.psum(0.5 * sq[0, 0] / D, ("x", "y", "c"))

    gw = {k: [None] * DEPTH for k in ("w_in", "br_conv", "br_attn", "br_gla", "out", "b_gate", "g_pre", "g_post",
                                      "conv_w", "qg", "kg", "wd", "bdec", "gla_g", "dmod")}
    dctx = []

    def in_slots(l):
        return _from_proj_layout(gw["w_in"][l]).reshape(NDEV, in_w, D)

    def br_slots():
        return [jnp.stack([gw[k][l].reshape(NDEV, br_r, D) for l in range(DEPTH)], axis=1)
                for k in ("br_conv", "br_attn", "br_gla", "out")]

    for l in reversed(range(DEPTH)):
        n = f"l{l}_b_"
        s = saved[l]
        p = s["p"]
        d_out, dgt, gw["g_post"][l] = _post_bwd(dx, s["out"], g_post[l][None, :], s["mod3"], nct, n + "post")
        dm = _mm(d_out, w_brs_f[3][l], n + "dm", tb=True)
        gw["out"][l] = _mm(s["m"], d_out, n + "dw_out", ta=True, out_dtype=BF16)
        dbra, dbrb, dbrc, dp, gw["b_gate"][l] = _merge_bwd(dm, s["bra"], s["brb"], s["brc"], p, b_gate[l][None, :], n + "merge")
        dya = _mm(dbra, w_brs_f[0][l], n + "dya", tb=True)
        dyb = _mm(dbrb, w_brs_f[1][l], n + "dyb", tb=True)
        dyc = _mm(dbrc, w_brs_f[2][l], n + "dyc", tb=True)
        gw["br_conv"][l] = _mm(s["ya"], dbra, n + "dw_conv", ta=True, out_dtype=BF16)
        gw["br_attn"][l] = _mm(s["yb"], dbrb, n + "dw_attn", ta=True, out_dtype=BF16)
        gw["br_gla"][l] = _mm(s["yc"], dbrc, n + "dw_gla", ta=True, out_dtype=BF16)
        dcv, dp = _conv_bwd_a(dya, p, s["cv"], dp, n + "conv_a")
        dp, gw["conv_w"][l] = _conv_bwd_b(dcv, p, conv8[l], nct, dp, n + "conv_b")
        datt, dgo, dp, gw["gla_g"][l] = _branch_bwd(dyb, dyc, s["att"], s["of"], s["ob"], p, gla_norm_g[l][None, :], dp, n + "branch")
        ex1 = _ExchangeRider([in_slots(DEPTH - 1)] + br_slots()) if l == 0 else None
        dqr, dkr, dv, *got = _attn_bwd(s["qr"], s["kr"], p, s["att"], s["lse"], datt, nct, n + "attn", rider=ex1)
        if l == 0:
            recv_in1, recv_br = got[0], got[1:]
        dp, dk, gw["qg"][l], gw["kg"][l] = _qk_prep_bwd(dqr, dkr, p, q_norm_g[l][None, :], k_norm_g[l][None, :], cos_t, sin_t, dp, n + "qk_prep")
        gf, gb = _gla_bwd(p, s["la"], dgo, s["stf"], s["stb"], ncc, n + "gla")
        dp, dr, gw["bdec"][l], gw["wd"][l] = _gla_merge_bwd(gf, gb, s["z"], p, wd_pad[l], dp, n + "gla_merge")
        dp = _dp_tail(dk, dv, dr, dp, n + "dp_tail")
        gw["w_in"][l] = _mm(dp, s["h"], n + "dw_in", ta=True, out_dtype=BF16, tk=t // 2 if t % 32 == 0 else None)
        if l == 0:
            dh, recv_in0 = _mm(dp, wp[l], n + "dh", tk=NP // 4, rider=_ExchangeRider([in_slots(0)]))
        else:
            dh = _mm(dp, wp[l], n + "dh", tk=NP // 4)
        dx, dsh, dsc, gw["g_pre"][l] = _prenorm_bwd(dh, s["x"], dx, g_pre[l][None, :], s["mod3"], nct, n + "prenorm")
        dmod = jnp.stack([_row0(dsh), _row0(dsc), _row0(dgt)], axis=1).reshape(2, 3 * D)
        gw["dmod"][l] = dmod
        dmod8 = jnp.concatenate([dmod, jnp.zeros((6, 3 * D), F32)], axis=0)
        dctx.append(_mm(dmod8, w_ada_f[l], n + "dsilu", tb=True))
    grad_x = dx[n_ctx:][None]
    g_cctx = _cctx_grad(dctx[0], dctx[1], dsilu_cc)[0]

    def st2(name):
        return jnp.stack(gw[name])

    g_b_ada = jnp.stack([gw["dmod"][l][0] + gw["dmod"][l][1] for l in range(DEPTH)])
    g_bdf = jnp.stack([gw["bdec"][l][0, :GH * GDK] for l in range(DEPTH)])
    g_bdb = jnp.stack([gw["bdec"][l][0, GH * GDK:] for l in range(DEPTH)])
    g_wdf = jnp.stack([gw["wd"][l][0:GLA_RANK, :GH * GDK] for l in range(DEPTH)])
    g_wdb = jnp.stack([gw["wd"][l][GLA_RANK:2 * GLA_RANK, GH * GDK:] for l in range(DEPTH)])
    rep_grads = [g_cctx, g_b_ada, st2("g_pre")[:, 0], st2("g_post")[:, 0], st2("qg")[:, 0], st2("kg")[:, 0], g_bdf, g_bdb,
                 st2("gla_g")[:, 0], st2("b_gate")[:, 0]]
    rep_w = [c_ctx, b_ada, g_pre, g_post, q_norm_g, k_norm_g, b_decay_fwd, b_decay_bwd, gla_norm_g, b_gate]
    rep_m = [m_c_ctx, m_b_ada, m_g_pre, m_g_post, m_q_norm_g, m_k_norm_g, m_b_decay_fwd, m_b_decay_bwd, m_gla_norm_g, m_b_gate]
    rep_v = [v_c_ctx, v_b_ada, v_g_pre, v_g_post, v_q_norm_g, v_k_norm_g, v_b_decay_fwd, v_b_decay_bwd, v_gla_norm_g, v_b_gate]
    def two_d(a):
        return a.reshape(1, -1) if a.ndim == 1 else a

    def owner_slots(g):
        return g.reshape(DEPTH, g.shape[1], NDEV, g.shape[2] // NDEV).transpose(2, 0, 1, 3)

    n_rep = len(rep_grads)
    small = _comm_alone(_Riders([
        _GatherRider([two_d(g) for g in rep_grads] + [silu_cc[0:2], jnp.stack(gw["dmod"])]),
        _ExchangeRider([owner_slots(st2("conv_w")[:, 0:3]), owner_slots(g_wdf), owner_slots(g_wdb)])]),
        "exchange_small_grads")
    rep_src, (a_all, d_all), sh_src = small[:n_rep], small[n_rep:n_rep + 2], small[n_rep + 2:]
    sh_w = [conv_w, w_decay_fwd, w_decay_bwd]
    sh_m = [m_conv_w, m_w_decay_fwd, m_w_decay_bwd]
    sh_v = [v_conv_w, v_w_decay_fwd, v_w_decay_bwd]
    small_out = _adamw_small(
        [(g, two_d(w), two_d(m), two_d(v)) for g, w, m, v in zip(rep_src, rep_w, rep_m, rep_v)]
        + list(zip(sh_src, sh_w, sh_m, sh_v)), "adam_small")
    rep_g, rep_d, rep_nm, rep_nv = [[small_out[j][k].reshape(rep_w[j].shape) for j in range(n_rep)] for k in range(4)]
    sh_gr, sh_d, sh_nm, sh_nv = [[small_out[n_rep + j][k] for j in range(len(sh_w))] for k in range(4)]

    a_all = a_all.reshape(NDEV * 2, D)
    d_all = d_all.transpose(1, 0, 2, 3).reshape(DEPTH, NDEV * 2, 3 * D)
    g_ada = jnp.stack([_mm(a_all, lax.dynamic_slice_in_dim(d_all[l], dev * ada_w, ada_w, axis=1), f"dw_ada{l}",
                           ta=True, precise=True, tk=NDEV * 2) for l in range(DEPTH)])
    ada_g, ada_d, ada_nm, ada_nv = _adamw(g_ada[None], w_ada, m_w_ada, v_w_ada, "adam_ada")

    big_w = [w_br_conv, w_br_attn, w_br_gla, w_out]
    big_m = [m_w_br_conv, m_w_br_attn, m_w_br_gla, m_w_out]
    big_v = [v_w_br_conv, v_w_br_attn, v_w_br_gla, v_w_out]
    big_out = [_adamw(recv_br[j], big_w[j], big_m[j], big_v[j], f"adam_big{j}") for j in range(len(big_w))]
    in_out = [_adamw(r_[:, None], in_t(w_in, l)[None], in_t(m_w_in, l)[None], in_t(v_w_in, l)[None], f"adam_in{l}")
              for l, r_ in enumerate((recv_in0, recv_in1))]
    in_res = [jnp.stack([in_out[l][k][0] for l in range(DEPTH)], axis=1).transpose(1, 2, 0) for k in range(4)]
    big_g, big_d, big_nm, big_nv = [[in_res[k]] + [o[k] for o in big_out] for k in range(4)]

    def ordered(rep, ada, big, sh):
        c_ctx_, b_ada_, g_pre_, g_post_, qg_, kg_, bdf_, bdb_, glag_, bgate_ = rep
        w_in_, brc_, bra_, brg_, wout_ = big
        conv_, wdf_, wdb_ = sh
        return [c_ctx_, ada, b_ada_, g_pre_, g_post_, w_in_, conv_, qg_, kg_, wdf_, bdf_, wdb_, bdb_, glag_,
                brc_, bra_, brg_, bgate_, wout_]

    return (loss, grad_x,
            *ordered(rep_g, ada_g, big_g, sh_gr), *ordered(rep_d, ada_d, big_d, sh_d),
            *ordered(rep_nm, ada_nm, big_nm, sh_nm), *ordered(rep_nv, ada_nv, big_nv, sh_nv))
```

```python
import functools

import numpy as np
import jax
import jax.numpy as jnp
from jax import lax
from jax.experimental import pallas as pl
from jax.experimental.pallas import tpu as pltpu

F32, BF16 = jnp.float32, jnp.bfloat16
HIGHEST = lax.Precision.HIGHEST

D = 1024
DEPTH = 2
GRID_W = 64
NH, NKV, HD = 8, 2, 128
GROUP = NH // NKV
ROPE_THETA = 10000.0
ATTN_SCALE = HD ** -0.5
Q_FOLD = ATTN_SCALE * 1.4426950408889634
P_HALO = 16
GH, GDK, GDV = 4, 128, 256
GLA_RANK = 16
GLA_TAU = 16.0
CH = 64
GLA_SCALE = GDK ** -0.5
EPS = 1e-6
NDEV = 8
LANE = 128
TM = 256
ATTN_HEADS_PER_STEP = 1
ATTN_KEY_CHUNK = 8192

ADAM_LR, ADAM_B1, ADAM_B2, ADAM_EPS, ADAM_WD, ADAM_STEP = 0.001, 0.9, 0.999, 1e-08, 0.01, 10

_SEGS = (("a_b", 0, 1024), ("a_z", 3072, 1024), ("a_c", 1024, 1024), ("a_x", 2048, 1024),
         ("z_attn", 5632, 1024), ("zg", 8736, 1024), ("gv", 7680, 1024), ("gq", 6656, 512), ("gk", 7168, 512),
         ("q", 4096, 1024), ("mg", 9760, 3072), ("k", 5120, 256), ("v", 5376, 256), ("r", 8704, 32))
DP_BLOCKS = {"conv_a": ("a_b", 2048), "conv_b": ("a_c", 2048), "branch": ("z_attn", 2048), "gla": ("gv", 2048),
             "q": ("q", 1024), "merge": ("mg", 3072), "tail": ("k", 1024)}
IN_WIDTH = 12832
NP = 13312
OFF = {}
_o = 0
for _n, _s, _w in _SEGS:
    OFF[_n] = _o
    _o += _w
R_PAD = 128


def _cparams(ngrid, vmem_mb):
    return pltpu.CompilerParams(dimension_semantics=("arbitrary",) * ngrid, vmem_limit_bytes=vmem_mb << 20)


def _pick(n, cands):
    for c in cands:
        if n % c == 0:
            return c
    return n


def _sigmoid(x):
    return 1.0 / (1.0 + jnp.exp(-x))


ADAM_SRC_BYTES = 8 << 20
ADAM_ROW_BYTES = 1 << 20


def _all_gather(xs, name):
    return _comm_alone(_GatherRider(xs), name)


_HBM = pl.BlockSpec(memory_space=pl.ANY)


class _Rider:
    def __init__(self, xs, out_shapes):
        self.xs, self.n = list(xs), len(xs)
        self.out_shape = [jax.ShapeDtypeStruct(s, x.dtype) for s, x in zip(out_shapes, xs)]
        self.scratch = [pltpu.SemaphoreType.DMA((7 * self.n,)), pltpu.SemaphoreType.DMA((7 * self.n,)),
                        pltpu.SemaphoreType.DMA((self.n,))]


class _GatherRider(_Rider):
    def __init__(self, xs):
        super().__init__(xs, [(NDEV,) + x.shape for x in xs])

    def _parts(self, x_refs, out_refs, sems):
        n = self.n
        send_sems, recv_sems, local_sems = sems
        mx, my, mc = lax.axis_index("x"), lax.axis_index("y"), lax.axis_index("c")
        me, sibling = (mx, my, mc), (mx, my, 1 - mc)
        chips = [(1 - mx, my), (mx, 1 - my), (1 - mx, 1 - my)]

        def slot(a, px, py, pc):
            return out_refs[a].at[4 * px + 2 * py + pc]

        def copy(k, a, block, to, own=False):
            return pltpu.make_async_remote_copy(
                src_ref=x_refs[a] if own else slot(a, *block), dst_ref=slot(a, *block),
                send_sem=send_sems.at[k * n + a], recv_sem=recv_sems.at[k * n + a],
                device_id=to, device_id_type=pl.DeviceIdType.MESH)

        mine = [pltpu.make_async_copy(x_refs[a], slot(a, *me), local_sems.at[a]) for a in range(n)]
        first = [copy(0, a, me, sibling, own=True) for a in range(n)]
        first += [copy(1 + j, a, me, (*chip, mc), own=True) for a in range(n) for j, chip in enumerate(chips)]
        landed = [copy(1 + j, a, (*chip, mc), me) for a in range(n) for j, chip in enumerate(chips)]
        passed = [copy(4 + j, a, (*chip, mc), sibling) for a in range(n) for j, chip in enumerate(chips)]
        from_sibling = [copy(0, a, sibling, me) for a in range(n)]
        from_sibling += [copy(4 + j, a, (*chip, 1 - mc), me) for a in range(n) for j, chip in enumerate(chips)]
        return mine, first, landed, passed, from_sibling

    def start(self, x_refs, out_refs, sems):
        mine, first, _, _, _ = self._parts(x_refs, out_refs, sems)
        for cp in mine + first:
            cp.start()

    def middle(self, x_refs, out_refs, sems):
        _, _, landed, passed, _ = self._parts(x_refs, out_refs, sems)
        for got, fwd in zip(landed, passed):
            got.wait_recv()
            fwd.start()

    def finish(self, x_refs, out_refs, sems):
        mine, first, _, passed, from_sibling = self._parts(x_refs, out_refs, sems)
        for cp in from_sibling:
            cp.wait_recv()
        for cp in first + passed:
            cp.wait_send()
        for cp in mine:
            cp.wait()


class _ExchangeRider(_Rider):
    def __init__(self, xs):
        super().__init__(xs, [x.shape for x in xs])

    def _parts(self, x_refs, out_refs, sems):
        n = self.n
        send_sems, recv_sems, local_sems = sems
        mx, my, mc = lax.axis_index("x"), lax.axis_index("y"), lax.axis_index("c")
        me = 4 * mx + 2 * my + mc
        mine = [pltpu.make_async_copy(x_refs[a].at[me], out_refs[a].at[me], local_sems.at[a]) for a in range(n)]
        copies = []
        for a in range(n):
            for rel in range(1, NDEV):
                px = (1 - mx) if rel & 4 else mx
                py = (1 - my) if rel & 2 else my
                pc = (1 - mc) if rel & 1 else mc
                peer = 4 * px + 2 * py + pc
                k = (rel - 1) * n + a
                copies.append(pltpu.make_async_remote_copy(
                    src_ref=x_refs[a].at[peer], dst_ref=out_refs[a].at[me],
                    send_sem=send_sems.at[k], recv_sem=recv_sems.at[k],
                    device_id=(px, py, pc), device_id_type=pl.DeviceIdType.MESH))
        return mine, copies

    def start(self, x_refs, out_refs, sems):
        mine, copies = self._parts(x_refs, out_refs, sems)
        for cp in mine + copies:
            cp.start()

    def middle(self, x_refs, out_refs, sems):
        pass

    def finish(self, x_refs, out_refs, sems):
        mine, copies = self._parts(x_refs, out_refs, sems)
        for cp in copies:
            cp.wait_recv()
        for cp in copies:
            cp.wait_send()
        for cp in mine:
            cp.wait()


class _Riders:
    def __init__(self, riders):
        self.riders = list(riders)
        self.xs = [x for r in self.riders for x in r.xs]
        self.n = len(self.xs)
        self.out_shape = [s for r in self.riders for s in r.out_shape]
        self.scratch = [s for r in self.riders for s in r.scratch]

    def _each(self, method, x_refs, out_refs, sems):
        a = b = 0
        for r in self.riders:
            getattr(r, method)(x_refs[a:a + r.n], out_refs[a:a + r.n], sems[b:b + len(r.scratch)])
            a, b = a + r.n, b + len(r.scratch)

    def start(self, *refs):
        self._each("start", *refs)

    def middle(self, *refs):
        self._each("middle", *refs)

    def finish(self, *refs):
        self._each("finish", *refs)


def _comm_alone(rider, name):
    n = rider.n

    def body(*refs):
        x_refs, out_refs, sems = refs[:n], refs[n:2 * n], refs[2 * n:]
        rider.start(x_refs, out_refs, sems)
        rider.middle(x_refs, out_refs, sems)
        rider.finish(x_refs, out_refs, sems)

    return pl.pallas_call(
        body, name=name, out_shape=tuple(rider.out_shape), in_specs=[_HBM] * n, out_specs=(_HBM,) * n,
        scratch_shapes=rider.scratch,
    )(*rider.xs)


def _with_rider(body, nin, nout, rider, first, mid, last):
    if rider is None:
        return body
    n = rider.n

    def wrapped(*refs):
        ins, x_refs = refs[:nin], refs[nin:nin + n]
        outs, out_refs = refs[nin + n:nin + n + nout], refs[nin + n + nout:nin + 2 * n + nout]
        ns = len(rider.scratch)
        scratch, sems = refs[nin + 2 * n + nout:len(refs) - ns], refs[len(refs) - ns:]

        @pl.when(first())
        def _():
            rider.start(x_refs, out_refs, sems)

        body(*ins, *outs, *scratch)

        @pl.when(mid())
        def _():
            rider.middle(x_refs, out_refs, sems)

        @pl.when(last())
        def _():
            rider.finish(x_refs, out_refs, sems)

    return wrapped


def _mm(a, b, name, ta=False, tb=False, out_dtype=F32, bias=None, precise=False, tm=None, tn=None, tk=None, rider=None):
    m, k = (a.shape[1], a.shape[0]) if ta else a.shape
    n = b.shape[0] if tb else b.shape[1]
    assert k == (b.shape[1] if tb else b.shape[0])
    tm = tm or _pick(m, (1088, 1024, 512, 256, 128))
    tn = tn or _pick(n, (1024, 512, 384, 256, 128))
    tk = tk or _pick(k, (1024, 1088, 512, 256, 128))
    nk = k // tk
    dn = (((0 if ta else 1,), (1 if tb else 0,)), ((), ()))

    def body(*refs):
        if bias is None:
            a_ref, b_ref, o_ref = refs[:3]
            bias_ref = None
        else:
            a_ref, b_ref, bias_ref, o_ref = refs[:4]
        x, y = a_ref[...], b_ref[...]
        if precise:
            p = lax.dot_general(x.astype(F32), y.astype(F32), dn, preferred_element_type=F32, precision=HIGHEST)
        else:
            p = lax.dot_general(x.astype(BF16), y.astype(BF16), dn, preferred_element_type=F32)

        def finish(acc):
            if bias_ref is not None:
                acc = acc + bias_ref[...]
            o_ref[...] = acc.astype(out_dtype)

        if nk == 1:
            finish(p)
        else:
            acc_ref = refs[-1]
            kk = pl.program_id(2)

            @pl.when(kk == 0)
            def _():
                acc_ref[...] = p

            @pl.when(kk > 0)
            def _():
                acc_ref[...] += p

            @pl.when(kk == nk - 1)
            def _():
                finish(acc_ref[...])

    a_spec = pl.BlockSpec((tk, tm), lambda i, j, kk: (kk, i)) if ta else pl.BlockSpec((tm, tk), lambda i, j, kk: (i, kk))
    b_spec = pl.BlockSpec((tn, tk), lambda i, j, kk: (j, kk)) if tb else pl.BlockSpec((tk, tn), lambda i, j, kk: (kk, j))
    in_specs = [a_spec, b_spec]
    args = [a, b]
    if bias is not None:
        in_specs.append(pl.BlockSpec((1, tn), lambda i, j, kk: (0, j)))
        args.append(bias)
    grid = (m // tm, n // tn, nk)
    out_spec = pl.BlockSpec((tm, tn), lambda i, j, kk: (i, j))
    scratch = [pltpu.VMEM((tm, tn), F32)] if nk > 1 else []
    if rider is None:
        return pl.pallas_call(
            body, name=name, grid=grid, in_specs=in_specs, out_specs=out_spec,
            out_shape=jax.ShapeDtypeStruct((m, n), out_dtype), scratch_shapes=scratch, compiler_params=_cparams(3, 56),
        )(*args)

    def at(step):
        return lambda: ((pl.program_id(0) == step[0]) & (pl.program_id(1) == step[1]) & (pl.program_id(2) == step[2]))

    end = tuple(g - 1 for g in grid)
    return pl.pallas_call(
        _with_rider(body, len(args), 1, rider, at((0, 0, 0)), at((grid[0] // 2, 0, 0)), at(end)),
        name=name, grid=grid, in_specs=in_specs + [_HBM] * rider.n, out_specs=(out_spec,) + (_HBM,) * rider.n,
        out_shape=(jax.ShapeDtypeStruct((m, n), out_dtype),) + tuple(rider.out_shape),
        scratch_shapes=scratch + rider.scratch, compiler_params=_cparams(3, 56),
    )(*args, *rider.xs)


def _ada_in(cc):
    def body(c_ref, s_ref, d_ref):
        x = c_ref[...]
        sg = _sigmoid(x)
        s_ref[...] = x * sg
        d_ref[...] = sg * (1.0 + x * (1.0 - sg))

    return pl.pallas_call(body, name="ada_in", out_shape=(jax.ShapeDtypeStruct(cc.shape, F32),) * 2)(cc)


def _cctx_grad(t0, t1, dsilu):
    def body(a_ref, b_ref, d_ref, o_ref):
        o_ref[...] = (a_ref[...] + b_ref[...]) * d_ref[...]

    return pl.pallas_call(body, name="cctx_grad", out_shape=jax.ShapeDtypeStruct(t0.shape, F32))(t0, t1, dsilu)


def _seg_spec(nct, rows=3):
    return pl.BlockSpec((None, rows, D), lambda i: (jnp.where(i >= nct, 1, 0), 0, 0))


def _prenorm_fwd(x, g_pre, mod3, nct, name):
    t = x.shape[0]

    def body(x_ref, g_ref, mod_ref, h_ref):
        xv = x_ref[...]
        r = lax.rsqrt(jnp.mean(xv * xv, axis=-1, keepdims=True) + EPS)
        y = xv * r * g_ref[...]
        h_ref[...] = (y * (1.0 + mod_ref[1:2, :]) + mod_ref[0:1, :]).astype(BF16)

    return pl.pallas_call(
        body, name=name, grid=(t // TM,),
        in_specs=[pl.BlockSpec((TM, D), lambda i: (i, 0)), pl.BlockSpec((1, D), lambda i: (0, 0)), _seg_spec(nct)],
        out_specs=pl.BlockSpec((TM, D), lambda i: (i, 0)),
        out_shape=jax.ShapeDtypeStruct((t, D), BF16), compiler_params=_cparams(1, 32),
    )(x, g_pre, mod3)


def _prenorm_bwd(dh, x, dxo, g_pre, mod3, nct, name):
    t = x.shape[0]

    def body(dh_ref, x_ref, dxo_ref, g_ref, mod_ref, dx_ref, dsh_ref, dsc_ref, dg_ref):
        i = pl.program_id(0)
        xv, dhv, g = x_ref[...], dh_ref[...], g_ref[...]
        r = lax.rsqrt(jnp.mean(xv * xv, axis=-1, keepdims=True) + EPS)
        xh = xv * r
        dy = dhv * (1.0 + mod_ref[1:2, :])
        dxh = dy * g
        dx_ref[...] = dxo_ref[...] + r * (dxh - xh * jnp.mean(dxh * xh, axis=-1, keepdims=True))

        @pl.when((i == 0) | (i == nct))
        def _():
            dsh_ref[...] = jnp.zeros_like(dsh_ref)
            dsc_ref[...] = jnp.zeros_like(dsc_ref)

        @pl.when(i == 0)
        def _():
            dg_ref[...] = jnp.zeros_like(dg_ref)

        dsh_ref[...] += jnp.sum(dhv, axis=0, keepdims=True)
        dsc_ref[...] += jnp.sum(dhv * (xh * g), axis=0, keepdims=True)
        dg_ref[...] += jnp.sum(dy * xh, axis=0, keepdims=True)

    row = pl.BlockSpec((TM, D), lambda i: (i, 0))
    seg8 = pl.BlockSpec((None, 8, D), lambda i: (jnp.where(i >= nct, 1, 0), 0, 0))
    return pl.pallas_call(
        body, name=name, grid=(t // TM,),
        in_specs=[row, row, row, pl.BlockSpec((1, D), lambda i: (0, 0)), _seg_spec(nct)],
        out_specs=(row, seg8, seg8, pl.BlockSpec((8, D), lambda i: (0, 0))),
        out_shape=(jax.ShapeDtypeStruct((t, D), F32), jax.ShapeDtypeStruct((2, 8, D), F32),
                   jax.ShapeDtypeStruct((2, 8, D), F32), jax.ShapeDtypeStruct((8, D), F32)),
        compiler_params=_cparams(1, 32),
    )(dh, x, dxo, g_pre, mod3)


def _post_fwd(x, out, g_post, mod3, nct, name):
    t = x.shape[0]

    def body(x_ref, o_ref, g_ref, mod_ref, y_ref):
        ov = o_ref[...]
        r = lax.rsqrt(jnp.mean(ov * ov, axis=-1, keepdims=True) + EPS)
        y_ref[...] = x_ref[...] + mod_ref[2:3, :] * (ov * r * g_ref[...])

    row = pl.BlockSpec((TM, D), lambda i: (i, 0))
    return pl.pallas_call(
        body, name=name, grid=(t // TM,),
        in_specs=[row, row, pl.BlockSpec((1, D), lambda i: (0, 0)), _seg_spec(nct)],
        out_specs=row, out_shape=jax.ShapeDtypeStruct((t, D), F32), compiler_params=_cparams(1, 32),
    )(x, out, g_post, mod3)


def _post_bwd(dxo, out, g_post, mod3, nct, name):
    t = out.shape[0]

    def body(dx_ref, o_ref, g_ref, mod_ref, do_ref, dgt_ref, dg_ref):
        i = pl.program_id(0)
        ov, dxv, g = o_ref[...], dx_ref[...], g_ref[...]
        r = lax.rsqrt(jnp.mean(ov * ov, axis=-1, keepdims=True) + EPS)
        nh = ov * r
        dn = dxv * mod_ref[2:3, :]
        dnh = dn * g
        do_ref[...] = (r * (dnh - nh * jnp.mean(dnh * nh, axis=-1, keepdims=True))).astype(BF16)

        @pl.when((i == 0) | (i == nct))
        def _():
            dgt_ref[...] = jnp.zeros_like(dgt_ref)

        @pl.when(i == 0)
        def _():
            dg_ref[...] = jnp.zeros_like(dg_ref)

        dgt_ref[...] += jnp.sum(dxv * (nh * g), axis=0, keepdims=True)
        dg_ref[...] += jnp.sum(dn * nh, axis=0, keepdims=True)

    row = pl.BlockSpec((TM, D), lambda i: (i, 0))
    seg8 = pl.BlockSpec((None, 8, D), lambda i: (jnp.where(i >= nct, 1, 0), 0, 0))
    return pl.pallas_call(
        body, name=name, grid=(t // TM,),
        in_specs=[row, row, pl.BlockSpec((1, D), lambda i: (0, 0)), _seg_spec(nct)],
        out_specs=(row, seg8, pl.BlockSpec((8, D), lambda i: (0, 0))),
        out_shape=(jax.ShapeDtypeStruct((t, D), BF16), jax.ShapeDtypeStruct((2, 8, D), F32),
                   jax.ShapeDtypeStruct((8, D), F32)),
        compiler_params=_cparams(1, 32),
    )(dxo, out, g_post, mod3)


def _loss_grad(y, target, nct, name):
    t = y.shape[0]

    def body(y_ref, t_ref, dy_ref, l_ref):
        i = pl.program_id(0)

        @pl.when(i == 0)
        def _():
            l_ref[...] = jnp.zeros_like(l_ref)

        @pl.when(i < nct)
        def _():
            dy_ref[...] = jnp.zeros_like(dy_ref)

        @pl.when(i >= nct)
        def _():
            err = y_ref[...] - t_ref[...]
            dy_ref[...] = err / D
            l_ref[...] += jnp.sum(jnp.sum(err * err, axis=1, keepdims=True), axis=0, keepdims=True)

    row = pl.BlockSpec((TM, D), lambda i: (i, 0))
    return pl.pallas_call(
        body, name=name, grid=(t // TM,),
        in_specs=[row, pl.BlockSpec((TM, D), lambda i: (jnp.maximum(i - nct, 0), 0))],
        out_specs=(row, pl.BlockSpec((8, LANE), lambda i: (0, 0))),
        out_shape=(jax.ShapeDtypeStruct((t, D), F32), jax.ShapeDtypeStruct((8, LANE), F32)),
        compiler_params=_cparams(1, 32),
    )(y, target)


def _pcol(name, width):
    assert OFF[name] % width == 0
    blk = OFF[name] // width
    return pl.BlockSpec((TM, width), lambda i: (i, blk))


def _shift_rows(u, prev_row, next_row):
    n = u.shape[0]
    row = lax.broadcasted_iota(jnp.int32, u.shape, 0)
    prev = jnp.where(row == 0, prev_row, pltpu.roll(u, 1, 0))
    nxt = jnp.where(row == n - 1, next_row, pltpu.roll(u, n - 1, 0))
    return prev, nxt


def _halo_specs(width, nt, blk=0, rows=8):
    per = TM // rows
    prev = pl.BlockSpec((rows, width), lambda i: (jnp.maximum(i * per - 1, 0), blk))
    nxt = pl.BlockSpec((rows, width), lambda i: (jnp.minimum((i + 1) * per, nt * per - 1), blk))
    return prev, nxt


def _conv_fwd(p, conv_w8, nct, name):
    t = p.shape[0]
    nt = t // TM

    def body(ab_ref, ac_ref, ax_ref, az_ref, acp_ref, axp_ref, acn_ref, axn_ref, w_ref, cv_ref, ya_ref):
        i = pl.program_id(0)
        def f(ref, rows=slice(None)):
            return ref[rows, :].astype(F32)

        u = f(ac_ref) * f(ax_ref)
        mp = jnp.where((i == 0) | (i == nct), 0.0, 1.0)
        mn = jnp.where((i == nct - 1) | (i == nt - 1), 0.0, 1.0)
        last, first = slice(P_HALO - 1, P_HALO), slice(0, 1)
        prev, nxt = _shift_rows(u, f(acp_ref, last) * f(axp_ref, last) * mp, f(acn_ref, first) * f(axn_ref, first) * mn)
        cv = w_ref[0:1, :] * prev + w_ref[1:2, :] * u + w_ref[2:3, :] * nxt
        az = f(az_ref)
        cv_ref[...] = cv.astype(BF16)
        ya_ref[...] = (f(ab_ref) * cv * (az * _sigmoid(az))).astype(BF16)

    acp, acn = _halo_specs(D, nt, OFF["a_c"] // D, P_HALO)
    axp, axn = _halo_specs(D, nt, OFF["a_x"] // D, P_HALO)
    row = pl.BlockSpec((TM, D), lambda i: (i, 0))
    return pl.pallas_call(
        body, name=name, grid=(nt,),
        in_specs=[_pcol("a_b", D), _pcol("a_c", D), _pcol("a_x", D), _pcol("a_z", D), acp, axp, acn, axn,
                  pl.BlockSpec((8, D), lambda i: (0, 0))],
        out_specs=(row, row),
        out_shape=(jax.ShapeDtypeStruct((t, D), BF16), jax.ShapeDtypeStruct((t, D), BF16)),
        compiler_params=_cparams(1, 40),
    )(p, p, p, p, p, p, p, p, conv_w8)


def _dp_spec(key):
    seg, width = DP_BLOCKS[key]
    assert OFF[seg] % width == 0
    blk = OFF[seg] // width
    return pl.BlockSpec((TM, width), lambda i: (i, blk))


def _conv_bwd_a(dya, p, cv, dp, name):
    t = p.shape[0]

    def body(dy_ref, ab_ref, az_ref, cv_ref, _, dcv_ref, dp_ref):
        dy, ab = dy_ref[...].astype(F32), ab_ref[...].astype(F32)
        az, c = az_ref[...].astype(F32), cv_ref[...].astype(F32)
        sg = _sigmoid(az)
        sz = az * sg
        dcv_ref[...] = dy * ab * sz
        dp_ref[:, 0:D] = (dy * c * sz).astype(BF16)
        dp_ref[:, D:2 * D] = (dy * ab * c * (sg * (1.0 + az * (1.0 - sg)))).astype(BF16)

    row = pl.BlockSpec((TM, D), lambda i: (i, 0))
    return pl.pallas_call(
        body, name=name, grid=(t // TM,),
        in_specs=[row, _pcol("a_b", D), _pcol("a_z", D), row, _HBM], out_specs=(row, _dp_spec("conv_a")),
        out_shape=(jax.ShapeDtypeStruct((t, D), F32), jax.ShapeDtypeStruct(dp.shape, dp.dtype)),
        input_output_aliases={4: 1}, compiler_params=_cparams(1, 40),
    )(dya, p, p, cv, dp)


def _conv_bwd_b(dcv, p, conv_w8, nct, dp, name):
    t = p.shape[0]
    nt = t // TM

    def body(dcv_ref, hp_ref, hn_ref, ac_ref, ax_ref, w_ref, _, dp_ref, dw_ref):
        i = pl.program_id(0)
        d, ac, ax = dcv_ref[...], ac_ref[...].astype(F32), ax_ref[...].astype(F32)
        u = ac * ax
        mp = jnp.where((i == 0) | (i == nct), 0.0, 1.0)
        mn = jnp.where((i == nct - 1) | (i == nt - 1), 0.0, 1.0)
        dprev, dnxt = _shift_rows(d, hp_ref[7:8, :] * mp, hn_ref[0:1, :] * mn)
        du = w_ref[0:1, :] * dnxt + w_ref[1:2, :] * d + w_ref[2:3, :] * dprev
        dp_ref[:, 0:D] = (du * ax).astype(BF16)
        dp_ref[:, D:2 * D] = (du * ac).astype(BF16)

        @pl.when(i == 0)
        def _():
            dw_ref[...] = jnp.zeros_like(dw_ref)

        dw0 = jnp.sum(u * dnxt, axis=0, keepdims=True)
        dw1 = jnp.sum(u * d, axis=0, keepdims=True)
        dw2 = jnp.sum(u * dprev, axis=0, keepdims=True)
        r8 = lax.broadcasted_iota(jnp.int32, (8, D), 0)
        dw_ref[...] += jnp.where(r8 == 0, dw0, jnp.where(r8 == 1, dw1, jnp.where(r8 == 2, dw2, 0.0)))

    hp, hn = _halo_specs(D, nt)
    row = pl.BlockSpec((TM, D), lambda i: (i, 0))
    return pl.pallas_call(
        body, name=name, grid=(nt,),
        in_specs=[row, hp, hn, _pcol("a_c", D), _pcol("a_x", D), pl.BlockSpec((8, D), lambda i: (0, 0)), _HBM],
        out_specs=(_dp_spec("conv_b"), pl.BlockSpec((8, D), lambda i: (0, 0))),
        out_shape=(jax.ShapeDtypeStruct(dp.shape, dp.dtype), jax.ShapeDtypeStruct((8, D), F32)),
        input_output_aliases={6: 0}, compiler_params=_cparams(1, 40),
    )(dcv, dcv, dcv, p, p, conv_w8, dp)


def _rot_half(x):
    lane = lax.broadcasted_iota(jnp.int32, x.shape, 1)
    return jnp.where((lane % 64) < 32, pltpu.roll(x, 96, 1), pltpu.roll(x, 32, 1))


def _qk_prep_fwd(p, qg, kg, cos_t, sin_t, name):
    t = p.shape[0]

    def body(q_ref, k_ref, qg_ref, kg_ref, c_ref, s_ref, qo_ref, ko_ref):
        c, s = c_ref[...], s_ref[...]

        def one(xv, g, scale):
            y = xv * lax.rsqrt(jnp.mean(xv * xv, axis=-1, keepdims=True) + EPS) * g
            return ((y * c + _rot_half(y) * s) * scale).astype(BF16)

        for h in range(NH):
            qo_ref[:, h * HD:(h + 1) * HD] = one(q_ref[:, h * HD:(h + 1) * HD].astype(F32), qg_ref[...], Q_FOLD)
        for h in range(NKV):
            ko_ref[:, h * HD:(h + 1) * HD] = one(k_ref[:, h * HD:(h + 1) * HD].astype(F32), kg_ref[...], 1.0)

    vec = pl.BlockSpec((1, HD), lambda i: (0, 0))
    tab = pl.BlockSpec((TM, HD), lambda i: (i, 0))
    return pl.pallas_call(
        body, name=name, grid=(t // TM,),
        in_specs=[_pcol("q", NH * HD), _pcol("k", NKV * HD), vec, vec, tab, tab],
        out_specs=(pl.BlockSpec((TM, NH * HD), lambda i: (i, 0)), pl.BlockSpec((TM, NKV * HD), lambda i: (i, 0))),
        out_shape=(jax.ShapeDtypeStruct((t, NH * HD), BF16), jax.ShapeDtypeStruct((t, NKV * HD), BF16)),
        compiler_params=_cparams(1, 32),
    )(p, p, qg, kg, cos_t, sin_t)


def _qk_prep_bwd(dqr, dkr, p, qg, kg, cos_t, sin_t, dp, name):
    t = p.shape[0]

    def body(dq_ref, dk_ref, q_ref, k_ref, qg_ref, kg_ref, c_ref, s_ref, _, dqo_ref, dko_ref, dqg_ref, dkg_ref):
        i = pl.program_id(0)
        c, s = c_ref[...], s_ref[...]

        @pl.when(i == 0)
        def _():
            dqg_ref[...] = jnp.zeros_like(dqg_ref)
            dkg_ref[...] = jnp.zeros_like(dkg_ref)

        def one(dyr, xv, g):
            dy = dyr * c + _rot_half(dyr * s)
            r = lax.rsqrt(jnp.mean(xv * xv, axis=-1, keepdims=True) + EPS)
            xh = xv * r
            dxh = dy * g
            dx = r * (dxh - xh * jnp.mean(dxh * xh, axis=-1, keepdims=True))
            return dx.astype(BF16), jnp.sum(dy * xh, axis=0, keepdims=True)

        for h in range(NH):
            sl = slice(h * HD, (h + 1) * HD)
            dx, dg = one(dq_ref[:, sl] * ATTN_SCALE, q_ref[:, sl].astype(F32), qg_ref[...])
            dqo_ref[:, sl] = dx
            dqg_ref[...] += dg
        for h in range(NKV):
            sl = slice(h * HD, (h + 1) * HD)
            dx, dg = one(dk_ref[:, sl] * (ATTN_SCALE / Q_FOLD), k_ref[:, sl].astype(F32), kg_ref[...])
            dko_ref[:, sl] = dx
            dkg_ref[...] += dg

    vec = pl.BlockSpec((1, HD), lambda i: (0, 0))
    tab = pl.BlockSpec((TM, HD), lambda i: (i, 0))
    acc = pl.BlockSpec((8, HD), lambda i: (0, 0))
    qrow = pl.BlockSpec((TM, NH * HD), lambda i: (i, 0))
    krow = pl.BlockSpec((TM, NKV * HD), lambda i: (i, 0))
    return pl.pallas_call(
        body, name=name, grid=(t // TM,),
        in_specs=[qrow, krow, _pcol("q", NH * HD), _pcol("k", NKV * HD), vec, vec, tab, tab, _HBM],
        out_specs=(_dp_spec("q"), krow, acc, acc),
        out_shape=(jax.ShapeDtypeStruct(dp.shape, dp.dtype), jax.ShapeDtypeStruct((t, NKV * HD), BF16),
                   jax.ShapeDtypeStruct((8, HD), F32), jax.ShapeDtypeStruct((8, HD), F32)),
        input_output_aliases={8: 0}, compiler_params=_cparams(1, 32),
    )(dqr, dkr, p, p, qg, kg, cos_t, sin_t, dp)


def _key_chunks(n):
    c = max(c for c in range(LANE, min(n, ATTN_KEY_CHUNK) + 1, LANE) if n % c == 0)
    return [(lo, lo + c) for lo in range(0, n, c)]


def _attn_fwd(qr, kr, p, nct, name, rider=None):
    t = qr.shape[0]
    nt = t // TM
    ctx = nct * TM
    vblk = OFF["v"] // HD
    hps = ATTN_HEADS_PER_STEP
    nhp, per_kv = NH // hps, GROUP // hps

    def body(q_ref, k_ref, v_ref, o_ref, lse_ref):
        def tile(nkeys):
            for j in range(hps):
                sl = slice(j * HD, (j + 1) * HD)
                q = q_ref[:, sl]
                m = l = acc = None
                for lo, hi in _key_chunks(nkeys):
                    s = lax.dot_general(q, k_ref[lo:hi, :], _NT, preferred_element_type=F32)
                    mc = jnp.max(s, axis=-1, keepdims=True)
                    m_new = mc if m is None else jnp.maximum(m, mc)
                    e = jnp.exp2(s - m_new)
                    lc = jnp.sum(e, axis=-1, keepdims=True)
                    pv = jnp.dot(e.astype(BF16), v_ref[lo:hi, :].astype(BF16), preferred_element_type=F32)
                    if m is None:
                        l, acc = lc, pv
                    else:
                        alpha = jnp.exp2(m - m_new)
                        l, acc = l * alpha + lc, acc * alpha + pv
                    m = m_new
                o_ref[:, sl] = (acc / l).astype(BF16)
                lse_ref[:, j:j + 1] = m + jnp.log2(l)

        pl.when(pl.program_id(1) < nct)(lambda: tile(ctx))
        pl.when(pl.program_id(1) >= nct)(lambda: tile(t))

    def at(h, i):
        return lambda: (pl.program_id(0) == h) & (pl.program_id(1) == i)

    rn = 0 if rider is None else rider.n
    qspec = pl.BlockSpec((TM, hps * HD), lambda h, i: (i, h))
    return pl.pallas_call(
        _with_rider(body, 3, 2, rider, at(0, 0), at(nhp * 7 // 8, 0), at(nhp - 1, nt - 1)),
        name=name, grid=(nhp, nt),
        in_specs=[qspec, pl.BlockSpec((t, HD), lambda h, i: (0, h // per_kv)),
                  pl.BlockSpec((t, HD), lambda h, i: (0, vblk + h // per_kv))] + [_HBM] * rn,
        out_specs=(qspec, pl.BlockSpec((None, TM, hps), lambda h, i: (h, i, 0))) + (_HBM,) * rn,
        out_shape=(jax.ShapeDtypeStruct((t, NH * HD), BF16), jax.ShapeDtypeStruct((nhp, t, hps), F32))
        + (() if rider is None else tuple(rider.out_shape)),
        scratch_shapes=[] if rider is None else rider.scratch,
        compiler_params=_cparams(2, 48),
    )(qr, kr, p, *(() if rider is None else rider.xs))


def _attn_bwd(qr, kr, p, o, lse, do, nct, name, rider=None):
    t = qr.shape[0]
    nt = t // TM
    ctx = nct * TM
    vblk = OFF["v"] // HD
    hps = ATTN_HEADS_PER_STEP

    def body(q_ref, k_ref, v_ref, o_ref, lse_ref, do_ref, dq_ref, dk_ref, dv_ref):
        g, i = pl.program_id(1), pl.program_id(2)

        @pl.when((g == 0) & (i == 0))
        def _():
            dk_ref[...] = jnp.zeros_like(dk_ref)
            dv_ref[...] = jnp.zeros_like(dv_ref)

        def tile(nkeys):
            heads = []
            for j in range(hps):
                sl = slice(j * HD, (j + 1) * HD)
                dob = do_ref[:, sl]
                drow = jnp.sum(dob.astype(F32) * o_ref[:, sl].astype(F32), axis=-1, keepdims=True)
                heads.append((sl, q_ref[:, sl], dob, drow, lse_ref[:, j:j + 1]))
            dq = [None] * hps
            for lo, hi in _key_chunks(nkeys):
                k = k_ref[lo:hi, :]
                vb = v_ref[lo:hi, :].astype(BF16)
                dk_c = dv_c = None
                for j, (sl, q, dob, drow, lse_j) in enumerate(heads):
                    s = lax.dot_general(q, k, _NT, preferred_element_type=F32)
                    pr = jnp.exp2(s - lse_j)
                    dp = lax.dot_general(dob, vb, _NT, preferred_element_type=F32)
                    ds = (pr * (dp - drow)).astype(BF16)
                    dq_c = jnp.dot(ds, k, preferred_element_type=F32)
                    dq[j] = dq_c if dq[j] is None else dq[j] + dq_c
                    dk_j = lax.dot_general(ds, q, _TN, preferred_element_type=F32)
                    dv_j = lax.dot_general(pr.astype(BF16), dob, _TN, preferred_element_type=F32)
                    dk_c = dk_j if dk_c is None else dk_c + dk_j
                    dv_c = dv_j if dv_c is None else dv_c + dv_j
                dk_ref[lo:hi, :] += dk_c
                dv_ref[lo:hi, :] += dv_c
            for j, (sl, *_) in enumerate(heads):
                dq_ref[:, sl] = dq[j]

        pl.when(i < nct)(lambda: tile(ctx))
        pl.when(i >= nct)(lambda: tile(t))

    def at(kv, g, i):
        return lambda: (pl.program_id(0) == kv) & (pl.program_id(1) == g) & (pl.program_id(2) == i)

    rn = 0 if rider is None else rider.n
    per_kv = GROUP // hps
    qspec = pl.BlockSpec((TM, hps * HD), lambda kv, g, i: (i, kv * per_kv + g))
    kvspec = pl.BlockSpec((t, HD), lambda kv, g, i: (0, kv))
    lspec = pl.BlockSpec((None, TM, hps), lambda kv, g, i: (kv * per_kv + g, i, 0))
    return pl.pallas_call(
        _with_rider(body, 6, 3, rider, at(0, 0, 0), at(NKV - 1, 0, 0), at(NKV - 1, per_kv - 1, nt - 1)),
        name=name, grid=(NKV, per_kv, nt),
        in_specs=[qspec, kvspec, pl.BlockSpec((t, HD), lambda kv, g, i: (0, vblk + kv)), qspec, lspec, qspec]
        + [_HBM] * rn,
        out_specs=(qspec, kvspec, kvspec) + (_HBM,) * rn,
        out_shape=(jax.ShapeDtypeStruct((t, NH * HD), F32), jax.ShapeDtypeStruct((t, NKV * HD), F32),
                   jax.ShapeDtypeStruct((t, NKV * HD), F32)) + (() if rider is None else tuple(rider.out_shape)),
        scratch_shapes=[] if rider is None else rider.scratch,
        compiler_params=_cparams(3, 48),
    )(qr, kr, p, o, lse, do, *(() if rider is None else rider.xs))


def _decay_fwd(p, wd, bd, name):
    t = p.shape[0]

    def body(r_ref, w_ref, b_ref, z_ref, la_ref):
        z = jnp.dot(r_ref[...].astype(BF16), w_ref[...].astype(BF16), preferred_element_type=F32) + b_ref[...]
        z_ref[...] = z
        la_ref[...] = (jnp.minimum(z, 0.0) - jnp.log(1.0 + jnp.exp(-jnp.abs(z)))) / GLA_TAU

    row = pl.BlockSpec((TM, D), lambda i: (i, 0))
    return pl.pallas_call(
        body, name=name, grid=(t // TM,),
        in_specs=[_pcol("r", R_PAD), pl.BlockSpec((R_PAD, D), lambda i: (0, 0)), pl.BlockSpec((1, D), lambda i: (0, 0))],
        out_specs=(row, row),
        out_shape=(jax.ShapeDtypeStruct((t, D), F32), jax.ShapeDtypeStruct((t, D), F32)),
        compiler_params=_cparams(1, 32),
    )(p, wd, bd)


def _chunk_order(s, ncc, nc, rev):
    if not rev:
        return s
    return jnp.where(s < ncc, ncc - 1 - s, nc - 1 - (s - ncc))


def _gla_chains(dirs):
    return [(rev, d, h) + tuple(refs) for d, (rev, *refs) in enumerate(dirs) for h in range(GH)]


def _hk(h):
    return slice(h * GDK, (h + 1) * GDK)


def _hv(h):
    return slice(h * GDV, (h + 1) * GDV)


def _gla_factors(qs, ks, las, revs):
    r = lax.broadcasted_iota(jnp.int32, (CH, CH), 0)
    c = lax.broadcasted_iota(jnp.int32, (CH, CH), 1)
    keeps = [(c >= r) if rev else (c <= r) for rev in revs]
    bcs = [jnp.dot(keep.astype(F32), la, preferred_element_type=F32, precision=HIGHEST) for keep, la in zip(keeps, las)]
    bls = [jnp.sum(la, axis=0, keepdims=True) for la in las]
    qs, ks = [q.astype(F32) for q in qs], [k.astype(F32) for k in ks]
    qts = [q * GLA_SCALE * jnp.exp(bc) for q, bc in zip(qs, bcs)]
    kts = [k * jnp.exp(-bc) for k, bc in zip(ks, bcs)]
    khs = [k * jnp.exp(bl - bc) for k, bl, bc in zip(ks, bls, bcs)]
    gls = [jnp.exp(bl) for bl in bls]
    return qts, kts, gls, khs, keeps, bcs


_NT = (((1,), (1,)), ((), ()))
_TN = (((0,), (0,)), ((), ()))


def _gla_specs(ncc, nc, rev, backward):
    def idx(s):
        return _chunk_order((nc - 1 - s) if backward else s, ncc, nc, rev)

    wk, wv = GH * GDK, GH * GDV
    qb, kb, vb = OFF["gq"] // wk, OFF["gk"] // wk, OFF["gv"] // wv
    lab = 1 if rev else 0
    q = pl.BlockSpec((CH, wk), lambda s: (idx(s), qb))
    k = pl.BlockSpec((CH, wk), lambda s: (idx(s), kb))
    v = pl.BlockSpec((CH, wv), lambda s: (idx(s), vb))
    la = pl.BlockSpec((CH, wk), lambda s: (idx(s), lab))
    o = pl.BlockSpec((CH, wv), lambda s: (idx(s), 0))
    dk = pl.BlockSpec((CH, wk), lambda s: (idx(s), 0))
    st = pl.BlockSpec((None, GH, GDV, GDK), lambda s: (idx(s), 0, 0, 0))
    return q, k, v, la, o, dk, st


def _gla_fwd(p, la, ncc, name):
    t = p.shape[0]
    nc = t // CH
    specs = [_gla_specs(ncc, nc, rev, False) for rev in (False, True)]

    def body(qf, kf, vf, laf, qb_, kb_, vb_, lab, of, stf, ob, stb, s_scr):
        @pl.when(pl.program_id(0) == 0)
        def _():
            s_scr[...] = jnp.zeros_like(s_scr)

        ch = _gla_chains(((False, qf, kf, vf, laf, of, stf), (True, qb_, kb_, vb_, lab, ob, stb)))
        qts, kts, gls, khs, keeps, _ = _gla_factors([c[3][:, _hk(c[2])] for c in ch], [c[4][:, _hk(c[2])] for c in ch],
                                                    [c[6][:, _hk(c[2])] for c in ch], [c[0] for c in ch])
        sts = [s_scr[c[1], c[2]] for c in ch]
        for c, st in zip(ch, sts):
            c[8][c[2]] = st
        vbs = [c[5][:, _hv(c[2])].astype(BF16) for c in ch]
        qbs = [qt.astype(BF16) for qt in qts]
        a_s = [jnp.where(keep, lax.dot_general(qb, kt.astype(BF16), _NT, preferred_element_type=F32), 0.0)
               for keep, qb, kt in zip(keeps, qbs, kts)]
        inter = [lax.dot_general(qb, st.astype(BF16), _NT, preferred_element_type=F32) for qb, st in zip(qbs, sts)]
        intra = [jnp.dot(a.astype(BF16), vb, preferred_element_type=F32) for a, vb in zip(a_s, vbs)]
        for c, x, y in zip(ch, inter, intra):
            c[7][:, _hv(c[2])] = (x + y).astype(BF16)
        upd = [lax.dot_general(vb, kh.astype(BF16), _TN, preferred_element_type=F32) for vb, kh in zip(vbs, khs)]
        for c, st, gl, u in zip(ch, sts, gls, upd):
            s_scr[c[1], c[2]] = st * gl + u

    o_shape = jax.ShapeDtypeStruct((t, GH * GDV), BF16)
    st_shape = jax.ShapeDtypeStruct((nc, GH, GDV, GDK), F32)
    return pl.pallas_call(
        body, name=name, grid=(nc,),
        in_specs=[sp for s_ in specs for sp in s_[:4]],
        out_specs=tuple(sp for s_ in specs for sp in (s_[4], s_[6])),
        out_shape=(o_shape, st_shape, o_shape, st_shape),
        scratch_shapes=[pltpu.VMEM((2, GH, GDV, GDK), F32)], compiler_params=_cparams(1, 32),
    )(p, p, p, la, p, p, p, la)


def _gla_bwd(p, la, do, stf, stb, ncc, name):
    t = p.shape[0]
    nc = t // CH
    specs = [_gla_specs(ncc, nc, rev, True) for rev in (False, True)]

    def mm(xs, ys, dims=None):
        if dims is None:
            return [jnp.dot(x, y, preferred_element_type=F32) for x, y in zip(xs, ys)]
        return [lax.dot_general(x, y, dims, preferred_element_type=F32) for x, y in zip(xs, ys)]

    def body(*refs):
        ins_f, ins_b, outs_f, outs_b, ds_scr = refs[0:6], refs[6:12], refs[12:16], refs[16:20], refs[20]

        @pl.when(pl.program_id(0) == 0)
        def _():
            ds_scr[...] = jnp.zeros_like(ds_scr)

        ch = _gla_chains(((False, *ins_f, *outs_f), (True, *ins_b, *outs_b)))
        revs = [c[0] for c in ch]
        qts, kts, gls, khs, keeps, bcs = _gla_factors([c[3][:, _hk(c[2])] for c in ch], [c[4][:, _hk(c[2])] for c in ch],
                                                      [c[6][:, _hk(c[2])] for c in ch], revs)
        stvs = [c[8][c[2]].astype(BF16) for c in ch]
        dsns = [ds_scr[c[1], c[2]] for c in ch]
        dsbs = [x.astype(BF16) for x in dsns]
        vbs = [c[5][:, _hv(c[2])].astype(BF16) for c in ch]
        dobs = [c[7][:, _hv(c[2])].astype(BF16) for c in ch]
        qbs, kbs = [x.astype(BF16) for x in qts], [x.astype(BF16) for x in kts]
        a_s = [jnp.where(keep, x, 0.0).astype(BF16) for keep, x in zip(keeps, mm(qbs, kbs, _NT))]
        das = [jnp.where(keep, x, 0.0).astype(BF16) for keep, x in zip(keeps, mm(dobs, vbs, _NT))]
        dqts = [x + y for x, y in zip(mm(dobs, stvs), mm(das, kbs))]
        dkhs = mm(vbs, dsbs)
        dkts = [x + dkh * gl for x, dkh, gl in zip(mm(das, qbs, _TN), dkhs, gls)]
        for c, x, y in zip(ch, mm(a_s, dobs, _TN), mm([kh.astype(BF16) for kh in khs], dsbs, _NT)):
            c[11][:, _hv(c[2])] = x + y
        for c, x, dsn, gl in zip(ch, mm(dobs, qbs, _TN), dsns, gls):
            ds_scr[c[1], c[2]] = x + dsn * gl
        dgls = [jnp.sum(c[8][c[2]] * dsn, axis=0, keepdims=True) + jnp.sum(dkh * kt, axis=0, keepdims=True)
                for c, dsn, dkh, kt in zip(ch, dsns, dkhs, kts)]
        row = lax.broadcasted_iota(jnp.int32, (CH, GDK), 0)
        dbcs = [dqt * qt - dkt * kt + jnp.where(row == (0 if rev else CH - 1), dgl * gl, 0.0)
                for rev, dqt, qt, dkt, kt, dgl, gl in zip(revs, dqts, qts, dkts, kts, dgls, gls)]
        r = lax.broadcasted_iota(jnp.int32, (CH, CH), 0)
        c_ = lax.broadcasted_iota(jnp.int32, (CH, CH), 1)
        dlas = [jnp.dot(((c_ <= r) if rev else (c_ >= r)).astype(F32), dbc, preferred_element_type=F32, precision=HIGHEST)
                for rev, dbc in zip(revs, dbcs)]
        for c, dla, dqt, dkt, bc in zip(ch, dlas, dqts, dkts, bcs):
            c[12][:, _hk(c[2])] = dla
            c[9][:, _hk(c[2])] = dqt * (GLA_SCALE * jnp.exp(bc))
            c[10][:, _hk(c[2])] = dkt * jnp.exp(-bc)

    k_shape = jax.ShapeDtypeStruct((t, GH * GDK), F32)
    v_shape = jax.ShapeDtypeStruct((t, GH * GDV), F32)
    res = pl.pallas_call(
        body, name=name, grid=(nc,),
        in_specs=[sp for q_s, k_s, v_s, la_s, o_s, _, st_s in specs for sp in (q_s, k_s, v_s, la_s, o_s, st_s)],
        out_specs=tuple(sp for _, _, _, _, o_s, dk_s, _ in specs for sp in (dk_s, dk_s, o_s, dk_s)),
        out_shape=(k_shape, k_shape, v_shape, k_shape) * 2,
        scratch_shapes=[pltpu.VMEM((2, GH, GDV, GDK), F32)], compiler_params=_cparams(1, 32),
    )(p, p, p, la, do, stf, p, p, p, la, do, stb)
    return res[:4], res[4:]


def _gla_merge_bwd(gf, gb, z, p, wd, dp, name):
    t = p.shape[0]
    w2 = GH * GDK

    def body(dqf, dkf, dvf, dlf, dqb, dkb, dvb, dlb, z_ref, r_ref, w_ref, _, dp_ref, dr_ref, db_ref, dw_ref):
        i = pl.program_id(0)
        dp_ref[:, 0:D] = (dvf[...] + dvb[...]).astype(BF16)
        dp_ref[:, D:D + w2] = (dqf[...] + dqb[...]).astype(BF16)
        dp_ref[:, D + w2:D + 2 * w2] = (dkf[...] + dkb[...]).astype(BF16)
        zv = z_ref[...]
        dz = jnp.concatenate([dlf[...], dlb[...]], axis=1) * (_sigmoid(-zv) / GLA_TAU)
        dzb = dz.astype(BF16)
        dr_ref[...] = lax.dot_general(dzb, w_ref[...].astype(BF16), _NT, preferred_element_type=F32).astype(BF16)

        @pl.when(i == 0)
        def _():
            db_ref[...] = jnp.zeros_like(db_ref)
            dw_ref[...] = jnp.zeros_like(dw_ref)

        db_ref[...] += jnp.sum(dz, axis=0, keepdims=True)
        dw_ref[...] += lax.dot_general(r_ref[...].astype(BF16), dzb, _TN, preferred_element_type=F32)

    half = pl.BlockSpec((TM, w2), lambda i: (i, 0))
    row = pl.BlockSpec((TM, D), lambda i: (i, 0))
    wspec = pl.BlockSpec((R_PAD, D), lambda i: (0, 0))
    return pl.pallas_call(
        body, name=name, grid=(t // TM,),
        in_specs=[half, half, row, half, half, half, row, half, row, _pcol("r", R_PAD), wspec, _HBM],
        out_specs=(_dp_spec("gla"), pl.BlockSpec((TM, R_PAD), lambda i: (i, 0)),
                   pl.BlockSpec((8, D), lambda i: (0, 0)), wspec),
        out_shape=(jax.ShapeDtypeStruct(dp.shape, dp.dtype), jax.ShapeDtypeStruct((t, R_PAD), BF16),
                   jax.ShapeDtypeStruct((8, D), F32), jax.ShapeDtypeStruct((R_PAD, D), F32)),
        input_output_aliases={11: 0}, compiler_params=_cparams(1, 40),
    )(*gf, *gb, z, p, wd, dp)


def _dp_tail(dk, dv, dr, dp, name):
    t = dk.shape[0]
    wk = NKV * HD

    def body(dk_ref, dv_ref, dr_ref, _, dp_ref):
        dp_ref[:, 0:wk] = dk_ref[...]
        dp_ref[:, wk:2 * wk] = dv_ref[...].astype(BF16)
        dp_ref[:, 2 * wk:2 * wk + R_PAD] = dr_ref[...]
        dp_ref[:, 2 * wk + R_PAD:] = jnp.zeros((TM, DP_BLOCKS["tail"][1] - 2 * wk - R_PAD), BF16)

    kv = pl.BlockSpec((TM, wk), lambda i: (i, 0))
    return pl.pallas_call(
        body, name=name, grid=(t // TM,),
        in_specs=[kv, kv, pl.BlockSpec((TM, R_PAD), lambda i: (i, 0)), _HBM], out_specs=_dp_spec("tail"),
        out_shape=jax.ShapeDtypeStruct(dp.shape, dp.dtype), input_output_aliases={3: 0},
        compiler_params=_cparams(1, 32),
    )(dk, dv, dr, dp)


def _branch_fwd(att, of, ob, p, gla_g, name):
    t = p.shape[0]

    def body(att_ref, of_ref, ob_ref, za_ref, zg_ref, g_ref, yb_ref, yc_ref):
        za = za_ref[...].astype(F32)
        yb_ref[...] = (att_ref[...].astype(F32) * (za * _sigmoid(za))).astype(BF16)
        for h in range(GH):
            sl = slice(h * GDV, (h + 1) * GDV)
            o = of_ref[:, sl].astype(F32) + ob_ref[:, sl].astype(F32)
            n = o * lax.rsqrt(jnp.mean(o * o, axis=-1, keepdims=True) + EPS) * g_ref[...]
            zh = zg_ref[:, sl].astype(F32)
            yc_ref[:, sl] = (n * (zh * _sigmoid(zh))).astype(BF16)

    row = pl.BlockSpec((TM, D), lambda i: (i, 0))
    return pl.pallas_call(
        body, name=name, grid=(t // TM,),
        in_specs=[row, row, row, _pcol("z_attn", D), _pcol("zg", D), pl.BlockSpec((1, GDV), lambda i: (0, 0))],
        out_specs=(row, row),
        out_shape=(jax.ShapeDtypeStruct((t, D), BF16), jax.ShapeDtypeStruct((t, D), BF16)),
        compiler_params=_cparams(1, 40),
    )(att, of, ob, p, p, gla_g)


def _branch_bwd(dyb, dyc, att, of, ob, p, gla_g, dp, name):
    t = p.shape[0]

    def body(dyb_ref, dyc_ref, att_ref, of_ref, ob_ref, za_ref, zg_ref, g_ref, _, datt_ref, do_ref, dp_ref, dg_ref):
        i = pl.program_id(0)

        @pl.when(i == 0)
        def _():
            dg_ref[...] = jnp.zeros_like(dg_ref)

        za, dyb = za_ref[...].astype(F32), dyb_ref[...].astype(F32)
        sa = _sigmoid(za)
        datt_ref[...] = (dyb * (za * sa)).astype(BF16)
        dp_ref[:, 0:D] = (dyb * att_ref[...].astype(F32) * (sa * (1.0 + za * (1.0 - sa)))).astype(BF16)
        g = g_ref[...]
        for h in range(GH):
            sl = slice(h * GDV, (h + 1) * GDV)
            o = of_ref[:, sl].astype(F32) + ob_ref[:, sl].astype(F32)
            r = lax.rsqrt(jnp.mean(o * o, axis=-1, keepdims=True) + EPS)
            oh = o * r
            zh, dyc = zg_ref[:, sl].astype(F32), dyc_ref[:, sl].astype(F32)
            sg = _sigmoid(zh)
            dn = dyc * (zh * sg)
            dp_ref[:, D + h * GDV:D + (h + 1) * GDV] = (dyc * (oh * g) * (sg * (1.0 + zh * (1.0 - sg)))).astype(BF16)
            doh = dn * g
            do_ref[:, sl] = (r * (doh - oh * jnp.mean(doh * oh, axis=-1, keepdims=True))).astype(BF16)
            dg_ref[...] += jnp.sum(dn * oh, axis=0, keepdims=True)

    row = pl.BlockSpec((TM, D), lambda i: (i, 0))
    return pl.pallas_call(
        body, name=name, grid=(t // TM,),
        in_specs=[row, row, row, row, row, _pcol("z_attn", D), _pcol("zg", D), pl.BlockSpec((1, GDV), lambda i: (0, 0)),
                  _HBM],
        out_specs=(row, row, _dp_spec("branch"), pl.BlockSpec((8, GDV), lambda i: (0, 0))),
        out_shape=(jax.ShapeDtypeStruct((t, D), BF16), jax.ShapeDtypeStruct((t, D), BF16),
                   jax.ShapeDtypeStruct(dp.shape, dp.dtype), jax.ShapeDtypeStruct((8, GDV), F32)),
        input_output_aliases={8: 2}, compiler_params=_cparams(1, 48),
    )(dyb, dyc, att, of, ob, p, p, gla_g, dp)


def _merge_fwd(bra, brb, brc, p, b_gate, name):
    t = p.shape[0]
    mgb = OFF["mg"] // D

    def body(a_ref, b_ref, c_ref, ga_ref, gb_ref, gc_ref, bg_ref, m_ref):
        m_ref[...] = (_sigmoid(ga_ref[...].astype(F32) + bg_ref[:, 0:D]) * a_ref[...].astype(F32)
                      + _sigmoid(gb_ref[...].astype(F32) + bg_ref[:, D:2 * D]) * b_ref[...].astype(F32)
                      + _sigmoid(gc_ref[...].astype(F32) + bg_ref[:, 2 * D:3 * D]) * c_ref[...].astype(F32)).astype(BF16)

    row = pl.BlockSpec((TM, D), lambda i: (i, 0))
    gates = [pl.BlockSpec((TM, D), functools.partial(lambda i, b: (i, b), b=mgb + j)) for j in range(3)]
    return pl.pallas_call(
        body, name=name, grid=(t // TM,),
        in_specs=[row, row, row, *gates, pl.BlockSpec((1, 3 * D), lambda i: (0, 0))],
        out_specs=row, out_shape=jax.ShapeDtypeStruct((t, D), BF16), compiler_params=_cparams(1, 40),
    )(bra, brb, brc, p, p, p, b_gate)


def _merge_bwd(dm, bra, brb, brc, p, b_gate, name):
    t = p.shape[0]
    mgb = OFF["mg"] // D

    def body(dm_ref, a_ref, b_ref, c_ref, ga_ref, gb_ref, gc_ref, bg_ref, da_ref, db_ref, dc_ref, dmg_ref, dbg_ref):
        i = pl.program_id(0)

        @pl.when(i == 0)
        def _():
            dbg_ref[...] = jnp.zeros_like(dbg_ref)

        dm = dm_ref[...].astype(F32)
        for j, (br_ref, g_ref, d_ref) in enumerate(((a_ref, ga_ref, da_ref), (b_ref, gb_ref, db_ref), (c_ref, gc_ref, dc_ref))):
            sl = slice(j * D, (j + 1) * D)
            g = _sigmoid(g_ref[...].astype(F32) + bg_ref[:, sl])
            d_ref[...] = (dm * g).astype(BF16)
            dmg = dm * br_ref[...].astype(F32) * (g * (1.0 - g))
            dmg_ref[:, sl] = dmg.astype(BF16)
            dbg_ref[:, sl] += jnp.sum(dmg, axis=0, keepdims=True)

    row = pl.BlockSpec((TM, D), lambda i: (i, 0))
    gates = [pl.BlockSpec((TM, D), functools.partial(lambda i, b: (i, b), b=mgb + j)) for j in range(3)]
    return pl.pallas_call(
        body, name=name, grid=(t // TM,),
        in_specs=[row, row, row, row, *gates, pl.BlockSpec((1, 3 * D), lambda i: (0, 0))],
        out_specs=(row, row, row, _dp_spec("merge"), pl.BlockSpec((8, 3 * D), lambda i: (0, 0))),
        out_shape=(jax.ShapeDtypeStruct((t, D), BF16),) * 3 + (jax.ShapeDtypeStruct((t, NP), BF16),
                                                                jax.ShapeDtypeStruct((8, 3 * D), F32)),
        compiler_params=_cparams(1, 48),
    )(dm, bra, brb, brc, p, p, p, b_gate)


def _adam_update(ns, g_ref, w_ref, m_ref, v_ref, go_ref, d_ref, mo_ref, vo_ref):
    g = g_ref[0].astype(F32)
    for s in range(1, ns):
        g = g + g_ref[s].astype(F32)
    mn = ADAM_B1 * m_ref[...] + (1.0 - ADAM_B1) * g
    vn = ADAM_B2 * v_ref[...] + (1.0 - ADAM_B2) * jnp.square(g)
    m_hat = mn / (1.0 - ADAM_B1 ** ADAM_STEP)
    v_hat = vn / (1.0 - ADAM_B2 ** ADAM_STEP)
    go_ref[...] = g
    d_ref[...] = -ADAM_LR * (m_hat / (jnp.sqrt(v_hat) + ADAM_EPS) + ADAM_WD * w_ref[...])
    mo_ref[...] = mn
    vo_ref[...] = vn


def _adamw(gsrc, w, m, v, name):
    ns, nl, r, c = gsrc.shape
    gb = gsrc.dtype.itemsize

    def fits(rows, cols):
        lanes = -(-cols // LANE) * LANE
        return ns * rows * lanes * gb <= ADAM_SRC_BYTES and rows * lanes * 4 <= ADAM_ROW_BYTES

    tr, tc = r, c
    if not fits(r, c):
        rows = [cand for cand in range(16, r, 16) if r % cand == 0 and fits(cand, c)]
        cols = [cand for cand in range(LANE, c, LANE) if c % cand == 0 and fits(r, cand)]
        if rows:
            tr = rows[-1]
        else:
            tc = cols[-1]

    def body(*refs):
        _adam_update(ns, *refs)

    row = pl.BlockSpec((None, tr, tc), lambda l, i, j: (l, i, j))
    return pl.pallas_call(
        body, name=name, grid=(nl, r // tr, c // tc),
        in_specs=[pl.BlockSpec((ns, None, tr, tc), lambda l, i, j: (0, l, i, j)), row, row, row],
        out_specs=(row,) * 4, out_shape=(jax.ShapeDtypeStruct((nl, r, c), F32),) * 4,
        compiler_params=_cparams(3, 48),
    )(gsrc, w, m, v)


def _adamw_small(items, name):
    k = len(items)

    def body(*refs):
        for j in range(k):
            _adam_update(items[j][0].shape[0], *refs[4 * j:4 * j + 4], *refs[4 * k + 4 * j:4 * k + 4 * j + 4])

    out = pl.pallas_call(
        body, name=name,
        out_shape=tuple(jax.ShapeDtypeStruct(w.shape, F32) for _, w, _, _ in items for _ in range(4)),
    )(*[a for item in items for a in item])
    return [out[4 * j:4 * j + 4] for j in range(k)]


def _rope_tables(ctx, seq):
    n_rows = seq // GRID_W
    pairs = HD // 4
    row = jnp.repeat(jnp.arange(n_rows, dtype=F32), GRID_W)
    col = jnp.tile(jnp.arange(GRID_W, dtype=F32), n_rows)
    freqs = ROPE_THETA ** (-jnp.arange(pairs, dtype=F32) * 2.0 / (HD // 2))
    ar, ac = row[:, None] * freqs, col[:, None] * freqs
    cos_l = jnp.concatenate([jnp.cos(ar), jnp.cos(ar), jnp.cos(ac), jnp.cos(ac)], axis=1)
    sin_l = jnp.concatenate([-jnp.sin(ar), jnp.sin(ar), -jnp.sin(ac), jnp.sin(ac)], axis=1)
    cos_t = jnp.concatenate([jnp.ones((ctx, HD), F32), cos_l], axis=0)
    sin_t = jnp.concatenate([jnp.zeros((ctx, HD), F32), sin_l], axis=0)
    return cos_t, sin_t


def _to_proj_layout(wt):
    parts = [wt[s:s + wd] for _, s, wd in _SEGS]
    used = sum(wd for _, _, wd in _SEGS)
    parts.append(jnp.zeros((NP - used, wt.shape[1]), wt.dtype))
    return jnp.concatenate(parts, axis=0)


def _from_proj_layout(g):
    order = sorted(_SEGS, key=lambda sg: sg[1])
    return jnp.concatenate([g[OFF[n]:OFF[n] + wd] for n, _, wd in order], axis=0)


def _row0(a):
    return a[..., 0, :]


def kernel(x, c, ctx, c_ctx, w_ada, b_ada, g_pre, g_post, w_in, conv_w, q_norm_g, k_norm_g, w_decay_fwd, b_decay_fwd, w_decay_bwd, b_decay_bwd, gla_norm_g, w_br_conv, w_br_attn, w_br_gla, b_gate, w_out, loss_target, m_c_ctx, m_w_ada, m_b_ada, m_g_pre, m_g_post, m_w_in, m_conv_w, m_q_norm_g, m_k_norm_g, m_w_decay_fwd, m_b_decay_fwd, m_w_decay_bwd, m_b_decay_bwd, m_gla_norm_g, m_w_br_conv, m_w_br_attn, m_w_br_gla, m_b_gate, m_w_out, v_c_ctx, v_w_ada, v_b_ada, v_g_pre, v_g_post, v_w_in, v_conv_w, v_q_norm_g, v_k_norm_g, v_w_decay_fwd, v_b_decay_fwd, v_w_decay_bwd, v_b_decay_bwd, v_gla_norm_g, v_w_br_conv, v_w_br_attn, v_w_br_gla, v_b_gate, v_w_out):
    seq, n_ctx = x.shape[1], ctx.shape[1]
    assert n_ctx % TM == 0 and seq % TM == 0 and seq % GRID_W == 0
    t = n_ctx + seq
    nct, ncc = n_ctx // TM, n_ctx // CH
    dev = 4 * lax.axis_index("x") + 2 * lax.axis_index("y") + lax.axis_index("c")
    ada_w = w_ada.shape[2]
    in_w = w_in.shape[2]
    br_r = w_br_conv.shape[1]

    def in_t(a, l):
        return a.transpose(2, 0, 1)[:, l, :]

    wb = [w.astype(BF16) for w in (w_ada, w_br_conv, w_br_attn, w_br_gla, w_out)]
    wall = _all_gather([wb[0][0], in_t(w_in, 0).astype(BF16), conv_w, w_decay_fwd, w_decay_bwd],
                       "gather_first")
    later = _GatherRider([in_t(w_in, 1).astype(BF16), wb[0][1], wb[1], wb[2], wb[3], wb[4]])

    def full_small(g):
        return g.transpose(1, 2, 0, 3).reshape(DEPTH, g.shape[2], NDEV * g.shape[3])

    def full_in(g):
        return _to_proj_layout(g.reshape(IN_WIDTH, D))

    def full_ada(g):
        return g.transpose(1, 0, 2).reshape(D, 3 * D)

    w_ada_f = [full_ada(wall[0]), None]
    wp = [full_in(wall[1]), None]
    conv_f, wdf_f, wdb_f = full_small(wall[2]), full_small(wall[3]), full_small(wall[4])

    cos_t, sin_t = _rope_tables(n_ctx, seq)
    cc = jnp.concatenate([c_ctx[None, :], c.reshape(1, D), jnp.zeros((6, D), F32)], axis=0)
    silu_cc, dsilu_cc = _ada_in(cc)

    conv8, wd_pad, bd = [], [], []
    for l in range(DEPTH):
        conv8.append(jnp.concatenate([conv_f[l], jnp.zeros((5, D), F32)], axis=0))
        zr = jnp.zeros((GLA_RANK, GH * GDK), F32)
        wd_pad.append(jnp.concatenate([jnp.concatenate([wdf_f[l], zr], axis=1), jnp.concatenate([zr, wdb_f[l]], axis=1),
                                       jnp.zeros((R_PAD - 2 * GLA_RANK, D), F32)], axis=0))
        bd.append(jnp.concatenate([b_decay_fwd[l], b_decay_bwd[l]])[None, :])

    xs = jnp.concatenate([ctx[0], x[0]], axis=0)
    saved = []
    for l in range(DEPTH):
        n = f"l{l}_"
        mod = _mm(silu_cc, w_ada_f[l], n + "mod", bias=b_ada[l][None, :])
        mod3 = mod[0:2].reshape(2, 3, D)
        h = _prenorm_fwd(xs, g_pre[l][None, :], mod3, nct, n + "prenorm")
        p = _mm(h, wp[l], n + "proj", tb=True, out_dtype=BF16)
        cv, ya = _conv_fwd(p, conv8[l], nct, n + "conv")
        qr, kr = _qk_prep_fwd(p, q_norm_g[l][None, :], k_norm_g[l][None, :], cos_t, sin_t, n + "qk_prep")
        att, lse, *got = _attn_fwd(qr, kr, p, nct, n + "attn", rider=later if l == 0 else None)
        if l == 0:
            wp[1], w_ada_f[1] = full_in(got[0]), full_ada(got[1])
            w_brs_f = [g.transpose(1, 0, 2, 3).reshape(DEPTH, D, D) for g in got[2:]]
        z, la = _decay_fwd(p, wd_pad[l], bd[l], n + "decay")
        of, stf, ob, stb = _gla_fwd(p, la, ncc, n + "gla")
        yb, yc = _branch_fwd(att, of, ob, p, gla_norm_g[l][None, :], n + "branch")
        bra = _mm(ya, w_brs_f[0][l], n + "br_conv", out_dtype=BF16)
        brb = _mm(yb, w_brs_f[1][l], n + "br_attn", out_dtype=BF16)
        brc = _mm(yc, w_brs_f[2][l], n + "br_gla", out_dtype=BF16)
        mm_ = _merge_fwd(bra, brb, brc, p, b_gate[l][None, :], n + "merge")
        out = _mm(mm_, w_brs_f[3][l], n + "out")
        x_new = _post_fwd(xs, out, g_post[l][None, :], mod3, nct, n + "post")
        saved.append(dict(x=xs, mod3=mod3, h=h, p=p, cv=cv, ya=ya, qr=qr, kr=kr, att=att, lse=lse, z=z, la=la, of=of, ob=ob,
                          stf=stf, stb=stb, yb=yb, yc=yc, bra=bra, brb=brb, brc=brc, m=mm_, out=out))
        xs = x_new

    dx, sq = _loss_grad(xs, loss_target[0], nct, "loss")
    loss = lax.psum(0.5 * sq[0, 0] / D, ("x", "y", "c"))

    gw = {k: [None] * DEPTH for k in ("w_in", "br_conv", "br_attn", "br_gla", "out", "b_gate", "g_pre", "g_post",
                                      "conv_w", "qg", "kg", "wd", "bdec", "gla_g", "dmod")}
    dctx = []

    def in_slots(l):
        return _from_proj_layout(gw["w_in"][l]).reshape(NDEV, in_w, D)

    def br_slots():
        return [jnp.stack([gw[k][l].reshape(NDEV, br_r, D) for l in range(DEPTH)], axis=1)
                for k in ("br_conv", "br_attn", "br_gla", "out")]

    for l in reversed(range(DEPTH)):
        n = f"l{l}_b_"
        s = saved[l]
        p = s["p"]
        d_out, dgt, gw["g_post"][l] = _post_bwd(dx, s["out"], g_post[l][None, :], s["mod3"], nct, n + "post")
        dm = _mm(d_out, w_brs_f[3][l], n + "dm", tb=True, out_dtype=BF16)
        gw["out"][l] = _mm(s["m"], d_out, n + "dw_out", ta=True, out_dtype=BF16)
        dbra, dbrb, dbrc, dp, gw["b_gate"][l] = _merge_bwd(dm, s["bra"], s["brb"], s["brc"], p, b_gate[l][None, :], n + "merge")
        dya = _mm(dbra, w_brs_f[0][l], n + "dya", tb=True, out_dtype=BF16)
        dyb = _mm(dbrb, w_brs_f[1][l], n + "dyb", tb=True, out_dtype=BF16)
        dyc = _mm(dbrc, w_brs_f[2][l], n + "dyc", tb=True, out_dtype=BF16)
        gw["br_conv"][l] = _mm(s["ya"], dbra, n + "dw_conv", ta=True, out_dtype=BF16)
        gw["br_attn"][l] = _mm(s["yb"], dbrb, n + "dw_attn", ta=True, out_dtype=BF16)
        gw["br_gla"][l] = _mm(s["yc"], dbrc, n + "dw_gla", ta=True, out_dtype=BF16)
        dcv, dp = _conv_bwd_a(dya, p, s["cv"], dp, n + "conv_a")
        dp, gw["conv_w"][l] = _conv_bwd_b(dcv, p, conv8[l], nct, dp, n + "conv_b")
        datt, dgo, dp, gw["gla_g"][l] = _branch_bwd(dyb, dyc, s["att"], s["of"], s["ob"], p, gla_norm_g[l][None, :], dp, n + "branch")
        ex1 = _ExchangeRider([in_slots(DEPTH - 1)] + br_slots()) if l == 0 else None
        dqr, dkr, dv, *got = _attn_bwd(s["qr"], s["kr"], p, s["att"], s["lse"], datt, nct, n + "attn", rider=ex1)
        if l == 0:
            recv_in1, recv_br = got[0], got[1:]
        dp, dk, gw["qg"][l], gw["kg"][l] = _qk_prep_bwd(dqr, dkr, p, q_norm_g[l][None, :], k_norm_g[l][None, :], cos_t, sin_t, dp, n + "qk_prep")
        gf, gb = _gla_bwd(p, s["la"], dgo, s["stf"], s["stb"], ncc, n + "gla")
        dp, dr, gw["bdec"][l], gw["wd"][l] = _gla_merge_bwd(gf, gb, s["z"], p, wd_pad[l], dp, n + "gla_merge")
        dp = _dp_tail(dk, dv, dr, dp, n + "dp_tail")
        gw["w_in"][l] = _mm(dp, s["h"], n + "dw_in", ta=True, out_dtype=BF16, tk=t // 2 if t % 32 == 0 else None)
        if l == 0:
            dh, recv_in0 = _mm(dp, wp[l], n + "dh", tk=NP // 4, rider=_ExchangeRider([in_slots(0)]))
        else:
            dh = _mm(dp, wp[l], n + "dh", tk=NP // 4)
        dx, dsh, dsc, gw["g_pre"][l] = _prenorm_bwd(dh, s["x"], dx, g_pre[l][None, :], s["mod3"], nct, n + "prenorm")
        dmod = jnp.stack([_row0(dsh), _row0(dsc), _row0(dgt)], axis=1).reshape(2, 3 * D)
        gw["dmod"][l] = dmod
        dmod8 = jnp.concatenate([dmod, jnp.zeros((6, 3 * D), F32)], axis=0)
        dctx.append(_mm(dmod8, w_ada_f[l], n + "dsilu", tb=True))
    grad_x = dx[n_ctx:][None]
    g_cctx = _cctx_grad(dctx[0], dctx[1], dsilu_cc)[0]

    def st2(name):
        return jnp.stack(gw[name])

    g_b_ada = jnp.stack([gw["dmod"][l][0] + gw["dmod"][l][1] for l in range(DEPTH)])
    g_bdf = jnp.stack([gw["bdec"][l][0, :GH * GDK] for l in range(DEPTH)])
    g_bdb = jnp.stack([gw["bdec"][l][0, GH * GDK:] for l in range(DEPTH)])
    g_wdf = jnp.stack([gw["wd"][l][0:GLA_RANK, :GH * GDK] for l in range(DEPTH)])
    g_wdb = jnp.stack([gw["wd"][l][GLA_RANK:2 * GLA_RANK, GH * GDK:] for l in range(DEPTH)])
    rep_grads = [g_cctx, g_b_ada, st2("g_pre")[:, 0], st2("g_post")[:, 0], st2("qg")[:, 0], st2("kg")[:, 0], g_bdf, g_bdb,
                 st2("gla_g")[:, 0], st2("b_gate")[:, 0]]
    rep_w = [c_ctx, b_ada, g_pre, g_post, q_norm_g, k_norm_g, b_decay_fwd, b_decay_bwd, gla_norm_g, b_gate]
    rep_m = [m_c_ctx, m_b_ada, m_g_pre, m_g_post, m_q_norm_g, m_k_norm_g, m_b_decay_fwd, m_b_decay_bwd, m_gla_norm_g, m_b_gate]
    rep_v = [v_c_ctx, v_b_ada, v_g_pre, v_g_post, v_q_norm_g, v_k_norm_g, v_b_decay_fwd, v_b_decay_bwd, v_gla_norm_g, v_b_gate]
    def two_d(a):
        return a.reshape(1, -1) if a.ndim == 1 else a

    def owner_slots(g):
        return g.reshape(DEPTH, g.shape[1], NDEV, g.shape[2] // NDEV).transpose(2, 0, 1, 3)

    n_rep = len(rep_grads)
    small = _comm_alone(_Riders([
        _GatherRider([two_d(g) for g in rep_grads] + [silu_cc[0:2], jnp.stack(gw["dmod"])]),
        _ExchangeRider([owner_slots(st2("conv_w")[:, 0:3]), owner_slots(g_wdf), owner_slots(g_wdb)])]),
        "exchange_small_grads")
    rep_src, (a_all, d_all), sh_src = small[:n_rep], small[n_rep:n_rep + 2], small[n_rep + 2:]
    sh_w = [conv_w, w_decay_fwd, w_decay_bwd]
    sh_m = [m_conv_w, m_w_decay_fwd, m_w_decay_bwd]
    sh_v = [v_conv_w, v_w_decay_fwd, v_w_decay_bwd]
    small_out = _adamw_small(
        [(g, two_d(w), two_d(m), two_d(v)) for g, w, m, v in zip(rep_src, rep_w, rep_m, rep_v)]
        + list(zip(sh_src, sh_w, sh_m, sh_v)), "adam_small")
    rep_g, rep_d, rep_nm, rep_nv = [[small_out[j][k].reshape(rep_w[j].shape) for j in range(n_rep)] for k in range(4)]
    sh_gr, sh_d, sh_nm, sh_nv = [[small_out[n_rep + j][k] for j in range(len(sh_w))] for k in range(4)]

    a_all = a_all.reshape(NDEV * 2, D)
    d_all = d_all.transpose(1, 0, 2, 3).reshape(DEPTH, NDEV * 2, 3 * D)
    g_ada = jnp.stack([_mm(a_all, lax.dynamic_slice_in_dim(d_all[l], dev * ada_w, ada_w, axis=1), f"dw_ada{l}",
                           ta=True, precise=True, tk=NDEV * 2) for l in range(DEPTH)])
    ada_g, ada_d, ada_nm, ada_nv = _adamw(g_ada[None], w_ada, m_w_ada, v_w_ada, "adam_ada")

    big_w = [w_br_conv, w_br_attn, w_br_gla, w_out]
    big_m = [m_w_br_conv, m_w_br_attn, m_w_br_gla, m_w_out]
    big_v = [v_w_br_conv, v_w_br_attn, v_w_br_gla, v_w_out]
    big_out = [_adamw(recv_br[j], big_w[j], big_m[j], big_v[j], f"adam_big{j}") for j in range(len(big_w))]
    in_out = [_adamw(r_[:, None], in_t(w_in, l)[None], in_t(m_w_in, l)[None], in_t(v_w_in, l)[None], f"adam_in{l}")
              for l, r_ in enumerate((recv_in0, recv_in1))]
    in_res = [jnp.stack([in_out[l][k][0] for l in range(DEPTH)], axis=1).transpose(1, 2, 0) for k in range(4)]
    big_g, big_d, big_nm, big_nv = [[in_res[k]] + [o[k] for o in big_out] for k in range(4)]

    def ordered(rep, ada, big, sh):
        c_ctx_, b_ada_, g_pre_, g_post_, qg_, kg_, bdf_, bdb_, glag_, bgate_ = rep
        w_in_, brc_, bra_, brg_, wout_ = big
        conv_, wdf_, wdb_ = sh
        return [c_ctx_, ada, b_ada_, g_pre_, g_post_, w_in_, conv_, qg_, kg_, wdf_, bdf_, wdb_, bdb_, glag_,
                brc_, bra_, brg_, bgate_, wout_]

    return (loss, grad_x,
            *ordered(rep_g, ada_g, big_g, sh_gr), *ordered(rep_d, ada_d, big_d, sh_d),
            *ordered(rep_nm, ada_nm, big_nm, sh_nm), *ordered(rep_nv, ada_nv, big_nv, sh_nv))
```

```python
import functools

import numpy as np
import jax
import jax.numpy as jnp
from jax import lax
from jax.experimental import pallas as pl
from jax.experimental.pallas import tpu as pltpu

F32, BF16 = jnp.float32, jnp.bfloat16
HIGHEST = lax.Precision.HIGHEST

D = 1024
DEPTH = 2
GRID_W = 64
NH, NKV, HD = 8, 2, 128
GROUP = NH // NKV
ROPE_THETA = 10000.0
ATTN_SCALE = HD ** -0.5
Q_FOLD = ATTN_SCALE * 1.4426950408889634
P_HALO = 16
GH, GDK, GDV = 4, 128, 256
GLA_RANK = 16
GLA_TAU = 16.0
CH = 64
GLA_SCALE = GDK ** -0.5
EPS = 1e-6
NDEV = 8
LANE = 128
TM = 256
ATTN_HEADS_PER_STEP = 1
ATTN_KEY_CHUNK = 8192

ADAM_LR, ADAM_B1, ADAM_B2, ADAM_EPS, ADAM_WD, ADAM_STEP = 0.001, 0.9, 0.999, 1e-08, 0.01, 10

_SEGS = (("a_b", 0, 1024), ("a_z", 3072, 1024), ("a_c", 1024, 1024), ("a_x", 2048, 1024),
         ("z_attn", 5632, 1024), ("zg", 8736, 1024), ("gv", 7680, 1024), ("gq", 6656, 512), ("gk", 7168, 512),
         ("q", 4096, 1024), ("mg", 9760, 3072), ("k", 5120, 256), ("v", 5376, 256), ("r", 8704, 32))
DP_BLOCKS = {"conv_a": ("a_b", 2048), "conv_b": ("a_c", 2048), "branch": ("z_attn", 2048), "gla": ("gv", 2048),
             "q": ("q", 1024), "merge": ("mg", 3072), "tail": ("k", 1024)}
IN_WIDTH = 12832
NP = 13312
OFF = {}
_o = 0
for _n, _s, _w in _SEGS:
    OFF[_n] = _o
    _o += _w
R_PAD = 128


def _cparams(ngrid, vmem_mb):
    return pltpu.CompilerParams(dimension_semantics=("arbitrary",) * ngrid, vmem_limit_bytes=vmem_mb << 20)


def _pick(n, cands):
    for c in cands:
        if n % c == 0:
            return c
    return n


def _sigmoid(x):
    return 1.0 / (1.0 + jnp.exp(-x))


ADAM_SRC_BYTES = 8 << 20
ADAM_ROW_BYTES = 1 << 20


def _all_gather(xs, name):
    return _comm_alone(_GatherRider(xs), name)


_HBM = pl.BlockSpec(memory_space=pl.ANY)


class _Rider:
    def __init__(self, xs, out_shapes, remote_copies=NDEV - 1):
        self.xs, self.n = list(xs), len(xs)
        self.out_shape = [jax.ShapeDtypeStruct(s, x.dtype) for s, x in zip(out_shapes, xs)]
        self.scratch = [pltpu.SemaphoreType.DMA((remote_copies * self.n,)),
                        pltpu.SemaphoreType.DMA((remote_copies * self.n,)), pltpu.SemaphoreType.DMA((self.n,))]


class _GatherRider(_Rider):
    def __init__(self, xs):
        super().__init__(xs, [(NDEV,) + x.shape for x in xs])

    def _parts(self, x_refs, out_refs, sems):
        n = self.n
        send_sems, recv_sems, local_sems = sems
        mx, my, mc = lax.axis_index("x"), lax.axis_index("y"), lax.axis_index("c")
        me, sibling = (mx, my, mc), (mx, my, 1 - mc)
        chips = [(1 - mx, my), (mx, 1 - my), (1 - mx, 1 - my)]

        def slot(a, px, py, pc):
            return out_refs[a].at[4 * px + 2 * py + pc]

        def copy(k, a, block, to, own=False):
            return pltpu.make_async_remote_copy(
                src_ref=x_refs[a] if own else slot(a, *block), dst_ref=slot(a, *block),
                send_sem=send_sems.at[k * n + a], recv_sem=recv_sems.at[k * n + a],
                device_id=to, device_id_type=pl.DeviceIdType.MESH)

        mine = [pltpu.make_async_copy(x_refs[a], slot(a, *me), local_sems.at[a]) for a in range(n)]
        first = [copy(0, a, me, sibling, own=True) for a in range(n)]
        first += [copy(1 + j, a, me, (*chip, mc), own=True) for a in range(n) for j, chip in enumerate(chips)]
        landed = [copy(1 + j, a, (*chip, mc), me) for a in range(n) for j, chip in enumerate(chips)]
        passed = [copy(4 + j, a, (*chip, mc), sibling) for a in range(n) for j, chip in enumerate(chips)]
        from_sibling = [copy(0, a, sibling, me) for a in range(n)]
        from_sibling += [copy(4 + j, a, (*chip, 1 - mc), me) for a in range(n) for j, chip in enumerate(chips)]
        return mine, first, landed, passed, from_sibling

    def start(self, x_refs, out_refs, sems):
        mine, first, _, _, _ = self._parts(x_refs, out_refs, sems)
        for cp in mine + first:
            cp.start()

    def middle(self, x_refs, out_refs, sems):
        _, _, landed, passed, _ = self._parts(x_refs, out_refs, sems)
        for got, fwd in zip(landed, passed):
            got.wait_recv()
            fwd.start()

    def finish(self, x_refs, out_refs, sems):
        mine, first, _, passed, from_sibling = self._parts(x_refs, out_refs, sems)
        for cp in from_sibling:
            cp.wait_recv()
        for cp in first + passed:
            cp.wait_send()
        for cp in mine:
            cp.wait()


class _ExchangeRider(_Rider):
    def __init__(self, xs, chips_only=False):
        self.chips_only = chips_only
        super().__init__(xs, [x.shape for x in xs], 3 if chips_only else NDEV - 1)

    def _parts(self, x_refs, out_refs, sems):
        n = self.n
        send_sems, recv_sems, local_sems = sems
        mx, my, mc = lax.axis_index("x"), lax.axis_index("y"), lax.axis_index("c")
        me = 2 * mx + my if self.chips_only else 4 * mx + 2 * my + mc
        mine = [pltpu.make_async_copy(x_refs[a].at[me], out_refs[a].at[me], local_sems.at[a]) for a in range(n)]
        copies = []
        for a in range(n):
            for rel in range(1, 4 if self.chips_only else NDEV):
                bits = rel << 1 if self.chips_only else rel
                px = (1 - mx) if bits & 4 else mx
                py = (1 - my) if bits & 2 else my
                pc = (1 - mc) if bits & 1 else mc
                peer = 2 * px + py if self.chips_only else 4 * px + 2 * py + pc
                k = (rel - 1) * n + a
                copies.append(pltpu.make_async_remote_copy(
                    src_ref=x_refs[a].at[peer], dst_ref=out_refs[a].at[me],
                    send_sem=send_sems.at[k], recv_sem=recv_sems.at[k],
                    device_id=(px, py, pc), device_id_type=pl.DeviceIdType.MESH))
        return mine, copies

    def start(self, x_refs, out_refs, sems):
        mine, copies = self._parts(x_refs, out_refs, sems)
        for cp in mine + copies:
            cp.start()

    def middle(self, x_refs, out_refs, sems):
        pass

    def finish(self, x_refs, out_refs, sems):
        mine, copies = self._parts(x_refs, out_refs, sems)
        for cp in copies:
            cp.wait_recv()
        for cp in copies:
            cp.wait_send()
        for cp in mine:
            cp.wait()


class _SwapRider(_Rider):
    def __init__(self, xs):
        super().__init__(xs, [x.shape for x in xs], 1)

    def _parts(self, x_refs, out_refs, sems):
        send_sems, recv_sems, _ = sems
        sibling = (lax.axis_index("x"), lax.axis_index("y"), 1 - lax.axis_index("c"))
        return [pltpu.make_async_remote_copy(
            src_ref=x_refs[a], dst_ref=out_refs[a], send_sem=send_sems.at[a], recv_sem=recv_sems.at[a],
            device_id=sibling, device_id_type=pl.DeviceIdType.MESH) for a in range(self.n)]

    def start(self, x_refs, out_refs, sems):
        for cp in self._parts(x_refs, out_refs, sems):
            cp.start()

    def middle(self, x_refs, out_refs, sems):
        pass

    def finish(self, x_refs, out_refs, sems):
        copies = self._parts(x_refs, out_refs, sems)
        for cp in copies:
            cp.wait_recv()
        for cp in copies:
            cp.wait_send()


class _Riders:
    def __init__(self, riders):
        self.riders = list(riders)
        self.xs = [x for r in self.riders for x in r.xs]
        self.n = len(self.xs)
        self.out_shape = [s for r in self.riders for s in r.out_shape]
        self.scratch = [s for r in self.riders for s in r.scratch]

    def _each(self, method, x_refs, out_refs, sems):
        a = b = 0
        for r in self.riders:
            getattr(r, method)(x_refs[a:a + r.n], out_refs[a:a + r.n], sems[b:b + len(r.scratch)])
            a, b = a + r.n, b + len(r.scratch)

    def start(self, *refs):
        self._each("start", *refs)

    def middle(self, *refs):
        self._each("middle", *refs)

    def finish(self, *refs):
        self._each("finish", *refs)


def _comm_alone(rider, name):
    n = rider.n

    def body(*refs):
        x_refs, out_refs, sems = refs[:n], refs[n:2 * n], refs[2 * n:]
        rider.start(x_refs, out_refs, sems)
        rider.middle(x_refs, out_refs, sems)
        rider.finish(x_refs, out_refs, sems)

    return pl.pallas_call(
        body, name=name, out_shape=tuple(rider.out_shape), in_specs=[_HBM] * n, out_specs=(_HBM,) * n,
        scratch_shapes=rider.scratch,
    )(*rider.xs)


def _with_rider(body, nin, nout, rider, first, mid, last):
    if rider is None:
        return body
    n = rider.n

    def wrapped(*refs):
        ins, x_refs = refs[:nin], refs[nin:nin + n]
        outs, out_refs = refs[nin + n:nin + n + nout], refs[nin + n + nout:nin + 2 * n + nout]
        ns = len(rider.scratch)
        scratch, sems = refs[nin + 2 * n + nout:len(refs) - ns], refs[len(refs) - ns:]

        @pl.when(first())
        def _():
            rider.start(x_refs, out_refs, sems)

        body(*ins, *outs, *scratch)

        @pl.when(mid())
        def _():
            rider.middle(x_refs, out_refs, sems)

        @pl.when(last())
        def _():
            rider.finish(x_refs, out_refs, sems)

    return wrapped


def _mm(a, b, name, ta=False, tb=False, out_dtype=F32, bias=None, precise=False, tm=None, tn=None, tk=None, rider=None):
    m, k = (a.shape[1], a.shape[0]) if ta else a.shape
    n = b.shape[0] if tb else b.shape[1]
    assert k == (b.shape[1] if tb else b.shape[0])
    tm = tm or _pick(m, (1088, 1024, 512, 256, 128))
    tn = tn or _pick(n, (1024, 512, 384, 256, 128))
    tk = tk or _pick(k, (1024, 1088, 512, 256, 128))
    nk = k // tk
    dn = (((0 if ta else 1,), (1 if tb else 0,)), ((), ()))

    def body(*refs):
        if bias is None:
            a_ref, b_ref, o_ref = refs[:3]
            bias_ref = None
        else:
            a_ref, b_ref, bias_ref, o_ref = refs[:4]
        x, y = a_ref[...], b_ref[...]
        if precise:
            p = lax.dot_general(x.astype(F32), y.astype(F32), dn, preferred_element_type=F32, precision=HIGHEST)
        else:
            p = lax.dot_general(x.astype(BF16), y.astype(BF16), dn, preferred_element_type=F32)

        def finish(acc):
            if bias_ref is not None:
                acc = acc + bias_ref[...]
            o_ref[...] = acc.astype(out_dtype)

        if nk == 1:
            finish(p)
        else:
            acc_ref = refs[-1]
            kk = pl.program_id(2)

            @pl.when(kk == 0)
            def _():
                acc_ref[...] = p

            @pl.when(kk > 0)
            def _():
                acc_ref[...] += p

            @pl.when(kk == nk - 1)
            def _():
                finish(acc_ref[...])

    a_spec = pl.BlockSpec((tk, tm), lambda i, j, kk: (kk, i)) if ta else pl.BlockSpec((tm, tk), lambda i, j, kk: (i, kk))
    b_spec = pl.BlockSpec((tn, tk), lambda i, j, kk: (j, kk)) if tb else pl.BlockSpec((tk, tn), lambda i, j, kk: (kk, j))
    in_specs = [a_spec, b_spec]
    args = [a, b]
    if bias is not None:
        in_specs.append(pl.BlockSpec((1, tn), lambda i, j, kk: (0, j)))
        args.append(bias)
    grid = (m // tm, n // tn, nk)
    out_spec = pl.BlockSpec((tm, tn), lambda i, j, kk: (i, j))
    scratch = [pltpu.VMEM((tm, tn), F32)] if nk > 1 else []
    if rider is None:
        return pl.pallas_call(
            body, name=name, grid=grid, in_specs=in_specs, out_specs=out_spec,
            out_shape=jax.ShapeDtypeStruct((m, n), out_dtype), scratch_shapes=scratch, compiler_params=_cparams(3, 56),
        )(*args)

    def at(step):
        return lambda: ((pl.program_id(0) == step[0]) & (pl.program_id(1) == step[1]) & (pl.program_id(2) == step[2]))

    end = tuple(g - 1 for g in grid)
    return pl.pallas_call(
        _with_rider(body, len(args), 1, rider, at((0, 0, 0)), at((grid[0] // 2, 0, 0)), at(end)),
        name=name, grid=grid, in_specs=in_specs + [_HBM] * rider.n, out_specs=(out_spec,) + (_HBM,) * rider.n,
        out_shape=(jax.ShapeDtypeStruct((m, n), out_dtype),) + tuple(rider.out_shape),
        scratch_shapes=scratch + rider.scratch, compiler_params=_cparams(3, 56),
    )(*args, *rider.xs)


def _ada_in(cc):
    def body(c_ref, s_ref, d_ref):
        x = c_ref[...]
        sg = _sigmoid(x)
        s_ref[...] = x * sg
        d_ref[...] = sg * (1.0 + x * (1.0 - sg))

    return pl.pallas_call(body, name="ada_in", out_shape=(jax.ShapeDtypeStruct(cc.shape, F32),) * 2)(cc)


def _cctx_grad(t0, t1, dsilu):
    def body(a_ref, b_ref, d_ref, o_ref):
        o_ref[...] = (a_ref[...] + b_ref[...]) * d_ref[...]

    return pl.pallas_call(body, name="cctx_grad", out_shape=jax.ShapeDtypeStruct(t0.shape, F32))(t0, t1, dsilu)


def _seg_spec(nct, rows=3):
    return pl.BlockSpec((None, rows, D), lambda i: (jnp.where(i >= nct, 1, 0), 0, 0))


def _prenorm_fwd(x, g_pre, mod3, nct, name):
    t = x.shape[0]

    def body(x_ref, g_ref, mod_ref, h_ref):
        xv = x_ref[...]
        r = lax.rsqrt(jnp.mean(xv * xv, axis=-1, keepdims=True) + EPS)
        y = xv * r * g_ref[...]
        h_ref[...] = (y * (1.0 + mod_ref[1:2, :]) + mod_ref[0:1, :]).astype(BF16)

    return pl.pallas_call(
        body, name=name, grid=(t // TM,),
        in_specs=[pl.BlockSpec((TM, D), lambda i: (i, 0)), pl.BlockSpec((1, D), lambda i: (0, 0)), _seg_spec(nct)],
        out_specs=pl.BlockSpec((TM, D), lambda i: (i, 0)),
        out_shape=jax.ShapeDtypeStruct((t, D), BF16), compiler_params=_cparams(1, 32),
    )(x, g_pre, mod3)


def _prenorm_bwd(dh, x, dxo, g_pre, mod3, nct, name):
    t = x.shape[0]

    def body(dh_ref, x_ref, dxo_ref, g_ref, mod_ref, dx_ref, dsh_ref, dsc_ref, dg_ref):
        i = pl.program_id(0)
        xv, dhv, g = x_ref[...], dh_ref[...], g_ref[...]
        r = lax.rsqrt(jnp.mean(xv * xv, axis=-1, keepdims=True) + EPS)
        xh = xv * r
        dy = dhv * (1.0 + mod_ref[1:2, :])
        dxh = dy * g
        dx_ref[...] = dxo_ref[...] + r * (dxh - xh * jnp.mean(dxh * xh, axis=-1, keepdims=True))

        @pl.when((i == 0) | (i == nct))
        def _():
            dsh_ref[...] = jnp.zeros_like(dsh_ref)
            dsc_ref[...] = jnp.zeros_like(dsc_ref)

        @pl.when(i == 0)
        def _():
            dg_ref[...] = jnp.zeros_like(dg_ref)

        dsh_ref[...] += jnp.sum(dhv, axis=0, keepdims=True)
        dsc_ref[...] += jnp.sum(dhv * (xh * g), axis=0, keepdims=True)
        dg_ref[...] += jnp.sum(dy * xh, axis=0, keepdims=True)

    row = pl.BlockSpec((TM, D), lambda i: (i, 0))
    seg8 = pl.BlockSpec((None, 8, D), lambda i: (jnp.where(i >= nct, 1, 0), 0, 0))
    return pl.pallas_call(
        body, name=name, grid=(t // TM,),
        in_specs=[row, row, row, pl.BlockSpec((1, D), lambda i: (0, 0)), _seg_spec(nct)],
        out_specs=(row, seg8, seg8, pl.BlockSpec((8, D), lambda i: (0, 0))),
        out_shape=(jax.ShapeDtypeStruct((t, D), F32), jax.ShapeDtypeStruct((2, 8, D), F32),
                   jax.ShapeDtypeStruct((2, 8, D), F32), jax.ShapeDtypeStruct((8, D), F32)),
        compiler_params=_cparams(1, 32),
    )(dh, x, dxo, g_pre, mod3)


def _post_fwd(x, out, g_post, mod3, nct, name):
    t = x.shape[0]

    def body(x_ref, o_ref, g_ref, mod_ref, y_ref):
        ov = o_ref[...]
        r = lax.rsqrt(jnp.mean(ov * ov, axis=-1, keepdims=True) + EPS)
        y_ref[...] = x_ref[...] + mod_ref[2:3, :] * (ov * r * g_ref[...])

    row = pl.BlockSpec((TM, D), lambda i: (i, 0))
    return pl.pallas_call(
        body, name=name, grid=(t // TM,),
        in_specs=[row, row, pl.BlockSpec((1, D), lambda i: (0, 0)), _seg_spec(nct)],
        out_specs=row, out_shape=jax.ShapeDtypeStruct((t, D), F32), compiler_params=_cparams(1, 32),
    )(x, out, g_post, mod3)


def _post_bwd(dxo, out, g_post, mod3, nct, name):
    t = out.shape[0]

    def body(dx_ref, o_ref, g_ref, mod_ref, do_ref, dgt_ref, dg_ref):
        i = pl.program_id(0)
        ov, dxv, g = o_ref[...], dx_ref[...], g_ref[...]
        r = lax.rsqrt(jnp.mean(ov * ov, axis=-1, keepdims=True) + EPS)
        nh = ov * r
        dn = dxv * mod_ref[2:3, :]
        dnh = dn * g
        do_ref[...] = (r * (dnh - nh * jnp.mean(dnh * nh, axis=-1, keepdims=True))).astype(BF16)

        @pl.when((i == 0) | (i == nct))
        def _():
            dgt_ref[...] = jnp.zeros_like(dgt_ref)

        @pl.when(i == 0)
        def _():
            dg_ref[...] = jnp.zeros_like(dg_ref)

        dgt_ref[...] += jnp.sum(dxv * (nh * g), axis=0, keepdims=True)
        dg_ref[...] += jnp.sum(dn * nh, axis=0, keepdims=True)

    row = pl.BlockSpec((TM, D), lambda i: (i, 0))
    seg8 = pl.BlockSpec((None, 8, D), lambda i: (jnp.where(i >= nct, 1, 0), 0, 0))
    return pl.pallas_call(
        body, name=name, grid=(t // TM,),
        in_specs=[row, row, pl.BlockSpec((1, D), lambda i: (0, 0)), _seg_spec(nct)],
        out_specs=(row, seg8, pl.BlockSpec((8, D), lambda i: (0, 0))),
        out_shape=(jax.ShapeDtypeStruct((t, D), BF16), jax.ShapeDtypeStruct((2, 8, D), F32),
                   jax.ShapeDtypeStruct((8, D), F32)),
        compiler_params=_cparams(1, 32),
    )(dxo, out, g_post, mod3)


def _loss_grad(y, target, nct, name):
    t = y.shape[0]

    def body(y_ref, t_ref, dy_ref, l_ref):
        i = pl.program_id(0)

        @pl.when(i == 0)
        def _():
            l_ref[...] = jnp.zeros_like(l_ref)

        @pl.when(i < nct)
        def _():
            dy_ref[...] = jnp.zeros_like(dy_ref)

        @pl.when(i >= nct)
        def _():
            err = y_ref[...] - t_ref[...]
            dy_ref[...] = err / D
            l_ref[...] += jnp.sum(jnp.sum(err * err, axis=1, keepdims=True), axis=0, keepdims=True)

    row = pl.BlockSpec((TM, D), lambda i: (i, 0))
    return pl.pallas_call(
        body, name=name, grid=(t // TM,),
        in_specs=[row, pl.BlockSpec((TM, D), lambda i: (jnp.maximum(i - nct, 0), 0))],
        out_specs=(row, pl.BlockSpec((8, LANE), lambda i: (0, 0))),
        out_shape=(jax.ShapeDtypeStruct((t, D), F32), jax.ShapeDtypeStruct((8, LANE), F32)),
        compiler_params=_cparams(1, 32),
    )(y, target)


def _pcol(name, width):
    assert OFF[name] % width == 0
    blk = OFF[name] // width
    return pl.BlockSpec((TM, width), lambda i: (i, blk))


def _shift_rows(u, prev_row, next_row):
    n = u.shape[0]
    row = lax.broadcasted_iota(jnp.int32, u.shape, 0)
    prev = jnp.where(row == 0, prev_row, pltpu.roll(u, 1, 0))
    nxt = jnp.where(row == n - 1, next_row, pltpu.roll(u, n - 1, 0))
    return prev, nxt


def _halo_specs(width, nt, blk=0, rows=8):
    per = TM // rows
    prev = pl.BlockSpec((rows, width), lambda i: (jnp.maximum(i * per - 1, 0), blk))
    nxt = pl.BlockSpec((rows, width), lambda i: (jnp.minimum((i + 1) * per, nt * per - 1), blk))
    return prev, nxt


def _conv_fwd(p, conv_w8, nct, name):
    t = p.shape[0]
    nt = t // TM

    def body(ab_ref, ac_ref, ax_ref, az_ref, acp_ref, axp_ref, acn_ref, axn_ref, w_ref, cv_ref, ya_ref):
        i = pl.program_id(0)
        def f(ref, rows=slice(None)):
            return ref[rows, :].astype(F32)

        u = f(ac_ref) * f(ax_ref)
        mp = jnp.where((i == 0) | (i == nct), 0.0, 1.0)
        mn = jnp.where((i == nct - 1) | (i == nt - 1), 0.0, 1.0)
        last, first = slice(P_HALO - 1, P_HALO), slice(0, 1)
        prev, nxt = _shift_rows(u, f(acp_ref, last) * f(axp_ref, last) * mp, f(acn_ref, first) * f(axn_ref, first) * mn)
        cv = w_ref[0:1, :] * prev + w_ref[1:2, :] * u + w_ref[2:3, :] * nxt
        az = f(az_ref)
        cv_ref[...] = cv.astype(BF16)
        ya_ref[...] = (f(ab_ref) * cv * (az * _sigmoid(az))).astype(BF16)

    acp, acn = _halo_specs(D, nt, OFF["a_c"] // D, P_HALO)
    axp, axn = _halo_specs(D, nt, OFF["a_x"] // D, P_HALO)
    row = pl.BlockSpec((TM, D), lambda i: (i, 0))
    return pl.pallas_call(
        body, name=name, grid=(nt,),
        in_specs=[_pcol("a_b", D), _pcol("a_c", D), _pcol("a_x", D), _pcol("a_z", D), acp, axp, acn, axn,
                  pl.BlockSpec((8, D), lambda i: (0, 0))],
        out_specs=(row, row),
        out_shape=(jax.ShapeDtypeStruct((t, D), BF16), jax.ShapeDtypeStruct((t, D), BF16)),
        compiler_params=_cparams(1, 40),
    )(p, p, p, p, p, p, p, p, conv_w8)


def _dp_spec(key):
    seg, width = DP_BLOCKS[key]
    assert OFF[seg] % width == 0
    blk = OFF[seg] // width
    return pl.BlockSpec((TM, width), lambda i: (i, blk))


def _conv_bwd_a(dya, p, cv, dp, name):
    t = p.shape[0]

    def body(dy_ref, ab_ref, az_ref, cv_ref, _, dcv_ref, dp_ref):
        dy, ab = dy_ref[...].astype(F32), ab_ref[...].astype(F32)
        az, c = az_ref[...].astype(F32), cv_ref[...].astype(F32)
        sg = _sigmoid(az)
        sz = az * sg
        dcv_ref[...] = dy * ab * sz
        dp_ref[:, 0:D] = (dy * c * sz).astype(BF16)
        dp_ref[:, D:2 * D] = (dy * ab * c * (sg * (1.0 + az * (1.0 - sg)))).astype(BF16)

    row = pl.BlockSpec((TM, D), lambda i: (i, 0))
    return pl.pallas_call(
        body, name=name, grid=(t // TM,),
        in_specs=[row, _pcol("a_b", D), _pcol("a_z", D), row, _HBM], out_specs=(row, _dp_spec("conv_a")),
        out_shape=(jax.ShapeDtypeStruct((t, D), F32), jax.ShapeDtypeStruct(dp.shape, dp.dtype)),
        input_output_aliases={4: 1}, compiler_params=_cparams(1, 40),
    )(dya, p, p, cv, dp)


def _conv_bwd_b(dcv, p, conv_w8, nct, dp, name):
    t = p.shape[0]
    nt = t // TM

    def body(dcv_ref, hp_ref, hn_ref, ac_ref, ax_ref, w_ref, _, dp_ref, dw_ref):
        i = pl.program_id(0)
        d, ac, ax = dcv_ref[...], ac_ref[...].astype(F32), ax_ref[...].astype(F32)
        u = ac * ax
        mp = jnp.where((i == 0) | (i == nct), 0.0, 1.0)
        mn = jnp.where((i == nct - 1) | (i == nt - 1), 0.0, 1.0)
        dprev, dnxt = _shift_rows(d, hp_ref[7:8, :] * mp, hn_ref[0:1, :] * mn)
        du = w_ref[0:1, :] * dnxt + w_ref[1:2, :] * d + w_ref[2:3, :] * dprev
        dp_ref[:, 0:D] = (du * ax).astype(BF16)
        dp_ref[:, D:2 * D] = (du * ac).astype(BF16)

        @pl.when(i == 0)
        def _():
            dw_ref[...] = jnp.zeros_like(dw_ref)

        dw0 = jnp.sum(u * dnxt, axis=0, keepdims=True)
        dw1 = jnp.sum(u * d, axis=0, keepdims=True)
        dw2 = jnp.sum(u * dprev, axis=0, keepdims=True)
        r8 = lax.broadcasted_iota(jnp.int32, (8, D), 0)
        dw_ref[...] += jnp.where(r8 == 0, dw0, jnp.where(r8 == 1, dw1, jnp.where(r8 == 2, dw2, 0.0)))

    hp, hn = _halo_specs(D, nt)
    row = pl.BlockSpec((TM, D), lambda i: (i, 0))
    return pl.pallas_call(
        body, name=name, grid=(nt,),
        in_specs=[row, hp, hn, _pcol("a_c", D), _pcol("a_x", D), pl.BlockSpec((8, D), lambda i: (0, 0)), _HBM],
        out_specs=(_dp_spec("conv_b"), pl.BlockSpec((8, D), lambda i: (0, 0))),
        out_shape=(jax.ShapeDtypeStruct(dp.shape, dp.dtype), jax.ShapeDtypeStruct((8, D), F32)),
        input_output_aliases={6: 0}, compiler_params=_cparams(1, 40),
    )(dcv, dcv, dcv, p, p, conv_w8, dp)


def _rot_half(x):
    lane = lax.broadcasted_iota(jnp.int32, x.shape, 1)
    return jnp.where((lane % 64) < 32, pltpu.roll(x, 96, 1), pltpu.roll(x, 32, 1))


def _qk_prep_fwd(p, qg, kg, cos_t, sin_t, name):
    t = p.shape[0]

    def body(q_ref, k_ref, qg_ref, kg_ref, c_ref, s_ref, qo_ref, ko_ref):
        c, s = c_ref[...], s_ref[...]

        def one(xv, g, scale):
            y = xv * lax.rsqrt(jnp.mean(xv * xv, axis=-1, keepdims=True) + EPS) * g
            return ((y * c + _rot_half(y) * s) * scale).astype(BF16)

        for h in range(NH):
            qo_ref[:, h * HD:(h + 1) * HD] = one(q_ref[:, h * HD:(h + 1) * HD].astype(F32), qg_ref[...], Q_FOLD)
        for h in range(NKV):
            ko_ref[:, h * HD:(h + 1) * HD] = one(k_ref[:, h * HD:(h + 1) * HD].astype(F32), kg_ref[...], 1.0)

    vec = pl.BlockSpec((1, HD), lambda i: (0, 0))
    tab = pl.BlockSpec((TM, HD), lambda i: (i, 0))
    return pl.pallas_call(
        body, name=name, grid=(t // TM,),
        in_specs=[_pcol("q", NH * HD), _pcol("k", NKV * HD), vec, vec, tab, tab],
        out_specs=(pl.BlockSpec((TM, NH * HD), lambda i: (i, 0)), pl.BlockSpec((TM, NKV * HD), lambda i: (i, 0))),
        out_shape=(jax.ShapeDtypeStruct((t, NH * HD), BF16), jax.ShapeDtypeStruct((t, NKV * HD), BF16)),
        compiler_params=_cparams(1, 32),
    )(p, p, qg, kg, cos_t, sin_t)


def _qk_prep_bwd(dqr, dkr, p, qg, kg, cos_t, sin_t, dp, name):
    t = p.shape[0]

    def body(dq_ref, dk_ref, q_ref, k_ref, qg_ref, kg_ref, c_ref, s_ref, _, dqo_ref, dko_ref, dqg_ref, dkg_ref):
        i = pl.program_id(0)
        c, s = c_ref[...], s_ref[...]

        @pl.when(i == 0)
        def _():
            dqg_ref[...] = jnp.zeros_like(dqg_ref)
            dkg_ref[...] = jnp.zeros_like(dkg_ref)

        def one(dyr, xv, g):
            dy = dyr * c + _rot_half(dyr * s)
            r = lax.rsqrt(jnp.mean(xv * xv, axis=-1, keepdims=True) + EPS)
            xh = xv * r
            dxh = dy * g
            dx = r * (dxh - xh * jnp.mean(dxh * xh, axis=-1, keepdims=True))
            return dx.astype(BF16), jnp.sum(dy * xh, axis=0, keepdims=True)

        for h in range(NH):
            sl = slice(h * HD, (h + 1) * HD)
            dx, dg = one(dq_ref[:, sl] * ATTN_SCALE, q_ref[:, sl].astype(F32), qg_ref[...])
            dqo_ref[:, sl] = dx
            dqg_ref[...] += dg
        for h in range(NKV):
            sl = slice(h * HD, (h + 1) * HD)
            dx, dg = one(dk_ref[:, sl] * (ATTN_SCALE / Q_FOLD), k_ref[:, sl].astype(F32), kg_ref[...])
            dko_ref[:, sl] = dx
            dkg_ref[...] += dg

    vec = pl.BlockSpec((1, HD), lambda i: (0, 0))
    tab = pl.BlockSpec((TM, HD), lambda i: (i, 0))
    acc = pl.BlockSpec((8, HD), lambda i: (0, 0))
    qrow = pl.BlockSpec((TM, NH * HD), lambda i: (i, 0))
    krow = pl.BlockSpec((TM, NKV * HD), lambda i: (i, 0))
    return pl.pallas_call(
        body, name=name, grid=(t // TM,),
        in_specs=[qrow, krow, _pcol("q", NH * HD), _pcol("k", NKV * HD), vec, vec, tab, tab, _HBM],
        out_specs=(_dp_spec("q"), krow, acc, acc),
        out_shape=(jax.ShapeDtypeStruct(dp.shape, dp.dtype), jax.ShapeDtypeStruct((t, NKV * HD), BF16),
                   jax.ShapeDtypeStruct((8, HD), F32), jax.ShapeDtypeStruct((8, HD), F32)),
        input_output_aliases={8: 0}, compiler_params=_cparams(1, 32),
    )(dqr, dkr, p, p, qg, kg, cos_t, sin_t, dp)


def _key_chunks(n):
    c = max(c for c in range(LANE, min(n, ATTN_KEY_CHUNK) + 1, LANE) if n % c == 0)
    return [(lo, lo + c) for lo in range(0, n, c)]


def _attn_fwd(qr, kr, p, nct, name, rider=None):
    t = qr.shape[0]
    nt = t // TM
    ctx = nct * TM
    vblk = OFF["v"] // HD
    hps = ATTN_HEADS_PER_STEP
    nhp, per_kv = NH // hps, GROUP // hps

    def body(q_ref, k_ref, v_ref, o_ref, lse_ref):
        def tile(nkeys):
            for j in range(hps):
                sl = slice(j * HD, (j + 1) * HD)
                q = q_ref[:, sl]
                m = l = acc = None
                for lo, hi in _key_chunks(nkeys):
                    s = lax.dot_general(q, k_ref[lo:hi, :], _NT, preferred_element_type=F32)
                    mc = jnp.max(s, axis=-1, keepdims=True)
                    m_new = mc if m is None else jnp.maximum(m, mc)
                    e = jnp.exp2(s - m_new)
                    lc = jnp.sum(e, axis=-1, keepdims=True)
                    pv = jnp.dot(e.astype(BF16), v_ref[lo:hi, :].astype(BF16), preferred_element_type=F32)
                    if m is None:
                        l, acc = lc, pv
                    else:
                        alpha = jnp.exp2(m - m_new)
                        l, acc = l * alpha + lc, acc * alpha + pv
                    m = m_new
                o_ref[:, sl] = (acc / l).astype(BF16)
                lse_ref[:, j:j + 1] = m + jnp.log2(l)

        pl.when(pl.program_id(1) < nct)(lambda: tile(ctx))
        pl.when(pl.program_id(1) >= nct)(lambda: tile(t))

    def at(h, i):
        return lambda: (pl.program_id(0) == h) & (pl.program_id(1) == i)

    rn = 0 if rider is None else rider.n
    qspec = pl.BlockSpec((TM, hps * HD), lambda h, i: (i, h))
    return pl.pallas_call(
        _with_rider(body, 3, 2, rider, at(0, 0), at(nhp * 7 // 8, 0), at(nhp - 1, nt - 1)),
        name=name, grid=(nhp, nt),
        in_specs=[qspec, pl.BlockSpec((t, HD), lambda h, i: (0, h // per_kv)),
                  pl.BlockSpec((t, HD), lambda h, i: (0, vblk + h // per_kv))] + [_HBM] * rn,
        out_specs=(qspec, pl.BlockSpec((None, TM, hps), lambda h, i: (h, i, 0))) + (_HBM,) * rn,
        out_shape=(jax.ShapeDtypeStruct((t, NH * HD), BF16), jax.ShapeDtypeStruct((nhp, t, hps), F32))
        + (() if rider is None else tuple(rider.out_shape)),
        scratch_shapes=[] if rider is None else rider.scratch,
        compiler_params=_cparams(2, 48),
    )(qr, kr, p, *(() if rider is None else rider.xs))


def _attn_bwd(qr, kr, p, o, lse, do, nct, name, rider=None):
    t = qr.shape[0]
    nt = t // TM
    ctx = nct * TM
    vblk = OFF["v"] // HD
    hps = ATTN_HEADS_PER_STEP

    def body(q_ref, k_ref, v_ref, o_ref, lse_ref, do_ref, dq_ref, dk_ref, dv_ref):
        g, i = pl.program_id(1), pl.program_id(2)

        @pl.when((g == 0) & (i == 0))
        def _():
            dk_ref[...] = jnp.zeros_like(dk_ref)
            dv_ref[...] = jnp.zeros_like(dv_ref)

        def tile(nkeys):
            heads = []
            for j in range(hps):
                sl = slice(j * HD, (j + 1) * HD)
                dob = do_ref[:, sl]
                drow = jnp.sum(dob.astype(F32) * o_ref[:, sl].astype(F32), axis=-1, keepdims=True)
                heads.append((sl, q_ref[:, sl], dob, drow, lse_ref[:, j:j + 1]))
            dq = [None] * hps
            for lo, hi in _key_chunks(nkeys):
                k = k_ref[lo:hi, :]
                vb = v_ref[lo:hi, :].astype(BF16)
                dk_c = dv_c = None
                for j, (sl, q, dob, drow, lse_j) in enumerate(heads):
                    s = lax.dot_general(q, k, _NT, preferred_element_type=F32)
                    pr = jnp.exp2(s - lse_j)
                    dp = lax.dot_general(dob, vb, _NT, preferred_element_type=F32)
                    ds = (pr * (dp - drow)).astype(BF16)
                    dq_c = jnp.dot(ds, k, preferred_element_type=F32)
                    dq[j] = dq_c if dq[j] is None else dq[j] + dq_c
                    dk_j = lax.dot_general(ds, q, _TN, preferred_element_type=F32)
                    dv_j = lax.dot_general(pr.astype(BF16), dob, _TN, preferred_element_type=F32)
                    dk_c = dk_j if dk_c is None else dk_c + dk_j
                    dv_c = dv_j if dv_c is None else dv_c + dv_j
                dk_ref[lo:hi, :] += dk_c
                dv_ref[lo:hi, :] += dv_c
            for j, (sl, *_) in enumerate(heads):
                dq_ref[:, sl] = dq[j]

        pl.when(i < nct)(lambda: tile(ctx))
        pl.when(i >= nct)(lambda: tile(t))

    def at(kv, g, i):
        return lambda: (pl.program_id(0) == kv) & (pl.program_id(1) == g) & (pl.program_id(2) == i)

    rn = 0 if rider is None else rider.n
    per_kv = GROUP // hps
    qspec = pl.BlockSpec((TM, hps * HD), lambda kv, g, i: (i, kv * per_kv + g))
    kvspec = pl.BlockSpec((t, HD), lambda kv, g, i: (0, kv))
    lspec = pl.BlockSpec((None, TM, hps), lambda kv, g, i: (kv * per_kv + g, i, 0))
    return pl.pallas_call(
        _with_rider(body, 6, 3, rider, at(0, 0, 0), at(NKV - 1, 0, 0), at(NKV - 1, per_kv - 1, nt - 1)),
        name=name, grid=(NKV, per_kv, nt),
        in_specs=[qspec, kvspec, pl.BlockSpec((t, HD), lambda kv, g, i: (0, vblk + kv)), qspec, lspec, qspec]
        + [_HBM] * rn,
        out_specs=(qspec, kvspec, kvspec) + (_HBM,) * rn,
        out_shape=(jax.ShapeDtypeStruct((t, NH * HD), F32), jax.ShapeDtypeStruct((t, NKV * HD), F32),
                   jax.ShapeDtypeStruct((t, NKV * HD), F32)) + (() if rider is None else tuple(rider.out_shape)),
        scratch_shapes=[] if rider is None else rider.scratch,
        compiler_params=_cparams(3, 48),
    )(qr, kr, p, o, lse, do, *(() if rider is None else rider.xs))


def _decay_fwd(p, wd, bd, name):
    t = p.shape[0]

    def body(r_ref, w_ref, b_ref, z_ref, la_ref):
        z = jnp.dot(r_ref[...].astype(BF16), w_ref[...].astype(BF16), preferred_element_type=F32) + b_ref[...]
        z_ref[...] = z
        la_ref[...] = (jnp.minimum(z, 0.0) - jnp.log(1.0 + jnp.exp(-jnp.abs(z)))) / GLA_TAU

    row = pl.BlockSpec((TM, D), lambda i: (i, 0))
    return pl.pallas_call(
        body, name=name, grid=(t // TM,),
        in_specs=[_pcol("r", R_PAD), pl.BlockSpec((R_PAD, D), lambda i: (0, 0)), pl.BlockSpec((1, D), lambda i: (0, 0))],
        out_specs=(row, row),
        out_shape=(jax.ShapeDtypeStruct((t, D), F32), jax.ShapeDtypeStruct((t, D), F32)),
        compiler_params=_cparams(1, 32),
    )(p, wd, bd)


def _chunk_order(s, ncc, nc, rev):
    if not rev:
        return s
    return jnp.where(s < ncc, ncc - 1 - s, nc - 1 - (s - ncc))


def _gla_chains(dirs):
    return [(rev, d, h) + tuple(refs) for d, (rev, *refs) in enumerate(dirs) for h in range(GH)]


def _hk(h):
    return slice(h * GDK, (h + 1) * GDK)


def _hv(h):
    return slice(h * GDV, (h + 1) * GDV)


def _gla_factors(qs, ks, las, revs):
    r = lax.broadcasted_iota(jnp.int32, (CH, CH), 0)
    c = lax.broadcasted_iota(jnp.int32, (CH, CH), 1)
    keeps = [(c >= r) if rev else (c <= r) for rev in revs]
    bcs = [jnp.dot(keep.astype(F32), la, preferred_element_type=F32, precision=HIGHEST) for keep, la in zip(keeps, las)]
    bls = [jnp.sum(la, axis=0, keepdims=True) for la in las]
    qs, ks = [q.astype(F32) for q in qs], [k.astype(F32) for k in ks]
    qts = [q * GLA_SCALE * jnp.exp(bc) for q, bc in zip(qs, bcs)]
    kts = [k * jnp.exp(-bc) for k, bc in zip(ks, bcs)]
    khs = [k * jnp.exp(bl - bc) for k, bl, bc in zip(ks, bls, bcs)]
    gls = [jnp.exp(bl) for bl in bls]
    return qts, kts, gls, khs, keeps, bcs


_NT = (((1,), (1,)), ((), ()))
_TN = (((0,), (0,)), ((), ()))


def _gla_specs(ncc, nc, rev, backward):
    def idx(s):
        return _chunk_order((nc - 1 - s) if backward else s, ncc, nc, rev)

    wk, wv = GH * GDK, GH * GDV
    qb, kb, vb = OFF["gq"] // wk, OFF["gk"] // wk, OFF["gv"] // wv
    lab = 1 if rev else 0
    q = pl.BlockSpec((CH, wk), lambda s: (idx(s), qb))
    k = pl.BlockSpec((CH, wk), lambda s: (idx(s), kb))
    v = pl.BlockSpec((CH, wv), lambda s: (idx(s), vb))
    la = pl.BlockSpec((CH, wk), lambda s: (idx(s), lab))
    o = pl.BlockSpec((CH, wv), lambda s: (idx(s), 0))
    dk = pl.BlockSpec((CH, wk), lambda s: (idx(s), 0))
    st = pl.BlockSpec((None, GH, GDV, GDK), lambda s: (idx(s), 0, 0, 0))
    return q, k, v, la, o, dk, st


def _gla_fwd(p, la, ncc, name):
    t = p.shape[0]
    nc = t // CH
    specs = [_gla_specs(ncc, nc, rev, False) for rev in (False, True)]

    def body(qf, kf, vf, laf, qb_, kb_, vb_, lab, of, stf, ob, stb, s_scr):
        @pl.when(pl.program_id(0) == 0)
        def _():
            s_scr[...] = jnp.zeros_like(s_scr)

        ch = _gla_chains(((False, qf, kf, vf, laf, of, stf), (True, qb_, kb_, vb_, lab, ob, stb)))
        qts, kts, gls, khs, keeps, _ = _gla_factors([c[3][:, _hk(c[2])] for c in ch], [c[4][:, _hk(c[2])] for c in ch],
                                                    [c[6][:, _hk(c[2])] for c in ch], [c[0] for c in ch])
        sts = [s_scr[c[1], c[2]] for c in ch]
        for c, st in zip(ch, sts):
            c[8][c[2]] = st
        vbs = [c[5][:, _hv(c[2])].astype(BF16) for c in ch]
        qbs = [qt.astype(BF16) for qt in qts]
        a_s = [jnp.where(keep, lax.dot_general(qb, kt.astype(BF16), _NT, preferred_element_type=F32), 0.0)
               for keep, qb, kt in zip(keeps, qbs, kts)]
        inter = [lax.dot_general(qb, st.astype(BF16), _NT, preferred_element_type=F32) for qb, st in zip(qbs, sts)]
        intra = [jnp.dot(a.astype(BF16), vb, preferred_element_type=F32) for a, vb in zip(a_s, vbs)]
        for c, x, y in zip(ch, inter, intra):
            c[7][:, _hv(c[2])] = (x + y).astype(BF16)
        upd = [lax.dot_general(vb, kh.astype(BF16), _TN, preferred_element_type=F32) for vb, kh in zip(vbs, khs)]
        for c, st, gl, u in zip(ch, sts, gls, upd):
            s_scr[c[1], c[2]] = st * gl + u

    o_shape = jax.ShapeDtypeStruct((t, GH * GDV), BF16)
    st_shape = jax.ShapeDtypeStruct((nc, GH, GDV, GDK), F32)
    return pl.pallas_call(
        body, name=name, grid=(nc,),
        in_specs=[sp for s_ in specs for sp in s_[:4]],
        out_specs=tuple(sp for s_ in specs for sp in (s_[4], s_[6])),
        out_shape=(o_shape, st_shape, o_shape, st_shape),
        scratch_shapes=[pltpu.VMEM((2, GH, GDV, GDK), F32)], compiler_params=_cparams(1, 32),
    )(p, p, p, la, p, p, p, la)


def _gla_bwd(p, la, do, stf, stb, ncc, name):
    t = p.shape[0]
    nc = t // CH
    specs = [_gla_specs(ncc, nc, rev, True) for rev in (False, True)]

    def mm(xs, ys, dims=None):
        if dims is None:
            return [jnp.dot(x, y, preferred_element_type=F32) for x, y in zip(xs, ys)]
        return [lax.dot_general(x, y, dims, preferred_element_type=F32) for x, y in zip(xs, ys)]

    def body(*refs):
        ins_f, ins_b, outs_f, outs_b, ds_scr = refs[0:6], refs[6:12], refs[12:16], refs[16:20], refs[20]

        @pl.when(pl.program_id(0) == 0)
        def _():
            ds_scr[...] = jnp.zeros_like(ds_scr)

        ch = _gla_chains(((False, *ins_f, *outs_f), (True, *ins_b, *outs_b)))
        revs = [c[0] for c in ch]
        qts, kts, gls, khs, keeps, bcs = _gla_factors([c[3][:, _hk(c[2])] for c in ch], [c[4][:, _hk(c[2])] for c in ch],
                                                      [c[6][:, _hk(c[2])] for c in ch], revs)
        stvs = [c[8][c[2]].astype(BF16) for c in ch]
        dsns = [ds_scr[c[1], c[2]] for c in ch]
        dsbs = [x.astype(BF16) for x in dsns]
        vbs = [c[5][:, _hv(c[2])].astype(BF16) for c in ch]
        dobs = [c[7][:, _hv(c[2])].astype(BF16) for c in ch]
        qbs, kbs = [x.astype(BF16) for x in qts], [x.astype(BF16) for x in kts]
        a_s = [jnp.where(keep, x, 0.0).astype(BF16) for keep, x in zip(keeps, mm(qbs, kbs, _NT))]
        das = [jnp.where(keep, x, 0.0).astype(BF16) for keep, x in zip(keeps, mm(dobs, vbs, _NT))]
        dqts = [x + y for x, y in zip(mm(dobs, stvs), mm(das, kbs))]
        dkhs = mm(vbs, dsbs)
        dkts = [x + dkh * gl for x, dkh, gl in zip(mm(das, qbs, _TN), dkhs, gls)]
        for c, x, y in zip(ch, mm(a_s, dobs, _TN), mm([kh.astype(BF16) for kh in khs], dsbs, _NT)):
            c[11][:, _hv(c[2])] = x + y
        for c, x, dsn, gl in zip(ch, mm(dobs, qbs, _TN), dsns, gls):
            ds_scr[c[1], c[2]] = x + dsn * gl
        dgls = [jnp.sum(c[8][c[2]] * dsn, axis=0, keepdims=True) + jnp.sum(dkh * kt, axis=0, keepdims=True)
                for c, dsn, dkh, kt in zip(ch, dsns, dkhs, kts)]
        row = lax.broadcasted_iota(jnp.int32, (CH, GDK), 0)
        dbcs = [dqt * qt - dkt * kt + jnp.where(row == (0 if rev else CH - 1), dgl * gl, 0.0)
                for rev, dqt, qt, dkt, kt, dgl, gl in zip(revs, dqts, qts, dkts, kts, dgls, gls)]
        r = lax.broadcasted_iota(jnp.int32, (CH, CH), 0)
        c_ = lax.broadcasted_iota(jnp.int32, (CH, CH), 1)
        dlas = [jnp.dot(((c_ <= r) if rev else (c_ >= r)).astype(F32), dbc, preferred_element_type=F32, precision=HIGHEST)
                for rev, dbc in zip(revs, dbcs)]
        for c, dla, dqt, dkt, bc in zip(ch, dlas, dqts, dkts, bcs):
            c[12][:, _hk(c[2])] = dla
            c[9][:, _hk(c[2])] = dqt * (GLA_SCALE * jnp.exp(bc))
            c[10][:, _hk(c[2])] = dkt * jnp.exp(-bc)

    k_shape = jax.ShapeDtypeStruct((t, GH * GDK), F32)
    v_shape = jax.ShapeDtypeStruct((t, GH * GDV), F32)
    res = pl.pallas_call(
        body, name=name, grid=(nc,),
        in_specs=[sp for q_s, k_s, v_s, la_s, o_s, _, st_s in specs for sp in (q_s, k_s, v_s, la_s, o_s, st_s)],
        out_specs=tuple(sp for _, _, _, _, o_s, dk_s, _ in specs for sp in (dk_s, dk_s, o_s, dk_s)),
        out_shape=(k_shape, k_shape, v_shape, k_shape) * 2,
        scratch_shapes=[pltpu.VMEM((2, GH, GDV, GDK), F32)], compiler_params=_cparams(1, 32),
    )(p, p, p, la, do, stf, p, p, p, la, do, stb)
    return res[:4], res[4:]


def _gla_merge_bwd(gf, gb, z, p, wd, dp, name):
    t = p.shape[0]
    w2 = GH * GDK

    def body(dqf, dkf, dvf, dlf, dqb, dkb, dvb, dlb, z_ref, r_ref, w_ref, _, dp_ref, dr_ref, db_ref, dw_ref):
        i = pl.program_id(0)
        dp_ref[:, 0:D] = (dvf[...] + dvb[...]).astype(BF16)
        dp_ref[:, D:D + w2] = (dqf[...] + dqb[...]).astype(BF16)
        dp_ref[:, D + w2:D + 2 * w2] = (dkf[...] + dkb[...]).astype(BF16)
        zv = z_ref[...]
        dz = jnp.concatenate([dlf[...], dlb[...]], axis=1) * (_sigmoid(-zv) / GLA_TAU)
        dzb = dz.astype(BF16)
        dr_ref[...] = lax.dot_general(dzb, w_ref[...].astype(BF16), _NT, preferred_element_type=F32).astype(BF16)

        @pl.when(i == 0)
        def _():
            db_ref[...] = jnp.zeros_like(db_ref)
            dw_ref[...] = jnp.zeros_like(dw_ref)

        db_ref[...] += jnp.sum(dz, axis=0, keepdims=True)
        dw_ref[...] += lax.dot_general(r_ref[...].astype(BF16), dzb, _TN, preferred_element_type=F32)

    half = pl.BlockSpec((TM, w2), lambda i: (i, 0))
    row = pl.BlockSpec((TM, D), lambda i: (i, 0))
    wspec = pl.BlockSpec((R_PAD, D), lambda i: (0, 0))
    return pl.pallas_call(
        body, name=name, grid=(t // TM,),
        in_specs=[half, half, row, half, half, half, row, half, row, _pcol("r", R_PAD), wspec, _HBM],
        out_specs=(_dp_spec("gla"), pl.BlockSpec((TM, R_PAD), lambda i: (i, 0)),
                   pl.BlockSpec((8, D), lambda i: (0, 0)), wspec),
        out_shape=(jax.ShapeDtypeStruct(dp.shape, dp.dtype), jax.ShapeDtypeStruct((t, R_PAD), BF16),
                   jax.ShapeDtypeStruct((8, D), F32), jax.ShapeDtypeStruct((R_PAD, D), F32)),
        input_output_aliases={11: 0}, compiler_params=_cparams(1, 40),
    )(*gf, *gb, z, p, wd, dp)


def _dp_tail(dk, dv, dr, dp, name):
    t = dk.shape[0]
    wk = NKV * HD

    def body(dk_ref, dv_ref, dr_ref, _, dp_ref):
        dp_ref[:, 0:wk] = dk_ref[...]
        dp_ref[:, wk:2 * wk] = dv_ref[...].astype(BF16)
        dp_ref[:, 2 * wk:2 * wk + R_PAD] = dr_ref[...]
        dp_ref[:, 2 * wk + R_PAD:] = jnp.zeros((TM, DP_BLOCKS["tail"][1] - 2 * wk - R_PAD), BF16)

    kv = pl.BlockSpec((TM, wk), lambda i: (i, 0))
    return pl.pallas_call(
        body, name=name, grid=(t // TM,),
        in_specs=[kv, kv, pl.BlockSpec((TM, R_PAD), lambda i: (i, 0)), _HBM], out_specs=_dp_spec("tail"),
        out_shape=jax.ShapeDtypeStruct(dp.shape, dp.dtype), input_output_aliases={3: 0},
        compiler_params=_cparams(1, 32),
    )(dk, dv, dr, dp)


def _branch_fwd(att, of, ob, p, gla_g, name):
    t = p.shape[0]

    def body(att_ref, of_ref, ob_ref, za_ref, zg_ref, g_ref, yb_ref, yc_ref):
        za = za_ref[...].astype(F32)
        yb_ref[...] = (att_ref[...].astype(F32) * (za * _sigmoid(za))).astype(BF16)
        for h in range(GH):
            sl = slice(h * GDV, (h + 1) * GDV)
            o = of_ref[:, sl].astype(F32) + ob_ref[:, sl].astype(F32)
            n = o * lax.rsqrt(jnp.mean(o * o, axis=-1, keepdims=True) + EPS) * g_ref[...]
            zh = zg_ref[:, sl].astype(F32)
            yc_ref[:, sl] = (n * (zh * _sigmoid(zh))).astype(BF16)

    row = pl.BlockSpec((TM, D), lambda i: (i, 0))
    return pl.pallas_call(
        body, name=name, grid=(t // TM,),
        in_specs=[row, row, row, _pcol("z_attn", D), _pcol("zg", D), pl.BlockSpec((1, GDV), lambda i: (0, 0))],
        out_specs=(row, row),
        out_shape=(jax.ShapeDtypeStruct((t, D), BF16), jax.ShapeDtypeStruct((t, D), BF16)),
        compiler_params=_cparams(1, 40),
    )(att, of, ob, p, p, gla_g)


def _branch_bwd(dyb, dyc, att, of, ob, p, gla_g, dp, name):
    t = p.shape[0]

    def body(dyb_ref, dyc_ref, att_ref, of_ref, ob_ref, za_ref, zg_ref, g_ref, _, datt_ref, do_ref, dp_ref, dg_ref):
        i = pl.program_id(0)

        @pl.when(i == 0)
        def _():
            dg_ref[...] = jnp.zeros_like(dg_ref)

        za, dyb = za_ref[...].astype(F32), dyb_ref[...].astype(F32)
        sa = _sigmoid(za)
        datt_ref[...] = (dyb * (za * sa)).astype(BF16)
        dp_ref[:, 0:D] = (dyb * att_ref[...].astype(F32) * (sa * (1.0 + za * (1.0 - sa)))).astype(BF16)
        g = g_ref[...]
        for h in range(GH):
            sl = slice(h * GDV, (h + 1) * GDV)
            o = of_ref[:, sl].astype(F32) + ob_ref[:, sl].astype(F32)
            r = lax.rsqrt(jnp.mean(o * o, axis=-1, keepdims=True) + EPS)
            oh = o * r
            zh, dyc = zg_ref[:, sl].astype(F32), dyc_ref[:, sl].astype(F32)
            sg = _sigmoid(zh)
            dn = dyc * (zh * sg)
            dp_ref[:, D + h * GDV:D + (h + 1) * GDV] = (dyc * (oh * g) * (sg * (1.0 + zh * (1.0 - sg)))).astype(BF16)
            doh = dn * g
            do_ref[:, sl] = (r * (doh - oh * jnp.mean(doh * oh, axis=-1, keepdims=True))).astype(BF16)
            dg_ref[...] += jnp.sum(dn * oh, axis=0, keepdims=True)

    row = pl.BlockSpec((TM, D), lambda i: (i, 0))
    return pl.pallas_call(
        body, name=name, grid=(t // TM,),
        in_specs=[row, row, row, row, row, _pcol("z_attn", D), _pcol("zg", D), pl.BlockSpec((1, GDV), lambda i: (0, 0)),
                  _HBM],
        out_specs=(row, row, _dp_spec("branch"), pl.BlockSpec((8, GDV), lambda i: (0, 0))),
        out_shape=(jax.ShapeDtypeStruct((t, D), BF16), jax.ShapeDtypeStruct((t, D), BF16),
                   jax.ShapeDtypeStruct(dp.shape, dp.dtype), jax.ShapeDtypeStruct((8, GDV), F32)),
        input_output_aliases={8: 2}, compiler_params=_cparams(1, 48),
    )(dyb, dyc, att, of, ob, p, p, gla_g, dp)


def _merge_fwd(bra, brb, brc, p, b_gate, name):
    t = p.shape[0]
    mgb = OFF["mg"] // D

    def body(a_ref, b_ref, c_ref, ga_ref, gb_ref, gc_ref, bg_ref, m_ref):
        m_ref[...] = (_sigmoid(ga_ref[...].astype(F32) + bg_ref[:, 0:D]) * a_ref[...].astype(F32)
                      + _sigmoid(gb_ref[...].astype(F32) + bg_ref[:, D:2 * D]) * b_ref[...].astype(F32)
                      + _sigmoid(gc_ref[...].astype(F32) + bg_ref[:, 2 * D:3 * D]) * c_ref[...].astype(F32)).astype(BF16)

    row = pl.BlockSpec((TM, D), lambda i: (i, 0))
    gates = [pl.BlockSpec((TM, D), functools.partial(lambda i, b: (i, b), b=mgb + j)) for j in range(3)]
    return pl.pallas_call(
        body, name=name, grid=(t // TM,),
        in_specs=[row, row, row, *gates, pl.BlockSpec((1, 3 * D), lambda i: (0, 0))],
        out_specs=row, out_shape=jax.ShapeDtypeStruct((t, D), BF16), compiler_params=_cparams(1, 40),
    )(bra, brb, brc, p, p, p, b_gate)


def _merge_bwd(dm, bra, brb, brc, p, b_gate, name):
    t = p.shape[0]
    mgb = OFF["mg"] // D

    def body(dm_ref, a_ref, b_ref, c_ref, ga_ref, gb_ref, gc_ref, bg_ref, da_ref, db_ref, dc_ref, dmg_ref, dbg_ref):
        i = pl.program_id(0)

        @pl.when(i == 0)
        def _():
            dbg_ref[...] = jnp.zeros_like(dbg_ref)

        dm = dm_ref[...].astype(F32)
        for j, (br_ref, g_ref, d_ref) in enumerate(((a_ref, ga_ref, da_ref), (b_ref, gb_ref, db_ref), (c_ref, gc_ref, dc_ref))):
            sl = slice(j * D, (j + 1) * D)
            g = _sigmoid(g_ref[...].astype(F32) + bg_ref[:, sl])
            d_ref[...] = (dm * g).astype(BF16)
            dmg = dm * br_ref[...].astype(F32) * (g * (1.0 - g))
            dmg_ref[:, sl] = dmg.astype(BF16)
            dbg_ref[:, sl] += jnp.sum(dmg, axis=0, keepdims=True)

    row = pl.BlockSpec((TM, D), lambda i: (i, 0))
    gates = [pl.BlockSpec((TM, D), functools.partial(lambda i, b: (i, b), b=mgb + j)) for j in range(3)]
    return pl.pallas_call(
        body, name=name, grid=(t // TM,),
        in_specs=[row, row, row, row, *gates, pl.BlockSpec((1, 3 * D), lambda i: (0, 0))],
        out_specs=(row, row, row, _dp_spec("merge"), pl.BlockSpec((8, 3 * D), lambda i: (0, 0))),
        out_shape=(jax.ShapeDtypeStruct((t, D), BF16),) * 3 + (jax.ShapeDtypeStruct((t, NP), BF16),
                                                                jax.ShapeDtypeStruct((8, 3 * D), F32)),
        compiler_params=_cparams(1, 48),
    )(dm, bra, brb, brc, p, p, p, b_gate)


def _adam_update(ns, g_ref, w_ref, m_ref, v_ref, go_ref, d_ref, mo_ref, vo_ref):
    g = g_ref[0].astype(F32)
    for s in range(1, ns):
        g = g + g_ref[s].astype(F32)
    mn = ADAM_B1 * m_ref[...] + (1.0 - ADAM_B1) * g
    vn = ADAM_B2 * v_ref[...] + (1.0 - ADAM_B2) * jnp.square(g)
    m_hat = mn / (1.0 - ADAM_B1 ** ADAM_STEP)
    v_hat = vn / (1.0 - ADAM_B2 ** ADAM_STEP)
    go_ref[...] = g
    d_ref[...] = -ADAM_LR * (m_hat / (jnp.sqrt(v_hat) + ADAM_EPS) + ADAM_WD * w_ref[...])
    mo_ref[...] = mn
    vo_ref[...] = vn


def _adamw(gsrc, w, m, v, name):
    ns, nl, r, c = gsrc.shape
    gb = gsrc.dtype.itemsize

    def fits(rows, cols):
        lanes = -(-cols // LANE) * LANE
        return ns * rows * lanes * gb <= ADAM_SRC_BYTES and rows * lanes * 4 <= ADAM_ROW_BYTES

    tr, tc = r, c
    if not fits(r, c):
        rows = [cand for cand in range(16, r, 16) if r % cand == 0 and fits(cand, c)]
        cols = [cand for cand in range(LANE, c, LANE) if c % cand == 0 and fits(r, cand)]
        if rows:
            tr = rows[-1]
        else:
            tc = cols[-1]

    def body(*refs):
        _adam_update(ns, *refs)

    row = pl.BlockSpec((None, tr, tc), lambda l, i, j: (l, i, j))
    return pl.pallas_call(
        body, name=name, grid=(nl, r // tr, c // tc),
        in_specs=[pl.BlockSpec((ns, None, tr, tc), lambda l, i, j: (0, l, i, j)), row, row, row],
        out_specs=(row,) * 4, out_shape=(jax.ShapeDtypeStruct((nl, r, c), F32),) * 4,
        compiler_params=_cparams(3, 48),
    )(gsrc, w, m, v)


def _pair_sum(a, b, name):
    s, r, c = a.shape
    tc = _pick(c, (256, 128))

    def body(a_ref, b_ref, o_ref):
        o_ref[...] = (a_ref[...].astype(F32) + b_ref[...].astype(F32)).astype(BF16)

    blk = pl.BlockSpec((None, r, tc), lambda i, j: (i, 0, j))
    return pl.pallas_call(
        body, name=name, grid=(s, c // tc), in_specs=[blk, blk], out_specs=blk,
        out_shape=jax.ShapeDtypeStruct(a.shape, BF16), compiler_params=_cparams(2, 32),
    )(a, b)


def _adamw_small(items, name):
    k = len(items)

    def body(*refs):
        for j in range(k):
            _adam_update(items[j][0].shape[0], *refs[4 * j:4 * j + 4], *refs[4 * k + 4 * j:4 * k + 4 * j + 4])

    out = pl.pallas_call(
        body, name=name,
        out_shape=tuple(jax.ShapeDtypeStruct(w.shape, F32) for _, w, _, _ in items for _ in range(4)),
    )(*[a for item in items for a in item])
    return [out[4 * j:4 * j + 4] for j in range(k)]


def _rope_tables(ctx, seq):
    n_rows = seq // GRID_W
    pairs = HD // 4
    row = jnp.repeat(jnp.arange(n_rows, dtype=F32), GRID_W)
    col = jnp.tile(jnp.arange(GRID_W, dtype=F32), n_rows)
    freqs = ROPE_THETA ** (-jnp.arange(pairs, dtype=F32) * 2.0 / (HD // 2))
    ar, ac = row[:, None] * freqs, col[:, None] * freqs
    cos_l = jnp.concatenate([jnp.cos(ar), jnp.cos(ar), jnp.cos(ac), jnp.cos(ac)], axis=1)
    sin_l = jnp.concatenate([-jnp.sin(ar), jnp.sin(ar), -jnp.sin(ac), jnp.sin(ac)], axis=1)
    cos_t = jnp.concatenate([jnp.ones((ctx, HD), F32), cos_l], axis=0)
    sin_t = jnp.concatenate([jnp.zeros((ctx, HD), F32), sin_l], axis=0)
    return cos_t, sin_t


def _to_proj_layout(wt):
    parts = [wt[s:s + wd] for _, s, wd in _SEGS]
    used = sum(wd for _, _, wd in _SEGS)
    parts.append(jnp.zeros((NP - used, wt.shape[1]), wt.dtype))
    return jnp.concatenate(parts, axis=0)


def _from_proj_layout(g):
    order = sorted(_SEGS, key=lambda sg: sg[1])
    return jnp.concatenate([g[OFF[n]:OFF[n] + wd] for n, _, wd in order], axis=0)


def _row0(a):
    return a[..., 0, :]


def kernel(x, c, ctx, c_ctx, w_ada, b_ada, g_pre, g_post, w_in, conv_w, q_norm_g, k_norm_g, w_decay_fwd, b_decay_fwd, w_decay_bwd, b_decay_bwd, gla_norm_g, w_br_conv, w_br_attn, w_br_gla, b_gate, w_out, loss_target, m_c_ctx, m_w_ada, m_b_ada, m_g_pre, m_g_post, m_w_in, m_conv_w, m_q_norm_g, m_k_norm_g, m_w_decay_fwd, m_b_decay_fwd, m_w_decay_bwd, m_b_decay_bwd, m_gla_norm_g, m_w_br_conv, m_w_br_attn, m_w_br_gla, m_b_gate, m_w_out, v_c_ctx, v_w_ada, v_b_ada, v_g_pre, v_g_post, v_w_in, v_conv_w, v_q_norm_g, v_k_norm_g, v_w_decay_fwd, v_b_decay_fwd, v_w_decay_bwd, v_b_decay_bwd, v_gla_norm_g, v_w_br_conv, v_w_br_attn, v_w_br_gla, v_b_gate, v_w_out):
    seq, n_ctx = x.shape[1], ctx.shape[1]
    assert n_ctx % TM == 0 and seq % TM == 0 and seq % GRID_W == 0
    t = n_ctx + seq
    nct, ncc = n_ctx // TM, n_ctx // CH
    dev = 4 * lax.axis_index("x") + 2 * lax.axis_index("y") + lax.axis_index("c")
    ada_w = w_ada.shape[2]
    in_w = w_in.shape[2]
    br_r = w_br_conv.shape[1]

    def in_t(a, l):
        return a.transpose(2, 0, 1)[:, l, :]

    wb = [w.astype(BF16) for w in (w_ada, w_br_conv, w_br_attn, w_br_gla, w_out)]
    wall = _all_gather([wb[0][0], in_t(w_in, 0).astype(BF16), conv_w, w_decay_fwd, w_decay_bwd],
                       "gather_first")
    later = _GatherRider([in_t(w_in, 1).astype(BF16), wb[0][1], wb[1], wb[2], wb[3], wb[4]])

    def full_small(g):
        return g.transpose(1, 2, 0, 3).reshape(DEPTH, g.shape[2], NDEV * g.shape[3])

    def full_in(g):
        return _to_proj_layout(g.reshape(IN_WIDTH, D))

    def full_ada(g):
        return g.transpose(1, 0, 2).reshape(D, 3 * D)

    w_ada_f = [full_ada(wall[0]), None]
    wp = [full_in(wall[1]), None]
    conv_f, wdf_f, wdb_f = full_small(wall[2]), full_small(wall[3]), full_small(wall[4])

    cos_t, sin_t = _rope_tables(n_ctx, seq)
    cc = jnp.concatenate([c_ctx[None, :], c.reshape(1, D), jnp.zeros((6, D), F32)], axis=0)
    silu_cc, dsilu_cc = _ada_in(cc)

    conv8, wd_pad, bd = [], [], []
    for l in range(DEPTH):
        conv8.append(jnp.concatenate([conv_f[l], jnp.zeros((5, D), F32)], axis=0))
        zr = jnp.zeros((GLA_RANK, GH * GDK), F32)
        wd_pad.append(jnp.concatenate([jnp.concatenate([wdf_f[l], zr], axis=1), jnp.concatenate([zr, wdb_f[l]], axis=1),
                                       jnp.zeros((R_PAD - 2 * GLA_RANK, D), F32)], axis=0))
        bd.append(jnp.concatenate([b_decay_fwd[l], b_decay_bwd[l]])[None, :])

    xs = jnp.concatenate([ctx[0], x[0]], axis=0)
    saved = []
    for l in range(DEPTH):
        n = f"l{l}_"
        mod = _mm(silu_cc, w_ada_f[l], n + "mod", bias=b_ada[l][None, :])
        mod3 = mod[0:2].reshape(2, 3, D)
        h = _prenorm_fwd(xs, g_pre[l][None, :], mod3, nct, n + "prenorm")
        p = _mm(h, wp[l], n + "proj", tb=True, out_dtype=BF16)
        cv, ya = _conv_fwd(p, conv8[l], nct, n + "conv")
        qr, kr = _qk_prep_fwd(p, q_norm_g[l][None, :], k_norm_g[l][None, :], cos_t, sin_t, n + "qk_prep")
        att, lse, *got = _attn_fwd(qr, kr, p, nct, n + "attn", rider=later if l == 0 else None)
        if l == 0:
            wp[1], w_ada_f[1] = full_in(got[0]), full_ada(got[1])
            w_brs_f = [g.transpose(1, 0, 2, 3).reshape(DEPTH, D, D) for g in got[2:]]
        z, la = _decay_fwd(p, wd_pad[l], bd[l], n + "decay")
        of, stf, ob, stb = _gla_fwd(p, la, ncc, n + "gla")
        yb, yc = _branch_fwd(att, of, ob, p, gla_norm_g[l][None, :], n + "branch")
        bra = _mm(ya, w_brs_f[0][l], n + "br_conv", out_dtype=BF16)
        brb = _mm(yb, w_brs_f[1][l], n + "br_attn", out_dtype=BF16)
        brc = _mm(yc, w_brs_f[2][l], n + "br_gla", out_dtype=BF16)
        mm_ = _merge_fwd(bra, brb, brc, p, b_gate[l][None, :], n + "merge")
        out = _mm(mm_, w_brs_f[3][l], n + "out")
        x_new = _post_fwd(xs, out, g_post[l][None, :], mod3, nct, n + "post")
        saved.append(dict(x=xs, mod3=mod3, h=h, p=p, cv=cv, ya=ya, qr=qr, kr=kr, att=att, lse=lse, z=z, la=la, of=of, ob=ob,
                          stf=stf, stb=stb, yb=yb, yc=yc, bra=bra, brb=brb, brc=brc, m=mm_, out=out))
        xs = x_new

    dx, sq = _loss_grad(xs, loss_target[0], nct, "loss")
    loss = lax.psum(0.5 * sq[0, 0] / D, ("x", "y", "c"))

    gw = {k: [None] * DEPTH for k in ("w_in", "br_conv", "br_attn", "br_gla", "out", "b_gate", "g_pre", "g_post",
                                      "conv_w", "qg", "kg", "wd", "bdec", "gla_g", "dmod")}
    dctx = []

    def in_slots(l):
        return _from_proj_layout(gw["w_in"][l]).reshape(NDEV, in_w, D)

    def br_slots():
        return [jnp.stack([gw[k][l].reshape(NDEV, br_r, D) for l in range(DEPTH)], axis=1)
                for k in ("br_conv", "br_attn", "br_gla", "out")]

    for l in reversed(range(DEPTH)):
        n = f"l{l}_b_"
        s = saved[l]
        p = s["p"]
        d_out, dgt, gw["g_post"][l] = _post_bwd(dx, s["out"], g_post[l][None, :], s["mod3"], nct, n + "post")
        dm = _mm(d_out, w_brs_f[3][l], n + "dm", tb=True, out_dtype=BF16)
        gw["out"][l] = _mm(s["m"], d_out, n + "dw_out", ta=True, out_dtype=BF16)
        dbra, dbrb, dbrc, dp, gw["b_gate"][l] = _merge_bwd(dm, s["bra"], s["brb"], s["brc"], p, b_gate[l][None, :], n + "merge")
        dya = _mm(dbra, w_brs_f[0][l], n + "dya", tb=True, out_dtype=BF16)
        dyb = _mm(dbrb, w_brs_f[1][l], n + "dyb", tb=True, out_dtype=BF16)
        dyc = _mm(dbrc, w_brs_f[2][l], n + "dyc", tb=True, out_dtype=BF16)
        gw["br_conv"][l] = _mm(s["ya"], dbra, n + "dw_conv", ta=True, out_dtype=BF16)
        gw["br_attn"][l] = _mm(s["yb"], dbrb, n + "dw_attn", ta=True, out_dtype=BF16)
        gw["br_gla"][l] = _mm(s["yc"], dbrc, n + "dw_gla", ta=True, out_dtype=BF16)
        dcv, dp = _conv_bwd_a(dya, p, s["cv"], dp, n + "conv_a")
        dp, gw["conv_w"][l] = _conv_bwd_b(dcv, p, conv8[l], nct, dp, n + "conv_b")
        datt, dgo, dp, gw["gla_g"][l] = _branch_bwd(dyb, dyc, s["att"], s["of"], s["ob"], p, gla_norm_g[l][None, :], dp, n + "branch")
        ex1 = _ExchangeRider([in_slots(DEPTH - 1)] + br_slots()) if l == 0 else None
        dqr, dkr, dv, *got = _attn_bwd(s["qr"], s["kr"], p, s["att"], s["lse"], datt, nct, n + "attn", rider=ex1)
        if l == 0:
            recv_in1, recv_br = got[0], got[1:]
        dp, dk, gw["qg"][l], gw["kg"][l] = _qk_prep_bwd(dqr, dkr, p, q_norm_g[l][None, :], k_norm_g[l][None, :], cos_t, sin_t, dp, n + "qk_prep")
        gf, gb = _gla_bwd(p, s["la"], dgo, s["stf"], s["stb"], ncc, n + "gla")
        dp, dr, gw["bdec"][l], gw["wd"][l] = _gla_merge_bwd(gf, gb, s["z"], p, wd_pad[l], dp, n + "gla_merge")
        dp = _dp_tail(dk, dv, dr, dp, n + "dp_tail")
        gw["w_in"][l] = _mm(dp, s["h"], n + "dw_in", ta=True, out_dtype=BF16, tk=t // 2 if t % 32 == 0 else None)
        if l == 0:
            core = lax.axis_index("c")
            halves = in_slots(0).reshape(NDEV // 2, 2, in_w, D)
            kept = lax.dynamic_index_in_dim(halves, core, axis=1, keepdims=False)
            sent = lax.dynamic_index_in_dim(halves, 1 - core, axis=1, keepdims=False)
            from_sibling, = _comm_alone(_SwapRider([sent]), n + "swap_dw_in")
            chip_sum = _pair_sum(kept, from_sibling, n + "chip_sum_dw_in")
            dh, recv_in0 = _mm(dp, wp[l], n + "dh", tk=NP // 4, rider=_ExchangeRider([chip_sum], chips_only=True))
        else:
            dh = _mm(dp, wp[l], n + "dh", tk=NP // 4)
        dx, dsh, dsc, gw["g_pre"][l] = _prenorm_bwd(dh, s["x"], dx, g_pre[l][None, :], s["mod3"], nct, n + "prenorm")
        dmod = jnp.stack([_row0(dsh), _row0(dsc), _row0(dgt)], axis=1).reshape(2, 3 * D)
        gw["dmod"][l] = dmod
        dmod8 = jnp.concatenate([dmod, jnp.zeros((6, 3 * D), F32)], axis=0)
        dctx.append(_mm(dmod8, w_ada_f[l], n + "dsilu", tb=True))
    grad_x = dx[n_ctx:][None]
    g_cctx = _cctx_grad(dctx[0], dctx[1], dsilu_cc)[0]

    def st2(name):
        return jnp.stack(gw[name])

    g_b_ada = jnp.stack([gw["dmod"][l][0] + gw["dmod"][l][1] for l in range(DEPTH)])
    g_bdf = jnp.stack([gw["bdec"][l][0, :GH * GDK] for l in range(DEPTH)])
    g_bdb = jnp.stack([gw["bdec"][l][0, GH * GDK:] for l in range(DEPTH)])
    g_wdf = jnp.stack([gw["wd"][l][0:GLA_RANK, :GH * GDK] for l in range(DEPTH)])
    g_wdb = jnp.stack([gw["wd"][l][GLA_RANK:2 * GLA_RANK, GH * GDK:] for l in range(DEPTH)])
    rep_grads = [g_cctx, g_b_ada, st2("g_pre")[:, 0], st2("g_post")[:, 0], st2("qg")[:, 0], st2("kg")[:, 0], g_bdf, g_bdb,
                 st2("gla_g")[:, 0], st2("b_gate")[:, 0]]
    rep_w = [c_ctx, b_ada, g_pre, g_post, q_norm_g, k_norm_g, b_decay_fwd, b_decay_bwd, gla_norm_g, b_gate]
    rep_m = [m_c_ctx, m_b_ada, m_g_pre, m_g_post, m_q_norm_g, m_k_norm_g, m_b_decay_fwd, m_b_decay_bwd, m_gla_norm_g, m_b_gate]
    rep_v = [v_c_ctx, v_b_ada, v_g_pre, v_g_post, v_q_norm_g, v_k_norm_g, v_b_decay_fwd, v_b_decay_bwd, v_gla_norm_g, v_b_gate]
    def two_d(a):
        return a.reshape(1, -1) if a.ndim == 1 else a

    def owner_slots(g):
        return g.reshape(DEPTH, g.shape[1], NDEV, g.shape[2] // NDEV).transpose(2, 0, 1, 3)

    n_rep = len(rep_grads)
    small = _comm_alone(_Riders([
        _GatherRider([two_d(g) for g in rep_grads] + [silu_cc[0:2], jnp.stack(gw["dmod"])]),
        _ExchangeRider([owner_slots(st2("conv_w")[:, 0:3]), owner_slots(g_wdf), owner_slots(g_wdb)])]),
        "exchange_small_grads")
    rep_src, (a_all, d_all), sh_src = small[:n_rep], small[n_rep:n_rep + 2], small[n_rep + 2:]
    sh_w = [conv_w, w_decay_fwd, w_decay_bwd]
    sh_m = [m_conv_w, m_w_decay_fwd, m_w_decay_bwd]
    sh_v = [v_conv_w, v_w_decay_fwd, v_w_decay_bwd]
    small_out = _adamw_small(
        [(g, two_d(w), two_d(m), two_d(v)) for g, w, m, v in zip(rep_src, rep_w, rep_m, rep_v)]
        + list(zip(sh_src, sh_w, sh_m, sh_v)), "adam_small")
    rep_g, rep_d, rep_nm, rep_nv = [[small_out[j][k].reshape(rep_w[j].shape) for j in range(n_rep)] for k in range(4)]
    sh_gr, sh_d, sh_nm, sh_nv = [[small_out[n_rep + j][k] for j in range(len(sh_w))] for k in range(4)]

    a_all = a_all.reshape(NDEV * 2, D)
    d_all = d_all.transpose(1, 0, 2, 3).reshape(DEPTH, NDEV * 2, 3 * D)
    g_ada = jnp.stack([_mm(a_all, lax.dynamic_slice_in_dim(d_all[l], dev * ada_w, ada_w, axis=1), f"dw_ada{l}",
                           ta=True, precise=True, tk=NDEV * 2) for l in range(DEPTH)])
    ada_g, ada_d, ada_nm, ada_nv = _adamw(g_ada[None], w_ada, m_w_ada, v_w_ada, "adam_ada")

    big_w = [w_br_conv, w_br_attn, w_br_gla, w_out]
    big_m = [m_w_br_conv, m_w_br_attn, m_w_br_gla, m_w_out]
    big_v = [v_w_br_conv, v_w_br_attn, v_w_br_gla, v_w_out]
    big_out = [_adamw(recv_br[j], big_w[j], big_m[j], big_v[j], f"adam_big{j}") for j in range(len(big_w))]
    in_out = [_adamw(r_[:, None], in_t(w_in, l)[None], in_t(m_w_in, l)[None], in_t(v_w_in, l)[None], f"adam_in{l}")
              for l, r_ in enumerate((recv_in0, recv_in1))]
    in_res = [jnp.stack([in_out[l][k][0] for l in range(DEPTH)], axis=1).transpose(1, 2, 0) for k in range(4)]
    big_g, big_d, big_nm, big_nv = [[in_res[k]] + [o[k] for o in big_out] for k in range(4)]

    def ordered(rep, ada, big, sh):
        c_ctx_, b_ada_, g_pre_, g_post_, qg_, kg_, bdf_, bdb_, glag_, bgate_ = rep
        w_in_, brc_, bra_, brg_, wout_ = big
        conv_, wdf_, wdb_ = sh
        return [c_ctx_, ada, b_ada_, g_pre_, g_post_, w_in_, conv_, qg_, kg_, wdf_, bdf_, wdb_, bdb_, glag_,
                brc_, bra_, brg_, bgate_, wout_]

    return (loss, grad_x,
            *ordered(rep_g, ada_g, big_g, sh_gr), *ordered(rep_d, ada_d, big_d, sh_d),
            *ordered(rep_nm, ada_nm, big_nm, sh_nm), *ordered(rep_nv, ada_nv, big_nv, sh_nv))
```

```python
import functools

import numpy as np
import jax
import jax.numpy as jnp
from jax import lax
from jax.experimental import pallas as pl
from jax.experimental.pallas import tpu as pltpu

F32, BF16 = jnp.float32, jnp.bfloat16
HIGHEST = lax.Precision.HIGHEST

D = 1024
DEPTH = 2
GRID_W = 64
NH, NKV, HD = 8, 2, 128
GROUP = NH // NKV
ROPE_THETA = 10000.0
ATTN_SCALE = HD ** -0.5
Q_FOLD = ATTN_SCALE * 1.4426950408889634
P_HALO = 16
GH, GDK, GDV = 4, 128, 256
GLA_RANK = 16
GLA_TAU = 16.0
CH = 64
GLA_SCALE = GDK ** -0.5
EPS = 1e-6
NDEV = 8
LANE = 128
TM = 256
ATTN_HEADS_PER_STEP = 1
ATTN_KEY_CHUNK = 8192

ADAM_LR, ADAM_B1, ADAM_B2, ADAM_EPS, ADAM_WD, ADAM_STEP = 0.001, 0.9, 0.999, 1e-08, 0.01, 10

_SEGS = (("a_b", 0, 1024), ("a_z", 3072, 1024), ("a_c", 1024, 1024), ("a_x", 2048, 1024),
         ("z_attn", 5632, 1024), ("zg", 8736, 1024), ("gv", 7680, 1024), ("gq", 6656, 512), ("gk", 7168, 512),
         ("q", 4096, 1024), ("mg", 9760, 3072), ("k", 5120, 256), ("v", 5376, 256), ("r", 8704, 32))
DP_BLOCKS = {"conv_a": ("a_b", 2048), "conv_b": ("a_c", 2048), "branch": ("z_attn", 2048), "gla": ("gv", 2048),
             "q": ("q", 1024), "merge": ("mg", 3072), "tail": ("k", 1024)}
IN_WIDTH = 12832
NP = 13312
OFF = {}
_o = 0
for _n, _s, _w in _SEGS:
    OFF[_n] = _o
    _o += _w
R_PAD = 128


def _cparams(ngrid, vmem_mb):
    return pltpu.CompilerParams(dimension_semantics=("arbitrary",) * ngrid, vmem_limit_bytes=vmem_mb << 20)


def _pick(n, cands):
    for c in cands:
        if n % c == 0:
            return c
    return n


def _sigmoid(x):
    return 1.0 / (1.0 + jnp.exp(-x))


ADAM_SRC_BYTES = 8 << 20
ADAM_ROW_BYTES = 1 << 20


def _all_gather(xs, name):
    return _comm_alone(_GatherRider(xs), name)


_HBM = pl.BlockSpec(memory_space=pl.ANY)


class _Rider:
    def __init__(self, xs, out_shapes, remote_copies=NDEV - 1):
        self.xs, self.n = list(xs), len(xs)
        self.out_shape = [jax.ShapeDtypeStruct(s, x.dtype) for s, x in zip(out_shapes, xs)]
        self.scratch = [pltpu.SemaphoreType.DMA((remote_copies * self.n,)),
                        pltpu.SemaphoreType.DMA((remote_copies * self.n,)), pltpu.SemaphoreType.DMA((self.n,))]


class _GatherRider(_Rider):
    def __init__(self, xs):
        super().__init__(xs, [(NDEV,) + x.shape for x in xs])

    def _parts(self, x_refs, out_refs, sems):
        n = self.n
        send_sems, recv_sems, local_sems = sems
        mx, my, mc = lax.axis_index("x"), lax.axis_index("y"), lax.axis_index("c")
        me, sibling = (mx, my, mc), (mx, my, 1 - mc)
        chips = [(1 - mx, my), (mx, 1 - my), (1 - mx, 1 - my)]

        def slot(a, px, py, pc):
            return out_refs[a].at[4 * px + 2 * py + pc]

        def copy(k, a, block, to, own=False):
            return pltpu.make_async_remote_copy(
                src_ref=x_refs[a] if own else slot(a, *block), dst_ref=slot(a, *block),
                send_sem=send_sems.at[k * n + a], recv_sem=recv_sems.at[k * n + a],
                device_id=to, device_id_type=pl.DeviceIdType.MESH)

        mine = [pltpu.make_async_copy(x_refs[a], slot(a, *me), local_sems.at[a]) for a in range(n)]
        first = [copy(0, a, me, sibling, own=True) for a in range(n)]
        first += [copy(1 + j, a, me, (*chip, mc), own=True) for a in range(n) for j, chip in enumerate(chips)]
        landed = [copy(1 + j, a, (*chip, mc), me) for a in range(n) for j, chip in enumerate(chips)]
        passed = [copy(4 + j, a, (*chip, mc), sibling) for a in range(n) for j, chip in enumerate(chips)]
        from_sibling = [copy(0, a, sibling, me) for a in range(n)]
        from_sibling += [copy(4 + j, a, (*chip, 1 - mc), me) for a in range(n) for j, chip in enumerate(chips)]
        return mine, first, landed, passed, from_sibling

    def start(self, x_refs, out_refs, sems):
        mine, first, _, _, _ = self._parts(x_refs, out_refs, sems)
        for cp in mine + first:
            cp.start()

    def middle(self, x_refs, out_refs, sems):
        _, _, landed, passed, _ = self._parts(x_refs, out_refs, sems)
        for got, fwd in zip(landed, passed):
            got.wait_recv()
            fwd.start()

    def finish(self, x_refs, out_refs, sems):
        mine, first, _, passed, from_sibling = self._parts(x_refs, out_refs, sems)
        for cp in from_sibling:
            cp.wait_recv()
        for cp in first + passed:
            cp.wait_send()
        for cp in mine:
            cp.wait()


class _ExchangeRider(_Rider):
    def __init__(self, xs, chips_only=False):
        self.chips_only = chips_only
        super().__init__(xs, [x.shape for x in xs], 3 if chips_only else NDEV - 1)

    def _parts(self, x_refs, out_refs, sems):
        n = self.n
        send_sems, recv_sems, local_sems = sems
        mx, my, mc = lax.axis_index("x"), lax.axis_index("y"), lax.axis_index("c")
        me = 2 * mx + my if self.chips_only else 4 * mx + 2 * my + mc
        mine = [pltpu.make_async_copy(x_refs[a].at[me], out_refs[a].at[me], local_sems.at[a]) for a in range(n)]
        copies = []
        for a in range(n):
            for rel in range(1, 4 if self.chips_only else NDEV):
                bits = rel << 1 if self.chips_only else rel
                px = (1 - mx) if bits & 4 else mx
                py = (1 - my) if bits & 2 else my
                pc = (1 - mc) if bits & 1 else mc
                peer = 2 * px + py if self.chips_only else 4 * px + 2 * py + pc
                k = (rel - 1) * n + a
                copies.append(pltpu.make_async_remote_copy(
                    src_ref=x_refs[a].at[peer], dst_ref=out_refs[a].at[me],
                    send_sem=send_sems.at[k], recv_sem=recv_sems.at[k],
                    device_id=(px, py, pc), device_id_type=pl.DeviceIdType.MESH))
        return mine, copies

    def start(self, x_refs, out_refs, sems):
        mine, copies = self._parts(x_refs, out_refs, sems)
        for cp in mine + copies:
            cp.start()

    def middle(self, x_refs, out_refs, sems):
        pass

    def finish(self, x_refs, out_refs, sems):
        mine, copies = self._parts(x_refs, out_refs, sems)
        for cp in copies:
            cp.wait_recv()
        for cp in copies:
            cp.wait_send()
        for cp in mine:
            cp.wait()


class _SwapRider(_Rider):
    def __init__(self, xs):
        super().__init__(xs, [x.shape for x in xs], 1)

    def _parts(self, x_refs, out_refs, sems):
        send_sems, recv_sems, _ = sems
        sibling = (lax.axis_index("x"), lax.axis_index("y"), 1 - lax.axis_index("c"))
        return [pltpu.make_async_remote_copy(
            src_ref=x_refs[a], dst_ref=out_refs[a], send_sem=send_sems.at[a], recv_sem=recv_sems.at[a],
            device_id=sibling, device_id_type=pl.DeviceIdType.MESH) for a in range(self.n)]

    def start(self, x_refs, out_refs, sems):
        for cp in self._parts(x_refs, out_refs, sems):
            cp.start()

    def middle(self, x_refs, out_refs, sems):
        pass

    def finish(self, x_refs, out_refs, sems):
        copies = self._parts(x_refs, out_refs, sems)
        for cp in copies:
            cp.wait_recv()
        for cp in copies:
            cp.wait_send()


class _Riders:
    def __init__(self, riders):
        self.riders = list(riders)
        self.xs = [x for r in self.riders for x in r.xs]
        self.n = len(self.xs)
        self.out_shape = [s for r in self.riders for s in r.out_shape]
        self.scratch = [s for r in self.riders for s in r.scratch]

    def _each(self, method, x_refs, out_refs, sems):
        a = b = 0
        for r in self.riders:
            getattr(r, method)(x_refs[a:a + r.n], out_refs[a:a + r.n], sems[b:b + len(r.scratch)])
            a, b = a + r.n, b + len(r.scratch)

    def start(self, *refs):
        self._each("start", *refs)

    def middle(self, *refs):
        self._each("middle", *refs)

    def finish(self, *refs):
        self._each("finish", *refs)


def _comm_alone(rider, name):
    n = rider.n

    def body(*refs):
        x_refs, out_refs, sems = refs[:n], refs[n:2 * n], refs[2 * n:]
        rider.start(x_refs, out_refs, sems)
        rider.middle(x_refs, out_refs, sems)
        rider.finish(x_refs, out_refs, sems)

    return pl.pallas_call(
        body, name=name, out_shape=tuple(rider.out_shape), in_specs=[_HBM] * n, out_specs=(_HBM,) * n,
        scratch_shapes=rider.scratch,
    )(*rider.xs)


def _with_rider(body, nin, nout, rider, first, mid, last):
    if rider is None:
        return body
    n = rider.n

    def wrapped(*refs):
        ins, x_refs = refs[:nin], refs[nin:nin + n]
        outs, out_refs = refs[nin + n:nin + n + nout], refs[nin + n + nout:nin + 2 * n + nout]
        ns = len(rider.scratch)
        scratch, sems = refs[nin + 2 * n + nout:len(refs) - ns], refs[len(refs) - ns:]

        @pl.when(first())
        def _():
            rider.start(x_refs, out_refs, sems)

        body(*ins, *outs, *scratch)

        @pl.when(mid())
        def _():
            rider.middle(x_refs, out_refs, sems)

        @pl.when(last())
        def _():
            rider.finish(x_refs, out_refs, sems)

    return wrapped


def _mm(a, b, name, ta=False, tb=False, out_dtype=F32, bias=None, precise=False, tm=None, tn=None, tk=None, rider=None):
    m, k = (a.shape[1], a.shape[0]) if ta else a.shape
    n = b.shape[0] if tb else b.shape[1]
    assert k == (b.shape[1] if tb else b.shape[0])
    tm = tm or _pick(m, (1088, 1024, 512, 256, 128))
    tn = tn or _pick(n, (1024, 512, 384, 256, 128))
    tk = tk or _pick(k, (1024, 1088, 512, 256, 128))
    nk = k // tk
    dn = (((0 if ta else 1,), (1 if tb else 0,)), ((), ()))

    def body(*refs):
        if bias is None:
            a_ref, b_ref, o_ref = refs[:3]
            bias_ref = None
        else:
            a_ref, b_ref, bias_ref, o_ref = refs[:4]
        x, y = a_ref[...], b_ref[...]
        if precise:
            p = lax.dot_general(x.astype(F32), y.astype(F32), dn, preferred_element_type=F32, precision=HIGHEST)
        else:
            p = lax.dot_general(x.astype(BF16), y.astype(BF16), dn, preferred_element_type=F32)

        def finish(acc):
            if bias_ref is not None:
                acc = acc + bias_ref[...]
            o_ref[...] = acc.astype(out_dtype)

        if nk == 1:
            finish(p)
        else:
            acc_ref = refs[-1]
            kk = pl.program_id(2)

            @pl.when(kk == 0)
            def _():
                acc_ref[...] = p

            @pl.when(kk > 0)
            def _():
                acc_ref[...] += p

            @pl.when(kk == nk - 1)
            def _():
                finish(acc_ref[...])

    a_spec = pl.BlockSpec((tk, tm), lambda i, j, kk: (kk, i)) if ta else pl.BlockSpec((tm, tk), lambda i, j, kk: (i, kk))
    b_spec = pl.BlockSpec((tn, tk), lambda i, j, kk: (j, kk)) if tb else pl.BlockSpec((tk, tn), lambda i, j, kk: (kk, j))
    in_specs = [a_spec, b_spec]
    args = [a, b]
    if bias is not None:
        in_specs.append(pl.BlockSpec((1, tn), lambda i, j, kk: (0, j)))
        args.append(bias)
    grid = (m // tm, n // tn, nk)
    out_spec = pl.BlockSpec((tm, tn), lambda i, j, kk: (i, j))
    scratch = [pltpu.VMEM((tm, tn), F32)] if nk > 1 else []
    if rider is None:
        return pl.pallas_call(
            body, name=name, grid=grid, in_specs=in_specs, out_specs=out_spec,
            out_shape=jax.ShapeDtypeStruct((m, n), out_dtype), scratch_shapes=scratch, compiler_params=_cparams(3, 56),
        )(*args)

    def at(step):
        return lambda: ((pl.program_id(0) == step[0]) & (pl.program_id(1) == step[1]) & (pl.program_id(2) == step[2]))

    end = tuple(g - 1 for g in grid)
    return pl.pallas_call(
        _with_rider(body, len(args), 1, rider, at((0, 0, 0)), at((grid[0] // 2, 0, 0)), at(end)),
        name=name, grid=grid, in_specs=in_specs + [_HBM] * rider.n, out_specs=(out_spec,) + (_HBM,) * rider.n,
        out_shape=(jax.ShapeDtypeStruct((m, n), out_dtype),) + tuple(rider.out_shape),
        scratch_shapes=scratch + rider.scratch, compiler_params=_cparams(3, 56),
    )(*args, *rider.xs)


def _ada_in(cc):
    def body(c_ref, s_ref, d_ref):
        x = c_ref[...]
        sg = _sigmoid(x)
        s_ref[...] = x * sg
        d_ref[...] = sg * (1.0 + x * (1.0 - sg))

    return pl.pallas_call(body, name="ada_in", out_shape=(jax.ShapeDtypeStruct(cc.shape, F32),) * 2)(cc)


def _cctx_grad(t0, t1, dsilu):
    def body(a_ref, b_ref, d_ref, o_ref):
        o_ref[...] = (a_ref[...] + b_ref[...]) * d_ref[...]

    return pl.pallas_call(body, name="cctx_grad", out_shape=jax.ShapeDtypeStruct(t0.shape, F32))(t0, t1, dsilu)


def _seg_spec(nct, rows=3):
    return pl.BlockSpec((None, rows, D), lambda i: (jnp.where(i >= nct, 1, 0), 0, 0))


def _prenorm_fwd(x, g_pre, mod3, nct, name):
    t = x.shape[0]

    def body(x_ref, g_ref, mod_ref, h_ref):
        xv = x_ref[...]
        r = lax.rsqrt(jnp.mean(xv * xv, axis=-1, keepdims=True) + EPS)
        y = xv * r * g_ref[...]
        h_ref[...] = (y * (1.0 + mod_ref[1:2, :]) + mod_ref[0:1, :]).astype(BF16)

    return pl.pallas_call(
        body, name=name, grid=(t // TM,),
        in_specs=[pl.BlockSpec((TM, D), lambda i: (i, 0)), pl.BlockSpec((1, D), lambda i: (0, 0)), _seg_spec(nct)],
        out_specs=pl.BlockSpec((TM, D), lambda i: (i, 0)),
        out_shape=jax.ShapeDtypeStruct((t, D), BF16), compiler_params=_cparams(1, 32),
    )(x, g_pre, mod3)


def _prenorm_bwd(dh, x, dxo, g_pre, mod3, nct, name):
    t = x.shape[0]

    def body(dh_ref, x_ref, dxo_ref, g_ref, mod_ref, dx_ref, dsh_ref, dsc_ref, dg_ref):
        i = pl.program_id(0)
        xv, dhv, g = x_ref[...], dh_ref[...], g_ref[...]
        r = lax.rsqrt(jnp.mean(xv * xv, axis=-1, keepdims=True) + EPS)
        xh = xv * r
        dy = dhv * (1.0 + mod_ref[1:2, :])
        dxh = dy * g
        dx_ref[...] = dxo_ref[...] + r * (dxh - xh * jnp.mean(dxh * xh, axis=-1, keepdims=True))

        @pl.when((i == 0) | (i == nct))
        def _():
            dsh_ref[...] = jnp.zeros_like(dsh_ref)
            dsc_ref[...] = jnp.zeros_like(dsc_ref)

        @pl.when(i == 0)
        def _():
            dg_ref[...] = jnp.zeros_like(dg_ref)

        dsh_ref[...] += jnp.sum(dhv, axis=0, keepdims=True)
        dsc_ref[...] += jnp.sum(dhv * (xh * g), axis=0, keepdims=True)
        dg_ref[...] += jnp.sum(dy * xh, axis=0, keepdims=True)

    row = pl.BlockSpec((TM, D), lambda i: (i, 0))
    seg8 = pl.BlockSpec((None, 8, D), lambda i: (jnp.where(i >= nct, 1, 0), 0, 0))
    return pl.pallas_call(
        body, name=name, grid=(t // TM,),
        in_specs=[row, row, row, pl.BlockSpec((1, D), lambda i: (0, 0)), _seg_spec(nct)],
        out_specs=(row, seg8, seg8, pl.BlockSpec((8, D), lambda i: (0, 0))),
        out_shape=(jax.ShapeDtypeStruct((t, D), F32), jax.ShapeDtypeStruct((2, 8, D), F32),
                   jax.ShapeDtypeStruct((2, 8, D), F32), jax.ShapeDtypeStruct((8, D), F32)),
        compiler_params=_cparams(1, 32),
    )(dh, x, dxo, g_pre, mod3)


def _post_fwd(x, out, g_post, mod3, nct, name):
    t = x.shape[0]

    def body(x_ref, o_ref, g_ref, mod_ref, y_ref):
        ov = o_ref[...]
        r = lax.rsqrt(jnp.mean(ov * ov, axis=-1, keepdims=True) + EPS)
        y_ref[...] = x_ref[...] + mod_ref[2:3, :] * (ov * r * g_ref[...])

    row = pl.BlockSpec((TM, D), lambda i: (i, 0))
    return pl.pallas_call(
        body, name=name, grid=(t // TM,),
        in_specs=[row, row, pl.BlockSpec((1, D), lambda i: (0, 0)), _seg_spec(nct)],
        out_specs=row, out_shape=jax.ShapeDtypeStruct((t, D), F32), compiler_params=_cparams(1, 32),
    )(x, out, g_post, mod3)


def _post_bwd(dxo, out, g_post, mod3, nct, name):
    t = out.shape[0]

    def body(dx_ref, o_ref, g_ref, mod_ref, do_ref, dgt_ref, dg_ref):
        i = pl.program_id(0)
        ov, dxv, g = o_ref[...], dx_ref[...], g_ref[...]
        r = lax.rsqrt(jnp.mean(ov * ov, axis=-1, keepdims=True) + EPS)
        nh = ov * r
        dn = dxv * mod_ref[2:3, :]
        dnh = dn * g
        do_ref[...] = (r * (dnh - nh * jnp.mean(dnh * nh, axis=-1, keepdims=True))).astype(BF16)

        @pl.when((i == 0) | (i == nct))
        def _():
            dgt_ref[...] = jnp.zeros_like(dgt_ref)

        @pl.when(i == 0)
        def _():
            dg_ref[...] = jnp.zeros_like(dg_ref)

        dgt_ref[...] += jnp.sum(dxv * (nh * g), axis=0, keepdims=True)
        dg_ref[...] += jnp.sum(dn * nh, axis=0, keepdims=True)

    row = pl.BlockSpec((TM, D), lambda i: (i, 0))
    seg8 = pl.BlockSpec((None, 8, D), lambda i: (jnp.where(i >= nct, 1, 0), 0, 0))
    return pl.pallas_call(
        body, name=name, grid=(t // TM,),
        in_specs=[row, row, pl.BlockSpec((1, D), lambda i: (0, 0)), _seg_spec(nct)],
        out_specs=(row, seg8, pl.BlockSpec((8, D), lambda i: (0, 0))),
        out_shape=(jax.ShapeDtypeStruct((t, D), BF16), jax.ShapeDtypeStruct((2, 8, D), F32),
                   jax.ShapeDtypeStruct((8, D), F32)),
        compiler_params=_cparams(1, 32),
    )(dxo, out, g_post, mod3)


def _loss_grad(y, target, nct, name):
    t = y.shape[0]

    def body(y_ref, t_ref, dy_ref, l_ref):
        i = pl.program_id(0)

        @pl.when(i == 0)
        def _():
            l_ref[...] = jnp.zeros_like(l_ref)

        @pl.when(i < nct)
        def _():
            dy_ref[...] = jnp.zeros_like(dy_ref)

        @pl.when(i >= nct)
        def _():
            err = y_ref[...] - t_ref[...]
            dy_ref[...] = err / D
            l_ref[...] += jnp.sum(jnp.sum(err * err, axis=1, keepdims=True), axis=0, keepdims=True)

    row = pl.BlockSpec((TM, D), lambda i: (i, 0))
    return pl.pallas_call(
        body, name=name, grid=(t // TM,),
        in_specs=[row, pl.BlockSpec((TM, D), lambda i: (jnp.maximum(i - nct, 0), 0))],
        out_specs=(row, pl.BlockSpec((8, LANE), lambda i: (0, 0))),
        out_shape=(jax.ShapeDtypeStruct((t, D), F32), jax.ShapeDtypeStruct((8, LANE), F32)),
        compiler_params=_cparams(1, 32),
    )(y, target)


def _pcol(name, width):
    assert OFF[name] % width == 0
    blk = OFF[name] // width
    return pl.BlockSpec((TM, width), lambda i: (i, blk))


def _shift_rows(u, prev_row, next_row):
    n = u.shape[0]
    row = lax.broadcasted_iota(jnp.int32, u.shape, 0)
    prev = jnp.where(row == 0, prev_row, pltpu.roll(u, 1, 0))
    nxt = jnp.where(row == n - 1, next_row, pltpu.roll(u, n - 1, 0))
    return prev, nxt


def _halo_specs(width, nt, blk=0, rows=8):
    per = TM // rows
    prev = pl.BlockSpec((rows, width), lambda i: (jnp.maximum(i * per - 1, 0), blk))
    nxt = pl.BlockSpec((rows, width), lambda i: (jnp.minimum((i + 1) * per, nt * per - 1), blk))
    return prev, nxt


def _conv_fwd(p, conv_w8, nct, name):
    t = p.shape[0]
    nt = t // TM

    def body(ab_ref, ac_ref, ax_ref, az_ref, acp_ref, axp_ref, acn_ref, axn_ref, w_ref, cv_ref, ya_ref):
        i = pl.program_id(0)
        def f(ref, rows=slice(None)):
            return ref[rows, :].astype(F32)

        u = f(ac_ref) * f(ax_ref)
        mp = jnp.where((i == 0) | (i == nct), 0.0, 1.0)
        mn = jnp.where((i == nct - 1) | (i == nt - 1), 0.0, 1.0)
        last, first = slice(P_HALO - 1, P_HALO), slice(0, 1)
        prev, nxt = _shift_rows(u, f(acp_ref, last) * f(axp_ref, last) * mp, f(acn_ref, first) * f(axn_ref, first) * mn)
        cv = w_ref[0:1, :] * prev + w_ref[1:2, :] * u + w_ref[2:3, :] * nxt
        az = f(az_ref)
        cv_ref[...] = cv.astype(BF16)
        ya_ref[...] = (f(ab_ref) * cv * (az * _sigmoid(az))).astype(BF16)

    acp, acn = _halo_specs(D, nt, OFF["a_c"] // D, P_HALO)
    axp, axn = _halo_specs(D, nt, OFF["a_x"] // D, P_HALO)
    row = pl.BlockSpec((TM, D), lambda i: (i, 0))
    return pl.pallas_call(
        body, name=name, grid=(nt,),
        in_specs=[_pcol("a_b", D), _pcol("a_c", D), _pcol("a_x", D), _pcol("a_z", D), acp, axp, acn, axn,
                  pl.BlockSpec((8, D), lambda i: (0, 0))],
        out_specs=(row, row),
        out_shape=(jax.ShapeDtypeStruct((t, D), BF16), jax.ShapeDtypeStruct((t, D), BF16)),
        compiler_params=_cparams(1, 40),
    )(p, p, p, p, p, p, p, p, conv_w8)


def _dp_spec(key):
    seg, width = DP_BLOCKS[key]
    assert OFF[seg] % width == 0
    blk = OFF[seg] // width
    return pl.BlockSpec((TM, width), lambda i: (i, blk))


def _conv_bwd_a(dya, p, cv, dp, name):
    t = p.shape[0]

    def body(dy_ref, ab_ref, az_ref, cv_ref, _, dcv_ref, dp_ref):
        dy, ab = dy_ref[...].astype(F32), ab_ref[...].astype(F32)
        az, c = az_ref[...].astype(F32), cv_ref[...].astype(F32)
        sg = _sigmoid(az)
        sz = az * sg
        dcv_ref[...] = dy * ab * sz
        dp_ref[:, 0:D] = (dy * c * sz).astype(BF16)
        dp_ref[:, D:2 * D] = (dy * ab * c * (sg * (1.0 + az * (1.0 - sg)))).astype(BF16)

    row = pl.BlockSpec((TM, D), lambda i: (i, 0))
    return pl.pallas_call(
        body, name=name, grid=(t // TM,),
        in_specs=[row, _pcol("a_b", D), _pcol("a_z", D), row, _HBM], out_specs=(row, _dp_spec("conv_a")),
        out_shape=(jax.ShapeDtypeStruct((t, D), F32), jax.ShapeDtypeStruct(dp.shape, dp.dtype)),
        input_output_aliases={4: 1}, compiler_params=_cparams(1, 40),
    )(dya, p, p, cv, dp)


def _conv_bwd_b(dcv, p, conv_w8, nct, dp, name):
    t = p.shape[0]
    nt = t // TM

    def body(dcv_ref, hp_ref, hn_ref, ac_ref, ax_ref, w_ref, _, dp_ref, dw_ref):
        i = pl.program_id(0)
        d, ac, ax = dcv_ref[...], ac_ref[...].astype(F32), ax_ref[...].astype(F32)
        u = ac * ax
        mp = jnp.where((i == 0) | (i == nct), 0.0, 1.0)
        mn = jnp.where((i == nct - 1) | (i == nt - 1), 0.0, 1.0)
        dprev, dnxt = _shift_rows(d, hp_ref[7:8, :] * mp, hn_ref[0:1, :] * mn)
        du = w_ref[0:1, :] * dnxt + w_ref[1:2, :] * d + w_ref[2:3, :] * dprev
        dp_ref[:, 0:D] = (du * ax).astype(BF16)
        dp_ref[:, D:2 * D] = (du * ac).astype(BF16)

        @pl.when(i == 0)
        def _():
            dw_ref[...] = jnp.zeros_like(dw_ref)

        dw0 = jnp.sum(u * dnxt, axis=0, keepdims=True)
        dw1 = jnp.sum(u * d, axis=0, keepdims=True)
        dw2 = jnp.sum(u * dprev, axis=0, keepdims=True)
        r8 = lax.broadcasted_iota(jnp.int32, (8, D), 0)
        dw_ref[...] += jnp.where(r8 == 0, dw0, jnp.where(r8 == 1, dw1, jnp.where(r8 == 2, dw2, 0.0)))

    hp, hn = _halo_specs(D, nt)
    row = pl.BlockSpec((TM, D), lambda i: (i, 0))
    return pl.pallas_call(
        body, name=name, grid=(nt,),
        in_specs=[row, hp, hn, _pcol("a_c", D), _pcol("a_x", D), pl.BlockSpec((8, D), lambda i: (0, 0)), _HBM],
        out_specs=(_dp_spec("conv_b"), pl.BlockSpec((8, D), lambda i: (0, 0))),
        out_shape=(jax.ShapeDtypeStruct(dp.shape, dp.dtype), jax.ShapeDtypeStruct((8, D), F32)),
        input_output_aliases={6: 0}, compiler_params=_cparams(1, 40),
    )(dcv, dcv, dcv, p, p, conv_w8, dp)


def _rot_half(x):
    lane = lax.broadcasted_iota(jnp.int32, x.shape, 1)
    return jnp.where((lane % 64) < 32, pltpu.roll(x, 96, 1), pltpu.roll(x, 32, 1))


def _qk_prep_fwd(p, qg, kg, cos_t, sin_t, name):
    t = p.shape[0]

    def body(q_ref, k_ref, qg_ref, kg_ref, c_ref, s_ref, qo_ref, ko_ref):
        c, s = c_ref[...], s_ref[...]

        def one(xv, g, scale):
            y = xv * lax.rsqrt(jnp.mean(xv * xv, axis=-1, keepdims=True) + EPS) * g
            return ((y * c + _rot_half(y) * s) * scale).astype(BF16)

        for h in range(NH):
            qo_ref[:, h * HD:(h + 1) * HD] = one(q_ref[:, h * HD:(h + 1) * HD].astype(F32), qg_ref[...], Q_FOLD)
        for h in range(NKV):
            ko_ref[:, h * HD:(h + 1) * HD] = one(k_ref[:, h * HD:(h + 1) * HD].astype(F32), kg_ref[...], 1.0)

    vec = pl.BlockSpec((1, HD), lambda i: (0, 0))
    tab = pl.BlockSpec((TM, HD), lambda i: (i, 0))
    return pl.pallas_call(
        body, name=name, grid=(t // TM,),
        in_specs=[_pcol("q", NH * HD), _pcol("k", NKV * HD), vec, vec, tab, tab],
        out_specs=(pl.BlockSpec((TM, NH * HD), lambda i: (i, 0)), pl.BlockSpec((TM, NKV * HD), lambda i: (i, 0))),
        out_shape=(jax.ShapeDtypeStruct((t, NH * HD), BF16), jax.ShapeDtypeStruct((t, NKV * HD), BF16)),
        compiler_params=_cparams(1, 32),
    )(p, p, qg, kg, cos_t, sin_t)


def _qk_prep_bwd(dqr, dkr, p, qg, kg, cos_t, sin_t, dp, name):
    t = p.shape[0]

    def body(dq_ref, dk_ref, q_ref, k_ref, qg_ref, kg_ref, c_ref, s_ref, _, dqo_ref, dko_ref, dqg_ref, dkg_ref):
        i = pl.program_id(0)
        c, s = c_ref[...], s_ref[...]

        @pl.when(i == 0)
        def _():
            dqg_ref[...] = jnp.zeros_like(dqg_ref)
            dkg_ref[...] = jnp.zeros_like(dkg_ref)

        def one(dyr, xv, g):
            dy = dyr * c + _rot_half(dyr * s)
            r = lax.rsqrt(jnp.mean(xv * xv, axis=-1, keepdims=True) + EPS)
            xh = xv * r
            dxh = dy * g
            dx = r * (dxh - xh * jnp.mean(dxh * xh, axis=-1, keepdims=True))
            return dx.astype(BF16), jnp.sum(dy * xh, axis=0, keepdims=True)

        for h in range(NH):
            sl = slice(h * HD, (h + 1) * HD)
            dx, dg = one(dq_ref[:, sl] * ATTN_SCALE, q_ref[:, sl].astype(F32), qg_ref[...])
            dqo_ref[:, sl] = dx
            dqg_ref[...] += dg
        for h in range(NKV):
            sl = slice(h * HD, (h + 1) * HD)
            dx, dg = one(dk_ref[:, sl] * (ATTN_SCALE / Q_FOLD), k_ref[:, sl].astype(F32), kg_ref[...])
            dko_ref[:, sl] = dx
            dkg_ref[...] += dg

    vec = pl.BlockSpec((1, HD), lambda i: (0, 0))
    tab = pl.BlockSpec((TM, HD), lambda i: (i, 0))
    acc = pl.BlockSpec((8, HD), lambda i: (0, 0))
    qrow = pl.BlockSpec((TM, NH * HD), lambda i: (i, 0))
    krow = pl.BlockSpec((TM, NKV * HD), lambda i: (i, 0))
    return pl.pallas_call(
        body, name=name, grid=(t // TM,),
        in_specs=[qrow, krow, _pcol("q", NH * HD), _pcol("k", NKV * HD), vec, vec, tab, tab, _HBM],
        out_specs=(_dp_spec("q"), krow, acc, acc),
        out_shape=(jax.ShapeDtypeStruct(dp.shape, dp.dtype), jax.ShapeDtypeStruct((t, NKV * HD), BF16),
                   jax.ShapeDtypeStruct((8, HD), F32), jax.ShapeDtypeStruct((8, HD), F32)),
        input_output_aliases={8: 0}, compiler_params=_cparams(1, 32),
    )(dqr, dkr, p, p, qg, kg, cos_t, sin_t, dp)


def _key_chunks(n):
    c = max(c for c in range(LANE, min(n, ATTN_KEY_CHUNK) + 1, LANE) if n % c == 0)
    return [(lo, lo + c) for lo in range(0, n, c)]


def _attn_fwd(qr, kr, p, nct, name, rider=None):
    t = qr.shape[0]
    nt = t // TM
    ctx = nct * TM
    vblk = OFF["v"] // HD
    hps = ATTN_HEADS_PER_STEP
    nhp, per_kv = NH // hps, GROUP // hps

    def body(q_ref, k_ref, v_ref, o_ref, lse_ref):
        def tile(nkeys):
            for j in range(hps):
                sl = slice(j * HD, (j + 1) * HD)
                q = q_ref[:, sl]
                m = l = acc = None
                for lo, hi in _key_chunks(nkeys):
                    s = lax.dot_general(q, k_ref[lo:hi, :], _NT, preferred_element_type=F32)
                    mc = jnp.max(s, axis=-1, keepdims=True)
                    m_new = mc if m is None else jnp.maximum(m, mc)
                    e = jnp.exp2(s - m_new)
                    lc = jnp.sum(e, axis=-1, keepdims=True)
                    pv = jnp.dot(e.astype(BF16), v_ref[lo:hi, :].astype(BF16), preferred_element_type=F32)
                    if m is None:
                        l, acc = lc, pv
                    else:
                        alpha = jnp.exp2(m - m_new)
                        l, acc = l * alpha + lc, acc * alpha + pv
                    m = m_new
                o_ref[:, sl] = (acc / l).astype(BF16)
                lse_ref[:, j:j + 1] = m + jnp.log2(l)

        pl.when(pl.program_id(1) < nct)(lambda: tile(ctx))
        pl.when(pl.program_id(1) >= nct)(lambda: tile(t))

    def at(h, i):
        return lambda: (pl.program_id(0) == h) & (pl.program_id(1) == i)

    rn = 0 if rider is None else rider.n
    qspec = pl.BlockSpec((TM, hps * HD), lambda h, i: (i, h))
    return pl.pallas_call(
        _with_rider(body, 3, 2, rider, at(0, 0), at(nhp * 7 // 8, 0), at(nhp - 1, nt - 1)),
        name=name, grid=(nhp, nt),
        in_specs=[qspec, pl.BlockSpec((t, HD), lambda h, i: (0, h // per_kv)),
                  pl.BlockSpec((t, HD), lambda h, i: (0, vblk + h // per_kv))] + [_HBM] * rn,
        out_specs=(qspec, pl.BlockSpec((None, TM, hps), lambda h, i: (h, i, 0))) + (_HBM,) * rn,
        out_shape=(jax.ShapeDtypeStruct((t, NH * HD), BF16), jax.ShapeDtypeStruct((nhp, t, hps), F32))
        + (() if rider is None else tuple(rider.out_shape)),
        scratch_shapes=[] if rider is None else rider.scratch,
        compiler_params=_cparams(2, 48),
    )(qr, kr, p, *(() if rider is None else rider.xs))


def _attn_bwd(qr, kr, p, o, lse, do, nct, name, rider=None):
    t = qr.shape[0]
    nt = t // TM
    ctx = nct * TM
    vblk = OFF["v"] // HD
    hps = ATTN_HEADS_PER_STEP

    def body(q_ref, k_ref, v_ref, o_ref, lse_ref, do_ref, dq_ref, dk_ref, dv_ref):
        g, i = pl.program_id(1), pl.program_id(2)

        @pl.when((g == 0) & (i == 0))
        def _():
            dk_ref[...] = jnp.zeros_like(dk_ref)
            dv_ref[...] = jnp.zeros_like(dv_ref)

        def tile(nkeys):
            heads = []
            for j in range(hps):
                sl = slice(j * HD, (j + 1) * HD)
                dob = do_ref[:, sl]
                drow = jnp.sum(dob.astype(F32) * o_ref[:, sl].astype(F32), axis=-1, keepdims=True)
                heads.append((sl, q_ref[:, sl], dob, drow, lse_ref[:, j:j + 1]))
            dq = [None] * hps
            for lo, hi in _key_chunks(nkeys):
                k = k_ref[lo:hi, :]
                vb = v_ref[lo:hi, :].astype(BF16)
                dk_c = dv_c = None
                for j, (sl, q, dob, drow, lse_j) in enumerate(heads):
                    s = lax.dot_general(q, k, _NT, preferred_element_type=F32)
                    pr = jnp.exp2(s - lse_j)
                    dp = lax.dot_general(dob, vb, _NT, preferred_element_type=F32)
                    ds = (pr * (dp - drow)).astype(BF16)
                    dq_c = jnp.dot(ds, k, preferred_element_type=F32)
                    dq[j] = dq_c if dq[j] is None else dq[j] + dq_c
                    dk_j = lax.dot_general(ds, q, _TN, preferred_element_type=F32)
                    dv_j = lax.dot_general(pr.astype(BF16), dob, _TN, preferred_element_type=F32)
                    dk_c = dk_j if dk_c is None else dk_c + dk_j
                    dv_c = dv_j if dv_c is None else dv_c + dv_j
                dk_ref[lo:hi, :] += dk_c
                dv_ref[lo:hi, :] += dv_c
            for j, (sl, *_) in enumerate(heads):
                dq_ref[:, sl] = dq[j]

        pl.when(i < nct)(lambda: tile(ctx))
        pl.when(i >= nct)(lambda: tile(t))

    def at(kv, g, i):
        return lambda: (pl.program_id(0) == kv) & (pl.program_id(1) == g) & (pl.program_id(2) == i)

    rn = 0 if rider is None else rider.n
    per_kv = GROUP // hps
    qspec = pl.BlockSpec((TM, hps * HD), lambda kv, g, i: (i, kv * per_kv + g))
    kvspec = pl.BlockSpec((t, HD), lambda kv, g, i: (0, kv))
    lspec = pl.BlockSpec((None, TM, hps), lambda kv, g, i: (kv * per_kv + g, i, 0))
    return pl.pallas_call(
        _with_rider(body, 6, 3, rider, at(0, 0, 0), at(NKV - 1, 0, 0), at(NKV - 1, per_kv - 1, nt - 1)),
        name=name, grid=(NKV, per_kv, nt),
        in_specs=[qspec, kvspec, pl.BlockSpec((t, HD), lambda kv, g, i: (0, vblk + kv)), qspec, lspec, qspec]
        + [_HBM] * rn,
        out_specs=(qspec, kvspec, kvspec) + (_HBM,) * rn,
        out_shape=(jax.ShapeDtypeStruct((t, NH * HD), F32), jax.ShapeDtypeStruct((t, NKV * HD), F32),
                   jax.ShapeDtypeStruct((t, NKV * HD), F32)) + (() if rider is None else tuple(rider.out_shape)),
        scratch_shapes=[] if rider is None else rider.scratch,
        compiler_params=_cparams(3, 48),
    )(qr, kr, p, o, lse, do, *(() if rider is None else rider.xs))


def _decay_fwd(p, wd, bd, name):
    t = p.shape[0]

    def body(r_ref, w_ref, b_ref, z_ref, bc_ref):
        z = jnp.dot(r_ref[...].astype(BF16), w_ref[...].astype(BF16), preferred_element_type=F32) + b_ref[...]
        z_ref[...] = z
        la = (jnp.minimum(z, 0.0) - jnp.log(1.0 + jnp.exp(-jnp.abs(z)))) / GLA_TAU
        half = GH * GDK
        bc_ref[:, 0:half] = _chunk_sums(la[:, 0:half], False)
        bc_ref[:, half:] = _chunk_sums(la[:, half:], True)

    row = pl.BlockSpec((TM, D), lambda i: (i, 0))
    return pl.pallas_call(
        body, name=name, grid=(t // TM,),
        in_specs=[_pcol("r", R_PAD), pl.BlockSpec((R_PAD, D), lambda i: (0, 0)), pl.BlockSpec((1, D), lambda i: (0, 0))],
        out_specs=(row, row),
        out_shape=(jax.ShapeDtypeStruct((t, D), F32), jax.ShapeDtypeStruct((t, D), F32)),
        compiler_params=_cparams(1, 32),
    )(p, wd, bd)


def _chunk_order(s, ncc, nc, rev):
    if not rev:
        return s
    return jnp.where(s < ncc, ncc - 1 - s, nc - 1 - (s - ncc))


def _gla_chains(dirs):
    return [(rev, d, h) + tuple(refs) for d, (rev, *refs) in enumerate(dirs) for h in range(GH)]


def _hk(h):
    return slice(h * GDK, (h + 1) * GDK)


def _hv(h):
    return slice(h * GDV, (h + 1) * GDV)


def _chunk_sums(x, from_end):
    r = lax.broadcasted_iota(jnp.int32, (CH, CH), 0)
    c = lax.broadcasted_iota(jnp.int32, (CH, CH), 1)
    tri = ((c >= r) if from_end else (c <= r)).astype(F32)
    return jnp.concatenate([jnp.dot(tri, x[lo:lo + CH], preferred_element_type=F32, precision=HIGHEST)
                            for lo in range(0, x.shape[0], CH)], axis=0)


def _gla_factors(qs, ks, bcs, bls, revs):
    r = lax.broadcasted_iota(jnp.int32, (CH, CH), 0)
    c = lax.broadcasted_iota(jnp.int32, (CH, CH), 1)
    keeps = [(c >= r) if rev else (c <= r) for rev in revs]
    qs, ks = [q.astype(F32) for q in qs], [k.astype(F32) for k in ks]
    qts = [q * GLA_SCALE * jnp.exp(bc) for q, bc in zip(qs, bcs)]
    kts = [k * jnp.exp(-bc) for k, bc in zip(ks, bcs)]
    khs = [k * jnp.exp(bl - bc) for k, bl, bc in zip(ks, bls, bcs)]
    gls = [jnp.exp(bl) for bl in bls]
    return qts, kts, gls, khs, keeps


def _gla_loads(ch):
    qs = [c[3][:, _hk(c[2])] for c in ch]
    ks = [c[4][:, _hk(c[2])] for c in ch]
    bcs = [c[6][:, _hk(c[2])] for c in ch]
    bls = [c[6][(0 if c[0] else CH - 1):(1 if c[0] else CH), _hk(c[2])] for c in ch]
    return qs, ks, bcs, bls


_NT = (((1,), (1,)), ((), ()))
_TN = (((0,), (0,)), ((), ()))


def _gla_specs(ncc, nc, rev, backward):
    def idx(s):
        return _chunk_order((nc - 1 - s) if backward else s, ncc, nc, rev)

    wk, wv = GH * GDK, GH * GDV
    qb, kb, vb = OFF["gq"] // wk, OFF["gk"] // wk, OFF["gv"] // wv
    lab = 1 if rev else 0
    q = pl.BlockSpec((CH, wk), lambda s: (idx(s), qb))
    k = pl.BlockSpec((CH, wk), lambda s: (idx(s), kb))
    v = pl.BlockSpec((CH, wv), lambda s: (idx(s), vb))
    la = pl.BlockSpec((CH, wk), lambda s: (idx(s), lab))
    o = pl.BlockSpec((CH, wv), lambda s: (idx(s), 0))
    dk = pl.BlockSpec((CH, wk), lambda s: (idx(s), 0))
    st = pl.BlockSpec((None, GH, GDV, GDK), lambda s: (idx(s), 0, 0, 0))
    return q, k, v, la, o, dk, st


def _gla_fwd(p, la, ncc, name):
    t = p.shape[0]
    nc = t // CH
    specs = [_gla_specs(ncc, nc, rev, False) for rev in (False, True)]

    def body(qf, kf, vf, laf, qb_, kb_, vb_, lab, of, stf, ob, stb, s_scr):
        @pl.when(pl.program_id(0) == 0)
        def _():
            s_scr[...] = jnp.zeros_like(s_scr)

        ch = _gla_chains(((False, qf, kf, vf, laf, of, stf), (True, qb_, kb_, vb_, lab, ob, stb)))
        qts, kts, gls, khs, keeps = _gla_factors(*_gla_loads(ch), [c[0] for c in ch])
        sts = [s_scr[c[1], c[2]] for c in ch]
        for c, st in zip(ch, sts):
            c[8][c[2]] = st
        vbs = [c[5][:, _hv(c[2])].astype(BF16) for c in ch]
        qbs = [qt.astype(BF16) for qt in qts]
        a_s = [jnp.where(keep, lax.dot_general(qb, kt.astype(BF16), _NT, preferred_element_type=F32), 0.0)
               for keep, qb, kt in zip(keeps, qbs, kts)]
        inter = [lax.dot_general(qb, st.astype(BF16), _NT, preferred_element_type=F32) for qb, st in zip(qbs, sts)]
        intra = [jnp.dot(a.astype(BF16), vb, preferred_element_type=F32) for a, vb in zip(a_s, vbs)]
        for c, x, y in zip(ch, inter, intra):
            c[7][:, _hv(c[2])] = (x + y).astype(BF16)
        upd = [lax.dot_general(vb, kh.astype(BF16), _TN, preferred_element_type=F32) for vb, kh in zip(vbs, khs)]
        for c, st, gl, u in zip(ch, sts, gls, upd):
            s_scr[c[1], c[2]] = st * gl + u

    o_shape = jax.ShapeDtypeStruct((t, GH * GDV), BF16)
    st_shape = jax.ShapeDtypeStruct((nc, GH, GDV, GDK), F32)
    return pl.pallas_call(
        body, name=name, grid=(nc,),
        in_specs=[sp for s_ in specs for sp in s_[:4]],
        out_specs=tuple(sp for s_ in specs for sp in (s_[4], s_[6])),
        out_shape=(o_shape, st_shape, o_shape, st_shape),
        scratch_shapes=[pltpu.VMEM((2, GH, GDV, GDK), F32)], compiler_params=_cparams(1, 32),
    )(p, p, p, la, p, p, p, la)


def _gla_bwd(p, la, do, stf, stb, ncc, name):
    t = p.shape[0]
    nc = t // CH
    specs = [_gla_specs(ncc, nc, rev, True) for rev in (False, True)]

    def mm(xs, ys, dims=None):
        if dims is None:
            return [jnp.dot(x, y, preferred_element_type=F32) for x, y in zip(xs, ys)]
        return [lax.dot_general(x, y, dims, preferred_element_type=F32) for x, y in zip(xs, ys)]

    def body(*refs):
        ins_f, ins_b, outs_f, outs_b, ds_scr = refs[0:6], refs[6:12], refs[12:16], refs[16:20], refs[20]

        @pl.when(pl.program_id(0) == 0)
        def _():
            ds_scr[...] = jnp.zeros_like(ds_scr)

        ch = _gla_chains(((False, *ins_f, *outs_f), (True, *ins_b, *outs_b)))
        revs = [c[0] for c in ch]
        loads = _gla_loads(ch)
        bcs = loads[2]
        qts, kts, gls, khs, keeps = _gla_factors(*loads, revs)
        stvs = [c[8][c[2]].astype(BF16) for c in ch]
        dsns = [ds_scr[c[1], c[2]] for c in ch]
        dsbs = [x.astype(BF16) for x in dsns]
        vbs = [c[5][:, _hv(c[2])].astype(BF16) for c in ch]
        dobs = [c[7][:, _hv(c[2])].astype(BF16) for c in ch]
        qbs, kbs = [x.astype(BF16) for x in qts], [x.astype(BF16) for x in kts]
        a_s = [jnp.where(keep, x, 0.0).astype(BF16) for keep, x in zip(keeps, mm(qbs, kbs, _NT))]
        das = [jnp.where(keep, x, 0.0).astype(BF16) for keep, x in zip(keeps, mm(dobs, vbs, _NT))]
        dqts = [x + y for x, y in zip(mm(dobs, stvs), mm(das, kbs))]
        dkhs = mm(vbs, dsbs)
        dkts = [x + dkh * gl for x, dkh, gl in zip(mm(das, qbs, _TN), dkhs, gls)]
        for c, x, y in zip(ch, mm(a_s, dobs, _TN), mm([kh.astype(BF16) for kh in khs], dsbs, _NT)):
            c[11][:, _hv(c[2])] = x + y
        for c, x, dsn, gl in zip(ch, mm(dobs, qbs, _TN), dsns, gls):
            ds_scr[c[1], c[2]] = x + dsn * gl
        dgls = [jnp.sum(c[8][c[2]] * dsn, axis=0, keepdims=True) + jnp.sum(dkh * kt, axis=0, keepdims=True)
                for c, dsn, dkh, kt in zip(ch, dsns, dkhs, kts)]
        row = lax.broadcasted_iota(jnp.int32, (CH, GDK), 0)
        dbcs = [dqt * qt - dkt * kt + jnp.where(row == (0 if rev else CH - 1), dgl * gl, 0.0)
                for rev, dqt, qt, dkt, kt, dgl, gl in zip(revs, dqts, qts, dkts, kts, dgls, gls)]
        for c, dbc, dqt, dkt, bc in zip(ch, dbcs, dqts, dkts, bcs):
            c[12][:, _hk(c[2])] = dbc
            c[9][:, _hk(c[2])] = dqt * (GLA_SCALE * jnp.exp(bc))
            c[10][:, _hk(c[2])] = dkt * jnp.exp(-bc)

    k_shape = jax.ShapeDtypeStruct((t, GH * GDK), F32)
    v_shape = jax.ShapeDtypeStruct((t, GH * GDV), F32)
    res = pl.pallas_call(
        body, name=name, grid=(nc,),
        in_specs=[sp for q_s, k_s, v_s, la_s, o_s, _, st_s in specs for sp in (q_s, k_s, v_s, la_s, o_s, st_s)],
        out_specs=tuple(sp for _, _, _, _, o_s, dk_s, _ in specs for sp in (dk_s, dk_s, o_s, dk_s)),
        out_shape=(k_shape, k_shape, v_shape, k_shape) * 2,
        scratch_shapes=[pltpu.VMEM((2, GH, GDV, GDK), F32)], compiler_params=_cparams(1, 32),
    )(p, p, p, la, do, stf, p, p, p, la, do, stb)
    return res[:4], res[4:]


def _gla_merge_bwd(gf, gb, z, p, wd, dp, name):
    t = p.shape[0]
    w2 = GH * GDK

    def body(dqf, dkf, dvf, dlf, dqb, dkb, dvb, dlb, z_ref, r_ref, w_ref, _, dp_ref, dr_ref, db_ref, dw_ref):
        i = pl.program_id(0)
        dp_ref[:, 0:D] = (dvf[...] + dvb[...]).astype(BF16)
        dp_ref[:, D:D + w2] = (dqf[...] + dqb[...]).astype(BF16)
        dp_ref[:, D + w2:D + 2 * w2] = (dkf[...] + dkb[...]).astype(BF16)
        zv = z_ref[...]
        dlf_, dlb_ = _chunk_sums(dlf[...], True), _chunk_sums(dlb[...], False)
        dz = jnp.concatenate([dlf_, dlb_], axis=1) * (_sigmoid(-zv) / GLA_TAU)
        dzb = dz.astype(BF16)
        dr_ref[...] = lax.dot_general(dzb, w_ref[...].astype(BF16), _NT, preferred_element_type=F32).astype(BF16)

        @pl.when(i == 0)
        def _():
            db_ref[...] = jnp.zeros_like(db_ref)
            dw_ref[...] = jnp.zeros_like(dw_ref)

        db_ref[...] += jnp.sum(dz, axis=0, keepdims=True)
        dw_ref[...] += lax.dot_general(r_ref[...].astype(BF16), dzb, _TN, preferred_element_type=F32)

    half = pl.BlockSpec((TM, w2), lambda i: (i, 0))
    row = pl.BlockSpec((TM, D), lambda i: (i, 0))
    wspec = pl.BlockSpec((R_PAD, D), lambda i: (0, 0))
    return pl.pallas_call(
        body, name=name, grid=(t // TM,),
        in_specs=[half, half, row, half, half, half, row, half, row, _pcol("r", R_PAD), wspec, _HBM],
        out_specs=(_dp_spec("gla"), pl.BlockSpec((TM, R_PAD), lambda i: (i, 0)),
                   pl.BlockSpec((8, D), lambda i: (0, 0)), wspec),
        out_shape=(jax.ShapeDtypeStruct(dp.shape, dp.dtype), jax.ShapeDtypeStruct((t, R_PAD), BF16),
                   jax.ShapeDtypeStruct((8, D), F32), jax.ShapeDtypeStruct((R_PAD, D), F32)),
        input_output_aliases={11: 0}, compiler_params=_cparams(1, 40),
    )(*gf, *gb, z, p, wd, dp)


def _dp_tail(dk, dv, dr, dp, name):
    t = dk.shape[0]
    wk = NKV * HD

    def body(dk_ref, dv_ref, dr_ref, _, dp_ref):
        dp_ref[:, 0:wk] = dk_ref[...]
        dp_ref[:, wk:2 * wk] = dv_ref[...].astype(BF16)
        dp_ref[:, 2 * wk:2 * wk + R_PAD] = dr_ref[...]
        dp_ref[:, 2 * wk + R_PAD:] = jnp.zeros((TM, DP_BLOCKS["tail"][1] - 2 * wk - R_PAD), BF16)

    kv = pl.BlockSpec((TM, wk), lambda i: (i, 0))
    return pl.pallas_call(
        body, name=name, grid=(t // TM,),
        in_specs=[kv, kv, pl.BlockSpec((TM, R_PAD), lambda i: (i, 0)), _HBM], out_specs=_dp_spec("tail"),
        out_shape=jax.ShapeDtypeStruct(dp.shape, dp.dtype), input_output_aliases={3: 0},
        compiler_params=_cparams(1, 32),
    )(dk, dv, dr, dp)


def _branch_fwd(att, of, ob, p, gla_g, name):
    t = p.shape[0]

    def body(att_ref, of_ref, ob_ref, za_ref, zg_ref, g_ref, yb_ref, yc_ref):
        za = za_ref[...].astype(F32)
        yb_ref[...] = (att_ref[...].astype(F32) * (za * _sigmoid(za))).astype(BF16)
        for h in range(GH):
            sl = slice(h * GDV, (h + 1) * GDV)
            o = of_ref[:, sl].astype(F32) + ob_ref[:, sl].astype(F32)
            n = o * lax.rsqrt(jnp.mean(o * o, axis=-1, keepdims=True) + EPS) * g_ref[...]
            zh = zg_ref[:, sl].astype(F32)
            yc_ref[:, sl] = (n * (zh * _sigmoid(zh))).astype(BF16)

    row = pl.BlockSpec((TM, D), lambda i: (i, 0))
    return pl.pallas_call(
        body, name=name, grid=(t // TM,),
        in_specs=[row, row, row, _pcol("z_attn", D), _pcol("zg", D), pl.BlockSpec((1, GDV), lambda i: (0, 0))],
        out_specs=(row, row),
        out_shape=(jax.ShapeDtypeStruct((t, D), BF16), jax.ShapeDtypeStruct((t, D), BF16)),
        compiler_params=_cparams(1, 40),
    )(att, of, ob, p, p, gla_g)


def _branch_bwd(dyb, dyc, att, of, ob, p, gla_g, dp, name):
    t = p.shape[0]

    def body(dyb_ref, dyc_ref, att_ref, of_ref, ob_ref, za_ref, zg_ref, g_ref, _, datt_ref, do_ref, dp_ref, dg_ref):
        i = pl.program_id(0)

        @pl.when(i == 0)
        def _():
            dg_ref[...] = jnp.zeros_like(dg_ref)

        za, dyb = za_ref[...].astype(F32), dyb_ref[...].astype(F32)
        sa = _sigmoid(za)
        datt_ref[...] = (dyb * (za * sa)).astype(BF16)
        dp_ref[:, 0:D] = (dyb * att_ref[...].astype(F32) * (sa * (1.0 + za * (1.0 - sa)))).astype(BF16)
        g = g_ref[...]
        for h in range(GH):
            sl = slice(h * GDV, (h + 1) * GDV)
            o = of_ref[:, sl].astype(F32) + ob_ref[:, sl].astype(F32)
            r = lax.rsqrt(jnp.mean(o * o, axis=-1, keepdims=True) + EPS)
            oh = o * r
            zh, dyc = zg_ref[:, sl].astype(F32), dyc_ref[:, sl].astype(F32)
            sg = _sigmoid(zh)
            dn = dyc * (zh * sg)
            dp_ref[:, D + h * GDV:D + (h + 1) * GDV] = (dyc * (oh * g) * (sg * (1.0 + zh * (1.0 - sg)))).astype(BF16)
            doh = dn * g
            do_ref[:, sl] = (r * (doh - oh * jnp.mean(doh * oh, axis=-1, keepdims=True))).astype(BF16)
            dg_ref[...] += jnp.sum(dn * oh, axis=0, keepdims=True)

    row = pl.BlockSpec((TM, D), lambda i: (i, 0))
    return pl.pallas_call(
        body, name=name, grid=(t // TM,),
        in_specs=[row, row, row, row, row, _pcol("z_attn", D), _pcol("zg", D), pl.BlockSpec((1, GDV), lambda i: (0, 0)),
                  _HBM],
        out_specs=(row, row, _dp_spec("branch"), pl.BlockSpec((8, GDV), lambda i: (0, 0))),
        out_shape=(jax.ShapeDtypeStruct((t, D), BF16), jax.ShapeDtypeStruct((t, D), BF16),
                   jax.ShapeDtypeStruct(dp.shape, dp.dtype), jax.ShapeDtypeStruct((8, GDV), F32)),
        input_output_aliases={8: 2}, compiler_params=_cparams(1, 48),
    )(dyb, dyc, att, of, ob, p, p, gla_g, dp)


def _merge_fwd(bra, brb, brc, p, b_gate, name):
    t = p.shape[0]
    mgb = OFF["mg"] // D

    def body(a_ref, b_ref, c_ref, ga_ref, gb_ref, gc_ref, bg_ref, m_ref):
        m_ref[...] = (_sigmoid(ga_ref[...].astype(F32) + bg_ref[:, 0:D]) * a_ref[...].astype(F32)
                      + _sigmoid(gb_ref[...].astype(F32) + bg_ref[:, D:2 * D]) * b_ref[...].astype(F32)
                      + _sigmoid(gc_ref[...].astype(F32) + bg_ref[:, 2 * D:3 * D]) * c_ref[...].astype(F32)).astype(BF16)

    row = pl.BlockSpec((TM, D), lambda i: (i, 0))
    gates = [pl.BlockSpec((TM, D), functools.partial(lambda i, b: (i, b), b=mgb + j)) for j in range(3)]
    return pl.pallas_call(
        body, name=name, grid=(t // TM,),
        in_specs=[row, row, row, *gates, pl.BlockSpec((1, 3 * D), lambda i: (0, 0))],
        out_specs=row, out_shape=jax.ShapeDtypeStruct((t, D), BF16), compiler_params=_cparams(1, 40),
    )(bra, brb, brc, p, p, p, b_gate)


def _merge_bwd(dm, bra, brb, brc, p, b_gate, name):
    t = p.shape[0]
    mgb = OFF["mg"] // D

    def body(dm_ref, a_ref, b_ref, c_ref, ga_ref, gb_ref, gc_ref, bg_ref, da_ref, db_ref, dc_ref, dmg_ref, dbg_ref):
        i = pl.program_id(0)

        @pl.when(i == 0)
        def _():
            dbg_ref[...] = jnp.zeros_like(dbg_ref)

        dm = dm_ref[...].astype(F32)
        for j, (br_ref, g_ref, d_ref) in enumerate(((a_ref, ga_ref, da_ref), (b_ref, gb_ref, db_ref), (c_ref, gc_ref, dc_ref))):
            sl = slice(j * D, (j + 1) * D)
            g = _sigmoid(g_ref[...].astype(F32) + bg_ref[:, sl])
            d_ref[...] = (dm * g).astype(BF16)
            dmg = dm * br_ref[...].astype(F32) * (g * (1.0 - g))
            dmg_ref[:, sl] = dmg.astype(BF16)
            dbg_ref[:, sl] += jnp.sum(dmg, axis=0, keepdims=True)

    row = pl.BlockSpec((TM, D), lambda i: (i, 0))
    gates = [pl.BlockSpec((TM, D), functools.partial(lambda i, b: (i, b), b=mgb + j)) for j in range(3)]
    return pl.pallas_call(
        body, name=name, grid=(t // TM,),
        in_specs=[row, row, row, row, *gates, pl.BlockSpec((1, 3 * D), lambda i: (0, 0))],
        out_specs=(row, row, row, _dp_spec("merge"), pl.BlockSpec((8, 3 * D), lambda i: (0, 0))),
        out_shape=(jax.ShapeDtypeStruct((t, D), BF16),) * 3 + (jax.ShapeDtypeStruct((t, NP), BF16),
                                                                jax.ShapeDtypeStruct((8, 3 * D), F32)),
        compiler_params=_cparams(1, 48),
    )(dm, bra, brb, brc, p, p, p, b_gate)


def _adam_update(ns, g_ref, w_ref, m_ref, v_ref, go_ref, d_ref, mo_ref, vo_ref):
    g = g_ref[0].astype(F32)
    for s in range(1, ns):
        g = g + g_ref[s].astype(F32)
    mn = ADAM_B1 * m_ref[...] + (1.0 - ADAM_B1) * g
    vn = ADAM_B2 * v_ref[...] + (1.0 - ADAM_B2) * jnp.square(g)
    m_hat = mn / (1.0 - ADAM_B1 ** ADAM_STEP)
    v_hat = vn / (1.0 - ADAM_B2 ** ADAM_STEP)
    go_ref[...] = g
    d_ref[...] = -ADAM_LR * (m_hat / (jnp.sqrt(v_hat) + ADAM_EPS) + ADAM_WD * w_ref[...])
    mo_ref[...] = mn
    vo_ref[...] = vn


def _adamw(gsrc, w, m, v, name):
    ns, nl, r, c = gsrc.shape
    gb = gsrc.dtype.itemsize

    def fits(rows, cols):
        lanes = -(-cols // LANE) * LANE
        return ns * rows * lanes * gb <= ADAM_SRC_BYTES and rows * lanes * 4 <= ADAM_ROW_BYTES

    tr, tc = r, c
    if not fits(r, c):
        rows = [cand for cand in range(16, r, 16) if r % cand == 0 and fits(cand, c)]
        cols = [cand for cand in range(LANE, c, LANE) if c % cand == 0 and fits(r, cand)]
        if rows:
            tr = rows[-1]
        else:
            tc = cols[-1]

    def body(*refs):
        _adam_update(ns, *refs)

    row = pl.BlockSpec((None, tr, tc), lambda l, i, j: (l, i, j))
    return pl.pallas_call(
        body, name=name, grid=(nl, r // tr, c // tc),
        in_specs=[pl.BlockSpec((ns, None, tr, tc), lambda l, i, j: (0, l, i, j)), row, row, row],
        out_specs=(row,) * 4, out_shape=(jax.ShapeDtypeStruct((nl, r, c), F32),) * 4,
        compiler_params=_cparams(3, 48),
    )(gsrc, w, m, v)


def _pair_sum(a, b, name):
    s, r, c = a.shape
    tc = _pick(c, (256, 128))

    def body(a_ref, b_ref, o_ref):
        o_ref[...] = (a_ref[...].astype(F32) + b_ref[...].astype(F32)).astype(BF16)

    blk = pl.BlockSpec((None, r, tc), lambda i, j: (i, 0, j))
    return pl.pallas_call(
        body, name=name, grid=(s, c // tc), in_specs=[blk, blk], out_specs=blk,
        out_shape=jax.ShapeDtypeStruct(a.shape, BF16), compiler_params=_cparams(2, 32),
    )(a, b)


def _adamw_small(items, name):
    k = len(items)

    def body(*refs):
        for j in range(k):
            _adam_update(items[j][0].shape[0], *refs[4 * j:4 * j + 4], *refs[4 * k + 4 * j:4 * k + 4 * j + 4])

    out = pl.pallas_call(
        body, name=name,
        out_shape=tuple(jax.ShapeDtypeStruct(w.shape, F32) for _, w, _, _ in items for _ in range(4)),
    )(*[a for item in items for a in item])
    return [out[4 * j:4 * j + 4] for j in range(k)]


def _rope_tables(ctx, seq):
    n_rows = seq // GRID_W
    pairs = HD // 4
    row = jnp.repeat(jnp.arange(n_rows, dtype=F32), GRID_W)
    col = jnp.tile(jnp.arange(GRID_W, dtype=F32), n_rows)
    freqs = ROPE_THETA ** (-jnp.arange(pairs, dtype=F32) * 2.0 / (HD // 2))
    ar, ac = row[:, None] * freqs, col[:, None] * freqs
    cos_l = jnp.concatenate([jnp.cos(ar), jnp.cos(ar), jnp.cos(ac), jnp.cos(ac)], axis=1)
    sin_l = jnp.concatenate([-jnp.sin(ar), jnp.sin(ar), -jnp.sin(ac), jnp.sin(ac)], axis=1)
    cos_t = jnp.concatenate([jnp.ones((ctx, HD), F32), cos_l], axis=0)
    sin_t = jnp.concatenate([jnp.zeros((ctx, HD), F32), sin_l], axis=0)
    return cos_t, sin_t


def _to_proj_layout(wt):
    parts = [wt[s:s + wd] for _, s, wd in _SEGS]
    used = sum(wd for _, _, wd in _SEGS)
    parts.append(jnp.zeros((NP - used, wt.shape[1]), wt.dtype))
    return jnp.concatenate(parts, axis=0)


def _from_proj_layout(g):
    order = sorted(_SEGS, key=lambda sg: sg[1])
    return jnp.concatenate([g[OFF[n]:OFF[n] + wd] for n, _, wd in order], axis=0)


def _row0(a):
    return a[..., 0, :]


def kernel(x, c, ctx, c_ctx, w_ada, b_ada, g_pre, g_post, w_in, conv_w, q_norm_g, k_norm_g, w_decay_fwd, b_decay_fwd, w_decay_bwd, b_decay_bwd, gla_norm_g, w_br_conv, w_br_attn, w_br_gla, b_gate, w_out, loss_target, m_c_ctx, m_w_ada, m_b_ada, m_g_pre, m_g_post, m_w_in, m_conv_w, m_q_norm_g, m_k_norm_g, m_w_decay_fwd, m_b_decay_fwd, m_w_decay_bwd, m_b_decay_bwd, m_gla_norm_g, m_w_br_conv, m_w_br_attn, m_w_br_gla, m_b_gate, m_w_out, v_c_ctx, v_w_ada, v_b_ada, v_g_pre, v_g_post, v_w_in, v_conv_w, v_q_norm_g, v_k_norm_g, v_w_decay_fwd, v_b_decay_fwd, v_w_decay_bwd, v_b_decay_bwd, v_gla_norm_g, v_w_br_conv, v_w_br_attn, v_w_br_gla, v_b_gate, v_w_out):
    seq, n_ctx = x.shape[1], ctx.shape[1]
    assert n_ctx % TM == 0 and seq % TM == 0 and seq % GRID_W == 0
    t = n_ctx + seq
    nct, ncc = n_ctx // TM, n_ctx // CH
    dev = 4 * lax.axis_index("x") + 2 * lax.axis_index("y") + lax.axis_index("c")
    ada_w = w_ada.shape[2]
    in_w = w_in.shape[2]
    br_r = w_br_conv.shape[1]

    def in_t(a, l):
        return a.transpose(2, 0, 1)[:, l, :]

    wb = [w.astype(BF16) for w in (w_ada, w_br_conv, w_br_attn, w_br_gla, w_out)]
    wall = _all_gather([wb[0][0], in_t(w_in, 0).astype(BF16), conv_w, w_decay_fwd, w_decay_bwd],
                       "gather_first")
    later = _GatherRider([in_t(w_in, 1).astype(BF16), wb[0][1], wb[1], wb[2], wb[3], wb[4]])

    def full_small(g):
        return g.transpose(1, 2, 0, 3).reshape(DEPTH, g.shape[2], NDEV * g.shape[3])

    def full_in(g):
        return _to_proj_layout(g.reshape(IN_WIDTH, D))

    def full_ada(g):
        return g.transpose(1, 0, 2).reshape(D, 3 * D)

    w_ada_f = [full_ada(wall[0]), None]
    wp = [full_in(wall[1]), None]
    conv_f, wdf_f, wdb_f = full_small(wall[2]), full_small(wall[3]), full_small(wall[4])

    cos_t, sin_t = _rope_tables(n_ctx, seq)
    cc = jnp.concatenate([c_ctx[None, :], c.reshape(1, D), jnp.zeros((6, D), F32)], axis=0)
    silu_cc, dsilu_cc = _ada_in(cc)

    conv8, wd_pad, bd = [], [], []
    for l in range(DEPTH):
        conv8.append(jnp.concatenate([conv_f[l], jnp.zeros((5, D), F32)], axis=0))
        zr = jnp.zeros((GLA_RANK, GH * GDK), F32)
        wd_pad.append(jnp.concatenate([jnp.concatenate([wdf_f[l], zr], axis=1), jnp.concatenate([zr, wdb_f[l]], axis=1),
                                       jnp.zeros((R_PAD - 2 * GLA_RANK, D), F32)], axis=0))
        bd.append(jnp.concatenate([b_decay_fwd[l], b_decay_bwd[l]])[None, :])

    xs = jnp.concatenate([ctx[0], x[0]], axis=0)
    saved = []
    for l in range(DEPTH):
        n = f"l{l}_"
        mod = _mm(silu_cc, w_ada_f[l], n + "mod", bias=b_ada[l][None, :])
        mod3 = mod[0:2].reshape(2, 3, D)
        h = _prenorm_fwd(xs, g_pre[l][None, :], mod3, nct, n + "prenorm")
        p = _mm(h, wp[l], n + "proj", tb=True, out_dtype=BF16, tm=t // 2)
        cv, ya = _conv_fwd(p, conv8[l], nct, n + "conv")
        qr, kr = _qk_prep_fwd(p, q_norm_g[l][None, :], k_norm_g[l][None, :], cos_t, sin_t, n + "qk_prep")
        att, lse, *got = _attn_fwd(qr, kr, p, nct, n + "attn", rider=later if l == 0 else None)
        if l == 0:
            wp[1], w_ada_f[1] = full_in(got[0]), full_ada(got[1])
            w_brs_f = [g.transpose(1, 0, 2, 3).reshape(DEPTH, D, D) for g in got[2:]]
        z, la = _decay_fwd(p, wd_pad[l], bd[l], n + "decay")
        of, stf, ob, stb = _gla_fwd(p, la, ncc, n + "gla")
        yb, yc = _branch_fwd(att, of, ob, p, gla_norm_g[l][None, :], n + "branch")
        bra = _mm(ya, w_brs_f[0][l], n + "br_conv", out_dtype=BF16)
        brb = _mm(yb, w_brs_f[1][l], n + "br_attn", out_dtype=BF16)
        brc = _mm(yc, w_brs_f[2][l], n + "br_gla", out_dtype=BF16)
        mm_ = _merge_fwd(bra, brb, brc, p, b_gate[l][None, :], n + "merge")
        out = _mm(mm_, w_brs_f[3][l], n + "out")
        x_new = _post_fwd(xs, out, g_post[l][None, :], mod3, nct, n + "post")
        saved.append(dict(x=xs, mod3=mod3, h=h, p=p, cv=cv, ya=ya, qr=qr, kr=kr, att=att, lse=lse, z=z, la=la, of=of, ob=ob,
                          stf=stf, stb=stb, yb=yb, yc=yc, bra=bra, brb=brb, brc=brc, m=mm_, out=out))
        xs = x_new

    dx, sq = _loss_grad(xs, loss_target[0], nct, "loss")
    loss = lax.psum(0.5 * sq[0, 0] / D, ("x", "y", "c"))

    gw = {k: [None] * DEPTH for k in ("w_in", "br_conv", "br_attn", "br_gla", "out", "b_gate", "g_pre", "g_post",
                                      "conv_w", "qg", "kg", "wd", "bdec", "gla_g", "dmod")}
    dctx = []

    def in_slots(l):
        return _from_proj_layout(gw["w_in"][l]).reshape(NDEV, in_w, D)

    def br_slots():
        return [jnp.stack([gw[k][l].reshape(NDEV, br_r, D) for l in range(DEPTH)], axis=1)
                for k in ("br_conv", "br_attn", "br_gla", "out")]

    for l in reversed(range(DEPTH)):
        n = f"l{l}_b_"
        s = saved[l]
        p = s["p"]
        d_out, dgt, gw["g_post"][l] = _post_bwd(dx, s["out"], g_post[l][None, :], s["mod3"], nct, n + "post")
        dm = _mm(d_out, w_brs_f[3][l], n + "dm", tb=True, out_dtype=BF16)
        gw["out"][l] = _mm(s["m"], d_out, n + "dw_out", ta=True, out_dtype=BF16)
        dbra, dbrb, dbrc, dp, gw["b_gate"][l] = _merge_bwd(dm, s["bra"], s["brb"], s["brc"], p, b_gate[l][None, :], n + "merge")
        dya = _mm(dbra, w_brs_f[0][l], n + "dya", tb=True, out_dtype=BF16)
        dyb = _mm(dbrb, w_brs_f[1][l], n + "dyb", tb=True, out_dtype=BF16)
        dyc = _mm(dbrc, w_brs_f[2][l], n + "dyc", tb=True, out_dtype=BF16)
        gw["br_conv"][l] = _mm(s["ya"], dbra, n + "dw_conv", ta=True, out_dtype=BF16)
        gw["br_attn"][l] = _mm(s["yb"], dbrb, n + "dw_attn", ta=True, out_dtype=BF16)
        gw["br_gla"][l] = _mm(s["yc"], dbrc, n + "dw_gla", ta=True, out_dtype=BF16)
        dcv, dp = _conv_bwd_a(dya, p, s["cv"], dp, n + "conv_a")
        dp, gw["conv_w"][l] = _conv_bwd_b(dcv, p, conv8[l], nct, dp, n + "conv_b")
        datt, dgo, dp, gw["gla_g"][l] = _branch_bwd(dyb, dyc, s["att"], s["of"], s["ob"], p, gla_norm_g[l][None, :], dp, n + "branch")
        ex1 = _ExchangeRider([in_slots(DEPTH - 1)] + br_slots()) if l == 0 else None
        dqr, dkr, dv, *got = _attn_bwd(s["qr"], s["kr"], p, s["att"], s["lse"], datt, nct, n + "attn", rider=ex1)
        if l == 0:
            recv_in1, recv_br = got[0], got[1:]
        dp, dk, gw["qg"][l], gw["kg"][l] = _qk_prep_bwd(dqr, dkr, p, q_norm_g[l][None, :], k_norm_g[l][None, :], cos_t, sin_t, dp, n + "qk_prep")
        gf, gb = _gla_bwd(p, s["la"], dgo, s["stf"], s["stb"], ncc, n + "gla")
        dp, dr, gw["bdec"][l], gw["wd"][l] = _gla_merge_bwd(gf, gb, s["z"], p, wd_pad[l], dp, n + "gla_merge")
        dp = _dp_tail(dk, dv, dr, dp, n + "dp_tail")
        gw["w_in"][l] = _mm(dp, s["h"], n + "dw_in", ta=True, out_dtype=BF16, tk=t // 2 if t % 32 == 0 else None)
        if l == 0:
            core = lax.axis_index("c")
            halves = in_slots(0).reshape(NDEV // 2, 2, in_w, D)
            kept = lax.dynamic_index_in_dim(halves, core, axis=1, keepdims=False)
            sent = lax.dynamic_index_in_dim(halves, 1 - core, axis=1, keepdims=False)
            from_sibling, = _comm_alone(_SwapRider([sent]), n + "swap_dw_in")
            chip_sum = _pair_sum(kept, from_sibling, n + "chip_sum_dw_in")
            dh, recv_in0 = _mm(dp, wp[l], n + "dh", tk=NP // 4, rider=_ExchangeRider([chip_sum], chips_only=True))
        else:
            dh = _mm(dp, wp[l], n + "dh", tk=NP // 4)
        dx, dsh, dsc, gw["g_pre"][l] = _prenorm_bwd(dh, s["x"], dx, g_pre[l][None, :], s["mod3"], nct, n + "prenorm")
        dmod = jnp.stack([_row0(dsh), _row0(dsc), _row0(dgt)], axis=1).reshape(2, 3 * D)
        gw["dmod"][l] = dmod
        dmod8 = jnp.concatenate([dmod, jnp.zeros((6, 3 * D), F32)], axis=0)
        dctx.append(_mm(dmod8, w_ada_f[l], n + "dsilu", tb=True))
    grad_x = dx[n_ctx:][None]
    g_cctx = _cctx_grad(dctx[0], dctx[1], dsilu_cc)[0]

    def st2(name):
        return jnp.stack(gw[name])

    g_b_ada = jnp.stack([gw["dmod"][l][0] + gw["dmod"][l][1] for l in range(DEPTH)])
    g_bdf = jnp.stack([gw["bdec"][l][0, :GH * GDK] for l in range(DEPTH)])
    g_bdb = jnp.stack([gw["bdec"][l][0, GH * GDK:] for l in range(DEPTH)])
    g_wdf = jnp.stack([gw["wd"][l][0:GLA_RANK, :GH * GDK] for l in range(DEPTH)])
    g_wdb = jnp.stack([gw["wd"][l][GLA_RANK:2 * GLA_RANK, GH * GDK:] for l in range(DEPTH)])
    rep_grads = [g_cctx, g_b_ada, st2("g_pre")[:, 0], st2("g_post")[:, 0], st2("qg")[:, 0], st2("kg")[:, 0], g_bdf, g_bdb,
                 st2("gla_g")[:, 0], st2("b_gate")[:, 0]]
    rep_w = [c_ctx, b_ada, g_pre, g_post, q_norm_g, k_norm_g, b_decay_fwd, b_decay_bwd, gla_norm_g, b_gate]
    rep_m = [m_c_ctx, m_b_ada, m_g_pre, m_g_post, m_q_norm_g, m_k_norm_g, m_b_decay_fwd, m_b_decay_bwd, m_gla_norm_g, m_b_gate]
    rep_v = [v_c_ctx, v_b_ada, v_g_pre, v_g_post, v_q_norm_g, v_k_norm_g, v_b_decay_fwd, v_b_decay_bwd, v_gla_norm_g, v_b_gate]
    def two_d(a):
        return a.reshape(1, -1) if a.ndim == 1 else a

    def owner_slots(g):
        return g.reshape(DEPTH, g.shape[1], NDEV, g.shape[2] // NDEV).transpose(2, 0, 1, 3)

    n_rep = len(rep_grads)
    small = _comm_alone(_Riders([
        _GatherRider([two_d(g) for g in rep_grads] + [silu_cc[0:2], jnp.stack(gw["dmod"])]),
        _ExchangeRider([owner_slots(st2("conv_w")[:, 0:3]), owner_slots(g_wdf), owner_slots(g_wdb)])]),
        "exchange_small_grads")
    rep_src, (a_all, d_all), sh_src = small[:n_rep], small[n_rep:n_rep + 2], small[n_rep + 2:]
    sh_w = [conv_w, w_decay_fwd, w_decay_bwd]
    sh_m = [m_conv_w, m_w_decay_fwd, m_w_decay_bwd]
    sh_v = [v_conv_w, v_w_decay_fwd, v_w_decay_bwd]
    small_out = _adamw_small(
        [(g, two_d(w), two_d(m), two_d(v)) for g, w, m, v in zip(rep_src, rep_w, rep_m, rep_v)]
        + list(zip(sh_src, sh_w, sh_m, sh_v)), "adam_small")
    rep_g, rep_d, rep_nm, rep_nv = [[small_out[j][k].reshape(rep_w[j].shape) for j in range(n_rep)] for k in range(4)]
    sh_gr, sh_d, sh_nm, sh_nv = [[small_out[n_rep + j][k] for j in range(len(sh_w))] for k in range(4)]

    a_all = a_all.reshape(NDEV * 2, D)
    d_all = d_all.transpose(1, 0, 2, 3).reshape(DEPTH, NDEV * 2, 3 * D)
    g_ada = jnp.stack([_mm(a_all, lax.dynamic_slice_in_dim(d_all[l], dev * ada_w, ada_w, axis=1), f"dw_ada{l}",
                           ta=True, precise=True, tk=NDEV * 2) for l in range(DEPTH)])
    ada_g, ada_d, ada_nm, ada_nv = _adamw(g_ada[None], w_ada, m_w_ada, v_w_ada, "adam_ada")

    big_w = [w_br_conv, w_br_attn, w_br_gla, w_out]
    big_m = [m_w_br_conv, m_w_br_attn, m_w_br_gla, m_w_out]
    big_v = [v_w_br_conv, v_w_br_attn, v_w_br_gla, v_w_out]
    big_out = [_adamw(recv_br[j], big_w[j], big_m[j], big_v[j], f"adam_big{j}") for j in range(len(big_w))]
    in_out = [_adamw(r_[:, None], in_t(w_in, l)[None], in_t(m_w_in, l)[None], in_t(v_w_in, l)[None], f"adam_in{l}")
              for l, r_ in enumerate((recv_in0, recv_in1))]
    in_res = [jnp.stack([in_out[l][k][0] for l in range(DEPTH)], axis=1).transpose(1, 2, 0) for k in range(4)]
    big_g, big_d, big_nm, big_nv = [[in_res[k]] + [o[k] for o in big_out] for k in range(4)]

    def ordered(rep, ada, big, sh):
        c_ctx_, b_ada_, g_pre_, g_post_, qg_, kg_, bdf_, bdb_, glag_, bgate_ = rep
        w_in_, brc_, bra_, brg_, wout_ = big
        conv_, wdf_, wdb_ = sh
        return [c_ctx_, ada, b_ada_, g_pre_, g_post_, w_in_, conv_, qg_, kg_, wdf_, bdf_, wdb_, bdb_, glag_,
                brc_, bra_, brg_, bgate_, wout_]

    return (loss, grad_x,
            *ordered(rep_g, ada_g, big_g, sh_gr), *ordered(rep_d, ada_d, big_d, sh_d),
            *ordered(rep_nm, ada_nm, big_nm, sh_nm), *ordered(rep_nv, ada_nv, big_nv, sh_nv))
```

```python
import functools

import numpy as np
import jax
import jax.numpy as jnp
from jax import lax
from jax.experimental import pallas as pl
from jax.experimental.pallas import tpu as pltpu

F32, BF16 = jnp.float32, jnp.bfloat16
HIGHEST = lax.Precision.HIGHEST

D = 1024
DEPTH = 2
GRID_W = 64
NH, NKV, HD = 8, 2, 128
GROUP = NH // NKV
ROPE_THETA = 10000.0
ATTN_SCALE = HD ** -0.5
Q_FOLD = ATTN_SCALE * 1.4426950408889634
P_HALO = 16
GH, GDK, GDV = 4, 128, 256
GLA_RANK = 16
GLA_TAU = 16.0
CH = 64
GLA_SCALE = GDK ** -0.5
EPS = 1e-6
NDEV = 8
LANE = 128
TM = 256
ATTN_HEADS_PER_STEP = 4
ATTN_FWD_KEY_CHUNK = 2176
ATTN_BWD_KEY_CHUNK = 256
KEY_ALIGN = LANE

ADAM_LR, ADAM_B1, ADAM_B2, ADAM_EPS, ADAM_WD, ADAM_STEP = 0.001, 0.9, 0.999, 1e-08, 0.01, 10

_SEGS = (("a_b", 0, 1024), ("a_z", 3072, 1024), ("a_c", 1024, 1024), ("a_x", 2048, 1024),
         ("z_attn", 5632, 1024), ("zg", 8736, 1024), ("gv", 7680, 1024), ("gq", 6656, 512), ("gk", 7168, 512),
         ("q", 4096, 1024), ("mg", 9760, 3072), ("k", 5120, 256), ("v", 5376, 256), ("r", 8704, 32))
DP_BLOCKS = {"conv_a": ("a_b", 2048), "conv_b": ("a_c", 2048), "branch": ("z_attn", 2048), "gla": ("gv", 2048),
             "q": ("q", 1024), "merge": ("mg", 3072), "tail": ("k", 1024)}
IN_WIDTH = 12832
NP = 13312
OFF = {}
_o = 0
for _n, _s, _w in _SEGS:
    OFF[_n] = _o
    _o += _w
R_PAD = 128


def _cparams(ngrid, vmem_mb):
    return pltpu.CompilerParams(dimension_semantics=("arbitrary",) * ngrid, vmem_limit_bytes=vmem_mb << 20)


def _pick(n, cands):
    for c in cands:
        if n % c == 0:
            return c
    return n


def _sigmoid(x):
    return 1.0 / (1.0 + jnp.exp(-x))


ADAM_SRC_BYTES = 8 << 20
ADAM_ROW_BYTES = 1 << 20


def _all_gather(xs, name):
    return _comm_alone(_GatherRider(xs), name)


_HBM = pl.BlockSpec(memory_space=pl.ANY)


class _Rider:
    def __init__(self, xs, out_shapes, remote_copies=NDEV - 1):
        self.xs, self.n = list(xs), len(xs)
        self.out_shape = [jax.ShapeDtypeStruct(s, x.dtype) for s, x in zip(out_shapes, xs)]
        self.scratch = [pltpu.SemaphoreType.DMA((remote_copies * self.n,)),
                        pltpu.SemaphoreType.DMA((remote_copies * self.n,)), pltpu.SemaphoreType.DMA((self.n,))]


class _GatherRider(_Rider):
    def __init__(self, xs):
        super().__init__(xs, [(NDEV,) + x.shape for x in xs])

    def _parts(self, x_refs, out_refs, sems):
        n = self.n
        send_sems, recv_sems, local_sems = sems
        mx, my, mc = lax.axis_index("x"), lax.axis_index("y"), lax.axis_index("c")
        me, sibling = (mx, my, mc), (mx, my, 1 - mc)
        chips = [(1 - mx, my), (mx, 1 - my), (1 - mx, 1 - my)]

        def slot(a, px, py, pc):
            return out_refs[a].at[4 * px + 2 * py + pc]

        def copy(k, a, block, to, own=False):
            return pltpu.make_async_remote_copy(
                src_ref=x_refs[a] if own else slot(a, *block), dst_ref=slot(a, *block),
                send_sem=send_sems.at[k * n + a], recv_sem=recv_sems.at[k * n + a],
                device_id=to, device_id_type=pl.DeviceIdType.MESH)

        mine = [pltpu.make_async_copy(x_refs[a], slot(a, *me), local_sems.at[a]) for a in range(n)]
        first = [copy(0, a, me, sibling, own=True) for a in range(n)]
        first += [copy(1 + j, a, me, (*chip, mc), own=True) for a in range(n) for j, chip in enumerate(chips)]
        landed = [copy(1 + j, a, (*chip, mc), me) for a in range(n) for j, chip in enumerate(chips)]
        passed = [copy(4 + j, a, (*chip, mc), sibling) for a in range(n) for j, chip in enumerate(chips)]
        from_sibling = [copy(0, a, sibling, me) for a in range(n)]
        from_sibling += [copy(4 + j, a, (*chip, 1 - mc), me) for a in range(n) for j, chip in enumerate(chips)]
        return mine, first, landed, passed, from_sibling

    def start(self, x_refs, out_refs, sems):
        mine, first, _, _, _ = self._parts(x_refs, out_refs, sems)
        for cp in mine + first:
            cp.start()

    def middle(self, x_refs, out_refs, sems):
        _, _, landed, passed, _ = self._parts(x_refs, out_refs, sems)
        for got, fwd in zip(landed, passed):
            got.wait_recv()
            fwd.start()

    def finish(self, x_refs, out_refs, sems):
        mine, first, _, passed, from_sibling = self._parts(x_refs, out_refs, sems)
        for cp in from_sibling:
            cp.wait_recv()
        for cp in first + passed:
            cp.wait_send()
        for cp in mine:
            cp.wait()


class _ExchangeRider(_Rider):
    def __init__(self, xs, chips_only=False):
        self.chips_only = chips_only
        super().__init__(xs, [x.shape for x in xs], 3 if chips_only else NDEV - 1)

    def _parts(self, x_refs, out_refs, sems):
        n = self.n
        send_sems, recv_sems, local_sems = sems
        mx, my, mc = lax.axis_index("x"), lax.axis_index("y"), lax.axis_index("c")
        me = 2 * mx + my if self.chips_only else 4 * mx + 2 * my + mc
        mine = [pltpu.make_async_copy(x_refs[a].at[me], out_refs[a].at[me], local_sems.at[a]) for a in range(n)]
        copies = []
        for a in range(n):
            for rel in range(1, 4 if self.chips_only else NDEV):
                bits = rel << 1 if self.chips_only else rel
                px = (1 - mx) if bits & 4 else mx
                py = (1 - my) if bits & 2 else my
                pc = (1 - mc) if bits & 1 else mc
                peer = 2 * px + py if self.chips_only else 4 * px + 2 * py + pc
                k = (rel - 1) * n + a
                copies.append(pltpu.make_async_remote_copy(
                    src_ref=x_refs[a].at[peer], dst_ref=out_refs[a].at[me],
                    send_sem=send_sems.at[k], recv_sem=recv_sems.at[k],
                    device_id=(px, py, pc), device_id_type=pl.DeviceIdType.MESH))
        return mine, copies

    def start(self, x_refs, out_refs, sems):
        mine, copies = self._parts(x_refs, out_refs, sems)
        for cp in mine + copies:
            cp.start()

    def middle(self, x_refs, out_refs, sems):
        pass

    def finish(self, x_refs, out_refs, sems):
        mine, copies = self._parts(x_refs, out_refs, sems)
        for cp in copies:
            cp.wait_recv()
        for cp in copies:
            cp.wait_send()
        for cp in mine:
            cp.wait()


class _SwapRider(_Rider):
    def __init__(self, xs):
        super().__init__(xs, [x.shape for x in xs], 1)

    def _parts(self, x_refs, out_refs, sems):
        send_sems, recv_sems, _ = sems
        sibling = (lax.axis_index("x"), lax.axis_index("y"), 1 - lax.axis_index("c"))
        return [pltpu.make_async_remote_copy(
            src_ref=x_refs[a], dst_ref=out_refs[a], send_sem=send_sems.at[a], recv_sem=recv_sems.at[a],
            device_id=sibling, device_id_type=pl.DeviceIdType.MESH) for a in range(self.n)]

    def start(self, x_refs, out_refs, sems):
        for cp in self._parts(x_refs, out_refs, sems):
            cp.start()

    def middle(self, x_refs, out_refs, sems):
        pass

    def finish(self, x_refs, out_refs, sems):
        copies = self._parts(x_refs, out_refs, sems)
        for cp in copies:
            cp.wait_recv()
        for cp in copies:
            cp.wait_send()


class _Riders:
    def __init__(self, riders):
        self.riders = list(riders)
        self.xs = [x for r in self.riders for x in r.xs]
        self.n = len(self.xs)
        self.out_shape = [s for r in self.riders for s in r.out_shape]
        self.scratch = [s for r in self.riders for s in r.scratch]

    def _each(self, method, x_refs, out_refs, sems):
        a = b = 0
        for r in self.riders:
            getattr(r, method)(x_refs[a:a + r.n], out_refs[a:a + r.n], sems[b:b + len(r.scratch)])
            a, b = a + r.n, b + len(r.scratch)

    def start(self, *refs):
        self._each("start", *refs)

    def middle(self, *refs):
        self._each("middle", *refs)

    def finish(self, *refs):
        self._each("finish", *refs)


def _comm_alone(rider, name):
    n = rider.n

    def body(*refs):
        x_refs, out_refs, sems = refs[:n], refs[n:2 * n], refs[2 * n:]
        rider.start(x_refs, out_refs, sems)
        rider.middle(x_refs, out_refs, sems)
        rider.finish(x_refs, out_refs, sems)

    return pl.pallas_call(
        body, name=name, out_shape=tuple(rider.out_shape), in_specs=[_HBM] * n, out_specs=(_HBM,) * n,
        scratch_shapes=rider.scratch,
    )(*rider.xs)


def _with_rider(body, nin, nout, rider, first, mid, last):
    if rider is None:
        return body
    n = rider.n

    def wrapped(*refs):
        ins, x_refs = refs[:nin], refs[nin:nin + n]
        outs, out_refs = refs[nin + n:nin + n + nout], refs[nin + n + nout:nin + 2 * n + nout]
        ns = len(rider.scratch)
        scratch, sems = refs[nin + 2 * n + nout:len(refs) - ns], refs[len(refs) - ns:]

        @pl.when(first())
        def _():
            rider.start(x_refs, out_refs, sems)

        body(*ins, *outs, *scratch)

        @pl.when(mid())
        def _():
            rider.middle(x_refs, out_refs, sems)

        @pl.when(last())
        def _():
            rider.finish(x_refs, out_refs, sems)

    return wrapped


def _mm(a, b, name, ta=False, tb=False, out_dtype=F32, bias=None, precise=False, tm=None, tn=None, tk=None, rider=None):
    m, k = (a.shape[1], a.shape[0]) if ta else a.shape
    n = b.shape[0] if tb else b.shape[1]
    assert k == (b.shape[1] if tb else b.shape[0])
    tm = tm or _pick(m, (1088, 1024, 512, 256, 128))
    tn = tn or _pick(n, (1024, 512, 384, 256, 128))
    tk = tk or _pick(k, (1024, 1088, 512, 256, 128))
    nk = k // tk
    dn = (((0 if ta else 1,), (1 if tb else 0,)), ((), ()))

    def body(*refs):
        if bias is None:
            a_ref, b_ref, o_ref = refs[:3]
            bias_ref = None
        else:
            a_ref, b_ref, bias_ref, o_ref = refs[:4]
        x, y = a_ref[...], b_ref[...]
        if precise:
            p = lax.dot_general(x.astype(F32), y.astype(F32), dn, preferred_element_type=F32, precision=HIGHEST)
        else:
            p = lax.dot_general(x.astype(BF16), y.astype(BF16), dn, preferred_element_type=F32)

        def finish(acc):
            if bias_ref is not None:
                acc = acc + bias_ref[...]
            o_ref[...] = acc.astype(out_dtype)

        if nk == 1:
            finish(p)
        else:
            acc_ref = refs[-1]
            kk = pl.program_id(2)

            @pl.when(kk == 0)
            def _():
                acc_ref[...] = p

            @pl.when(kk > 0)
            def _():
                acc_ref[...] += p

            @pl.when(kk == nk - 1)
            def _():
                finish(acc_ref[...])

    a_spec = pl.BlockSpec((tk, tm), lambda i, j, kk: (kk, i)) if ta else pl.BlockSpec((tm, tk), lambda i, j, kk: (i, kk))
    b_spec = pl.BlockSpec((tn, tk), lambda i, j, kk: (j, kk)) if tb else pl.BlockSpec((tk, tn), lambda i, j, kk: (kk, j))
    in_specs = [a_spec, b_spec]
    args = [a, b]
    if bias is not None:
        in_specs.append(pl.BlockSpec((1, tn), lambda i, j, kk: (0, j)))
        args.append(bias)
    grid = (m // tm, n // tn, nk)
    out_spec = pl.BlockSpec((tm, tn), lambda i, j, kk: (i, j))
    scratch = [pltpu.VMEM((tm, tn), F32)] if nk > 1 else []
    if rider is None:
        return pl.pallas_call(
            body, name=name, grid=grid, in_specs=in_specs, out_specs=out_spec,
            out_shape=jax.ShapeDtypeStruct((m, n), out_dtype), scratch_shapes=scratch, compiler_params=_cparams(3, 56),
        )(*args)

    def at(step):
        return lambda: ((pl.program_id(0) == step[0]) & (pl.program_id(1) == step[1]) & (pl.program_id(2) == step[2]))

    end = tuple(g - 1 for g in grid)
    return pl.pallas_call(
        _with_rider(body, len(args), 1, rider, at((0, 0, 0)), at((grid[0] // 2, 0, 0)), at(end)),
        name=name, grid=grid, in_specs=in_specs + [_HBM] * rider.n, out_specs=(out_spec,) + (_HBM,) * rider.n,
        out_shape=(jax.ShapeDtypeStruct((m, n), out_dtype),) + tuple(rider.out_shape),
        scratch_shapes=scratch + rider.scratch, compiler_params=_cparams(3, 56),
    )(*args, *rider.xs)


def _ada_in(cc):
    def body(c_ref, s_ref, d_ref):
        x = c_ref[...]
        sg = _sigmoid(x)
        s_ref[...] = x * sg
        d_ref[...] = sg * (1.0 + x * (1.0 - sg))

    return pl.pallas_call(body, name="ada_in", out_shape=(jax.ShapeDtypeStruct(cc.shape, F32),) * 2)(cc)


def _cctx_grad(t0, t1, dsilu):
    def body(a_ref, b_ref, d_ref, o_ref):
        o_ref[...] = (a_ref[...] + b_ref[...]) * d_ref[...]

    return pl.pallas_call(body, name="cctx_grad", out_shape=jax.ShapeDtypeStruct(t0.shape, F32))(t0, t1, dsilu)


def _seg_spec(nct, rows=3):
    return pl.BlockSpec((None, rows, D), lambda i: (jnp.where(i >= nct, 1, 0), 0, 0))


def _prenorm_fwd(x, g_pre, mod3, nct, name):
    t = x.shape[0]

    def body(x_ref, g_ref, mod_ref, h_ref):
        xv = x_ref[...]
        r = lax.rsqrt(jnp.mean(xv * xv, axis=-1, keepdims=True) + EPS)
        y = xv * r * g_ref[...]
        h_ref[...] = (y * (1.0 + mod_ref[1:2, :]) + mod_ref[0:1, :]).astype(BF16)

    return pl.pallas_call(
        body, name=name, grid=(t // TM,),
        in_specs=[pl.BlockSpec((TM, D), lambda i: (i, 0)), pl.BlockSpec((1, D), lambda i: (0, 0)), _seg_spec(nct)],
        out_specs=pl.BlockSpec((TM, D), lambda i: (i, 0)),
        out_shape=jax.ShapeDtypeStruct((t, D), BF16), compiler_params=_cparams(1, 32),
    )(x, g_pre, mod3)


def _prenorm_bwd(dh, x, dxo, g_pre, mod3, nct, name):
    t = x.shape[0]

    def body(dh_ref, x_ref, dxo_ref, g_ref, mod_ref, dx_ref, dsh_ref, dsc_ref, dg_ref):
        i = pl.program_id(0)
        xv, dhv, g = x_ref[...], dh_ref[...], g_ref[...]
        r = lax.rsqrt(jnp.mean(xv * xv, axis=-1, keepdims=True) + EPS)
        xh = xv * r
        dy = dhv * (1.0 + mod_ref[1:2, :])
        dxh = dy * g
        dx_ref[...] = dxo_ref[...] + r * (dxh - xh * jnp.mean(dxh * xh, axis=-1, keepdims=True))

        @pl.when((i == 0) | (i == nct))
        def _():
            dsh_ref[...] = jnp.zeros_like(dsh_ref)
            dsc_ref[...] = jnp.zeros_like(dsc_ref)

        @pl.when(i == 0)
        def _():
            dg_ref[...] = jnp.zeros_like(dg_ref)

        dsh_ref[...] += jnp.sum(dhv, axis=0, keepdims=True)
        dsc_ref[...] += jnp.sum(dhv * (xh * g), axis=0, keepdims=True)
        dg_ref[...] += jnp.sum(dy * xh, axis=0, keepdims=True)

    row = pl.BlockSpec((TM, D), lambda i: (i, 0))
    seg8 = pl.BlockSpec((None, 8, D), lambda i: (jnp.where(i >= nct, 1, 0), 0, 0))
    return pl.pallas_call(
        body, name=name, grid=(t // TM,),
        in_specs=[row, row, row, pl.BlockSpec((1, D), lambda i: (0, 0)), _seg_spec(nct)],
        out_specs=(row, seg8, seg8, pl.BlockSpec((8, D), lambda i: (0, 0))),
        out_shape=(jax.ShapeDtypeStruct((t, D), F32), jax.ShapeDtypeStruct((2, 8, D), F32),
                   jax.ShapeDtypeStruct((2, 8, D), F32), jax.ShapeDtypeStruct((8, D), F32)),
        compiler_params=_cparams(1, 32),
    )(dh, x, dxo, g_pre, mod3)


def _post_fwd(x, out, g_post, mod3, nct, name):
    t = x.shape[0]

    def body(x_ref, o_ref, g_ref, mod_ref, y_ref):
        ov = o_ref[...]
        r = lax.rsqrt(jnp.mean(ov * ov, axis=-1, keepdims=True) + EPS)
        y_ref[...] = x_ref[...] + mod_ref[2:3, :] * (ov * r * g_ref[...])

    row = pl.BlockSpec((TM, D), lambda i: (i, 0))
    return pl.pallas_call(
        body, name=name, grid=(t // TM,),
        in_specs=[row, row, pl.BlockSpec((1, D), lambda i: (0, 0)), _seg_spec(nct)],
        out_specs=row, out_shape=jax.ShapeDtypeStruct((t, D), F32), compiler_params=_cparams(1, 32),
    )(x, out, g_post, mod3)


def _post_bwd(dxo, out, g_post, mod3, nct, name):
    t = out.shape[0]

    def body(dx_ref, o_ref, g_ref, mod_ref, do_ref, dgt_ref, dg_ref):
        i = pl.program_id(0)
        ov, dxv, g = o_ref[...], dx_ref[...], g_ref[...]
        r = lax.rsqrt(jnp.mean(ov * ov, axis=-1, keepdims=True) + EPS)
        nh = ov * r
        dn = dxv * mod_ref[2:3, :]
        dnh = dn * g
        do_ref[...] = (r * (dnh - nh * jnp.mean(dnh * nh, axis=-1, keepdims=True))).astype(BF16)

        @pl.when((i == 0) | (i == nct))
        def _():
            dgt_ref[...] = jnp.zeros_like(dgt_ref)

        @pl.when(i == 0)
        def _():
            dg_ref[...] = jnp.zeros_like(dg_ref)

        dgt_ref[...] += jnp.sum(dxv * (nh * g), axis=0, keepdims=True)
        dg_ref[...] += jnp.sum(dn * nh, axis=0, keepdims=True)

    row = pl.BlockSpec((TM, D), lambda i: (i, 0))
    seg8 = pl.BlockSpec((None, 8, D), lambda i: (jnp.where(i >= nct, 1, 0), 0, 0))
    return pl.pallas_call(
        body, name=name, grid=(t // TM,),
        in_specs=[row, row, pl.BlockSpec((1, D), lambda i: (0, 0)), _seg_spec(nct)],
        out_specs=(row, seg8, pl.BlockSpec((8, D), lambda i: (0, 0))),
        out_shape=(jax.ShapeDtypeStruct((t, D), BF16), jax.ShapeDtypeStruct((2, 8, D), F32),
                   jax.ShapeDtypeStruct((8, D), F32)),
        compiler_params=_cparams(1, 32),
    )(dxo, out, g_post, mod3)


def _loss_grad(y, target, nct, name):
    t = y.shape[0]

    def body(y_ref, t_ref, dy_ref, l_ref):
        i = pl.program_id(0)

        @pl.when(i == 0)
        def _():
            l_ref[...] = jnp.zeros_like(l_ref)

        @pl.when(i < nct)
        def _():
            dy_ref[...] = jnp.zeros_like(dy_ref)

        @pl.when(i >= nct)
        def _():
            err = y_ref[...] - t_ref[...]
            dy_ref[...] = err / D
            l_ref[...] += jnp.sum(jnp.sum(err * err, axis=1, keepdims=True), axis=0, keepdims=True)

    row = pl.BlockSpec((TM, D), lambda i: (i, 0))
    return pl.pallas_call(
        body, name=name, grid=(t // TM,),
        in_specs=[row, pl.BlockSpec((TM, D), lambda i: (jnp.maximum(i - nct, 0), 0))],
        out_specs=(row, pl.BlockSpec((8, LANE), lambda i: (0, 0))),
        out_shape=(jax.ShapeDtypeStruct((t, D), F32), jax.ShapeDtypeStruct((8, LANE), F32)),
        compiler_params=_cparams(1, 32),
    )(y, target)


def _pcol(name, width):
    assert OFF[name] % width == 0
    blk = OFF[name] // width
    return pl.BlockSpec((TM, width), lambda i: (i, blk))


def _shift_rows(u, prev_row, next_row):
    n = u.shape[0]
    row = lax.broadcasted_iota(jnp.int32, u.shape, 0)
    prev = jnp.where(row == 0, prev_row, pltpu.roll(u, 1, 0))
    nxt = jnp.where(row == n - 1, next_row, pltpu.roll(u, n - 1, 0))
    return prev, nxt


def _halo_specs(width, nt, blk=0, rows=8):
    per = TM // rows
    prev = pl.BlockSpec((rows, width), lambda i: (jnp.maximum(i * per - 1, 0), blk))
    nxt = pl.BlockSpec((rows, width), lambda i: (jnp.minimum((i + 1) * per, nt * per - 1), blk))
    return prev, nxt


def _conv_fwd(p, conv_w8, nct, name):
    t = p.shape[0]
    nt = t // TM

    def body(ab_ref, ac_ref, ax_ref, az_ref, acp_ref, axp_ref, acn_ref, axn_ref, w_ref, cv_ref, ya_ref):
        i = pl.program_id(0)
        def f(ref, rows=slice(None)):
            return ref[rows, :].astype(F32)

        u = f(ac_ref) * f(ax_ref)
        mp = jnp.where((i == 0) | (i == nct), 0.0, 1.0)
        mn = jnp.where((i == nct - 1) | (i == nt - 1), 0.0, 1.0)
        last, first = slice(P_HALO - 1, P_HALO), slice(0, 1)
        prev, nxt = _shift_rows(u, f(acp_ref, last) * f(axp_ref, last) * mp, f(acn_ref, first) * f(axn_ref, first) * mn)
        cv = w_ref[0:1, :] * prev + w_ref[1:2, :] * u + w_ref[2:3, :] * nxt
        az = f(az_ref)
        cv_ref[...] = cv.astype(BF16)
        ya_ref[...] = (f(ab_ref) * cv * (az * _sigmoid(az))).astype(BF16)

    acp, acn = _halo_specs(D, nt, OFF["a_c"] // D, P_HALO)
    axp, axn = _halo_specs(D, nt, OFF["a_x"] // D, P_HALO)
    row = pl.BlockSpec((TM, D), lambda i: (i, 0))
    return pl.pallas_call(
        body, name=name, grid=(nt,),
        in_specs=[_pcol("a_b", D), _pcol("a_c", D), _pcol("a_x", D), _pcol("a_z", D), acp, axp, acn, axn,
                  pl.BlockSpec((8, D), lambda i: (0, 0))],
        out_specs=(row, row),
        out_shape=(jax.ShapeDtypeStruct((t, D), BF16), jax.ShapeDtypeStruct((t, D), BF16)),
        compiler_params=_cparams(1, 40),
    )(p, p, p, p, p, p, p, p, conv_w8)


def _dp_spec(key):
    seg, width = DP_BLOCKS[key]
    assert OFF[seg] % width == 0
    blk = OFF[seg] // width
    return pl.BlockSpec((TM, width), lambda i: (i, blk))


def _conv_bwd_a(dya, p, cv, dp, name):
    t = p.shape[0]

    def body(dy_ref, ab_ref, az_ref, cv_ref, _, dcv_ref, dp_ref):
        dy, ab = dy_ref[...].astype(F32), ab_ref[...].astype(F32)
        az, c = az_ref[...].astype(F32), cv_ref[...].astype(F32)
        sg = _sigmoid(az)
        sz = az * sg
        dcv_ref[...] = dy * ab * sz
        dp_ref[:, 0:D] = (dy * c * sz).astype(BF16)
        dp_ref[:, D:2 * D] = (dy * ab * c * (sg * (1.0 + az * (1.0 - sg)))).astype(BF16)

    row = pl.BlockSpec((TM, D), lambda i: (i, 0))
    return pl.pallas_call(
        body, name=name, grid=(t // TM,),
        in_specs=[row, _pcol("a_b", D), _pcol("a_z", D), row, _HBM], out_specs=(row, _dp_spec("conv_a")),
        out_shape=(jax.ShapeDtypeStruct((t, D), F32), jax.ShapeDtypeStruct(dp.shape, dp.dtype)),
        input_output_aliases={4: 1}, compiler_params=_cparams(1, 40),
    )(dya, p, p, cv, dp)


def _conv_bwd_b(dcv, p, conv_w8, nct, dp, name):
    t = p.shape[0]
    nt = t // TM

    def body(dcv_ref, hp_ref, hn_ref, ac_ref, ax_ref, w_ref, _, dp_ref, dw_ref):
        i = pl.program_id(0)
        d, ac, ax = dcv_ref[...], ac_ref[...].astype(F32), ax_ref[...].astype(F32)
        u = ac * ax
        mp = jnp.where((i == 0) | (i == nct), 0.0, 1.0)
        mn = jnp.where((i == nct - 1) | (i == nt - 1), 0.0, 1.0)
        dprev, dnxt = _shift_rows(d, hp_ref[7:8, :] * mp, hn_ref[0:1, :] * mn)
        du = w_ref[0:1, :] * dnxt + w_ref[1:2, :] * d + w_ref[2:3, :] * dprev
        dp_ref[:, 0:D] = (du * ax).astype(BF16)
        dp_ref[:, D:2 * D] = (du * ac).astype(BF16)

        @pl.when(i == 0)
        def _():
            dw_ref[...] = jnp.zeros_like(dw_ref)

        dw0 = jnp.sum(u * dnxt, axis=0, keepdims=True)
        dw1 = jnp.sum(u * d, axis=0, keepdims=True)
        dw2 = jnp.sum(u * dprev, axis=0, keepdims=True)
        r8 = lax.broadcasted_iota(jnp.int32, (8, D), 0)
        dw_ref[...] += jnp.where(r8 == 0, dw0, jnp.where(r8 == 1, dw1, jnp.where(r8 == 2, dw2, 0.0)))

    hp, hn = _halo_specs(D, nt)
    row = pl.BlockSpec((TM, D), lambda i: (i, 0))
    return pl.pallas_call(
        body, name=name, grid=(nt,),
        in_specs=[row, hp, hn, _pcol("a_c", D), _pcol("a_x", D), pl.BlockSpec((8, D), lambda i: (0, 0)), _HBM],
        out_specs=(_dp_spec("conv_b"), pl.BlockSpec((8, D), lambda i: (0, 0))),
        out_shape=(jax.ShapeDtypeStruct(dp.shape, dp.dtype), jax.ShapeDtypeStruct((8, D), F32)),
        input_output_aliases={6: 0}, compiler_params=_cparams(1, 40),
    )(dcv, dcv, dcv, p, p, conv_w8, dp)


def _rot_half(x):
    lane = lax.broadcasted_iota(jnp.int32, x.shape, 1)
    return jnp.where((lane % 64) < 32, pltpu.roll(x, 96, 1), pltpu.roll(x, 32, 1))


def _qk_prep_fwd(p, qg, kg, cos_t, sin_t, name):
    t = p.shape[0]

    def body(q_ref, k_ref, qg_ref, kg_ref, c_ref, s_ref, qo_ref, ko_ref):
        c, s = c_ref[...], s_ref[...]

        def one(xv, g, scale):
            y = xv * lax.rsqrt(jnp.mean(xv * xv, axis=-1, keepdims=True) + EPS) * g
            return ((y * c + _rot_half(y) * s) * scale).astype(BF16)

        for h in range(NH):
            qo_ref[:, h * HD:(h + 1) * HD] = one(q_ref[:, h * HD:(h + 1) * HD].astype(F32), qg_ref[...], Q_FOLD)
        for h in range(NKV):
            ko_ref[:, h * HD:(h + 1) * HD] = one(k_ref[:, h * HD:(h + 1) * HD].astype(F32), kg_ref[...], 1.0)

    vec = pl.BlockSpec((1, HD), lambda i: (0, 0))
    tab = pl.BlockSpec((TM, HD), lambda i: (i, 0))
    return pl.pallas_call(
        body, name=name, grid=(t // TM,),
        in_specs=[_pcol("q", NH * HD), _pcol("k", NKV * HD), vec, vec, tab, tab],
        out_specs=(pl.BlockSpec((TM, NH * HD), lambda i: (i, 0)), pl.BlockSpec((TM, NKV * HD), lambda i: (i, 0))),
        out_shape=(jax.ShapeDtypeStruct((t, NH * HD), BF16), jax.ShapeDtypeStruct((t, NKV * HD), BF16)),
        compiler_params=_cparams(1, 32),
    )(p, p, qg, kg, cos_t, sin_t)


def _qk_prep_bwd(dqr, dkr, p, qg, kg, cos_t, sin_t, dp, name):
    t = p.shape[0]

    def body(dq_ref, dk_ref, q_ref, k_ref, qg_ref, kg_ref, c_ref, s_ref, _, dqo_ref, dko_ref, dqg_ref, dkg_ref):
        i = pl.program_id(0)
        c, s = c_ref[...], s_ref[...]

        @pl.when(i == 0)
        def _():
            dqg_ref[...] = jnp.zeros_like(dqg_ref)
            dkg_ref[...] = jnp.zeros_like(dkg_ref)

        def one(dyr, xv, g):
            dy = dyr * c + _rot_half(dyr * s)
            r = lax.rsqrt(jnp.mean(xv * xv, axis=-1, keepdims=True) + EPS)
            xh = xv * r
            dxh = dy * g
            dx = r * (dxh - xh * jnp.mean(dxh * xh, axis=-1, keepdims=True))
            return dx.astype(BF16), jnp.sum(dy * xh, axis=0, keepdims=True)

        for h in range(NH):
            sl = slice(h * HD, (h + 1) * HD)
            dx, dg = one(dq_ref[:, sl] * ATTN_SCALE, q_ref[:, sl].astype(F32), qg_ref[...])
            dqo_ref[:, sl] = dx
            dqg_ref[...] += dg
        for h in range(NKV):
            sl = slice(h * HD, (h + 1) * HD)
            dx, dg = one(dk_ref[:, sl] * (ATTN_SCALE / Q_FOLD), k_ref[:, sl].astype(F32), kg_ref[...])
            dko_ref[:, sl] = dx
            dkg_ref[...] += dg

    vec = pl.BlockSpec((1, HD), lambda i: (0, 0))
    tab = pl.BlockSpec((TM, HD), lambda i: (i, 0))
    acc = pl.BlockSpec((8, HD), lambda i: (0, 0))
    qrow = pl.BlockSpec((TM, NH * HD), lambda i: (i, 0))
    krow = pl.BlockSpec((TM, NKV * HD), lambda i: (i, 0))
    return pl.pallas_call(
        body, name=name, grid=(t // TM,),
        in_specs=[qrow, krow, _pcol("q", NH * HD), _pcol("k", NKV * HD), vec, vec, tab, tab, _HBM],
        out_specs=(_dp_spec("q"), krow, acc, acc),
        out_shape=(jax.ShapeDtypeStruct(dp.shape, dp.dtype), jax.ShapeDtypeStruct((t, NKV * HD), BF16),
                   jax.ShapeDtypeStruct((8, HD), F32), jax.ShapeDtypeStruct((8, HD), F32)),
        input_output_aliases={8: 0}, compiler_params=_cparams(1, 32),
    )(dqr, dkr, p, p, qg, kg, cos_t, sin_t, dp)


def _key_chunks(n, limit):
    c = max(c for c in range(KEY_ALIGN, min(n, limit) + 1, KEY_ALIGN) if n % c == 0)
    return [(lo, lo + c) for lo in range(0, n, c)]


def _attn_fwd(qr, kr, p, nct, name, rider=None):
    t = qr.shape[0]
    nt = t // TM
    ctx = nct * TM
    vblk = OFF["v"] // HD
    hps = ATTN_HEADS_PER_STEP
    nhp, per_kv = NH // hps, GROUP // hps

    def body(q_ref, k_ref, v_ref, o_ref, lse_ref):
        def tile(nkeys):
            sls = [slice(j * HD, (j + 1) * HD) for j in range(hps)]
            qs = [q_ref[:, sl] for sl in sls]
            m = l = acc = None
            for lo, hi in _key_chunks(nkeys, ATTN_FWD_KEY_CHUNK):
                k, vb = k_ref[lo:hi, :], v_ref[lo:hi, :].astype(BF16)
                ss = [lax.dot_general(q, k, _NT, preferred_element_type=F32) for q in qs]
                mcs = [jnp.max(s, axis=-1, keepdims=True) for s in ss]
                m_new = mcs if m is None else [jnp.maximum(a, b) for a, b in zip(m, mcs)]
                es = [jnp.exp2(s - mn) for s, mn in zip(ss, m_new)]
                lcs = [jnp.sum(e, axis=-1, keepdims=True) for e in es]
                pvs = [jnp.dot(e.astype(BF16), vb, preferred_element_type=F32) for e in es]
                if m is None:
                    l, acc = lcs, pvs
                else:
                    alphas = [jnp.exp2(a - b) for a, b in zip(m, m_new)]
                    l = [x * al + y for x, al, y in zip(l, alphas, lcs)]
                    acc = [x * al + y for x, al, y in zip(acc, alphas, pvs)]
                m = m_new
            for j, sl in enumerate(sls):
                o_ref[:, sl] = (acc[j] / l[j]).astype(BF16)
                lse_ref[:, j:j + 1] = m[j] + jnp.log2(l[j])

        pl.when(pl.program_id(1) < nct)(lambda: tile(ctx))
        pl.when(pl.program_id(1) >= nct)(lambda: tile(t))

    def at(h, i):
        return lambda: (pl.program_id(0) == h) & (pl.program_id(1) == i)

    rn = 0 if rider is None else rider.n
    qspec = pl.BlockSpec((TM, hps * HD), lambda h, i: (i, h))
    return pl.pallas_call(
        _with_rider(body, 3, 2, rider, at(0, 0), at(*divmod(nhp * nt * 7 // 8, nt)), at(nhp - 1, nt - 1)),
        name=name, grid=(nhp, nt),
        in_specs=[qspec, pl.BlockSpec((t, HD), lambda h, i: (0, h // per_kv)),
                  pl.BlockSpec((t, HD), lambda h, i: (0, vblk + h // per_kv))] + [_HBM] * rn,
        out_specs=(qspec, pl.BlockSpec((None, TM, hps), lambda h, i: (h, i, 0))) + (_HBM,) * rn,
        out_shape=(jax.ShapeDtypeStruct((t, NH * HD), BF16), jax.ShapeDtypeStruct((nhp, t, hps), F32))
        + (() if rider is None else tuple(rider.out_shape)),
        scratch_shapes=[] if rider is None else rider.scratch,
        compiler_params=_cparams(2, 48),
    )(qr, kr, p, *(() if rider is None else rider.xs))


def _attn_bwd(qr, kr, p, o, lse, do, nct, name, rider=None):
    t = qr.shape[0]
    nt = t // TM
    ctx = nct * TM
    vblk = OFF["v"] // HD
    hps = ATTN_HEADS_PER_STEP

    def body(q_ref, k_ref, v_ref, o_ref, lse_ref, do_ref, dq_ref, dk_ref, dv_ref):
        g, i = pl.program_id(1), pl.program_id(2)

        @pl.when((g == 0) & (i == 0))
        def _():
            dk_ref[...] = jnp.zeros_like(dk_ref)
            dv_ref[...] = jnp.zeros_like(dv_ref)

        def tile(nkeys):
            heads = []
            for j in range(hps):
                sl = slice(j * HD, (j + 1) * HD)
                dob = do_ref[:, sl]
                drow = jnp.sum(dob.astype(F32) * o_ref[:, sl].astype(F32), axis=-1, keepdims=True)
                heads.append((sl, q_ref[:, sl], dob, drow, lse_ref[:, j:j + 1]))
            dq = [None] * hps
            for lo, hi in _key_chunks(nkeys, ATTN_BWD_KEY_CHUNK):
                k = k_ref[lo:hi, :]
                vb = v_ref[lo:hi, :].astype(BF16)
                ss = [lax.dot_general(q, k, _NT, preferred_element_type=F32) for _, q, _, _, _ in heads]
                dps = [lax.dot_general(dob, vb, _NT, preferred_element_type=F32) for _, _, dob, _, _ in heads]
                prs = [jnp.exp2(s - h[4]) for s, h in zip(ss, heads)]
                dss = [(pr * (dp - h[3])).astype(BF16) for pr, dp, h in zip(prs, dps, heads)]
                pbs = [pr.astype(BF16) for pr in prs]
                dqs = [jnp.dot(ds, k, preferred_element_type=F32) for ds in dss]
                dks = [lax.dot_general(ds, h[1], _TN, preferred_element_type=F32) for ds, h in zip(dss, heads)]
                dvs = [lax.dot_general(pb, h[2], _TN, preferred_element_type=F32) for pb, h in zip(pbs, heads)]
                dq = [x if y is None else y + x for x, y in zip(dqs, dq)]
                dk_ref[lo:hi, :] += functools.reduce(lambda a, b: a + b, dks)
                dv_ref[lo:hi, :] += functools.reduce(lambda a, b: a + b, dvs)
            for j, (sl, *_) in enumerate(heads):
                dq_ref[:, sl] = dq[j]

        pl.when(i < nct)(lambda: tile(ctx))
        pl.when(i >= nct)(lambda: tile(t))

    def at(kv, g, i):
        return lambda: (pl.program_id(0) == kv) & (pl.program_id(1) == g) & (pl.program_id(2) == i)

    rn = 0 if rider is None else rider.n
    per_kv = GROUP // hps
    qspec = pl.BlockSpec((TM, hps * HD), lambda kv, g, i: (i, kv * per_kv + g))
    kvspec = pl.BlockSpec((t, HD), lambda kv, g, i: (0, kv))
    lspec = pl.BlockSpec((None, TM, hps), lambda kv, g, i: (kv * per_kv + g, i, 0))
    return pl.pallas_call(
        _with_rider(body, 6, 3, rider, at(0, 0, 0), at(NKV - 1, 0, 0), at(NKV - 1, per_kv - 1, nt - 1)),
        name=name, grid=(NKV, per_kv, nt),
        in_specs=[qspec, kvspec, pl.BlockSpec((t, HD), lambda kv, g, i: (0, vblk + kv)), qspec, lspec, qspec]
        + [_HBM] * rn,
        out_specs=(qspec, kvspec, kvspec) + (_HBM,) * rn,
        out_shape=(jax.ShapeDtypeStruct((t, NH * HD), F32), jax.ShapeDtypeStruct((t, NKV * HD), F32),
                   jax.ShapeDtypeStruct((t, NKV * HD), F32)) + (() if rider is None else tuple(rider.out_shape)),
        scratch_shapes=[] if rider is None else rider.scratch,
        compiler_params=_cparams(3, 48),
    )(qr, kr, p, o, lse, do, *(() if rider is None else rider.xs))


def _decay_fwd(p, wd, bd, name):
    t = p.shape[0]

    def body(r_ref, w_ref, b_ref, z_ref, bc_ref):
        z = jnp.dot(r_ref[...].astype(BF16), w_ref[...].astype(BF16), preferred_element_type=F32) + b_ref[...]
        z_ref[...] = z
        la = (jnp.minimum(z, 0.0) - jnp.log(1.0 + jnp.exp(-jnp.abs(z)))) / GLA_TAU
        half = GH * GDK
        bc_ref[:, 0:half] = _chunk_sums(la[:, 0:half], False)
        bc_ref[:, half:] = _chunk_sums(la[:, half:], True)

    row = pl.BlockSpec((TM, D), lambda i: (i, 0))
    return pl.pallas_call(
        body, name=name, grid=(t // TM,),
        in_specs=[_pcol("r", R_PAD), pl.BlockSpec((R_PAD, D), lambda i: (0, 0)), pl.BlockSpec((1, D), lambda i: (0, 0))],
        out_specs=(row, row),
        out_shape=(jax.ShapeDtypeStruct((t, D), F32), jax.ShapeDtypeStruct((t, D), F32)),
        compiler_params=_cparams(1, 32),
    )(p, wd, bd)


def _chunk_order(s, ncc, nc, rev):
    if not rev:
        return s
    return jnp.where(s < ncc, ncc - 1 - s, nc - 1 - (s - ncc))


def _gla_chains(dirs):
    return [(rev, d, h) + tuple(refs) for d, (rev, *refs) in enumerate(dirs) for h in range(GH)]


def _hk(h):
    return slice(h * GDK, (h + 1) * GDK)


def _hv(h):
    return slice(h * GDV, (h + 1) * GDV)


def _chunk_sums(x, from_end):
    r = lax.broadcasted_iota(jnp.int32, (CH, CH), 0)
    c = lax.broadcasted_iota(jnp.int32, (CH, CH), 1)
    tri = ((c >= r) if from_end else (c <= r)).astype(F32)
    return jnp.concatenate([jnp.dot(tri, x[lo:lo + CH], preferred_element_type=F32, precision=HIGHEST)
                            for lo in range(0, x.shape[0], CH)], axis=0)


def _gla_factors(qs, ks, bcs, bls, revs):
    r = lax.broadcasted_iota(jnp.int32, (CH, CH), 0)
    c = lax.broadcasted_iota(jnp.int32, (CH, CH), 1)
    keeps = [(c >= r) if rev else (c <= r) for rev in revs]
    qs, ks = [q.astype(F32) for q in qs], [k.astype(F32) for k in ks]
    qts = [q * GLA_SCALE * jnp.exp(bc) for q, bc in zip(qs, bcs)]
    kts = [k * jnp.exp(-bc) for k, bc in zip(ks, bcs)]
    khs = [k * jnp.exp(bl - bc) for k, bl, bc in zip(ks, bls, bcs)]
    gls = [jnp.exp(bl) for bl in bls]
    return qts, kts, gls, khs, keeps


def _gla_loads(ch):
    qs = [c[3][:, _hk(c[2])] for c in ch]
    ks = [c[4][:, _hk(c[2])] for c in ch]
    bcs = [c[6][:, _hk(c[2])] for c in ch]
    bls = [c[6][(0 if c[0] else CH - 1):(1 if c[0] else CH), _hk(c[2])] for c in ch]
    return qs, ks, bcs, bls


_NT = (((1,), (1,)), ((), ()))
_TN = (((0,), (0,)), ((), ()))


def _gla_specs(ncc, nc, rev, backward):
    def idx(s):
        return _chunk_order((nc - 1 - s) if backward else s, ncc, nc, rev)

    wk, wv = GH * GDK, GH * GDV
    qb, kb, vb = OFF["gq"] // wk, OFF["gk"] // wk, OFF["gv"] // wv
    lab = 1 if rev else 0
    q = pl.BlockSpec((CH, wk), lambda s: (idx(s), qb))
    k = pl.BlockSpec((CH, wk), lambda s: (idx(s), kb))
    v = pl.BlockSpec((CH, wv), lambda s: (idx(s), vb))
    la = pl.BlockSpec((CH, wk), lambda s: (idx(s), lab))
    o = pl.BlockSpec((CH, wv), lambda s: (idx(s), 0))
    dk = pl.BlockSpec((CH, wk), lambda s: (idx(s), 0))
    st = pl.BlockSpec((None, GH, GDV, GDK), lambda s: (idx(s), 0, 0, 0))
    return q, k, v, la, o, dk, st


def _gla_fwd(p, la, ncc, name):
    t = p.shape[0]
    nc = t // CH
    specs = [_gla_specs(ncc, nc, rev, False) for rev in (False, True)]

    def body(qf, kf, vf, laf, qb_, kb_, vb_, lab, of, stf, ob, stb, s_scr):
        @pl.when(pl.program_id(0) == 0)
        def _():
            s_scr[...] = jnp.zeros_like(s_scr)

        ch = _gla_chains(((False, qf, kf, vf, laf, of, stf), (True, qb_, kb_, vb_, lab, ob, stb)))
        qts, kts, gls, khs, keeps = _gla_factors(*_gla_loads(ch), [c[0] for c in ch])
        sts = [s_scr[c[1], c[2]] for c in ch]
        for c, st in zip(ch, sts):
            c[8][c[2]] = st
        vbs = [c[5][:, _hv(c[2])].astype(BF16) for c in ch]
        qbs = [qt.astype(BF16) for qt in qts]
        a_s = [jnp.where(keep, lax.dot_general(qb, kt.astype(BF16), _NT, preferred_element_type=F32), 0.0)
               for keep, qb, kt in zip(keeps, qbs, kts)]
        inter = [lax.dot_general(qb, st.astype(BF16), _NT, preferred_element_type=F32) for qb, st in zip(qbs, sts)]
        intra = [jnp.dot(a.astype(BF16), vb, preferred_element_type=F32) for a, vb in zip(a_s, vbs)]
        for c, x, y in zip(ch, inter, intra):
            c[7][:, _hv(c[2])] = (x + y).astype(BF16)
        upd = [lax.dot_general(vb, kh.astype(BF16), _TN, preferred_element_type=F32) for vb, kh in zip(vbs, khs)]
        for c, st, gl, u in zip(ch, sts, gls, upd):
            s_scr[c[1], c[2]] = st * gl + u

    o_shape = jax.ShapeDtypeStruct((t, GH * GDV), BF16)
    st_shape = jax.ShapeDtypeStruct((nc, GH, GDV, GDK), F32)
    return pl.pallas_call(
        body, name=name, grid=(nc,),
        in_specs=[sp for s_ in specs for sp in s_[:4]],
        out_specs=tuple(sp for s_ in specs for sp in (s_[4], s_[6])),
        out_shape=(o_shape, st_shape, o_shape, st_shape),
        scratch_shapes=[pltpu.VMEM((2, GH, GDV, GDK), F32)], compiler_params=_cparams(1, 32),
    )(p, p, p, la, p, p, p, la)


def _gla_bwd(p, la, do, stf, stb, ncc, name):
    t = p.shape[0]
    nc = t // CH
    specs = [_gla_specs(ncc, nc, rev, True) for rev in (False, True)]

    def mm(xs, ys, dims=None):
        if dims is None:
            return [jnp.dot(x, y, preferred_element_type=F32) for x, y in zip(xs, ys)]
        return [lax.dot_general(x, y, dims, preferred_element_type=F32) for x, y in zip(xs, ys)]

    def body(*refs):
        ins_f, ins_b, outs_f, outs_b, ds_scr = refs[0:6], refs[6:12], refs[12:16], refs[16:20], refs[20]

        @pl.when(pl.program_id(0) == 0)
        def _():
            ds_scr[...] = jnp.zeros_like(ds_scr)

        ch = _gla_chains(((False, *ins_f, *outs_f), (True, *ins_b, *outs_b)))
        revs = [c[0] for c in ch]
        loads = _gla_loads(ch)
        bcs = loads[2]
        qts, kts, gls, khs, keeps = _gla_factors(*loads, revs)
        stvs = [c[8][c[2]].astype(BF16) for c in ch]
        dsns = [ds_scr[c[1], c[2]] for c in ch]
        dsbs = [x.astype(BF16) for x in dsns]
        vbs = [c[5][:, _hv(c[2])].astype(BF16) for c in ch]
        dobs = [c[7][:, _hv(c[2])].astype(BF16) for c in ch]
        qbs, kbs = [x.astype(BF16) for x in qts], [x.astype(BF16) for x in kts]
        a_s = [jnp.where(keep, x, 0.0).astype(BF16) for keep, x in zip(keeps, mm(qbs, kbs, _NT))]
        das = [jnp.where(keep, x, 0.0).astype(BF16) for keep, x in zip(keeps, mm(dobs, vbs, _NT))]
        dqts = [x + y for x, y in zip(mm(dobs, stvs), mm(das, kbs))]
        dkhs = mm(vbs, dsbs)
        dkts = [x + dkh * gl for x, dkh, gl in zip(mm(das, qbs, _TN), dkhs, gls)]
        for c, x, y in zip(ch, mm(a_s, dobs, _TN), mm([kh.astype(BF16) for kh in khs], dsbs, _NT)):
            c[11][:, _hv(c[2])] = x + y
        for c, x, dsn, gl in zip(ch, mm(dobs, qbs, _TN), dsns, gls):
            ds_scr[c[1], c[2]] = x + dsn * gl
        dgls = [jnp.sum(c[8][c[2]] * dsn, axis=0, keepdims=True) + jnp.sum(dkh * kt, axis=0, keepdims=True)
                for c, dsn, dkh, kt in zip(ch, dsns, dkhs, kts)]
        row = lax.broadcasted_iota(jnp.int32, (CH, GDK), 0)
        dbcs = [dqt * qt - dkt * kt + jnp.where(row == (0 if rev else CH - 1), dgl * gl, 0.0)
                for rev, dqt, qt, dkt, kt, dgl, gl in zip(revs, dqts, qts, dkts, kts, dgls, gls)]
        for c, dbc, dqt, dkt, bc in zip(ch, dbcs, dqts, dkts, bcs):
            c[12][:, _hk(c[2])] = dbc
            c[9][:, _hk(c[2])] = dqt * (GLA_SCALE * jnp.exp(bc))
            c[10][:, _hk(c[2])] = dkt * jnp.exp(-bc)

    k_shape = jax.ShapeDtypeStruct((t, GH * GDK), F32)
    v_shape = jax.ShapeDtypeStruct((t, GH * GDV), F32)
    res = pl.pallas_call(
        body, name=name, grid=(nc,),
        in_specs=[sp for q_s, k_s, v_s, la_s, o_s, _, st_s in specs for sp in (q_s, k_s, v_s, la_s, o_s, st_s)],
        out_specs=tuple(sp for _, _, _, _, o_s, dk_s, _ in specs for sp in (dk_s, dk_s, o_s, dk_s)),
        out_shape=(k_shape, k_shape, v_shape, k_shape) * 2,
        scratch_shapes=[pltpu.VMEM((2, GH, GDV, GDK), F32)], compiler_params=_cparams(1, 32),
    )(p, p, p, la, do, stf, p, p, p, la, do, stb)
    return res[:4], res[4:]


def _gla_merge_bwd(gf, gb, z, p, wd, dp, name):
    t = p.shape[0]
    w2 = GH * GDK

    def body(dqf, dkf, dvf, dlf, dqb, dkb, dvb, dlb, z_ref, r_ref, w_ref, _, dp_ref, dr_ref, db_ref, dw_ref):
        i = pl.program_id(0)
        dp_ref[:, 0:D] = (dvf[...] + dvb[...]).astype(BF16)
        dp_ref[:, D:D + w2] = (dqf[...] + dqb[...]).astype(BF16)
        dp_ref[:, D + w2:D + 2 * w2] = (dkf[...] + dkb[...]).astype(BF16)
        zv = z_ref[...]
        dlf_, dlb_ = _chunk_sums(dlf[...], True), _chunk_sums(dlb[...], False)
        dz = jnp.concatenate([dlf_, dlb_], axis=1) * (_sigmoid(-zv) / GLA_TAU)
        dzb = dz.astype(BF16)
        dr_ref[...] = lax.dot_general(dzb, w_ref[...].astype(BF16), _NT, preferred_element_type=F32).astype(BF16)

        @pl.when(i == 0)
        def _():
            db_ref[...] = jnp.zeros_like(db_ref)
            dw_ref[...] = jnp.zeros_like(dw_ref)

        db_ref[...] += jnp.sum(dz, axis=0, keepdims=True)
        dw_ref[...] += lax.dot_general(r_ref[...].astype(BF16), dzb, _TN, preferred_element_type=F32)

    half = pl.BlockSpec((TM, w2), lambda i: (i, 0))
    row = pl.BlockSpec((TM, D), lambda i: (i, 0))
    wspec = pl.BlockSpec((R_PAD, D), lambda i: (0, 0))
    return pl.pallas_call(
        body, name=name, grid=(t // TM,),
        in_specs=[half, half, row, half, half, half, row, half, row, _pcol("r", R_PAD), wspec, _HBM],
        out_specs=(_dp_spec("gla"), pl.BlockSpec((TM, R_PAD), lambda i: (i, 0)),
                   pl.BlockSpec((8, D), lambda i: (0, 0)), wspec),
        out_shape=(jax.ShapeDtypeStruct(dp.shape, dp.dtype), jax.ShapeDtypeStruct((t, R_PAD), BF16),
                   jax.ShapeDtypeStruct((8, D), F32), jax.ShapeDtypeStruct((R_PAD, D), F32)),
        input_output_aliases={11: 0}, compiler_params=_cparams(1, 40),
    )(*gf, *gb, z, p, wd, dp)


def _dp_tail(dk, dv, dr, dp, name):
    t = dk.shape[0]
    wk = NKV * HD

    def body(dk_ref, dv_ref, dr_ref, _, dp_ref):
        dp_ref[:, 0:wk] = dk_ref[...]
        dp_ref[:, wk:2 * wk] = dv_ref[...].astype(BF16)
        dp_ref[:, 2 * wk:2 * wk + R_PAD] = dr_ref[...]
        dp_ref[:, 2 * wk + R_PAD:] = jnp.zeros((TM, DP_BLOCKS["tail"][1] - 2 * wk - R_PAD), BF16)

    kv = pl.BlockSpec((TM, wk), lambda i: (i, 0))
    return pl.pallas_call(
        body, name=name, grid=(t // TM,),
        in_specs=[kv, kv, pl.BlockSpec((TM, R_PAD), lambda i: (i, 0)), _HBM], out_specs=_dp_spec("tail"),
        out_shape=jax.ShapeDtypeStruct(dp.shape, dp.dtype), input_output_aliases={3: 0},
        compiler_params=_cparams(1, 32),
    )(dk, dv, dr, dp)


def _branch_fwd(att, of, ob, p, gla_g, name):
    t = p.shape[0]

    def body(att_ref, of_ref, ob_ref, za_ref, zg_ref, g_ref, yb_ref, yc_ref):
        za = za_ref[...].astype(F32)
        yb_ref[...] = (att_ref[...].astype(F32) * (za * _sigmoid(za))).astype(BF16)
        for h in range(GH):
            sl = slice(h * GDV, (h + 1) * GDV)
            o = of_ref[:, sl].astype(F32) + ob_ref[:, sl].astype(F32)
            n = o * lax.rsqrt(jnp.mean(o * o, axis=-1, keepdims=True) + EPS) * g_ref[...]
            zh = zg_ref[:, sl].astype(F32)
            yc_ref[:, sl] = (n * (zh * _sigmoid(zh))).astype(BF16)

    row = pl.BlockSpec((TM, D), lambda i: (i, 0))
    return pl.pallas_call(
        body, name=name, grid=(t // TM,),
        in_specs=[row, row, row, _pcol("z_attn", D), _pcol("zg", D), pl.BlockSpec((1, GDV), lambda i: (0, 0))],
        out_specs=(row, row),
        out_shape=(jax.ShapeDtypeStruct((t, D), BF16), jax.ShapeDtypeStruct((t, D), BF16)),
        compiler_params=_cparams(1, 40),
    )(att, of, ob, p, p, gla_g)


def _branch_bwd(dyb, dyc, att, of, ob, p, gla_g, dp, name):
    t = p.shape[0]

    def body(dyb_ref, dyc_ref, att_ref, of_ref, ob_ref, za_ref, zg_ref, g_ref, _, datt_ref, do_ref, dp_ref, dg_ref):
        i = pl.program_id(0)

        @pl.when(i == 0)
        def _():
            dg_ref[...] = jnp.zeros_like(dg_ref)

        za, dyb = za_ref[...].astype(F32), dyb_ref[...].astype(F32)
        sa = _sigmoid(za)
        datt_ref[...] = (dyb * (za * sa)).astype(BF16)
        dp_ref[:, 0:D] = (dyb * att_ref[...].astype(F32) * (sa * (1.0 + za * (1.0 - sa)))).astype(BF16)
        g = g_ref[...]
        for h in range(GH):
            sl = slice(h * GDV, (h + 1) * GDV)
            o = of_ref[:, sl].astype(F32) + ob_ref[:, sl].astype(F32)
            r = lax.rsqrt(jnp.mean(o * o, axis=-1, keepdims=True) + EPS)
            oh = o * r
            zh, dyc = zg_ref[:, sl].astype(F32), dyc_ref[:, sl].astype(F32)
            sg = _sigmoid(zh)
            dn = dyc * (zh * sg)
            dp_ref[:, D + h * GDV:D + (h + 1) * GDV] = (dyc * (oh * g) * (sg * (1.0 + zh * (1.0 - sg)))).astype(BF16)
            doh = dn * g
            do_ref[:, sl] = (r * (doh - oh * jnp.mean(doh * oh, axis=-1, keepdims=True))).astype(BF16)
            dg_ref[...] += jnp.sum(dn * oh, axis=0, keepdims=True)

    row = pl.BlockSpec((TM, D), lambda i: (i, 0))
    return pl.pallas_call(
        body, name=name, grid=(t // TM,),
        in_specs=[row, row, row, row, row, _pcol("z_attn", D), _pcol("zg", D), pl.BlockSpec((1, GDV), lambda i: (0, 0)),
                  _HBM],
        out_specs=(row, row, _dp_spec("branch"), pl.BlockSpec((8, GDV), lambda i: (0, 0))),
        out_shape=(jax.ShapeDtypeStruct((t, D), BF16), jax.ShapeDtypeStruct((t, D), BF16),
                   jax.ShapeDtypeStruct(dp.shape, dp.dtype), jax.ShapeDtypeStruct((8, GDV), F32)),
        input_output_aliases={8: 2}, compiler_params=_cparams(1, 48),
    )(dyb, dyc, att, of, ob, p, p, gla_g, dp)


def _merge_fwd(bra, brb, brc, p, b_gate, name):
    t = p.shape[0]
    mgb = OFF["mg"] // D

    def body(a_ref, b_ref, c_ref, ga_ref, gb_ref, gc_ref, bg_ref, m_ref):
        m_ref[...] = (_sigmoid(ga_ref[...].astype(F32) + bg_ref[:, 0:D]) * a_ref[...].astype(F32)
                      + _sigmoid(gb_ref[...].astype(F32) + bg_ref[:, D:2 * D]) * b_ref[...].astype(F32)
                      + _sigmoid(gc_ref[...].astype(F32) + bg_ref[:, 2 * D:3 * D]) * c_ref[...].astype(F32)).astype(BF16)

    row = pl.BlockSpec((TM, D), lambda i: (i, 0))
    gates = [pl.BlockSpec((TM, D), functools.partial(lambda i, b: (i, b), b=mgb + j)) for j in range(3)]
    return pl.pallas_call(
        body, name=name, grid=(t // TM,),
        in_specs=[row, row, row, *gates, pl.BlockSpec((1, 3 * D), lambda i: (0, 0))],
        out_specs=row, out_shape=jax.ShapeDtypeStruct((t, D), BF16), compiler_params=_cparams(1, 40),
    )(bra, brb, brc, p, p, p, b_gate)


def _merge_bwd(dm, bra, brb, brc, p, b_gate, name):
    t = p.shape[0]
    mgb = OFF["mg"] // D

    def body(dm_ref, a_ref, b_ref, c_ref, ga_ref, gb_ref, gc_ref, bg_ref, da_ref, db_ref, dc_ref, dmg_ref, dbg_ref):
        i = pl.program_id(0)

        @pl.when(i == 0)
        def _():
            dbg_ref[...] = jnp.zeros_like(dbg_ref)

        dm = dm_ref[...].astype(F32)
        for j, (br_ref, g_ref, d_ref) in enumerate(((a_ref, ga_ref, da_ref), (b_ref, gb_ref, db_ref), (c_ref, gc_ref, dc_ref))):
            sl = slice(j * D, (j + 1) * D)
            g = _sigmoid(g_ref[...].astype(F32) + bg_ref[:, sl])
            d_ref[...] = (dm * g).astype(BF16)
            dmg = dm * br_ref[...].astype(F32) * (g * (1.0 - g))
            dmg_ref[:, sl] = dmg.astype(BF16)
            dbg_ref[:, sl] += jnp.sum(dmg, axis=0, keepdims=True)

    row = pl.BlockSpec((TM, D), lambda i: (i, 0))
    gates = [pl.BlockSpec((TM, D), functools.partial(lambda i, b: (i, b), b=mgb + j)) for j in range(3)]
    return pl.pallas_call(
        body, name=name, grid=(t // TM,),
        in_specs=[row, row, row, row, *gates, pl.BlockSpec((1, 3 * D), lambda i: (0, 0))],
        out_specs=(row, row, row, _dp_spec("merge"), pl.BlockSpec((8, 3 * D), lambda i: (0, 0))),
        out_shape=(jax.ShapeDtypeStruct((t, D), BF16),) * 3 + (jax.ShapeDtypeStruct((t, NP), BF16),
                                                                jax.ShapeDtypeStruct((8, 3 * D), F32)),
        compiler_params=_cparams(1, 48),
    )(dm, bra, brb, brc, p, p, p, b_gate)


def _adam_update(ns, g_ref, w_ref, m_ref, v_ref, go_ref, d_ref, mo_ref, vo_ref):
    g = g_ref[0].astype(F32)
    for s in range(1, ns):
        g = g + g_ref[s].astype(F32)
    mn = ADAM_B1 * m_ref[...] + (1.0 - ADAM_B1) * g
    vn = ADAM_B2 * v_ref[...] + (1.0 - ADAM_B2) * jnp.square(g)
    m_hat = mn / (1.0 - ADAM_B1 ** ADAM_STEP)
    v_hat = vn / (1.0 - ADAM_B2 ** ADAM_STEP)
    go_ref[...] = g
    d_ref[...] = -ADAM_LR * (m_hat / (jnp.sqrt(v_hat) + ADAM_EPS) + ADAM_WD * w_ref[...])
    mo_ref[...] = mn
    vo_ref[...] = vn


def _adamw(gsrc, w, m, v, name):
    ns, nl, r, c = gsrc.shape
    gb = gsrc.dtype.itemsize

    def fits(rows, cols):
        lanes = -(-cols // LANE) * LANE
        return ns * rows * lanes * gb <= ADAM_SRC_BYTES and rows * lanes * 4 <= ADAM_ROW_BYTES

    tr, tc = r, c
    if not fits(r, c):
        rows = [cand for cand in range(16, r, 16) if r % cand == 0 and fits(cand, c)]
        cols = [cand for cand in range(LANE, c, LANE) if c % cand == 0 and fits(r, cand)]
        if rows:
            tr = rows[-1]
        else:
            tc = cols[-1]

    def body(*refs):
        _adam_update(ns, *refs)

    row = pl.BlockSpec((None, tr, tc), lambda l, i, j: (l, i, j))
    return pl.pallas_call(
        body, name=name, grid=(nl, r // tr, c // tc),
        in_specs=[pl.BlockSpec((ns, None, tr, tc), lambda l, i, j: (0, l, i, j)), row, row, row],
        out_specs=(row,) * 4, out_shape=(jax.ShapeDtypeStruct((nl, r, c), F32),) * 4,
        compiler_params=_cparams(3, 48),
    )(gsrc, w, m, v)


def _pair_sum(a, b, name):
    s, r, c = a.shape
    tc = _pick(c, (256, 128))

    def body(a_ref, b_ref, o_ref):
        o_ref[...] = (a_ref[...].astype(F32) + b_ref[...].astype(F32)).astype(BF16)

    blk = pl.BlockSpec((None, r, tc), lambda i, j: (i, 0, j))
    return pl.pallas_call(
        body, name=name, grid=(s, c // tc), in_specs=[blk, blk], out_specs=blk,
        out_shape=jax.ShapeDtypeStruct(a.shape, BF16), compiler_params=_cparams(2, 32),
    )(a, b)


def _adamw_small(items, name):
    k = len(items)

    def body(*refs):
        for j in range(k):
            _adam_update(items[j][0].shape[0], *refs[4 * j:4 * j + 4], *refs[4 * k + 4 * j:4 * k + 4 * j + 4])

    out = pl.pallas_call(
        body, name=name,
        out_shape=tuple(jax.ShapeDtypeStruct(w.shape, F32) for _, w, _, _ in items for _ in range(4)),
    )(*[a for item in items for a in item])
    return [out[4 * j:4 * j + 4] for j in range(k)]


def _rope_tables(ctx, seq):
    n_rows = seq // GRID_W
    pairs = HD // 4
    row = jnp.repeat(jnp.arange(n_rows, dtype=F32), GRID_W)
    col = jnp.tile(jnp.arange(GRID_W, dtype=F32), n_rows)
    freqs = ROPE_THETA ** (-jnp.arange(pairs, dtype=F32) * 2.0 / (HD // 2))
    ar, ac = row[:, None] * freqs, col[:, None] * freqs
    cos_l = jnp.concatenate([jnp.cos(ar), jnp.cos(ar), jnp.cos(ac), jnp.cos(ac)], axis=1)
    sin_l = jnp.concatenate([-jnp.sin(ar), jnp.sin(ar), -jnp.sin(ac), jnp.sin(ac)], axis=1)
    cos_t = jnp.concatenate([jnp.ones((ctx, HD), F32), cos_l], axis=0)
    sin_t = jnp.concatenate([jnp.zeros((ctx, HD), F32), sin_l], axis=0)
    return cos_t, sin_t


def _to_proj_layout(wt):
    parts = [wt[s:s + wd] for _, s, wd in _SEGS]
    used = sum(wd for _, _, wd in _SEGS)
    parts.append(jnp.zeros((NP - used, wt.shape[1]), wt.dtype))
    return jnp.concatenate(parts, axis=0)


def _from_proj_layout(g):
    order = sorted(_SEGS, key=lambda sg: sg[1])
    return jnp.concatenate([g[OFF[n]:OFF[n] + wd] for n, _, wd in order], axis=0)


def _row0(a):
    return a[..., 0, :]


def kernel(x, c, ctx, c_ctx, w_ada, b_ada, g_pre, g_post, w_in, conv_w, q_norm_g, k_norm_g, w_decay_fwd, b_decay_fwd, w_decay_bwd, b_decay_bwd, gla_norm_g, w_br_conv, w_br_attn, w_br_gla, b_gate, w_out, loss_target, m_c_ctx, m_w_ada, m_b_ada, m_g_pre, m_g_post, m_w_in, m_conv_w, m_q_norm_g, m_k_norm_g, m_w_decay_fwd, m_b_decay_fwd, m_w_decay_bwd, m_b_decay_bwd, m_gla_norm_g, m_w_br_conv, m_w_br_attn, m_w_br_gla, m_b_gate, m_w_out, v_c_ctx, v_w_ada, v_b_ada, v_g_pre, v_g_post, v_w_in, v_conv_w, v_q_norm_g, v_k_norm_g, v_w_decay_fwd, v_b_decay_fwd, v_w_decay_bwd, v_b_decay_bwd, v_gla_norm_g, v_w_br_conv, v_w_br_attn, v_w_br_gla, v_b_gate, v_w_out):
    seq, n_ctx = x.shape[1], ctx.shape[1]
    assert n_ctx % TM == 0 and seq % TM == 0 and seq % GRID_W == 0
    t = n_ctx + seq
    nct, ncc = n_ctx // TM, n_ctx // CH
    dev = 4 * lax.axis_index("x") + 2 * lax.axis_index("y") + lax.axis_index("c")
    ada_w = w_ada.shape[2]
    in_w = w_in.shape[2]
    br_r = w_br_conv.shape[1]

    def in_t(a, l):
        return a.transpose(2, 0, 1)[:, l, :]

    wb = [w.astype(BF16) for w in (w_ada, w_br_conv, w_br_attn, w_br_gla, w_out)]
    wall = _all_gather([wb[0][0], in_t(w_in, 0).astype(BF16), conv_w, w_decay_fwd, w_decay_bwd],
                       "gather_first")
    later = _GatherRider([in_t(w_in, 1).astype(BF16), wb[0][1], wb[1], wb[2], wb[3], wb[4]])

    def full_small(g):
        return g.transpose(1, 2, 0, 3).reshape(DEPTH, g.shape[2], NDEV * g.shape[3])

    def full_in(g):
        return _to_proj_layout(g.reshape(IN_WIDTH, D))

    def full_ada(g):
        return g.transpose(1, 0, 2).reshape(D, 3 * D)

    w_ada_f = [full_ada(wall[0]), None]
    wp = [full_in(wall[1]), None]
    conv_f, wdf_f, wdb_f = full_small(wall[2]), full_small(wall[3]), full_small(wall[4])

    cos_t, sin_t = _rope_tables(n_ctx, seq)
    cc = jnp.concatenate([c_ctx[None, :], c.reshape(1, D), jnp.zeros((6, D), F32)], axis=0)
    silu_cc, dsilu_cc = _ada_in(cc)

    conv8, wd_pad, bd = [], [], []
    for l in range(DEPTH):
        conv8.append(jnp.concatenate([conv_f[l], jnp.zeros((5, D), F32)], axis=0))
        zr = jnp.zeros((GLA_RANK, GH * GDK), F32)
        wd_pad.append(jnp.concatenate([jnp.concatenate([wdf_f[l], zr], axis=1), jnp.concatenate([zr, wdb_f[l]], axis=1),
                                       jnp.zeros((R_PAD - 2 * GLA_RANK, D), F32)], axis=0))
        bd.append(jnp.concatenate([b_decay_fwd[l], b_decay_bwd[l]])[None, :])

    xs = jnp.concatenate([ctx[0], x[0]], axis=0)
    saved = []
    for l in range(DEPTH):
        n = f"l{l}_"
        mod = _mm(silu_cc, w_ada_f[l], n + "mod", bias=b_ada[l][None, :])
        mod3 = mod[0:2].reshape(2, 3, D)
        h = _prenorm_fwd(xs, g_pre[l][None, :], mod3, nct, n + "prenorm")
        p = _mm(h, wp[l], n + "proj", tb=True, out_dtype=BF16, tm=t // 2)
        cv, ya = _conv_fwd(p, conv8[l], nct, n + "conv")
        qr, kr = _qk_prep_fwd(p, q_norm_g[l][None, :], k_norm_g[l][None, :], cos_t, sin_t, n + "qk_prep")
        att, lse, *got = _attn_fwd(qr, kr, p, nct, n + "attn", rider=later if l == 0 else None)
        if l == 0:
            wp[1], w_ada_f[1] = full_in(got[0]), full_ada(got[1])
            w_brs_f = [g.transpose(1, 0, 2, 3).reshape(DEPTH, D, D) for g in got[2:]]
        z, la = _decay_fwd(p, wd_pad[l], bd[l], n + "decay")
        of, stf, ob, stb = _gla_fwd(p, la, ncc, n + "gla")
        yb, yc = _branch_fwd(att, of, ob, p, gla_norm_g[l][None, :], n + "branch")
        bra = _mm(ya, w_brs_f[0][l], n + "br_conv", out_dtype=BF16)
        brb = _mm(yb, w_brs_f[1][l], n + "br_attn", out_dtype=BF16)
        brc = _mm(yc, w_brs_f[2][l], n + "br_gla", out_dtype=BF16)
        mm_ = _merge_fwd(bra, brb, brc, p, b_gate[l][None, :], n + "merge")
        out = _mm(mm_, w_brs_f[3][l], n + "out")
        x_new = _post_fwd(xs, out, g_post[l][None, :], mod3, nct, n + "post")
        saved.append(dict(x=xs, mod3=mod3, h=h, p=p, cv=cv, ya=ya, qr=qr, kr=kr, att=att, lse=lse, z=z, la=la, of=of, ob=ob,
                          stf=stf, stb=stb, yb=yb, yc=yc, bra=bra, brb=brb, brc=brc, m=mm_, out=out))
        xs = x_new

    dx, sq = _loss_grad(xs, loss_target[0], nct, "loss")
    loss = lax.psum(0.5 * sq[0, 0] / D, ("x", "y", "c"))

    gw = {k: [None] * DEPTH for k in ("w_in", "br_conv", "br_attn", "br_gla", "out", "b_gate", "g_pre", "g_post",
                                      "conv_w", "qg", "kg", "wd", "bdec", "gla_g", "dmod")}
    dctx = []

    def in_slots(l):
        return _from_proj_layout(gw["w_in"][l]).reshape(NDEV, in_w, D)

    def br_slots():
        return [jnp.stack([gw[k][l].reshape(NDEV, br_r, D) for l in range(DEPTH)], axis=1)
                for k in ("br_conv", "br_attn", "br_gla", "out")]

    for l in reversed(range(DEPTH)):
        n = f"l{l}_b_"
        s = saved[l]
        p = s["p"]
        d_out, dgt, gw["g_post"][l] = _post_bwd(dx, s["out"], g_post[l][None, :], s["mod3"], nct, n + "post")
        dm = _mm(d_out, w_brs_f[3][l], n + "dm", tb=True, out_dtype=BF16)
        gw["out"][l] = _mm(s["m"], d_out, n + "dw_out", ta=True, out_dtype=BF16)
        dbra, dbrb, dbrc, dp, gw["b_gate"][l] = _merge_bwd(dm, s["bra"], s["brb"], s["brc"], p, b_gate[l][None, :], n + "merge")
        dya = _mm(dbra, w_brs_f[0][l], n + "dya", tb=True, out_dtype=BF16)
        dyb = _mm(dbrb, w_brs_f[1][l], n + "dyb", tb=True, out_dtype=BF16)
        dyc = _mm(dbrc, w_brs_f[2][l], n + "dyc", tb=True, out_dtype=BF16)
        gw["br_conv"][l] = _mm(s["ya"], dbra, n + "dw_conv", ta=True, out_dtype=BF16)
        gw["br_attn"][l] = _mm(s["yb"], dbrb, n + "dw_attn", ta=True, out_dtype=BF16)
        gw["br_gla"][l] = _mm(s["yc"], dbrc, n + "dw_gla", ta=True, out_dtype=BF16)
        dcv, dp = _conv_bwd_a(dya, p, s["cv"], dp, n + "conv_a")
        dp, gw["conv_w"][l] = _conv_bwd_b(dcv, p, conv8[l], nct, dp, n + "conv_b")
        datt, dgo, dp, gw["gla_g"][l] = _branch_bwd(dyb, dyc, s["att"], s["of"], s["ob"], p, gla_norm_g[l][None, :], dp, n + "branch")
        ex1 = _ExchangeRider([in_slots(DEPTH - 1)] + br_slots()) if l == 0 else None
        dqr, dkr, dv, *got = _attn_bwd(s["qr"], s["kr"], p, s["att"], s["lse"], datt, nct, n + "attn", rider=ex1)
        if l == 0:
            recv_in1, recv_br = got[0], got[1:]
        dp, dk, gw["qg"][l], gw["kg"][l] = _qk_prep_bwd(dqr, dkr, p, q_norm_g[l][None, :], k_norm_g[l][None, :], cos_t, sin_t, dp, n + "qk_prep")
        gf, gb = _gla_bwd(p, s["la"], dgo, s["stf"], s["stb"], ncc, n + "gla")
        dp, dr, gw["bdec"][l], gw["wd"][l] = _gla_merge_bwd(gf, gb, s["z"], p, wd_pad[l], dp, n + "gla_merge")
        dp = _dp_tail(dk, dv, dr, dp, n + "dp_tail")
        gw["w_in"][l] = _mm(dp, s["h"], n + "dw_in", ta=True, out_dtype=BF16, tk=t // 2 if t % 32 == 0 else None)
        if l == 0:
            core = lax.axis_index("c")
            halves = in_slots(0).reshape(NDEV // 2, 2, in_w, D)
            kept = lax.dynamic_index_in_dim(halves, core, axis=1, keepdims=False)
            sent = lax.dynamic_index_in_dim(halves, 1 - core, axis=1, keepdims=False)
            from_sibling, = _comm_alone(_SwapRider([sent]), n + "swap_dw_in")
            chip_sum = _pair_sum(kept, from_sibling, n + "chip_sum_dw_in")
            dh, recv_in0 = _mm(dp, wp[l], n + "dh", tk=NP // 4, rider=_ExchangeRider([chip_sum], chips_only=True))
        else:
            dh = _mm(dp, wp[l], n + "dh", tk=NP // 4)
        dx, dsh, dsc, gw["g_pre"][l] = _prenorm_bwd(dh, s["x"], dx, g_pre[l][None, :], s["mod3"], nct, n + "prenorm")
        dmod = jnp.stack([_row0(dsh), _row0(dsc), _row0(dgt)], axis=1).reshape(2, 3 * D)
        gw["dmod"][l] = dmod
        dmod8 = jnp.concatenate([dmod, jnp.zeros((6, 3 * D), F32)], axis=0)
        dctx.append(_mm(dmod8, w_ada_f[l], n + "dsilu", tb=True))
    grad_x = dx[n_ctx:][None]
    g_cctx = _cctx_grad(dctx[0], dctx[1], dsilu_cc)[0]

    def st2(name):
        return jnp.stack(gw[name])

    g_b_ada = jnp.stack([gw["dmod"][l][0] + gw["dmod"][l][1] for l in range(DEPTH)])
    g_bdf = jnp.stack([gw["bdec"][l][0, :GH * GDK] for l in range(DEPTH)])
    g_bdb = jnp.stack([gw["bdec"][l][0, GH * GDK:] for l in range(DEPTH)])
    g_wdf = jnp.stack([gw["wd"][l][0:GLA_RANK, :GH * GDK] for l in range(DEPTH)])
    g_wdb = jnp.stack([gw["wd"][l][GLA_RANK:2 * GLA_RANK, GH * GDK:] for l in range(DEPTH)])
    rep_grads = [g_cctx, g_b_ada, st2("g_pre")[:, 0], st2("g_post")[:, 0], st2("qg")[:, 0], st2("kg")[:, 0], g_bdf, g_bdb,
                 st2("gla_g")[:, 0], st2("b_gate")[:, 0]]
    rep_w = [c_ctx, b_ada, g_pre, g_post, q_norm_g, k_norm_g, b_decay_fwd, b_decay_bwd, gla_norm_g, b_gate]
    rep_m = [m_c_ctx, m_b_ada, m_g_pre, m_g_post, m_q_norm_g, m_k_norm_g, m_b_decay_fwd, m_b_decay_bwd, m_gla_norm_g, m_b_gate]
    rep_v = [v_c_ctx, v_b_ada, v_g_pre, v_g_post, v_q_norm_g, v_k_norm_g, v_b_decay_fwd, v_b_decay_bwd, v_gla_norm_g, v_b_gate]
    def two_d(a):
        return a.reshape(1, -1) if a.ndim == 1 else a

    def owner_slots(g):
        return g.reshape(DEPTH, g.shape[1], NDEV, g.shape[2] // NDEV).transpose(2, 0, 1, 3)

    n_rep = len(rep_grads)
    small = _comm_alone(_Riders([
        _GatherRider([two_d(g) for g in rep_grads] + [silu_cc[0:2], jnp.stack(gw["dmod"])]),
        _ExchangeRider([owner_slots(st2("conv_w")[:, 0:3]), owner_slots(g_wdf), owner_slots(g_wdb)])]),
        "exchange_small_grads")
    rep_src, (a_all, d_all), sh_src = small[:n_rep], small[n_rep:n_rep + 2], small[n_rep + 2:]
    sh_w = [conv_w, w_decay_fwd, w_decay_bwd]
    sh_m = [m_conv_w, m_w_decay_fwd, m_w_decay_bwd]
    sh_v = [v_conv_w, v_w_decay_fwd, v_w_decay_bwd]
    small_out = _adamw_small(
        [(g, two_d(w), two_d(m), two_d(v)) for g, w, m, v in zip(rep_src, rep_w, rep_m, rep_v)]
        + list(zip(sh_src, sh_w, sh_m, sh_v)), "adam_small")
    rep_g, rep_d, rep_nm, rep_nv = [[small_out[j][k].reshape(rep_w[j].shape) for j in range(n_rep)] for k in range(4)]
    sh_gr, sh_d, sh_nm, sh_nv = [[small_out[n_rep + j][k] for j in range(len(sh_w))] for k in range(4)]

    a_all = a_all.reshape(NDEV * 2, D)
    d_all = d_all.transpose(1, 0, 2, 3).reshape(DEPTH, NDEV * 2, 3 * D)
    g_ada = jnp.stack([_mm(a_all, lax.dynamic_slice_in_dim(d_all[l], dev * ada_w, ada_w, axis=1), f"dw_ada{l}",
                           ta=True, precise=True, tk=NDEV * 2) for l in range(DEPTH)])
    ada_g, ada_d, ada_nm, ada_nv = _adamw(g_ada[None], w_ada, m_w_ada, v_w_ada, "adam_ada")

    big_w = [w_br_conv, w_br_attn, w_br_gla, w_out]
    big_m = [m_w_br_conv, m_w_br_attn, m_w_br_gla, m_w_out]
    big_v = [v_w_br_conv, v_w_br_attn, v_w_br_gla, v_w_out]
    big_out = [_adamw(recv_br[j], big_w[j], big_m[j], big_v[j], f"adam_big{j}") for j in range(len(big_w))]
    in_out = [_adamw(r_[:, None], in_t(w_in, l)[None], in_t(m_w_in, l)[None], in_t(v_w_in, l)[None], f"adam_in{l}")
              for l, r_ in enumerate((recv_in0, recv_in1))]
    in_res = [jnp.stack([in_out[l][k][0] for l in range(DEPTH)], axis=1).transpose(1, 2, 0) for k in range(4)]
    big_g, big_d, big_nm, big_nv = [[in_res[k]] + [o[k] for o in big_out] for k in range(4)]

    def ordered(rep, ada, big, sh):
        c_ctx_, b_ada_, g_pre_, g_post_, qg_, kg_, bdf_, bdb_, glag_, bgate_ = rep
        w_in_, brc_, bra_, brg_, wout_ = big
        conv_, wdf_, wdb_ = sh
        return [c_ctx_, ada, b_ada_, g_pre_, g_post_, w_in_, conv_, qg_, kg_, wdf_, bdf_, wdb_, bdb_, glag_,
                brc_, bra_, brg_, bgate_, wout_]

    return (loss, grad_x,
            *ordered(rep_g, ada_g, big_g, sh_gr), *ordered(rep_d, ada_d, big_d, sh_d),
            *ordered(rep_nm, ada_nm, big_nm, sh_nm), *ordered(rep_nv, ada_nv, big_nv, sh_nv))
```

```python
import functools

import numpy as np
import jax
import jax.numpy as jnp
from jax import lax
from jax.experimental import pallas as pl
from jax.experimental.pallas import tpu as pltpu

F32, BF16 = jnp.float32, jnp.bfloat16
HIGHEST = lax.Precision.HIGHEST

D = 1024
DEPTH = 2
GRID_W = 64
NH, NKV, HD = 8, 2, 128
GROUP = NH // NKV
ROPE_THETA = 10000.0
ATTN_SCALE = HD ** -0.5
Q_FOLD = ATTN_SCALE * 1.4426950408889634
P_HALO = 16
GH, GDK, GDV = 4, 128, 256
GLA_RANK = 16
GLA_TAU = 16.0
CH = 64
GLA_SCALE = GDK ** -0.5
EPS = 1e-6
NDEV = 8
LANE = 128
TM = 256
ATTN_HEADS_PER_STEP = 4
ATTN_FWD_KEY_CHUNK = 2176
ATTN_BWD_KEY_CHUNK = 256
KEY_ALIGN = LANE

ADAM_LR, ADAM_B1, ADAM_B2, ADAM_EPS, ADAM_WD, ADAM_STEP = 0.001, 0.9, 0.999, 1e-08, 0.01, 10

_SEGS = (("a_b", 0, 1024), ("a_z", 3072, 1024), ("a_c", 1024, 1024), ("a_x", 2048, 1024),
         ("z_attn", 5632, 1024), ("zg", 8736, 1024), ("gv", 7680, 1024), ("gq", 6656, 512), ("gk", 7168, 512),
         ("q", 4096, 1024), ("mg", 9760, 3072), ("k", 5120, 256), ("v", 5376, 256), ("r", 8704, 32))
DP_BLOCKS = {"conv_a": ("a_b", 2048), "conv_b": ("a_c", 2048), "branch": ("z_attn", 2048), "gla": ("gv", 2048),
             "q": ("q", 1024), "merge": ("mg", 3072), "tail": ("k", 1024)}
IN_WIDTH = 12832
NP = 13312
OFF = {}
_o = 0
for _n, _s, _w in _SEGS:
    OFF[_n] = _o
    _o += _w
R_PAD = 128


def _cparams(ngrid, vmem_mb):
    return pltpu.CompilerParams(dimension_semantics=("arbitrary",) * ngrid, vmem_limit_bytes=vmem_mb << 20)


def _pick(n, cands):
    for c in cands:
        if n % c == 0:
            return c
    return n


def _sigmoid(x):
    return 1.0 / (1.0 + jnp.exp(-x))


ADAM_SRC_BYTES = 8 << 20
ADAM_ROW_BYTES = 1 << 20


def _all_gather(xs, name):
    return _comm_alone(_GatherRider(xs), name)


_HBM = pl.BlockSpec(memory_space=pl.ANY)


class _Rider:
    def __init__(self, xs, out_shapes, remote_copies=NDEV - 1):
        self.xs, self.n = list(xs), len(xs)
        self.out_shape = [jax.ShapeDtypeStruct(s, x.dtype) for s, x in zip(out_shapes, xs)]
        self.scratch = [pltpu.SemaphoreType.DMA((remote_copies * self.n,)),
                        pltpu.SemaphoreType.DMA((remote_copies * self.n,)), pltpu.SemaphoreType.DMA((self.n,))]


class _GatherRider(_Rider):
    def __init__(self, xs):
        super().__init__(xs, [(NDEV,) + x.shape for x in xs])

    def _parts(self, x_refs, out_refs, sems):
        n = self.n
        send_sems, recv_sems, local_sems = sems
        mx, my, mc = lax.axis_index("x"), lax.axis_index("y"), lax.axis_index("c")
        me, sibling = (mx, my, mc), (mx, my, 1 - mc)
        chips = [(1 - mx, my), (mx, 1 - my), (1 - mx, 1 - my)]

        def slot(a, px, py, pc):
            return out_refs[a].at[4 * px + 2 * py + pc]

        def copy(k, a, block, to, own=False):
            return pltpu.make_async_remote_copy(
                src_ref=x_refs[a] if own else slot(a, *block), dst_ref=slot(a, *block),
                send_sem=send_sems.at[k * n + a], recv_sem=recv_sems.at[k * n + a],
                device_id=to, device_id_type=pl.DeviceIdType.MESH)

        mine = [pltpu.make_async_copy(x_refs[a], slot(a, *me), local_sems.at[a]) for a in range(n)]
        first = [copy(0, a, me, sibling, own=True) for a in range(n)]
        first += [copy(1 + j, a, me, (*chip, mc), own=True) for a in range(n) for j, chip in enumerate(chips)]
        landed = [copy(1 + j, a, (*chip, mc), me) for a in range(n) for j, chip in enumerate(chips)]
        passed = [copy(4 + j, a, (*chip, mc), sibling) for a in range(n) for j, chip in enumerate(chips)]
        from_sibling = [copy(0, a, sibling, me) for a in range(n)]
        from_sibling += [copy(4 + j, a, (*chip, 1 - mc), me) for a in range(n) for j, chip in enumerate(chips)]
        return mine, first, landed, passed, from_sibling

    def start(self, x_refs, out_refs, sems):
        mine, first, _, _, _ = self._parts(x_refs, out_refs, sems)
        for cp in mine + first:
            cp.start()

    def middle(self, x_refs, out_refs, sems):
        _, _, landed, passed, _ = self._parts(x_refs, out_refs, sems)
        for got, fwd in zip(landed, passed):
            got.wait_recv()
            fwd.start()

    def finish(self, x_refs, out_refs, sems):
        mine, first, _, passed, from_sibling = self._parts(x_refs, out_refs, sems)
        for cp in from_sibling:
            cp.wait_recv()
        for cp in first + passed:
            cp.wait_send()
        for cp in mine:
            cp.wait()


class _ExchangeRider(_Rider):
    def __init__(self, xs, chips_only=False):
        self.chips_only = chips_only
        super().__init__(xs, [x.shape for x in xs], 3 if chips_only else NDEV - 1)

    def _parts(self, x_refs, out_refs, sems):
        n = self.n
        send_sems, recv_sems, local_sems = sems
        mx, my, mc = lax.axis_index("x"), lax.axis_index("y"), lax.axis_index("c")
        me = 2 * mx + my if self.chips_only else 4 * mx + 2 * my + mc
        mine = [pltpu.make_async_copy(x_refs[a].at[me], out_refs[a].at[me], local_sems.at[a]) for a in range(n)]
        copies = []
        for a in range(n):
            for rel in range(1, 4 if self.chips_only else NDEV):
                bits = rel << 1 if self.chips_only else rel
                px = (1 - mx) if bits & 4 else mx
                py = (1 - my) if bits & 2 else my
                pc = (1 - mc) if bits & 1 else mc
                peer = 2 * px + py if self.chips_only else 4 * px + 2 * py + pc
                k = (rel - 1) * n + a
                copies.append(pltpu.make_async_remote_copy(
                    src_ref=x_refs[a].at[peer], dst_ref=out_refs[a].at[me],
                    send_sem=send_sems.at[k], recv_sem=recv_sems.at[k],
                    device_id=(px, py, pc), device_id_type=pl.DeviceIdType.MESH))
        return mine, copies

    def start(self, x_refs, out_refs, sems):
        mine, copies = self._parts(x_refs, out_refs, sems)
        for cp in mine + copies:
            cp.start()

    def middle(self, x_refs, out_refs, sems):
        pass

    def finish(self, x_refs, out_refs, sems):
        mine, copies = self._parts(x_refs, out_refs, sems)
        for cp in copies:
            cp.wait_recv()
        for cp in copies:
            cp.wait_send()
        for cp in mine:
            cp.wait()


class _SwapRider(_Rider):
    def __init__(self, xs):
        super().__init__(xs, [x.shape for x in xs], 1)

    def _parts(self, x_refs, out_refs, sems):
        send_sems, recv_sems, _ = sems
        sibling = (lax.axis_index("x"), lax.axis_index("y"), 1 - lax.axis_index("c"))
        return [pltpu.make_async_remote_copy(
            src_ref=x_refs[a], dst_ref=out_refs[a], send_sem=send_sems.at[a], recv_sem=recv_sems.at[a],
            device_id=sibling, device_id_type=pl.DeviceIdType.MESH) for a in range(self.n)]

    def start(self, x_refs, out_refs, sems):
        for cp in self._parts(x_refs, out_refs, sems):
            cp.start()

    def middle(self, x_refs, out_refs, sems):
        pass

    def finish(self, x_refs, out_refs, sems):
        copies = self._parts(x_refs, out_refs, sems)
        for cp in copies:
            cp.wait_recv()
        for cp in copies:
            cp.wait_send()


class _Riders:
    def __init__(self, riders):
        self.riders = list(riders)
        self.xs = [x for r in self.riders for x in r.xs]
        self.n = len(self.xs)
        self.out_shape = [s for r in self.riders for s in r.out_shape]
        self.scratch = [s for r in self.riders for s in r.scratch]

    def _each(self, method, x_refs, out_refs, sems):
        a = b = 0
        for r in self.riders:
            getattr(r, method)(x_refs[a:a + r.n], out_refs[a:a + r.n], sems[b:b + len(r.scratch)])
            a, b = a + r.n, b + len(r.scratch)

    def start(self, *refs):
        self._each("start", *refs)

    def middle(self, *refs):
        self._each("middle", *refs)

    def finish(self, *refs):
        self._each("finish", *refs)


def _comm_alone(rider, name):
    n = rider.n

    def body(*refs):
        x_refs, out_refs, sems = refs[:n], refs[n:2 * n], refs[2 * n:]
        rider.start(x_refs, out_refs, sems)
        rider.middle(x_refs, out_refs, sems)
        rider.finish(x_refs, out_refs, sems)

    return pl.pallas_call(
        body, name=name, out_shape=tuple(rider.out_shape), in_specs=[_HBM] * n, out_specs=(_HBM,) * n,
        scratch_shapes=rider.scratch,
    )(*rider.xs)


def _with_rider(body, nin, nout, rider, first, mid, last):
    if rider is None:
        return body
    n = rider.n

    def wrapped(*refs):
        ins, x_refs = refs[:nin], refs[nin:nin + n]
        outs, out_refs = refs[nin + n:nin + n + nout], refs[nin + n + nout:nin + 2 * n + nout]
        ns = len(rider.scratch)
        scratch, sems = refs[nin + 2 * n + nout:len(refs) - ns], refs[len(refs) - ns:]

        @pl.when(first())
        def _():
            rider.start(x_refs, out_refs, sems)

        body(*ins, *outs, *scratch)

        @pl.when(mid())
        def _():
            rider.middle(x_refs, out_refs, sems)

        @pl.when(last())
        def _():
            rider.finish(x_refs, out_refs, sems)

    return wrapped


def _mm(a, b, name, ta=False, tb=False, out_dtype=F32, bias=None, precise=False, tm=None, tn=None, tk=None, rider=None):
    m, k = (a.shape[1], a.shape[0]) if ta else a.shape
    n = b.shape[0] if tb else b.shape[1]
    assert k == (b.shape[1] if tb else b.shape[0])
    tm = tm or _pick(m, (1088, 1024, 512, 256, 128))
    tn = tn or _pick(n, (1024, 512, 384, 256, 128))
    tk = tk or _pick(k, (1024, 1088, 512, 256, 128))
    nk = k // tk
    dn = (((0 if ta else 1,), (1 if tb else 0,)), ((), ()))

    def body(*refs):
        if bias is None:
            a_ref, b_ref, o_ref = refs[:3]
            bias_ref = None
        else:
            a_ref, b_ref, bias_ref, o_ref = refs[:4]
        x, y = a_ref[...], b_ref[...]
        if precise:
            p = lax.dot_general(x.astype(F32), y.astype(F32), dn, preferred_element_type=F32, precision=HIGHEST)
        else:
            p = lax.dot_general(x.astype(BF16), y.astype(BF16), dn, preferred_element_type=F32)

        def finish(acc):
            if bias_ref is not None:
                acc = acc + bias_ref[...]
            o_ref[...] = acc.astype(out_dtype)

        if nk == 1:
            finish(p)
        else:
            acc_ref = refs[-1]
            kk = pl.program_id(2)

            @pl.when(kk == 0)
            def _():
                acc_ref[...] = p

            @pl.when(kk > 0)
            def _():
                acc_ref[...] += p

            @pl.when(kk == nk - 1)
            def _():
                finish(acc_ref[...])

    a_spec = pl.BlockSpec((tk, tm), lambda i, j, kk: (kk, i)) if ta else pl.BlockSpec((tm, tk), lambda i, j, kk: (i, kk))
    b_spec = pl.BlockSpec((tn, tk), lambda i, j, kk: (j, kk)) if tb else pl.BlockSpec((tk, tn), lambda i, j, kk: (kk, j))
    in_specs = [a_spec, b_spec]
    args = [a, b]
    if bias is not None:
        in_specs.append(pl.BlockSpec((1, tn), lambda i, j, kk: (0, j)))
        args.append(bias)
    grid = (m // tm, n // tn, nk)
    out_spec = pl.BlockSpec((tm, tn), lambda i, j, kk: (i, j))
    scratch = [pltpu.VMEM((tm, tn), F32)] if nk > 1 else []
    if rider is None:
        return pl.pallas_call(
            body, name=name, grid=grid, in_specs=in_specs, out_specs=out_spec,
            out_shape=jax.ShapeDtypeStruct((m, n), out_dtype), scratch_shapes=scratch, compiler_params=_cparams(3, 56),
        )(*args)

    def at(step):
        return lambda: ((pl.program_id(0) == step[0]) & (pl.program_id(1) == step[1]) & (pl.program_id(2) == step[2]))

    end = tuple(g - 1 for g in grid)
    step = grid[0] * grid[1] * grid[2] * 7 // 8
    late = (step // (grid[1] * grid[2]), step // grid[2] % grid[1], step % grid[2])
    return pl.pallas_call(
        _with_rider(body, len(args), 1, rider, at((0, 0, 0)), at(late), at(end)),
        name=name, grid=grid, in_specs=in_specs + [_HBM] * rider.n, out_specs=(out_spec,) + (_HBM,) * rider.n,
        out_shape=(jax.ShapeDtypeStruct((m, n), out_dtype),) + tuple(rider.out_shape),
        scratch_shapes=scratch + rider.scratch, compiler_params=_cparams(3, 56),
    )(*args, *rider.xs)


def _ada_in(cc):
    def body(c_ref, s_ref, d_ref):
        x = c_ref[...]
        sg = _sigmoid(x)
        s_ref[...] = x * sg
        d_ref[...] = sg * (1.0 + x * (1.0 - sg))

    return pl.pallas_call(body, name="ada_in", out_shape=(jax.ShapeDtypeStruct(cc.shape, F32),) * 2)(cc)


def _cctx_grad(t0, t1, dsilu):
    def body(a_ref, b_ref, d_ref, o_ref):
        o_ref[...] = (a_ref[...] + b_ref[...]) * d_ref[...]

    return pl.pallas_call(body, name="cctx_grad", out_shape=jax.ShapeDtypeStruct(t0.shape, F32))(t0, t1, dsilu)


def _seg_spec(nct, rows=3):
    return pl.BlockSpec((None, rows, D), lambda i: (jnp.where(i >= nct, 1, 0), 0, 0))


def _prenorm_fwd(x, g_pre, mod3, nct, name):
    t = x.shape[0]

    def body(x_ref, g_ref, mod_ref, h_ref):
        xv = x_ref[...]
        r = lax.rsqrt(jnp.mean(xv * xv, axis=-1, keepdims=True) + EPS)
        y = xv * r * g_ref[...]
        h_ref[...] = (y * (1.0 + mod_ref[1:2, :]) + mod_ref[0:1, :]).astype(BF16)

    return pl.pallas_call(
        body, name=name, grid=(t // TM,),
        in_specs=[pl.BlockSpec((TM, D), lambda i: (i, 0)), pl.BlockSpec((1, D), lambda i: (0, 0)), _seg_spec(nct)],
        out_specs=pl.BlockSpec((TM, D), lambda i: (i, 0)),
        out_shape=jax.ShapeDtypeStruct((t, D), BF16), compiler_params=_cparams(1, 32),
    )(x, g_pre, mod3)


def _prenorm_bwd(dh, x, dxo, g_pre, mod3, nct, name):
    t = x.shape[0]

    def body(dh_ref, x_ref, dxo_ref, g_ref, mod_ref, dx_ref, dsh_ref, dsc_ref, dg_ref):
        i = pl.program_id(0)
        xv, dhv, g = x_ref[...], dh_ref[...], g_ref[...]
        r = lax.rsqrt(jnp.mean(xv * xv, axis=-1, keepdims=True) + EPS)
        xh = xv * r
        dy = dhv * (1.0 + mod_ref[1:2, :])
        dxh = dy * g
        dx_ref[...] = dxo_ref[...] + r * (dxh - xh * jnp.mean(dxh * xh, axis=-1, keepdims=True))

        @pl.when((i == 0) | (i == nct))
        def _():
            dsh_ref[...] = jnp.zeros_like(dsh_ref)
            dsc_ref[...] = jnp.zeros_like(dsc_ref)

        @pl.when(i == 0)
        def _():
            dg_ref[...] = jnp.zeros_like(dg_ref)

        dsh_ref[...] += jnp.sum(dhv, axis=0, keepdims=True)
        dsc_ref[...] += jnp.sum(dhv * (xh * g), axis=0, keepdims=True)
        dg_ref[...] += jnp.sum(dy * xh, axis=0, keepdims=True)

    row = pl.BlockSpec((TM, D), lambda i: (i, 0))
    seg8 = pl.BlockSpec((None, 8, D), lambda i: (jnp.where(i >= nct, 1, 0), 0, 0))
    return pl.pallas_call(
        body, name=name, grid=(t // TM,),
        in_specs=[row, row, row, pl.BlockSpec((1, D), lambda i: (0, 0)), _seg_spec(nct)],
        out_specs=(row, seg8, seg8, pl.BlockSpec((8, D), lambda i: (0, 0))),
        out_shape=(jax.ShapeDtypeStruct((t, D), F32), jax.ShapeDtypeStruct((2, 8, D), F32),
                   jax.ShapeDtypeStruct((2, 8, D), F32), jax.ShapeDtypeStruct((8, D), F32)),
        compiler_params=_cparams(1, 32),
    )(dh, x, dxo, g_pre, mod3)


def _post_fwd(x, out, g_post, mod3, nct, name):
    t = x.shape[0]

    def body(x_ref, o_ref, g_ref, mod_ref, y_ref):
        ov = o_ref[...]
        r = lax.rsqrt(jnp.mean(ov * ov, axis=-1, keepdims=True) + EPS)
        y_ref[...] = x_ref[...] + mod_ref[2:3, :] * (ov * r * g_ref[...])

    row = pl.BlockSpec((TM, D), lambda i: (i, 0))
    return pl.pallas_call(
        body, name=name, grid=(t // TM,),
        in_specs=[row, row, pl.BlockSpec((1, D), lambda i: (0, 0)), _seg_spec(nct)],
        out_specs=row, out_shape=jax.ShapeDtypeStruct((t, D), F32), compiler_params=_cparams(1, 32),
    )(x, out, g_post, mod3)


def _post_bwd(dxo, out, g_post, mod3, nct, name):
    t = out.shape[0]

    def body(dx_ref, o_ref, g_ref, mod_ref, do_ref, dgt_ref, dg_ref):
        i = pl.program_id(0)
        ov, dxv, g = o_ref[...], dx_ref[...], g_ref[...]
        r = lax.rsqrt(jnp.mean(ov * ov, axis=-1, keepdims=True) + EPS)
        nh = ov * r
        dn = dxv * mod_ref[2:3, :]
        dnh = dn * g
        do_ref[...] = (r * (dnh - nh * jnp.mean(dnh * nh, axis=-1, keepdims=True))).astype(BF16)

        @pl.when((i == 0) | (i == nct))
        def _():
            dgt_ref[...] = jnp.zeros_like(dgt_ref)

        @pl.when(i == 0)
        def _():
            dg_ref[...] = jnp.zeros_like(dg_ref)

        dgt_ref[...] += jnp.sum(dxv * (nh * g), axis=0, keepdims=True)
        dg_ref[...] += jnp.sum(dn * nh, axis=0, keepdims=True)

    row = pl.BlockSpec((TM, D), lambda i: (i, 0))
    seg8 = pl.BlockSpec((None, 8, D), lambda i: (jnp.where(i >= nct, 1, 0), 0, 0))
    return pl.pallas_call(
        body, name=name, grid=(t // TM,),
        in_specs=[row, row, pl.BlockSpec((1, D), lambda i: (0, 0)), _seg_spec(nct)],
        out_specs=(row, seg8, pl.BlockSpec((8, D), lambda i: (0, 0))),
        out_shape=(jax.ShapeDtypeStruct((t, D), BF16), jax.ShapeDtypeStruct((2, 8, D), F32),
                   jax.ShapeDtypeStruct((8, D), F32)),
        compiler_params=_cparams(1, 32),
    )(dxo, out, g_post, mod3)


def _loss_grad(y, target, nct, name):
    t = y.shape[0]

    def body(y_ref, t_ref, dy_ref, l_ref):
        i = pl.program_id(0)

        @pl.when(i == 0)
        def _():
            l_ref[...] = jnp.zeros_like(l_ref)

        @pl.when(i < nct)
        def _():
            dy_ref[...] = jnp.zeros_like(dy_ref)

        @pl.when(i >= nct)
        def _():
            err = y_ref[...] - t_ref[...]
            dy_ref[...] = err / D
            l_ref[...] += jnp.sum(jnp.sum(err * err, axis=1, keepdims=True), axis=0, keepdims=True)

    row = pl.BlockSpec((TM, D), lambda i: (i, 0))
    return pl.pallas_call(
        body, name=name, grid=(t // TM,),
        in_specs=[row, pl.BlockSpec((TM, D), lambda i: (jnp.maximum(i - nct, 0), 0))],
        out_specs=(row, pl.BlockSpec((8, LANE), lambda i: (0, 0))),
        out_shape=(jax.ShapeDtypeStruct((t, D), F32), jax.ShapeDtypeStruct((8, LANE), F32)),
        compiler_params=_cparams(1, 32),
    )(y, target)


def _pcol(name, width):
    assert OFF[name] % width == 0
    blk = OFF[name] // width
    return pl.BlockSpec((TM, width), lambda i: (i, blk))


def _shift_rows(u, prev_row, next_row):
    n = u.shape[0]
    row = lax.broadcasted_iota(jnp.int32, u.shape, 0)
    prev = jnp.where(row == 0, prev_row, pltpu.roll(u, 1, 0))
    nxt = jnp.where(row == n - 1, next_row, pltpu.roll(u, n - 1, 0))
    return prev, nxt


def _halo_specs(width, nt, blk=0, rows=8):
    per = TM // rows
    prev = pl.BlockSpec((rows, width), lambda i: (jnp.maximum(i * per - 1, 0), blk))
    nxt = pl.BlockSpec((rows, width), lambda i: (jnp.minimum((i + 1) * per, nt * per - 1), blk))
    return prev, nxt


def _conv_fwd(p, conv_w8, nct, name):
    t = p.shape[0]
    nt = t // TM

    def body(ab_ref, ac_ref, ax_ref, az_ref, acp_ref, axp_ref, acn_ref, axn_ref, w_ref, cv_ref, ya_ref):
        i = pl.program_id(0)
        def f(ref, rows=slice(None)):
            return ref[rows, :].astype(F32)

        u = f(ac_ref) * f(ax_ref)
        mp = jnp.where((i == 0) | (i == nct), 0.0, 1.0)
        mn = jnp.where((i == nct - 1) | (i == nt - 1), 0.0, 1.0)
        last, first = slice(P_HALO - 1, P_HALO), slice(0, 1)
        prev, nxt = _shift_rows(u, f(acp_ref, last) * f(axp_ref, last) * mp, f(acn_ref, first) * f(axn_ref, first) * mn)
        cv = w_ref[0:1, :] * prev + w_ref[1:2, :] * u + w_ref[2:3, :] * nxt
        az = f(az_ref)
        cv_ref[...] = cv.astype(BF16)
        ya_ref[...] = (f(ab_ref) * cv * (az * _sigmoid(az))).astype(BF16)

    acp, acn = _halo_specs(D, nt, OFF["a_c"] // D, P_HALO)
    axp, axn = _halo_specs(D, nt, OFF["a_x"] // D, P_HALO)
    row = pl.BlockSpec((TM, D), lambda i: (i, 0))
    return pl.pallas_call(
        body, name=name, grid=(nt,),
        in_specs=[_pcol("a_b", D), _pcol("a_c", D), _pcol("a_x", D), _pcol("a_z", D), acp, axp, acn, axn,
                  pl.BlockSpec((8, D), lambda i: (0, 0))],
        out_specs=(row, row),
        out_shape=(jax.ShapeDtypeStruct((t, D), BF16), jax.ShapeDtypeStruct((t, D), BF16)),
        compiler_params=_cparams(1, 40),
    )(p, p, p, p, p, p, p, p, conv_w8)


def _dp_spec(key):
    seg, width = DP_BLOCKS[key]
    assert OFF[seg] % width == 0
    blk = OFF[seg] // width
    return pl.BlockSpec((TM, width), lambda i: (i, blk))


def _conv_bwd_a(dya, p, cv, dp, name):
    t = p.shape[0]

    def body(dy_ref, ab_ref, az_ref, cv_ref, _, dcv_ref, dp_ref):
        dy, ab = dy_ref[...].astype(F32), ab_ref[...].astype(F32)
        az, c = az_ref[...].astype(F32), cv_ref[...].astype(F32)
        sg = _sigmoid(az)
        sz = az * sg
        dcv_ref[...] = dy * ab * sz
        dp_ref[:, 0:D] = (dy * c * sz).astype(BF16)
        dp_ref[:, D:2 * D] = (dy * ab * c * (sg * (1.0 + az * (1.0 - sg)))).astype(BF16)

    row = pl.BlockSpec((TM, D), lambda i: (i, 0))
    return pl.pallas_call(
        body, name=name, grid=(t // TM,),
        in_specs=[row, _pcol("a_b", D), _pcol("a_z", D), row, _HBM], out_specs=(row, _dp_spec("conv_a")),
        out_shape=(jax.ShapeDtypeStruct((t, D), F32), jax.ShapeDtypeStruct(dp.shape, dp.dtype)),
        input_output_aliases={4: 1}, compiler_params=_cparams(1, 40),
    )(dya, p, p, cv, dp)


def _conv_bwd_b(dcv, p, conv_w8, nct, dp, name):
    t = p.shape[0]
    nt = t // TM

    def body(dcv_ref, hp_ref, hn_ref, ac_ref, ax_ref, w_ref, _, dp_ref, dw_ref):
        i = pl.program_id(0)
        d, ac, ax = dcv_ref[...], ac_ref[...].astype(F32), ax_ref[...].astype(F32)
        u = ac * ax
        mp = jnp.where((i == 0) | (i == nct), 0.0, 1.0)
        mn = jnp.where((i == nct - 1) | (i == nt - 1), 0.0, 1.0)
        dprev, dnxt = _shift_rows(d, hp_ref[7:8, :] * mp, hn_ref[0:1, :] * mn)
        du = w_ref[0:1, :] * dnxt + w_ref[1:2, :] * d + w_ref[2:3, :] * dprev
        dp_ref[:, 0:D] = (du * ax).astype(BF16)
        dp_ref[:, D:2 * D] = (du * ac).astype(BF16)

        @pl.when(i == 0)
        def _():
            dw_ref[...] = jnp.zeros_like(dw_ref)

        dw0 = jnp.sum(u * dnxt, axis=0, keepdims=True)
        dw1 = jnp.sum(u * d, axis=0, keepdims=True)
        dw2 = jnp.sum(u * dprev, axis=0, keepdims=True)
        r8 = lax.broadcasted_iota(jnp.int32, (8, D), 0)
        dw_ref[...] += jnp.where(r8 == 0, dw0, jnp.where(r8 == 1, dw1, jnp.where(r8 == 2, dw2, 0.0)))

    hp, hn = _halo_specs(D, nt)
    row = pl.BlockSpec((TM, D), lambda i: (i, 0))
    return pl.pallas_call(
        body, name=name, grid=(nt,),
        in_specs=[row, hp, hn, _pcol("a_c", D), _pcol("a_x", D), pl.BlockSpec((8, D), lambda i: (0, 0)), _HBM],
        out_specs=(_dp_spec("conv_b"), pl.BlockSpec((8, D), lambda i: (0, 0))),
        out_shape=(jax.ShapeDtypeStruct(dp.shape, dp.dtype), jax.ShapeDtypeStruct((8, D), F32)),
        input_output_aliases={6: 0}, compiler_params=_cparams(1, 40),
    )(dcv, dcv, dcv, p, p, conv_w8, dp)


def _rot_half(x):
    lane = lax.broadcasted_iota(jnp.int32, x.shape, 1)
    return jnp.where((lane % 64) < 32, pltpu.roll(x, 96, 1), pltpu.roll(x, 32, 1))


def _qk_prep_fwd(p, qg, kg, cos_t, sin_t, name):
    t = p.shape[0]

    def body(q_ref, k_ref, qg_ref, kg_ref, c_ref, s_ref, qo_ref, ko_ref):
        c, s = c_ref[...], s_ref[...]

        def one(xv, g, scale):
            y = xv * lax.rsqrt(jnp.mean(xv * xv, axis=-1, keepdims=True) + EPS) * g
            return ((y * c + _rot_half(y) * s) * scale).astype(BF16)

        for h in range(NH):
            qo_ref[:, h * HD:(h + 1) * HD] = one(q_ref[:, h * HD:(h + 1) * HD].astype(F32), qg_ref[...], Q_FOLD)
        for h in range(NKV):
            ko_ref[:, h * HD:(h + 1) * HD] = one(k_ref[:, h * HD:(h + 1) * HD].astype(F32), kg_ref[...], 1.0)

    vec = pl.BlockSpec((1, HD), lambda i: (0, 0))
    tab = pl.BlockSpec((TM, HD), lambda i: (i, 0))
    return pl.pallas_call(
        body, name=name, grid=(t // TM,),
        in_specs=[_pcol("q", NH * HD), _pcol("k", NKV * HD), vec, vec, tab, tab],
        out_specs=(pl.BlockSpec((TM, NH * HD), lambda i: (i, 0)), pl.BlockSpec((TM, NKV * HD), lambda i: (i, 0))),
        out_shape=(jax.ShapeDtypeStruct((t, NH * HD), BF16), jax.ShapeDtypeStruct((t, NKV * HD), BF16)),
        compiler_params=_cparams(1, 32),
    )(p, p, qg, kg, cos_t, sin_t)


def _qk_prep_bwd(dqr, dkr, p, qg, kg, cos_t, sin_t, dp, name):
    t = p.shape[0]

    def body(dq_ref, dk_ref, q_ref, k_ref, qg_ref, kg_ref, c_ref, s_ref, _, dqo_ref, dko_ref, dqg_ref, dkg_ref):
        i = pl.program_id(0)
        c, s = c_ref[...], s_ref[...]

        @pl.when(i == 0)
        def _():
            dqg_ref[...] = jnp.zeros_like(dqg_ref)
            dkg_ref[...] = jnp.zeros_like(dkg_ref)

        def one(dyr, xv, g):
            dy = dyr * c + _rot_half(dyr * s)
            r = lax.rsqrt(jnp.mean(xv * xv, axis=-1, keepdims=True) + EPS)
            xh = xv * r
            dxh = dy * g
            dx = r * (dxh - xh * jnp.mean(dxh * xh, axis=-1, keepdims=True))
            return dx.astype(BF16), jnp.sum(dy * xh, axis=0, keepdims=True)

        for h in range(NH):
            sl = slice(h * HD, (h + 1) * HD)
            dx, dg = one(dq_ref[:, sl] * ATTN_SCALE, q_ref[:, sl].astype(F32), qg_ref[...])
            dqo_ref[:, sl] = dx
            dqg_ref[...] += dg
        for h in range(NKV):
            sl = slice(h * HD, (h + 1) * HD)
            dx, dg = one(dk_ref[:, sl] * (ATTN_SCALE / Q_FOLD), k_ref[:, sl].astype(F32), kg_ref[...])
            dko_ref[:, sl] = dx
            dkg_ref[...] += dg

    vec = pl.BlockSpec((1, HD), lambda i: (0, 0))
    tab = pl.BlockSpec((TM, HD), lambda i: (i, 0))
    acc = pl.BlockSpec((8, HD), lambda i: (0, 0))
    qrow = pl.BlockSpec((TM, NH * HD), lambda i: (i, 0))
    krow = pl.BlockSpec((TM, NKV * HD), lambda i: (i, 0))
    return pl.pallas_call(
        body, name=name, grid=(t // TM,),
        in_specs=[qrow, krow, _pcol("q", NH * HD), _pcol("k", NKV * HD), vec, vec, tab, tab, _HBM],
        out_specs=(_dp_spec("q"), krow, acc, acc),
        out_shape=(jax.ShapeDtypeStruct(dp.shape, dp.dtype), jax.ShapeDtypeStruct((t, NKV * HD), BF16),
                   jax.ShapeDtypeStruct((8, HD), F32), jax.ShapeDtypeStruct((8, HD), F32)),
        input_output_aliases={8: 0}, compiler_params=_cparams(1, 32),
    )(dqr, dkr, p, p, qg, kg, cos_t, sin_t, dp)


def _key_chunks(n, limit):
    c = max(c for c in range(KEY_ALIGN, min(n, limit) + 1, KEY_ALIGN) if n % c == 0)
    return [(lo, lo + c) for lo in range(0, n, c)]


def _attn_fwd(qr, kr, p, nct, name, rider=None):
    t = qr.shape[0]
    nt = t // TM
    ctx = nct * TM
    vblk = OFF["v"] // HD
    hps = ATTN_HEADS_PER_STEP
    nhp, per_kv = NH // hps, GROUP // hps

    def body(q_ref, k_ref, v_ref, o_ref, lse_ref):
        def tile(nkeys):
            sls = [slice(j * HD, (j + 1) * HD) for j in range(hps)]
            qs = [q_ref[:, sl] for sl in sls]
            m = l = acc = None
            for lo, hi in _key_chunks(nkeys, ATTN_FWD_KEY_CHUNK):
                k, vb = k_ref[lo:hi, :], v_ref[lo:hi, :].astype(BF16)
                ss = [lax.dot_general(q, k, _NT, preferred_element_type=F32) for q in qs]
                mcs = [jnp.max(s, axis=-1, keepdims=True) for s in ss]
                m_new = mcs if m is None else [jnp.maximum(a, b) for a, b in zip(m, mcs)]
                es = [jnp.exp2(s - mn) for s, mn in zip(ss, m_new)]
                lcs = [jnp.sum(e, axis=-1, keepdims=True) for e in es]
                pvs = [jnp.dot(e.astype(BF16), vb, preferred_element_type=F32) for e in es]
                if m is None:
                    l, acc = lcs, pvs
                else:
                    alphas = [jnp.exp2(a - b) for a, b in zip(m, m_new)]
                    l = [x * al + y for x, al, y in zip(l, alphas, lcs)]
                    acc = [x * al + y for x, al, y in zip(acc, alphas, pvs)]
                m = m_new
            for j, sl in enumerate(sls):
                o_ref[:, sl] = (acc[j] / l[j]).astype(BF16)
                lse_ref[:, j:j + 1] = m[j] + jnp.log2(l[j])

        pl.when(pl.program_id(1) < nct)(lambda: tile(ctx))
        pl.when(pl.program_id(1) >= nct)(lambda: tile(t))

    def at(h, i):
        return lambda: (pl.program_id(0) == h) & (pl.program_id(1) == i)

    rn = 0 if rider is None else rider.n
    qspec = pl.BlockSpec((TM, hps * HD), lambda h, i: (i, h))
    return pl.pallas_call(
        _with_rider(body, 3, 2, rider, at(0, 0), at(*divmod(nhp * nt * 7 // 8, nt)), at(nhp - 1, nt - 1)),
        name=name, grid=(nhp, nt),
        in_specs=[qspec, pl.BlockSpec((t, HD), lambda h, i: (0, h // per_kv)),
                  pl.BlockSpec((t, HD), lambda h, i: (0, vblk + h // per_kv))] + [_HBM] * rn,
        out_specs=(qspec, pl.BlockSpec((None, TM, hps), lambda h, i: (h, i, 0))) + (_HBM,) * rn,
        out_shape=(jax.ShapeDtypeStruct((t, NH * HD), BF16), jax.ShapeDtypeStruct((nhp, t, hps), F32))
        + (() if rider is None else tuple(rider.out_shape)),
        scratch_shapes=[] if rider is None else rider.scratch,
        compiler_params=_cparams(2, 48),
    )(qr, kr, p, *(() if rider is None else rider.xs))


def _attn_bwd(qr, kr, p, o, lse, do, nct, name, rider=None):
    t = qr.shape[0]
    nt = t // TM
    ctx = nct * TM
    vblk = OFF["v"] // HD
    hps = ATTN_HEADS_PER_STEP

    def body(q_ref, k_ref, v_ref, o_ref, lse_ref, do_ref, dq_ref, dk_ref, dv_ref):
        g, i = pl.program_id(1), pl.program_id(2)

        @pl.when((g == 0) & (i == 0))
        def _():
            dk_ref[...] = jnp.zeros_like(dk_ref)
            dv_ref[...] = jnp.zeros_like(dv_ref)

        def tile(nkeys):
            heads = []
            for j in range(hps):
                sl = slice(j * HD, (j + 1) * HD)
                dob = do_ref[:, sl]
                drow = jnp.sum(dob.astype(F32) * o_ref[:, sl].astype(F32), axis=-1, keepdims=True)
                heads.append((sl, q_ref[:, sl], dob, drow, lse_ref[:, j:j + 1]))
            dq = [None] * hps
            for lo, hi in _key_chunks(nkeys, ATTN_BWD_KEY_CHUNK):
                k = k_ref[lo:hi, :]
                vb = v_ref[lo:hi, :].astype(BF16)
                ss = [lax.dot_general(q, k, _NT, preferred_element_type=F32) for _, q, _, _, _ in heads]
                dps = [lax.dot_general(dob, vb, _NT, preferred_element_type=F32) for _, _, dob, _, _ in heads]
                prs = [jnp.exp2(s - h[4]) for s, h in zip(ss, heads)]
                dss = [(pr * (dp - h[3])).astype(BF16) for pr, dp, h in zip(prs, dps, heads)]
                pbs = [pr.astype(BF16) for pr in prs]
                dqs = [jnp.dot(ds, k, preferred_element_type=F32) for ds in dss]
                dks = [lax.dot_general(ds, h[1], _TN, preferred_element_type=F32) for ds, h in zip(dss, heads)]
                dvs = [lax.dot_general(pb, h[2], _TN, preferred_element_type=F32) for pb, h in zip(pbs, heads)]
                dq = [x if y is None else y + x for x, y in zip(dqs, dq)]
                dk_ref[lo:hi, :] += functools.reduce(lambda a, b: a + b, dks)
                dv_ref[lo:hi, :] += functools.reduce(lambda a, b: a + b, dvs)
            for j, (sl, *_) in enumerate(heads):
                dq_ref[:, sl] = dq[j]

        pl.when(i < nct)(lambda: tile(ctx))
        pl.when(i >= nct)(lambda: tile(t))

    def at(kv, g, i):
        return lambda: (pl.program_id(0) == kv) & (pl.program_id(1) == g) & (pl.program_id(2) == i)

    rn = 0 if rider is None else rider.n
    per_kv = GROUP // hps
    qspec = pl.BlockSpec((TM, hps * HD), lambda kv, g, i: (i, kv * per_kv + g))
    kvspec = pl.BlockSpec((t, HD), lambda kv, g, i: (0, kv))
    lspec = pl.BlockSpec((None, TM, hps), lambda kv, g, i: (kv * per_kv + g, i, 0))
    return pl.pallas_call(
        _with_rider(body, 6, 3, rider, at(0, 0, 0), at(NKV - 1, 0, 0), at(NKV - 1, per_kv - 1, nt - 1)),
        name=name, grid=(NKV, per_kv, nt),
        in_specs=[qspec, kvspec, pl.BlockSpec((t, HD), lambda kv, g, i: (0, vblk + kv)), qspec, lspec, qspec]
        + [_HBM] * rn,
        out_specs=(qspec, kvspec, kvspec) + (_HBM,) * rn,
        out_shape=(jax.ShapeDtypeStruct((t, NH * HD), F32), jax.ShapeDtypeStruct((t, NKV * HD), F32),
                   jax.ShapeDtypeStruct((t, NKV * HD), F32)) + (() if rider is None else tuple(rider.out_shape)),
        scratch_shapes=[] if rider is None else rider.scratch,
        compiler_params=_cparams(3, 48),
    )(qr, kr, p, o, lse, do, *(() if rider is None else rider.xs))


def _decay_fwd(p, wd, bd, name):
    t = p.shape[0]

    def body(r_ref, w_ref, b_ref, z_ref, bc_ref):
        z = jnp.dot(r_ref[...].astype(BF16), w_ref[...].astype(BF16), preferred_element_type=F32) + b_ref[...]
        z_ref[...] = z
        la = (jnp.minimum(z, 0.0) - jnp.log(1.0 + jnp.exp(-jnp.abs(z)))) / GLA_TAU
        half = GH * GDK
        bc_ref[:, 0:half] = _chunk_sums(la[:, 0:half], False)
        bc_ref[:, half:] = _chunk_sums(la[:, half:], True)

    row = pl.BlockSpec((TM, D), lambda i: (i, 0))
    return pl.pallas_call(
        body, name=name, grid=(t // TM,),
        in_specs=[_pcol("r", R_PAD), pl.BlockSpec((R_PAD, D), lambda i: (0, 0)), pl.BlockSpec((1, D), lambda i: (0, 0))],
        out_specs=(row, row),
        out_shape=(jax.ShapeDtypeStruct((t, D), F32), jax.ShapeDtypeStruct((t, D), F32)),
        compiler_params=_cparams(1, 32),
    )(p, wd, bd)


def _chunk_order(s, ncc, nc, rev):
    if not rev:
        return s
    return jnp.where(s < ncc, ncc - 1 - s, nc - 1 - (s - ncc))


def _gla_chains(dirs):
    return [(rev, d, h) + tuple(refs) for d, (rev, *refs) in enumerate(dirs) for h in range(GH)]


def _hk(h):
    return slice(h * GDK, (h + 1) * GDK)


def _hv(h):
    return slice(h * GDV, (h + 1) * GDV)


def _chunk_sums(x, from_end):
    r = lax.broadcasted_iota(jnp.int32, (CH, CH), 0)
    c = lax.broadcasted_iota(jnp.int32, (CH, CH), 1)
    tri = ((c >= r) if from_end else (c <= r)).astype(F32)
    return jnp.concatenate([jnp.dot(tri, x[lo:lo + CH], preferred_element_type=F32, precision=HIGHEST)
                            for lo in range(0, x.shape[0], CH)], axis=0)


def _gla_factors(qs, ks, bcs, bls, revs):
    r = lax.broadcasted_iota(jnp.int32, (CH, CH), 0)
    c = lax.broadcasted_iota(jnp.int32, (CH, CH), 1)
    keeps = [(c >= r) if rev else (c <= r) for rev in revs]
    qs, ks = [q.astype(F32) for q in qs], [k.astype(F32) for k in ks]
    qts = [q * GLA_SCALE * jnp.exp(bc) for q, bc in zip(qs, bcs)]
    kts = [k * jnp.exp(-bc) for k, bc in zip(ks, bcs)]
    khs = [k * jnp.exp(bl - bc) for k, bl, bc in zip(ks, bls, bcs)]
    gls = [jnp.exp(bl) for bl in bls]
    return qts, kts, gls, khs, keeps


def _gla_loads(ch):
    qs = [c[3][:, _hk(c[2])] for c in ch]
    ks = [c[4][:, _hk(c[2])] for c in ch]
    bcs = [c[6][:, _hk(c[2])] for c in ch]
    bls = [c[6][(0 if c[0] else CH - 1):(1 if c[0] else CH), _hk(c[2])] for c in ch]
    return qs, ks, bcs, bls


_NT = (((1,), (1,)), ((), ()))
_TN = (((0,), (0,)), ((), ()))


def _gla_specs(ncc, nc, rev, backward):
    def idx(s):
        return _chunk_order((nc - 1 - s) if backward else s, ncc, nc, rev)

    wk, wv = GH * GDK, GH * GDV
    qb, kb, vb = OFF["gq"] // wk, OFF["gk"] // wk, OFF["gv"] // wv
    lab = 1 if rev else 0
    q = pl.BlockSpec((CH, wk), lambda s: (idx(s), qb))
    k = pl.BlockSpec((CH, wk), lambda s: (idx(s), kb))
    v = pl.BlockSpec((CH, wv), lambda s: (idx(s), vb))
    la = pl.BlockSpec((CH, wk), lambda s: (idx(s), lab))
    o = pl.BlockSpec((CH, wv), lambda s: (idx(s), 0))
    dk = pl.BlockSpec((CH, wk), lambda s: (idx(s), 0))
    st = pl.BlockSpec((None, GH, GDV, GDK), lambda s: (idx(s), 0, 0, 0))
    return q, k, v, la, o, dk, st


def _gla_fwd(p, la, ncc, name):
    t = p.shape[0]
    nc = t // CH
    specs = [_gla_specs(ncc, nc, rev, False) for rev in (False, True)]

    def body(qf, kf, vf, laf, qb_, kb_, vb_, lab, of, stf, ob, stb, s_scr):
        @pl.when(pl.program_id(0) == 0)
        def _():
            s_scr[...] = jnp.zeros_like(s_scr)

        ch = _gla_chains(((False, qf, kf, vf, laf, of, stf), (True, qb_, kb_, vb_, lab, ob, stb)))
        qts, kts, gls, khs, keeps = _gla_factors(*_gla_loads(ch), [c[0] for c in ch])
        sts = [s_scr[c[1], c[2]] for c in ch]
        for c, st in zip(ch, sts):
            c[8][c[2]] = st
        vbs = [c[5][:, _hv(c[2])].astype(BF16) for c in ch]
        qbs = [qt.astype(BF16) for qt in qts]
        a_s = [jnp.where(keep, lax.dot_general(qb, kt.astype(BF16), _NT, preferred_element_type=F32), 0.0)
               for keep, qb, kt in zip(keeps, qbs, kts)]
        inter = [lax.dot_general(qb, st.astype(BF16), _NT, preferred_element_type=F32) for qb, st in zip(qbs, sts)]
        intra = [jnp.dot(a.astype(BF16), vb, preferred_element_type=F32) for a, vb in zip(a_s, vbs)]
        for c, x, y in zip(ch, inter, intra):
            c[7][:, _hv(c[2])] = (x + y).astype(BF16)
        upd = [lax.dot_general(vb, kh.astype(BF16), _TN, preferred_element_type=F32) for vb, kh in zip(vbs, khs)]
        for c, st, gl, u in zip(ch, sts, gls, upd):
            s_scr[c[1], c[2]] = st * gl + u

    o_shape = jax.ShapeDtypeStruct((t, GH * GDV), BF16)
    st_shape = jax.ShapeDtypeStruct((nc, GH, GDV, GDK), F32)
    return pl.pallas_call(
        body, name=name, grid=(nc,),
        in_specs=[sp for s_ in specs for sp in s_[:4]],
        out_specs=tuple(sp for s_ in specs for sp in (s_[4], s_[6])),
        out_shape=(o_shape, st_shape, o_shape, st_shape),
        scratch_shapes=[pltpu.VMEM((2, GH, GDV, GDK), F32)], compiler_params=_cparams(1, 32),
    )(p, p, p, la, p, p, p, la)


def _gla_bwd(p, la, do, stf, stb, ncc, name):
    t = p.shape[0]
    nc = t // CH
    specs = [_gla_specs(ncc, nc, rev, True) for rev in (False, True)]

    def mm(xs, ys, dims=None):
        if dims is None:
            return [jnp.dot(x, y, preferred_element_type=F32) for x, y in zip(xs, ys)]
        return [lax.dot_general(x, y, dims, preferred_element_type=F32) for x, y in zip(xs, ys)]

    def body(*refs):
        ins_f, ins_b, outs_f, outs_b, ds_scr = refs[0:6], refs[6:12], refs[12:16], refs[16:20], refs[20]

        @pl.when(pl.program_id(0) == 0)
        def _():
            ds_scr[...] = jnp.zeros_like(ds_scr)

        ch = _gla_chains(((False, *ins_f, *outs_f), (True, *ins_b, *outs_b)))
        revs = [c[0] for c in ch]
        loads = _gla_loads(ch)
        bcs = loads[2]
        qts, kts, gls, khs, keeps = _gla_factors(*loads, revs)
        stvs = [c[8][c[2]].astype(BF16) for c in ch]
        dsns = [ds_scr[c[1], c[2]] for c in ch]
        dsbs = [x.astype(BF16) for x in dsns]
        vbs = [c[5][:, _hv(c[2])].astype(BF16) for c in ch]
        dobs = [c[7][:, _hv(c[2])].astype(BF16) for c in ch]
        qbs, kbs = [x.astype(BF16) for x in qts], [x.astype(BF16) for x in kts]
        a_s = [jnp.where(keep, x, 0.0).astype(BF16) for keep, x in zip(keeps, mm(qbs, kbs, _NT))]
        das = [jnp.where(keep, x, 0.0).astype(BF16) for keep, x in zip(keeps, mm(dobs, vbs, _NT))]
        dqts = [x + y for x, y in zip(mm(dobs, stvs), mm(das, kbs))]
        dkhs = mm(vbs, dsbs)
        dkts = [x + dkh * gl for x, dkh, gl in zip(mm(das, qbs, _TN), dkhs, gls)]
        for c, x, y in zip(ch, mm(a_s, dobs, _TN), mm([kh.astype(BF16) for kh in khs], dsbs, _NT)):
            c[11][:, _hv(c[2])] = x + y
        for c, x, dsn, gl in zip(ch, mm(dobs, qbs, _TN), dsns, gls):
            ds_scr[c[1], c[2]] = x + dsn * gl
        dgls = [jnp.sum(c[8][c[2]] * dsn, axis=0, keepdims=True) + jnp.sum(dkh * kt, axis=0, keepdims=True)
                for c, dsn, dkh, kt in zip(ch, dsns, dkhs, kts)]
        row = lax.broadcasted_iota(jnp.int32, (CH, GDK), 0)
        dbcs = [dqt * qt - dkt * kt + jnp.where(row == (0 if rev else CH - 1), dgl * gl, 0.0)
                for rev, dqt, qt, dkt, kt, dgl, gl in zip(revs, dqts, qts, dkts, kts, dgls, gls)]
        for c, dbc, dqt, dkt, bc in zip(ch, dbcs, dqts, dkts, bcs):
            c[12][:, _hk(c[2])] = dbc
            c[9][:, _hk(c[2])] = dqt * (GLA_SCALE * jnp.exp(bc))
            c[10][:, _hk(c[2])] = dkt * jnp.exp(-bc)

    k_shape = jax.ShapeDtypeStruct((t, GH * GDK), F32)
    v_shape = jax.ShapeDtypeStruct((t, GH * GDV), F32)
    res = pl.pallas_call(
        body, name=name, grid=(nc,),
        in_specs=[sp for q_s, k_s, v_s, la_s, o_s, _, st_s in specs for sp in (q_s, k_s, v_s, la_s, o_s, st_s)],
        out_specs=tuple(sp for _, _, _, _, o_s, dk_s, _ in specs for sp in (dk_s, dk_s, o_s, dk_s)),
        out_shape=(k_shape, k_shape, v_shape, k_shape) * 2,
        scratch_shapes=[pltpu.VMEM((2, GH, GDV, GDK), F32)], compiler_params=_cparams(1, 32),
    )(p, p, p, la, do, stf, p, p, p, la, do, stb)
    return res[:4], res[4:]


def _gla_merge_bwd(gf, gb, z, p, wd, dp, name):
    t = p.shape[0]
    w2 = GH * GDK

    def body(dqf, dkf, dvf, dlf, dqb, dkb, dvb, dlb, z_ref, r_ref, w_ref, _, dp_ref, dr_ref, db_ref, dw_ref):
        i = pl.program_id(0)
        dp_ref[:, 0:D] = (dvf[...] + dvb[...]).astype(BF16)
        dp_ref[:, D:D + w2] = (dqf[...] + dqb[...]).astype(BF16)
        dp_ref[:, D + w2:D + 2 * w2] = (dkf[...] + dkb[...]).astype(BF16)
        zv = z_ref[...]
        dlf_, dlb_ = _chunk_sums(dlf[...], True), _chunk_sums(dlb[...], False)
        dz = jnp.concatenate([dlf_, dlb_], axis=1) * (_sigmoid(-zv) / GLA_TAU)
        dzb = dz.astype(BF16)
        dr_ref[...] = lax.dot_general(dzb, w_ref[...].astype(BF16), _NT, preferred_element_type=F32).astype(BF16)

        @pl.when(i == 0)
        def _():
            db_ref[...] = jnp.zeros_like(db_ref)
            dw_ref[...] = jnp.zeros_like(dw_ref)

        db_ref[...] += jnp.sum(dz, axis=0, keepdims=True)
        dw_ref[...] += lax.dot_general(r_ref[...].astype(BF16), dzb, _TN, preferred_element_type=F32)

    half = pl.BlockSpec((TM, w2), lambda i: (i, 0))
    row = pl.BlockSpec((TM, D), lambda i: (i, 0))
    wspec = pl.BlockSpec((R_PAD, D), lambda i: (0, 0))
    return pl.pallas_call(
        body, name=name, grid=(t // TM,),
        in_specs=[half, half, row, half, half, half, row, half, row, _pcol("r", R_PAD), wspec, _HBM],
        out_specs=(_dp_spec("gla"), pl.BlockSpec((TM, R_PAD), lambda i: (i, 0)),
                   pl.BlockSpec((8, D), lambda i: (0, 0)), wspec),
        out_shape=(jax.ShapeDtypeStruct(dp.shape, dp.dtype), jax.ShapeDtypeStruct((t, R_PAD), BF16),
                   jax.ShapeDtypeStruct((8, D), F32), jax.ShapeDtypeStruct((R_PAD, D), F32)),
        input_output_aliases={11: 0}, compiler_params=_cparams(1, 40),
    )(*gf, *gb, z, p, wd, dp)


def _dp_tail(dk, dv, dr, dp, name):
    t = dk.shape[0]
    wk = NKV * HD

    def body(dk_ref, dv_ref, dr_ref, _, dp_ref):
        dp_ref[:, 0:wk] = dk_ref[...]
        dp_ref[:, wk:2 * wk] = dv_ref[...].astype(BF16)
        dp_ref[:, 2 * wk:2 * wk + R_PAD] = dr_ref[...]
        dp_ref[:, 2 * wk + R_PAD:] = jnp.zeros((TM, DP_BLOCKS["tail"][1] - 2 * wk - R_PAD), BF16)

    kv = pl.BlockSpec((TM, wk), lambda i: (i, 0))
    return pl.pallas_call(
        body, name=name, grid=(t // TM,),
        in_specs=[kv, kv, pl.BlockSpec((TM, R_PAD), lambda i: (i, 0)), _HBM], out_specs=_dp_spec("tail"),
        out_shape=jax.ShapeDtypeStruct(dp.shape, dp.dtype), input_output_aliases={3: 0},
        compiler_params=_cparams(1, 32),
    )(dk, dv, dr, dp)


def _branch_fwd(att, of, ob, p, gla_g, name):
    t = p.shape[0]

    def body(att_ref, of_ref, ob_ref, za_ref, zg_ref, g_ref, yb_ref, yc_ref):
        za = za_ref[...].astype(F32)
        yb_ref[...] = (att_ref[...].astype(F32) * (za * _sigmoid(za))).astype(BF16)
        for h in range(GH):
            sl = slice(h * GDV, (h + 1) * GDV)
            o = of_ref[:, sl].astype(F32) + ob_ref[:, sl].astype(F32)
            n = o * lax.rsqrt(jnp.mean(o * o, axis=-1, keepdims=True) + EPS) * g_ref[...]
            zh = zg_ref[:, sl].astype(F32)
            yc_ref[:, sl] = (n * (zh * _sigmoid(zh))).astype(BF16)

    row = pl.BlockSpec((TM, D), lambda i: (i, 0))
    return pl.pallas_call(
        body, name=name, grid=(t // TM,),
        in_specs=[row, row, row, _pcol("z_attn", D), _pcol("zg", D), pl.BlockSpec((1, GDV), lambda i: (0, 0))],
        out_specs=(row, row),
        out_shape=(jax.ShapeDtypeStruct((t, D), BF16), jax.ShapeDtypeStruct((t, D), BF16)),
        compiler_params=_cparams(1, 40),
    )(att, of, ob, p, p, gla_g)


def _branch_bwd(dyb, dyc, att, of, ob, p, gla_g, dp, name):
    t = p.shape[0]

    def body(dyb_ref, dyc_ref, att_ref, of_ref, ob_ref, za_ref, zg_ref, g_ref, _, datt_ref, do_ref, dp_ref, dg_ref):
        i = pl.program_id(0)

        @pl.when(i == 0)
        def _():
            dg_ref[...] = jnp.zeros_like(dg_ref)

        za, dyb = za_ref[...].astype(F32), dyb_ref[...].astype(F32)
        sa = _sigmoid(za)
        datt_ref[...] = (dyb * (za * sa)).astype(BF16)
        dp_ref[:, 0:D] = (dyb * att_ref[...].astype(F32) * (sa * (1.0 + za * (1.0 - sa)))).astype(BF16)
        g = g_ref[...]
        for h in range(GH):
            sl = slice(h * GDV, (h + 1) * GDV)
            o = of_ref[:, sl].astype(F32) + ob_ref[:, sl].astype(F32)
            r = lax.rsqrt(jnp.mean(o * o, axis=-1, keepdims=True) + EPS)
            oh = o * r
            zh, dyc = zg_ref[:, sl].astype(F32), dyc_ref[:, sl].astype(F32)
            sg = _sigmoid(zh)
            dn = dyc * (zh * sg)
            dp_ref[:, D + h * GDV:D + (h + 1) * GDV] = (dyc * (oh * g) * (sg * (1.0 + zh * (1.0 - sg)))).astype(BF16)
            doh = dn * g
            do_ref[:, sl] = (r * (doh - oh * jnp.mean(doh * oh, axis=-1, keepdims=True))).astype(BF16)
            dg_ref[...] += jnp.sum(dn * oh, axis=0, keepdims=True)

    row = pl.BlockSpec((TM, D), lambda i: (i, 0))
    return pl.pallas_call(
        body, name=name, grid=(t // TM,),
        in_specs=[row, row, row, row, row, _pcol("z_attn", D), _pcol("zg", D), pl.BlockSpec((1, GDV), lambda i: (0, 0)),
                  _HBM],
        out_specs=(row, row, _dp_spec("branch"), pl.BlockSpec((8, GDV), lambda i: (0, 0))),
        out_shape=(jax.ShapeDtypeStruct((t, D), BF16), jax.ShapeDtypeStruct((t, D), BF16),
                   jax.ShapeDtypeStruct(dp.shape, dp.dtype), jax.ShapeDtypeStruct((8, GDV), F32)),
        input_output_aliases={8: 2}, compiler_params=_cparams(1, 48),
    )(dyb, dyc, att, of, ob, p, p, gla_g, dp)


def _merge_fwd(bra, brb, brc, p, b_gate, name):
    t = p.shape[0]
    mgb = OFF["mg"] // D

    def body(a_ref, b_ref, c_ref, ga_ref, gb_ref, gc_ref, bg_ref, m_ref):
        m_ref[...] = (_sigmoid(ga_ref[...].astype(F32) + bg_ref[:, 0:D]) * a_ref[...].astype(F32)
                      + _sigmoid(gb_ref[...].astype(F32) + bg_ref[:, D:2 * D]) * b_ref[...].astype(F32)
                      + _sigmoid(gc_ref[...].astype(F32) + bg_ref[:, 2 * D:3 * D]) * c_ref[...].astype(F32)).astype(BF16)

    row = pl.BlockSpec((TM, D), lambda i: (i, 0))
    gates = [pl.BlockSpec((TM, D), functools.partial(lambda i, b: (i, b), b=mgb + j)) for j in range(3)]
    return pl.pallas_call(
        body, name=name, grid=(t // TM,),
        in_specs=[row, row, row, *gates, pl.BlockSpec((1, 3 * D), lambda i: (0, 0))],
        out_specs=row, out_shape=jax.ShapeDtypeStruct((t, D), BF16), compiler_params=_cparams(1, 40),
    )(bra, brb, brc, p, p, p, b_gate)


def _merge_bwd(dm, bra, brb, brc, p, b_gate, name):
    t = p.shape[0]
    mgb = OFF["mg"] // D

    def body(dm_ref, a_ref, b_ref, c_ref, ga_ref, gb_ref, gc_ref, bg_ref, da_ref, db_ref, dc_ref, dmg_ref, dbg_ref):
        i = pl.program_id(0)

        @pl.when(i == 0)
        def _():
            dbg_ref[...] = jnp.zeros_like(dbg_ref)

        dm = dm_ref[...].astype(F32)
        for j, (br_ref, g_ref, d_ref) in enumerate(((a_ref, ga_ref, da_ref), (b_ref, gb_ref, db_ref), (c_ref, gc_ref, dc_ref))):
            sl = slice(j * D, (j + 1) * D)
            g = _sigmoid(g_ref[...].astype(F32) + bg_ref[:, sl])
            d_ref[...] = (dm * g).astype(BF16)
            dmg = dm * br_ref[...].astype(F32) * (g * (1.0 - g))
            dmg_ref[:, sl] = dmg.astype(BF16)
            dbg_ref[:, sl] += jnp.sum(dmg, axis=0, keepdims=True)

    row = pl.BlockSpec((TM, D), lambda i: (i, 0))
    gates = [pl.BlockSpec((TM, D), functools.partial(lambda i, b: (i, b), b=mgb + j)) for j in range(3)]
    return pl.pallas_call(
        body, name=name, grid=(t // TM,),
        in_specs=[row, row, row, row, *gates, pl.BlockSpec((1, 3 * D), lambda i: (0, 0))],
        out_specs=(row, row, row, _dp_spec("merge"), pl.BlockSpec((8, 3 * D), lambda i: (0, 0))),
        out_shape=(jax.ShapeDtypeStruct((t, D), BF16),) * 3 + (jax.ShapeDtypeStruct((t, NP), BF16),
                                                                jax.ShapeDtypeStruct((8, 3 * D), F32)),
        compiler_params=_cparams(1, 48),
    )(dm, bra, brb, brc, p, p, p, b_gate)


def _adam_update(ns, g_ref, w_ref, m_ref, v_ref, go_ref, d_ref, mo_ref, vo_ref):
    g = g_ref[0].astype(F32)
    for s in range(1, ns):
        g = g + g_ref[s].astype(F32)
    mn = ADAM_B1 * m_ref[...] + (1.0 - ADAM_B1) * g
    vn = ADAM_B2 * v_ref[...] + (1.0 - ADAM_B2) * jnp.square(g)
    m_hat = mn / (1.0 - ADAM_B1 ** ADAM_STEP)
    v_hat = vn / (1.0 - ADAM_B2 ** ADAM_STEP)
    go_ref[...] = g
    d_ref[...] = -ADAM_LR * (m_hat / (jnp.sqrt(v_hat) + ADAM_EPS) + ADAM_WD * w_ref[...])
    mo_ref[...] = mn
    vo_ref[...] = vn


def _adamw(gsrc, w, m, v, name):
    ns, nl, r, c = gsrc.shape
    gb = gsrc.dtype.itemsize

    def fits(rows, cols):
        lanes = -(-cols // LANE) * LANE
        return ns * rows * lanes * gb <= ADAM_SRC_BYTES and rows * lanes * 4 <= ADAM_ROW_BYTES

    tr, tc = r, c
    if not fits(r, c):
        rows = [cand for cand in range(16, r, 16) if r % cand == 0 and fits(cand, c)]
        cols = [cand for cand in range(LANE, c, LANE) if c % cand == 0 and fits(r, cand)]
        if rows:
            tr = rows[-1]
        else:
            tc = cols[-1]

    def body(*refs):
        _adam_update(ns, *refs)

    row = pl.BlockSpec((None, tr, tc), lambda l, i, j: (l, i, j))
    return pl.pallas_call(
        body, name=name, grid=(nl, r // tr, c // tc),
        in_specs=[pl.BlockSpec((ns, None, tr, tc), lambda l, i, j: (0, l, i, j)), row, row, row],
        out_specs=(row,) * 4, out_shape=(jax.ShapeDtypeStruct((nl, r, c), F32),) * 4,
        compiler_params=_cparams(3, 48),
    )(gsrc, w, m, v)


def _pair_sum(a, b, name):
    s, r, c = a.shape
    tc = _pick(c, (256, 128))

    def body(a_ref, b_ref, o_ref):
        o_ref[...] = (a_ref[...].astype(F32) + b_ref[...].astype(F32)).astype(BF16)

    blk = pl.BlockSpec((None, r, tc), lambda i, j: (i, 0, j))
    return pl.pallas_call(
        body, name=name, grid=(s, c // tc), in_specs=[blk, blk], out_specs=blk,
        out_shape=jax.ShapeDtypeStruct(a.shape, BF16), compiler_params=_cparams(2, 32),
    )(a, b)


def _adamw_small(items, name):
    k = len(items)

    def body(*refs):
        for j in range(k):
            _adam_update(items[j][0].shape[0], *refs[4 * j:4 * j + 4], *refs[4 * k + 4 * j:4 * k + 4 * j + 4])

    out = pl.pallas_call(
        body, name=name,
        out_shape=tuple(jax.ShapeDtypeStruct(w.shape, F32) for _, w, _, _ in items for _ in range(4)),
    )(*[a for item in items for a in item])
    return [out[4 * j:4 * j + 4] for j in range(k)]


def _rope_tables(ctx, seq):
    n_rows = seq // GRID_W
    pairs = HD // 4
    row = jnp.repeat(jnp.arange(n_rows, dtype=F32), GRID_W)
    col = jnp.tile(jnp.arange(GRID_W, dtype=F32), n_rows)
    freqs = ROPE_THETA ** (-jnp.arange(pairs, dtype=F32) * 2.0 / (HD // 2))
    ar, ac = row[:, None] * freqs, col[:, None] * freqs
    cos_l = jnp.concatenate([jnp.cos(ar), jnp.cos(ar), jnp.cos(ac), jnp.cos(ac)], axis=1)
    sin_l = jnp.concatenate([-jnp.sin(ar), jnp.sin(ar), -jnp.sin(ac), jnp.sin(ac)], axis=1)
    cos_t = jnp.concatenate([jnp.ones((ctx, HD), F32), cos_l], axis=0)
    sin_t = jnp.concatenate([jnp.zeros((ctx, HD), F32), sin_l], axis=0)
    return cos_t, sin_t


def _to_proj_layout(wt):
    parts = [wt[s:s + wd] for _, s, wd in _SEGS]
    used = sum(wd for _, _, wd in _SEGS)
    parts.append(jnp.zeros((NP - used, wt.shape[1]), wt.dtype))
    return jnp.concatenate(parts, axis=0)


def _from_proj_layout(g):
    order = sorted(_SEGS, key=lambda sg: sg[1])
    return jnp.concatenate([g[OFF[n]:OFF[n] + wd] for n, _, wd in order], axis=0)


def _row0(a):
    return a[..., 0, :]


def kernel(x, c, ctx, c_ctx, w_ada, b_ada, g_pre, g_post, w_in, conv_w, q_norm_g, k_norm_g, w_decay_fwd, b_decay_fwd, w_decay_bwd, b_decay_bwd, gla_norm_g, w_br_conv, w_br_attn, w_br_gla, b_gate, w_out, loss_target, m_c_ctx, m_w_ada, m_b_ada, m_g_pre, m_g_post, m_w_in, m_conv_w, m_q_norm_g, m_k_norm_g, m_w_decay_fwd, m_b_decay_fwd, m_w_decay_bwd, m_b_decay_bwd, m_gla_norm_g, m_w_br_conv, m_w_br_attn, m_w_br_gla, m_b_gate, m_w_out, v_c_ctx, v_w_ada, v_b_ada, v_g_pre, v_g_post, v_w_in, v_conv_w, v_q_norm_g, v_k_norm_g, v_w_decay_fwd, v_b_decay_fwd, v_w_decay_bwd, v_b_decay_bwd, v_gla_norm_g, v_w_br_conv, v_w_br_attn, v_w_br_gla, v_b_gate, v_w_out):
    seq, n_ctx = x.shape[1], ctx.shape[1]
    assert n_ctx % TM == 0 and seq % TM == 0 and seq % GRID_W == 0
    t = n_ctx + seq
    nct, ncc = n_ctx // TM, n_ctx // CH
    dev = 4 * lax.axis_index("x") + 2 * lax.axis_index("y") + lax.axis_index("c")
    ada_w = w_ada.shape[2]
    in_w = w_in.shape[2]
    br_r = w_br_conv.shape[1]

    def in_t(a, l):
        return a.transpose(2, 0, 1)[:, l, :]

    wb = [w.astype(BF16) for w in (w_ada, w_br_conv, w_br_attn, w_br_gla, w_out)]
    wall = _all_gather([wb[0][0], in_t(w_in, 0).astype(BF16), conv_w, w_decay_fwd, w_decay_bwd],
                       "gather_first")
    later_square = _GatherRider([wb[1], wb[2], wb[3], wb[4]])
    later_in = _GatherRider([in_t(w_in, 1).astype(BF16), wb[0][1]])

    def full_small(g):
        return g.transpose(1, 2, 0, 3).reshape(DEPTH, g.shape[2], NDEV * g.shape[3])

    def full_in(g):
        return _to_proj_layout(g.reshape(IN_WIDTH, D))

    def full_ada(g):
        return g.transpose(1, 0, 2).reshape(D, 3 * D)

    w_ada_f = [full_ada(wall[0]), None]
    wp = [full_in(wall[1]), None]
    conv_f, wdf_f, wdb_f = full_small(wall[2]), full_small(wall[3]), full_small(wall[4])

    cos_t, sin_t = _rope_tables(n_ctx, seq)
    cc = jnp.concatenate([c_ctx[None, :], c.reshape(1, D), jnp.zeros((6, D), F32)], axis=0)
    silu_cc, dsilu_cc = _ada_in(cc)

    conv8, wd_pad, bd = [], [], []
    for l in range(DEPTH):
        conv8.append(jnp.concatenate([conv_f[l], jnp.zeros((5, D), F32)], axis=0))
        zr = jnp.zeros((GLA_RANK, GH * GDK), F32)
        wd_pad.append(jnp.concatenate([jnp.concatenate([wdf_f[l], zr], axis=1), jnp.concatenate([zr, wdb_f[l]], axis=1),
                                       jnp.zeros((R_PAD - 2 * GLA_RANK, D), F32)], axis=0))
        bd.append(jnp.concatenate([b_decay_fwd[l], b_decay_bwd[l]])[None, :])

    xs = jnp.concatenate([ctx[0], x[0]], axis=0)
    saved = []
    for l in range(DEPTH):
        n = f"l{l}_"
        mod = _mm(silu_cc, w_ada_f[l], n + "mod", bias=b_ada[l][None, :])
        mod3 = mod[0:2].reshape(2, 3, D)
        h = _prenorm_fwd(xs, g_pre[l][None, :], mod3, nct, n + "prenorm")
        if l == 0:
            p, *got = _mm(h, wp[l], n + "proj", tb=True, out_dtype=BF16, tm=t // 2, rider=later_square)
            w_brs_f = [g.transpose(1, 0, 2, 3).reshape(DEPTH, D, D) for g in got]
        else:
            p = _mm(h, wp[l], n + "proj", tb=True, out_dtype=BF16, tm=t // 2)
        cv, ya = _conv_fwd(p, conv8[l], nct, n + "conv")
        qr, kr = _qk_prep_fwd(p, q_norm_g[l][None, :], k_norm_g[l][None, :], cos_t, sin_t, n + "qk_prep")
        att, lse, *got = _attn_fwd(qr, kr, p, nct, n + "attn", rider=later_in if l == 0 else None)
        if l == 0:
            wp[1], w_ada_f[1] = full_in(got[0]), full_ada(got[1])
        z, la = _decay_fwd(p, wd_pad[l], bd[l], n + "decay")
        of, stf, ob, stb = _gla_fwd(p, la, ncc, n + "gla")
        yb, yc = _branch_fwd(att, of, ob, p, gla_norm_g[l][None, :], n + "branch")
        bra = _mm(ya, w_brs_f[0][l], n + "br_conv", out_dtype=BF16)
        brb = _mm(yb, w_brs_f[1][l], n + "br_attn", out_dtype=BF16)
        brc = _mm(yc, w_brs_f[2][l], n + "br_gla", out_dtype=BF16)
        mm_ = _merge_fwd(bra, brb, brc, p, b_gate[l][None, :], n + "merge")
        out = _mm(mm_, w_brs_f[3][l], n + "out")
        x_new = _post_fwd(xs, out, g_post[l][None, :], mod3, nct, n + "post")
        saved.append(dict(x=xs, mod3=mod3, h=h, p=p, cv=cv, ya=ya, qr=qr, kr=kr, att=att, lse=lse, z=z, la=la, of=of, ob=ob,
                          stf=stf, stb=stb, yb=yb, yc=yc, bra=bra, brb=brb, brc=brc, m=mm_, out=out))
        xs = x_new

    dx, sq = _loss_grad(xs, loss_target[0], nct, "loss")
    loss = lax.psum(0.5 * sq[0, 0] / D, ("x", "y", "c"))

    gw = {k: [None] * DEPTH for k in ("w_in", "br_conv", "br_attn", "br_gla", "out", "b_gate", "g_pre", "g_post",
                                      "conv_w", "qg", "kg", "wd", "bdec", "gla_g", "dmod")}
    dctx = []

    def in_slots(l):
        return _from_proj_layout(gw["w_in"][l]).reshape(NDEV, in_w, D)

    def br_slots(l):
        return [gw[k][l].reshape(NDEV, br_r, D) for k in ("br_conv", "br_attn", "br_gla", "out")]

    for l in reversed(range(DEPTH)):
        n = f"l{l}_b_"
        s = saved[l]
        p = s["p"]
        d_out, dgt, gw["g_post"][l] = _post_bwd(dx, s["out"], g_post[l][None, :], s["mod3"], nct, n + "post")
        dm = _mm(d_out, w_brs_f[3][l], n + "dm", tb=True, out_dtype=BF16)
        gw["out"][l] = _mm(s["m"], d_out, n + "dw_out", ta=True, out_dtype=BF16)
        dbra, dbrb, dbrc, dp, gw["b_gate"][l] = _merge_bwd(dm, s["bra"], s["brb"], s["brc"], p, b_gate[l][None, :], n + "merge")
        dya = _mm(dbra, w_brs_f[0][l], n + "dya", tb=True, out_dtype=BF16)
        dyb = _mm(dbrb, w_brs_f[1][l], n + "dyb", tb=True, out_dtype=BF16)
        dyc = _mm(dbrc, w_brs_f[2][l], n + "dyc", tb=True, out_dtype=BF16)
        gw["br_conv"][l] = _mm(s["ya"], dbra, n + "dw_conv", ta=True, out_dtype=BF16)
        gw["br_attn"][l] = _mm(s["yb"], dbrb, n + "dw_attn", ta=True, out_dtype=BF16)
        gw["br_gla"][l] = _mm(s["yc"], dbrc, n + "dw_gla", ta=True, out_dtype=BF16)
        dcv, dp = _conv_bwd_a(dya, p, s["cv"], dp, n + "conv_a")
        dp, gw["conv_w"][l] = _conv_bwd_b(dcv, p, conv8[l], nct, dp, n + "conv_b")
        datt, dgo, dp, gw["gla_g"][l] = _branch_bwd(dyb, dyc, s["att"], s["of"], s["ob"], p, gla_norm_g[l][None, :], dp, n + "branch")
        ex1 = _ExchangeRider([in_slots(DEPTH - 1)] + br_slots(DEPTH - 1)) if l == 0 else None
        dqr, dkr, dv, *got = _attn_bwd(s["qr"], s["kr"], p, s["att"], s["lse"], datt, nct, n + "attn", rider=ex1)
        if l == 0:
            recv_in1, recv_br1 = got[0], got[1:]
        dp, dk, gw["qg"][l], gw["kg"][l] = _qk_prep_bwd(dqr, dkr, p, q_norm_g[l][None, :], k_norm_g[l][None, :], cos_t, sin_t, dp, n + "qk_prep")
        gf, gb = _gla_bwd(p, s["la"], dgo, s["stf"], s["stb"], ncc, n + "gla")
        dp, dr, gw["bdec"][l], gw["wd"][l] = _gla_merge_bwd(gf, gb, s["z"], p, wd_pad[l], dp, n + "gla_merge")
        dp = _dp_tail(dk, dv, dr, dp, n + "dp_tail")
        tk_in = t // 2 if t % 32 == 0 else None
        if l == 0:
            gw["w_in"][l], *recv_br0 = _mm(dp, s["h"], n + "dw_in", ta=True, out_dtype=BF16, tk=tk_in,
                                           rider=_ExchangeRider(br_slots(0)))
        else:
            gw["w_in"][l] = _mm(dp, s["h"], n + "dw_in", ta=True, out_dtype=BF16, tk=tk_in)
        if l == 0:
            core = lax.axis_index("c")
            halves = in_slots(0).reshape(NDEV // 2, 2, in_w, D)
            kept = lax.dynamic_index_in_dim(halves, core, axis=1, keepdims=False)
            sent = lax.dynamic_index_in_dim(halves, 1 - core, axis=1, keepdims=False)
            from_sibling, = _comm_alone(_SwapRider([sent]), n + "swap_dw_in")
            chip_sum = _pair_sum(kept, from_sibling, n + "chip_sum_dw_in")
            dh, recv_in0 = _mm(dp, wp[l], n + "dh", tk=NP // 4, rider=_ExchangeRider([chip_sum], chips_only=True))
        else:
            dh = _mm(dp, wp[l], n + "dh", tk=NP // 4)
        dx, dsh, dsc, gw["g_pre"][l] = _prenorm_bwd(dh, s["x"], dx, g_pre[l][None, :], s["mod3"], nct, n + "prenorm")
        dmod = jnp.stack([_row0(dsh), _row0(dsc), _row0(dgt)], axis=1).reshape(2, 3 * D)
        gw["dmod"][l] = dmod
        dmod8 = jnp.concatenate([dmod, jnp.zeros((6, 3 * D), F32)], axis=0)
        dctx.append(_mm(dmod8, w_ada_f[l], n + "dsilu", tb=True))
    grad_x = dx[n_ctx:][None]
    g_cctx = _cctx_grad(dctx[0], dctx[1], dsilu_cc)[0]

    def st2(name):
        return jnp.stack(gw[name])

    g_b_ada = jnp.stack([gw["dmod"][l][0] + gw["dmod"][l][1] for l in range(DEPTH)])
    g_bdf = jnp.stack([gw["bdec"][l][0, :GH * GDK] for l in range(DEPTH)])
    g_bdb = jnp.stack([gw["bdec"][l][0, GH * GDK:] for l in range(DEPTH)])
    g_wdf = jnp.stack([gw["wd"][l][0:GLA_RANK, :GH * GDK] for l in range(DEPTH)])
    g_wdb = jnp.stack([gw["wd"][l][GLA_RANK:2 * GLA_RANK, GH * GDK:] for l in range(DEPTH)])
    rep_grads = [g_cctx, g_b_ada, st2("g_pre")[:, 0], st2("g_post")[:, 0], st2("qg")[:, 0], st2("kg")[:, 0], g_bdf, g_bdb,
                 st2("gla_g")[:, 0], st2("b_gate")[:, 0]]
    rep_w = [c_ctx, b_ada, g_pre, g_post, q_norm_g, k_norm_g, b_decay_fwd, b_decay_bwd, gla_norm_g, b_gate]
    rep_m = [m_c_ctx, m_b_ada, m_g_pre, m_g_post, m_q_norm_g, m_k_norm_g, m_b_decay_fwd, m_b_decay_bwd, m_gla_norm_g, m_b_gate]
    rep_v = [v_c_ctx, v_b_ada, v_g_pre, v_g_post, v_q_norm_g, v_k_norm_g, v_b_decay_fwd, v_b_decay_bwd, v_gla_norm_g, v_b_gate]
    def two_d(a):
        return a.reshape(1, -1) if a.ndim == 1 else a

    def owner_slots(g):
        return g.reshape(DEPTH, g.shape[1], NDEV, g.shape[2] // NDEV).transpose(2, 0, 1, 3)

    n_rep = len(rep_grads)
    small = _comm_alone(_Riders([
        _GatherRider([two_d(g) for g in rep_grads] + [silu_cc[0:2], jnp.stack(gw["dmod"])]),
        _ExchangeRider([owner_slots(st2("conv_w")[:, 0:3]), owner_slots(g_wdf), owner_slots(g_wdb)])]),
        "exchange_small_grads")
    rep_src, (a_all, d_all), sh_src = small[:n_rep], small[n_rep:n_rep + 2], small[n_rep + 2:]
    sh_w = [conv_w, w_decay_fwd, w_decay_bwd]
    sh_m = [m_conv_w, m_w_decay_fwd, m_w_decay_bwd]
    sh_v = [v_conv_w, v_w_decay_fwd, v_w_decay_bwd]
    small_out = _adamw_small(
        [(g, two_d(w), two_d(m), two_d(v)) for g, w, m, v in zip(rep_src, rep_w, rep_m, rep_v)]
        + list(zip(sh_src, sh_w, sh_m, sh_v)), "adam_small")
    rep_g, rep_d, rep_nm, rep_nv = [[small_out[j][k].reshape(rep_w[j].shape) for j in range(n_rep)] for k in range(4)]
    sh_gr, sh_d, sh_nm, sh_nv = [[small_out[n_rep + j][k] for j in range(len(sh_w))] for k in range(4)]

    a_all = a_all.reshape(NDEV * 2, D)
    d_all = d_all.transpose(1, 0, 2, 3).reshape(DEPTH, NDEV * 2, 3 * D)
    g_ada = jnp.stack([_mm(a_all, lax.dynamic_slice_in_dim(d_all[l], dev * ada_w, ada_w, axis=1), f"dw_ada{l}",
                           ta=True, precise=True, tk=NDEV * 2) for l in range(DEPTH)])
    ada_g, ada_d, ada_nm, ada_nv = _adamw(g_ada[None], w_ada, m_w_ada, v_w_ada, "adam_ada")

    big_w = [w_br_conv, w_br_attn, w_br_gla, w_out]
    big_m = [m_w_br_conv, m_w_br_attn, m_w_br_gla, m_w_out]
    big_v = [v_w_br_conv, v_w_br_attn, v_w_br_gla, v_w_out]
    big_out = [_adamw(jnp.stack([recv_br0[j], recv_br1[j]], axis=1), big_w[j], big_m[j], big_v[j], f"adam_big{j}")
               for j in range(len(big_w))]
    in_out = [_adamw(r_[:, None], in_t(w_in, l)[None], in_t(m_w_in, l)[None], in_t(v_w_in, l)[None], f"adam_in{l}")
              for l, r_ in enumerate((recv_in0, recv_in1))]
    in_res = [jnp.stack([in_out[l][k][0] for l in range(DEPTH)], axis=1).transpose(1, 2, 0) for k in range(4)]
    big_g, big_d, big_nm, big_nv = [[in_res[k]] + [o[k] for o in big_out] for k in range(4)]

    def ordered(rep, ada, big, sh):
        c_ctx_, b_ada_, g_pre_, g_post_, qg_, kg_, bdf_, bdb_, glag_, bgate_ = rep
        w_in_, brc_, bra_, brg_, wout_ = big
        conv_, wdf_, wdb_ = sh
        return [c_ctx_, ada, b_ada_, g_pre_, g_post_, w_in_, conv_, qg_, kg_, wdf_, bdf_, wdb_, bdb_, glag_,
                brc_, bra_, brg_, bgate_, wout_]

    return (loss, grad_x,
            *ordered(rep_g, ada_g, big_g, sh_gr), *ordered(rep_d, ada_d, big_d, sh_d),
            *ordered(rep_nm, ada_nm, big_nm, sh_nm), *ordered(rep_nv, ada_nv, big_nv, sh_nv))
```

```python
import functools

import numpy as np
import jax
import jax.numpy as jnp
from jax import lax
from jax.experimental import pallas as pl
from jax.experimental.pallas import tpu as pltpu

F32, BF16 = jnp.float32, jnp.bfloat16
HIGHEST = lax.Precision.HIGHEST

D = 1024
DEPTH = 2
GRID_W = 64
NH, NKV, HD = 8, 2, 128
GROUP = NH // NKV
ROPE_THETA = 10000.0
ATTN_SCALE = HD ** -0.5
Q_FOLD = ATTN_SCALE * 1.4426950408889634
P_HALO = 16
GH, GDK, GDV = 4, 128, 256
GLA_RANK = 16
GLA_TAU = 16.0
CH = 64
GLA_SCALE = GDK ** -0.5
EPS = 1e-6
NDEV = 8
LANE = 128
TM = 256
ATTN_HEADS_PER_STEP = 4
ATTN_FWD_KEY_CHUNK = 2176
ATTN_BWD_KEY_CHUNK = 256
KEY_ALIGN = LANE

ADAM_LR, ADAM_B1, ADAM_B2, ADAM_EPS, ADAM_WD, ADAM_STEP = 0.001, 0.9, 0.999, 1e-08, 0.01, 10

_SEGS = (("a_b", 0, 1024), ("a_z", 3072, 1024), ("a_c", 1024, 1024), ("a_x", 2048, 1024),
         ("z_attn", 5632, 1024), ("zg", 8736, 1024), ("gv", 7680, 1024), ("gq", 6656, 512), ("gk", 7168, 512),
         ("q", 4096, 1024), ("mg", 9760, 3072), ("k", 5120, 256), ("v", 5376, 256), ("r", 8704, 32))
DP_BLOCKS = {"conv_a": ("a_b", 2048), "conv_b": ("a_c", 2048), "branch": ("z_attn", 2048), "gla": ("gv", 2048),
             "q": ("q", 1024), "merge": ("mg", 3072), "tail": ("k", 1024)}
IN_WIDTH = 12832
NP = 13312
OFF = {}
_o = 0
for _n, _s, _w in _SEGS:
    OFF[_n] = _o
    _o += _w
R_PAD = 128


def _cparams(ngrid, vmem_mb):
    return pltpu.CompilerParams(dimension_semantics=("arbitrary",) * ngrid, vmem_limit_bytes=vmem_mb << 20)


def _pick(n, cands):
    for c in cands:
        if n % c == 0:
            return c
    return n


def _sigmoid(x):
    return 1.0 / (1.0 + jnp.exp(-x))


ADAM_SRC_BYTES = 8 << 20
ADAM_ROW_BYTES = 1 << 20


def _all_gather(xs, name):
    return _comm_alone(_GatherRider(xs), name)


_HBM = pl.BlockSpec(memory_space=pl.ANY)


class _Rider:
    def __init__(self, xs, out_shapes, remote_copies=NDEV - 1):
        self.xs, self.n = list(xs), len(xs)
        self.out_shape = [jax.ShapeDtypeStruct(s, x.dtype) for s, x in zip(out_shapes, xs)]
        self.scratch = [pltpu.SemaphoreType.DMA((remote_copies * self.n,)),
                        pltpu.SemaphoreType.DMA((remote_copies * self.n,)), pltpu.SemaphoreType.DMA((self.n,))]


class _GatherRider(_Rider):
    def __init__(self, xs):
        super().__init__(xs, [(NDEV,) + x.shape for x in xs])

    def _parts(self, x_refs, out_refs, sems):
        n = self.n
        send_sems, recv_sems, local_sems = sems
        mx, my, mc = lax.axis_index("x"), lax.axis_index("y"), lax.axis_index("c")
        me, sibling = (mx, my, mc), (mx, my, 1 - mc)
        chips = [(1 - mx, my), (mx, 1 - my), (1 - mx, 1 - my)]

        def slot(a, px, py, pc):
            return out_refs[a].at[4 * px + 2 * py + pc]

        def copy(k, a, block, to, own=False):
            return pltpu.make_async_remote_copy(
                src_ref=x_refs[a] if own else slot(a, *block), dst_ref=slot(a, *block),
                send_sem=send_sems.at[k * n + a], recv_sem=recv_sems.at[k * n + a],
                device_id=to, device_id_type=pl.DeviceIdType.MESH)

        mine = [pltpu.make_async_copy(x_refs[a], slot(a, *me), local_sems.at[a]) for a in range(n)]
        first = [copy(0, a, me, sibling, own=True) for a in range(n)]
        first += [copy(1 + j, a, me, (*chip, mc), own=True) for a in range(n) for j, chip in enumerate(chips)]
        landed = [copy(1 + j, a, (*chip, mc), me) for a in range(n) for j, chip in enumerate(chips)]
        passed = [copy(4 + j, a, (*chip, mc), sibling) for a in range(n) for j, chip in enumerate(chips)]
        from_sibling = [copy(0, a, sibling, me) for a in range(n)]
        from_sibling += [copy(4 + j, a, (*chip, 1 - mc), me) for a in range(n) for j, chip in enumerate(chips)]
        return mine, first, landed, passed, from_sibling

    def start(self, x_refs, out_refs, sems):
        mine, first, _, _, _ = self._parts(x_refs, out_refs, sems)
        for cp in mine + first:
            cp.start()

    def middle(self, x_refs, out_refs, sems):
        _, _, landed, passed, _ = self._parts(x_refs, out_refs, sems)
        for got, fwd in zip(landed, passed):
            got.wait_recv()
            fwd.start()

    def finish(self, x_refs, out_refs, sems):
        mine, first, _, passed, from_sibling = self._parts(x_refs, out_refs, sems)
        for cp in from_sibling:
            cp.wait_recv()
        for cp in first + passed:
            cp.wait_send()
        for cp in mine:
            cp.wait()


class _ExchangeRider(_Rider):
    def __init__(self, xs, chips_only=False):
        self.chips_only = chips_only
        super().__init__(xs, [x.shape for x in xs], 3 if chips_only else NDEV - 1)

    def _parts(self, x_refs, out_refs, sems):
        n = self.n
        send_sems, recv_sems, local_sems = sems
        mx, my, mc = lax.axis_index("x"), lax.axis_index("y"), lax.axis_index("c")
        me = 2 * mx + my if self.chips_only else 4 * mx + 2 * my + mc
        mine = [pltpu.make_async_copy(x_refs[a].at[me], out_refs[a].at[me], local_sems.at[a]) for a in range(n)]
        copies = []
        for a in range(n):
            for rel in range(1, 4 if self.chips_only else NDEV):
                bits = rel << 1 if self.chips_only else rel
                px = (1 - mx) if bits & 4 else mx
                py = (1 - my) if bits & 2 else my
                pc = (1 - mc) if bits & 1 else mc
                peer = 2 * px + py if self.chips_only else 4 * px + 2 * py + pc
                k = (rel - 1) * n + a
                copies.append(pltpu.make_async_remote_copy(
                    src_ref=x_refs[a].at[peer], dst_ref=out_refs[a].at[me],
                    send_sem=send_sems.at[k], recv_sem=recv_sems.at[k],
                    device_id=(px, py, pc), device_id_type=pl.DeviceIdType.MESH))
        return mine, copies

    def start(self, x_refs, out_refs, sems):
        mine, copies = self._parts(x_refs, out_refs, sems)
        for cp in mine + copies:
            cp.start()

    def middle(self, x_refs, out_refs, sems):
        pass

    def finish(self, x_refs, out_refs, sems):
        mine, copies = self._parts(x_refs, out_refs, sems)
        for cp in copies:
            cp.wait_recv()
        for cp in copies:
            cp.wait_send()
        for cp in mine:
            cp.wait()


class _SwapRider(_Rider):
    def __init__(self, xs):
        super().__init__(xs, [x.shape for x in xs], 1)

    def _parts(self, x_refs, out_refs, sems):
        send_sems, recv_sems, _ = sems
        sibling = (lax.axis_index("x"), lax.axis_index("y"), 1 - lax.axis_index("c"))
        return [pltpu.make_async_remote_copy(
            src_ref=x_refs[a], dst_ref=out_refs[a], send_sem=send_sems.at[a], recv_sem=recv_sems.at[a],
            device_id=sibling, device_id_type=pl.DeviceIdType.MESH) for a in range(self.n)]

    def start(self, x_refs, out_refs, sems):
        for cp in self._parts(x_refs, out_refs, sems):
            cp.start()

    def middle(self, x_refs, out_refs, sems):
        pass

    def finish(self, x_refs, out_refs, sems):
        copies = self._parts(x_refs, out_refs, sems)
        for cp in copies:
            cp.wait_recv()
        for cp in copies:
            cp.wait_send()


class _Riders:
    def __init__(self, riders):
        self.riders = list(riders)
        self.xs = [x for r in self.riders for x in r.xs]
        self.n = len(self.xs)
        self.out_shape = [s for r in self.riders for s in r.out_shape]
        self.scratch = [s for r in self.riders for s in r.scratch]

    def _each(self, method, x_refs, out_refs, sems):
        a = b = 0
        for r in self.riders:
            getattr(r, method)(x_refs[a:a + r.n], out_refs[a:a + r.n], sems[b:b + len(r.scratch)])
            a, b = a + r.n, b + len(r.scratch)

    def start(self, *refs):
        self._each("start", *refs)

    def middle(self, *refs):
        self._each("middle", *refs)

    def finish(self, *refs):
        self._each("finish", *refs)


def _comm_alone(rider, name):
    n = rider.n

    def body(*refs):
        x_refs, out_refs, sems = refs[:n], refs[n:2 * n], refs[2 * n:]
        rider.start(x_refs, out_refs, sems)
        rider.middle(x_refs, out_refs, sems)
        rider.finish(x_refs, out_refs, sems)

    return pl.pallas_call(
        body, name=name, out_shape=tuple(rider.out_shape), in_specs=[_HBM] * n, out_specs=(_HBM,) * n,
        scratch_shapes=rider.scratch,
    )(*rider.xs)


def _with_rider(body, nin, nout, rider, first, mid, last):
    if rider is None:
        return body
    n = rider.n

    def wrapped(*refs):
        ins, x_refs = refs[:nin], refs[nin:nin + n]
        outs, out_refs = refs[nin + n:nin + n + nout], refs[nin + n + nout:nin + 2 * n + nout]
        ns = len(rider.scratch)
        scratch, sems = refs[nin + 2 * n + nout:len(refs) - ns], refs[len(refs) - ns:]

        @pl.when(first())
        def _():
            rider.start(x_refs, out_refs, sems)

        body(*ins, *outs, *scratch)

        @pl.when(mid())
        def _():
            rider.middle(x_refs, out_refs, sems)

        @pl.when(last())
        def _():
            rider.finish(x_refs, out_refs, sems)

    return wrapped


def _mm(a, b, name, ta=False, tb=False, out_dtype=F32, bias=None, precise=False, tm=None, tn=None, tk=None, rider=None):
    m, k = (a.shape[1], a.shape[0]) if ta else a.shape
    n = b.shape[0] if tb else b.shape[1]
    assert k == (b.shape[1] if tb else b.shape[0])
    tm = tm or _pick(m, (1088, 1024, 512, 256, 128))
    tn = tn or _pick(n, (1024, 512, 384, 256, 128))
    tk = tk or _pick(k, (1024, 1088, 512, 256, 128))
    nk = k // tk
    dn = (((0 if ta else 1,), (1 if tb else 0,)), ((), ()))

    def body(*refs):
        if bias is None:
            a_ref, b_ref, o_ref = refs[:3]
            bias_ref = None
        else:
            a_ref, b_ref, bias_ref, o_ref = refs[:4]
        x, y = a_ref[...], b_ref[...]
        if precise:
            p = lax.dot_general(x.astype(F32), y.astype(F32), dn, preferred_element_type=F32, precision=HIGHEST)
        else:
            p = lax.dot_general(x.astype(BF16), y.astype(BF16), dn, preferred_element_type=F32)

        def finish(acc):
            if bias_ref is not None:
                acc = acc + bias_ref[...]
            o_ref[...] = acc.astype(out_dtype)

        if nk == 1:
            finish(p)
        else:
            acc_ref = refs[-1]
            kk = pl.program_id(2)

            @pl.when(kk == 0)
            def _():
                acc_ref[...] = p

            @pl.when(kk > 0)
            def _():
                acc_ref[...] += p

            @pl.when(kk == nk - 1)
            def _():
                finish(acc_ref[...])

    a_spec = pl.BlockSpec((tk, tm), lambda i, j, kk: (kk, i)) if ta else pl.BlockSpec((tm, tk), lambda i, j, kk: (i, kk))
    b_spec = pl.BlockSpec((tn, tk), lambda i, j, kk: (j, kk)) if tb else pl.BlockSpec((tk, tn), lambda i, j, kk: (kk, j))
    in_specs = [a_spec, b_spec]
    args = [a, b]
    if bias is not None:
        in_specs.append(pl.BlockSpec((1, tn), lambda i, j, kk: (0, j)))
        args.append(bias)
    grid = (m // tm, n // tn, nk)
    out_spec = pl.BlockSpec((tm, tn), lambda i, j, kk: (i, j))
    scratch = [pltpu.VMEM((tm, tn), F32)] if nk > 1 else []
    if rider is None:
        return pl.pallas_call(
            body, name=name, grid=grid, in_specs=in_specs, out_specs=out_spec,
            out_shape=jax.ShapeDtypeStruct((m, n), out_dtype), scratch_shapes=scratch, compiler_params=_cparams(3, 56),
        )(*args)

    def at(step):
        return lambda: ((pl.program_id(0) == step[0]) & (pl.program_id(1) == step[1]) & (pl.program_id(2) == step[2]))

    end = tuple(g - 1 for g in grid)
    step = grid[0] * grid[1] * grid[2] * 7 // 8
    late = (step // (grid[1] * grid[2]), step // grid[2] % grid[1], step % grid[2])
    return pl.pallas_call(
        _with_rider(body, len(args), 1, rider, at((0, 0, 0)), at(late), at(end)),
        name=name, grid=grid, in_specs=in_specs + [_HBM] * rider.n, out_specs=(out_spec,) + (_HBM,) * rider.n,
        out_shape=(jax.ShapeDtypeStruct((m, n), out_dtype),) + tuple(rider.out_shape),
        scratch_shapes=scratch + rider.scratch, compiler_params=_cparams(3, 56),
    )(*args, *rider.xs)


def _ada_in(cc):
    def body(c_ref, s_ref, d_ref):
        x = c_ref[...]
        sg = _sigmoid(x)
        s_ref[...] = x * sg
        d_ref[...] = sg * (1.0 + x * (1.0 - sg))

    return pl.pallas_call(body, name="ada_in", out_shape=(jax.ShapeDtypeStruct(cc.shape, F32),) * 2)(cc)


def _cctx_grad(t0, t1, dsilu):
    def body(a_ref, b_ref, d_ref, o_ref):
        o_ref[...] = (a_ref[...] + b_ref[...]) * d_ref[...]

    return pl.pallas_call(body, name="cctx_grad", out_shape=jax.ShapeDtypeStruct(t0.shape, F32))(t0, t1, dsilu)


def _seg_spec(nct, rows=3):
    return pl.BlockSpec((None, rows, D), lambda i: (jnp.where(i >= nct, 1, 0), 0, 0))


def _prenorm_fwd(x, g_pre, mod3, nct, name):
    t = x.shape[0]

    def body(x_ref, g_ref, mod_ref, h_ref):
        xv = x_ref[...]
        r = lax.rsqrt(jnp.mean(xv * xv, axis=-1, keepdims=True) + EPS)
        y = xv * r * g_ref[...]
        h_ref[...] = (y * (1.0 + mod_ref[1:2, :]) + mod_ref[0:1, :]).astype(BF16)

    return pl.pallas_call(
        body, name=name, grid=(t // TM,),
        in_specs=[pl.BlockSpec((TM, D), lambda i: (i, 0)), pl.BlockSpec((1, D), lambda i: (0, 0)), _seg_spec(nct)],
        out_specs=pl.BlockSpec((TM, D), lambda i: (i, 0)),
        out_shape=jax.ShapeDtypeStruct((t, D), BF16), compiler_params=_cparams(1, 32),
    )(x, g_pre, mod3)


def _prenorm_bwd(dh, x, dxo, g_pre, mod3, nct, name):
    t = x.shape[0]

    def body(dh_ref, x_ref, dxo_ref, g_ref, mod_ref, dx_ref, dsh_ref, dsc_ref, dg_ref):
        i = pl.program_id(0)
        xv, dhv, g = x_ref[...], dh_ref[...], g_ref[...]
        r = lax.rsqrt(jnp.mean(xv * xv, axis=-1, keepdims=True) + EPS)
        xh = xv * r
        dy = dhv * (1.0 + mod_ref[1:2, :])
        dxh = dy * g
        dx_ref[...] = dxo_ref[...] + r * (dxh - xh * jnp.mean(dxh * xh, axis=-1, keepdims=True))

        @pl.when((i == 0) | (i == nct))
        def _():
            dsh_ref[...] = jnp.zeros_like(dsh_ref)
            dsc_ref[...] = jnp.zeros_like(dsc_ref)

        @pl.when(i == 0)
        def _():
            dg_ref[...] = jnp.zeros_like(dg_ref)

        dsh_ref[...] += jnp.sum(dhv, axis=0, keepdims=True)
        dsc_ref[...] += jnp.sum(dhv * (xh * g), axis=0, keepdims=True)
        dg_ref[...] += jnp.sum(dy * xh, axis=0, keepdims=True)

    row = pl.BlockSpec((TM, D), lambda i: (i, 0))
    seg8 = pl.BlockSpec((None, 8, D), lambda i: (jnp.where(i >= nct, 1, 0), 0, 0))
    return pl.pallas_call(
        body, name=name, grid=(t // TM,),
        in_specs=[row, row, row, pl.BlockSpec((1, D), lambda i: (0, 0)), _seg_spec(nct)],
        out_specs=(row, seg8, seg8, pl.BlockSpec((8, D), lambda i: (0, 0))),
        out_shape=(jax.ShapeDtypeStruct((t, D), F32), jax.ShapeDtypeStruct((2, 8, D), F32),
                   jax.ShapeDtypeStruct((2, 8, D), F32), jax.ShapeDtypeStruct((8, D), F32)),
        compiler_params=_cparams(1, 32),
    )(dh, x, dxo, g_pre, mod3)


def _post_fwd(x, out, g_post, mod3, nct, name):
    t = x.shape[0]

    def body(x_ref, o_ref, g_ref, mod_ref, y_ref):
        ov = o_ref[...]
        r = lax.rsqrt(jnp.mean(ov * ov, axis=-1, keepdims=True) + EPS)
        y_ref[...] = x_ref[...] + mod_ref[2:3, :] * (ov * r * g_ref[...])

    row = pl.BlockSpec((TM, D), lambda i: (i, 0))
    return pl.pallas_call(
        body, name=name, grid=(t // TM,),
        in_specs=[row, row, pl.BlockSpec((1, D), lambda i: (0, 0)), _seg_spec(nct)],
        out_specs=row, out_shape=jax.ShapeDtypeStruct((t, D), F32), compiler_params=_cparams(1, 32),
    )(x, out, g_post, mod3)


def _post_bwd(dxo, out, g_post, mod3, nct, name):
    t = out.shape[0]

    def body(dx_ref, o_ref, g_ref, mod_ref, do_ref, dgt_ref, dg_ref):
        i = pl.program_id(0)
        ov, dxv, g = o_ref[...], dx_ref[...], g_ref[...]
        r = lax.rsqrt(jnp.mean(ov * ov, axis=-1, keepdims=True) + EPS)
        nh = ov * r
        dn = dxv * mod_ref[2:3, :]
        dnh = dn * g
        do_ref[...] = (r * (dnh - nh * jnp.mean(dnh * nh, axis=-1, keepdims=True))).astype(BF16)

        @pl.when((i == 0) | (i == nct))
        def _():
            dgt_ref[...] = jnp.zeros_like(dgt_ref)

        @pl.when(i == 0)
        def _():
            dg_ref[...] = jnp.zeros_like(dg_ref)

        dgt_ref[...] += jnp.sum(dxv * (nh * g), axis=0, keepdims=True)
        dg_ref[...] += jnp.sum(dn * nh, axis=0, keepdims=True)

    row = pl.BlockSpec((TM, D), lambda i: (i, 0))
    seg8 = pl.BlockSpec((None, 8, D), lambda i: (jnp.where(i >= nct, 1, 0), 0, 0))
    return pl.pallas_call(
        body, name=name, grid=(t // TM,),
        in_specs=[row, row, pl.BlockSpec((1, D), lambda i: (0, 0)), _seg_spec(nct)],
        out_specs=(row, seg8, pl.BlockSpec((8, D), lambda i: (0, 0))),
        out_shape=(jax.ShapeDtypeStruct((t, D), BF16), jax.ShapeDtypeStruct((2, 8, D), F32),
                   jax.ShapeDtypeStruct((8, D), F32)),
        compiler_params=_cparams(1, 32),
    )(dxo, out, g_post, mod3)


def _loss_grad(y, target, nct, name):
    t = y.shape[0]

    def body(y_ref, t_ref, dy_ref, l_ref):
        i = pl.program_id(0)

        @pl.when(i == 0)
        def _():
            l_ref[...] = jnp.zeros_like(l_ref)

        @pl.when(i < nct)
        def _():
            dy_ref[...] = jnp.zeros_like(dy_ref)

        @pl.when(i >= nct)
        def _():
            err = y_ref[...] - t_ref[...]
            dy_ref[...] = err / D
            l_ref[...] += jnp.sum(jnp.sum(err * err, axis=1, keepdims=True), axis=0, keepdims=True)

    row = pl.BlockSpec((TM, D), lambda i: (i, 0))
    return pl.pallas_call(
        body, name=name, grid=(t // TM,),
        in_specs=[row, pl.BlockSpec((TM, D), lambda i: (jnp.maximum(i - nct, 0), 0))],
        out_specs=(row, pl.BlockSpec((8, LANE), lambda i: (0, 0))),
        out_shape=(jax.ShapeDtypeStruct((t, D), F32), jax.ShapeDtypeStruct((8, LANE), F32)),
        compiler_params=_cparams(1, 32),
    )(y, target)


def _pcol(name, width):
    assert OFF[name] % width == 0
    blk = OFF[name] // width
    return pl.BlockSpec((TM, width), lambda i: (i, blk))


def _shift_rows(u, prev_row, next_row):
    n = u.shape[0]
    row = lax.broadcasted_iota(jnp.int32, u.shape, 0)
    prev = jnp.where(row == 0, prev_row, pltpu.roll(u, 1, 0))
    nxt = jnp.where(row == n - 1, next_row, pltpu.roll(u, n - 1, 0))
    return prev, nxt


def _halo_specs(width, nt, blk=0, rows=8):
    per = TM // rows
    prev = pl.BlockSpec((rows, width), lambda i: (jnp.maximum(i * per - 1, 0), blk))
    nxt = pl.BlockSpec((rows, width), lambda i: (jnp.minimum((i + 1) * per, nt * per - 1), blk))
    return prev, nxt


def _conv_fwd(p, conv_w8, nct, name):
    t = p.shape[0]
    nt = t // TM

    def body(ab_ref, ac_ref, ax_ref, az_ref, acp_ref, axp_ref, acn_ref, axn_ref, w_ref, cv_ref, ya_ref):
        i = pl.program_id(0)
        def f(ref, rows=slice(None)):
            return ref[rows, :].astype(F32)

        u = f(ac_ref) * f(ax_ref)
        mp = jnp.where((i == 0) | (i == nct), 0.0, 1.0)
        mn = jnp.where((i == nct - 1) | (i == nt - 1), 0.0, 1.0)
        last, first = slice(P_HALO - 1, P_HALO), slice(0, 1)
        prev, nxt = _shift_rows(u, f(acp_ref, last) * f(axp_ref, last) * mp, f(acn_ref, first) * f(axn_ref, first) * mn)
        cv = w_ref[0:1, :] * prev + w_ref[1:2, :] * u + w_ref[2:3, :] * nxt
        az = f(az_ref)
        cv_ref[...] = cv.astype(BF16)
        ya_ref[...] = (f(ab_ref) * cv * (az * _sigmoid(az))).astype(BF16)

    acp, acn = _halo_specs(D, nt, OFF["a_c"] // D, P_HALO)
    axp, axn = _halo_specs(D, nt, OFF["a_x"] // D, P_HALO)
    row = pl.BlockSpec((TM, D), lambda i: (i, 0))
    return pl.pallas_call(
        body, name=name, grid=(nt,),
        in_specs=[_pcol("a_b", D), _pcol("a_c", D), _pcol("a_x", D), _pcol("a_z", D), acp, axp, acn, axn,
                  pl.BlockSpec((8, D), lambda i: (0, 0))],
        out_specs=(row, row),
        out_shape=(jax.ShapeDtypeStruct((t, D), BF16), jax.ShapeDtypeStruct((t, D), BF16)),
        compiler_params=_cparams(1, 40),
    )(p, p, p, p, p, p, p, p, conv_w8)


def _dp_spec(key):
    seg, width = DP_BLOCKS[key]
    assert OFF[seg] % width == 0
    blk = OFF[seg] // width
    return pl.BlockSpec((TM, width), lambda i: (i, blk))


def _conv_bwd_a(dya, p, cv, dp, name):
    t = p.shape[0]

    def body(dy_ref, ab_ref, az_ref, cv_ref, _, dcv_ref, dp_ref):
        dy, ab = dy_ref[...].astype(F32), ab_ref[...].astype(F32)
        az, c = az_ref[...].astype(F32), cv_ref[...].astype(F32)
        sg = _sigmoid(az)
        sz = az * sg
        dcv_ref[...] = dy * ab * sz
        dp_ref[:, 0:D] = (dy * c * sz).astype(BF16)
        dp_ref[:, D:2 * D] = (dy * ab * c * (sg * (1.0 + az * (1.0 - sg)))).astype(BF16)

    row = pl.BlockSpec((TM, D), lambda i: (i, 0))
    return pl.pallas_call(
        body, name=name, grid=(t // TM,),
        in_specs=[row, _pcol("a_b", D), _pcol("a_z", D), row, _HBM], out_specs=(row, _dp_spec("conv_a")),
        out_shape=(jax.ShapeDtypeStruct((t, D), F32), jax.ShapeDtypeStruct(dp.shape, dp.dtype)),
        input_output_aliases={4: 1}, compiler_params=_cparams(1, 40),
    )(dya, p, p, cv, dp)


def _conv_bwd_b(dcv, p, conv_w8, nct, dp, name):
    t = p.shape[0]
    nt = t // TM

    def body(dcv_ref, hp_ref, hn_ref, ac_ref, ax_ref, w_ref, _, dp_ref, dw_ref):
        i = pl.program_id(0)
        d, ac, ax = dcv_ref[...], ac_ref[...].astype(F32), ax_ref[...].astype(F32)
        u = ac * ax
        mp = jnp.where((i == 0) | (i == nct), 0.0, 1.0)
        mn = jnp.where((i == nct - 1) | (i == nt - 1), 0.0, 1.0)
        dprev, dnxt = _shift_rows(d, hp_ref[7:8, :] * mp, hn_ref[0:1, :] * mn)
        du = w_ref[0:1, :] * dnxt + w_ref[1:2, :] * d + w_ref[2:3, :] * dprev
        dp_ref[:, 0:D] = (du * ax).astype(BF16)
        dp_ref[:, D:2 * D] = (du * ac).astype(BF16)

        @pl.when(i == 0)
        def _():
            dw_ref[...] = jnp.zeros_like(dw_ref)

        dw0 = jnp.sum(u * dnxt, axis=0, keepdims=True)
        dw1 = jnp.sum(u * d, axis=0, keepdims=True)
        dw2 = jnp.sum(u * dprev, axis=0, keepdims=True)
        r8 = lax.broadcasted_iota(jnp.int32, (8, D), 0)
        dw_ref[...] += jnp.where(r8 == 0, dw0, jnp.where(r8 == 1, dw1, jnp.where(r8 == 2, dw2, 0.0)))

    hp, hn = _halo_specs(D, nt)
    row = pl.BlockSpec((TM, D), lambda i: (i, 0))
    return pl.pallas_call(
        body, name=name, grid=(nt,),
        in_specs=[row, hp, hn, _pcol("a_c", D), _pcol("a_x", D), pl.BlockSpec((8, D), lambda i: (0, 0)), _HBM],
        out_specs=(_dp_spec("conv_b"), pl.BlockSpec((8, D), lambda i: (0, 0))),
        out_shape=(jax.ShapeDtypeStruct(dp.shape, dp.dtype), jax.ShapeDtypeStruct((8, D), F32)),
        input_output_aliases={6: 0}, compiler_params=_cparams(1, 40),
    )(dcv, dcv, dcv, p, p, conv_w8, dp)


def _rot_half(x):
    lane = lax.broadcasted_iota(jnp.int32, x.shape, 1)
    return jnp.where((lane % 64) < 32, pltpu.roll(x, 96, 1), pltpu.roll(x, 32, 1))


def _qk_prep_fwd(p, qg, kg, cos_t, sin_t, name):
    t = p.shape[0]

    def body(q_ref, k_ref, qg_ref, kg_ref, c_ref, s_ref, qo_ref, ko_ref):
        c, s = c_ref[...], s_ref[...]

        def one(xv, g, scale):
            y = xv * lax.rsqrt(jnp.mean(xv * xv, axis=-1, keepdims=True) + EPS) * g
            return ((y * c + _rot_half(y) * s) * scale).astype(BF16)

        for h in range(NH):
            qo_ref[:, h * HD:(h + 1) * HD] = one(q_ref[:, h * HD:(h + 1) * HD].astype(F32), qg_ref[...], Q_FOLD)
        for h in range(NKV):
            ko_ref[:, h * HD:(h + 1) * HD] = one(k_ref[:, h * HD:(h + 1) * HD].astype(F32), kg_ref[...], 1.0)

    vec = pl.BlockSpec((1, HD), lambda i: (0, 0))
    tab = pl.BlockSpec((TM, HD), lambda i: (i, 0))
    return pl.pallas_call(
        body, name=name, grid=(t // TM,),
        in_specs=[_pcol("q", NH * HD), _pcol("k", NKV * HD), vec, vec, tab, tab],
        out_specs=(pl.BlockSpec((TM, NH * HD), lambda i: (i, 0)), pl.BlockSpec((TM, NKV * HD), lambda i: (i, 0))),
        out_shape=(jax.ShapeDtypeStruct((t, NH * HD), BF16), jax.ShapeDtypeStruct((t, NKV * HD), BF16)),
        compiler_params=_cparams(1, 32),
    )(p, p, qg, kg, cos_t, sin_t)


def _qk_prep_bwd(dqr, dkr, p, qg, kg, cos_t, sin_t, dp, name):
    t = p.shape[0]

    def body(dq_ref, dk_ref, q_ref, k_ref, qg_ref, kg_ref, c_ref, s_ref, _, dqo_ref, dko_ref, dqg_ref, dkg_ref):
        i = pl.program_id(0)
        c, s = c_ref[...], s_ref[...]

        @pl.when(i == 0)
        def _():
            dqg_ref[...] = jnp.zeros_like(dqg_ref)
            dkg_ref[...] = jnp.zeros_like(dkg_ref)

        def one(dyr, xv, g):
            dy = dyr * c + _rot_half(dyr * s)
            r = lax.rsqrt(jnp.mean(xv * xv, axis=-1, keepdims=True) + EPS)
            xh = xv * r
            dxh = dy * g
            dx = r * (dxh - xh * jnp.mean(dxh * xh, axis=-1, keepdims=True))
            return dx.astype(BF16), jnp.sum(dy * xh, axis=0, keepdims=True)

        for h in range(NH):
            sl = slice(h * HD, (h + 1) * HD)
            dx, dg = one(dq_ref[:, sl] * ATTN_SCALE, q_ref[:, sl].astype(F32), qg_ref[...])
            dqo_ref[:, sl] = dx
            dqg_ref[...] += dg
        for h in range(NKV):
            sl = slice(h * HD, (h + 1) * HD)
            dx, dg = one(dk_ref[:, sl] * (ATTN_SCALE / Q_FOLD), k_ref[:, sl].astype(F32), kg_ref[...])
            dko_ref[:, sl] = dx
            dkg_ref[...] += dg

    vec = pl.BlockSpec((1, HD), lambda i: (0, 0))
    tab = pl.BlockSpec((TM, HD), lambda i: (i, 0))
    acc = pl.BlockSpec((8, HD), lambda i: (0, 0))
    qrow = pl.BlockSpec((TM, NH * HD), lambda i: (i, 0))
    krow = pl.BlockSpec((TM, NKV * HD), lambda i: (i, 0))
    return pl.pallas_call(
        body, name=name, grid=(t // TM,),
        in_specs=[qrow, krow, _pcol("q", NH * HD), _pcol("k", NKV * HD), vec, vec, tab, tab, _HBM],
        out_specs=(_dp_spec("q"), krow, acc, acc),
        out_shape=(jax.ShapeDtypeStruct(dp.shape, dp.dtype), jax.ShapeDtypeStruct((t, NKV * HD), BF16),
                   jax.ShapeDtypeStruct((8, HD), F32), jax.ShapeDtypeStruct((8, HD), F32)),
        input_output_aliases={8: 0}, compiler_params=_cparams(1, 32),
    )(dqr, dkr, p, p, qg, kg, cos_t, sin_t, dp)


def _key_chunks(n, limit):
    c = max(c for c in range(KEY_ALIGN, min(n, limit) + 1, KEY_ALIGN) if n % c == 0)
    return [(lo, lo + c) for lo in range(0, n, c)]


def _attn_fwd(qr, kr, p, nct, name, rider=None):
    t = qr.shape[0]
    nt = t // TM
    ctx = nct * TM
    vblk = OFF["v"] // HD
    hps = ATTN_HEADS_PER_STEP
    nhp, per_kv = NH // hps, GROUP // hps

    def body(q_ref, k_ref, v_ref, o_ref, lse_ref):
        def tile(nkeys):
            sls = [slice(j * HD, (j + 1) * HD) for j in range(hps)]
            qs = [q_ref[:, sl] for sl in sls]
            m = l = acc = None
            for lo, hi in _key_chunks(nkeys, ATTN_FWD_KEY_CHUNK):
                k, vb = k_ref[lo:hi, :], v_ref[lo:hi, :].astype(BF16)
                ss = [lax.dot_general(q, k, _NT, preferred_element_type=F32) for q in qs]
                mcs = [jnp.max(s, axis=-1, keepdims=True) for s in ss]
                m_new = mcs if m is None else [jnp.maximum(a, b) for a, b in zip(m, mcs)]
                es = [jnp.exp2(s - mn) for s, mn in zip(ss, m_new)]
                lcs = [jnp.sum(e, axis=-1, keepdims=True) for e in es]
                pvs = [jnp.dot(e.astype(BF16), vb, preferred_element_type=F32) for e in es]
                if m is None:
                    l, acc = lcs, pvs
                else:
                    alphas = [jnp.exp2(a - b) for a, b in zip(m, m_new)]
                    l = [x * al + y for x, al, y in zip(l, alphas, lcs)]
                    acc = [x * al + y for x, al, y in zip(acc, alphas, pvs)]
                m = m_new
            for j, sl in enumerate(sls):
                o_ref[:, sl] = (acc[j] / l[j]).astype(BF16)
                lse_ref[:, j:j + 1] = m[j] + jnp.log2(l[j])

        pl.when(pl.program_id(1) < nct)(lambda: tile(ctx))
        pl.when(pl.program_id(1) >= nct)(lambda: tile(t))

    def at(h, i):
        return lambda: (pl.program_id(0) == h) & (pl.program_id(1) == i)

    rn = 0 if rider is None else rider.n
    qspec = pl.BlockSpec((TM, hps * HD), lambda h, i: (i, h))
    return pl.pallas_call(
        _with_rider(body, 3, 2, rider, at(0, 0), at(*divmod(nhp * nt * 7 // 8, nt)), at(nhp - 1, nt - 1)),
        name=name, grid=(nhp, nt),
        in_specs=[qspec, pl.BlockSpec((t, HD), lambda h, i: (0, h // per_kv)),
                  pl.BlockSpec((t, HD), lambda h, i: (0, vblk + h // per_kv))] + [_HBM] * rn,
        out_specs=(qspec, pl.BlockSpec((None, TM, hps), lambda h, i: (h, i, 0))) + (_HBM,) * rn,
        out_shape=(jax.ShapeDtypeStruct((t, NH * HD), BF16), jax.ShapeDtypeStruct((nhp, t, hps), F32))
        + (() if rider is None else tuple(rider.out_shape)),
        scratch_shapes=[] if rider is None else rider.scratch,
        compiler_params=_cparams(2, 48),
    )(qr, kr, p, *(() if rider is None else rider.xs))


def _attn_bwd(qr, kr, p, o, lse, do, nct, name, rider=None):
    t = qr.shape[0]
    nt = t // TM
    ctx = nct * TM
    vblk = OFF["v"] // HD
    hps = ATTN_HEADS_PER_STEP

    def body(q_ref, k_ref, v_ref, o_ref, lse_ref, do_ref, dq_ref, dk_ref, dv_ref):
        g, i = pl.program_id(1), pl.program_id(2)

        @pl.when((g == 0) & (i == 0))
        def _():
            dk_ref[...] = jnp.zeros_like(dk_ref)
            dv_ref[...] = jnp.zeros_like(dv_ref)

        def tile(nkeys):
            heads = []
            for j in range(hps):
                sl = slice(j * HD, (j + 1) * HD)
                dob = do_ref[:, sl]
                drow = jnp.sum(dob.astype(F32) * o_ref[:, sl].astype(F32), axis=-1, keepdims=True)
                heads.append((sl, q_ref[:, sl], dob, drow, lse_ref[:, j:j + 1]))
            dq = [None] * hps
            for lo, hi in _key_chunks(nkeys, ATTN_BWD_KEY_CHUNK):
                k = k_ref[lo:hi, :]
                vb = v_ref[lo:hi, :].astype(BF16)
                ss = [lax.dot_general(q, k, _NT, preferred_element_type=F32) for _, q, _, _, _ in heads]
                dps = [lax.dot_general(dob, vb, _NT, preferred_element_type=F32) for _, _, dob, _, _ in heads]
                prs = [jnp.exp2(s - h[4]) for s, h in zip(ss, heads)]
                dss = [(pr * (dp - h[3])).astype(BF16) for pr, dp, h in zip(prs, dps, heads)]
                pbs = [pr.astype(BF16) for pr in prs]
                dqs = [jnp.dot(ds, k, preferred_element_type=F32) for ds in dss]
                dks = [lax.dot_general(ds, h[1], _TN, preferred_element_type=F32) for ds, h in zip(dss, heads)]
                dvs = [lax.dot_general(pb, h[2], _TN, preferred_element_type=F32) for pb, h in zip(pbs, heads)]
                dq = [x if y is None else y + x for x, y in zip(dqs, dq)]
                dk_ref[lo:hi, :] += functools.reduce(lambda a, b: a + b, dks)
                dv_ref[lo:hi, :] += functools.reduce(lambda a, b: a + b, dvs)
            for j, (sl, *_) in enumerate(heads):
                dq_ref[:, sl] = dq[j]

        pl.when(i < nct)(lambda: tile(ctx))
        pl.when(i >= nct)(lambda: tile(t))

    def at(kv, g, i):
        return lambda: (pl.program_id(0) == kv) & (pl.program_id(1) == g) & (pl.program_id(2) == i)

    rn = 0 if rider is None else rider.n
    per_kv = GROUP // hps
    qspec = pl.BlockSpec((TM, hps * HD), lambda kv, g, i: (i, kv * per_kv + g))
    kvspec = pl.BlockSpec((t, HD), lambda kv, g, i: (0, kv))
    lspec = pl.BlockSpec((None, TM, hps), lambda kv, g, i: (kv * per_kv + g, i, 0))
    return pl.pallas_call(
        _with_rider(body, 6, 3, rider, at(0, 0, 0), at(NKV - 1, 0, 0), at(NKV - 1, per_kv - 1, nt - 1)),
        name=name, grid=(NKV, per_kv, nt),
        in_specs=[qspec, kvspec, pl.BlockSpec((t, HD), lambda kv, g, i: (0, vblk + kv)), qspec, lspec, qspec]
        + [_HBM] * rn,
        out_specs=(qspec, kvspec, kvspec) + (_HBM,) * rn,
        out_shape=(jax.ShapeDtypeStruct((t, NH * HD), F32), jax.ShapeDtypeStruct((t, NKV * HD), F32),
                   jax.ShapeDtypeStruct((t, NKV * HD), F32)) + (() if rider is None else tuple(rider.out_shape)),
        scratch_shapes=[] if rider is None else rider.scratch,
        compiler_params=_cparams(3, 48),
    )(qr, kr, p, o, lse, do, *(() if rider is None else rider.xs))


def _decay_fwd(p, wd, bd, name):
    t = p.shape[0]

    def body(r_ref, w_ref, b_ref, z_ref, bc_ref):
        z = jnp.dot(r_ref[...].astype(BF16), w_ref[...].astype(BF16), preferred_element_type=F32) + b_ref[...]
        z_ref[...] = z
        la = (jnp.minimum(z, 0.0) - jnp.log(1.0 + jnp.exp(-jnp.abs(z)))) / GLA_TAU
        half = GH * GDK
        bc_ref[:, 0:half] = _chunk_sums(la[:, 0:half], False)
        bc_ref[:, half:] = _chunk_sums(la[:, half:], True)

    row = pl.BlockSpec((TM, D), lambda i: (i, 0))
    return pl.pallas_call(
        body, name=name, grid=(t // TM,),
        in_specs=[_pcol("r", R_PAD), pl.BlockSpec((R_PAD, D), lambda i: (0, 0)), pl.BlockSpec((1, D), lambda i: (0, 0))],
        out_specs=(row, row),
        out_shape=(jax.ShapeDtypeStruct((t, D), F32), jax.ShapeDtypeStruct((t, D), F32)),
        compiler_params=_cparams(1, 32),
    )(p, wd, bd)


def _chunk_order(s, ncc, nc, rev):
    if not rev:
        return s
    return jnp.where(s < ncc, ncc - 1 - s, nc - 1 - (s - ncc))


GLA_CPS = TM // CH


class _Chain:
    def __init__(self, rev, d, h, sub, refs):
        self.rev, self.d, self.h, self.sub, self.refs = rev, d, h, sub, refs
        self.rows, self.k, self.v = slice(sub * CH, (sub + 1) * CH), _hk(h), _hv(h)
        self.last = sub * CH + (0 if rev else CH - 1)


def _gla_chains(dirs, step, backward):
    return [_Chain(rev, d, h, step if rev == backward else GLA_CPS - 1 - step, refs)
            for d, (rev, refs) in enumerate(dirs) for h in range(GH)]


def _hk(h):
    return slice(h * GDK, (h + 1) * GDK)


def _hv(h):
    return slice(h * GDV, (h + 1) * GDV)


def _chunk_sums(x, from_end):
    r = lax.broadcasted_iota(jnp.int32, (CH, CH), 0)
    c = lax.broadcasted_iota(jnp.int32, (CH, CH), 1)
    tri = ((c >= r) if from_end else (c <= r)).astype(F32)
    return jnp.concatenate([jnp.dot(tri, x[lo:lo + CH], preferred_element_type=F32, precision=HIGHEST)
                            for lo in range(0, x.shape[0], CH)], axis=0)


def _gla_factors(qs, ks, bcs, bls, revs):
    r = lax.broadcasted_iota(jnp.int32, (CH, CH), 0)
    c = lax.broadcasted_iota(jnp.int32, (CH, CH), 1)
    keeps = [(c >= r) if rev else (c <= r) for rev in revs]
    qs, ks = [q.astype(F32) for q in qs], [k.astype(F32) for k in ks]
    qts = [q * GLA_SCALE * jnp.exp(bc) for q, bc in zip(qs, bcs)]
    kts = [k * jnp.exp(-bc) for k, bc in zip(ks, bcs)]
    khs = [k * jnp.exp(bl - bc) for k, bl, bc in zip(ks, bls, bcs)]
    gls = [jnp.exp(bl) for bl in bls]
    return qts, kts, gls, khs, keeps


def _gla_loads(ch):
    qs = [c.refs[0][c.rows, c.k] for c in ch]
    ks = [c.refs[1][c.rows, c.k] for c in ch]
    bcs = [c.refs[3][c.rows, c.k] for c in ch]
    bls = [c.refs[3][c.last:c.last + 1, c.k] for c in ch]
    return qs, ks, bcs, bls


_NT = (((1,), (1,)), ((), ()))
_TN = (((0,), (0,)), ((), ()))


def _gla_specs(ncs, ns, rev, backward):
    def idx(s):
        return _chunk_order((ns - 1 - s) if backward else s, ncs, ns, rev)

    wk, wv = GH * GDK, GH * GDV
    qb, kb, vb = OFF["gq"] // wk, OFF["gk"] // wk, OFF["gv"] // wv
    lab = 1 if rev else 0
    q = pl.BlockSpec((TM, wk), lambda s: (idx(s), qb))
    k = pl.BlockSpec((TM, wk), lambda s: (idx(s), kb))
    v = pl.BlockSpec((TM, wv), lambda s: (idx(s), vb))
    la = pl.BlockSpec((TM, wk), lambda s: (idx(s), lab))
    o = pl.BlockSpec((TM, wv), lambda s: (idx(s), 0))
    dk = pl.BlockSpec((TM, wk), lambda s: (idx(s), 0))
    st = pl.BlockSpec((GLA_CPS, GH, GDV, GDK), lambda s: (idx(s), 0, 0, 0))
    return q, k, v, la, o, dk, st


def _gla_fwd(p, la, ncs, name):
    t = p.shape[0]
    nc, ns = t // CH, t // TM
    specs = [_gla_specs(ncs, ns, rev, False) for rev in (False, True)]

    def body(qf, kf, vf, laf, qb_, kb_, vb_, lab, of, stf, ob, stb, s_scr):
        @pl.when(pl.program_id(0) == 0)
        def _():
            s_scr[...] = jnp.zeros_like(s_scr)

        dirs = ((False, (qf, kf, vf, laf, of, stf)), (True, (qb_, kb_, vb_, lab, ob, stb)))
        for step in range(GLA_CPS):
            ch = _gla_chains(dirs, step, False)
            qts, kts, gls, khs, keeps = _gla_factors(*_gla_loads(ch), [c.rev for c in ch])
            sts = [s_scr[c.d, c.h] for c in ch]
            for c, st in zip(ch, sts):
                c.refs[5][c.sub, c.h] = st
            vbs = [c.refs[2][c.rows, c.v].astype(BF16) for c in ch]
            qbs = [qt.astype(BF16) for qt in qts]
            a_s = [jnp.where(keep, lax.dot_general(qb, kt.astype(BF16), _NT, preferred_element_type=F32), 0.0)
                   for keep, qb, kt in zip(keeps, qbs, kts)]
            inter = [lax.dot_general(qb, st.astype(BF16), _NT, preferred_element_type=F32) for qb, st in zip(qbs, sts)]
            intra = [jnp.dot(a.astype(BF16), vb, preferred_element_type=F32) for a, vb in zip(a_s, vbs)]
            for c, x, y in zip(ch, inter, intra):
                c.refs[4][c.rows, c.v] = (x + y).astype(BF16)
            upd = [lax.dot_general(vb, kh.astype(BF16), _TN, preferred_element_type=F32) for vb, kh in zip(vbs, khs)]
            for c, st, gl, u in zip(ch, sts, gls, upd):
                s_scr[c.d, c.h] = st * gl + u

    o_shape = jax.ShapeDtypeStruct((t, GH * GDV), BF16)
    st_shape = jax.ShapeDtypeStruct((nc, GH, GDV, GDK), F32)
    return pl.pallas_call(
        body, name=name, grid=(ns,),
        in_specs=[sp for s_ in specs for sp in s_[:4]],
        out_specs=tuple(sp for s_ in specs for sp in (s_[4], s_[6])),
        out_shape=(o_shape, st_shape, o_shape, st_shape),
        scratch_shapes=[pltpu.VMEM((2, GH, GDV, GDK), F32)], compiler_params=_cparams(1, 32),
    )(p, p, p, la, p, p, p, la)


def _gla_bwd(p, la, do, stf, stb, ncs, name):
    t = p.shape[0]
    ns = t // TM
    specs = [_gla_specs(ncs, ns, rev, True) for rev in (False, True)]

    def mm(xs, ys, dims=None):
        if dims is None:
            return [jnp.dot(x, y, preferred_element_type=F32) for x, y in zip(xs, ys)]
        return [lax.dot_general(x, y, dims, preferred_element_type=F32) for x, y in zip(xs, ys)]

    def body(*refs):
        ins_f, ins_b, outs_f, outs_b, ds_scr = refs[0:6], refs[6:12], refs[12:16], refs[16:20], refs[20]

        @pl.when(pl.program_id(0) == 0)
        def _():
            ds_scr[...] = jnp.zeros_like(ds_scr)

        dirs = ((False, (*ins_f, *outs_f)), (True, (*ins_b, *outs_b)))
        row = lax.broadcasted_iota(jnp.int32, (CH, GDK), 0)
        for step in range(GLA_CPS):
            ch = _gla_chains(dirs, step, True)
            revs = [c.rev for c in ch]
            loads = _gla_loads(ch)
            bcs = loads[2]
            qts, kts, gls, khs, keeps = _gla_factors(*loads, revs)
            stvs = [c.refs[5][c.sub, c.h].astype(BF16) for c in ch]
            dsns = [ds_scr[c.d, c.h] for c in ch]
            dsbs = [x.astype(BF16) for x in dsns]
            vbs = [c.refs[2][c.rows, c.v].astype(BF16) for c in ch]
            dobs = [c.refs[4][c.rows, c.v].astype(BF16) for c in ch]
            qbs, kbs = [x.astype(BF16) for x in qts], [x.astype(BF16) for x in kts]
            a_s = [jnp.where(keep, x, 0.0).astype(BF16) for keep, x in zip(keeps, mm(qbs, kbs, _NT))]
            das = [jnp.where(keep, x, 0.0).astype(BF16) for keep, x in zip(keeps, mm(dobs, vbs, _NT))]
            dqts = [x + y for x, y in zip(mm(dobs, stvs), mm(das, kbs))]
            dkhs = mm(vbs, dsbs)
            dkts = [x + dkh * gl for x, dkh, gl in zip(mm(das, qbs, _TN), dkhs, gls)]
            for c, x, y in zip(ch, mm(a_s, dobs, _TN), mm([kh.astype(BF16) for kh in khs], dsbs, _NT)):
                c.refs[8][c.rows, c.v] = x + y
            for c, x, dsn, gl in zip(ch, mm(dobs, qbs, _TN), dsns, gls):
                ds_scr[c.d, c.h] = x + dsn * gl
            dgls = [jnp.sum(c.refs[5][c.sub, c.h] * dsn, axis=0, keepdims=True) + jnp.sum(dkh * kt, axis=0, keepdims=True)
                    for c, dsn, dkh, kt in zip(ch, dsns, dkhs, kts)]
            dbcs = [dqt * qt - dkt * kt + jnp.where(row == (0 if rev else CH - 1), dgl * gl, 0.0)
                    for rev, dqt, qt, dkt, kt, dgl, gl in zip(revs, dqts, qts, dkts, kts, dgls, gls)]
            for c, dbc, dqt, dkt, bc in zip(ch, dbcs, dqts, dkts, bcs):
                c.refs[9][c.rows, c.k] = dbc
                c.refs[6][c.rows, c.k] = dqt * (GLA_SCALE * jnp.exp(bc))
                c.refs[7][c.rows, c.k] = dkt * jnp.exp(-bc)

    k_shape = jax.ShapeDtypeStruct((t, GH * GDK), F32)
    v_shape = jax.ShapeDtypeStruct((t, GH * GDV), F32)
    res = pl.pallas_call(
        body, name=name, grid=(ns,),
        in_specs=[sp for q_s, k_s, v_s, la_s, o_s, _, st_s in specs for sp in (q_s, k_s, v_s, la_s, o_s, st_s)],
        out_specs=tuple(sp for _, _, _, _, o_s, dk_s, _ in specs for sp in (dk_s, dk_s, o_s, dk_s)),
        out_shape=(k_shape, k_shape, v_shape, k_shape) * 2,
        scratch_shapes=[pltpu.VMEM((2, GH, GDV, GDK), F32)], compiler_params=_cparams(1, 32),
    )(p, p, p, la, do, stf, p, p, p, la, do, stb)
    return res[:4], res[4:]


def _gla_merge_bwd(gf, gb, z, p, wd, dp, name):
    t = p.shape[0]
    w2 = GH * GDK

    def body(dqf, dkf, dvf, dlf, dqb, dkb, dvb, dlb, z_ref, r_ref, w_ref, _, dp_ref, dr_ref, db_ref, dw_ref):
        i = pl.program_id(0)
        dp_ref[:, 0:D] = (dvf[...] + dvb[...]).astype(BF16)
        dp_ref[:, D:D + w2] = (dqf[...] + dqb[...]).astype(BF16)
        dp_ref[:, D + w2:D + 2 * w2] = (dkf[...] + dkb[...]).astype(BF16)
        zv = z_ref[...]
        dlf_, dlb_ = _chunk_sums(dlf[...], True), _chunk_sums(dlb[...], False)
        dz = jnp.concatenate([dlf_, dlb_], axis=1) * (_sigmoid(-zv) / GLA_TAU)
        dzb = dz.astype(BF16)
        dr_ref[...] = lax.dot_general(dzb, w_ref[...].astype(BF16), _NT, preferred_element_type=F32).astype(BF16)

        @pl.when(i == 0)
        def _():
            db_ref[...] = jnp.zeros_like(db_ref)
            dw_ref[...] = jnp.zeros_like(dw_ref)

        db_ref[...] += jnp.sum(dz, axis=0, keepdims=True)
        dw_ref[...] += lax.dot_general(r_ref[...].astype(BF16), dzb, _TN, preferred_element_type=F32)

    half = pl.BlockSpec((TM, w2), lambda i: (i, 0))
    row = pl.BlockSpec((TM, D), lambda i: (i, 0))
    wspec = pl.BlockSpec((R_PAD, D), lambda i: (0, 0))
    return pl.pallas_call(
        body, name=name, grid=(t // TM,),
        in_specs=[half, half, row, half, half, half, row, half, row, _pcol("r", R_PAD), wspec, _HBM],
        out_specs=(_dp_spec("gla"), pl.BlockSpec((TM, R_PAD), lambda i: (i, 0)),
                   pl.BlockSpec((8, D), lambda i: (0, 0)), wspec),
        out_shape=(jax.ShapeDtypeStruct(dp.shape, dp.dtype), jax.ShapeDtypeStruct((t, R_PAD), BF16),
                   jax.ShapeDtypeStruct((8, D), F32), jax.ShapeDtypeStruct((R_PAD, D), F32)),
        input_output_aliases={11: 0}, compiler_params=_cparams(1, 40),
    )(*gf, *gb, z, p, wd, dp)


def _dp_tail(dk, dv, dr, dp, name):
    t = dk.shape[0]
    wk = NKV * HD

    def body(dk_ref, dv_ref, dr_ref, _, dp_ref):
        dp_ref[:, 0:wk] = dk_ref[...]
        dp_ref[:, wk:2 * wk] = dv_ref[...].astype(BF16)
        dp_ref[:, 2 * wk:2 * wk + R_PAD] = dr_ref[...]
        dp_ref[:, 2 * wk + R_PAD:] = jnp.zeros((TM, DP_BLOCKS["tail"][1] - 2 * wk - R_PAD), BF16)

    kv = pl.BlockSpec((TM, wk), lambda i: (i, 0))
    return pl.pallas_call(
        body, name=name, grid=(t // TM,),
        in_specs=[kv, kv, pl.BlockSpec((TM, R_PAD), lambda i: (i, 0)), _HBM], out_specs=_dp_spec("tail"),
        out_shape=jax.ShapeDtypeStruct(dp.shape, dp.dtype), input_output_aliases={3: 0},
        compiler_params=_cparams(1, 32),
    )(dk, dv, dr, dp)


def _branch_fwd(att, of, ob, p, gla_g, name):
    t = p.shape[0]

    def body(att_ref, of_ref, ob_ref, za_ref, zg_ref, g_ref, yb_ref, yc_ref):
        za = za_ref[...].astype(F32)
        yb_ref[...] = (att_ref[...].astype(F32) * (za * _sigmoid(za))).astype(BF16)
        for h in range(GH):
            sl = slice(h * GDV, (h + 1) * GDV)
            o = of_ref[:, sl].astype(F32) + ob_ref[:, sl].astype(F32)
            n = o * lax.rsqrt(jnp.mean(o * o, axis=-1, keepdims=True) + EPS) * g_ref[...]
            zh = zg_ref[:, sl].astype(F32)
            yc_ref[:, sl] = (n * (zh * _sigmoid(zh))).astype(BF16)

    row = pl.BlockSpec((TM, D), lambda i: (i, 0))
    return pl.pallas_call(
        body, name=name, grid=(t // TM,),
        in_specs=[row, row, row, _pcol("z_attn", D), _pcol("zg", D), pl.BlockSpec((1, GDV), lambda i: (0, 0))],
        out_specs=(row, row),
        out_shape=(jax.ShapeDtypeStruct((t, D), BF16), jax.ShapeDtypeStruct((t, D), BF16)),
        compiler_params=_cparams(1, 40),
    )(att, of, ob, p, p, gla_g)


def _branch_bwd(dyb, dyc, att, of, ob, p, gla_g, dp, name):
    t = p.shape[0]

    def body(dyb_ref, dyc_ref, att_ref, of_ref, ob_ref, za_ref, zg_ref, g_ref, _, datt_ref, do_ref, dp_ref, dg_ref):
        i = pl.program_id(0)

        @pl.when(i == 0)
        def _():
            dg_ref[...] = jnp.zeros_like(dg_ref)

        za, dyb = za_ref[...].astype(F32), dyb_ref[...].astype(F32)
        sa = _sigmoid(za)
        datt_ref[...] = (dyb * (za * sa)).astype(BF16)
        dp_ref[:, 0:D] = (dyb * att_ref[...].astype(F32) * (sa * (1.0 + za * (1.0 - sa)))).astype(BF16)
        g = g_ref[...]
        for h in range(GH):
            sl = slice(h * GDV, (h + 1) * GDV)
            o = of_ref[:, sl].astype(F32) + ob_ref[:, sl].astype(F32)
            r = lax.rsqrt(jnp.mean(o * o, axis=-1, keepdims=True) + EPS)
            oh = o * r
            zh, dyc = zg_ref[:, sl].astype(F32), dyc_ref[:, sl].astype(F32)
            sg = _sigmoid(zh)
            dn = dyc * (zh * sg)
            dp_ref[:, D + h * GDV:D + (h + 1) * GDV] = (dyc * (oh * g) * (sg * (1.0 + zh * (1.0 - sg)))).astype(BF16)
            doh = dn * g
            do_ref[:, sl] = (r * (doh - oh * jnp.mean(doh * oh, axis=-1, keepdims=True))).astype(BF16)
            dg_ref[...] += jnp.sum(dn * oh, axis=0, keepdims=True)

    row = pl.BlockSpec((TM, D), lambda i: (i, 0))
    return pl.pallas_call(
        body, name=name, grid=(t // TM,),
        in_specs=[row, row, row, row, row, _pcol("z_attn", D), _pcol("zg", D), pl.BlockSpec((1, GDV), lambda i: (0, 0)),
                  _HBM],
        out_specs=(row, row, _dp_spec("branch"), pl.BlockSpec((8, GDV), lambda i: (0, 0))),
        out_shape=(jax.ShapeDtypeStruct((t, D), BF16), jax.ShapeDtypeStruct((t, D), BF16),
                   jax.ShapeDtypeStruct(dp.shape, dp.dtype), jax.ShapeDtypeStruct((8, GDV), F32)),
        input_output_aliases={8: 2}, compiler_params=_cparams(1, 48),
    )(dyb, dyc, att, of, ob, p, p, gla_g, dp)


def _merge_fwd(bra, brb, brc, p, b_gate, name):
    t = p.shape[0]
    mgb = OFF["mg"] // D

    def body(a_ref, b_ref, c_ref, ga_ref, gb_ref, gc_ref, bg_ref, m_ref):
        m_ref[...] = (_sigmoid(ga_ref[...].astype(F32) + bg_ref[:, 0:D]) * a_ref[...].astype(F32)
                      + _sigmoid(gb_ref[...].astype(F32) + bg_ref[:, D:2 * D]) * b_ref[...].astype(F32)
                      + _sigmoid(gc_ref[...].astype(F32) + bg_ref[:, 2 * D:3 * D]) * c_ref[...].astype(F32)).astype(BF16)

    row = pl.BlockSpec((TM, D), lambda i: (i, 0))
    gates = [pl.BlockSpec((TM, D), functools.partial(lambda i, b: (i, b), b=mgb + j)) for j in range(3)]
    return pl.pallas_call(
        body, name=name, grid=(t // TM,),
        in_specs=[row, row, row, *gates, pl.BlockSpec((1, 3 * D), lambda i: (0, 0))],
        out_specs=row, out_shape=jax.ShapeDtypeStruct((t, D), BF16), compiler_params=_cparams(1, 40),
    )(bra, brb, brc, p, p, p, b_gate)


def _merge_bwd(dm, bra, brb, brc, p, b_gate, name):
    t = p.shape[0]
    mgb = OFF["mg"] // D

    def body(dm_ref, a_ref, b_ref, c_ref, ga_ref, gb_ref, gc_ref, bg_ref, da_ref, db_ref, dc_ref, dmg_ref, dbg_ref):
        i = pl.program_id(0)

        @pl.when(i == 0)
        def _():
            dbg_ref[...] = jnp.zeros_like(dbg_ref)

        dm = dm_ref[...].astype(F32)
        for j, (br_ref, g_ref, d_ref) in enumerate(((a_ref, ga_ref, da_ref), (b_ref, gb_ref, db_ref), (c_ref, gc_ref, dc_ref))):
            sl = slice(j * D, (j + 1) * D)
            g = _sigmoid(g_ref[...].astype(F32) + bg_ref[:, sl])
            d_ref[...] = (dm * g).astype(BF16)
            dmg = dm * br_ref[...].astype(F32) * (g * (1.0 - g))
            dmg_ref[:, sl] = dmg.astype(BF16)
            dbg_ref[:, sl] += jnp.sum(dmg, axis=0, keepdims=True)

    row = pl.BlockSpec((TM, D), lambda i: (i, 0))
    gates = [pl.BlockSpec((TM, D), functools.partial(lambda i, b: (i, b), b=mgb + j)) for j in range(3)]
    return pl.pallas_call(
        body, name=name, grid=(t // TM,),
        in_specs=[row, row, row, row, *gates, pl.BlockSpec((1, 3 * D), lambda i: (0, 0))],
        out_specs=(row, row, row, _dp_spec("merge"), pl.BlockSpec((8, 3 * D), lambda i: (0, 0))),
        out_shape=(jax.ShapeDtypeStruct((t, D), BF16),) * 3 + (jax.ShapeDtypeStruct((t, NP), BF16),
                                                                jax.ShapeDtypeStruct((8, 3 * D), F32)),
        compiler_params=_cparams(1, 48),
    )(dm, bra, brb, brc, p, p, p, b_gate)


def _adam_update(ns, g_ref, w_ref, m_ref, v_ref, go_ref, d_ref, mo_ref, vo_ref):
    g = g_ref[0].astype(F32)
    for s in range(1, ns):
        g = g + g_ref[s].astype(F32)
    mn = ADAM_B1 * m_ref[...] + (1.0 - ADAM_B1) * g
    vn = ADAM_B2 * v_ref[...] + (1.0 - ADAM_B2) * jnp.square(g)
    m_hat = mn / (1.0 - ADAM_B1 ** ADAM_STEP)
    v_hat = vn / (1.0 - ADAM_B2 ** ADAM_STEP)
    go_ref[...] = g
    d_ref[...] = -ADAM_LR * (m_hat / (jnp.sqrt(v_hat) + ADAM_EPS) + ADAM_WD * w_ref[...])
    mo_ref[...] = mn
    vo_ref[...] = vn


def _adamw(gsrc, w, m, v, name):
    ns, nl, r, c = gsrc.shape
    gb = gsrc.dtype.itemsize

    def fits(rows, cols):
        lanes = -(-cols // LANE) * LANE
        return ns * rows * lanes * gb <= ADAM_SRC_BYTES and rows * lanes * 4 <= ADAM_ROW_BYTES

    tr, tc = r, c
    if not fits(r, c):
        rows = [cand for cand in range(16, r, 16) if r % cand == 0 and fits(cand, c)]
        cols = [cand for cand in range(LANE, c, LANE) if c % cand == 0 and fits(r, cand)]
        if rows:
            tr = rows[-1]
        else:
            tc = cols[-1]

    def body(*refs):
        _adam_update(ns, *refs)

    row = pl.BlockSpec((None, tr, tc), lambda l, i, j: (l, i, j))
    return pl.pallas_call(
        body, name=name, grid=(nl, r // tr, c // tc),
        in_specs=[pl.BlockSpec((ns, None, tr, tc), lambda l, i, j: (0, l, i, j)), row, row, row],
        out_specs=(row,) * 4, out_shape=(jax.ShapeDtypeStruct((nl, r, c), F32),) * 4,
        compiler_params=_cparams(3, 48),
    )(gsrc, w, m, v)


def _pair_sum(a, b, name):
    s, r, c = a.shape
    tc = _pick(c, (256, 128))

    def body(a_ref, b_ref, o_ref):
        o_ref[...] = (a_ref[...].astype(F32) + b_ref[...].astype(F32)).astype(BF16)

    blk = pl.BlockSpec((None, r, tc), lambda i, j: (i, 0, j))
    return pl.pallas_call(
        body, name=name, grid=(s, c // tc), in_specs=[blk, blk], out_specs=blk,
        out_shape=jax.ShapeDtypeStruct(a.shape, BF16), compiler_params=_cparams(2, 32),
    )(a, b)


def _adamw_small(items, name):
    k = len(items)

    def body(*refs):
        for j in range(k):
            _adam_update(items[j][0].shape[0], *refs[4 * j:4 * j + 4], *refs[4 * k + 4 * j:4 * k + 4 * j + 4])

    out = pl.pallas_call(
        body, name=name,
        out_shape=tuple(jax.ShapeDtypeStruct(w.shape, F32) for _, w, _, _ in items for _ in range(4)),
    )(*[a for item in items for a in item])
    return [out[4 * j:4 * j + 4] for j in range(k)]


def _rope_tables(ctx, seq):
    n_rows = seq // GRID_W
    pairs = HD // 4
    row = jnp.repeat(jnp.arange(n_rows, dtype=F32), GRID_W)
    col = jnp.tile(jnp.arange(GRID_W, dtype=F32), n_rows)
    freqs = ROPE_THETA ** (-jnp.arange(pairs, dtype=F32) * 2.0 / (HD // 2))
    ar, ac = row[:, None] * freqs, col[:, None] * freqs
    cos_l = jnp.concatenate([jnp.cos(ar), jnp.cos(ar), jnp.cos(ac), jnp.cos(ac)], axis=1)
    sin_l = jnp.concatenate([-jnp.sin(ar), jnp.sin(ar), -jnp.sin(ac), jnp.sin(ac)], axis=1)
    cos_t = jnp.concatenate([jnp.ones((ctx, HD), F32), cos_l], axis=0)
    sin_t = jnp.concatenate([jnp.zeros((ctx, HD), F32), sin_l], axis=0)
    return cos_t, sin_t


def _to_proj_layout(wt):
    parts = [wt[s:s + wd] for _, s, wd in _SEGS]
    used = sum(wd for _, _, wd in _SEGS)
    parts.append(jnp.zeros((NP - used, wt.shape[1]), wt.dtype))
    return jnp.concatenate(parts, axis=0)


def _from_proj_layout(g):
    order = sorted(_SEGS, key=lambda sg: sg[1])
    return jnp.concatenate([g[OFF[n]:OFF[n] + wd] for n, _, wd in order], axis=0)


def _row0(a):
    return a[..., 0, :]


def kernel(x, c, ctx, c_ctx, w_ada, b_ada, g_pre, g_post, w_in, conv_w, q_norm_g, k_norm_g, w_decay_fwd, b_decay_fwd, w_decay_bwd, b_decay_bwd, gla_norm_g, w_br_conv, w_br_attn, w_br_gla, b_gate, w_out, loss_target, m_c_ctx, m_w_ada, m_b_ada, m_g_pre, m_g_post, m_w_in, m_conv_w, m_q_norm_g, m_k_norm_g, m_w_decay_fwd, m_b_decay_fwd, m_w_decay_bwd, m_b_decay_bwd, m_gla_norm_g, m_w_br_conv, m_w_br_attn, m_w_br_gla, m_b_gate, m_w_out, v_c_ctx, v_w_ada, v_b_ada, v_g_pre, v_g_post, v_w_in, v_conv_w, v_q_norm_g, v_k_norm_g, v_w_decay_fwd, v_b_decay_fwd, v_w_decay_bwd, v_b_decay_bwd, v_gla_norm_g, v_w_br_conv, v_w_br_attn, v_w_br_gla, v_b_gate, v_w_out):
    seq, n_ctx = x.shape[1], ctx.shape[1]
    assert n_ctx % TM == 0 and seq % TM == 0 and seq % GRID_W == 0
    t = n_ctx + seq
    nct = n_ctx // TM
    dev = 4 * lax.axis_index("x") + 2 * lax.axis_index("y") + lax.axis_index("c")
    ada_w = w_ada.shape[2]
    in_w = w_in.shape[2]
    br_r = w_br_conv.shape[1]

    def in_t(a, l):
        return a.transpose(2, 0, 1)[:, l, :]

    wb = [w.astype(BF16) for w in (w_ada, w_br_conv, w_br_attn, w_br_gla, w_out)]
    wall = _all_gather([wb[0][0], in_t(w_in, 0).astype(BF16), conv_w, w_decay_fwd, w_decay_bwd],
                       "gather_first")
    later_square = _GatherRider([wb[1], wb[2], wb[3], wb[4]])
    later_in = _GatherRider([in_t(w_in, 1).astype(BF16), wb[0][1]])

    def full_small(g):
        return g.transpose(1, 2, 0, 3).reshape(DEPTH, g.shape[2], NDEV * g.shape[3])

    def full_in(g):
        return _to_proj_layout(g.reshape(IN_WIDTH, D))

    def full_ada(g):
        return g.transpose(1, 0, 2).reshape(D, 3 * D)

    w_ada_f = [full_ada(wall[0]), None]
    wp = [full_in(wall[1]), None]
    conv_f, wdf_f, wdb_f = full_small(wall[2]), full_small(wall[3]), full_small(wall[4])

    cos_t, sin_t = _rope_tables(n_ctx, seq)
    cc = jnp.concatenate([c_ctx[None, :], c.reshape(1, D), jnp.zeros((6, D), F32)], axis=0)
    silu_cc, dsilu_cc = _ada_in(cc)

    conv8, wd_pad, bd = [], [], []
    for l in range(DEPTH):
        conv8.append(jnp.concatenate([conv_f[l], jnp.zeros((5, D), F32)], axis=0))
        zr = jnp.zeros((GLA_RANK, GH * GDK), F32)
        wd_pad.append(jnp.concatenate([jnp.concatenate([wdf_f[l], zr], axis=1), jnp.concatenate([zr, wdb_f[l]], axis=1),
                                       jnp.zeros((R_PAD - 2 * GLA_RANK, D), F32)], axis=0))
        bd.append(jnp.concatenate([b_decay_fwd[l], b_decay_bwd[l]])[None, :])

    xs = jnp.concatenate([ctx[0], x[0]], axis=0)
    saved = []
    for l in range(DEPTH):
        n = f"l{l}_"
        mod = _mm(silu_cc, w_ada_f[l], n + "mod", bias=b_ada[l][None, :])
        mod3 = mod[0:2].reshape(2, 3, D)
        h = _prenorm_fwd(xs, g_pre[l][None, :], mod3, nct, n + "prenorm")
        if l == 0:
            p, *got = _mm(h, wp[l], n + "proj", tb=True, out_dtype=BF16, tm=t // 2, rider=later_square)
            w_brs_f = [g.transpose(1, 0, 2, 3).reshape(DEPTH, D, D) for g in got]
        else:
            p = _mm(h, wp[l], n + "proj", tb=True, out_dtype=BF16, tm=t // 2)
        cv, ya = _conv_fwd(p, conv8[l], nct, n + "conv")
        qr, kr = _qk_prep_fwd(p, q_norm_g[l][None, :], k_norm_g[l][None, :], cos_t, sin_t, n + "qk_prep")
        att, lse, *got = _attn_fwd(qr, kr, p, nct, n + "attn", rider=later_in if l == 0 else None)
        if l == 0:
            wp[1], w_ada_f[1] = full_in(got[0]), full_ada(got[1])
        z, la = _decay_fwd(p, wd_pad[l], bd[l], n + "decay")
        of, stf, ob, stb = _gla_fwd(p, la, nct, n + "gla")
        yb, yc = _branch_fwd(att, of, ob, p, gla_norm_g[l][None, :], n + "branch")
        bra = _mm(ya, w_brs_f[0][l], n + "br_conv", out_dtype=BF16)
        brb = _mm(yb, w_brs_f[1][l], n + "br_attn", out_dtype=BF16)
        brc = _mm(yc, w_brs_f[2][l], n + "br_gla", out_dtype=BF16)
        mm_ = _merge_fwd(bra, brb, brc, p, b_gate[l][None, :], n + "merge")
        out = _mm(mm_, w_brs_f[3][l], n + "out")
        x_new = _post_fwd(xs, out, g_post[l][None, :], mod3, nct, n + "post")
        saved.append(dict(x=xs, mod3=mod3, h=h, p=p, cv=cv, ya=ya, qr=qr, kr=kr, att=att, lse=lse, z=z, la=la, of=of, ob=ob,
                          stf=stf, stb=stb, yb=yb, yc=yc, bra=bra, brb=brb, brc=brc, m=mm_, out=out))
        xs = x_new

    dx, sq = _loss_grad(xs, loss_target[0], nct, "loss")
    loss = lax.psum(0.5 * sq[0, 0] / D, ("x", "y", "c"))

    gw = {k: [None] * DEPTH for k in ("w_in", "br_conv", "br_attn", "br_gla", "out", "b_gate", "g_pre", "g_post",
                                      "conv_w", "qg", "kg", "wd", "bdec", "gla_g", "dmod")}
    dctx = []

    def in_slots(l):
        return _from_proj_layout(gw["w_in"][l]).reshape(NDEV, in_w, D)

    def br_slots(l):
        return [gw[k][l].reshape(NDEV, br_r, D) for k in ("br_conv", "br_attn", "br_gla", "out")]

    for l in reversed(range(DEPTH)):
        n = f"l{l}_b_"
        s = saved[l]
        p = s["p"]
        d_out, dgt, gw["g_post"][l] = _post_bwd(dx, s["out"], g_post[l][None, :], s["mod3"], nct, n + "post")
        dm = _mm(d_out, w_brs_f[3][l], n + "dm", tb=True, out_dtype=BF16)
        gw["out"][l] = _mm(s["m"], d_out, n + "dw_out", ta=True, out_dtype=BF16)
        dbra, dbrb, dbrc, dp, gw["b_gate"][l] = _merge_bwd(dm, s["bra"], s["brb"], s["brc"], p, b_gate[l][None, :], n + "merge")
        dya = _mm(dbra, w_brs_f[0][l], n + "dya", tb=True, out_dtype=BF16)
        dyb = _mm(dbrb, w_brs_f[1][l], n + "dyb", tb=True, out_dtype=BF16)
        dyc = _mm(dbrc, w_brs_f[2][l], n + "dyc", tb=True, out_dtype=BF16)
        gw["br_conv"][l] = _mm(s["ya"], dbra, n + "dw_conv", ta=True, out_dtype=BF16)
        gw["br_attn"][l] = _mm(s["yb"], dbrb, n + "dw_attn", ta=True, out_dtype=BF16)
        gw["br_gla"][l] = _mm(s["yc"], dbrc, n + "dw_gla", ta=True, out_dtype=BF16)
        dcv, dp = _conv_bwd_a(dya, p, s["cv"], dp, n + "conv_a")
        dp, gw["conv_w"][l] = _conv_bwd_b(dcv, p, conv8[l], nct, dp, n + "conv_b")
        datt, dgo, dp, gw["gla_g"][l] = _branch_bwd(dyb, dyc, s["att"], s["of"], s["ob"], p, gla_norm_g[l][None, :], dp, n + "branch")
        ex1 = _ExchangeRider([in_slots(DEPTH - 1)] + br_slots(DEPTH - 1)) if l == 0 else None
        dqr, dkr, dv, *got = _attn_bwd(s["qr"], s["kr"], p, s["att"], s["lse"], datt, nct, n + "attn", rider=ex1)
        if l == 0:
            recv_in1, recv_br1 = got[0], got[1:]
        dp, dk, gw["qg"][l], gw["kg"][l] = _qk_prep_bwd(dqr, dkr, p, q_norm_g[l][None, :], k_norm_g[l][None, :], cos_t, sin_t, dp, n + "qk_prep")
        gf, gb = _gla_bwd(p, s["la"], dgo, s["stf"], s["stb"], nct, n + "gla")
        dp, dr, gw["bdec"][l], gw["wd"][l] = _gla_merge_bwd(gf, gb, s["z"], p, wd_pad[l], dp, n + "gla_merge")
        dp = _dp_tail(dk, dv, dr, dp, n + "dp_tail")
        tk_in = t // 2 if t % 32 == 0 else None
        if l == 0:
            gw["w_in"][l], *recv_br0 = _mm(dp, s["h"], n + "dw_in", ta=True, out_dtype=BF16, tk=tk_in,
                                           rider=_ExchangeRider(br_slots(0)))
        else:
            gw["w_in"][l] = _mm(dp, s["h"], n + "dw_in", ta=True, out_dtype=BF16, tk=tk_in)
        if l == 0:
            core = lax.axis_index("c")
            halves = in_slots(0).reshape(NDEV // 2, 2, in_w, D)
            kept = lax.dynamic_index_in_dim(halves, core, axis=1, keepdims=False)
            sent = lax.dynamic_index_in_dim(halves, 1 - core, axis=1, keepdims=False)
            from_sibling, = _comm_alone(_SwapRider([sent]), n + "swap_dw_in")
            chip_sum = _pair_sum(kept, from_sibling, n + "chip_sum_dw_in")
            dh, recv_in0 = _mm(dp, wp[l], n + "dh", tk=NP // 4, rider=_ExchangeRider([chip_sum], chips_only=True))
        else:
            dh = _mm(dp, wp[l], n + "dh", tk=NP // 4)
        dx, dsh, dsc, gw["g_pre"][l] = _prenorm_bwd(dh, s["x"], dx, g_pre[l][None, :], s["mod3"], nct, n + "prenorm")
        dmod = jnp.stack([_row0(dsh), _row0(dsc), _row0(dgt)], axis=1).reshape(2, 3 * D)
        gw["dmod"][l] = dmod
        dmod8 = jnp.concatenate([dmod, jnp.zeros((6, 3 * D), F32)], axis=0)
        dctx.append(_mm(dmod8, w_ada_f[l], n + "dsilu", tb=True))
    grad_x = dx[n_ctx:][None]
    g_cctx = _cctx_grad(dctx[0], dctx[1], dsilu_cc)[0]

    def st2(name):
        return jnp.stack(gw[name])

    g_b_ada = jnp.stack([gw["dmod"][l][0] + gw["dmod"][l][1] for l in range(DEPTH)])
    g_bdf = jnp.stack([gw["bdec"][l][0, :GH * GDK] for l in range(DEPTH)])
    g_bdb = jnp.stack([gw["bdec"][l][0, GH * GDK:] for l in range(DEPTH)])
    g_wdf = jnp.stack([gw["wd"][l][0:GLA_RANK, :GH * GDK] for l in range(DEPTH)])
    g_wdb = jnp.stack([gw["wd"][l][GLA_RANK:2 * GLA_RANK, GH * GDK:] for l in range(DEPTH)])
    rep_grads = [g_cctx, g_b_ada, st2("g_pre")[:, 0], st2("g_post")[:, 0], st2("qg")[:, 0], st2("kg")[:, 0], g_bdf, g_bdb,
                 st2("gla_g")[:, 0], st2("b_gate")[:, 0]]
    rep_w = [c_ctx, b_ada, g_pre, g_post, q_norm_g, k_norm_g, b_decay_fwd, b_decay_bwd, gla_norm_g, b_gate]
    rep_m = [m_c_ctx, m_b_ada, m_g_pre, m_g_post, m_q_norm_g, m_k_norm_g, m_b_decay_fwd, m_b_decay_bwd, m_gla_norm_g, m_b_gate]
    rep_v = [v_c_ctx, v_b_ada, v_g_pre, v_g_post, v_q_norm_g, v_k_norm_g, v_b_decay_fwd, v_b_decay_bwd, v_gla_norm_g, v_b_gate]
    def two_d(a):
        return a.reshape(1, -1) if a.ndim == 1 else a

    def owner_slots(g):
        return g.reshape(DEPTH, g.shape[1], NDEV, g.shape[2] // NDEV).transpose(2, 0, 1, 3)

    n_rep = len(rep_grads)
    small = _comm_alone(_Riders([
        _GatherRider([two_d(g) for g in rep_grads] + [silu_cc[0:2], jnp.stack(gw["dmod"])]),
        _ExchangeRider([owner_slots(st2("conv_w")[:, 0:3]), owner_slots(g_wdf), owner_slots(g_wdb)])]),
        "exchange_small_grads")
    rep_src, (a_all, d_all), sh_src = small[:n_rep], small[n_rep:n_rep + 2], small[n_rep + 2:]
    sh_w = [conv_w, w_decay_fwd, w_decay_bwd]
    sh_m = [m_conv_w, m_w_decay_fwd, m_w_decay_bwd]
    sh_v = [v_conv_w, v_w_decay_fwd, v_w_decay_bwd]
    small_out = _adamw_small(
        [(g, two_d(w), two_d(m), two_d(v)) for g, w, m, v in zip(rep_src, rep_w, rep_m, rep_v)]
        + list(zip(sh_src, sh_w, sh_m, sh_v)), "adam_small")
    rep_g, rep_d, rep_nm, rep_nv = [[small_out[j][k].reshape(rep_w[j].shape) for j in range(n_rep)] for k in range(4)]
    sh_gr, sh_d, sh_nm, sh_nv = [[small_out[n_rep + j][k] for j in range(len(sh_w))] for k in range(4)]

    a_all = a_all.reshape(NDEV * 2, D)
    d_all = d_all.transpose(1, 0, 2, 3).reshape(DEPTH, NDEV * 2, 3 * D)
    g_ada = jnp.stack([_mm(a_all, lax.dynamic_slice_in_dim(d_all[l], dev * ada_w, ada_w, axis=1), f"dw_ada{l}",
                           ta=True, precise=True, tk=NDEV * 2) for l in range(DEPTH)])
    ada_g, ada_d, ada_nm, ada_nv = _adamw(g_ada[None], w_ada, m_w_ada, v_w_ada, "adam_ada")

    big_w = [w_br_conv, w_br_attn, w_br_gla, w_out]
    big_m = [m_w_br_conv, m_w_br_attn, m_w_br_gla, m_w_out]
    big_v = [v_w_br_conv, v_w_br_attn, v_w_br_gla, v_w_out]
    big_out = [_adamw(jnp.stack([recv_br0[j], recv_br1[j]], axis=1), big_w[j], big_m[j], big_v[j], f"adam_big{j}")
               for j in range(len(big_w))]
    in_out = [_adamw(r_[:, None], in_t(w_in, l)[None], in_t(m_w_in, l)[None], in_t(v_w_in, l)[None], f"adam_in{l}")
              for l, r_ in enumerate((recv_in0, recv_in1))]
    in_res = [jnp.stack([in_out[l][k][0] for l in range(DEPTH)], axis=1).transpose(1, 2, 0) for k in range(4)]
    big_g, big_d, big_nm, big_nv = [[in_res[k]] + [o[k] for o in big_out] for k in range(4)]

    def ordered(rep, ada, big, sh):
        c_ctx_, b_ada_, g_pre_, g_post_, qg_, kg_, bdf_, bdb_, glag_, bgate_ = rep
        w_in_, brc_, bra_, brg_, wout_ = big
        conv_, wdf_, wdb_ = sh
        return [c_ctx_, ada, b_ada_, g_pre_, g_post_, w_in_, conv_, qg_, kg_, wdf_, bdf_, wdb_, bdb_, glag_,
                brc_, bra_, brg_, bgate_, wout_]

    return (loss, grad_x,
            *ordered(rep_g, ada_g, big_g, sh_gr), *ordered(rep_d, ada_d, big_d, sh_d),
            *ordered(rep_nm, ada_nm, big_nm, sh_nm), *ordered(rep_nv, ada_nv, big_nv, sh_nv))
```

```python
import functools

import numpy as np
import jax
import jax.numpy as jnp
from jax import lax
from jax.experimental import pallas as pl
from jax.experimental.pallas import tpu as pltpu

F32, BF16 = jnp.float32, jnp.bfloat16
HIGHEST = lax.Precision.HIGHEST

D = 1024
DEPTH = 2
GRID_W = 64
NH, NKV, HD = 8, 2, 128
GROUP = NH // NKV
ROPE_THETA = 10000.0
ATTN_SCALE = HD ** -0.5
Q_FOLD = ATTN_SCALE * 1.4426950408889634
P_HALO = 16
GH, GDK, GDV = 4, 128, 256
GLA_RANK = 16
GLA_TAU = 16.0
CH = 64
GLA_SCALE = GDK ** -0.5
EPS = 1e-6
NDEV = 8
LANE = 128
TM = 256
ATTN_HEADS_PER_STEP = 4
ATTN_FWD_KEY_CHUNK = 2176
ATTN_BWD_KEY_CHUNK = 256
KEY_ALIGN = LANE

ADAM_LR, ADAM_B1, ADAM_B2, ADAM_EPS, ADAM_WD, ADAM_STEP = 0.001, 0.9, 0.999, 1e-08, 0.01, 10

_SEGS = (("a_b", 0, 1024), ("a_z", 3072, 1024), ("a_c", 1024, 1024), ("a_x", 2048, 1024),
         ("z_attn", 5632, 1024), ("zg", 8736, 1024), ("gv", 7680, 1024), ("gq", 6656, 512), ("gk", 7168, 512),
         ("q", 4096, 1024), ("mg", 9760, 3072), ("k", 5120, 256), ("v", 5376, 256), ("r", 8704, 32))
DP_BLOCKS = {"conv_a": ("a_b", 2048), "conv_b": ("a_c", 2048), "branch": ("z_attn", 2048), "gla": ("gv", 2048),
             "q": ("q", 1024), "merge": ("mg", 3072), "tail": ("k", 1024)}
IN_WIDTH = 12832
NP = 13312
OFF = {}
_o = 0
for _n, _s, _w in _SEGS:
    OFF[_n] = _o
    _o += _w
R_PAD = 128


def _cparams(ngrid, vmem_mb):
    return pltpu.CompilerParams(dimension_semantics=("arbitrary",) * ngrid, vmem_limit_bytes=vmem_mb << 20)


def _pick(n, cands):
    for c in cands:
        if n % c == 0:
            return c
    return n


def _sigmoid(x):
    return 1.0 / (1.0 + jnp.exp(-x))


ADAM_SRC_BYTES = 8 << 20
ADAM_ROW_BYTES = 1 << 20


def _all_gather(xs, name):
    return _comm_alone(_GatherRider(xs), name)


_HBM = pl.BlockSpec(memory_space=pl.ANY)


class _Rider:
    def __init__(self, xs, out_shapes, remote_copies=NDEV - 1):
        self.xs, self.n = list(xs), len(xs)
        self.out_shape = [jax.ShapeDtypeStruct(s, x.dtype) for s, x in zip(out_shapes, xs)]
        self.scratch = [pltpu.SemaphoreType.DMA((remote_copies * self.n,)),
                        pltpu.SemaphoreType.DMA((remote_copies * self.n,)), pltpu.SemaphoreType.DMA((self.n,))]


class _GatherRider(_Rider):
    def __init__(self, xs):
        super().__init__(xs, [(NDEV,) + x.shape for x in xs])

    def _parts(self, x_refs, out_refs, sems):
        n = self.n
        send_sems, recv_sems, local_sems = sems
        mx, my, mc = lax.axis_index("x"), lax.axis_index("y"), lax.axis_index("c")
        me, sibling = (mx, my, mc), (mx, my, 1 - mc)
        chips = [(1 - mx, my), (mx, 1 - my), (1 - mx, 1 - my)]

        def slot(a, px, py, pc):
            return out_refs[a].at[4 * px + 2 * py + pc]

        def copy(k, a, block, to, own=False):
            return pltpu.make_async_remote_copy(
                src_ref=x_refs[a] if own else slot(a, *block), dst_ref=slot(a, *block),
                send_sem=send_sems.at[k * n + a], recv_sem=recv_sems.at[k * n + a],
                device_id=to, device_id_type=pl.DeviceIdType.MESH)

        mine = [pltpu.make_async_copy(x_refs[a], slot(a, *me), local_sems.at[a]) for a in range(n)]
        first = [copy(0, a, me, sibling, own=True) for a in range(n)]
        first += [copy(1 + j, a, me, (*chip, mc), own=True) for a in range(n) for j, chip in enumerate(chips)]
        landed = [copy(1 + j, a, (*chip, mc), me) for a in range(n) for j, chip in enumerate(chips)]
        passed = [copy(4 + j, a, (*chip, mc), sibling) for a in range(n) for j, chip in enumerate(chips)]
        from_sibling = [copy(0, a, sibling, me) for a in range(n)]
        from_sibling += [copy(4 + j, a, (*chip, 1 - mc), me) for a in range(n) for j, chip in enumerate(chips)]
        return mine, first, landed, passed, from_sibling

    def start(self, x_refs, out_refs, sems):
        mine, first, _, _, _ = self._parts(x_refs, out_refs, sems)
        for cp in mine + first:
            cp.start()

    def middle(self, x_refs, out_refs, sems):
        _, _, landed, passed, _ = self._parts(x_refs, out_refs, sems)
        for got, fwd in zip(landed, passed):
            got.wait_recv()
            fwd.start()

    def finish(self, x_refs, out_refs, sems):
        mine, first, _, passed, from_sibling = self._parts(x_refs, out_refs, sems)
        for cp in from_sibling:
            cp.wait_recv()
        for cp in first + passed:
            cp.wait_send()
        for cp in mine:
            cp.wait()


class _ExchangeRider(_Rider):
    def __init__(self, xs, chips_only=False):
        self.chips_only = chips_only
        super().__init__(xs, [x.shape for x in xs], 3 if chips_only else NDEV - 1)

    def _parts(self, x_refs, out_refs, sems):
        n = self.n
        send_sems, recv_sems, local_sems = sems
        mx, my, mc = lax.axis_index("x"), lax.axis_index("y"), lax.axis_index("c")
        me = 2 * mx + my if self.chips_only else 4 * mx + 2 * my + mc
        mine = [pltpu.make_async_copy(x_refs[a].at[me], out_refs[a].at[me], local_sems.at[a]) for a in range(n)]
        copies = []
        for a in range(n):
            for rel in range(1, 4 if self.chips_only else NDEV):
                bits = rel << 1 if self.chips_only else rel
                px = (1 - mx) if bits & 4 else mx
                py = (1 - my) if bits & 2 else my
                pc = (1 - mc) if bits & 1 else mc
                peer = 2 * px + py if self.chips_only else 4 * px + 2 * py + pc
                k = (rel - 1) * n + a
                copies.append(pltpu.make_async_remote_copy(
                    src_ref=x_refs[a].at[peer], dst_ref=out_refs[a].at[me],
                    send_sem=send_sems.at[k], recv_sem=recv_sems.at[k],
                    device_id=(px, py, pc), device_id_type=pl.DeviceIdType.MESH))
        return mine, copies

    def start(self, x_refs, out_refs, sems):
        mine, copies = self._parts(x_refs, out_refs, sems)
        for cp in mine + copies:
            cp.start()

    def middle(self, x_refs, out_refs, sems):
        pass

    def finish(self, x_refs, out_refs, sems):
        mine, copies = self._parts(x_refs, out_refs, sems)
        for cp in copies:
            cp.wait_recv()
        for cp in copies:
            cp.wait_send()
        for cp in mine:
            cp.wait()


class _SwapRider(_Rider):
    def __init__(self, xs):
        super().__init__(xs, [x.shape for x in xs], 1)

    def _parts(self, x_refs, out_refs, sems):
        send_sems, recv_sems, _ = sems
        sibling = (lax.axis_index("x"), lax.axis_index("y"), 1 - lax.axis_index("c"))
        return [pltpu.make_async_remote_copy(
            src_ref=x_refs[a], dst_ref=out_refs[a], send_sem=send_sems.at[a], recv_sem=recv_sems.at[a],
            device_id=sibling, device_id_type=pl.DeviceIdType.MESH) for a in range(self.n)]

    def start(self, x_refs, out_refs, sems):
        for cp in self._parts(x_refs, out_refs, sems):
            cp.start()

    def middle(self, x_refs, out_refs, sems):
        pass

    def finish(self, x_refs, out_refs, sems):
        copies = self._parts(x_refs, out_refs, sems)
        for cp in copies:
            cp.wait_recv()
        for cp in copies:
            cp.wait_send()


class _Riders:
    def __init__(self, riders):
        self.riders = list(riders)
        self.xs = [x for r in self.riders for x in r.xs]
        self.n = len(self.xs)
        self.out_shape = [s for r in self.riders for s in r.out_shape]
        self.scratch = [s for r in self.riders for s in r.scratch]

    def _each(self, method, x_refs, out_refs, sems):
        a = b = 0
        for r in self.riders:
            getattr(r, method)(x_refs[a:a + r.n], out_refs[a:a + r.n], sems[b:b + len(r.scratch)])
            a, b = a + r.n, b + len(r.scratch)

    def start(self, *refs):
        self._each("start", *refs)

    def middle(self, *refs):
        self._each("middle", *refs)

    def finish(self, *refs):
        self._each("finish", *refs)


def _comm_alone(rider, name):
    n = rider.n

    def body(*refs):
        x_refs, out_refs, sems = refs[:n], refs[n:2 * n], refs[2 * n:]
        rider.start(x_refs, out_refs, sems)
        rider.middle(x_refs, out_refs, sems)
        rider.finish(x_refs, out_refs, sems)

    return pl.pallas_call(
        body, name=name, out_shape=tuple(rider.out_shape), in_specs=[_HBM] * n, out_specs=(_HBM,) * n,
        scratch_shapes=rider.scratch,
    )(*rider.xs)


def _with_rider(body, nin, nout, rider, first, mid, last):
    if rider is None:
        return body
    n = rider.n

    def wrapped(*refs):
        ins, x_refs = refs[:nin], refs[nin:nin + n]
        outs, out_refs = refs[nin + n:nin + n + nout], refs[nin + n + nout:nin + 2 * n + nout]
        ns = len(rider.scratch)
        scratch, sems = refs[nin + 2 * n + nout:len(refs) - ns], refs[len(refs) - ns:]

        @pl.when(first())
        def _():
            rider.start(x_refs, out_refs, sems)

        body(*ins, *outs, *scratch)

        @pl.when(mid())
        def _():
            rider.middle(x_refs, out_refs, sems)

        @pl.when(last())
        def _():
            rider.finish(x_refs, out_refs, sems)

    return wrapped


def _mm(a, b, name, ta=False, tb=False, out_dtype=F32, bias=None, precise=False, tm=None, tn=None, tk=None, rider=None):
    m, k = (a.shape[1], a.shape[0]) if ta else a.shape
    n = b.shape[0] if tb else b.shape[1]
    assert k == (b.shape[1] if tb else b.shape[0])
    tm = tm or _pick(m, (1088, 1024, 512, 256, 128))
    tn = tn or _pick(n, (1024, 512, 384, 256, 128))
    tk = tk or _pick(k, (1024, 1088, 512, 256, 128))
    nk = k // tk
    dn = (((0 if ta else 1,), (1 if tb else 0,)), ((), ()))

    def body(*refs):
        if bias is None:
            a_ref, b_ref, o_ref = refs[:3]
            bias_ref = None
        else:
            a_ref, b_ref, bias_ref, o_ref = refs[:4]
        x, y = a_ref[...], b_ref[...]
        if precise:
            p = lax.dot_general(x.astype(F32), y.astype(F32), dn, preferred_element_type=F32, precision=HIGHEST)
        else:
            p = lax.dot_general(x.astype(BF16), y.astype(BF16), dn, preferred_element_type=F32)

        def finish(acc):
            if bias_ref is not None:
                acc = acc + bias_ref[...]
            o_ref[...] = acc.astype(out_dtype)

        if nk == 1:
            finish(p)
        else:
            acc_ref = refs[-1]
            kk = pl.program_id(2)

            @pl.when(kk == 0)
            def _():
                acc_ref[...] = p

            @pl.when(kk > 0)
            def _():
                acc_ref[...] += p

            @pl.when(kk == nk - 1)
            def _():
                finish(acc_ref[...])

    a_spec = pl.BlockSpec((tk, tm), lambda i, j, kk: (kk, i)) if ta else pl.BlockSpec((tm, tk), lambda i, j, kk: (i, kk))
    b_spec = pl.BlockSpec((tn, tk), lambda i, j, kk: (j, kk)) if tb else pl.BlockSpec((tk, tn), lambda i, j, kk: (kk, j))
    in_specs = [a_spec, b_spec]
    args = [a, b]
    if bias is not None:
        in_specs.append(pl.BlockSpec((1, tn), lambda i, j, kk: (0, j)))
        args.append(bias)
    grid = (m // tm, n // tn, nk)
    out_spec = pl.BlockSpec((tm, tn), lambda i, j, kk: (i, j))
    scratch = [pltpu.VMEM((tm, tn), F32)] if nk > 1 else []
    if rider is None:
        return pl.pallas_call(
            body, name=name, grid=grid, in_specs=in_specs, out_specs=out_spec,
            out_shape=jax.ShapeDtypeStruct((m, n), out_dtype), scratch_shapes=scratch, compiler_params=_cparams(3, 56),
        )(*args)

    def at(step):
        return lambda: ((pl.program_id(0) == step[0]) & (pl.program_id(1) == step[1]) & (pl.program_id(2) == step[2]))

    end = tuple(g - 1 for g in grid)
    step = grid[0] * grid[1] * grid[2] * 7 // 8
    late = (step // (grid[1] * grid[2]), step // grid[2] % grid[1], step % grid[2])
    return pl.pallas_call(
        _with_rider(body, len(args), 1, rider, at((0, 0, 0)), at(late), at(end)),
        name=name, grid=grid, in_specs=in_specs + [_HBM] * rider.n, out_specs=(out_spec,) + (_HBM,) * rider.n,
        out_shape=(jax.ShapeDtypeStruct((m, n), out_dtype),) + tuple(rider.out_shape),
        scratch_shapes=scratch + rider.scratch, compiler_params=_cparams(3, 56),
    )(*args, *rider.xs)


def _ada_in(cc):
    def body(c_ref, s_ref, d_ref):
        x = c_ref[...]
        sg = _sigmoid(x)
        s_ref[...] = x * sg
        d_ref[...] = sg * (1.0 + x * (1.0 - sg))

    return pl.pallas_call(body, name="ada_in", out_shape=(jax.ShapeDtypeStruct(cc.shape, F32),) * 2)(cc)


def _cctx_grad(t0, t1, dsilu):
    def body(a_ref, b_ref, d_ref, o_ref):
        o_ref[...] = (a_ref[...] + b_ref[...]) * d_ref[...]

    return pl.pallas_call(body, name="cctx_grad", out_shape=jax.ShapeDtypeStruct(t0.shape, F32))(t0, t1, dsilu)


def _seg_spec(nct, rows=3):
    return pl.BlockSpec((None, rows, D), lambda i: (jnp.where(i >= nct, 1, 0), 0, 0))


def _prenorm_fwd(x, g_pre, mod3, nct, name):
    t = x.shape[0]

    def body(x_ref, g_ref, mod_ref, h_ref):
        xv = x_ref[...]
        r = lax.rsqrt(jnp.mean(xv * xv, axis=-1, keepdims=True) + EPS)
        y = xv * r * g_ref[...]
        h_ref[...] = (y * (1.0 + mod_ref[1:2, :]) + mod_ref[0:1, :]).astype(BF16)

    return pl.pallas_call(
        body, name=name, grid=(t // TM,),
        in_specs=[pl.BlockSpec((TM, D), lambda i: (i, 0)), pl.BlockSpec((1, D), lambda i: (0, 0)), _seg_spec(nct)],
        out_specs=pl.BlockSpec((TM, D), lambda i: (i, 0)),
        out_shape=jax.ShapeDtypeStruct((t, D), BF16), compiler_params=_cparams(1, 32),
    )(x, g_pre, mod3)


def _prenorm_bwd(dh, x, dxo, g_pre, mod3, nct, name):
    t = x.shape[0]

    def body(dh_ref, x_ref, dxo_ref, g_ref, mod_ref, dx_ref, dsh_ref, dsc_ref, dg_ref):
        i = pl.program_id(0)
        xv, dhv, g = x_ref[...], dh_ref[...], g_ref[...]
        r = lax.rsqrt(jnp.mean(xv * xv, axis=-1, keepdims=True) + EPS)
        xh = xv * r
        dy = dhv * (1.0 + mod_ref[1:2, :])
        dxh = dy * g
        dx_ref[...] = dxo_ref[...] + r * (dxh - xh * jnp.mean(dxh * xh, axis=-1, keepdims=True))

        @pl.when((i == 0) | (i == nct))
        def _():
            dsh_ref[...] = jnp.zeros_like(dsh_ref)
            dsc_ref[...] = jnp.zeros_like(dsc_ref)

        @pl.when(i == 0)
        def _():
            dg_ref[...] = jnp.zeros_like(dg_ref)

        dsh_ref[...] += jnp.sum(dhv, axis=0, keepdims=True)
        dsc_ref[...] += jnp.sum(dhv * (xh * g), axis=0, keepdims=True)
        dg_ref[...] += jnp.sum(dy * xh, axis=0, keepdims=True)

    row = pl.BlockSpec((TM, D), lambda i: (i, 0))
    seg8 = pl.BlockSpec((None, 8, D), lambda i: (jnp.where(i >= nct, 1, 0), 0, 0))
    return pl.pallas_call(
        body, name=name, grid=(t // TM,),
        in_specs=[row, row, row, pl.BlockSpec((1, D), lambda i: (0, 0)), _seg_spec(nct)],
        out_specs=(row, seg8, seg8, pl.BlockSpec((8, D), lambda i: (0, 0))),
        out_shape=(jax.ShapeDtypeStruct((t, D), F32), jax.ShapeDtypeStruct((2, 8, D), F32),
                   jax.ShapeDtypeStruct((2, 8, D), F32), jax.ShapeDtypeStruct((8, D), F32)),
        compiler_params=_cparams(1, 32),
    )(dh, x, dxo, g_pre, mod3)


def _post_fwd(x, out, g_post, mod3, nct, name):
    t = x.shape[0]

    def body(x_ref, o_ref, g_ref, mod_ref, y_ref):
        ov = o_ref[...]
        r = lax.rsqrt(jnp.mean(ov * ov, axis=-1, keepdims=True) + EPS)
        y_ref[...] = x_ref[...] + mod_ref[2:3, :] * (ov * r * g_ref[...])

    row = pl.BlockSpec((TM, D), lambda i: (i, 0))
    return pl.pallas_call(
        body, name=name, grid=(t // TM,),
        in_specs=[row, row, pl.BlockSpec((1, D), lambda i: (0, 0)), _seg_spec(nct)],
        out_specs=row, out_shape=jax.ShapeDtypeStruct((t, D), F32), compiler_params=_cparams(1, 32),
    )(x, out, g_post, mod3)


def _post_bwd(dxo, out, g_post, mod3, nct, name):
    t = out.shape[0]

    def body(dx_ref, o_ref, g_ref, mod_ref, do_ref, dgt_ref, dg_ref):
        i = pl.program_id(0)
        ov, dxv, g = o_ref[...], dx_ref[...], g_ref[...]
        r = lax.rsqrt(jnp.mean(ov * ov, axis=-1, keepdims=True) + EPS)
        nh = ov * r
        dn = dxv * mod_ref[2:3, :]
        dnh = dn * g
        do_ref[...] = (r * (dnh - nh * jnp.mean(dnh * nh, axis=-1, keepdims=True))).astype(BF16)

        @pl.when((i == 0) | (i == nct))
        def _():
            dgt_ref[...] = jnp.zeros_like(dgt_ref)

        @pl.when(i == 0)
        def _():
            dg_ref[...] = jnp.zeros_like(dg_ref)

        dgt_ref[...] += jnp.sum(dxv * (nh * g), axis=0, keepdims=True)
        dg_ref[...] += jnp.sum(dn * nh, axis=0, keepdims=True)

    row = pl.BlockSpec((TM, D), lambda i: (i, 0))
    seg8 = pl.BlockSpec((None, 8, D), lambda i: (jnp.where(i >= nct, 1, 0), 0, 0))
    return pl.pallas_call(
        body, name=name, grid=(t // TM,),
        in_specs=[row, row, pl.BlockSpec((1, D), lambda i: (0, 0)), _seg_spec(nct)],
        out_specs=(row, seg8, pl.BlockSpec((8, D), lambda i: (0, 0))),
        out_shape=(jax.ShapeDtypeStruct((t, D), BF16), jax.ShapeDtypeStruct((2, 8, D), F32),
                   jax.ShapeDtypeStruct((8, D), F32)),
        compiler_params=_cparams(1, 32),
    )(dxo, out, g_post, mod3)


def _loss_grad(y, target, nct, name):
    t = y.shape[0]

    def body(y_ref, t_ref, dy_ref, l_ref):
        i = pl.program_id(0)

        @pl.when(i == 0)
        def _():
            l_ref[...] = jnp.zeros_like(l_ref)

        @pl.when(i < nct)
        def _():
            dy_ref[...] = jnp.zeros_like(dy_ref)

        @pl.when(i >= nct)
        def _():
            err = y_ref[...] - t_ref[...]
            dy_ref[...] = err / D
            l_ref[...] += jnp.sum(jnp.sum(err * err, axis=1, keepdims=True), axis=0, keepdims=True)

    row = pl.BlockSpec((TM, D), lambda i: (i, 0))
    return pl.pallas_call(
        body, name=name, grid=(t // TM,),
        in_specs=[row, pl.BlockSpec((TM, D), lambda i: (jnp.maximum(i - nct, 0), 0))],
        out_specs=(row, pl.BlockSpec((8, LANE), lambda i: (0, 0))),
        out_shape=(jax.ShapeDtypeStruct((t, D), F32), jax.ShapeDtypeStruct((8, LANE), F32)),
        compiler_params=_cparams(1, 32),
    )(y, target)


def _pcol(name, width):
    assert OFF[name] % width == 0
    blk = OFF[name] // width
    return pl.BlockSpec((TM, width), lambda i: (i, blk))


def _shift_rows(u, prev_row, next_row):
    n = u.shape[0]
    row = lax.broadcasted_iota(jnp.int32, u.shape, 0)
    prev = jnp.where(row == 0, prev_row, pltpu.roll(u, 1, 0))
    nxt = jnp.where(row == n - 1, next_row, pltpu.roll(u, n - 1, 0))
    return prev, nxt


def _halo_specs(width, nt, blk=0, rows=8):
    per = TM // rows
    prev = pl.BlockSpec((rows, width), lambda i: (jnp.maximum(i * per - 1, 0), blk))
    nxt = pl.BlockSpec((rows, width), lambda i: (jnp.minimum((i + 1) * per, nt * per - 1), blk))
    return prev, nxt


def _conv_fwd(p, conv_w8, nct, name):
    t = p.shape[0]
    nt = t // TM

    def body(ab_ref, ac_ref, ax_ref, az_ref, acp_ref, axp_ref, acn_ref, axn_ref, w_ref, cv_ref, ya_ref):
        i = pl.program_id(0)
        def f(ref, rows=slice(None)):
            return ref[rows, :].astype(F32)

        u = f(ac_ref) * f(ax_ref)
        mp = jnp.where((i == 0) | (i == nct), 0.0, 1.0)
        mn = jnp.where((i == nct - 1) | (i == nt - 1), 0.0, 1.0)
        last, first = slice(P_HALO - 1, P_HALO), slice(0, 1)
        prev, nxt = _shift_rows(u, f(acp_ref, last) * f(axp_ref, last) * mp, f(acn_ref, first) * f(axn_ref, first) * mn)
        cv = w_ref[0:1, :] * prev + w_ref[1:2, :] * u + w_ref[2:3, :] * nxt
        az = f(az_ref)
        cv_ref[...] = cv.astype(BF16)
        ya_ref[...] = (f(ab_ref) * cv * (az * _sigmoid(az))).astype(BF16)

    acp, acn = _halo_specs(D, nt, OFF["a_c"] // D, P_HALO)
    axp, axn = _halo_specs(D, nt, OFF["a_x"] // D, P_HALO)
    row = pl.BlockSpec((TM, D), lambda i: (i, 0))
    return pl.pallas_call(
        body, name=name, grid=(nt,),
        in_specs=[_pcol("a_b", D), _pcol("a_c", D), _pcol("a_x", D), _pcol("a_z", D), acp, axp, acn, axn,
                  pl.BlockSpec((8, D), lambda i: (0, 0))],
        out_specs=(row, row),
        out_shape=(jax.ShapeDtypeStruct((t, D), BF16), jax.ShapeDtypeStruct((t, D), BF16)),
        compiler_params=_cparams(1, 40),
    )(p, p, p, p, p, p, p, p, conv_w8)


def _dp_spec(key):
    seg, width = DP_BLOCKS[key]
    assert OFF[seg] % width == 0
    blk = OFF[seg] // width
    return pl.BlockSpec((TM, width), lambda i: (i, blk))


def _conv_bwd_a(dya, p, cv, dp, name):
    t = p.shape[0]

    def body(dy_ref, ab_ref, az_ref, cv_ref, _, dcv_ref, dp_ref):
        dy, ab = dy_ref[...].astype(F32), ab_ref[...].astype(F32)
        az, c = az_ref[...].astype(F32), cv_ref[...].astype(F32)
        sg = _sigmoid(az)
        sz = az * sg
        dcv_ref[...] = dy * ab * sz
        dp_ref[:, 0:D] = (dy * c * sz).astype(BF16)
        dp_ref[:, D:2 * D] = (dy * ab * c * (sg * (1.0 + az * (1.0 - sg)))).astype(BF16)

    row = pl.BlockSpec((TM, D), lambda i: (i, 0))
    return pl.pallas_call(
        body, name=name, grid=(t // TM,),
        in_specs=[row, _pcol("a_b", D), _pcol("a_z", D), row, _HBM], out_specs=(row, _dp_spec("conv_a")),
        out_shape=(jax.ShapeDtypeStruct((t, D), F32), jax.ShapeDtypeStruct(dp.shape, dp.dtype)),
        input_output_aliases={4: 1}, compiler_params=_cparams(1, 40),
    )(dya, p, p, cv, dp)


def _conv_bwd_b(dcv, p, conv_w8, nct, dp, name):
    t = p.shape[0]
    nt = t // TM

    def body(dcv_ref, hp_ref, hn_ref, ac_ref, ax_ref, w_ref, _, dp_ref, dw_ref):
        i = pl.program_id(0)
        d, ac, ax = dcv_ref[...], ac_ref[...].astype(F32), ax_ref[...].astype(F32)
        u = ac * ax
        mp = jnp.where((i == 0) | (i == nct), 0.0, 1.0)
        mn = jnp.where((i == nct - 1) | (i == nt - 1), 0.0, 1.0)
        dprev, dnxt = _shift_rows(d, hp_ref[7:8, :] * mp, hn_ref[0:1, :] * mn)
        du = w_ref[0:1, :] * dnxt + w_ref[1:2, :] * d + w_ref[2:3, :] * dprev
        dp_ref[:, 0:D] = (du * ax).astype(BF16)
        dp_ref[:, D:2 * D] = (du * ac).astype(BF16)

        @pl.when(i == 0)
        def _():
            dw_ref[...] = jnp.zeros_like(dw_ref)

        dw0 = jnp.sum(u * dnxt, axis=0, keepdims=True)
        dw1 = jnp.sum(u * d, axis=0, keepdims=True)
        dw2 = jnp.sum(u * dprev, axis=0, keepdims=True)
        r8 = lax.broadcasted_iota(jnp.int32, (8, D), 0)
        dw_ref[...] += jnp.where(r8 == 0, dw0, jnp.where(r8 == 1, dw1, jnp.where(r8 == 2, dw2, 0.0)))

    hp, hn = _halo_specs(D, nt)
    row = pl.BlockSpec((TM, D), lambda i: (i, 0))
    return pl.pallas_call(
        body, name=name, grid=(nt,),
        in_specs=[row, hp, hn, _pcol("a_c", D), _pcol("a_x", D), pl.BlockSpec((8, D), lambda i: (0, 0)), _HBM],
        out_specs=(_dp_spec("conv_b"), pl.BlockSpec((8, D), lambda i: (0, 0))),
        out_shape=(jax.ShapeDtypeStruct(dp.shape, dp.dtype), jax.ShapeDtypeStruct((8, D), F32)),
        input_output_aliases={6: 0}, compiler_params=_cparams(1, 40),
    )(dcv, dcv, dcv, p, p, conv_w8, dp)


def _rot_half(x):
    lane = lax.broadcasted_iota(jnp.int32, x.shape, 1)
    return jnp.where((lane % 64) < 32, pltpu.roll(x, 96, 1), pltpu.roll(x, 32, 1))


def _qk_prep_fwd(p, qg, kg, cos_t, sin_t, name):
    t = p.shape[0]

    def body(q_ref, k_ref, qg_ref, kg_ref, c_ref, s_ref, qo_ref, ko_ref):
        c, s = c_ref[...], s_ref[...]

        def one(xv, g, scale):
            y = xv * lax.rsqrt(jnp.mean(xv * xv, axis=-1, keepdims=True) + EPS) * g
            return ((y * c + _rot_half(y) * s) * scale).astype(BF16)

        for h in range(NH):
            qo_ref[:, h * HD:(h + 1) * HD] = one(q_ref[:, h * HD:(h + 1) * HD].astype(F32), qg_ref[...], Q_FOLD)
        for h in range(NKV):
            ko_ref[:, h * HD:(h + 1) * HD] = one(k_ref[:, h * HD:(h + 1) * HD].astype(F32), kg_ref[...], 1.0)

    vec = pl.BlockSpec((1, HD), lambda i: (0, 0))
    tab = pl.BlockSpec((TM, HD), lambda i: (i, 0))
    return pl.pallas_call(
        body, name=name, grid=(t // TM,),
        in_specs=[_pcol("q", NH * HD), _pcol("k", NKV * HD), vec, vec, tab, tab],
        out_specs=(pl.BlockSpec((TM, NH * HD), lambda i: (i, 0)), pl.BlockSpec((TM, NKV * HD), lambda i: (i, 0))),
        out_shape=(jax.ShapeDtypeStruct((t, NH * HD), BF16), jax.ShapeDtypeStruct((t, NKV * HD), BF16)),
        compiler_params=_cparams(1, 32),
    )(p, p, qg, kg, cos_t, sin_t)


def _qk_prep_bwd(dqr, dkr, p, qg, kg, cos_t, sin_t, dp, name):
    t = p.shape[0]

    def body(dq_ref, dk_ref, q_ref, k_ref, qg_ref, kg_ref, c_ref, s_ref, _, dqo_ref, dko_ref, dqg_ref, dkg_ref):
        i = pl.program_id(0)
        c, s = c_ref[...], s_ref[...]

        @pl.when(i == 0)
        def _():
            dqg_ref[...] = jnp.zeros_like(dqg_ref)
            dkg_ref[...] = jnp.zeros_like(dkg_ref)

        def one(dyr, xv, g):
            dy = dyr * c + _rot_half(dyr * s)
            r = lax.rsqrt(jnp.mean(xv * xv, axis=-1, keepdims=True) + EPS)
            xh = xv * r
            dxh = dy * g
            dx = r * (dxh - xh * jnp.mean(dxh * xh, axis=-1, keepdims=True))
            return dx.astype(BF16), jnp.sum(dy * xh, axis=0, keepdims=True)

        for h in range(NH):
            sl = slice(h * HD, (h + 1) * HD)
            dx, dg = one(dq_ref[:, sl] * ATTN_SCALE, q_ref[:, sl].astype(F32), qg_ref[...])
            dqo_ref[:, sl] = dx
            dqg_ref[...] += dg
        for h in range(NKV):
            sl = slice(h * HD, (h + 1) * HD)
            dx, dg = one(dk_ref[:, sl] * (ATTN_SCALE / Q_FOLD), k_ref[:, sl].astype(F32), kg_ref[...])
            dko_ref[:, sl] = dx
            dkg_ref[...] += dg

    vec = pl.BlockSpec((1, HD), lambda i: (0, 0))
    tab = pl.BlockSpec((TM, HD), lambda i: (i, 0))
    acc = pl.BlockSpec((8, HD), lambda i: (0, 0))
    qrow = pl.BlockSpec((TM, NH * HD), lambda i: (i, 0))
    krow = pl.BlockSpec((TM, NKV * HD), lambda i: (i, 0))
    return pl.pallas_call(
        body, name=name, grid=(t // TM,),
        in_specs=[qrow, krow, _pcol("q", NH * HD), _pcol("k", NKV * HD), vec, vec, tab, tab, _HBM],
        out_specs=(_dp_spec("q"), krow, acc, acc),
        out_shape=(jax.ShapeDtypeStruct(dp.shape, dp.dtype), jax.ShapeDtypeStruct((t, NKV * HD), BF16),
                   jax.ShapeDtypeStruct((8, HD), F32), jax.ShapeDtypeStruct((8, HD), F32)),
        input_output_aliases={8: 0}, compiler_params=_cparams(1, 32),
    )(dqr, dkr, p, p, qg, kg, cos_t, sin_t, dp)


def _key_chunks(n, limit):
    c = max(c for c in range(KEY_ALIGN, min(n, limit) + 1, KEY_ALIGN) if n % c == 0)
    return [(lo, lo + c) for lo in range(0, n, c)]


def _attn_fwd(qr, kr, p, nct, name, rider=None):
    t = qr.shape[0]
    nt = t // TM
    ctx = nct * TM
    vblk = OFF["v"] // HD
    hps = ATTN_HEADS_PER_STEP
    nhp, per_kv = NH // hps, GROUP // hps

    def body(q_ref, k_ref, v_ref, o_ref, lse_ref):
        def tile(nkeys):
            sls = [slice(j * HD, (j + 1) * HD) for j in range(hps)]
            qs = [q_ref[:, sl] for sl in sls]
            m = l = acc = None
            for lo, hi in _key_chunks(nkeys, ATTN_FWD_KEY_CHUNK):
                k, vb = k_ref[lo:hi, :], v_ref[lo:hi, :].astype(BF16)
                ss = [lax.dot_general(q, k, _NT, preferred_element_type=F32) for q in qs]
                mcs = [jnp.max(s, axis=-1, keepdims=True) for s in ss]
                m_new = mcs if m is None else [jnp.maximum(a, b) for a, b in zip(m, mcs)]
                es = [jnp.exp2(s - mn) for s, mn in zip(ss, m_new)]
                lcs = [jnp.sum(e, axis=-1, keepdims=True) for e in es]
                pvs = [jnp.dot(e.astype(BF16), vb, preferred_element_type=F32) for e in es]
                if m is None:
                    l, acc = lcs, pvs
                else:
                    alphas = [jnp.exp2(a - b) for a, b in zip(m, m_new)]
                    l = [x * al + y for x, al, y in zip(l, alphas, lcs)]
                    acc = [x * al + y for x, al, y in zip(acc, alphas, pvs)]
                m = m_new
            for j, sl in enumerate(sls):
                o_ref[:, sl] = (acc[j] / l[j]).astype(BF16)
                lse_ref[:, j:j + 1] = m[j] + jnp.log2(l[j])

        pl.when(pl.program_id(1) < nct)(lambda: tile(ctx))
        pl.when(pl.program_id(1) >= nct)(lambda: tile(t))

    def at(h, i):
        return lambda: (pl.program_id(0) == h) & (pl.program_id(1) == i)

    rn = 0 if rider is None else rider.n
    qspec = pl.BlockSpec((TM, hps * HD), lambda h, i: (i, h))
    return pl.pallas_call(
        _with_rider(body, 3, 2, rider, at(0, 0), at(*divmod(nhp * nt * 7 // 8, nt)), at(nhp - 1, nt - 1)),
        name=name, grid=(nhp, nt),
        in_specs=[qspec, pl.BlockSpec((t, HD), lambda h, i: (0, h // per_kv)),
                  pl.BlockSpec((t, HD), lambda h, i: (0, vblk + h // per_kv))] + [_HBM] * rn,
        out_specs=(qspec, pl.BlockSpec((None, TM, hps), lambda h, i: (h, i, 0))) + (_HBM,) * rn,
        out_shape=(jax.ShapeDtypeStruct((t, NH * HD), BF16), jax.ShapeDtypeStruct((nhp, t, hps), F32))
        + (() if rider is None else tuple(rider.out_shape)),
        scratch_shapes=[] if rider is None else rider.scratch,
        compiler_params=_cparams(2, 48),
    )(qr, kr, p, *(() if rider is None else rider.xs))


def _attn_bwd(qr, kr, p, o, lse, do, nct, name, rider=None):
    t = qr.shape[0]
    nt = t // TM
    ctx = nct * TM
    vblk = OFF["v"] // HD
    hps = ATTN_HEADS_PER_STEP

    def body(q_ref, k_ref, v_ref, o_ref, lse_ref, do_ref, dq_ref, dk_ref, dv_ref):
        g, i = pl.program_id(1), pl.program_id(2)

        @pl.when((g == 0) & (i == 0))
        def _():
            dk_ref[...] = jnp.zeros_like(dk_ref)
            dv_ref[...] = jnp.zeros_like(dv_ref)

        def tile(nkeys):
            heads = []
            for j in range(hps):
                sl = slice(j * HD, (j + 1) * HD)
                dob = do_ref[:, sl]
                drow = jnp.sum(dob.astype(F32) * o_ref[:, sl].astype(F32), axis=-1, keepdims=True)
                heads.append((sl, q_ref[:, sl], dob, drow, lse_ref[:, j:j + 1]))
            dq = [None] * hps
            for lo, hi in _key_chunks(nkeys, ATTN_BWD_KEY_CHUNK):
                k = k_ref[lo:hi, :]
                vb = v_ref[lo:hi, :].astype(BF16)
                ss = [lax.dot_general(q, k, _NT, preferred_element_type=F32) for _, q, _, _, _ in heads]
                dps = [lax.dot_general(dob, vb, _NT, preferred_element_type=F32) for _, _, dob, _, _ in heads]
                prs = [jnp.exp2(s - h[4]) for s, h in zip(ss, heads)]
                dss = [(pr * (dp - h[3])).astype(BF16) for pr, dp, h in zip(prs, dps, heads)]
                pbs = [pr.astype(BF16) for pr in prs]
                dqs = [jnp.dot(ds, k, preferred_element_type=F32) for ds in dss]
                dks = [lax.dot_general(ds, h[1], _TN, preferred_element_type=F32) for ds, h in zip(dss, heads)]
                dvs = [lax.dot_general(pb, h[2], _TN, preferred_element_type=F32) for pb, h in zip(pbs, heads)]
                dq = [x if y is None else y + x for x, y in zip(dqs, dq)]
                dk_ref[lo:hi, :] += functools.reduce(lambda a, b: a + b, dks)
                dv_ref[lo:hi, :] += functools.reduce(lambda a, b: a + b, dvs)
            for j, (sl, *_) in enumerate(heads):
                dq_ref[:, sl] = dq[j]

        pl.when(i < nct)(lambda: tile(ctx))
        pl.when(i >= nct)(lambda: tile(t))

    def at(kv, g, i):
        return lambda: (pl.program_id(0) == kv) & (pl.program_id(1) == g) & (pl.program_id(2) == i)

    rn = 0 if rider is None else rider.n
    per_kv = GROUP // hps
    qspec = pl.BlockSpec((TM, hps * HD), lambda kv, g, i: (i, kv * per_kv + g))
    kvspec = pl.BlockSpec((t, HD), lambda kv, g, i: (0, kv))
    lspec = pl.BlockSpec((None, TM, hps), lambda kv, g, i: (kv * per_kv + g, i, 0))
    return pl.pallas_call(
        _with_rider(body, 6, 3, rider, at(0, 0, 0), at(NKV - 1, 0, 0), at(NKV - 1, per_kv - 1, nt - 1)),
        name=name, grid=(NKV, per_kv, nt),
        in_specs=[qspec, kvspec, pl.BlockSpec((t, HD), lambda kv, g, i: (0, vblk + kv)), qspec, lspec, qspec]
        + [_HBM] * rn,
        out_specs=(qspec, kvspec, kvspec) + (_HBM,) * rn,
        out_shape=(jax.ShapeDtypeStruct((t, NH * HD), F32), jax.ShapeDtypeStruct((t, NKV * HD), F32),
                   jax.ShapeDtypeStruct((t, NKV * HD), F32)) + (() if rider is None else tuple(rider.out_shape)),
        scratch_shapes=[] if rider is None else rider.scratch,
        compiler_params=_cparams(3, 48),
    )(qr, kr, p, o, lse, do, *(() if rider is None else rider.xs))


def _decay_fwd(p, wd, bd, name):
    t = p.shape[0]

    def body(r_ref, w_ref, b_ref, z_ref, bc_ref):
        z = jnp.dot(r_ref[...].astype(BF16), w_ref[...].astype(BF16), preferred_element_type=F32) + b_ref[...]
        z_ref[...] = z
        la = (jnp.minimum(z, 0.0) - jnp.log(1.0 + jnp.exp(-jnp.abs(z)))) / GLA_TAU
        half = GH * GDK
        bc_ref[:, 0:half] = _chunk_sums(la[:, 0:half], False)
        bc_ref[:, half:] = _chunk_sums(la[:, half:], True)

    row = pl.BlockSpec((TM, D), lambda i: (i, 0))
    return pl.pallas_call(
        body, name=name, grid=(t // TM,),
        in_specs=[_pcol("r", R_PAD), pl.BlockSpec((R_PAD, D), lambda i: (0, 0)), pl.BlockSpec((1, D), lambda i: (0, 0))],
        out_specs=(row, row),
        out_shape=(jax.ShapeDtypeStruct((t, D), F32), jax.ShapeDtypeStruct((t, D), F32)),
        compiler_params=_cparams(1, 32),
    )(p, wd, bd)


def _chunk_order(s, ncc, nc, rev):
    if not rev:
        return s
    return jnp.where(s < ncc, ncc - 1 - s, nc - 1 - (s - ncc))


GLA_CPS = TM // CH


class _Chain:
    def __init__(self, rev, d, h, sub, refs):
        self.rev, self.d, self.h, self.sub, self.refs = rev, d, h, sub, refs
        self.rows, self.k, self.v = slice(sub * CH, (sub + 1) * CH), _hk(h), _hv(h)
        self.last = sub * CH + (0 if rev else CH - 1)


def _gla_chains(dirs, step, backward):
    return [_Chain(rev, d, h, step if rev == backward else GLA_CPS - 1 - step, refs)
            for d, (rev, refs) in enumerate(dirs) for h in range(GH)]


def _hk(h):
    return slice(h * GDK, (h + 1) * GDK)


def _hv(h):
    return slice(h * GDV, (h + 1) * GDV)


def _chunk_sums(x, from_end):
    r = lax.broadcasted_iota(jnp.int32, (CH, CH), 0)
    c = lax.broadcasted_iota(jnp.int32, (CH, CH), 1)
    tri = ((c >= r) if from_end else (c <= r)).astype(F32)
    return jnp.concatenate([jnp.dot(tri, x[lo:lo + CH], preferred_element_type=F32, precision=HIGHEST)
                            for lo in range(0, x.shape[0], CH)], axis=0)


def _gla_factors(qs, ks, bcs, bls, revs):
    r = lax.broadcasted_iota(jnp.int32, (CH, CH), 0)
    c = lax.broadcasted_iota(jnp.int32, (CH, CH), 1)
    keeps = [(c >= r) if rev else (c <= r) for rev in revs]
    qs, ks = [q.astype(F32) for q in qs], [k.astype(F32) for k in ks]
    qts = [q * GLA_SCALE * jnp.exp(bc) for q, bc in zip(qs, bcs)]
    kts = [k * jnp.exp(-bc) for k, bc in zip(ks, bcs)]
    khs = [k * jnp.exp(bl - bc) for k, bl, bc in zip(ks, bls, bcs)]
    gls = [jnp.exp(bl) for bl in bls]
    return qts, kts, gls, khs, keeps


def _gla_loads(ch):
    qs = [c.refs[0][c.rows, c.k] for c in ch]
    ks = [c.refs[1][c.rows, c.k] for c in ch]
    bcs = [c.refs[3][c.rows, c.k] for c in ch]
    bls = [c.refs[3][c.last:c.last + 1, c.k] for c in ch]
    return qs, ks, bcs, bls


_NT = (((1,), (1,)), ((), ()))
_TN = (((0,), (0,)), ((), ()))


def _gla_specs(ncs, ns, rev, backward):
    def idx(s):
        return _chunk_order((ns - 1 - s) if backward else s, ncs, ns, rev)

    wk, wv = GH * GDK, GH * GDV
    qb, kb, vb = OFF["gq"] // wk, OFF["gk"] // wk, OFF["gv"] // wv
    lab = 1 if rev else 0
    q = pl.BlockSpec((TM, wk), lambda s: (idx(s), qb))
    k = pl.BlockSpec((TM, wk), lambda s: (idx(s), kb))
    v = pl.BlockSpec((TM, wv), lambda s: (idx(s), vb))
    la = pl.BlockSpec((TM, wk), lambda s: (idx(s), lab))
    o = pl.BlockSpec((TM, wv), lambda s: (idx(s), 0))
    dk = pl.BlockSpec((TM, wk), lambda s: (idx(s), 0))
    st = pl.BlockSpec((GLA_CPS, GH, GDV, GDK), lambda s: (idx(s), 0, 0, 0))
    return q, k, v, la, o, dk, st


def _gla_fwd(p, la, ncs, name):
    t = p.shape[0]
    nc, ns = t // CH, t // TM
    specs = [_gla_specs(ncs, ns, rev, False) for rev in (False, True)]

    def body(qf, kf, vf, laf, qb_, kb_, vb_, lab, of, stf, ob, stb, s_scr):
        @pl.when(pl.program_id(0) == 0)
        def _():
            s_scr[...] = jnp.zeros_like(s_scr)

        dirs = ((False, (qf, kf, vf, laf, of, stf)), (True, (qb_, kb_, vb_, lab, ob, stb)))
        for step in range(GLA_CPS):
            ch = _gla_chains(dirs, step, False)
            qts, kts, gls, khs, keeps = _gla_factors(*_gla_loads(ch), [c.rev for c in ch])
            sts = [s_scr[c.d, c.h] for c in ch]
            for c, st in zip(ch, sts):
                c.refs[5][c.sub, c.h] = st.astype(BF16)
            vbs = [c.refs[2][c.rows, c.v].astype(BF16) for c in ch]
            qbs = [qt.astype(BF16) for qt in qts]
            a_s = [jnp.where(keep, lax.dot_general(qb, kt.astype(BF16), _NT, preferred_element_type=F32), 0.0)
                   for keep, qb, kt in zip(keeps, qbs, kts)]
            inter = [lax.dot_general(qb, st.astype(BF16), _NT, preferred_element_type=F32) for qb, st in zip(qbs, sts)]
            intra = [jnp.dot(a.astype(BF16), vb, preferred_element_type=F32) for a, vb in zip(a_s, vbs)]
            for c, x, y in zip(ch, inter, intra):
                c.refs[4][c.rows, c.v] = (x + y).astype(BF16)
            upd = [lax.dot_general(vb, kh.astype(BF16), _TN, preferred_element_type=F32) for vb, kh in zip(vbs, khs)]
            for c, st, gl, u in zip(ch, sts, gls, upd):
                s_scr[c.d, c.h] = st * gl + u

    o_shape = jax.ShapeDtypeStruct((t, GH * GDV), BF16)
    st_shape = jax.ShapeDtypeStruct((nc, GH, GDV, GDK), BF16)
    return pl.pallas_call(
        body, name=name, grid=(ns,),
        in_specs=[sp for s_ in specs for sp in s_[:4]],
        out_specs=tuple(sp for s_ in specs for sp in (s_[4], s_[6])),
        out_shape=(o_shape, st_shape, o_shape, st_shape),
        scratch_shapes=[pltpu.VMEM((2, GH, GDV, GDK), F32)], compiler_params=_cparams(1, 32),
    )(p, p, p, la, p, p, p, la)


def _gla_bwd(p, la, do, stf, stb, ncs, name):
    t = p.shape[0]
    ns = t // TM
    specs = [_gla_specs(ncs, ns, rev, True) for rev in (False, True)]

    def mm(xs, ys, dims=None):
        if dims is None:
            return [jnp.dot(x, y, preferred_element_type=F32) for x, y in zip(xs, ys)]
        return [lax.dot_general(x, y, dims, preferred_element_type=F32) for x, y in zip(xs, ys)]

    def body(*refs):
        ins_f, ins_b, outs_f, outs_b, ds_scr = refs[0:6], refs[6:12], refs[12:16], refs[16:20], refs[20]

        @pl.when(pl.program_id(0) == 0)
        def _():
            ds_scr[...] = jnp.zeros_like(ds_scr)

        dirs = ((False, (*ins_f, *outs_f)), (True, (*ins_b, *outs_b)))
        row = lax.broadcasted_iota(jnp.int32, (CH, GDK), 0)
        for step in range(GLA_CPS):
            ch = _gla_chains(dirs, step, True)
            revs = [c.rev for c in ch]
            loads = _gla_loads(ch)
            bcs = loads[2]
            qts, kts, gls, khs, keeps = _gla_factors(*loads, revs)
            stvs = [c.refs[5][c.sub, c.h] for c in ch]
            dsns = [ds_scr[c.d, c.h] for c in ch]
            dsbs = [x.astype(BF16) for x in dsns]
            vbs = [c.refs[2][c.rows, c.v].astype(BF16) for c in ch]
            dobs = [c.refs[4][c.rows, c.v].astype(BF16) for c in ch]
            qbs, kbs = [x.astype(BF16) for x in qts], [x.astype(BF16) for x in kts]
            a_s = [jnp.where(keep, x, 0.0).astype(BF16) for keep, x in zip(keeps, mm(qbs, kbs, _NT))]
            das = [jnp.where(keep, x, 0.0).astype(BF16) for keep, x in zip(keeps, mm(dobs, vbs, _NT))]
            dqts = [x + y for x, y in zip(mm(dobs, stvs), mm(das, kbs))]
            dkhs = mm(vbs, dsbs)
            dkts = [x + dkh * gl for x, dkh, gl in zip(mm(das, qbs, _TN), dkhs, gls)]
            for c, x, y in zip(ch, mm(a_s, dobs, _TN), mm([kh.astype(BF16) for kh in khs], dsbs, _NT)):
                c.refs[8][c.rows, c.v] = (x + y).astype(BF16)
            for c, x, dsn, gl in zip(ch, mm(dobs, qbs, _TN), dsns, gls):
                ds_scr[c.d, c.h] = x + dsn * gl
            dgls = [jnp.sum(st.astype(F32) * dsn, axis=0, keepdims=True) + jnp.sum(dkh * kt, axis=0, keepdims=True)
                    for st, dsn, dkh, kt in zip(stvs, dsns, dkhs, kts)]
            dbcs = [dqt * qt - dkt * kt + jnp.where(row == (0 if rev else CH - 1), dgl * gl, 0.0)
                    for rev, dqt, qt, dkt, kt, dgl, gl in zip(revs, dqts, qts, dkts, kts, dgls, gls)]
            for c, dbc, dqt, dkt, bc in zip(ch, dbcs, dqts, dkts, bcs):
                c.refs[9][c.rows, c.k] = dbc
                c.refs[6][c.rows, c.k] = (dqt * (GLA_SCALE * jnp.exp(bc))).astype(BF16)
                c.refs[7][c.rows, c.k] = (dkt * jnp.exp(-bc)).astype(BF16)

    k_shape = jax.ShapeDtypeStruct((t, GH * GDK), BF16)
    v_shape = jax.ShapeDtypeStruct((t, GH * GDV), BF16)
    c_shape = jax.ShapeDtypeStruct((t, GH * GDK), F32)
    res = pl.pallas_call(
        body, name=name, grid=(ns,),
        in_specs=[sp for q_s, k_s, v_s, la_s, o_s, _, st_s in specs for sp in (q_s, k_s, v_s, la_s, o_s, st_s)],
        out_specs=tuple(sp for _, _, _, _, o_s, dk_s, _ in specs for sp in (dk_s, dk_s, o_s, dk_s)),
        out_shape=(k_shape, k_shape, v_shape, c_shape) * 2,
        scratch_shapes=[pltpu.VMEM((2, GH, GDV, GDK), F32)], compiler_params=_cparams(1, 32),
    )(p, p, p, la, do, stf, p, p, p, la, do, stb)
    return res[:4], res[4:]


def _gla_merge_bwd(gf, gb, z, p, wd, dp, name):
    t = p.shape[0]
    w2 = GH * GDK

    def body(dqf, dkf, dvf, dlf, dqb, dkb, dvb, dlb, z_ref, r_ref, w_ref, _, dp_ref, dr_ref, db_ref, dw_ref):
        i = pl.program_id(0)
        dp_ref[:, 0:D] = (dvf[...].astype(F32) + dvb[...].astype(F32)).astype(BF16)
        dp_ref[:, D:D + w2] = (dqf[...].astype(F32) + dqb[...].astype(F32)).astype(BF16)
        dp_ref[:, D + w2:D + 2 * w2] = (dkf[...].astype(F32) + dkb[...].astype(F32)).astype(BF16)
        zv = z_ref[...]
        dlf_, dlb_ = _chunk_sums(dlf[...], True), _chunk_sums(dlb[...], False)
        dz = jnp.concatenate([dlf_, dlb_], axis=1) * (_sigmoid(-zv) / GLA_TAU)
        dzb = dz.astype(BF16)
        dr_ref[...] = lax.dot_general(dzb, w_ref[...].astype(BF16), _NT, preferred_element_type=F32).astype(BF16)

        @pl.when(i == 0)
        def _():
            db_ref[...] = jnp.zeros_like(db_ref)
            dw_ref[...] = jnp.zeros_like(dw_ref)

        db_ref[...] += jnp.sum(dz, axis=0, keepdims=True)
        dw_ref[...] += lax.dot_general(r_ref[...].astype(BF16), dzb, _TN, preferred_element_type=F32)

    half = pl.BlockSpec((TM, w2), lambda i: (i, 0))
    row = pl.BlockSpec((TM, D), lambda i: (i, 0))
    wspec = pl.BlockSpec((R_PAD, D), lambda i: (0, 0))
    return pl.pallas_call(
        body, name=name, grid=(t // TM,),
        in_specs=[half, half, row, half, half, half, row, half, row, _pcol("r", R_PAD), wspec, _HBM],
        out_specs=(_dp_spec("gla"), pl.BlockSpec((TM, R_PAD), lambda i: (i, 0)),
                   pl.BlockSpec((8, D), lambda i: (0, 0)), wspec),
        out_shape=(jax.ShapeDtypeStruct(dp.shape, dp.dtype), jax.ShapeDtypeStruct((t, R_PAD), BF16),
                   jax.ShapeDtypeStruct((8, D), F32), jax.ShapeDtypeStruct((R_PAD, D), F32)),
        input_output_aliases={11: 0}, compiler_params=_cparams(1, 40),
    )(*gf, *gb, z, p, wd, dp)


def _dp_tail(dk, dv, dr, dp, name):
    t = dk.shape[0]
    wk = NKV * HD

    def body(dk_ref, dv_ref, dr_ref, _, dp_ref):
        dp_ref[:, 0:wk] = dk_ref[...]
        dp_ref[:, wk:2 * wk] = dv_ref[...].astype(BF16)
        dp_ref[:, 2 * wk:2 * wk + R_PAD] = dr_ref[...]
        dp_ref[:, 2 * wk + R_PAD:] = jnp.zeros((TM, DP_BLOCKS["tail"][1] - 2 * wk - R_PAD), BF16)

    kv = pl.BlockSpec((TM, wk), lambda i: (i, 0))
    return pl.pallas_call(
        body, name=name, grid=(t // TM,),
        in_specs=[kv, kv, pl.BlockSpec((TM, R_PAD), lambda i: (i, 0)), _HBM], out_specs=_dp_spec("tail"),
        out_shape=jax.ShapeDtypeStruct(dp.shape, dp.dtype), input_output_aliases={3: 0},
        compiler_params=_cparams(1, 32),
    )(dk, dv, dr, dp)


def _branch_fwd(att, of, ob, p, gla_g, name):
    t = p.shape[0]

    def body(att_ref, of_ref, ob_ref, za_ref, zg_ref, g_ref, yb_ref, yc_ref):
        za = za_ref[...].astype(F32)
        yb_ref[...] = (att_ref[...].astype(F32) * (za * _sigmoid(za))).astype(BF16)
        for h in range(GH):
            sl = slice(h * GDV, (h + 1) * GDV)
            o = of_ref[:, sl].astype(F32) + ob_ref[:, sl].astype(F32)
            n = o * lax.rsqrt(jnp.mean(o * o, axis=-1, keepdims=True) + EPS) * g_ref[...]
            zh = zg_ref[:, sl].astype(F32)
            yc_ref[:, sl] = (n * (zh * _sigmoid(zh))).astype(BF16)

    row = pl.BlockSpec((TM, D), lambda i: (i, 0))
    return pl.pallas_call(
        body, name=name, grid=(t // TM,),
        in_specs=[row, row, row, _pcol("z_attn", D), _pcol("zg", D), pl.BlockSpec((1, GDV), lambda i: (0, 0))],
        out_specs=(row, row),
        out_shape=(jax.ShapeDtypeStruct((t, D), BF16), jax.ShapeDtypeStruct((t, D), BF16)),
        compiler_params=_cparams(1, 40),
    )(att, of, ob, p, p, gla_g)


def _branch_bwd(dyb, dyc, att, of, ob, p, gla_g, dp, name):
    t = p.shape[0]

    def body(dyb_ref, dyc_ref, att_ref, of_ref, ob_ref, za_ref, zg_ref, g_ref, _, datt_ref, do_ref, dp_ref, dg_ref):
        i = pl.program_id(0)

        @pl.when(i == 0)
        def _():
            dg_ref[...] = jnp.zeros_like(dg_ref)

        za, dyb = za_ref[...].astype(F32), dyb_ref[...].astype(F32)
        sa = _sigmoid(za)
        datt_ref[...] = (dyb * (za * sa)).astype(BF16)
        dp_ref[:, 0:D] = (dyb * att_ref[...].astype(F32) * (sa * (1.0 + za * (1.0 - sa)))).astype(BF16)
        g = g_ref[...]
        for h in range(GH):
            sl = slice(h * GDV, (h + 1) * GDV)
            o = of_ref[:, sl].astype(F32) + ob_ref[:, sl].astype(F32)
            r = lax.rsqrt(jnp.mean(o * o, axis=-1, keepdims=True) + EPS)
            oh = o * r
            zh, dyc = zg_ref[:, sl].astype(F32), dyc_ref[:, sl].astype(F32)
            sg = _sigmoid(zh)
            dn = dyc * (zh * sg)
            dp_ref[:, D + h * GDV:D + (h + 1) * GDV] = (dyc * (oh * g) * (sg * (1.0 + zh * (1.0 - sg)))).astype(BF16)
            doh = dn * g
            do_ref[:, sl] = (r * (doh - oh * jnp.mean(doh * oh, axis=-1, keepdims=True))).astype(BF16)
            dg_ref[...] += jnp.sum(dn * oh, axis=0, keepdims=True)

    row = pl.BlockSpec((TM, D), lambda i: (i, 0))
    return pl.pallas_call(
        body, name=name, grid=(t // TM,),
        in_specs=[row, row, row, row, row, _pcol("z_attn", D), _pcol("zg", D), pl.BlockSpec((1, GDV), lambda i: (0, 0)),
                  _HBM],
        out_specs=(row, row, _dp_spec("branch"), pl.BlockSpec((8, GDV), lambda i: (0, 0))),
        out_shape=(jax.ShapeDtypeStruct((t, D), BF16), jax.ShapeDtypeStruct((t, D), BF16),
                   jax.ShapeDtypeStruct(dp.shape, dp.dtype), jax.ShapeDtypeStruct((8, GDV), F32)),
        input_output_aliases={8: 2}, compiler_params=_cparams(1, 48),
    )(dyb, dyc, att, of, ob, p, p, gla_g, dp)


def _merge_fwd(bra, brb, brc, p, b_gate, name):
    t = p.shape[0]
    mgb = OFF["mg"] // D

    def body(a_ref, b_ref, c_ref, ga_ref, gb_ref, gc_ref, bg_ref, m_ref):
        m_ref[...] = (_sigmoid(ga_ref[...].astype(F32) + bg_ref[:, 0:D]) * a_ref[...].astype(F32)
                      + _sigmoid(gb_ref[...].astype(F32) + bg_ref[:, D:2 * D]) * b_ref[...].astype(F32)
                      + _sigmoid(gc_ref[...].astype(F32) + bg_ref[:, 2 * D:3 * D]) * c_ref[...].astype(F32)).astype(BF16)

    row = pl.BlockSpec((TM, D), lambda i: (i, 0))
    gates = [pl.BlockSpec((TM, D), functools.partial(lambda i, b: (i, b), b=mgb + j)) for j in range(3)]
    return pl.pallas_call(
        body, name=name, grid=(t // TM,),
        in_specs=[row, row, row, *gates, pl.BlockSpec((1, 3 * D), lambda i: (0, 0))],
        out_specs=row, out_shape=jax.ShapeDtypeStruct((t, D), BF16), compiler_params=_cparams(1, 40),
    )(bra, brb, brc, p, p, p, b_gate)


def _merge_bwd(dm, bra, brb, brc, p, b_gate, name):
    t = p.shape[0]
    mgb = OFF["mg"] // D

    def body(dm_ref, a_ref, b_ref, c_ref, ga_ref, gb_ref, gc_ref, bg_ref, da_ref, db_ref, dc_ref, dmg_ref, dbg_ref):
        i = pl.program_id(0)

        @pl.when(i == 0)
        def _():
            dbg_ref[...] = jnp.zeros_like(dbg_ref)

        dm = dm_ref[...].astype(F32)
        for j, (br_ref, g_ref, d_ref) in enumerate(((a_ref, ga_ref, da_ref), (b_ref, gb_ref, db_ref), (c_ref, gc_ref, dc_ref))):
            sl = slice(j * D, (j + 1) * D)
            g = _sigmoid(g_ref[...].astype(F32) + bg_ref[:, sl])
            d_ref[...] = (dm * g).astype(BF16)
            dmg = dm * br_ref[...].astype(F32) * (g * (1.0 - g))
            dmg_ref[:, sl] = dmg.astype(BF16)
            dbg_ref[:, sl] += jnp.sum(dmg, axis=0, keepdims=True)

    row = pl.BlockSpec((TM, D), lambda i: (i, 0))
    gates = [pl.BlockSpec((TM, D), functools.partial(lambda i, b: (i, b), b=mgb + j)) for j in range(3)]
    return pl.pallas_call(
        body, name=name, grid=(t // TM,),
        in_specs=[row, row, row, row, *gates, pl.BlockSpec((1, 3 * D), lambda i: (0, 0))],
        out_specs=(row, row, row, _dp_spec("merge"), pl.BlockSpec((8, 3 * D), lambda i: (0, 0))),
        out_shape=(jax.ShapeDtypeStruct((t, D), BF16),) * 3 + (jax.ShapeDtypeStruct((t, NP), BF16),
                                                                jax.ShapeDtypeStruct((8, 3 * D), F32)),
        compiler_params=_cparams(1, 48),
    )(dm, bra, brb, brc, p, p, p, b_gate)


def _adam_update(ns, g_ref, w_ref, m_ref, v_ref, go_ref, d_ref, mo_ref, vo_ref):
    g = g_ref[0].astype(F32)
    for s in range(1, ns):
        g = g + g_ref[s].astype(F32)
    mn = ADAM_B1 * m_ref[...] + (1.0 - ADAM_B1) * g
    vn = ADAM_B2 * v_ref[...] + (1.0 - ADAM_B2) * jnp.square(g)
    m_hat = mn / (1.0 - ADAM_B1 ** ADAM_STEP)
    v_hat = vn / (1.0 - ADAM_B2 ** ADAM_STEP)
    go_ref[...] = g
    d_ref[...] = -ADAM_LR * (m_hat / (jnp.sqrt(v_hat) + ADAM_EPS) + ADAM_WD * w_ref[...])
    mo_ref[...] = mn
    vo_ref[...] = vn


def _adamw(gsrc, w, m, v, name):
    ns, nl, r, c = gsrc.shape
    gb = gsrc.dtype.itemsize

    def fits(rows, cols):
        lanes = -(-cols // LANE) * LANE
        return ns * rows * lanes * gb <= ADAM_SRC_BYTES and rows * lanes * 4 <= ADAM_ROW_BYTES

    tr, tc = r, c
    if not fits(r, c):
        rows = [cand for cand in range(16, r, 16) if r % cand == 0 and fits(cand, c)]
        cols = [cand for cand in range(LANE, c, LANE) if c % cand == 0 and fits(r, cand)]
        if rows:
            tr = rows[-1]
        else:
            tc = cols[-1]

    def body(*refs):
        _adam_update(ns, *refs)

    row = pl.BlockSpec((None, tr, tc), lambda l, i, j: (l, i, j))
    return pl.pallas_call(
        body, name=name, grid=(nl, r // tr, c // tc),
        in_specs=[pl.BlockSpec((ns, None, tr, tc), lambda l, i, j: (0, l, i, j)), row, row, row],
        out_specs=(row,) * 4, out_shape=(jax.ShapeDtypeStruct((nl, r, c), F32),) * 4,
        compiler_params=_cparams(3, 48),
    )(gsrc, w, m, v)


def _pair_sum(a, b, name):
    s, r, c = a.shape
    tc = _pick(c, (256, 128))

    def body(a_ref, b_ref, o_ref):
        o_ref[...] = (a_ref[...].astype(F32) + b_ref[...].astype(F32)).astype(BF16)

    blk = pl.BlockSpec((None, r, tc), lambda i, j: (i, 0, j))
    return pl.pallas_call(
        body, name=name, grid=(s, c // tc), in_specs=[blk, blk], out_specs=blk,
        out_shape=jax.ShapeDtypeStruct(a.shape, BF16), compiler_params=_cparams(2, 32),
    )(a, b)


def _adamw_small(items, name):
    k = len(items)

    def body(*refs):
        for j in range(k):
            _adam_update(items[j][0].shape[0], *refs[4 * j:4 * j + 4], *refs[4 * k + 4 * j:4 * k + 4 * j + 4])

    out = pl.pallas_call(
        body, name=name,
        out_shape=tuple(jax.ShapeDtypeStruct(w.shape, F32) for _, w, _, _ in items for _ in range(4)),
    )(*[a for item in items for a in item])
    return [out[4 * j:4 * j + 4] for j in range(k)]


def _rope_tables(ctx, seq):
    n_rows = seq // GRID_W
    pairs = HD // 4
    row = jnp.repeat(jnp.arange(n_rows, dtype=F32), GRID_W)
    col = jnp.tile(jnp.arange(GRID_W, dtype=F32), n_rows)
    freqs = ROPE_THETA ** (-jnp.arange(pairs, dtype=F32) * 2.0 / (HD // 2))
    ar, ac = row[:, None] * freqs, col[:, None] * freqs
    cos_l = jnp.concatenate([jnp.cos(ar), jnp.cos(ar), jnp.cos(ac), jnp.cos(ac)], axis=1)
    sin_l = jnp.concatenate([-jnp.sin(ar), jnp.sin(ar), -jnp.sin(ac), jnp.sin(ac)], axis=1)
    cos_t = jnp.concatenate([jnp.ones((ctx, HD), F32), cos_l], axis=0)
    sin_t = jnp.concatenate([jnp.zeros((ctx, HD), F32), sin_l], axis=0)
    return cos_t, sin_t


def _to_proj_layout(wt):
    parts = [wt[s:s + wd] for _, s, wd in _SEGS]
    used = sum(wd for _, _, wd in _SEGS)
    parts.append(jnp.zeros((NP - used, wt.shape[1]), wt.dtype))
    return jnp.concatenate(parts, axis=0)


def _from_proj_layout(g):
    order = sorted(_SEGS, key=lambda sg: sg[1])
    return jnp.concatenate([g[OFF[n]:OFF[n] + wd] for n, _, wd in order], axis=0)


def _row0(a):
    return a[..., 0, :]


def kernel(x, c, ctx, c_ctx, w_ada, b_ada, g_pre, g_post, w_in, conv_w, q_norm_g, k_norm_g, w_decay_fwd, b_decay_fwd, w_decay_bwd, b_decay_bwd, gla_norm_g, w_br_conv, w_br_attn, w_br_gla, b_gate, w_out, loss_target, m_c_ctx, m_w_ada, m_b_ada, m_g_pre, m_g_post, m_w_in, m_conv_w, m_q_norm_g, m_k_norm_g, m_w_decay_fwd, m_b_decay_fwd, m_w_decay_bwd, m_b_decay_bwd, m_gla_norm_g, m_w_br_conv, m_w_br_attn, m_w_br_gla, m_b_gate, m_w_out, v_c_ctx, v_w_ada, v_b_ada, v_g_pre, v_g_post, v_w_in, v_conv_w, v_q_norm_g, v_k_norm_g, v_w_decay_fwd, v_b_decay_fwd, v_w_decay_bwd, v_b_decay_bwd, v_gla_norm_g, v_w_br_conv, v_w_br_attn, v_w_br_gla, v_b_gate, v_w_out):
    seq, n_ctx = x.shape[1], ctx.shape[1]
    assert n_ctx % TM == 0 and seq % TM == 0 and seq % GRID_W == 0
    t = n_ctx + seq
    nct = n_ctx // TM
    dev = 4 * lax.axis_index("x") + 2 * lax.axis_index("y") + lax.axis_index("c")
    ada_w = w_ada.shape[2]
    in_w = w_in.shape[2]
    br_r = w_br_conv.shape[1]

    def in_t(a, l):
        return a.transpose(2, 0, 1)[:, l, :]

    wb = [w.astype(BF16) for w in (w_ada, w_br_conv, w_br_attn, w_br_gla, w_out)]
    wall = _all_gather([wb[0][0], in_t(w_in, 0).astype(BF16), conv_w, w_decay_fwd, w_decay_bwd],
                       "gather_first")
    later_square = _GatherRider([wb[1], wb[2], wb[3], wb[4]])
    later_in = _GatherRider([in_t(w_in, 1).astype(BF16), wb[0][1]])

    def full_small(g):
        return g.transpose(1, 2, 0, 3).reshape(DEPTH, g.shape[2], NDEV * g.shape[3])

    def full_in(g):
        return _to_proj_layout(g.reshape(IN_WIDTH, D))

    def full_ada(g):
        return g.transpose(1, 0, 2).reshape(D, 3 * D)

    w_ada_f = [full_ada(wall[0]), None]
    wp = [full_in(wall[1]), None]
    conv_f, wdf_f, wdb_f = full_small(wall[2]), full_small(wall[3]), full_small(wall[4])

    cos_t, sin_t = _rope_tables(n_ctx, seq)
    cc = jnp.concatenate([c_ctx[None, :], c.reshape(1, D), jnp.zeros((6, D), F32)], axis=0)
    silu_cc, dsilu_cc = _ada_in(cc)

    conv8, wd_pad, bd = [], [], []
    for l in range(DEPTH):
        conv8.append(jnp.concatenate([conv_f[l], jnp.zeros((5, D), F32)], axis=0))
        zr = jnp.zeros((GLA_RANK, GH * GDK), F32)
        wd_pad.append(jnp.concatenate([jnp.concatenate([wdf_f[l], zr], axis=1), jnp.concatenate([zr, wdb_f[l]], axis=1),
                                       jnp.zeros((R_PAD - 2 * GLA_RANK, D), F32)], axis=0))
        bd.append(jnp.concatenate([b_decay_fwd[l], b_decay_bwd[l]])[None, :])

    xs = jnp.concatenate([ctx[0], x[0]], axis=0)
    saved = []
    for l in range(DEPTH):
        n = f"l{l}_"
        mod = _mm(silu_cc, w_ada_f[l], n + "mod", bias=b_ada[l][None, :])
        mod3 = mod[0:2].reshape(2, 3, D)
        h = _prenorm_fwd(xs, g_pre[l][None, :], mod3, nct, n + "prenorm")
        if l == 0:
            p, *got = _mm(h, wp[l], n + "proj", tb=True, out_dtype=BF16, tm=t // 2, rider=later_square)
            w_brs_f = [g.transpose(1, 0, 2, 3).reshape(DEPTH, D, D) for g in got]
        else:
            p = _mm(h, wp[l], n + "proj", tb=True, out_dtype=BF16, tm=t // 2)
        cv, ya = _conv_fwd(p, conv8[l], nct, n + "conv")
        qr, kr = _qk_prep_fwd(p, q_norm_g[l][None, :], k_norm_g[l][None, :], cos_t, sin_t, n + "qk_prep")
        att, lse, *got = _attn_fwd(qr, kr, p, nct, n + "attn", rider=later_in if l == 0 else None)
        if l == 0:
            wp[1], w_ada_f[1] = full_in(got[0]), full_ada(got[1])
        z, la = _decay_fwd(p, wd_pad[l], bd[l], n + "decay")
        of, stf, ob, stb = _gla_fwd(p, la, nct, n + "gla")
        yb, yc = _branch_fwd(att, of, ob, p, gla_norm_g[l][None, :], n + "branch")
        bra = _mm(ya, w_brs_f[0][l], n + "br_conv", out_dtype=BF16)
        brb = _mm(yb, w_brs_f[1][l], n + "br_attn", out_dtype=BF16)
        brc = _mm(yc, w_brs_f[2][l], n + "br_gla", out_dtype=BF16)
        mm_ = _merge_fwd(bra, brb, brc, p, b_gate[l][None, :], n + "merge")
        out = _mm(mm_, w_brs_f[3][l], n + "out")
        x_new = _post_fwd(xs, out, g_post[l][None, :], mod3, nct, n + "post")
        saved.append(dict(x=xs, mod3=mod3, h=h, p=p, cv=cv, ya=ya, qr=qr, kr=kr, att=att, lse=lse, z=z, la=la, of=of, ob=ob,
                          stf=stf, stb=stb, yb=yb, yc=yc, bra=bra, brb=brb, brc=brc, m=mm_, out=out))
        xs = x_new

    dx, sq = _loss_grad(xs, loss_target[0], nct, "loss")
    loss = lax.psum(0.5 * sq[0, 0] / D, ("x", "y", "c"))

    gw = {k: [None] * DEPTH for k in ("w_in", "br_conv", "br_attn", "br_gla", "out", "b_gate", "g_pre", "g_post",
                                      "conv_w", "qg", "kg", "wd", "bdec", "gla_g", "dmod")}
    dctx = []

    def in_slots(l):
        return _from_proj_layout(gw["w_in"][l]).reshape(NDEV, in_w, D)

    def br_slots(l):
        return [gw[k][l].reshape(NDEV, br_r, D) for k in ("br_conv", "br_attn", "br_gla", "out")]

    for l in reversed(range(DEPTH)):
        n = f"l{l}_b_"
        s = saved[l]
        p = s["p"]
        d_out, dgt, gw["g_post"][l] = _post_bwd(dx, s["out"], g_post[l][None, :], s["mod3"], nct, n + "post")
        dm = _mm(d_out, w_brs_f[3][l], n + "dm", tb=True, out_dtype=BF16)
        gw["out"][l] = _mm(s["m"], d_out, n + "dw_out", ta=True, out_dtype=BF16)
        dbra, dbrb, dbrc, dp, gw["b_gate"][l] = _merge_bwd(dm, s["bra"], s["brb"], s["brc"], p, b_gate[l][None, :], n + "merge")
        dya = _mm(dbra, w_brs_f[0][l], n + "dya", tb=True, out_dtype=BF16)
        dyb = _mm(dbrb, w_brs_f[1][l], n + "dyb", tb=True, out_dtype=BF16)
        dyc = _mm(dbrc, w_brs_f[2][l], n + "dyc", tb=True, out_dtype=BF16)
        gw["br_conv"][l] = _mm(s["ya"], dbra, n + "dw_conv", ta=True, out_dtype=BF16)
        gw["br_attn"][l] = _mm(s["yb"], dbrb, n + "dw_attn", ta=True, out_dtype=BF16)
        gw["br_gla"][l] = _mm(s["yc"], dbrc, n + "dw_gla", ta=True, out_dtype=BF16)
        dcv, dp = _conv_bwd_a(dya, p, s["cv"], dp, n + "conv_a")
        dp, gw["conv_w"][l] = _conv_bwd_b(dcv, p, conv8[l], nct, dp, n + "conv_b")
        datt, dgo, dp, gw["gla_g"][l] = _branch_bwd(dyb, dyc, s["att"], s["of"], s["ob"], p, gla_norm_g[l][None, :], dp, n + "branch")
        ex1 = _ExchangeRider([in_slots(DEPTH - 1)] + br_slots(DEPTH - 1)) if l == 0 else None
        dqr, dkr, dv, *got = _attn_bwd(s["qr"], s["kr"], p, s["att"], s["lse"], datt, nct, n + "attn", rider=ex1)
        if l == 0:
            recv_in1, recv_br1 = got[0], got[1:]
        dp, dk, gw["qg"][l], gw["kg"][l] = _qk_prep_bwd(dqr, dkr, p, q_norm_g[l][None, :], k_norm_g[l][None, :], cos_t, sin_t, dp, n + "qk_prep")
        gf, gb = _gla_bwd(p, s["la"], dgo, s["stf"], s["stb"], nct, n + "gla")
        dp, dr, gw["bdec"][l], gw["wd"][l] = _gla_merge_bwd(gf, gb, s["z"], p, wd_pad[l], dp, n + "gla_merge")
        dp = _dp_tail(dk, dv, dr, dp, n + "dp_tail")
        tk_in = t // 2 if t % 32 == 0 else None
        if l == 0:
            gw["w_in"][l], *recv_br0 = _mm(dp, s["h"], n + "dw_in", ta=True, out_dtype=BF16, tk=tk_in,
                                           rider=_ExchangeRider(br_slots(0)))
        else:
            gw["w_in"][l] = _mm(dp, s["h"], n + "dw_in", ta=True, out_dtype=BF16, tk=tk_in)
        if l == 0:
            core = lax.axis_index("c")
            halves = in_slots(0).reshape(NDEV // 2, 2, in_w, D)
            kept = lax.dynamic_index_in_dim(halves, core, axis=1, keepdims=False)
            sent = lax.dynamic_index_in_dim(halves, 1 - core, axis=1, keepdims=False)
            from_sibling, = _comm_alone(_SwapRider([sent]), n + "swap_dw_in")
            chip_sum = _pair_sum(kept, from_sibling, n + "chip_sum_dw_in")
            dh, recv_in0 = _mm(dp, wp[l], n + "dh", tk=NP // 4, rider=_ExchangeRider([chip_sum], chips_only=True))
        else:
            dh = _mm(dp, wp[l], n + "dh", tk=NP // 4)
        dx, dsh, dsc, gw["g_pre"][l] = _prenorm_bwd(dh, s["x"], dx, g_pre[l][None, :], s["mod3"], nct, n + "prenorm")
        dmod = jnp.stack([_row0(dsh), _row0(dsc), _row0(dgt)], axis=1).reshape(2, 3 * D)
        gw["dmod"][l] = dmod
        dmod8 = jnp.concatenate([dmod, jnp.zeros((6, 3 * D), F32)], axis=0)
        dctx.append(_mm(dmod8, w_ada_f[l], n + "dsilu", tb=True))
    grad_x = dx[n_ctx:][None]
    g_cctx = _cctx_grad(dctx[0], dctx[1], dsilu_cc)[0]

    def st2(name):
        return jnp.stack(gw[name])

    g_b_ada = jnp.stack([gw["dmod"][l][0] + gw["dmod"][l][1] for l in range(DEPTH)])
    g_bdf = jnp.stack([gw["bdec"][l][0, :GH * GDK] for l in range(DEPTH)])
    g_bdb = jnp.stack([gw["bdec"][l][0, GH * GDK:] for l in range(DEPTH)])
    g_wdf = jnp.stack([gw["wd"][l][0:GLA_RANK, :GH * GDK] for l in range(DEPTH)])
    g_wdb = jnp.stack([gw["wd"][l][GLA_RANK:2 * GLA_RANK, GH * GDK:] for l in range(DEPTH)])
    rep_grads = [g_cctx, g_b_ada, st2("g_pre")[:, 0], st2("g_post")[:, 0], st2("qg")[:, 0], st2("kg")[:, 0], g_bdf, g_bdb,
                 st2("gla_g")[:, 0], st2("b_gate")[:, 0]]
    rep_w = [c_ctx, b_ada, g_pre, g_post, q_norm_g, k_norm_g, b_decay_fwd, b_decay_bwd, gla_norm_g, b_gate]
    rep_m = [m_c_ctx, m_b_ada, m_g_pre, m_g_post, m_q_norm_g, m_k_norm_g, m_b_decay_fwd, m_b_decay_bwd, m_gla_norm_g, m_b_gate]
    rep_v = [v_c_ctx, v_b_ada, v_g_pre, v_g_post, v_q_norm_g, v_k_norm_g, v_b_decay_fwd, v_b_decay_bwd, v_gla_norm_g, v_b_gate]
    def two_d(a):
        return a.reshape(1, -1) if a.ndim == 1 else a

    def owner_slots(g):
        return g.reshape(DEPTH, g.shape[1], NDEV, g.shape[2] // NDEV).transpose(2, 0, 1, 3)

    n_rep = len(rep_grads)
    small = _comm_alone(_Riders([
        _GatherRider([two_d(g) for g in rep_grads] + [silu_cc[0:2], jnp.stack(gw["dmod"])]),
        _ExchangeRider([owner_slots(st2("conv_w")[:, 0:3]), owner_slots(g_wdf), owner_slots(g_wdb)])]),
        "exchange_small_grads")
    rep_src, (a_all, d_all), sh_src = small[:n_rep], small[n_rep:n_rep + 2], small[n_rep + 2:]
    sh_w = [conv_w, w_decay_fwd, w_decay_bwd]
    sh_m = [m_conv_w, m_w_decay_fwd, m_w_decay_bwd]
    sh_v = [v_conv_w, v_w_decay_fwd, v_w_decay_bwd]
    small_out = _adamw_small(
        [(g, two_d(w), two_d(m), two_d(v)) for g, w, m, v in zip(rep_src, rep_w, rep_m, rep_v)]
        + list(zip(sh_src, sh_w, sh_m, sh_v)), "adam_small")
    rep_g, rep_d, rep_nm, rep_nv = [[small_out[j][k].reshape(rep_w[j].shape) for j in range(n_rep)] for k in range(4)]
    sh_gr, sh_d, sh_nm, sh_nv = [[small_out[n_rep + j][k] for j in range(len(sh_w))] for k in range(4)]

    a_all = a_all.reshape(NDEV * 2, D)
    d_all = d_all.transpose(1, 0, 2, 3).reshape(DEPTH, NDEV * 2, 3 * D)
    g_ada = jnp.stack([_mm(a_all, lax.dynamic_slice_in_dim(d_all[l], dev * ada_w, ada_w, axis=1), f"dw_ada{l}",
                           ta=True, precise=True, tk=NDEV * 2) for l in range(DEPTH)])
    ada_g, ada_d, ada_nm, ada_nv = _adamw(g_ada[None], w_ada, m_w_ada, v_w_ada, "adam_ada")

    big_w = [w_br_conv, w_br_attn, w_br_gla, w_out]
    big_m = [m_w_br_conv, m_w_br_attn, m_w_br_gla, m_w_out]
    big_v = [v_w_br_conv, v_w_br_attn, v_w_br_gla, v_w_out]
    big_out = [_adamw(jnp.stack([recv_br0[j], recv_br1[j]], axis=1), big_w[j], big_m[j], big_v[j], f"adam_big{j}")
               for j in range(len(big_w))]
    in_out = [_adamw(r_[:, None], in_t(w_in, l)[None], in_t(m_w_in, l)[None], in_t(v_w_in, l)[None], f"adam_in{l}")
              for l, r_ in enumerate((recv_in0, recv_in1))]
    in_res = [jnp.stack([in_out[l][k][0] for l in range(DEPTH)], axis=1).transpose(1, 2, 0) for k in range(4)]
    big_g, big_d, big_nm, big_nv = [[in_res[k]] + [o[k] for o in big_out] for k in range(4)]

    def ordered(rep, ada, big, sh):
        c_ctx_, b_ada_, g_pre_, g_post_, qg_, kg_, bdf_, bdb_, glag_, bgate_ = rep
        w_in_, brc_, bra_, brg_, wout_ = big
        conv_, wdf_, wdb_ = sh
        return [c_ctx_, ada, b_ada_, g_pre_, g_post_, w_in_, conv_, qg_, kg_, wdf_, bdf_, wdb_, bdb_, glag_,
                brc_, bra_, brg_, bgate_, wout_]

    return (loss, grad_x,
            *ordered(rep_g, ada_g, big_g, sh_gr), *ordered(rep_d, ada_d, big_d, sh_d),
            *ordered(rep_nm, ada_nm, big_nm, sh_nm), *ordered(rep_nv, ada_nv, big_nv, sh_nv))
```

```python
import functools

import numpy as np
import jax
import jax.numpy as jnp
from jax import lax
from jax.experimental import pallas as pl
from jax.experimental.pallas import tpu as pltpu

F32, BF16 = jnp.float32, jnp.bfloat16
HIGHEST = lax.Precision.HIGHEST

D = 1024
DEPTH = 2
GRID_W = 64
NH, NKV, HD = 8, 2, 128
GROUP = NH // NKV
ROPE_THETA = 10000.0
ATTN_SCALE = HD ** -0.5
Q_FOLD = ATTN_SCALE * 1.4426950408889634
P_HALO = 16
GH, GDK, GDV = 4, 128, 256
GLA_RANK = 16
GLA_TAU = 16.0
CH = 64
GLA_SCALE = GDK ** -0.5
EPS = 1e-6
NDEV = 8
LANE = 128
TM = 256
BIG_ROWS = 544
ATTN_HEADS_PER_STEP = 4
ATTN_FWD_KEY_CHUNK = 2176
ATTN_BWD_KEY_CHUNK = 256
KEY_ALIGN = LANE

ADAM_LR, ADAM_B1, ADAM_B2, ADAM_EPS, ADAM_WD, ADAM_STEP = 0.001, 0.9, 0.999, 1e-08, 0.01, 10

_SEGS = (("a_b", 0, 1024), ("a_z", 3072, 1024), ("a_c", 1024, 1024), ("a_x", 2048, 1024),
         ("z_attn", 5632, 1024), ("zg", 8736, 1024), ("gv", 7680, 1024), ("gq", 6656, 512), ("gk", 7168, 512),
         ("q", 4096, 1024), ("mg", 9760, 3072), ("k", 5120, 256), ("v", 5376, 256), ("r", 8704, 32))
DP_BLOCKS = {"conv_a": ("a_b", 2048), "conv_b": ("a_c", 2048), "branch": ("z_attn", 2048), "gla": ("gv", 2048),
             "q": ("q", 1024), "merge": ("mg", 3072), "tail": ("k", 1024)}
IN_WIDTH = 12832
NP = 13312
OFF = {}
_o = 0
for _n, _s, _w in _SEGS:
    OFF[_n] = _o
    _o += _w
R_PAD = 128


def _cparams(ngrid, vmem_mb):
    return pltpu.CompilerParams(dimension_semantics=("arbitrary",) * ngrid, vmem_limit_bytes=vmem_mb << 20)


def _pick(n, cands):
    for c in cands:
        if n % c == 0:
            return c
    return n


def _sigmoid(x):
    return 1.0 / (1.0 + jnp.exp(-x))


ADAM_SRC_BYTES = 8 << 20
ADAM_ROW_BYTES = 1 << 20


def _all_gather(xs, name):
    return _comm_alone(_GatherRider(xs), name)


_HBM = pl.BlockSpec(memory_space=pl.ANY)


class _Rider:
    def __init__(self, xs, out_shapes, remote_copies=NDEV - 1):
        self.xs, self.n = list(xs), len(xs)
        self.out_shape = [jax.ShapeDtypeStruct(s, x.dtype) for s, x in zip(out_shapes, xs)]
        self.scratch = [pltpu.SemaphoreType.DMA((remote_copies * self.n,)),
                        pltpu.SemaphoreType.DMA((remote_copies * self.n,)), pltpu.SemaphoreType.DMA((self.n,))]


class _GatherRider(_Rider):
    def __init__(self, xs):
        super().__init__(xs, [(NDEV,) + x.shape for x in xs])

    def _parts(self, x_refs, out_refs, sems):
        n = self.n
        send_sems, recv_sems, local_sems = sems
        mx, my, mc = lax.axis_index("x"), lax.axis_index("y"), lax.axis_index("c")
        me, sibling = (mx, my, mc), (mx, my, 1 - mc)
        chips = [(1 - mx, my), (mx, 1 - my), (1 - mx, 1 - my)]

        def slot(a, px, py, pc):
            return out_refs[a].at[4 * px + 2 * py + pc]

        def copy(k, a, block, to, own=False):
            return pltpu.make_async_remote_copy(
                src_ref=x_refs[a] if own else slot(a, *block), dst_ref=slot(a, *block),
                send_sem=send_sems.at[k * n + a], recv_sem=recv_sems.at[k * n + a],
                device_id=to, device_id_type=pl.DeviceIdType.MESH)

        mine = [pltpu.make_async_copy(x_refs[a], slot(a, *me), local_sems.at[a]) for a in range(n)]
        first = [copy(0, a, me, sibling, own=True) for a in range(n)]
        first += [copy(1 + j, a, me, (*chip, mc), own=True) for a in range(n) for j, chip in enumerate(chips)]
        landed = [copy(1 + j, a, (*chip, mc), me) for a in range(n) for j, chip in enumerate(chips)]
        passed = [copy(4 + j, a, (*chip, mc), sibling) for a in range(n) for j, chip in enumerate(chips)]
        from_sibling = [copy(0, a, sibling, me) for a in range(n)]
        from_sibling += [copy(4 + j, a, (*chip, 1 - mc), me) for a in range(n) for j, chip in enumerate(chips)]
        return mine, first, landed, passed, from_sibling

    def start(self, x_refs, out_refs, sems):
        mine, first, _, _, _ = self._parts(x_refs, out_refs, sems)
        for cp in mine + first:
            cp.start()

    def middle(self, x_refs, out_refs, sems):
        _, _, landed, passed, _ = self._parts(x_refs, out_refs, sems)
        for got, fwd in zip(landed, passed):
            got.wait_recv()
            fwd.start()

    def finish(self, x_refs, out_refs, sems):
        mine, first, _, passed, from_sibling = self._parts(x_refs, out_refs, sems)
        for cp in from_sibling:
            cp.wait_recv()
        for cp in first + passed:
            cp.wait_send()
        for cp in mine:
            cp.wait()


class _ExchangeRider(_Rider):
    def __init__(self, xs, chips_only=False):
        self.chips_only = chips_only
        super().__init__(xs, [x.shape for x in xs], 3 if chips_only else NDEV - 1)

    def _parts(self, x_refs, out_refs, sems):
        n = self.n
        send_sems, recv_sems, local_sems = sems
        mx, my, mc = lax.axis_index("x"), lax.axis_index("y"), lax.axis_index("c")
        me = 2 * mx + my if self.chips_only else 4 * mx + 2 * my + mc
        mine = [pltpu.make_async_copy(x_refs[a].at[me], out_refs[a].at[me], local_sems.at[a]) for a in range(n)]
        copies = []
        for a in range(n):
            for rel in range(1, 4 if self.chips_only else NDEV):
                bits = rel << 1 if self.chips_only else rel
                px = (1 - mx) if bits & 4 else mx
                py = (1 - my) if bits & 2 else my
                pc = (1 - mc) if bits & 1 else mc
                peer = 2 * px + py if self.chips_only else 4 * px + 2 * py + pc
                k = (rel - 1) * n + a
                copies.append(pltpu.make_async_remote_copy(
                    src_ref=x_refs[a].at[peer], dst_ref=out_refs[a].at[me],
                    send_sem=send_sems.at[k], recv_sem=recv_sems.at[k],
                    device_id=(px, py, pc), device_id_type=pl.DeviceIdType.MESH))
        return mine, copies

    def start(self, x_refs, out_refs, sems):
        mine, copies = self._parts(x_refs, out_refs, sems)
        for cp in mine + copies:
            cp.start()

    def middle(self, x_refs, out_refs, sems):
        pass

    def finish(self, x_refs, out_refs, sems):
        mine, copies = self._parts(x_refs, out_refs, sems)
        for cp in copies:
            cp.wait_recv()
        for cp in copies:
            cp.wait_send()
        for cp in mine:
            cp.wait()


class _SwapRider(_Rider):
    def __init__(self, xs):
        super().__init__(xs, [x.shape for x in xs], 1)

    def _parts(self, x_refs, out_refs, sems):
        send_sems, recv_sems, _ = sems
        sibling = (lax.axis_index("x"), lax.axis_index("y"), 1 - lax.axis_index("c"))
        return [pltpu.make_async_remote_copy(
            src_ref=x_refs[a], dst_ref=out_refs[a], send_sem=send_sems.at[a], recv_sem=recv_sems.at[a],
            device_id=sibling, device_id_type=pl.DeviceIdType.MESH) for a in range(self.n)]

    def start(self, x_refs, out_refs, sems):
        for cp in self._parts(x_refs, out_refs, sems):
            cp.start()

    def middle(self, x_refs, out_refs, sems):
        pass

    def finish(self, x_refs, out_refs, sems):
        copies = self._parts(x_refs, out_refs, sems)
        for cp in copies:
            cp.wait_recv()
        for cp in copies:
            cp.wait_send()


class _Riders:
    def __init__(self, riders):
        self.riders = list(riders)
        self.xs = [x for r in self.riders for x in r.xs]
        self.n = len(self.xs)
        self.out_shape = [s for r in self.riders for s in r.out_shape]
        self.scratch = [s for r in self.riders for s in r.scratch]

    def _each(self, method, x_refs, out_refs, sems):
        a = b = 0
        for r in self.riders:
            getattr(r, method)(x_refs[a:a + r.n], out_refs[a:a + r.n], sems[b:b + len(r.scratch)])
            a, b = a + r.n, b + len(r.scratch)

    def start(self, *refs):
        self._each("start", *refs)

    def middle(self, *refs):
        self._each("middle", *refs)

    def finish(self, *refs):
        self._each("finish", *refs)


def _comm_alone(rider, name):
    n = rider.n

    def body(*refs):
        x_refs, out_refs, sems = refs[:n], refs[n:2 * n], refs[2 * n:]
        rider.start(x_refs, out_refs, sems)
        rider.middle(x_refs, out_refs, sems)
        rider.finish(x_refs, out_refs, sems)

    return pl.pallas_call(
        body, name=name, out_shape=tuple(rider.out_shape), in_specs=[_HBM] * n, out_specs=(_HBM,) * n,
        scratch_shapes=rider.scratch,
    )(*rider.xs)


def _with_rider(body, nin, nout, rider, first, mid, last):
    if rider is None:
        return body
    n = rider.n

    def wrapped(*refs):
        ins, x_refs = refs[:nin], refs[nin:nin + n]
        outs, out_refs = refs[nin + n:nin + n + nout], refs[nin + n + nout:nin + 2 * n + nout]
        ns = len(rider.scratch)
        scratch, sems = refs[nin + 2 * n + nout:len(refs) - ns], refs[len(refs) - ns:]

        @pl.when(first())
        def _():
            rider.start(x_refs, out_refs, sems)

        body(*ins, *outs, *scratch)

        @pl.when(mid())
        def _():
            rider.middle(x_refs, out_refs, sems)

        @pl.when(last())
        def _():
            rider.finish(x_refs, out_refs, sems)

    return wrapped


def _mm(a, b, name, ta=False, tb=False, out_dtype=F32, bias=None, precise=False, tm=None, tn=None, tk=None, rider=None):
    m, k = (a.shape[1], a.shape[0]) if ta else a.shape
    n = b.shape[0] if tb else b.shape[1]
    assert k == (b.shape[1] if tb else b.shape[0])
    tm = tm or _pick(m, (1088, 1024, 512, 256, 128))
    tn = tn or _pick(n, (1024, 512, 384, 256, 128))
    tk = tk or _pick(k, (1024, 1088, 512, 256, 128))
    nk = k // tk
    dn = (((0 if ta else 1,), (1 if tb else 0,)), ((), ()))

    def body(*refs):
        if bias is None:
            a_ref, b_ref, o_ref = refs[:3]
            bias_ref = None
        else:
            a_ref, b_ref, bias_ref, o_ref = refs[:4]
        x, y = a_ref[...], b_ref[...]
        if precise:
            p = lax.dot_general(x.astype(F32), y.astype(F32), dn, preferred_element_type=F32, precision=HIGHEST)
        else:
            p = lax.dot_general(x.astype(BF16), y.astype(BF16), dn, preferred_element_type=F32)

        def finish(acc):
            if bias_ref is not None:
                acc = acc + bias_ref[...]
            o_ref[...] = acc.astype(out_dtype)

        if nk == 1:
            finish(p)
        else:
            acc_ref = refs[-1]
            kk = pl.program_id(2)

            @pl.when(kk == 0)
            def _():
                acc_ref[...] = p

            @pl.when(kk > 0)
            def _():
                acc_ref[...] += p

            @pl.when(kk == nk - 1)
            def _():
                finish(acc_ref[...])

    a_spec = pl.BlockSpec((tk, tm), lambda i, j, kk: (kk, i)) if ta else pl.BlockSpec((tm, tk), lambda i, j, kk: (i, kk))
    b_spec = pl.BlockSpec((tn, tk), lambda i, j, kk: (j, kk)) if tb else pl.BlockSpec((tk, tn), lambda i, j, kk: (kk, j))
    in_specs = [a_spec, b_spec]
    args = [a, b]
    if bias is not None:
        in_specs.append(pl.BlockSpec((1, tn), lambda i, j, kk: (0, j)))
        args.append(bias)
    grid = (m // tm, n // tn, nk)
    out_spec = pl.BlockSpec((tm, tn), lambda i, j, kk: (i, j))
    scratch = [pltpu.VMEM((tm, tn), F32)] if nk > 1 else []
    if rider is None:
        return pl.pallas_call(
            body, name=name, grid=grid, in_specs=in_specs, out_specs=out_spec,
            out_shape=jax.ShapeDtypeStruct((m, n), out_dtype), scratch_shapes=scratch, compiler_params=_cparams(3, 56),
        )(*args)

    def at(step):
        return lambda: ((pl.program_id(0) == step[0]) & (pl.program_id(1) == step[1]) & (pl.program_id(2) == step[2]))

    end = tuple(g - 1 for g in grid)
    step = grid[0] * grid[1] * grid[2] * 7 // 8
    late = (step // (grid[1] * grid[2]), step // grid[2] % grid[1], step % grid[2])
    return pl.pallas_call(
        _with_rider(body, len(args), 1, rider, at((0, 0, 0)), at(late), at(end)),
        name=name, grid=grid, in_specs=in_specs + [_HBM] * rider.n, out_specs=(out_spec,) + (_HBM,) * rider.n,
        out_shape=(jax.ShapeDtypeStruct((m, n), out_dtype),) + tuple(rider.out_shape),
        scratch_shapes=scratch + rider.scratch, compiler_params=_cparams(3, 56),
    )(*args, *rider.xs)


def _ada_in(cc):
    def body(c_ref, s_ref, d_ref):
        x = c_ref[...]
        sg = _sigmoid(x)
        s_ref[...] = x * sg
        d_ref[...] = sg * (1.0 + x * (1.0 - sg))

    return pl.pallas_call(body, name="ada_in", out_shape=(jax.ShapeDtypeStruct(cc.shape, F32),) * 2)(cc)


def _cctx_grad(t0, t1, dsilu):
    def body(a_ref, b_ref, d_ref, o_ref):
        o_ref[...] = (a_ref[...] + b_ref[...]) * d_ref[...]

    return pl.pallas_call(body, name="cctx_grad", out_shape=jax.ShapeDtypeStruct(t0.shape, F32))(t0, t1, dsilu)


def _seg_spec(nct, rows=3):
    return pl.BlockSpec((None, rows, D), lambda i: (jnp.where(i >= nct, 1, 0), 0, 0))


def _prenorm_fwd(x, g_pre, mod3, nct, name):
    t = x.shape[0]

    def body(x_ref, g_ref, mod_ref, h_ref):
        xv = x_ref[...]
        r = lax.rsqrt(jnp.mean(xv * xv, axis=-1, keepdims=True) + EPS)
        y = xv * r * g_ref[...]
        h_ref[...] = (y * (1.0 + mod_ref[1:2, :]) + mod_ref[0:1, :]).astype(BF16)

    return pl.pallas_call(
        body, name=name, grid=(t // TM,),
        in_specs=[pl.BlockSpec((TM, D), lambda i: (i, 0)), pl.BlockSpec((1, D), lambda i: (0, 0)), _seg_spec(nct)],
        out_specs=pl.BlockSpec((TM, D), lambda i: (i, 0)),
        out_shape=jax.ShapeDtypeStruct((t, D), BF16), compiler_params=_cparams(1, 32),
    )(x, g_pre, mod3)


def _prenorm_bwd(dh, x, dxo, g_pre, mod3, nct, name):
    t = x.shape[0]

    def body(dh_ref, x_ref, dxo_ref, g_ref, mod_ref, dx_ref, dsh_ref, dsc_ref, dg_ref):
        i = pl.program_id(0)
        xv, dhv, g = x_ref[...], dh_ref[...], g_ref[...]
        r = lax.rsqrt(jnp.mean(xv * xv, axis=-1, keepdims=True) + EPS)
        xh = xv * r
        dy = dhv * (1.0 + mod_ref[1:2, :])
        dxh = dy * g
        dx_ref[...] = dxo_ref[...] + r * (dxh - xh * jnp.mean(dxh * xh, axis=-1, keepdims=True))

        @pl.when((i == 0) | (i == nct))
        def _():
            dsh_ref[...] = jnp.zeros_like(dsh_ref)
            dsc_ref[...] = jnp.zeros_like(dsc_ref)

        @pl.when(i == 0)
        def _():
            dg_ref[...] = jnp.zeros_like(dg_ref)

        dsh_ref[...] += jnp.sum(dhv, axis=0, keepdims=True)
        dsc_ref[...] += jnp.sum(dhv * (xh * g), axis=0, keepdims=True)
        dg_ref[...] += jnp.sum(dy * xh, axis=0, keepdims=True)

    row = pl.BlockSpec((TM, D), lambda i: (i, 0))
    seg8 = pl.BlockSpec((None, 8, D), lambda i: (jnp.where(i >= nct, 1, 0), 0, 0))
    return pl.pallas_call(
        body, name=name, grid=(t // TM,),
        in_specs=[row, row, row, pl.BlockSpec((1, D), lambda i: (0, 0)), _seg_spec(nct)],
        out_specs=(row, seg8, seg8, pl.BlockSpec((8, D), lambda i: (0, 0))),
        out_shape=(jax.ShapeDtypeStruct((t, D), F32), jax.ShapeDtypeStruct((2, 8, D), F32),
                   jax.ShapeDtypeStruct((2, 8, D), F32), jax.ShapeDtypeStruct((8, D), F32)),
        compiler_params=_cparams(1, 32),
    )(dh, x, dxo, g_pre, mod3)


def _post_fwd(x, out, g_post, mod3, nct, name):
    t = x.shape[0]

    def body(x_ref, o_ref, g_ref, mod_ref, y_ref):
        ov = o_ref[...]
        r = lax.rsqrt(jnp.mean(ov * ov, axis=-1, keepdims=True) + EPS)
        y_ref[...] = x_ref[...] + mod_ref[2:3, :] * (ov * r * g_ref[...])

    row = pl.BlockSpec((TM, D), lambda i: (i, 0))
    return pl.pallas_call(
        body, name=name, grid=(t // TM,),
        in_specs=[row, row, pl.BlockSpec((1, D), lambda i: (0, 0)), _seg_spec(nct)],
        out_specs=row, out_shape=jax.ShapeDtypeStruct((t, D), F32), compiler_params=_cparams(1, 32),
    )(x, out, g_post, mod3)


def _post_bwd(dxo, out, g_post, mod3, nct, name):
    t = out.shape[0]

    def body(dx_ref, o_ref, g_ref, mod_ref, do_ref, dgt_ref, dg_ref):
        i = pl.program_id(0)
        ov, dxv, g = o_ref[...], dx_ref[...], g_ref[...]
        r = lax.rsqrt(jnp.mean(ov * ov, axis=-1, keepdims=True) + EPS)
        nh = ov * r
        dn = dxv * mod_ref[2:3, :]
        dnh = dn * g
        do_ref[...] = (r * (dnh - nh * jnp.mean(dnh * nh, axis=-1, keepdims=True))).astype(BF16)

        @pl.when((i == 0) | (i == nct))
        def _():
            dgt_ref[...] = jnp.zeros_like(dgt_ref)

        @pl.when(i == 0)
        def _():
            dg_ref[...] = jnp.zeros_like(dg_ref)

        dgt_ref[...] += jnp.sum(dxv * (nh * g), axis=0, keepdims=True)
        dg_ref[...] += jnp.sum(dn * nh, axis=0, keepdims=True)

    row = pl.BlockSpec((TM, D), lambda i: (i, 0))
    seg8 = pl.BlockSpec((None, 8, D), lambda i: (jnp.where(i >= nct, 1, 0), 0, 0))
    return pl.pallas_call(
        body, name=name, grid=(t // TM,),
        in_specs=[row, row, pl.BlockSpec((1, D), lambda i: (0, 0)), _seg_spec(nct)],
        out_specs=(row, seg8, pl.BlockSpec((8, D), lambda i: (0, 0))),
        out_shape=(jax.ShapeDtypeStruct((t, D), BF16), jax.ShapeDtypeStruct((2, 8, D), F32),
                   jax.ShapeDtypeStruct((8, D), F32)),
        compiler_params=_cparams(1, 32),
    )(dxo, out, g_post, mod3)


def _loss_grad(y, target, nct, name):
    t = y.shape[0]

    def body(y_ref, t_ref, dy_ref, l_ref):
        i = pl.program_id(0)

        @pl.when(i == 0)
        def _():
            l_ref[...] = jnp.zeros_like(l_ref)

        @pl.when(i < nct)
        def _():
            dy_ref[...] = jnp.zeros_like(dy_ref)

        @pl.when(i >= nct)
        def _():
            err = y_ref[...] - t_ref[...]
            dy_ref[...] = err / D
            l_ref[...] += jnp.sum(jnp.sum(err * err, axis=1, keepdims=True), axis=0, keepdims=True)

    row = pl.BlockSpec((TM, D), lambda i: (i, 0))
    return pl.pallas_call(
        body, name=name, grid=(t // TM,),
        in_specs=[row, pl.BlockSpec((TM, D), lambda i: (jnp.maximum(i - nct, 0), 0))],
        out_specs=(row, pl.BlockSpec((8, LANE), lambda i: (0, 0))),
        out_shape=(jax.ShapeDtypeStruct((t, D), F32), jax.ShapeDtypeStruct((8, LANE), F32)),
        compiler_params=_cparams(1, 32),
    )(y, target)


def _pcol(name, width, rows=TM):
    assert OFF[name] % width == 0
    blk = OFF[name] // width
    return pl.BlockSpec((rows, width), lambda i: (i, blk))


def _big_rows(t):
    return max(r for r in range(16, BIG_ROWS + 1, 16) if t % r == 0)


def _shift_rows(u, prev_row, next_row):
    n = u.shape[0]
    row = lax.broadcasted_iota(jnp.int32, u.shape, 0)
    prev = jnp.where(row == 0, prev_row, pltpu.roll(u, 1, 0))
    nxt = jnp.where(row == n - 1, next_row, pltpu.roll(u, n - 1, 0))
    return prev, nxt


def _halo_specs(width, nt, blk=0, rows=8):
    per = TM // rows
    prev = pl.BlockSpec((rows, width), lambda i: (jnp.maximum(i * per - 1, 0), blk))
    nxt = pl.BlockSpec((rows, width), lambda i: (jnp.minimum((i + 1) * per, nt * per - 1), blk))
    return prev, nxt


def _conv_fwd(p, conv_w8, nct, name):
    t = p.shape[0]
    nt = t // TM

    def body(ab_ref, ac_ref, ax_ref, az_ref, acp_ref, axp_ref, acn_ref, axn_ref, w_ref, cv_ref, ya_ref):
        i = pl.program_id(0)
        def f(ref, rows=slice(None)):
            return ref[rows, :].astype(F32)

        u = f(ac_ref) * f(ax_ref)
        mp = jnp.where((i == 0) | (i == nct), 0.0, 1.0)
        mn = jnp.where((i == nct - 1) | (i == nt - 1), 0.0, 1.0)
        last, first = slice(P_HALO - 1, P_HALO), slice(0, 1)
        prev, nxt = _shift_rows(u, f(acp_ref, last) * f(axp_ref, last) * mp, f(acn_ref, first) * f(axn_ref, first) * mn)
        cv = w_ref[0:1, :] * prev + w_ref[1:2, :] * u + w_ref[2:3, :] * nxt
        az = f(az_ref)
        cv_ref[...] = cv.astype(BF16)
        ya_ref[...] = (f(ab_ref) * cv * (az * _sigmoid(az))).astype(BF16)

    acp, acn = _halo_specs(D, nt, OFF["a_c"] // D, P_HALO)
    axp, axn = _halo_specs(D, nt, OFF["a_x"] // D, P_HALO)
    row = pl.BlockSpec((TM, D), lambda i: (i, 0))
    return pl.pallas_call(
        body, name=name, grid=(nt,),
        in_specs=[_pcol("a_b", D), _pcol("a_c", D), _pcol("a_x", D), _pcol("a_z", D), acp, axp, acn, axn,
                  pl.BlockSpec((8, D), lambda i: (0, 0))],
        out_specs=(row, row),
        out_shape=(jax.ShapeDtypeStruct((t, D), BF16), jax.ShapeDtypeStruct((t, D), BF16)),
        compiler_params=_cparams(1, 40),
    )(p, p, p, p, p, p, p, p, conv_w8)


def _dp_spec(key, rows=TM):
    seg, width = DP_BLOCKS[key]
    assert OFF[seg] % width == 0
    blk = OFF[seg] // width
    return pl.BlockSpec((rows, width), lambda i: (i, blk))


def _conv_bwd_a(dya, p, cv, dp, name):
    t = p.shape[0]

    def body(dy_ref, ab_ref, az_ref, cv_ref, _, dcv_ref, dp_ref):
        dy, ab = dy_ref[...].astype(F32), ab_ref[...].astype(F32)
        az, c = az_ref[...].astype(F32), cv_ref[...].astype(F32)
        sg = _sigmoid(az)
        sz = az * sg
        dcv_ref[...] = dy * ab * sz
        dp_ref[:, 0:D] = (dy * c * sz).astype(BF16)
        dp_ref[:, D:2 * D] = (dy * ab * c * (sg * (1.0 + az * (1.0 - sg)))).astype(BF16)

    rt = _big_rows(t)
    row = pl.BlockSpec((rt, D), lambda i: (i, 0))
    return pl.pallas_call(
        body, name=name, grid=(t // rt,),
        in_specs=[row, _pcol("a_b", D, rt), _pcol("a_z", D, rt), row, _HBM], out_specs=(row, _dp_spec("conv_a", rt)),
        out_shape=(jax.ShapeDtypeStruct((t, D), F32), jax.ShapeDtypeStruct(dp.shape, dp.dtype)),
        input_output_aliases={4: 1}, compiler_params=_cparams(1, 40),
    )(dya, p, p, cv, dp)


def _conv_bwd_b(dcv, p, conv_w8, nct, dp, name):
    t = p.shape[0]
    nt = t // TM

    def body(dcv_ref, hp_ref, hn_ref, ac_ref, ax_ref, w_ref, _, dp_ref, dw_ref):
        i = pl.program_id(0)
        d, ac, ax = dcv_ref[...], ac_ref[...].astype(F32), ax_ref[...].astype(F32)
        u = ac * ax
        mp = jnp.where((i == 0) | (i == nct), 0.0, 1.0)
        mn = jnp.where((i == nct - 1) | (i == nt - 1), 0.0, 1.0)
        dprev, dnxt = _shift_rows(d, hp_ref[7:8, :] * mp, hn_ref[0:1, :] * mn)
        du = w_ref[0:1, :] * dnxt + w_ref[1:2, :] * d + w_ref[2:3, :] * dprev
        dp_ref[:, 0:D] = (du * ax).astype(BF16)
        dp_ref[:, D:2 * D] = (du * ac).astype(BF16)

        @pl.when(i == 0)
        def _():
            dw_ref[...] = jnp.zeros_like(dw_ref)

        dw0 = jnp.sum(u * dnxt, axis=0, keepdims=True)
        dw1 = jnp.sum(u * d, axis=0, keepdims=True)
        dw2 = jnp.sum(u * dprev, axis=0, keepdims=True)
        r8 = lax.broadcasted_iota(jnp.int32, (8, D), 0)
        dw_ref[...] += jnp.where(r8 == 0, dw0, jnp.where(r8 == 1, dw1, jnp.where(r8 == 2, dw2, 0.0)))

    hp, hn = _halo_specs(D, nt)
    row = pl.BlockSpec((TM, D), lambda i: (i, 0))
    return pl.pallas_call(
        body, name=name, grid=(nt,),
        in_specs=[row, hp, hn, _pcol("a_c", D), _pcol("a_x", D), pl.BlockSpec((8, D), lambda i: (0, 0)), _HBM],
        out_specs=(_dp_spec("conv_b"), pl.BlockSpec((8, D), lambda i: (0, 0))),
        out_shape=(jax.ShapeDtypeStruct(dp.shape, dp.dtype), jax.ShapeDtypeStruct((8, D), F32)),
        input_output_aliases={6: 0}, compiler_params=_cparams(1, 40),
    )(dcv, dcv, dcv, p, p, conv_w8, dp)


def _rot_half(x):
    lane = lax.broadcasted_iota(jnp.int32, x.shape, 1)
    return jnp.where((lane % 64) < 32, pltpu.roll(x, 96, 1), pltpu.roll(x, 32, 1))


def _qk_prep_fwd(p, qg, kg, cos_t, sin_t, name):
    t = p.shape[0]

    def body(q_ref, k_ref, qg_ref, kg_ref, c_ref, s_ref, qo_ref, ko_ref):
        c, s = c_ref[...], s_ref[...]

        def one(xv, g, scale):
            y = xv * lax.rsqrt(jnp.mean(xv * xv, axis=-1, keepdims=True) + EPS) * g
            return ((y * c + _rot_half(y) * s) * scale).astype(BF16)

        for h in range(NH):
            qo_ref[:, h * HD:(h + 1) * HD] = one(q_ref[:, h * HD:(h + 1) * HD].astype(F32), qg_ref[...], Q_FOLD)
        for h in range(NKV):
            ko_ref[:, h * HD:(h + 1) * HD] = one(k_ref[:, h * HD:(h + 1) * HD].astype(F32), kg_ref[...], 1.0)

    vec = pl.BlockSpec((1, HD), lambda i: (0, 0))
    tab = pl.BlockSpec((TM, HD), lambda i: (i, 0))
    return pl.pallas_call(
        body, name=name, grid=(t // TM,),
        in_specs=[_pcol("q", NH * HD), _pcol("k", NKV * HD), vec, vec, tab, tab],
        out_specs=(pl.BlockSpec((TM, NH * HD), lambda i: (i, 0)), pl.BlockSpec((TM, NKV * HD), lambda i: (i, 0))),
        out_shape=(jax.ShapeDtypeStruct((t, NH * HD), BF16), jax.ShapeDtypeStruct((t, NKV * HD), BF16)),
        compiler_params=_cparams(1, 32),
    )(p, p, qg, kg, cos_t, sin_t)


def _qk_prep_bwd(dqr, dkr, p, qg, kg, cos_t, sin_t, dp, name):
    t = p.shape[0]

    def body(dq_ref, dk_ref, q_ref, k_ref, qg_ref, kg_ref, c_ref, s_ref, _, dqo_ref, dko_ref, dqg_ref, dkg_ref):
        i = pl.program_id(0)
        c, s = c_ref[...], s_ref[...]

        @pl.when(i == 0)
        def _():
            dqg_ref[...] = jnp.zeros_like(dqg_ref)
            dkg_ref[...] = jnp.zeros_like(dkg_ref)

        def one(dyr, xv, g):
            dy = dyr * c + _rot_half(dyr * s)
            r = lax.rsqrt(jnp.mean(xv * xv, axis=-1, keepdims=True) + EPS)
            xh = xv * r
            dxh = dy * g
            dx = r * (dxh - xh * jnp.mean(dxh * xh, axis=-1, keepdims=True))
            return dx.astype(BF16), jnp.sum(dy * xh, axis=0, keepdims=True)

        for h in range(NH):
            sl = slice(h * HD, (h + 1) * HD)
            dx, dg = one(dq_ref[:, sl] * ATTN_SCALE, q_ref[:, sl].astype(F32), qg_ref[...])
            dqo_ref[:, sl] = dx
            dqg_ref[...] += dg
        for h in range(NKV):
            sl = slice(h * HD, (h + 1) * HD)
            dx, dg = one(dk_ref[:, sl] * (ATTN_SCALE / Q_FOLD), k_ref[:, sl].astype(F32), kg_ref[...])
            dko_ref[:, sl] = dx
            dkg_ref[...] += dg

    vec = pl.BlockSpec((1, HD), lambda i: (0, 0))
    tab = pl.BlockSpec((TM, HD), lambda i: (i, 0))
    acc = pl.BlockSpec((8, HD), lambda i: (0, 0))
    qrow = pl.BlockSpec((TM, NH * HD), lambda i: (i, 0))
    krow = pl.BlockSpec((TM, NKV * HD), lambda i: (i, 0))
    return pl.pallas_call(
        body, name=name, grid=(t // TM,),
        in_specs=[qrow, krow, _pcol("q", NH * HD), _pcol("k", NKV * HD), vec, vec, tab, tab, _HBM],
        out_specs=(_dp_spec("q"), krow, acc, acc),
        out_shape=(jax.ShapeDtypeStruct(dp.shape, dp.dtype), jax.ShapeDtypeStruct((t, NKV * HD), BF16),
                   jax.ShapeDtypeStruct((8, HD), F32), jax.ShapeDtypeStruct((8, HD), F32)),
        input_output_aliases={8: 0}, compiler_params=_cparams(1, 32),
    )(dqr, dkr, p, p, qg, kg, cos_t, sin_t, dp)


def _key_chunks(n, limit):
    c = max(c for c in range(KEY_ALIGN, min(n, limit) + 1, KEY_ALIGN) if n % c == 0)
    return [(lo, lo + c) for lo in range(0, n, c)]


def _attn_fwd(qr, kr, p, nct, name, rider=None):
    t = qr.shape[0]
    nt = t // TM
    ctx = nct * TM
    vblk = OFF["v"] // HD
    hps = ATTN_HEADS_PER_STEP
    nhp, per_kv = NH // hps, GROUP // hps

    def body(q_ref, k_ref, v_ref, o_ref, lse_ref):
        def tile(nkeys):
            sls = [slice(j * HD, (j + 1) * HD) for j in range(hps)]
            qs = [q_ref[:, sl] for sl in sls]
            m = l = acc = None
            for lo, hi in _key_chunks(nkeys, ATTN_FWD_KEY_CHUNK):
                k, vb = k_ref[lo:hi, :], v_ref[lo:hi, :].astype(BF16)
                ss = [lax.dot_general(q, k, _NT, preferred_element_type=F32) for q in qs]
                mcs = [jnp.max(s, axis=-1, keepdims=True) for s in ss]
                m_new = mcs if m is None else [jnp.maximum(a, b) for a, b in zip(m, mcs)]
                es = [jnp.exp2(s - mn) for s, mn in zip(ss, m_new)]
                lcs = [jnp.sum(e, axis=-1, keepdims=True) for e in es]
                pvs = [jnp.dot(e.astype(BF16), vb, preferred_element_type=F32) for e in es]
                if m is None:
                    l, acc = lcs, pvs
                else:
                    alphas = [jnp.exp2(a - b) for a, b in zip(m, m_new)]
                    l = [x * al + y for x, al, y in zip(l, alphas, lcs)]
                    acc = [x * al + y for x, al, y in zip(acc, alphas, pvs)]
                m = m_new
            for j, sl in enumerate(sls):
                o_ref[:, sl] = (acc[j] / l[j]).astype(BF16)
                lse_ref[:, j:j + 1] = m[j] + jnp.log2(l[j])

        pl.when(pl.program_id(1) < nct)(lambda: tile(ctx))
        pl.when(pl.program_id(1) >= nct)(lambda: tile(t))

    def at(h, i):
        return lambda: (pl.program_id(0) == h) & (pl.program_id(1) == i)

    rn = 0 if rider is None else rider.n
    qspec = pl.BlockSpec((TM, hps * HD), lambda h, i: (i, h))
    return pl.pallas_call(
        _with_rider(body, 3, 2, rider, at(0, 0), at(*divmod(nhp * nt * 7 // 8, nt)), at(nhp - 1, nt - 1)),
        name=name, grid=(nhp, nt),
        in_specs=[qspec, pl.BlockSpec((t, HD), lambda h, i: (0, h // per_kv)),
                  pl.BlockSpec((t, HD), lambda h, i: (0, vblk + h // per_kv))] + [_HBM] * rn,
        out_specs=(qspec, pl.BlockSpec((None, TM, hps), lambda h, i: (h, i, 0))) + (_HBM,) * rn,
        out_shape=(jax.ShapeDtypeStruct((t, NH * HD), BF16), jax.ShapeDtypeStruct((nhp, t, hps), F32))
        + (() if rider is None else tuple(rider.out_shape)),
        scratch_shapes=[] if rider is None else rider.scratch,
        compiler_params=_cparams(2, 48),
    )(qr, kr, p, *(() if rider is None else rider.xs))


def _attn_bwd(qr, kr, p, o, lse, do, nct, name, rider=None):
    t = qr.shape[0]
    nt = t // TM
    ctx = nct * TM
    vblk = OFF["v"] // HD
    hps = ATTN_HEADS_PER_STEP

    def body(q_ref, k_ref, v_ref, o_ref, lse_ref, do_ref, dq_ref, dk_ref, dv_ref):
        g, i = pl.program_id(1), pl.program_id(2)

        @pl.when((g == 0) & (i == 0))
        def _():
            dk_ref[...] = jnp.zeros_like(dk_ref)
            dv_ref[...] = jnp.zeros_like(dv_ref)

        def tile(nkeys):
            heads = []
            for j in range(hps):
                sl = slice(j * HD, (j + 1) * HD)
                dob = do_ref[:, sl]
                drow = jnp.sum(dob.astype(F32) * o_ref[:, sl].astype(F32), axis=-1, keepdims=True)
                heads.append((sl, q_ref[:, sl], dob, drow, lse_ref[:, j:j + 1]))
            dq = [None] * hps
            for lo, hi in _key_chunks(nkeys, ATTN_BWD_KEY_CHUNK):
                k = k_ref[lo:hi, :]
                vb = v_ref[lo:hi, :].astype(BF16)
                ss = [lax.dot_general(q, k, _NT, preferred_element_type=F32) for _, q, _, _, _ in heads]
                dps = [lax.dot_general(dob, vb, _NT, preferred_element_type=F32) for _, _, dob, _, _ in heads]
                prs = [jnp.exp2(s - h[4]) for s, h in zip(ss, heads)]
                dss = [(pr * (dp - h[3])).astype(BF16) for pr, dp, h in zip(prs, dps, heads)]
                pbs = [pr.astype(BF16) for pr in prs]
                dqs = [jnp.dot(ds, k, preferred_element_type=F32) for ds in dss]
                dks = [lax.dot_general(ds, h[1], _TN, preferred_element_type=F32) for ds, h in zip(dss, heads)]
                dvs = [lax.dot_general(pb, h[2], _TN, preferred_element_type=F32) for pb, h in zip(pbs, heads)]
                dq = [x if y is None else y + x for x, y in zip(dqs, dq)]
                dk_ref[lo:hi, :] += functools.reduce(lambda a, b: a + b, dks)
                dv_ref[lo:hi, :] += functools.reduce(lambda a, b: a + b, dvs)
            for j, (sl, *_) in enumerate(heads):
                dq_ref[:, sl] = dq[j]

        pl.when(i < nct)(lambda: tile(ctx))
        pl.when(i >= nct)(lambda: tile(t))

    def at(kv, g, i):
        return lambda: (pl.program_id(0) == kv) & (pl.program_id(1) == g) & (pl.program_id(2) == i)

    rn = 0 if rider is None else rider.n
    per_kv = GROUP // hps
    qspec = pl.BlockSpec((TM, hps * HD), lambda kv, g, i: (i, kv * per_kv + g))
    kvspec = pl.BlockSpec((t, HD), lambda kv, g, i: (0, kv))
    lspec = pl.BlockSpec((None, TM, hps), lambda kv, g, i: (kv * per_kv + g, i, 0))
    return pl.pallas_call(
        _with_rider(body, 6, 3, rider, at(0, 0, 0), at(NKV - 1, 0, 0), at(NKV - 1, per_kv - 1, nt - 1)),
        name=name, grid=(NKV, per_kv, nt),
        in_specs=[qspec, kvspec, pl.BlockSpec((t, HD), lambda kv, g, i: (0, vblk + kv)), qspec, lspec, qspec]
        + [_HBM] * rn,
        out_specs=(qspec, kvspec, kvspec) + (_HBM,) * rn,
        out_shape=(jax.ShapeDtypeStruct((t, NH * HD), F32), jax.ShapeDtypeStruct((t, NKV * HD), F32),
                   jax.ShapeDtypeStruct((t, NKV * HD), F32)) + (() if rider is None else tuple(rider.out_shape)),
        scratch_shapes=[] if rider is None else rider.scratch,
        compiler_params=_cparams(3, 48),
    )(qr, kr, p, o, lse, do, *(() if rider is None else rider.xs))


def _decay_fwd(p, wd, bd, name):
    t = p.shape[0]

    def body(r_ref, w_ref, b_ref, z_ref, bc_ref):
        z = jnp.dot(r_ref[...].astype(BF16), w_ref[...].astype(BF16), preferred_element_type=F32) + b_ref[...]
        z_ref[...] = z
        la = (jnp.minimum(z, 0.0) - jnp.log(1.0 + jnp.exp(-jnp.abs(z)))) / GLA_TAU
        half = GH * GDK
        bc_ref[:, 0:half] = _chunk_sums(la[:, 0:half], False)
        bc_ref[:, half:] = _chunk_sums(la[:, half:], True)

    row = pl.BlockSpec((TM, D), lambda i: (i, 0))
    return pl.pallas_call(
        body, name=name, grid=(t // TM,),
        in_specs=[_pcol("r", R_PAD), pl.BlockSpec((R_PAD, D), lambda i: (0, 0)), pl.BlockSpec((1, D), lambda i: (0, 0))],
        out_specs=(row, row),
        out_shape=(jax.ShapeDtypeStruct((t, D), F32), jax.ShapeDtypeStruct((t, D), F32)),
        compiler_params=_cparams(1, 32),
    )(p, wd, bd)


def _chunk_order(s, ncc, nc, rev):
    if not rev:
        return s
    return jnp.where(s < ncc, ncc - 1 - s, nc - 1 - (s - ncc))


GLA_CPS = TM // CH


class _Chain:
    def __init__(self, rev, d, h, sub, refs):
        self.rev, self.d, self.h, self.sub, self.refs = rev, d, h, sub, refs
        self.rows, self.k, self.v = slice(sub * CH, (sub + 1) * CH), _hk(h), _hv(h)
        self.last = sub * CH + (0 if rev else CH - 1)


def _gla_chains(dirs, step, backward):
    return [_Chain(rev, d, h, step if rev == backward else GLA_CPS - 1 - step, refs)
            for d, (rev, refs) in enumerate(dirs) for h in range(GH)]


def _hk(h):
    return slice(h * GDK, (h + 1) * GDK)


def _hv(h):
    return slice(h * GDV, (h + 1) * GDV)


def _chunk_sums(x, from_end):
    r = lax.broadcasted_iota(jnp.int32, (CH, CH), 0)
    c = lax.broadcasted_iota(jnp.int32, (CH, CH), 1)
    tri = ((c >= r) if from_end else (c <= r)).astype(F32)
    return jnp.concatenate([jnp.dot(tri, x[lo:lo + CH], preferred_element_type=F32, precision=HIGHEST)
                            for lo in range(0, x.shape[0], CH)], axis=0)


def _gla_factors(qs, ks, bcs, bls, revs):
    r = lax.broadcasted_iota(jnp.int32, (CH, CH), 0)
    c = lax.broadcasted_iota(jnp.int32, (CH, CH), 1)
    keeps = [(c >= r) if rev else (c <= r) for rev in revs]
    qs, ks = [q.astype(F32) for q in qs], [k.astype(F32) for k in ks]
    qts = [q * GLA_SCALE * jnp.exp(bc) for q, bc in zip(qs, bcs)]
    kts = [k * jnp.exp(-bc) for k, bc in zip(ks, bcs)]
    khs = [k * jnp.exp(bl - bc) for k, bl, bc in zip(ks, bls, bcs)]
    gls = [jnp.exp(bl) for bl in bls]
    return qts, kts, gls, khs, keeps


def _gla_loads(ch):
    qs = [c.refs[0][c.rows, c.k] for c in ch]
    ks = [c.refs[1][c.rows, c.k] for c in ch]
    bcs = [c.refs[3][c.rows, c.k] for c in ch]
    bls = [c.refs[3][c.last:c.last + 1, c.k] for c in ch]
    return qs, ks, bcs, bls


_NT = (((1,), (1,)), ((), ()))
_TN = (((0,), (0,)), ((), ()))


def _gla_specs(ncs, ns, rev, backward):
    def idx(s):
        return _chunk_order((ns - 1 - s) if backward else s, ncs, ns, rev)

    wk, wv = GH * GDK, GH * GDV
    qb, kb, vb = OFF["gq"] // wk, OFF["gk"] // wk, OFF["gv"] // wv
    lab = 1 if rev else 0
    q = pl.BlockSpec((TM, wk), lambda s: (idx(s), qb))
    k = pl.BlockSpec((TM, wk), lambda s: (idx(s), kb))
    v = pl.BlockSpec((TM, wv), lambda s: (idx(s), vb))
    la = pl.BlockSpec((TM, wk), lambda s: (idx(s), lab))
    o = pl.BlockSpec((TM, wv), lambda s: (idx(s), 0))
    dk = pl.BlockSpec((TM, wk), lambda s: (idx(s), 0))
    st = pl.BlockSpec((GLA_CPS, GH, GDV, GDK), lambda s: (idx(s), 0, 0, 0))
    return q, k, v, la, o, dk, st


def _gla_fwd(p, la, ncs, name):
    t = p.shape[0]
    nc, ns = t // CH, t // TM
    specs = [_gla_specs(ncs, ns, rev, False) for rev in (False, True)]

    def body(qf, kf, vf, laf, qb_, kb_, vb_, lab, of, stf, ob, stb, s_scr):
        @pl.when(pl.program_id(0) == 0)
        def _():
            s_scr[...] = jnp.zeros_like(s_scr)

        dirs = ((False, (qf, kf, vf, laf, of, stf)), (True, (qb_, kb_, vb_, lab, ob, stb)))
        for step in range(GLA_CPS):
            ch = _gla_chains(dirs, step, False)
            qts, kts, gls, khs, keeps = _gla_factors(*_gla_loads(ch), [c.rev for c in ch])
            sts = [s_scr[c.d, c.h] for c in ch]
            for c, st in zip(ch, sts):
                c.refs[5][c.sub, c.h] = st.astype(BF16)
            vbs = [c.refs[2][c.rows, c.v].astype(BF16) for c in ch]
            qbs = [qt.astype(BF16) for qt in qts]
            a_s = [jnp.where(keep, lax.dot_general(qb, kt.astype(BF16), _NT, preferred_element_type=F32), 0.0)
                   for keep, qb, kt in zip(keeps, qbs, kts)]
            inter = [lax.dot_general(qb, st.astype(BF16), _NT, preferred_element_type=F32) for qb, st in zip(qbs, sts)]
            intra = [jnp.dot(a.astype(BF16), vb, preferred_element_type=F32) for a, vb in zip(a_s, vbs)]
            for c, x, y in zip(ch, inter, intra):
                c.refs[4][c.rows, c.v] = (x + y).astype(BF16)
            upd = [lax.dot_general(vb, kh.astype(BF16), _TN, preferred_element_type=F32) for vb, kh in zip(vbs, khs)]
            for c, st, gl, u in zip(ch, sts, gls, upd):
                s_scr[c.d, c.h] = st * gl + u

    o_shape = jax.ShapeDtypeStruct((t, GH * GDV), BF16)
    st_shape = jax.ShapeDtypeStruct((nc, GH, GDV, GDK), BF16)
    return pl.pallas_call(
        body, name=name, grid=(ns,),
        in_specs=[sp for s_ in specs for sp in s_[:4]],
        out_specs=tuple(sp for s_ in specs for sp in (s_[4], s_[6])),
        out_shape=(o_shape, st_shape, o_shape, st_shape),
        scratch_shapes=[pltpu.VMEM((2, GH, GDV, GDK), F32)], compiler_params=_cparams(1, 32),
    )(p, p, p, la, p, p, p, la)


def _gla_bwd(p, la, do, stf, stb, ncs, name):
    t = p.shape[0]
    ns = t // TM
    specs = [_gla_specs(ncs, ns, rev, True) for rev in (False, True)]

    def mm(xs, ys, dims=None):
        if dims is None:
            return [jnp.dot(x, y, preferred_element_type=F32) for x, y in zip(xs, ys)]
        return [lax.dot_general(x, y, dims, preferred_element_type=F32) for x, y in zip(xs, ys)]

    def body(*refs):
        ins_f, ins_b, outs_f, outs_b, ds_scr = refs[0:6], refs[6:12], refs[12:16], refs[16:20], refs[20]

        @pl.when(pl.program_id(0) == 0)
        def _():
            ds_scr[...] = jnp.zeros_like(ds_scr)

        dirs = ((False, (*ins_f, *outs_f)), (True, (*ins_b, *outs_b)))
        row = lax.broadcasted_iota(jnp.int32, (CH, GDK), 0)
        for step in range(GLA_CPS):
            ch = _gla_chains(dirs, step, True)
            revs = [c.rev for c in ch]
            loads = _gla_loads(ch)
            bcs = loads[2]
            qts, kts, gls, khs, keeps = _gla_factors(*loads, revs)
            stvs = [c.refs[5][c.sub, c.h] for c in ch]
            dsns = [ds_scr[c.d, c.h] for c in ch]
            dsbs = [x.astype(BF16) for x in dsns]
            vbs = [c.refs[2][c.rows, c.v].astype(BF16) for c in ch]
            dobs = [c.refs[4][c.rows, c.v].astype(BF16) for c in ch]
            qbs, kbs = [x.astype(BF16) for x in qts], [x.astype(BF16) for x in kts]
            a_s = [jnp.where(keep, x, 0.0).astype(BF16) for keep, x in zip(keeps, mm(qbs, kbs, _NT))]
            das = [jnp.where(keep, x, 0.0).astype(BF16) for keep, x in zip(keeps, mm(dobs, vbs, _NT))]
            dqts = [x + y for x, y in zip(mm(dobs, stvs), mm(das, kbs))]
            dkhs = mm(vbs, dsbs)
            dkts = [x + dkh * gl for x, dkh, gl in zip(mm(das, qbs, _TN), dkhs, gls)]
            for c, x, y in zip(ch, mm(a_s, dobs, _TN), mm([kh.astype(BF16) for kh in khs], dsbs, _NT)):
                c.refs[8][c.rows, c.v] = (x + y).astype(BF16)
            for c, x, dsn, gl in zip(ch, mm(dobs, qbs, _TN), dsns, gls):
                ds_scr[c.d, c.h] = x + dsn * gl
            dgls = [jnp.sum(st.astype(F32) * dsn, axis=0, keepdims=True) + jnp.sum(dkh * kt, axis=0, keepdims=True)
                    for st, dsn, dkh, kt in zip(stvs, dsns, dkhs, kts)]
            dbcs = [dqt * qt - dkt * kt + jnp.where(row == (0 if rev else CH - 1), dgl * gl, 0.0)
                    for rev, dqt, qt, dkt, kt, dgl, gl in zip(revs, dqts, qts, dkts, kts, dgls, gls)]
            for c, dbc, dqt, dkt, bc in zip(ch, dbcs, dqts, dkts, bcs):
                c.refs[9][c.rows, c.k] = dbc
                c.refs[6][c.rows, c.k] = (dqt * (GLA_SCALE * jnp.exp(bc))).astype(BF16)
                c.refs[7][c.rows, c.k] = (dkt * jnp.exp(-bc)).astype(BF16)

    k_shape = jax.ShapeDtypeStruct((t, GH * GDK), BF16)
    v_shape = jax.ShapeDtypeStruct((t, GH * GDV), BF16)
    c_shape = jax.ShapeDtypeStruct((t, GH * GDK), F32)
    res = pl.pallas_call(
        body, name=name, grid=(ns,),
        in_specs=[sp for q_s, k_s, v_s, la_s, o_s, _, st_s in specs for sp in (q_s, k_s, v_s, la_s, o_s, st_s)],
        out_specs=tuple(sp for _, _, _, _, o_s, dk_s, _ in specs for sp in (dk_s, dk_s, o_s, dk_s)),
        out_shape=(k_shape, k_shape, v_shape, c_shape) * 2,
        scratch_shapes=[pltpu.VMEM((2, GH, GDV, GDK), F32)], compiler_params=_cparams(1, 32),
    )(p, p, p, la, do, stf, p, p, p, la, do, stb)
    return res[:4], res[4:]


def _gla_merge_bwd(gf, gb, z, p, wd, dp, name):
    t = p.shape[0]
    w2 = GH * GDK

    def body(dqf, dkf, dvf, dlf, dqb, dkb, dvb, dlb, z_ref, r_ref, w_ref, _, dp_ref, dr_ref, db_ref, dw_ref):
        i = pl.program_id(0)
        dp_ref[:, 0:D] = (dvf[...].astype(F32) + dvb[...].astype(F32)).astype(BF16)
        dp_ref[:, D:D + w2] = (dqf[...].astype(F32) + dqb[...].astype(F32)).astype(BF16)
        dp_ref[:, D + w2:D + 2 * w2] = (dkf[...].astype(F32) + dkb[...].astype(F32)).astype(BF16)
        zv = z_ref[...]
        dlf_, dlb_ = _chunk_sums(dlf[...], True), _chunk_sums(dlb[...], False)
        dz = jnp.concatenate([dlf_, dlb_], axis=1) * (_sigmoid(-zv) / GLA_TAU)
        dzb = dz.astype(BF16)
        dr_ref[...] = lax.dot_general(dzb, w_ref[...].astype(BF16), _NT, preferred_element_type=F32).astype(BF16)

        @pl.when(i == 0)
        def _():
            db_ref[...] = jnp.zeros_like(db_ref)
            dw_ref[...] = jnp.zeros_like(dw_ref)

        db_ref[...] += jnp.sum(dz, axis=0, keepdims=True)
        dw_ref[...] += lax.dot_general(r_ref[...].astype(BF16), dzb, _TN, preferred_element_type=F32)

    half = pl.BlockSpec((TM, w2), lambda i: (i, 0))
    row = pl.BlockSpec((TM, D), lambda i: (i, 0))
    wspec = pl.BlockSpec((R_PAD, D), lambda i: (0, 0))
    return pl.pallas_call(
        body, name=name, grid=(t // TM,),
        in_specs=[half, half, row, half, half, half, row, half, row, _pcol("r", R_PAD), wspec, _HBM],
        out_specs=(_dp_spec("gla"), pl.BlockSpec((TM, R_PAD), lambda i: (i, 0)),
                   pl.BlockSpec((8, D), lambda i: (0, 0)), wspec),
        out_shape=(jax.ShapeDtypeStruct(dp.shape, dp.dtype), jax.ShapeDtypeStruct((t, R_PAD), BF16),
                   jax.ShapeDtypeStruct((8, D), F32), jax.ShapeDtypeStruct((R_PAD, D), F32)),
        input_output_aliases={11: 0}, compiler_params=_cparams(1, 40),
    )(*gf, *gb, z, p, wd, dp)


def _dp_tail(dk, dv, dr, dp, name):
    t = dk.shape[0]
    wk = NKV * HD

    def body(dk_ref, dv_ref, dr_ref, _, dp_ref):
        dp_ref[:, 0:wk] = dk_ref[...]
        dp_ref[:, wk:2 * wk] = dv_ref[...].astype(BF16)
        dp_ref[:, 2 * wk:2 * wk + R_PAD] = dr_ref[...]
        dp_ref[:, 2 * wk + R_PAD:] = jnp.zeros((TM, DP_BLOCKS["tail"][1] - 2 * wk - R_PAD), BF16)

    kv = pl.BlockSpec((TM, wk), lambda i: (i, 0))
    return pl.pallas_call(
        body, name=name, grid=(t // TM,),
        in_specs=[kv, kv, pl.BlockSpec((TM, R_PAD), lambda i: (i, 0)), _HBM], out_specs=_dp_spec("tail"),
        out_shape=jax.ShapeDtypeStruct(dp.shape, dp.dtype), input_output_aliases={3: 0},
        compiler_params=_cparams(1, 32),
    )(dk, dv, dr, dp)


def _branch_fwd(att, of, ob, p, gla_g, name):
    t = p.shape[0]

    def body(att_ref, of_ref, ob_ref, za_ref, zg_ref, g_ref, yb_ref, yc_ref):
        za = za_ref[...].astype(F32)
        yb_ref[...] = (att_ref[...].astype(F32) * (za * _sigmoid(za))).astype(BF16)
        for h in range(GH):
            sl = slice(h * GDV, (h + 1) * GDV)
            o = of_ref[:, sl].astype(F32) + ob_ref[:, sl].astype(F32)
            n = o * lax.rsqrt(jnp.mean(o * o, axis=-1, keepdims=True) + EPS) * g_ref[...]
            zh = zg_ref[:, sl].astype(F32)
            yc_ref[:, sl] = (n * (zh * _sigmoid(zh))).astype(BF16)

    rt = _big_rows(t)
    row = pl.BlockSpec((rt, D), lambda i: (i, 0))
    return pl.pallas_call(
        body, name=name, grid=(t // rt,),
        in_specs=[row, row, row, _pcol("z_attn", D, rt), _pcol("zg", D, rt), pl.BlockSpec((1, GDV), lambda i: (0, 0))],
        out_specs=(row, row),
        out_shape=(jax.ShapeDtypeStruct((t, D), BF16), jax.ShapeDtypeStruct((t, D), BF16)),
        compiler_params=_cparams(1, 40),
    )(att, of, ob, p, p, gla_g)


def _branch_bwd(dyb, dyc, att, of, ob, p, gla_g, dp, name):
    t = p.shape[0]

    def body(dyb_ref, dyc_ref, att_ref, of_ref, ob_ref, za_ref, zg_ref, g_ref, _, datt_ref, do_ref, dp_ref, dg_ref):
        i = pl.program_id(0)

        @pl.when(i == 0)
        def _():
            dg_ref[...] = jnp.zeros_like(dg_ref)

        za, dyb = za_ref[...].astype(F32), dyb_ref[...].astype(F32)
        sa = _sigmoid(za)
        datt_ref[...] = (dyb * (za * sa)).astype(BF16)
        dp_ref[:, 0:D] = (dyb * att_ref[...].astype(F32) * (sa * (1.0 + za * (1.0 - sa)))).astype(BF16)
        g = g_ref[...]
        for h in range(GH):
            sl = slice(h * GDV, (h + 1) * GDV)
            o = of_ref[:, sl].astype(F32) + ob_ref[:, sl].astype(F32)
            r = lax.rsqrt(jnp.mean(o * o, axis=-1, keepdims=True) + EPS)
            oh = o * r
            zh, dyc = zg_ref[:, sl].astype(F32), dyc_ref[:, sl].astype(F32)
            sg = _sigmoid(zh)
            dn = dyc * (zh * sg)
            dp_ref[:, D + h * GDV:D + (h + 1) * GDV] = (dyc * (oh * g) * (sg * (1.0 + zh * (1.0 - sg)))).astype(BF16)
            doh = dn * g
            do_ref[:, sl] = (r * (doh - oh * jnp.mean(doh * oh, axis=-1, keepdims=True))).astype(BF16)
            dg_ref[...] += jnp.sum(dn * oh, axis=0, keepdims=True)

    row = pl.BlockSpec((TM, D), lambda i: (i, 0))
    return pl.pallas_call(
        body, name=name, grid=(t // TM,),
        in_specs=[row, row, row, row, row, _pcol("z_attn", D), _pcol("zg", D), pl.BlockSpec((1, GDV), lambda i: (0, 0)),
                  _HBM],
        out_specs=(row, row, _dp_spec("branch"), pl.BlockSpec((8, GDV), lambda i: (0, 0))),
        out_shape=(jax.ShapeDtypeStruct((t, D), BF16), jax.ShapeDtypeStruct((t, D), BF16),
                   jax.ShapeDtypeStruct(dp.shape, dp.dtype), jax.ShapeDtypeStruct((8, GDV), F32)),
        input_output_aliases={8: 2}, compiler_params=_cparams(1, 48),
    )(dyb, dyc, att, of, ob, p, p, gla_g, dp)


def _merge_fwd(bra, brb, brc, p, b_gate, name):
    t = p.shape[0]
    mgb = OFF["mg"] // D

    def body(a_ref, b_ref, c_ref, ga_ref, gb_ref, gc_ref, bg_ref, m_ref):
        m_ref[...] = (_sigmoid(ga_ref[...].astype(F32) + bg_ref[:, 0:D]) * a_ref[...].astype(F32)
                      + _sigmoid(gb_ref[...].astype(F32) + bg_ref[:, D:2 * D]) * b_ref[...].astype(F32)
                      + _sigmoid(gc_ref[...].astype(F32) + bg_ref[:, 2 * D:3 * D]) * c_ref[...].astype(F32)).astype(BF16)

    rt = _big_rows(t)
    row = pl.BlockSpec((rt, D), lambda i: (i, 0))
    gates = [pl.BlockSpec((rt, D), functools.partial(lambda i, b: (i, b), b=mgb + j)) for j in range(3)]
    return pl.pallas_call(
        body, name=name, grid=(t // rt,),
        in_specs=[row, row, row, *gates, pl.BlockSpec((1, 3 * D), lambda i: (0, 0))],
        out_specs=row, out_shape=jax.ShapeDtypeStruct((t, D), BF16), compiler_params=_cparams(1, 40),
    )(bra, brb, brc, p, p, p, b_gate)


def _merge_bwd(dm, bra, brb, brc, p, b_gate, name):
    t = p.shape[0]
    mgb = OFF["mg"] // D

    def body(dm_ref, a_ref, b_ref, c_ref, ga_ref, gb_ref, gc_ref, bg_ref, da_ref, db_ref, dc_ref, dmg_ref, dbg_ref):
        i = pl.program_id(0)

        @pl.when(i == 0)
        def _():
            dbg_ref[...] = jnp.zeros_like(dbg_ref)

        dm = dm_ref[...].astype(F32)
        for j, (br_ref, g_ref, d_ref) in enumerate(((a_ref, ga_ref, da_ref), (b_ref, gb_ref, db_ref), (c_ref, gc_ref, dc_ref))):
            sl = slice(j * D, (j + 1) * D)
            g = _sigmoid(g_ref[...].astype(F32) + bg_ref[:, sl])
            d_ref[...] = (dm * g).astype(BF16)
            dmg = dm * br_ref[...].astype(F32) * (g * (1.0 - g))
            dmg_ref[:, sl] = dmg.astype(BF16)
            dbg_ref[:, sl] += jnp.sum(dmg, axis=0, keepdims=True)

    rt = _big_rows(t)
    row = pl.BlockSpec((rt, D), lambda i: (i, 0))
    gates = [pl.BlockSpec((rt, D), functools.partial(lambda i, b: (i, b), b=mgb + j)) for j in range(3)]
    return pl.pallas_call(
        body, name=name, grid=(t // rt,),
        in_specs=[row, row, row, row, *gates, pl.BlockSpec((1, 3 * D), lambda i: (0, 0))],
        out_specs=(row, row, row, _dp_spec("merge", rt), pl.BlockSpec((8, 3 * D), lambda i: (0, 0))),
        out_shape=(jax.ShapeDtypeStruct((t, D), BF16),) * 3 + (jax.ShapeDtypeStruct((t, NP), BF16),
                                                                jax.ShapeDtypeStruct((8, 3 * D), F32)),
        compiler_params=_cparams(1, 56),
    )(dm, bra, brb, brc, p, p, p, b_gate)


def _adam_update(ns, g_ref, w_ref, m_ref, v_ref, go_ref, d_ref, mo_ref, vo_ref):
    g = g_ref[0].astype(F32)
    for s in range(1, ns):
        g = g + g_ref[s].astype(F32)
    mn = ADAM_B1 * m_ref[...] + (1.0 - ADAM_B1) * g
    vn = ADAM_B2 * v_ref[...] + (1.0 - ADAM_B2) * jnp.square(g)
    m_hat = mn / (1.0 - ADAM_B1 ** ADAM_STEP)
    v_hat = vn / (1.0 - ADAM_B2 ** ADAM_STEP)
    go_ref[...] = g
    d_ref[...] = -ADAM_LR * (m_hat / (jnp.sqrt(v_hat) + ADAM_EPS) + ADAM_WD * w_ref[...])
    mo_ref[...] = mn
    vo_ref[...] = vn


def _adamw(gsrc, w, m, v, name):
    ns, nl, r, c = gsrc.shape
    gb = gsrc.dtype.itemsize

    def fits(rows, cols):
        lanes = -(-cols // LANE) * LANE
        return ns * rows * lanes * gb <= ADAM_SRC_BYTES and rows * lanes * 4 <= ADAM_ROW_BYTES

    tr, tc = r, c
    if not fits(r, c):
        rows = [cand for cand in range(16, r, 16) if r % cand == 0 and fits(cand, c)]
        cols = [cand for cand in range(LANE, c, LANE) if c % cand == 0 and fits(r, cand)]
        if rows:
            tr = rows[-1]
        else:
            tc = cols[-1]

    def body(*refs):
        _adam_update(ns, *refs)

    row = pl.BlockSpec((None, tr, tc), lambda l, i, j: (l, i, j))
    return pl.pallas_call(
        body, name=name, grid=(nl, r // tr, c // tc),
        in_specs=[pl.BlockSpec((ns, None, tr, tc), lambda l, i, j: (0, l, i, j)), row, row, row],
        out_specs=(row,) * 4, out_shape=(jax.ShapeDtypeStruct((nl, r, c), F32),) * 4,
        compiler_params=_cparams(3, 48),
    )(gsrc, w, m, v)


def _pair_sum(a, b, name):
    s, r, c = a.shape
    tc = _pick(c, (256, 128))

    def body(a_ref, b_ref, o_ref):
        o_ref[...] = (a_ref[...].astype(F32) + b_ref[...].astype(F32)).astype(BF16)

    blk = pl.BlockSpec((None, r, tc), lambda i, j: (i, 0, j))
    return pl.pallas_call(
        body, name=name, grid=(s, c // tc), in_specs=[blk, blk], out_specs=blk,
        out_shape=jax.ShapeDtypeStruct(a.shape, BF16), compiler_params=_cparams(2, 32),
    )(a, b)


def _adamw_small(items, name):
    k = len(items)

    def body(*refs):
        for j in range(k):
            _adam_update(items[j][0].shape[0], *refs[4 * j:4 * j + 4], *refs[4 * k + 4 * j:4 * k + 4 * j + 4])

    out = pl.pallas_call(
        body, name=name,
        out_shape=tuple(jax.ShapeDtypeStruct(w.shape, F32) for _, w, _, _ in items for _ in range(4)),
    )(*[a for item in items for a in item])
    return [out[4 * j:4 * j + 4] for j in range(k)]


def _rope_tables(ctx, seq):
    n_rows = seq // GRID_W
    pairs = HD // 4
    row = jnp.repeat(jnp.arange(n_rows, dtype=F32), GRID_W)
    col = jnp.tile(jnp.arange(GRID_W, dtype=F32), n_rows)
    freqs = ROPE_THETA ** (-jnp.arange(pairs, dtype=F32) * 2.0 / (HD // 2))
    ar, ac = row[:, None] * freqs, col[:, None] * freqs
    cos_l = jnp.concatenate([jnp.cos(ar), jnp.cos(ar), jnp.cos(ac), jnp.cos(ac)], axis=1)
    sin_l = jnp.concatenate([-jnp.sin(ar), jnp.sin(ar), -jnp.sin(ac), jnp.sin(ac)], axis=1)
    cos_t = jnp.concatenate([jnp.ones((ctx, HD), F32), cos_l], axis=0)
    sin_t = jnp.concatenate([jnp.zeros((ctx, HD), F32), sin_l], axis=0)
    return cos_t, sin_t


def _to_proj_layout(wt):
    parts = [wt[s:s + wd] for _, s, wd in _SEGS]
    used = sum(wd for _, _, wd in _SEGS)
    parts.append(jnp.zeros((NP - used, wt.shape[1]), wt.dtype))
    return jnp.concatenate(parts, axis=0)


def _from_proj_layout(g):
    order = sorted(_SEGS, key=lambda sg: sg[1])
    return jnp.concatenate([g[OFF[n]:OFF[n] + wd] for n, _, wd in order], axis=0)


def _row0(a):
    return a[..., 0, :]


def kernel(x, c, ctx, c_ctx, w_ada, b_ada, g_pre, g_post, w_in, conv_w, q_norm_g, k_norm_g, w_decay_fwd, b_decay_fwd, w_decay_bwd, b_decay_bwd, gla_norm_g, w_br_conv, w_br_attn, w_br_gla, b_gate, w_out, loss_target, m_c_ctx, m_w_ada, m_b_ada, m_g_pre, m_g_post, m_w_in, m_conv_w, m_q_norm_g, m_k_norm_g, m_w_decay_fwd, m_b_decay_fwd, m_w_decay_bwd, m_b_decay_bwd, m_gla_norm_g, m_w_br_conv, m_w_br_attn, m_w_br_gla, m_b_gate, m_w_out, v_c_ctx, v_w_ada, v_b_ada, v_g_pre, v_g_post, v_w_in, v_conv_w, v_q_norm_g, v_k_norm_g, v_w_decay_fwd, v_b_decay_fwd, v_w_decay_bwd, v_b_decay_bwd, v_gla_norm_g, v_w_br_conv, v_w_br_attn, v_w_br_gla, v_b_gate, v_w_out):
    seq, n_ctx = x.shape[1], ctx.shape[1]
    assert n_ctx % TM == 0 and seq % TM == 0 and seq % GRID_W == 0
    t = n_ctx + seq
    nct = n_ctx // TM
    dev = 4 * lax.axis_index("x") + 2 * lax.axis_index("y") + lax.axis_index("c")
    ada_w = w_ada.shape[2]
    in_w = w_in.shape[2]
    br_r = w_br_conv.shape[1]

    def in_t(a, l):
        return a.transpose(2, 0, 1)[:, l, :]

    wb = [w.astype(BF16) for w in (w_ada, w_br_conv, w_br_attn, w_br_gla, w_out)]
    wall = _all_gather([wb[0][0], in_t(w_in, 0).astype(BF16), conv_w, w_decay_fwd, w_decay_bwd],
                       "gather_first")
    later_square = _GatherRider([wb[1], wb[2], wb[3], wb[4]])
    later_in = _GatherRider([in_t(w_in, 1).astype(BF16), wb[0][1]])

    def full_small(g):
        return g.transpose(1, 2, 0, 3).reshape(DEPTH, g.shape[2], NDEV * g.shape[3])

    def full_in(g):
        return _to_proj_layout(g.reshape(IN_WIDTH, D))

    def full_ada(g):
        return g.transpose(1, 0, 2).reshape(D, 3 * D)

    w_ada_f = [full_ada(wall[0]), None]
    wp = [full_in(wall[1]), None]
    conv_f, wdf_f, wdb_f = full_small(wall[2]), full_small(wall[3]), full_small(wall[4])

    cos_t, sin_t = _rope_tables(n_ctx, seq)
    cc = jnp.concatenate([c_ctx[None, :], c.reshape(1, D), jnp.zeros((6, D), F32)], axis=0)
    silu_cc, dsilu_cc = _ada_in(cc)

    conv8, wd_pad, bd = [], [], []
    for l in range(DEPTH):
        conv8.append(jnp.concatenate([conv_f[l], jnp.zeros((5, D), F32)], axis=0))
        zr = jnp.zeros((GLA_RANK, GH * GDK), F32)
        wd_pad.append(jnp.concatenate([jnp.concatenate([wdf_f[l], zr], axis=1), jnp.concatenate([zr, wdb_f[l]], axis=1),
                                       jnp.zeros((R_PAD - 2 * GLA_RANK, D), F32)], axis=0))
        bd.append(jnp.concatenate([b_decay_fwd[l], b_decay_bwd[l]])[None, :])

    xs = jnp.concatenate([ctx[0], x[0]], axis=0)
    saved = []
    for l in range(DEPTH):
        n = f"l{l}_"
        mod = _mm(silu_cc, w_ada_f[l], n + "mod", bias=b_ada[l][None, :])
        mod3 = mod[0:2].reshape(2, 3, D)
        h = _prenorm_fwd(xs, g_pre[l][None, :], mod3, nct, n + "prenorm")
        if l == 0:
            p, *got = _mm(h, wp[l], n + "proj", tb=True, out_dtype=BF16, tm=t // 2, rider=later_square)
            w_brs_f = [g.transpose(1, 0, 2, 3).reshape(DEPTH, D, D) for g in got]
        else:
            p = _mm(h, wp[l], n + "proj", tb=True, out_dtype=BF16, tm=t // 2)
        cv, ya = _conv_fwd(p, conv8[l], nct, n + "conv")
        qr, kr = _qk_prep_fwd(p, q_norm_g[l][None, :], k_norm_g[l][None, :], cos_t, sin_t, n + "qk_prep")
        att, lse, *got = _attn_fwd(qr, kr, p, nct, n + "attn", rider=later_in if l == 0 else None)
        if l == 0:
            wp[1], w_ada_f[1] = full_in(got[0]), full_ada(got[1])
        z, la = _decay_fwd(p, wd_pad[l], bd[l], n + "decay")
        of, stf, ob, stb = _gla_fwd(p, la, nct, n + "gla")
        yb, yc = _branch_fwd(att, of, ob, p, gla_norm_g[l][None, :], n + "branch")
        bra = _mm(ya, w_brs_f[0][l], n + "br_conv", out_dtype=BF16)
        brb = _mm(yb, w_brs_f[1][l], n + "br_attn", out_dtype=BF16)
        brc = _mm(yc, w_brs_f[2][l], n + "br_gla", out_dtype=BF16)
        mm_ = _merge_fwd(bra, brb, brc, p, b_gate[l][None, :], n + "merge")
        out = _mm(mm_, w_brs_f[3][l], n + "out")
        x_new = _post_fwd(xs, out, g_post[l][None, :], mod3, nct, n + "post")
        saved.append(dict(x=xs, mod3=mod3, h=h, p=p, cv=cv, ya=ya, qr=qr, kr=kr, att=att, lse=lse, z=z, la=la, of=of, ob=ob,
                          stf=stf, stb=stb, yb=yb, yc=yc, bra=bra, brb=brb, brc=brc, m=mm_, out=out))
        xs = x_new

    dx, sq = _loss_grad(xs, loss_target[0], nct, "loss")
    loss = lax.psum(0.5 * sq[0, 0] / D, ("x", "y", "c"))

    gw = {k: [None] * DEPTH for k in ("w_in", "br_conv", "br_attn", "br_gla", "out", "b_gate", "g_pre", "g_post",
                                      "conv_w", "qg", "kg", "wd", "bdec", "gla_g", "dmod")}
    dctx = []

    def in_slots(l):
        return _from_proj_layout(gw["w_in"][l]).reshape(NDEV, in_w, D)

    def br_slots(l):
        return [gw[k][l].reshape(NDEV, br_r, D) for k in ("br_conv", "br_attn", "br_gla", "out")]

    for l in reversed(range(DEPTH)):
        n = f"l{l}_b_"
        s = saved[l]
        p = s["p"]
        d_out, dgt, gw["g_post"][l] = _post_bwd(dx, s["out"], g_post[l][None, :], s["mod3"], nct, n + "post")
        dm = _mm(d_out, w_brs_f[3][l], n + "dm", tb=True, out_dtype=BF16)
        gw["out"][l] = _mm(s["m"], d_out, n + "dw_out", ta=True, out_dtype=BF16)
        dbra, dbrb, dbrc, dp, gw["b_gate"][l] = _merge_bwd(dm, s["bra"], s["brb"], s["brc"], p, b_gate[l][None, :], n + "merge")
        dya = _mm(dbra, w_brs_f[0][l], n + "dya", tb=True, out_dtype=BF16)
        dyb = _mm(dbrb, w_brs_f[1][l], n + "dyb", tb=True, out_dtype=BF16)
        dyc = _mm(dbrc, w_brs_f[2][l], n + "dyc", tb=True, out_dtype=BF16)
        gw["br_conv"][l] = _mm(s["ya"], dbra, n + "dw_conv", ta=True, out_dtype=BF16)
        gw["br_attn"][l] = _mm(s["yb"], dbrb, n + "dw_attn", ta=True, out_dtype=BF16)
        gw["br_gla"][l] = _mm(s["yc"], dbrc, n + "dw_gla", ta=True, out_dtype=BF16)
        dcv, dp = _conv_bwd_a(dya, p, s["cv"], dp, n + "conv_a")
        dp, gw["conv_w"][l] = _conv_bwd_b(dcv, p, conv8[l], nct, dp, n + "conv_b")
        datt, dgo, dp, gw["gla_g"][l] = _branch_bwd(dyb, dyc, s["att"], s["of"], s["ob"], p, gla_norm_g[l][None, :], dp, n + "branch")
        ex1 = _ExchangeRider([in_slots(DEPTH - 1)] + br_slots(DEPTH - 1)) if l == 0 else None
        dqr, dkr, dv, *got = _attn_bwd(s["qr"], s["kr"], p, s["att"], s["lse"], datt, nct, n + "attn", rider=ex1)
        if l == 0:
            recv_in1, recv_br1 = got[0], got[1:]
        dp, dk, gw["qg"][l], gw["kg"][l] = _qk_prep_bwd(dqr, dkr, p, q_norm_g[l][None, :], k_norm_g[l][None, :], cos_t, sin_t, dp, n + "qk_prep")
        gf, gb = _gla_bwd(p, s["la"], dgo, s["stf"], s["stb"], nct, n + "gla")
        dp, dr, gw["bdec"][l], gw["wd"][l] = _gla_merge_bwd(gf, gb, s["z"], p, wd_pad[l], dp, n + "gla_merge")
        dp = _dp_tail(dk, dv, dr, dp, n + "dp_tail")
        tk_in = t // 2 if t % 32 == 0 else None
        if l == 0:
            gw["w_in"][l], *recv_br0 = _mm(dp, s["h"], n + "dw_in", ta=True, out_dtype=BF16, tk=tk_in,
                                           rider=_ExchangeRider(br_slots(0)))
        else:
            gw["w_in"][l] = _mm(dp, s["h"], n + "dw_in", ta=True, out_dtype=BF16, tk=tk_in)
        if l == 0:
            core = lax.axis_index("c")
            halves = in_slots(0).reshape(NDEV // 2, 2, in_w, D)
            kept = lax.dynamic_index_in_dim(halves, core, axis=1, keepdims=False)
            sent = lax.dynamic_index_in_dim(halves, 1 - core, axis=1, keepdims=False)
            from_sibling, = _comm_alone(_SwapRider([sent]), n + "swap_dw_in")
            chip_sum = _pair_sum(kept, from_sibling, n + "chip_sum_dw_in")
            dh, recv_in0 = _mm(dp, wp[l], n + "dh", tk=NP // 4, rider=_ExchangeRider([chip_sum], chips_only=True))
        else:
            dh = _mm(dp, wp[l], n + "dh", tk=NP // 4)
        dx, dsh, dsc, gw["g_pre"][l] = _prenorm_bwd(dh, s["x"], dx, g_pre[l][None, :], s["mod3"], nct, n + "prenorm")
        dmod = jnp.stack([_row0(dsh), _row0(dsc), _row0(dgt)], axis=1).reshape(2, 3 * D)
        gw["dmod"][l] = dmod
        dmod8 = jnp.concatenate([dmod, jnp.zeros((6, 3 * D), F32)], axis=0)
        dctx.append(_mm(dmod8, w_ada_f[l], n + "dsilu", tb=True))
    grad_x = dx[n_ctx:][None]
    g_cctx = _cctx_grad(dctx[0], dctx[1], dsilu_cc)[0]

    def st2(name):
        return jnp.stack(gw[name])

    g_b_ada = jnp.stack([gw["dmod"][l][0] + gw["dmod"][l][1] for l in range(DEPTH)])
    g_bdf = jnp.stack([gw["bdec"][l][0, :GH * GDK] for l in range(DEPTH)])
    g_bdb = jnp.stack([gw["bdec"][l][0, GH * GDK:] for l in range(DEPTH)])
    g_wdf = jnp.stack([gw["wd"][l][0:GLA_RANK, :GH * GDK] for l in range(DEPTH)])
    g_wdb = jnp.stack([gw["wd"][l][GLA_RANK:2 * GLA_RANK, GH * GDK:] for l in range(DEPTH)])
    rep_grads = [g_cctx, g_b_ada, st2("g_pre")[:, 0], st2("g_post")[:, 0], st2("qg")[:, 0], st2("kg")[:, 0], g_bdf, g_bdb,
                 st2("gla_g")[:, 0], st2("b_gate")[:, 0]]
    rep_w = [c_ctx, b_ada, g_pre, g_post, q_norm_g, k_norm_g, b_decay_fwd, b_decay_bwd, gla_norm_g, b_gate]
    rep_m = [m_c_ctx, m_b_ada, m_g_pre, m_g_post, m_q_norm_g, m_k_norm_g, m_b_decay_fwd, m_b_decay_bwd, m_gla_norm_g, m_b_gate]
    rep_v = [v_c_ctx, v_b_ada, v_g_pre, v_g_post, v_q_norm_g, v_k_norm_g, v_b_decay_fwd, v_b_decay_bwd, v_gla_norm_g, v_b_gate]
    def two_d(a):
        return a.reshape(1, -1) if a.ndim == 1 else a

    def owner_slots(g):
        return g.reshape(DEPTH, g.shape[1], NDEV, g.shape[2] // NDEV).transpose(2, 0, 1, 3)

    n_rep = len(rep_grads)
    small = _comm_alone(_Riders([
        _GatherRider([two_d(g) for g in rep_grads] + [silu_cc[0:2], jnp.stack(gw["dmod"])]),
        _ExchangeRider([owner_slots(st2("conv_w")[:, 0:3]), owner_slots(g_wdf), owner_slots(g_wdb)])]),
        "exchange_small_grads")
    rep_src, (a_all, d_all), sh_src = small[:n_rep], small[n_rep:n_rep + 2], small[n_rep + 2:]
    sh_w = [conv_w, w_decay_fwd, w_decay_bwd]
    sh_m = [m_conv_w, m_w_decay_fwd, m_w_decay_bwd]
    sh_v = [v_conv_w, v_w_decay_fwd, v_w_decay_bwd]
    small_out = _adamw_small(
        [(g, two_d(w), two_d(m), two_d(v)) for g, w, m, v in zip(rep_src, rep_w, rep_m, rep_v)]
        + list(zip(sh_src, sh_w, sh_m, sh_v)), "adam_small")
    rep_g, rep_d, rep_nm, rep_nv = [[small_out[j][k].reshape(rep_w[j].shape) for j in range(n_rep)] for k in range(4)]
    sh_gr, sh_d, sh_nm, sh_nv = [[small_out[n_rep + j][k] for j in range(len(sh_w))] for k in range(4)]

    a_all = a_all.reshape(NDEV * 2, D)
    d_all = d_all.transpose(1, 0, 2, 3).reshape(DEPTH, NDEV * 2, 3 * D)
    g_ada = jnp.stack([_mm(a_all, lax.dynamic_slice_in_dim(d_all[l], dev * ada_w, ada_w, axis=1), f"dw_ada{l}",
                           ta=True, precise=True, tk=NDEV * 2) for l in range(DEPTH)])
    ada_g, ada_d, ada_nm, ada_nv = _adamw(g_ada[None], w_ada, m_w_ada, v_w_ada, "adam_ada")

    big_w = [w_br_conv, w_br_attn, w_br_gla, w_out]
    big_m = [m_w_br_conv, m_w_br_attn, m_w_br_gla, m_w_out]
    big_v = [v_w_br_conv, v_w_br_attn, v_w_br_gla, v_w_out]
    big_out = [_adamw(jnp.stack([recv_br0[j], recv_br1[j]], axis=1), big_w[j], big_m[j], big_v[j], f"adam_big{j}")
               for j in range(len(big_w))]
    in_out = [_adamw(r_[:, None], in_t(w_in, l)[None], in_t(m_w_in, l)[None], in_t(v_w_in, l)[None], f"adam_in{l}")
              for l, r_ in enumerate((recv_in0, recv_in1))]
    in_res = [jnp.stack([in_out[l][k][0] for l in range(DEPTH)], axis=1).transpose(1, 2, 0) for k in range(4)]
    big_g, big_d, big_nm, big_nv = [[in_res[k]] + [o[k] for o in big_out] for k in range(4)]

    def ordered(rep, ada, big, sh):
        c_ctx_, b_ada_, g_pre_, g_post_, qg_, kg_, bdf_, bdb_, glag_, bgate_ = rep
        w_in_, brc_, bra_, brg_, wout_ = big
        conv_, wdf_, wdb_ = sh
        return [c_ctx_, ada, b_ada_, g_pre_, g_post_, w_in_, conv_, qg_, kg_, wdf_, bdf_, wdb_, bdb_, glag_,
                brc_, bra_, brg_, bgate_, wout_]

    return (loss, grad_x,
            *ordered(rep_g, ada_g, big_g, sh_gr), *ordered(rep_d, ada_d, big_d, sh_d),
            *ordered(rep_nm, ada_nm, big_nm, sh_nm), *ordered(rep_nv, ada_nv, big_nv, sh_nv))
```

```python
import functools

import numpy as np
import jax
import jax.numpy as jnp
from jax import lax
from jax.experimental import pallas as pl
from jax.experimental.pallas import tpu as pltpu

F32, BF16 = jnp.float32, jnp.bfloat16
HIGHEST = lax.Precision.HIGHEST

D = 1024
DEPTH = 2
GRID_W = 64
NH, NKV, HD = 8, 2, 128
GROUP = NH // NKV
ROPE_THETA = 10000.0
ATTN_SCALE = HD ** -0.5
Q_FOLD = ATTN_SCALE * 1.4426950408889634
P_HALO = 16
GH, GDK, GDV = 4, 128, 256
GLA_RANK = 16
GLA_TAU = 16.0
CH = 64
GLA_SCALE = GDK ** -0.5
EPS = 1e-6
NDEV = 8
LANE = 128
TM = 256
BIG_ROWS = 544
ATTN_HEADS_PER_STEP = 4
ATTN_FWD_KEY_CHUNK = 2176
ATTN_BWD_KEY_CHUNK = 256
KEY_ALIGN = LANE

ADAM_LR, ADAM_B1, ADAM_B2, ADAM_EPS, ADAM_WD, ADAM_STEP = 0.001, 0.9, 0.999, 1e-08, 0.01, 10

_SEGS = (("a_b", 0, 1024), ("a_z", 3072, 1024), ("a_c", 1024, 1024), ("a_x", 2048, 1024),
         ("z_attn", 5632, 1024), ("zg", 8736, 1024), ("gv", 7680, 1024), ("gq", 6656, 512), ("gk", 7168, 512),
         ("q", 4096, 1024), ("mg", 9760, 3072), ("k", 5120, 256), ("v", 5376, 256), ("r", 8704, 32))
DP_BLOCKS = {"conv_a": ("a_b", 2048), "conv_b": ("a_c", 2048), "branch": ("z_attn", 2048), "gla": ("gv", 2048),
             "q": ("q", 1024), "merge": ("mg", 3072), "tail": ("k", 1024)}
IN_WIDTH = 12832
NP = 13312
OFF = {}
_o = 0
for _n, _s, _w in _SEGS:
    OFF[_n] = _o
    _o += _w
R_PAD = 128


def _cparams(ngrid, vmem_mb):
    return pltpu.CompilerParams(dimension_semantics=("arbitrary",) * ngrid, vmem_limit_bytes=vmem_mb << 20)


def _pick(n, cands):
    for c in cands:
        if n % c == 0:
            return c
    return n


def _sigmoid(x):
    return 1.0 / (1.0 + jnp.exp(-x))


ADAM_SRC_BYTES = 8 << 20
ADAM_ROW_BYTES = 1 << 20


def _all_gather(xs, name):
    return _comm_alone(_GatherRider(xs), name)


_HBM = pl.BlockSpec(memory_space=pl.ANY)


class _Rider:
    def __init__(self, xs, out_shapes, remote_copies=NDEV - 1):
        self.xs, self.n = list(xs), len(xs)
        self.out_shape = [jax.ShapeDtypeStruct(s, x.dtype) for s, x in zip(out_shapes, xs)]
        self.scratch = [pltpu.SemaphoreType.DMA((remote_copies * self.n,)),
                        pltpu.SemaphoreType.DMA((remote_copies * self.n,)), pltpu.SemaphoreType.DMA((self.n,))]


class _GatherRider(_Rider):
    def __init__(self, xs):
        super().__init__(xs, [(NDEV,) + x.shape for x in xs])

    def _parts(self, x_refs, out_refs, sems):
        n = self.n
        send_sems, recv_sems, local_sems = sems
        mx, my, mc = lax.axis_index("x"), lax.axis_index("y"), lax.axis_index("c")
        me, sibling = (mx, my, mc), (mx, my, 1 - mc)
        chips = [(1 - mx, my), (mx, 1 - my), (1 - mx, 1 - my)]

        def slot(a, px, py, pc):
            return out_refs[a].at[4 * px + 2 * py + pc]

        def copy(k, a, block, to, own=False):
            return pltpu.make_async_remote_copy(
                src_ref=x_refs[a] if own else slot(a, *block), dst_ref=slot(a, *block),
                send_sem=send_sems.at[k * n + a], recv_sem=recv_sems.at[k * n + a],
                device_id=to, device_id_type=pl.DeviceIdType.MESH)

        mine = [pltpu.make_async_copy(x_refs[a], slot(a, *me), local_sems.at[a]) for a in range(n)]
        first = [copy(0, a, me, sibling, own=True) for a in range(n)]
        first += [copy(1 + j, a, me, (*chip, mc), own=True) for a in range(n) for j, chip in enumerate(chips)]
        landed = [copy(1 + j, a, (*chip, mc), me) for a in range(n) for j, chip in enumerate(chips)]
        passed = [copy(4 + j, a, (*chip, mc), sibling) for a in range(n) for j, chip in enumerate(chips)]
        from_sibling = [copy(0, a, sibling, me) for a in range(n)]
        from_sibling += [copy(4 + j, a, (*chip, 1 - mc), me) for a in range(n) for j, chip in enumerate(chips)]
        return mine, first, landed, passed, from_sibling

    def start(self, x_refs, out_refs, sems):
        mine, first, _, _, _ = self._parts(x_refs, out_refs, sems)
        for cp in mine + first:
            cp.start()

    def middle(self, x_refs, out_refs, sems):
        _, _, landed, passed, _ = self._parts(x_refs, out_refs, sems)
        for got, fwd in zip(landed, passed):
            got.wait_recv()
            fwd.start()

    def finish(self, x_refs, out_refs, sems):
        mine, first, _, passed, from_sibling = self._parts(x_refs, out_refs, sems)
        for cp in from_sibling:
            cp.wait_recv()
        for cp in first + passed:
            cp.wait_send()
        for cp in mine:
            cp.wait()


class _ExchangeRider(_Rider):
    def __init__(self, xs, chips_only=False):
        self.chips_only = chips_only
        super().__init__(xs, [x.shape for x in xs], 3 if chips_only else NDEV - 1)

    def _parts(self, x_refs, out_refs, sems):
        n = self.n
        send_sems, recv_sems, local_sems = sems
        mx, my, mc = lax.axis_index("x"), lax.axis_index("y"), lax.axis_index("c")
        me = 2 * mx + my if self.chips_only else 4 * mx + 2 * my + mc
        mine = [pltpu.make_async_copy(x_refs[a].at[me], out_refs[a].at[me], local_sems.at[a]) for a in range(n)]
        copies = []
        for a in range(n):
            for rel in range(1, 4 if self.chips_only else NDEV):
                bits = rel << 1 if self.chips_only else rel
                px = (1 - mx) if bits & 4 else mx
                py = (1 - my) if bits & 2 else my
                pc = (1 - mc) if bits & 1 else mc
                peer = 2 * px + py if self.chips_only else 4 * px + 2 * py + pc
                k = (rel - 1) * n + a
                copies.append(pltpu.make_async_remote_copy(
                    src_ref=x_refs[a].at[peer], dst_ref=out_refs[a].at[me],
                    send_sem=send_sems.at[k], recv_sem=recv_sems.at[k],
                    device_id=(px, py, pc), device_id_type=pl.DeviceIdType.MESH))
        return mine, copies

    def start(self, x_refs, out_refs, sems):
        mine, copies = self._parts(x_refs, out_refs, sems)
        for cp in mine + copies:
            cp.start()

    def middle(self, x_refs, out_refs, sems):
        pass

    def finish(self, x_refs, out_refs, sems):
        mine, copies = self._parts(x_refs, out_refs, sems)
        for cp in copies:
            cp.wait_recv()
        for cp in copies:
            cp.wait_send()
        for cp in mine:
            cp.wait()


class _SwapRider(_Rider):
    def __init__(self, xs):
        super().__init__(xs, [x.shape for x in xs], 1)

    def _parts(self, x_refs, out_refs, sems):
        send_sems, recv_sems, _ = sems
        sibling = (lax.axis_index("x"), lax.axis_index("y"), 1 - lax.axis_index("c"))
        return [pltpu.make_async_remote_copy(
            src_ref=x_refs[a], dst_ref=out_refs[a], send_sem=send_sems.at[a], recv_sem=recv_sems.at[a],
            device_id=sibling, device_id_type=pl.DeviceIdType.MESH) for a in range(self.n)]

    def start(self, x_refs, out_refs, sems):
        for cp in self._parts(x_refs, out_refs, sems):
            cp.start()

    def middle(self, x_refs, out_refs, sems):
        pass

    def finish(self, x_refs, out_refs, sems):
        copies = self._parts(x_refs, out_refs, sems)
        for cp in copies:
            cp.wait_recv()
        for cp in copies:
            cp.wait_send()


class _Riders:
    def __init__(self, riders):
        self.riders = list(riders)
        self.xs = [x for r in self.riders for x in r.xs]
        self.n = len(self.xs)
        self.out_shape = [s for r in self.riders for s in r.out_shape]
        self.scratch = [s for r in self.riders for s in r.scratch]

    def _each(self, method, x_refs, out_refs, sems):
        a = b = 0
        for r in self.riders:
            getattr(r, method)(x_refs[a:a + r.n], out_refs[a:a + r.n], sems[b:b + len(r.scratch)])
            a, b = a + r.n, b + len(r.scratch)

    def start(self, *refs):
        self._each("start", *refs)

    def middle(self, *refs):
        self._each("middle", *refs)

    def finish(self, *refs):
        self._each("finish", *refs)


def _comm_alone(rider, name):
    n = rider.n

    def body(*refs):
        x_refs, out_refs, sems = refs[:n], refs[n:2 * n], refs[2 * n:]
        rider.start(x_refs, out_refs, sems)
        rider.middle(x_refs, out_refs, sems)
        rider.finish(x_refs, out_refs, sems)

    return pl.pallas_call(
        body, name=name, out_shape=tuple(rider.out_shape), in_specs=[_HBM] * n, out_specs=(_HBM,) * n,
        scratch_shapes=rider.scratch,
    )(*rider.xs)


def _with_rider(body, nin, nout, rider, first, mid, last):
    if rider is None:
        return body
    n = rider.n

    def wrapped(*refs):
        ins, x_refs = refs[:nin], refs[nin:nin + n]
        outs, out_refs = refs[nin + n:nin + n + nout], refs[nin + n + nout:nin + 2 * n + nout]
        ns = len(rider.scratch)
        scratch, sems = refs[nin + 2 * n + nout:len(refs) - ns], refs[len(refs) - ns:]

        @pl.when(first())
        def _():
            rider.start(x_refs, out_refs, sems)

        body(*ins, *outs, *scratch)

        @pl.when(mid())
        def _():
            rider.middle(x_refs, out_refs, sems)

        @pl.when(last())
        def _():
            rider.finish(x_refs, out_refs, sems)

    return wrapped


def _mm(a, b, name, ta=False, tb=False, out_dtype=F32, bias=None, precise=False, tm=None, tn=None, tk=None, rider=None):
    m, k = (a.shape[1], a.shape[0]) if ta else a.shape
    n = b.shape[0] if tb else b.shape[1]
    assert k == (b.shape[1] if tb else b.shape[0])
    tm = tm or _pick(m, (1088, 1024, 512, 256, 128))
    tn = tn or _pick(n, (1024, 512, 384, 256, 128))
    tk = tk or _pick(k, (1024, 1088, 512, 256, 128))
    nk = k // tk
    dn = (((0 if ta else 1,), (1 if tb else 0,)), ((), ()))

    def body(*refs):
        if bias is None:
            a_ref, b_ref, o_ref = refs[:3]
            bias_ref = None
        else:
            a_ref, b_ref, bias_ref, o_ref = refs[:4]
        x, y = a_ref[...], b_ref[...]
        if precise:
            p = lax.dot_general(x.astype(F32), y.astype(F32), dn, preferred_element_type=F32, precision=HIGHEST)
        else:
            p = lax.dot_general(x.astype(BF16), y.astype(BF16), dn, preferred_element_type=F32)

        def finish(acc):
            if bias_ref is not None:
                acc = acc + bias_ref[...]
            o_ref[...] = acc.astype(out_dtype)

        if nk == 1:
            finish(p)
        else:
            acc_ref = refs[-1]
            kk = pl.program_id(2)

            @pl.when(kk == 0)
            def _():
                acc_ref[...] = p

            @pl.when(kk > 0)
            def _():
                acc_ref[...] += p

            @pl.when(kk == nk - 1)
            def _():
                finish(acc_ref[...])

    a_spec = pl.BlockSpec((tk, tm), lambda i, j, kk: (kk, i)) if ta else pl.BlockSpec((tm, tk), lambda i, j, kk: (i, kk))
    b_spec = pl.BlockSpec((tn, tk), lambda i, j, kk: (j, kk)) if tb else pl.BlockSpec((tk, tn), lambda i, j, kk: (kk, j))
    in_specs = [a_spec, b_spec]
    args = [a, b]
    if bias is not None:
        in_specs.append(pl.BlockSpec((1, tn), lambda i, j, kk: (0, j)))
        args.append(bias)
    grid = (m // tm, n // tn, nk)
    out_spec = pl.BlockSpec((tm, tn), lambda i, j, kk: (i, j))
    scratch = [pltpu.VMEM((tm, tn), F32)] if nk > 1 else []
    if rider is None:
        return pl.pallas_call(
            body, name=name, grid=grid, in_specs=in_specs, out_specs=out_spec,
            out_shape=jax.ShapeDtypeStruct((m, n), out_dtype), scratch_shapes=scratch, compiler_params=_cparams(3, 56),
        )(*args)

    def at(step):
        return lambda: ((pl.program_id(0) == step[0]) & (pl.program_id(1) == step[1]) & (pl.program_id(2) == step[2]))

    end = tuple(g - 1 for g in grid)
    step = grid[0] * grid[1] * grid[2] * 7 // 8
    late = (step // (grid[1] * grid[2]), step // grid[2] % grid[1], step % grid[2])
    return pl.pallas_call(
        _with_rider(body, len(args), 1, rider, at((0, 0, 0)), at(late), at(end)),
        name=name, grid=grid, in_specs=in_specs + [_HBM] * rider.n, out_specs=(out_spec,) + (_HBM,) * rider.n,
        out_shape=(jax.ShapeDtypeStruct((m, n), out_dtype),) + tuple(rider.out_shape),
        scratch_shapes=scratch + rider.scratch, compiler_params=_cparams(3, 56),
    )(*args, *rider.xs)


def _ada_in(cc):
    def body(c_ref, s_ref, d_ref):
        x = c_ref[...]
        sg = _sigmoid(x)
        s_ref[...] = x * sg
        d_ref[...] = sg * (1.0 + x * (1.0 - sg))

    return pl.pallas_call(body, name="ada_in", out_shape=(jax.ShapeDtypeStruct(cc.shape, F32),) * 2)(cc)


def _cctx_grad(t0, t1, dsilu):
    def body(a_ref, b_ref, d_ref, o_ref):
        o_ref[...] = (a_ref[...] + b_ref[...]) * d_ref[...]

    return pl.pallas_call(body, name="cctx_grad", out_shape=jax.ShapeDtypeStruct(t0.shape, F32))(t0, t1, dsilu)


def _seg_spec(nct, rows=3):
    return pl.BlockSpec((None, rows, D), lambda i: (jnp.where(i >= nct, 1, 0), 0, 0))


def _prenorm_fwd(x, g_pre, mod3, nct, name):
    t = x.shape[0]

    def body(x_ref, g_ref, mod_ref, h_ref):
        xv = x_ref[...]
        r = lax.rsqrt(jnp.mean(xv * xv, axis=-1, keepdims=True) + EPS)
        y = xv * r * g_ref[...]
        h_ref[...] = (y * (1.0 + mod_ref[1:2, :]) + mod_ref[0:1, :]).astype(BF16)

    return pl.pallas_call(
        body, name=name, grid=(t // TM,),
        in_specs=[pl.BlockSpec((TM, D), lambda i: (i, 0)), pl.BlockSpec((1, D), lambda i: (0, 0)), _seg_spec(nct)],
        out_specs=pl.BlockSpec((TM, D), lambda i: (i, 0)),
        out_shape=jax.ShapeDtypeStruct((t, D), BF16), compiler_params=_cparams(1, 32),
    )(x, g_pre, mod3)


def _prenorm_bwd(dh, x, dxo, g_pre, mod3, nct, name):
    t = x.shape[0]

    def body(dh_ref, x_ref, dxo_ref, g_ref, mod_ref, dx_ref, dsh_ref, dsc_ref, dg_ref):
        i = pl.program_id(0)
        xv, dhv, g = x_ref[...], dh_ref[...], g_ref[...]
        r = lax.rsqrt(jnp.mean(xv * xv, axis=-1, keepdims=True) + EPS)
        xh = xv * r
        dy = dhv * (1.0 + mod_ref[1:2, :])
        dxh = dy * g
        dx_ref[...] = dxo_ref[...] + r * (dxh - xh * jnp.mean(dxh * xh, axis=-1, keepdims=True))

        @pl.when((i == 0) | (i == nct))
        def _():
            dsh_ref[...] = jnp.zeros_like(dsh_ref)
            dsc_ref[...] = jnp.zeros_like(dsc_ref)

        @pl.when(i == 0)
        def _():
            dg_ref[...] = jnp.zeros_like(dg_ref)

        dsh_ref[...] += jnp.sum(dhv, axis=0, keepdims=True)
        dsc_ref[...] += jnp.sum(dhv * (xh * g), axis=0, keepdims=True)
        dg_ref[...] += jnp.sum(dy * xh, axis=0, keepdims=True)

    row = pl.BlockSpec((TM, D), lambda i: (i, 0))
    seg8 = pl.BlockSpec((None, 8, D), lambda i: (jnp.where(i >= nct, 1, 0), 0, 0))
    return pl.pallas_call(
        body, name=name, grid=(t // TM,),
        in_specs=[row, row, row, pl.BlockSpec((1, D), lambda i: (0, 0)), _seg_spec(nct)],
        out_specs=(row, seg8, seg8, pl.BlockSpec((8, D), lambda i: (0, 0))),
        out_shape=(jax.ShapeDtypeStruct((t, D), F32), jax.ShapeDtypeStruct((2, 8, D), F32),
                   jax.ShapeDtypeStruct((2, 8, D), F32), jax.ShapeDtypeStruct((8, D), F32)),
        compiler_params=_cparams(1, 32),
    )(dh, x, dxo, g_pre, mod3)


def _post_fwd(x, out, g_post, mod3, nct, name):
    t = x.shape[0]

    def body(x_ref, o_ref, g_ref, mod_ref, y_ref):
        ov = o_ref[...]
        r = lax.rsqrt(jnp.mean(ov * ov, axis=-1, keepdims=True) + EPS)
        y_ref[...] = x_ref[...] + mod_ref[2:3, :] * (ov * r * g_ref[...])

    row = pl.BlockSpec((TM, D), lambda i: (i, 0))
    return pl.pallas_call(
        body, name=name, grid=(t // TM,),
        in_specs=[row, row, pl.BlockSpec((1, D), lambda i: (0, 0)), _seg_spec(nct)],
        out_specs=row, out_shape=jax.ShapeDtypeStruct((t, D), F32), compiler_params=_cparams(1, 32),
    )(x, out, g_post, mod3)


def _post_bwd(dxo, out, g_post, mod3, nct, name):
    t = out.shape[0]

    def body(dx_ref, o_ref, g_ref, mod_ref, do_ref, dgt_ref, dg_ref):
        i = pl.program_id(0)
        ov, dxv, g = o_ref[...], dx_ref[...], g_ref[...]
        r = lax.rsqrt(jnp.mean(ov * ov, axis=-1, keepdims=True) + EPS)
        nh = ov * r
        dn = dxv * mod_ref[2:3, :]
        dnh = dn * g
        do_ref[...] = (r * (dnh - nh * jnp.mean(dnh * nh, axis=-1, keepdims=True))).astype(BF16)

        @pl.when((i == 0) | (i == nct))
        def _():
            dgt_ref[...] = jnp.zeros_like(dgt_ref)

        @pl.when(i == 0)
        def _():
            dg_ref[...] = jnp.zeros_like(dg_ref)

        dgt_ref[...] += jnp.sum(dxv * (nh * g), axis=0, keepdims=True)
        dg_ref[...] += jnp.sum(dn * nh, axis=0, keepdims=True)

    row = pl.BlockSpec((TM, D), lambda i: (i, 0))
    seg8 = pl.BlockSpec((None, 8, D), lambda i: (jnp.where(i >= nct, 1, 0), 0, 0))
    return pl.pallas_call(
        body, name=name, grid=(t // TM,),
        in_specs=[row, row, pl.BlockSpec((1, D), lambda i: (0, 0)), _seg_spec(nct)],
        out_specs=(row, seg8, pl.BlockSpec((8, D), lambda i: (0, 0))),
        out_shape=(jax.ShapeDtypeStruct((t, D), BF16), jax.ShapeDtypeStruct((2, 8, D), F32),
                   jax.ShapeDtypeStruct((8, D), F32)),
        compiler_params=_cparams(1, 32),
    )(dxo, out, g_post, mod3)


def _loss_grad(y, target, nct, name):
    t = y.shape[0]

    def body(y_ref, t_ref, dy_ref, l_ref):
        i = pl.program_id(0)

        @pl.when(i == 0)
        def _():
            l_ref[...] = jnp.zeros_like(l_ref)

        @pl.when(i < nct)
        def _():
            dy_ref[...] = jnp.zeros_like(dy_ref)

        @pl.when(i >= nct)
        def _():
            err = y_ref[...] - t_ref[...]
            dy_ref[...] = err / D
            l_ref[...] += jnp.sum(jnp.sum(err * err, axis=1, keepdims=True), axis=0, keepdims=True)

    row = pl.BlockSpec((TM, D), lambda i: (i, 0))
    return pl.pallas_call(
        body, name=name, grid=(t // TM,),
        in_specs=[row, pl.BlockSpec((TM, D), lambda i: (jnp.maximum(i - nct, 0), 0))],
        out_specs=(row, pl.BlockSpec((8, LANE), lambda i: (0, 0))),
        out_shape=(jax.ShapeDtypeStruct((t, D), F32), jax.ShapeDtypeStruct((8, LANE), F32)),
        compiler_params=_cparams(1, 32),
    )(y, target)


def _pcol(name, width, rows=TM):
    assert OFF[name] % width == 0
    blk = OFF[name] // width
    return pl.BlockSpec((rows, width), lambda i: (i, blk))


def _big_rows(t):
    return max(r for r in range(16, BIG_ROWS + 1, 16) if t % r == 0)


def _shift_rows(u, prev_row, next_row):
    n = u.shape[0]
    row = lax.broadcasted_iota(jnp.int32, u.shape, 0)
    prev = jnp.where(row == 0, prev_row, pltpu.roll(u, 1, 0))
    nxt = jnp.where(row == n - 1, next_row, pltpu.roll(u, n - 1, 0))
    return prev, nxt


def _halo_specs(width, nt, blk=0, rows=8):
    per = TM // rows
    prev = pl.BlockSpec((rows, width), lambda i: (jnp.maximum(i * per - 1, 0), blk))
    nxt = pl.BlockSpec((rows, width), lambda i: (jnp.minimum((i + 1) * per, nt * per - 1), blk))
    return prev, nxt


def _conv_fwd(p, conv_w8, nct, name):
    t = p.shape[0]
    nt = t // TM

    def body(ab_ref, ac_ref, ax_ref, az_ref, acp_ref, axp_ref, acn_ref, axn_ref, w_ref, cv_ref, ya_ref):
        i = pl.program_id(0)
        def f(ref, rows=slice(None)):
            return ref[rows, :].astype(F32)

        u = f(ac_ref) * f(ax_ref)
        mp = jnp.where((i == 0) | (i == nct), 0.0, 1.0)
        mn = jnp.where((i == nct - 1) | (i == nt - 1), 0.0, 1.0)
        last, first = slice(P_HALO - 1, P_HALO), slice(0, 1)
        prev, nxt = _shift_rows(u, f(acp_ref, last) * f(axp_ref, last) * mp, f(acn_ref, first) * f(axn_ref, first) * mn)
        cv = w_ref[0:1, :] * prev + w_ref[1:2, :] * u + w_ref[2:3, :] * nxt
        az = f(az_ref)
        cv_ref[...] = cv.astype(BF16)
        ya_ref[...] = (f(ab_ref) * cv * (az * _sigmoid(az))).astype(BF16)

    acp, acn = _halo_specs(D, nt, OFF["a_c"] // D, P_HALO)
    axp, axn = _halo_specs(D, nt, OFF["a_x"] // D, P_HALO)
    row = pl.BlockSpec((TM, D), lambda i: (i, 0))
    return pl.pallas_call(
        body, name=name, grid=(nt,),
        in_specs=[_pcol("a_b", D), _pcol("a_c", D), _pcol("a_x", D), _pcol("a_z", D), acp, axp, acn, axn,
                  pl.BlockSpec((8, D), lambda i: (0, 0))],
        out_specs=(row, row),
        out_shape=(jax.ShapeDtypeStruct((t, D), BF16), jax.ShapeDtypeStruct((t, D), BF16)),
        compiler_params=_cparams(1, 40),
    )(p, p, p, p, p, p, p, p, conv_w8)


def _dp_spec(key, rows=TM):
    seg, width = DP_BLOCKS[key]
    assert OFF[seg] % width == 0
    blk = OFF[seg] // width
    return pl.BlockSpec((rows, width), lambda i: (i, blk))


def _conv_bwd_a(dya, p, cv, dp, name):
    t = p.shape[0]

    def body(dy_ref, ab_ref, az_ref, cv_ref, _, dcv_ref, dp_ref):
        dy, ab = dy_ref[...].astype(F32), ab_ref[...].astype(F32)
        az, c = az_ref[...].astype(F32), cv_ref[...].astype(F32)
        sg = _sigmoid(az)
        sz = az * sg
        dcv_ref[...] = dy * ab * sz
        dp_ref[:, 0:D] = (dy * c * sz).astype(BF16)
        dp_ref[:, D:2 * D] = (dy * ab * c * (sg * (1.0 + az * (1.0 - sg)))).astype(BF16)

    rt = _big_rows(t)
    row = pl.BlockSpec((rt, D), lambda i: (i, 0))
    return pl.pallas_call(
        body, name=name, grid=(t // rt,),
        in_specs=[row, _pcol("a_b", D, rt), _pcol("a_z", D, rt), row, _HBM], out_specs=(row, _dp_spec("conv_a", rt)),
        out_shape=(jax.ShapeDtypeStruct((t, D), F32), jax.ShapeDtypeStruct(dp.shape, dp.dtype)),
        input_output_aliases={4: 1}, compiler_params=_cparams(1, 40),
    )(dya, p, p, cv, dp)


def _conv_bwd_b(dcv, p, conv_w8, nct, dp, name):
    t = p.shape[0]
    nt = t // TM

    def body(dcv_ref, hp_ref, hn_ref, ac_ref, ax_ref, w_ref, _, dp_ref, dw_ref):
        i = pl.program_id(0)
        d, ac, ax = dcv_ref[...], ac_ref[...].astype(F32), ax_ref[...].astype(F32)
        u = ac * ax
        mp = jnp.where((i == 0) | (i == nct), 0.0, 1.0)
        mn = jnp.where((i == nct - 1) | (i == nt - 1), 0.0, 1.0)
        dprev, dnxt = _shift_rows(d, hp_ref[7:8, :] * mp, hn_ref[0:1, :] * mn)
        du = w_ref[0:1, :] * dnxt + w_ref[1:2, :] * d + w_ref[2:3, :] * dprev
        dp_ref[:, 0:D] = (du * ax).astype(BF16)
        dp_ref[:, D:2 * D] = (du * ac).astype(BF16)

        @pl.when(i == 0)
        def _():
            dw_ref[...] = jnp.zeros_like(dw_ref)

        dw0 = jnp.sum(u * dnxt, axis=0, keepdims=True)
        dw1 = jnp.sum(u * d, axis=0, keepdims=True)
        dw2 = jnp.sum(u * dprev, axis=0, keepdims=True)
        r8 = lax.broadcasted_iota(jnp.int32, (8, D), 0)
        dw_ref[...] += jnp.where(r8 == 0, dw0, jnp.where(r8 == 1, dw1, jnp.where(r8 == 2, dw2, 0.0)))

    hp, hn = _halo_specs(D, nt)
    row = pl.BlockSpec((TM, D), lambda i: (i, 0))
    return pl.pallas_call(
        body, name=name, grid=(nt,),
        in_specs=[row, hp, hn, _pcol("a_c", D), _pcol("a_x", D), pl.BlockSpec((8, D), lambda i: (0, 0)), _HBM],
        out_specs=(_dp_spec("conv_b"), pl.BlockSpec((8, D), lambda i: (0, 0))),
        out_shape=(jax.ShapeDtypeStruct(dp.shape, dp.dtype), jax.ShapeDtypeStruct((8, D), F32)),
        input_output_aliases={6: 0}, compiler_params=_cparams(1, 40),
    )(dcv, dcv, dcv, p, p, conv_w8, dp)


def _rot_half(x):
    lane = lax.broadcasted_iota(jnp.int32, x.shape, 1)
    return jnp.where((lane % 64) < 32, pltpu.roll(x, 96, 1), pltpu.roll(x, 32, 1))


def _qk_prep_fwd(p, qg, kg, cos_t, sin_t, name):
    t = p.shape[0]

    def body(q_ref, k_ref, qg_ref, kg_ref, c_ref, s_ref, qo_ref, ko_ref):
        c, s = c_ref[...], s_ref[...]

        def one(xv, g, scale):
            y = xv * lax.rsqrt(jnp.mean(xv * xv, axis=-1, keepdims=True) + EPS) * g
            return ((y * c + _rot_half(y) * s) * scale).astype(BF16)

        for h in range(NH):
            qo_ref[:, h * HD:(h + 1) * HD] = one(q_ref[:, h * HD:(h + 1) * HD].astype(F32), qg_ref[...], Q_FOLD)
        for h in range(NKV):
            ko_ref[:, h * HD:(h + 1) * HD] = one(k_ref[:, h * HD:(h + 1) * HD].astype(F32), kg_ref[...], 1.0)

    vec = pl.BlockSpec((1, HD), lambda i: (0, 0))
    rt = _big_rows(t)
    tab = pl.BlockSpec((rt, HD), lambda i: (i, 0))
    return pl.pallas_call(
        body, name=name, grid=(t // rt,),
        in_specs=[_pcol("q", NH * HD, rt), _pcol("k", NKV * HD, rt), vec, vec, tab, tab],
        out_specs=(pl.BlockSpec((rt, NH * HD), lambda i: (i, 0)), pl.BlockSpec((rt, NKV * HD), lambda i: (i, 0))),
        out_shape=(jax.ShapeDtypeStruct((t, NH * HD), BF16), jax.ShapeDtypeStruct((t, NKV * HD), BF16)),
        compiler_params=_cparams(1, 32),
    )(p, p, qg, kg, cos_t, sin_t)


def _qk_prep_bwd(dqr, dkr, p, qg, kg, cos_t, sin_t, dp, name):
    t = p.shape[0]

    def body(dq_ref, dk_ref, q_ref, k_ref, qg_ref, kg_ref, c_ref, s_ref, _, dqo_ref, dko_ref, dqg_ref, dkg_ref):
        i = pl.program_id(0)
        c, s = c_ref[...], s_ref[...]

        @pl.when(i == 0)
        def _():
            dqg_ref[...] = jnp.zeros_like(dqg_ref)
            dkg_ref[...] = jnp.zeros_like(dkg_ref)

        def one(dyr, xv, g):
            dy = dyr * c + _rot_half(dyr * s)
            r = lax.rsqrt(jnp.mean(xv * xv, axis=-1, keepdims=True) + EPS)
            xh = xv * r
            dxh = dy * g
            dx = r * (dxh - xh * jnp.mean(dxh * xh, axis=-1, keepdims=True))
            return dx.astype(BF16), jnp.sum(dy * xh, axis=0, keepdims=True)

        for h in range(NH):
            sl = slice(h * HD, (h + 1) * HD)
            dx, dg = one(dq_ref[:, sl] * ATTN_SCALE, q_ref[:, sl].astype(F32), qg_ref[...])
            dqo_ref[:, sl] = dx
            dqg_ref[...] += dg
        for h in range(NKV):
            sl = slice(h * HD, (h + 1) * HD)
            dx, dg = one(dk_ref[:, sl] * (ATTN_SCALE / Q_FOLD), k_ref[:, sl].astype(F32), kg_ref[...])
            dko_ref[:, sl] = dx
            dkg_ref[...] += dg

    vec = pl.BlockSpec((1, HD), lambda i: (0, 0))
    rt = _big_rows(t)
    tab = pl.BlockSpec((rt, HD), lambda i: (i, 0))
    acc = pl.BlockSpec((8, HD), lambda i: (0, 0))
    qrow = pl.BlockSpec((rt, NH * HD), lambda i: (i, 0))
    krow = pl.BlockSpec((rt, NKV * HD), lambda i: (i, 0))
    return pl.pallas_call(
        body, name=name, grid=(t // rt,),
        in_specs=[qrow, krow, _pcol("q", NH * HD, rt), _pcol("k", NKV * HD, rt), vec, vec, tab, tab, _HBM],
        out_specs=(_dp_spec("q", rt), krow, acc, acc),
        out_shape=(jax.ShapeDtypeStruct(dp.shape, dp.dtype), jax.ShapeDtypeStruct((t, NKV * HD), BF16),
                   jax.ShapeDtypeStruct((8, HD), F32), jax.ShapeDtypeStruct((8, HD), F32)),
        input_output_aliases={8: 0}, compiler_params=_cparams(1, 32),
    )(dqr, dkr, p, p, qg, kg, cos_t, sin_t, dp)


def _key_chunks(n, limit):
    c = max(c for c in range(KEY_ALIGN, min(n, limit) + 1, KEY_ALIGN) if n % c == 0)
    return [(lo, lo + c) for lo in range(0, n, c)]


def _attn_fwd(qr, kr, p, nct, name, rider=None):
    t = qr.shape[0]
    nt = t // TM
    ctx = nct * TM
    vblk = OFF["v"] // HD
    hps = ATTN_HEADS_PER_STEP
    nhp, per_kv = NH // hps, GROUP // hps

    def body(q_ref, k_ref, v_ref, o_ref, lse_ref):
        def tile(nkeys):
            sls = [slice(j * HD, (j + 1) * HD) for j in range(hps)]
            qs = [q_ref[:, sl] for sl in sls]
            m = l = acc = None
            for lo, hi in _key_chunks(nkeys, ATTN_FWD_KEY_CHUNK):
                k, vb = k_ref[lo:hi, :], v_ref[lo:hi, :].astype(BF16)
                ss = [lax.dot_general(q, k, _NT, preferred_element_type=F32) for q in qs]
                mcs = [jnp.max(s, axis=-1, keepdims=True) for s in ss]
                m_new = mcs if m is None else [jnp.maximum(a, b) for a, b in zip(m, mcs)]
                es = [jnp.exp2(s - mn) for s, mn in zip(ss, m_new)]
                lcs = [jnp.sum(e, axis=-1, keepdims=True) for e in es]
                pvs = [jnp.dot(e.astype(BF16), vb, preferred_element_type=F32) for e in es]
                if m is None:
                    l, acc = lcs, pvs
                else:
                    alphas = [jnp.exp2(a - b) for a, b in zip(m, m_new)]
                    l = [x * al + y for x, al, y in zip(l, alphas, lcs)]
                    acc = [x * al + y for x, al, y in zip(acc, alphas, pvs)]
                m = m_new
            for j, sl in enumerate(sls):
                o_ref[:, sl] = (acc[j] / l[j]).astype(BF16)
                lse_ref[:, j:j + 1] = m[j] + jnp.log2(l[j])

        pl.when(pl.program_id(1) < nct)(lambda: tile(ctx))
        pl.when(pl.program_id(1) >= nct)(lambda: tile(t))

    def at(h, i):
        return lambda: (pl.program_id(0) == h) & (pl.program_id(1) == i)

    rn = 0 if rider is None else rider.n
    qspec = pl.BlockSpec((TM, hps * HD), lambda h, i: (i, h))
    return pl.pallas_call(
        _with_rider(body, 3, 2, rider, at(0, 0), at(*divmod(nhp * nt * 7 // 8, nt)), at(nhp - 1, nt - 1)),
        name=name, grid=(nhp, nt),
        in_specs=[qspec, pl.BlockSpec((t, HD), lambda h, i: (0, h // per_kv)),
                  pl.BlockSpec((t, HD), lambda h, i: (0, vblk + h // per_kv))] + [_HBM] * rn,
        out_specs=(qspec, pl.BlockSpec((None, TM, hps), lambda h, i: (h, i, 0))) + (_HBM,) * rn,
        out_shape=(jax.ShapeDtypeStruct((t, NH * HD), BF16), jax.ShapeDtypeStruct((nhp, t, hps), F32))
        + (() if rider is None else tuple(rider.out_shape)),
        scratch_shapes=[] if rider is None else rider.scratch,
        compiler_params=_cparams(2, 48),
    )(qr, kr, p, *(() if rider is None else rider.xs))


def _attn_bwd(qr, kr, p, o, lse, do, nct, name, rider=None):
    t = qr.shape[0]
    nt = t // TM
    ctx = nct * TM
    vblk = OFF["v"] // HD
    hps = ATTN_HEADS_PER_STEP

    def body(q_ref, k_ref, v_ref, o_ref, lse_ref, do_ref, dq_ref, dk_ref, dv_ref):
        g, i = pl.program_id(1), pl.program_id(2)

        @pl.when((g == 0) & (i == 0))
        def _():
            dk_ref[...] = jnp.zeros_like(dk_ref)
            dv_ref[...] = jnp.zeros_like(dv_ref)

        def tile(nkeys):
            heads = []
            for j in range(hps):
                sl = slice(j * HD, (j + 1) * HD)
                dob = do_ref[:, sl]
                drow = jnp.sum(dob.astype(F32) * o_ref[:, sl].astype(F32), axis=-1, keepdims=True)
                heads.append((sl, q_ref[:, sl], dob, drow, lse_ref[:, j:j + 1]))
            dq = [None] * hps
            for lo, hi in _key_chunks(nkeys, ATTN_BWD_KEY_CHUNK):
                k = k_ref[lo:hi, :]
                vb = v_ref[lo:hi, :].astype(BF16)
                ss = [lax.dot_general(q, k, _NT, preferred_element_type=F32) for _, q, _, _, _ in heads]
                dps = [lax.dot_general(dob, vb, _NT, preferred_element_type=F32) for _, _, dob, _, _ in heads]
                prs = [jnp.exp2(s - h[4]) for s, h in zip(ss, heads)]
                dss = [(pr * (dp - h[3])).astype(BF16) for pr, dp, h in zip(prs, dps, heads)]
                pbs = [pr.astype(BF16) for pr in prs]
                dqs = [jnp.dot(ds, k, preferred_element_type=F32) for ds in dss]
                dks = [lax.dot_general(ds, h[1], _TN, preferred_element_type=F32) for ds, h in zip(dss, heads)]
                dvs = [lax.dot_general(pb, h[2], _TN, preferred_element_type=F32) for pb, h in zip(pbs, heads)]
                dq = [x if y is None else y + x for x, y in zip(dqs, dq)]
                dk_ref[lo:hi, :] += functools.reduce(lambda a, b: a + b, dks)
                dv_ref[lo:hi, :] += functools.reduce(lambda a, b: a + b, dvs)
            for j, (sl, *_) in enumerate(heads):
                dq_ref[:, sl] = dq[j]

        pl.when(i < nct)(lambda: tile(ctx))
        pl.when(i >= nct)(lambda: tile(t))

    def at(kv, g, i):
        return lambda: (pl.program_id(0) == kv) & (pl.program_id(1) == g) & (pl.program_id(2) == i)

    rn = 0 if rider is None else rider.n
    per_kv = GROUP // hps
    qspec = pl.BlockSpec((TM, hps * HD), lambda kv, g, i: (i, kv * per_kv + g))
    kvspec = pl.BlockSpec((t, HD), lambda kv, g, i: (0, kv))
    lspec = pl.BlockSpec((None, TM, hps), lambda kv, g, i: (kv * per_kv + g, i, 0))
    return pl.pallas_call(
        _with_rider(body, 6, 3, rider, at(0, 0, 0), at(NKV - 1, 0, 0), at(NKV - 1, per_kv - 1, nt - 1)),
        name=name, grid=(NKV, per_kv, nt),
        in_specs=[qspec, kvspec, pl.BlockSpec((t, HD), lambda kv, g, i: (0, vblk + kv)), qspec, lspec, qspec]
        + [_HBM] * rn,
        out_specs=(qspec, kvspec, kvspec) + (_HBM,) * rn,
        out_shape=(jax.ShapeDtypeStruct((t, NH * HD), F32), jax.ShapeDtypeStruct((t, NKV * HD), F32),
                   jax.ShapeDtypeStruct((t, NKV * HD), F32)) + (() if rider is None else tuple(rider.out_shape)),
        scratch_shapes=[] if rider is None else rider.scratch,
        compiler_params=_cparams(3, 48),
    )(qr, kr, p, o, lse, do, *(() if rider is None else rider.xs))


def _decay_fwd(p, wd, bd, name):
    t = p.shape[0]

    def body(r_ref, w_ref, b_ref, z_ref, bc_ref):
        z = jnp.dot(r_ref[...].astype(BF16), w_ref[...].astype(BF16), preferred_element_type=F32) + b_ref[...]
        z_ref[...] = z
        la = (jnp.minimum(z, 0.0) - jnp.log(1.0 + jnp.exp(-jnp.abs(z)))) / GLA_TAU
        half = GH * GDK
        bc_ref[:, 0:half] = _chunk_sums(la[:, 0:half], False)
        bc_ref[:, half:] = _chunk_sums(la[:, half:], True)

    row = pl.BlockSpec((TM, D), lambda i: (i, 0))
    return pl.pallas_call(
        body, name=name, grid=(t // TM,),
        in_specs=[_pcol("r", R_PAD), pl.BlockSpec((R_PAD, D), lambda i: (0, 0)), pl.BlockSpec((1, D), lambda i: (0, 0))],
        out_specs=(row, row),
        out_shape=(jax.ShapeDtypeStruct((t, D), F32), jax.ShapeDtypeStruct((t, D), F32)),
        compiler_params=_cparams(1, 32),
    )(p, wd, bd)


def _chunk_order(s, ncc, nc, rev):
    if not rev:
        return s
    return jnp.where(s < ncc, ncc - 1 - s, nc - 1 - (s - ncc))


GLA_CPS = TM // CH


class _Chain:
    def __init__(self, rev, d, h, sub, refs):
        self.rev, self.d, self.h, self.sub, self.refs = rev, d, h, sub, refs
        self.rows, self.k, self.v = slice(sub * CH, (sub + 1) * CH), _hk(h), _hv(h)
        self.last = sub * CH + (0 if rev else CH - 1)


def _gla_chains(dirs, step, backward):
    return [_Chain(rev, d, h, step if rev == backward else GLA_CPS - 1 - step, refs)
            for d, (rev, refs) in enumerate(dirs) for h in range(GH)]


def _hk(h):
    return slice(h * GDK, (h + 1) * GDK)


def _hv(h):
    return slice(h * GDV, (h + 1) * GDV)


def _chunk_sums(x, from_end):
    r = lax.broadcasted_iota(jnp.int32, (CH, CH), 0)
    c = lax.broadcasted_iota(jnp.int32, (CH, CH), 1)
    tri = ((c >= r) if from_end else (c <= r)).astype(F32)
    return jnp.concatenate([jnp.dot(tri, x[lo:lo + CH], preferred_element_type=F32, precision=HIGHEST)
                            for lo in range(0, x.shape[0], CH)], axis=0)


def _gla_factors(qs, ks, bcs, bls, revs):
    r = lax.broadcasted_iota(jnp.int32, (CH, CH), 0)
    c = lax.broadcasted_iota(jnp.int32, (CH, CH), 1)
    keeps = [(c >= r) if rev else (c <= r) for rev in revs]
    qs, ks = [q.astype(F32) for q in qs], [k.astype(F32) for k in ks]
    qts = [q * GLA_SCALE * jnp.exp(bc) for q, bc in zip(qs, bcs)]
    kts = [k * jnp.exp(-bc) for k, bc in zip(ks, bcs)]
    khs = [k * jnp.exp(bl - bc) for k, bl, bc in zip(ks, bls, bcs)]
    gls = [jnp.exp(bl) for bl in bls]
    return qts, kts, gls, khs, keeps


def _gla_loads(ch):
    qs = [c.refs[0][c.rows, c.k] for c in ch]
    ks = [c.refs[1][c.rows, c.k] for c in ch]
    bcs = [c.refs[3][c.rows, c.k] for c in ch]
    bls = [c.refs[3][c.last:c.last + 1, c.k] for c in ch]
    return qs, ks, bcs, bls


_NT = (((1,), (1,)), ((), ()))
_TN = (((0,), (0,)), ((), ()))


def _gla_specs(ncs, ns, rev, backward):
    def idx(s):
        return _chunk_order((ns - 1 - s) if backward else s, ncs, ns, rev)

    wk, wv = GH * GDK, GH * GDV
    qb, kb, vb = OFF["gq"] // wk, OFF["gk"] // wk, OFF["gv"] // wv
    lab = 1 if rev else 0
    q = pl.BlockSpec((TM, wk), lambda s: (idx(s), qb))
    k = pl.BlockSpec((TM, wk), lambda s: (idx(s), kb))
    v = pl.BlockSpec((TM, wv), lambda s: (idx(s), vb))
    la = pl.BlockSpec((TM, wk), lambda s: (idx(s), lab))
    o = pl.BlockSpec((TM, wv), lambda s: (idx(s), 0))
    dk = pl.BlockSpec((TM, wk), lambda s: (idx(s), 0))
    st = pl.BlockSpec((GLA_CPS, GH, GDV, GDK), lambda s: (idx(s), 0, 0, 0))
    return q, k, v, la, o, dk, st


def _gla_fwd(p, la, ncs, name):
    t = p.shape[0]
    nc, ns = t // CH, t // TM
    specs = [_gla_specs(ncs, ns, rev, False) for rev in (False, True)]

    def body(qf, kf, vf, laf, qb_, kb_, vb_, lab, of, stf, ob, stb, s_scr):
        @pl.when(pl.program_id(0) == 0)
        def _():
            s_scr[...] = jnp.zeros_like(s_scr)

        dirs = ((False, (qf, kf, vf, laf, of, stf)), (True, (qb_, kb_, vb_, lab, ob, stb)))
        for step in range(GLA_CPS):
            ch = _gla_chains(dirs, step, False)
            qts, kts, gls, khs, keeps = _gla_factors(*_gla_loads(ch), [c.rev for c in ch])
            sts = [s_scr[c.d, c.h] for c in ch]
            for c, st in zip(ch, sts):
                c.refs[5][c.sub, c.h] = st.astype(BF16)
            vbs = [c.refs[2][c.rows, c.v].astype(BF16) for c in ch]
            qbs = [qt.astype(BF16) for qt in qts]
            a_s = [jnp.where(keep, lax.dot_general(qb, kt.astype(BF16), _NT, preferred_element_type=F32), 0.0)
                   for keep, qb, kt in zip(keeps, qbs, kts)]
            inter = [lax.dot_general(qb, st.astype(BF16), _NT, preferred_element_type=F32) for qb, st in zip(qbs, sts)]
            intra = [jnp.dot(a.astype(BF16), vb, preferred_element_type=F32) for a, vb in zip(a_s, vbs)]
            for c, x, y in zip(ch, inter, intra):
                c.refs[4][c.rows, c.v] = (x + y).astype(BF16)
            upd = [lax.dot_general(vb, kh.astype(BF16), _TN, preferred_element_type=F32) for vb, kh in zip(vbs, khs)]
            for c, st, gl, u in zip(ch, sts, gls, upd):
                s_scr[c.d, c.h] = st * gl + u

    o_shape = jax.ShapeDtypeStruct((t, GH * GDV), BF16)
    st_shape = jax.ShapeDtypeStruct((nc, GH, GDV, GDK), BF16)
    return pl.pallas_call(
        body, name=name, grid=(ns,),
        in_specs=[sp for s_ in specs for sp in s_[:4]],
        out_specs=tuple(sp for s_ in specs for sp in (s_[4], s_[6])),
        out_shape=(o_shape, st_shape, o_shape, st_shape),
        scratch_shapes=[pltpu.VMEM((2, GH, GDV, GDK), F32)], compiler_params=_cparams(1, 32),
    )(p, p, p, la, p, p, p, la)


def _gla_bwd(p, la, do, stf, stb, ncs, name):
    t = p.shape[0]
    ns = t // TM
    specs = [_gla_specs(ncs, ns, rev, True) for rev in (False, True)]

    def mm(xs, ys, dims=None):
        if dims is None:
            return [jnp.dot(x, y, preferred_element_type=F32) for x, y in zip(xs, ys)]
        return [lax.dot_general(x, y, dims, preferred_element_type=F32) for x, y in zip(xs, ys)]

    def body(*refs):
        ins_f, ins_b, outs_f, outs_b, ds_scr = refs[0:6], refs[6:12], refs[12:16], refs[16:20], refs[20]

        @pl.when(pl.program_id(0) == 0)
        def _():
            ds_scr[...] = jnp.zeros_like(ds_scr)

        dirs = ((False, (*ins_f, *outs_f)), (True, (*ins_b, *outs_b)))
        row = lax.broadcasted_iota(jnp.int32, (CH, GDK), 0)
        for step in range(GLA_CPS):
            ch = _gla_chains(dirs, step, True)
            revs = [c.rev for c in ch]
            loads = _gla_loads(ch)
            bcs = loads[2]
            qts, kts, gls, khs, keeps = _gla_factors(*loads, revs)
            stvs = [c.refs[5][c.sub, c.h] for c in ch]
            dsns = [ds_scr[c.d, c.h] for c in ch]
            dsbs = [x.astype(BF16) for x in dsns]
            vbs = [c.refs[2][c.rows, c.v].astype(BF16) for c in ch]
            dobs = [c.refs[4][c.rows, c.v].astype(BF16) for c in ch]
            qbs, kbs = [x.astype(BF16) for x in qts], [x.astype(BF16) for x in kts]
            a_s = [jnp.where(keep, x, 0.0).astype(BF16) for keep, x in zip(keeps, mm(qbs, kbs, _NT))]
            das = [jnp.where(keep, x, 0.0).astype(BF16) for keep, x in zip(keeps, mm(dobs, vbs, _NT))]
            dqts = [x + y for x, y in zip(mm(dobs, stvs), mm(das, kbs))]
            dkhs = mm(vbs, dsbs)
            dkts = [x + dkh * gl for x, dkh, gl in zip(mm(das, qbs, _TN), dkhs, gls)]
            for c, x, y in zip(ch, mm(a_s, dobs, _TN), mm([kh.astype(BF16) for kh in khs], dsbs, _NT)):
                c.refs[8][c.rows, c.v] = (x + y).astype(BF16)
            for c, x, dsn, gl in zip(ch, mm(dobs, qbs, _TN), dsns, gls):
                ds_scr[c.d, c.h] = x + dsn * gl
            dgls = [jnp.sum(st.astype(F32) * dsn, axis=0, keepdims=True) + jnp.sum(dkh * kt, axis=0, keepdims=True)
                    for st, dsn, dkh, kt in zip(stvs, dsns, dkhs, kts)]
            dbcs = [dqt * qt - dkt * kt + jnp.where(row == (0 if rev else CH - 1), dgl * gl, 0.0)
                    for rev, dqt, qt, dkt, kt, dgl, gl in zip(revs, dqts, qts, dkts, kts, dgls, gls)]
            for c, dbc, dqt, dkt, bc in zip(ch, dbcs, dqts, dkts, bcs):
                c.refs[9][c.rows, c.k] = dbc
                c.refs[6][c.rows, c.k] = (dqt * (GLA_SCALE * jnp.exp(bc))).astype(BF16)
                c.refs[7][c.rows, c.k] = (dkt * jnp.exp(-bc)).astype(BF16)

    k_shape = jax.ShapeDtypeStruct((t, GH * GDK), BF16)
    v_shape = jax.ShapeDtypeStruct((t, GH * GDV), BF16)
    c_shape = jax.ShapeDtypeStruct((t, GH * GDK), F32)
    res = pl.pallas_call(
        body, name=name, grid=(ns,),
        in_specs=[sp for q_s, k_s, v_s, la_s, o_s, _, st_s in specs for sp in (q_s, k_s, v_s, la_s, o_s, st_s)],
        out_specs=tuple(sp for _, _, _, _, o_s, dk_s, _ in specs for sp in (dk_s, dk_s, o_s, dk_s)),
        out_shape=(k_shape, k_shape, v_shape, c_shape) * 2,
        scratch_shapes=[pltpu.VMEM((2, GH, GDV, GDK), F32)], compiler_params=_cparams(1, 32),
    )(p, p, p, la, do, stf, p, p, p, la, do, stb)
    return res[:4], res[4:]


def _gla_merge_bwd(gf, gb, z, p, wd, dp, name):
    t = p.shape[0]
    w2 = GH * GDK

    def body(dqf, dkf, dvf, dlf, dqb, dkb, dvb, dlb, z_ref, r_ref, w_ref, _, dp_ref, dr_ref, db_ref, dw_ref):
        i = pl.program_id(0)
        dp_ref[:, 0:D] = (dvf[...].astype(F32) + dvb[...].astype(F32)).astype(BF16)
        dp_ref[:, D:D + w2] = (dqf[...].astype(F32) + dqb[...].astype(F32)).astype(BF16)
        dp_ref[:, D + w2:D + 2 * w2] = (dkf[...].astype(F32) + dkb[...].astype(F32)).astype(BF16)
        zv = z_ref[...]
        dlf_, dlb_ = _chunk_sums(dlf[...], True), _chunk_sums(dlb[...], False)
        dz = jnp.concatenate([dlf_, dlb_], axis=1) * (_sigmoid(-zv) / GLA_TAU)
        dzb = dz.astype(BF16)
        dr_ref[...] = lax.dot_general(dzb, w_ref[...].astype(BF16), _NT, preferred_element_type=F32).astype(BF16)

        @pl.when(i == 0)
        def _():
            db_ref[...] = jnp.zeros_like(db_ref)
            dw_ref[...] = jnp.zeros_like(dw_ref)

        db_ref[...] += jnp.sum(dz, axis=0, keepdims=True)
        dw_ref[...] += lax.dot_general(r_ref[...].astype(BF16), dzb, _TN, preferred_element_type=F32)

    half = pl.BlockSpec((TM, w2), lambda i: (i, 0))
    row = pl.BlockSpec((TM, D), lambda i: (i, 0))
    wspec = pl.BlockSpec((R_PAD, D), lambda i: (0, 0))
    return pl.pallas_call(
        body, name=name, grid=(t // TM,),
        in_specs=[half, half, row, half, half, half, row, half, row, _pcol("r", R_PAD), wspec, _HBM],
        out_specs=(_dp_spec("gla"), pl.BlockSpec((TM, R_PAD), lambda i: (i, 0)),
                   pl.BlockSpec((8, D), lambda i: (0, 0)), wspec),
        out_shape=(jax.ShapeDtypeStruct(dp.shape, dp.dtype), jax.ShapeDtypeStruct((t, R_PAD), BF16),
                   jax.ShapeDtypeStruct((8, D), F32), jax.ShapeDtypeStruct((R_PAD, D), F32)),
        input_output_aliases={11: 0}, compiler_params=_cparams(1, 40),
    )(*gf, *gb, z, p, wd, dp)


def _dp_tail(dk, dv, dr, dp, name):
    t = dk.shape[0]
    wk = NKV * HD

    def body(dk_ref, dv_ref, dr_ref, _, dp_ref):
        dp_ref[:, 0:wk] = dk_ref[...]
        dp_ref[:, wk:2 * wk] = dv_ref[...].astype(BF16)
        dp_ref[:, 2 * wk:2 * wk + R_PAD] = dr_ref[...]
        dp_ref[:, 2 * wk + R_PAD:] = jnp.zeros((rt, DP_BLOCKS["tail"][1] - 2 * wk - R_PAD), BF16)

    rt = _big_rows(t)
    kv = pl.BlockSpec((rt, wk), lambda i: (i, 0))
    return pl.pallas_call(
        body, name=name, grid=(t // rt,),
        in_specs=[kv, kv, pl.BlockSpec((rt, R_PAD), lambda i: (i, 0)), _HBM], out_specs=_dp_spec("tail", rt),
        out_shape=jax.ShapeDtypeStruct(dp.shape, dp.dtype), input_output_aliases={3: 0},
        compiler_params=_cparams(1, 32),
    )(dk, dv, dr, dp)


def _branch_fwd(att, of, ob, p, gla_g, name):
    t = p.shape[0]

    def body(att_ref, of_ref, ob_ref, za_ref, zg_ref, g_ref, yb_ref, yc_ref):
        za = za_ref[...].astype(F32)
        yb_ref[...] = (att_ref[...].astype(F32) * (za * _sigmoid(za))).astype(BF16)
        for h in range(GH):
            sl = slice(h * GDV, (h + 1) * GDV)
            o = of_ref[:, sl].astype(F32) + ob_ref[:, sl].astype(F32)
            n = o * lax.rsqrt(jnp.mean(o * o, axis=-1, keepdims=True) + EPS) * g_ref[...]
            zh = zg_ref[:, sl].astype(F32)
            yc_ref[:, sl] = (n * (zh * _sigmoid(zh))).astype(BF16)

    rt = _big_rows(t)
    row = pl.BlockSpec((rt, D), lambda i: (i, 0))
    return pl.pallas_call(
        body, name=name, grid=(t // rt,),
        in_specs=[row, row, row, _pcol("z_attn", D, rt), _pcol("zg", D, rt), pl.BlockSpec((1, GDV), lambda i: (0, 0))],
        out_specs=(row, row),
        out_shape=(jax.ShapeDtypeStruct((t, D), BF16), jax.ShapeDtypeStruct((t, D), BF16)),
        compiler_params=_cparams(1, 40),
    )(att, of, ob, p, p, gla_g)


def _branch_bwd(dyb, dyc, att, of, ob, p, gla_g, dp, name):
    t = p.shape[0]

    def body(dyb_ref, dyc_ref, att_ref, of_ref, ob_ref, za_ref, zg_ref, g_ref, _, datt_ref, do_ref, dp_ref, dg_ref):
        i = pl.program_id(0)

        @pl.when(i == 0)
        def _():
            dg_ref[...] = jnp.zeros_like(dg_ref)

        za, dyb = za_ref[...].astype(F32), dyb_ref[...].astype(F32)
        sa = _sigmoid(za)
        datt_ref[...] = (dyb * (za * sa)).astype(BF16)
        dp_ref[:, 0:D] = (dyb * att_ref[...].astype(F32) * (sa * (1.0 + za * (1.0 - sa)))).astype(BF16)
        g = g_ref[...]
        for h in range(GH):
            sl = slice(h * GDV, (h + 1) * GDV)
            o = of_ref[:, sl].astype(F32) + ob_ref[:, sl].astype(F32)
            r = lax.rsqrt(jnp.mean(o * o, axis=-1, keepdims=True) + EPS)
            oh = o * r
            zh, dyc = zg_ref[:, sl].astype(F32), dyc_ref[:, sl].astype(F32)
            sg = _sigmoid(zh)
            dn = dyc * (zh * sg)
            dp_ref[:, D + h * GDV:D + (h + 1) * GDV] = (dyc * (oh * g) * (sg * (1.0 + zh * (1.0 - sg)))).astype(BF16)
            doh = dn * g
            do_ref[:, sl] = (r * (doh - oh * jnp.mean(doh * oh, axis=-1, keepdims=True))).astype(BF16)
            dg_ref[...] += jnp.sum(dn * oh, axis=0, keepdims=True)

    row = pl.BlockSpec((TM, D), lambda i: (i, 0))
    return pl.pallas_call(
        body, name=name, grid=(t // TM,),
        in_specs=[row, row, row, row, row, _pcol("z_attn", D), _pcol("zg", D), pl.BlockSpec((1, GDV), lambda i: (0, 0)),
                  _HBM],
        out_specs=(row, row, _dp_spec("branch"), pl.BlockSpec((8, GDV), lambda i: (0, 0))),
        out_shape=(jax.ShapeDtypeStruct((t, D), BF16), jax.ShapeDtypeStruct((t, D), BF16),
                   jax.ShapeDtypeStruct(dp.shape, dp.dtype), jax.ShapeDtypeStruct((8, GDV), F32)),
        input_output_aliases={8: 2}, compiler_params=_cparams(1, 48),
    )(dyb, dyc, att, of, ob, p, p, gla_g, dp)


def _merge_fwd(bra, brb, brc, p, b_gate, name):
    t = p.shape[0]
    mgb = OFF["mg"] // D

    def body(a_ref, b_ref, c_ref, ga_ref, gb_ref, gc_ref, bg_ref, m_ref):
        m_ref[...] = (_sigmoid(ga_ref[...].astype(F32) + bg_ref[:, 0:D]) * a_ref[...].astype(F32)
                      + _sigmoid(gb_ref[...].astype(F32) + bg_ref[:, D:2 * D]) * b_ref[...].astype(F32)
                      + _sigmoid(gc_ref[...].astype(F32) + bg_ref[:, 2 * D:3 * D]) * c_ref[...].astype(F32)).astype(BF16)

    rt = _big_rows(t)
    row = pl.BlockSpec((rt, D), lambda i: (i, 0))
    gates = [pl.BlockSpec((rt, D), functools.partial(lambda i, b: (i, b), b=mgb + j)) for j in range(3)]
    return pl.pallas_call(
        body, name=name, grid=(t // rt,),
        in_specs=[row, row, row, *gates, pl.BlockSpec((1, 3 * D), lambda i: (0, 0))],
        out_specs=row, out_shape=jax.ShapeDtypeStruct((t, D), BF16), compiler_params=_cparams(1, 40),
    )(bra, brb, brc, p, p, p, b_gate)


def _merge_bwd(dm, bra, brb, brc, p, b_gate, name):
    t = p.shape[0]
    mgb = OFF["mg"] // D

    def body(dm_ref, a_ref, b_ref, c_ref, ga_ref, gb_ref, gc_ref, bg_ref, da_ref, db_ref, dc_ref, dmg_ref, dbg_ref):
        i = pl.program_id(0)

        @pl.when(i == 0)
        def _():
            dbg_ref[...] = jnp.zeros_like(dbg_ref)

        dm = dm_ref[...].astype(F32)
        for j, (br_ref, g_ref, d_ref) in enumerate(((a_ref, ga_ref, da_ref), (b_ref, gb_ref, db_ref), (c_ref, gc_ref, dc_ref))):
            sl = slice(j * D, (j + 1) * D)
            g = _sigmoid(g_ref[...].astype(F32) + bg_ref[:, sl])
            d_ref[...] = (dm * g).astype(BF16)
            dmg = dm * br_ref[...].astype(F32) * (g * (1.0 - g))
            dmg_ref[:, sl] = dmg.astype(BF16)
            dbg_ref[:, sl] += jnp.sum(dmg, axis=0, keepdims=True)

    rt = _big_rows(t)
    row = pl.BlockSpec((rt, D), lambda i: (i, 0))
    gates = [pl.BlockSpec((rt, D), functools.partial(lambda i, b: (i, b), b=mgb + j)) for j in range(3)]
    return pl.pallas_call(
        body, name=name, grid=(t // rt,),
        in_specs=[row, row, row, row, *gates, pl.BlockSpec((1, 3 * D), lambda i: (0, 0))],
        out_specs=(row, row, row, _dp_spec("merge", rt), pl.BlockSpec((8, 3 * D), lambda i: (0, 0))),
        out_shape=(jax.ShapeDtypeStruct((t, D), BF16),) * 3 + (jax.ShapeDtypeStruct((t, NP), BF16),
                                                                jax.ShapeDtypeStruct((8, 3 * D), F32)),
        compiler_params=_cparams(1, 56),
    )(dm, bra, brb, brc, p, p, p, b_gate)


def _adam_update(ns, g_ref, w_ref, m_ref, v_ref, go_ref, d_ref, mo_ref, vo_ref):
    g = g_ref[0].astype(F32)
    for s in range(1, ns):
        g = g + g_ref[s].astype(F32)
    mn = ADAM_B1 * m_ref[...] + (1.0 - ADAM_B1) * g
    vn = ADAM_B2 * v_ref[...] + (1.0 - ADAM_B2) * jnp.square(g)
    m_hat = mn / (1.0 - ADAM_B1 ** ADAM_STEP)
    v_hat = vn / (1.0 - ADAM_B2 ** ADAM_STEP)
    go_ref[...] = g
    d_ref[...] = -ADAM_LR * (m_hat / (jnp.sqrt(v_hat) + ADAM_EPS) + ADAM_WD * w_ref[...])
    mo_ref[...] = mn
    vo_ref[...] = vn


def _adamw(gsrc, w, m, v, name):
    ns, nl, r, c = gsrc.shape
    gb = gsrc.dtype.itemsize

    def fits(rows, cols):
        lanes = -(-cols // LANE) * LANE
        return ns * rows * lanes * gb <= ADAM_SRC_BYTES and rows * lanes * 4 <= ADAM_ROW_BYTES

    tr, tc = r, c
    if not fits(r, c):
        rows = [cand for cand in range(16, r, 16) if r % cand == 0 and fits(cand, c)]
        cols = [cand for cand in range(LANE, c, LANE) if c % cand == 0 and fits(r, cand)]
        if rows:
            tr = rows[-1]
        else:
            tc = cols[-1]

    def body(*refs):
        _adam_update(ns, *refs)

    row = pl.BlockSpec((None, tr, tc), lambda l, i, j: (l, i, j))
    return pl.pallas_call(
        body, name=name, grid=(nl, r // tr, c // tc),
        in_specs=[pl.BlockSpec((ns, None, tr, tc), lambda l, i, j: (0, l, i, j)), row, row, row],
        out_specs=(row,) * 4, out_shape=(jax.ShapeDtypeStruct((nl, r, c), F32),) * 4,
        compiler_params=_cparams(3, 48),
    )(gsrc, w, m, v)


def _pair_sum(a, b, name):
    s, r, c = a.shape
    tc = _pick(c, (256, 128))

    def body(a_ref, b_ref, o_ref):
        o_ref[...] = (a_ref[...].astype(F32) + b_ref[...].astype(F32)).astype(BF16)

    blk = pl.BlockSpec((None, r, tc), lambda i, j: (i, 0, j))
    return pl.pallas_call(
        body, name=name, grid=(s, c // tc), in_specs=[blk, blk], out_specs=blk,
        out_shape=jax.ShapeDtypeStruct(a.shape, BF16), compiler_params=_cparams(2, 32),
    )(a, b)


def _adamw_small(items, name):
    k = len(items)

    def body(*refs):
        for j in range(k):
            _adam_update(items[j][0].shape[0], *refs[4 * j:4 * j + 4], *refs[4 * k + 4 * j:4 * k + 4 * j + 4])

    out = pl.pallas_call(
        body, name=name,
        out_shape=tuple(jax.ShapeDtypeStruct(w.shape, F32) for _, w, _, _ in items for _ in range(4)),
    )(*[a for item in items for a in item])
    return [out[4 * j:4 * j + 4] for j in range(k)]


def _rope_tables(ctx, seq):
    n_rows = seq // GRID_W
    pairs = HD // 4
    row = jnp.repeat(jnp.arange(n_rows, dtype=F32), GRID_W)
    col = jnp.tile(jnp.arange(GRID_W, dtype=F32), n_rows)
    freqs = ROPE_THETA ** (-jnp.arange(pairs, dtype=F32) * 2.0 / (HD // 2))
    ar, ac = row[:, None] * freqs, col[:, None] * freqs
    cos_l = jnp.concatenate([jnp.cos(ar), jnp.cos(ar), jnp.cos(ac), jnp.cos(ac)], axis=1)
    sin_l = jnp.concatenate([-jnp.sin(ar), jnp.sin(ar), -jnp.sin(ac), jnp.sin(ac)], axis=1)
    cos_t = jnp.concatenate([jnp.ones((ctx, HD), F32), cos_l], axis=0)
    sin_t = jnp.concatenate([jnp.zeros((ctx, HD), F32), sin_l], axis=0)
    return cos_t, sin_t


def _to_proj_layout(wt):
    parts = [wt[s:s + wd] for _, s, wd in _SEGS]
    used = sum(wd for _, _, wd in _SEGS)
    parts.append(jnp.zeros((NP - used, wt.shape[1]), wt.dtype))
    return jnp.concatenate(parts, axis=0)


def _from_proj_layout(g):
    order = sorted(_SEGS, key=lambda sg: sg[1])
    return jnp.concatenate([g[OFF[n]:OFF[n] + wd] for n, _, wd in order], axis=0)


def _row0(a):
    return a[..., 0, :]


def kernel(x, c, ctx, c_ctx, w_ada, b_ada, g_pre, g_post, w_in, conv_w, q_norm_g, k_norm_g, w_decay_fwd, b_decay_fwd, w_decay_bwd, b_decay_bwd, gla_norm_g, w_br_conv, w_br_attn, w_br_gla, b_gate, w_out, loss_target, m_c_ctx, m_w_ada, m_b_ada, m_g_pre, m_g_post, m_w_in, m_conv_w, m_q_norm_g, m_k_norm_g, m_w_decay_fwd, m_b_decay_fwd, m_w_decay_bwd, m_b_decay_bwd, m_gla_norm_g, m_w_br_conv, m_w_br_attn, m_w_br_gla, m_b_gate, m_w_out, v_c_ctx, v_w_ada, v_b_ada, v_g_pre, v_g_post, v_w_in, v_conv_w, v_q_norm_g, v_k_norm_g, v_w_decay_fwd, v_b_decay_fwd, v_w_decay_bwd, v_b_decay_bwd, v_gla_norm_g, v_w_br_conv, v_w_br_attn, v_w_br_gla, v_b_gate, v_w_out):
    seq, n_ctx = x.shape[1], ctx.shape[1]
    assert n_ctx % TM == 0 and seq % TM == 0 and seq % GRID_W == 0
    t = n_ctx + seq
    nct = n_ctx // TM
    dev = 4 * lax.axis_index("x") + 2 * lax.axis_index("y") + lax.axis_index("c")
    ada_w = w_ada.shape[2]
    in_w = w_in.shape[2]
    br_r = w_br_conv.shape[1]

    def in_t(a, l):
        return a.transpose(2, 0, 1)[:, l, :]

    wb = [w.astype(BF16) for w in (w_ada, w_br_conv, w_br_attn, w_br_gla, w_out)]
    wall = _all_gather([wb[0][0], in_t(w_in, 0).astype(BF16), conv_w, w_decay_fwd, w_decay_bwd],
                       "gather_first")
    later_square = _GatherRider([wb[1], wb[2], wb[3], wb[4]])
    later_in = _GatherRider([in_t(w_in, 1).astype(BF16), wb[0][1]])

    def full_small(g):
        return g.transpose(1, 2, 0, 3).reshape(DEPTH, g.shape[2], NDEV * g.shape[3])

    def full_in(g):
        return _to_proj_layout(g.reshape(IN_WIDTH, D))

    def full_ada(g):
        return g.transpose(1, 0, 2).reshape(D, 3 * D)

    w_ada_f = [full_ada(wall[0]), None]
    wp = [full_in(wall[1]), None]
    conv_f, wdf_f, wdb_f = full_small(wall[2]), full_small(wall[3]), full_small(wall[4])

    cos_t, sin_t = _rope_tables(n_ctx, seq)
    cc = jnp.concatenate([c_ctx[None, :], c.reshape(1, D), jnp.zeros((6, D), F32)], axis=0)
    silu_cc, dsilu_cc = _ada_in(cc)

    conv8, wd_pad, bd = [], [], []
    for l in range(DEPTH):
        conv8.append(jnp.concatenate([conv_f[l], jnp.zeros((5, D), F32)], axis=0))
        zr = jnp.zeros((GLA_RANK, GH * GDK), F32)
        wd_pad.append(jnp.concatenate([jnp.concatenate([wdf_f[l], zr], axis=1), jnp.concatenate([zr, wdb_f[l]], axis=1),
                                       jnp.zeros((R_PAD - 2 * GLA_RANK, D), F32)], axis=0))
        bd.append(jnp.concatenate([b_decay_fwd[l], b_decay_bwd[l]])[None, :])

    xs = jnp.concatenate([ctx[0], x[0]], axis=0)
    saved = []
    for l in range(DEPTH):
        n = f"l{l}_"
        mod = _mm(silu_cc, w_ada_f[l], n + "mod", bias=b_ada[l][None, :])
        mod3 = mod[0:2].reshape(2, 3, D)
        h = _prenorm_fwd(xs, g_pre[l][None, :], mod3, nct, n + "prenorm")
        if l == 0:
            p, *got = _mm(h, wp[l], n + "proj", tb=True, out_dtype=BF16, tm=t // 2, rider=later_square)
            w_brs_f = [g.transpose(1, 0, 2, 3).reshape(DEPTH, D, D) for g in got]
        else:
            p = _mm(h, wp[l], n + "proj", tb=True, out_dtype=BF16, tm=t // 2)
        cv, ya = _conv_fwd(p, conv8[l], nct, n + "conv")
        qr, kr = _qk_prep_fwd(p, q_norm_g[l][None, :], k_norm_g[l][None, :], cos_t, sin_t, n + "qk_prep")
        att, lse, *got = _attn_fwd(qr, kr, p, nct, n + "attn", rider=later_in if l == 0 else None)
        if l == 0:
            wp[1], w_ada_f[1] = full_in(got[0]), full_ada(got[1])
        z, la = _decay_fwd(p, wd_pad[l], bd[l], n + "decay")
        of, stf, ob, stb = _gla_fwd(p, la, nct, n + "gla")
        yb, yc = _branch_fwd(att, of, ob, p, gla_norm_g[l][None, :], n + "branch")
        bra = _mm(ya, w_brs_f[0][l], n + "br_conv", out_dtype=BF16)
        brb = _mm(yb, w_brs_f[1][l], n + "br_attn", out_dtype=BF16)
        brc = _mm(yc, w_brs_f[2][l], n + "br_gla", out_dtype=BF16)
        mm_ = _merge_fwd(bra, brb, brc, p, b_gate[l][None, :], n + "merge")
        out = _mm(mm_, w_brs_f[3][l], n + "out")
        x_new = _post_fwd(xs, out, g_post[l][None, :], mod3, nct, n + "post")
        saved.append(dict(x=xs, mod3=mod3, h=h, p=p, cv=cv, ya=ya, qr=qr, kr=kr, att=att, lse=lse, z=z, la=la, of=of, ob=ob,
                          stf=stf, stb=stb, yb=yb, yc=yc, bra=bra, brb=brb, brc=brc, m=mm_, out=out))
        xs = x_new

    dx, sq = _loss_grad(xs, loss_target[0], nct, "loss")
    loss = lax.psum(0.5 * sq[0, 0] / D, ("x", "y", "c"))

    gw = {k: [None] * DEPTH for k in ("w_in", "br_conv", "br_attn", "br_gla", "out", "b_gate", "g_pre", "g_post",
                                      "conv_w", "qg", "kg", "wd", "bdec", "gla_g", "dmod")}
    dctx = []

    def in_slots(l):
        return _from_proj_layout(gw["w_in"][l]).reshape(NDEV, in_w, D)

    def br_slots(l):
        return [gw[k][l].reshape(NDEV, br_r, D) for k in ("br_conv", "br_attn", "br_gla", "out")]

    for l in reversed(range(DEPTH)):
        n = f"l{l}_b_"
        s = saved[l]
        p = s["p"]
        d_out, dgt, gw["g_post"][l] = _post_bwd(dx, s["out"], g_post[l][None, :], s["mod3"], nct, n + "post")
        dm = _mm(d_out, w_brs_f[3][l], n + "dm", tb=True, out_dtype=BF16)
        gw["out"][l] = _mm(s["m"], d_out, n + "dw_out", ta=True, out_dtype=BF16)
        dbra, dbrb, dbrc, dp, gw["b_gate"][l] = _merge_bwd(dm, s["bra"], s["brb"], s["brc"], p, b_gate[l][None, :], n + "merge")
        dya = _mm(dbra, w_brs_f[0][l], n + "dya", tb=True, out_dtype=BF16)
        dyb = _mm(dbrb, w_brs_f[1][l], n + "dyb", tb=True, out_dtype=BF16)
        dyc = _mm(dbrc, w_brs_f[2][l], n + "dyc", tb=True, out_dtype=BF16)
        gw["br_conv"][l] = _mm(s["ya"], dbra, n + "dw_conv", ta=True, out_dtype=BF16)
        gw["br_attn"][l] = _mm(s["yb"], dbrb, n + "dw_attn", ta=True, out_dtype=BF16)
        gw["br_gla"][l] = _mm(s["yc"], dbrc, n + "dw_gla", ta=True, out_dtype=BF16)
        dcv, dp = _conv_bwd_a(dya, p, s["cv"], dp, n + "conv_a")
        dp, gw["conv_w"][l] = _conv_bwd_b(dcv, p, conv8[l], nct, dp, n + "conv_b")
        datt, dgo, dp, gw["gla_g"][l] = _branch_bwd(dyb, dyc, s["att"], s["of"], s["ob"], p, gla_norm_g[l][None, :], dp, n + "branch")
        ex1 = _ExchangeRider([in_slots(DEPTH - 1)] + br_slots(DEPTH - 1)) if l == 0 else None
        dqr, dkr, dv, *got = _attn_bwd(s["qr"], s["kr"], p, s["att"], s["lse"], datt, nct, n + "attn", rider=ex1)
        if l == 0:
            recv_in1, recv_br1 = got[0], got[1:]
        dp, dk, gw["qg"][l], gw["kg"][l] = _qk_prep_bwd(dqr, dkr, p, q_norm_g[l][None, :], k_norm_g[l][None, :], cos_t, sin_t, dp, n + "qk_prep")
        gf, gb = _gla_bwd(p, s["la"], dgo, s["stf"], s["stb"], nct, n + "gla")
        dp, dr, gw["bdec"][l], gw["wd"][l] = _gla_merge_bwd(gf, gb, s["z"], p, wd_pad[l], dp, n + "gla_merge")
        dp = _dp_tail(dk, dv, dr, dp, n + "dp_tail")
        tk_in = t // 2 if t % 32 == 0 else None
        if l == 0:
            gw["w_in"][l], *recv_br0 = _mm(dp, s["h"], n + "dw_in", ta=True, out_dtype=BF16, tk=tk_in,
                                           rider=_ExchangeRider(br_slots(0)))
        else:
            gw["w_in"][l] = _mm(dp, s["h"], n + "dw_in", ta=True, out_dtype=BF16, tk=tk_in)
        if l == 0:
            core = lax.axis_index("c")
            halves = in_slots(0).reshape(NDEV // 2, 2, in_w, D)
            kept = lax.dynamic_index_in_dim(halves, core, axis=1, keepdims=False)
            sent = lax.dynamic_index_in_dim(halves, 1 - core, axis=1, keepdims=False)
            from_sibling, = _comm_alone(_SwapRider([sent]), n + "swap_dw_in")
            chip_sum = _pair_sum(kept, from_sibling, n + "chip_sum_dw_in")
            dh, recv_in0 = _mm(dp, wp[l], n + "dh", tk=NP // 4, rider=_ExchangeRider([chip_sum], chips_only=True))
        else:
            dh = _mm(dp, wp[l], n + "dh", tk=NP // 4)
        dx, dsh, dsc, gw["g_pre"][l] = _prenorm_bwd(dh, s["x"], dx, g_pre[l][None, :], s["mod3"], nct, n + "prenorm")
        dmod = jnp.stack([_row0(dsh), _row0(dsc), _row0(dgt)], axis=1).reshape(2, 3 * D)
        gw["dmod"][l] = dmod
        dmod8 = jnp.concatenate([dmod, jnp.zeros((6, 3 * D), F32)], axis=0)
        dctx.append(_mm(dmod8, w_ada_f[l], n + "dsilu", tb=True))
    grad_x = dx[n_ctx:][None]
    g_cctx = _cctx_grad(dctx[0], dctx[1], dsilu_cc)[0]

    def st2(name):
        return jnp.stack(gw[name])

    g_b_ada = jnp.stack([gw["dmod"][l][0] + gw["dmod"][l][1] for l in range(DEPTH)])
    g_bdf = jnp.stack([gw["bdec"][l][0, :GH * GDK] for l in range(DEPTH)])
    g_bdb = jnp.stack([gw["bdec"][l][0, GH * GDK:] for l in range(DEPTH)])
    g_wdf = jnp.stack([gw["wd"][l][0:GLA_RANK, :GH * GDK] for l in range(DEPTH)])
    g_wdb = jnp.stack([gw["wd"][l][GLA_RANK:2 * GLA_RANK, GH * GDK:] for l in range(DEPTH)])
    rep_grads = [g_cctx, g_b_ada, st2("g_pre")[:, 0], st2("g_post")[:, 0], st2("qg")[:, 0], st2("kg")[:, 0], g_bdf, g_bdb,
                 st2("gla_g")[:, 0], st2("b_gate")[:, 0]]
    rep_w = [c_ctx, b_ada, g_pre, g_post, q_norm_g, k_norm_g, b_decay_fwd, b_decay_bwd, gla_norm_g, b_gate]
    rep_m = [m_c_ctx, m_b_ada, m_g_pre, m_g_post, m_q_norm_g, m_k_norm_g, m_b_decay_fwd, m_b_decay_bwd, m_gla_norm_g, m_b_gate]
    rep_v = [v_c_ctx, v_b_ada, v_g_pre, v_g_post, v_q_norm_g, v_k_norm_g, v_b_decay_fwd, v_b_decay_bwd, v_gla_norm_g, v_b_gate]
    def two_d(a):
        return a.reshape(1, -1) if a.ndim == 1 else a

    def owner_slots(g):
        return g.reshape(DEPTH, g.shape[1], NDEV, g.shape[2] // NDEV).transpose(2, 0, 1, 3)

    n_rep = len(rep_grads)
    small = _comm_alone(_Riders([
        _GatherRider([two_d(g) for g in rep_grads] + [silu_cc[0:2], jnp.stack(gw["dmod"])]),
        _ExchangeRider([owner_slots(st2("conv_w")[:, 0:3]), owner_slots(g_wdf), owner_slots(g_wdb)])]),
        "exchange_small_grads")
    rep_src, (a_all, d_all), sh_src = small[:n_rep], small[n_rep:n_rep + 2], small[n_rep + 2:]
    sh_w = [conv_w, w_decay_fwd, w_decay_bwd]
    sh_m = [m_conv_w, m_w_decay_fwd, m_w_decay_bwd]
    sh_v = [v_conv_w, v_w_decay_fwd, v_w_decay_bwd]
    small_out = _adamw_small(
        [(g, two_d(w), two_d(m), two_d(v)) for g, w, m, v in zip(rep_src, rep_w, rep_m, rep_v)]
        + list(zip(sh_src, sh_w, sh_m, sh_v)), "adam_small")
    rep_g, rep_d, rep_nm, rep_nv = [[small_out[j][k].reshape(rep_w[j].shape) for j in range(n_rep)] for k in range(4)]
    sh_gr, sh_d, sh_nm, sh_nv = [[small_out[n_rep + j][k] for j in range(len(sh_w))] for k in range(4)]

    a_all = a_all.reshape(NDEV * 2, D)
    d_all = d_all.transpose(1, 0, 2, 3).reshape(DEPTH, NDEV * 2, 3 * D)
    g_ada = jnp.stack([_mm(a_all, lax.dynamic_slice_in_dim(d_all[l], dev * ada_w, ada_w, axis=1), f"dw_ada{l}",
                           ta=True, precise=True, tk=NDEV * 2) for l in range(DEPTH)])
    ada_g, ada_d, ada_nm, ada_nv = _adamw(g_ada[None], w_ada, m_w_ada, v_w_ada, "adam_ada")

    big_w = [w_br_conv, w_br_attn, w_br_gla, w_out]
    big_m = [m_w_br_conv, m_w_br_attn, m_w_br_gla, m_w_out]
    big_v = [v_w_br_conv, v_w_br_attn, v_w_br_gla, v_w_out]
    big_out = [_adamw(jnp.stack([recv_br0[j], recv_br1[j]], axis=1), big_w[j], big_m[j], big_v[j], f"adam_big{j}")
               for j in range(len(big_w))]
    in_out = [_adamw(r_[:, None], in_t(w_in, l)[None], in_t(m_w_in, l)[None], in_t(v_w_in, l)[None], f"adam_in{l}")
              for l, r_ in enumerate((recv_in0, recv_in1))]
    in_res = [jnp.stack([in_out[l][k][0] for l in range(DEPTH)], axis=1).transpose(1, 2, 0) for k in range(4)]
    big_g, big_d, big_nm, big_nv = [[in_res[k]] + [o[k] for o in big_out] for k in range(4)]

    def ordered(rep, ada, big, sh):
        c_ctx_, b_ada_, g_pre_, g_post_, qg_, kg_, bdf_, bdb_, glag_, bgate_ = rep
        w_in_, brc_, bra_, brg_, wout_ = big
        conv_, wdf_, wdb_ = sh
        return [c_ctx_, ada, b_ada_, g_pre_, g_post_, w_in_, conv_, qg_, kg_, wdf_, bdf_, wdb_, bdb_, glag_,
                brc_, bra_, brg_, bgate_, wout_]

    return (loss, grad_x,
            *ordered(rep_g, ada_g, big_g, sh_gr), *ordered(rep_d, ada_d, big_d, sh_d),
            *ordered(rep_nm, ada_nm, big_nm, sh_nm), *ordered(rep_nv, ada_nv, big_nv, sh_nv))
```

```python
import functools

import numpy as np
import jax
import jax.numpy as jnp
from jax import lax
from jax.experimental import pallas as pl
from jax.experimental.pallas import tpu as pltpu

F32, BF16 = jnp.float32, jnp.bfloat16
HIGHEST = lax.Precision.HIGHEST

D = 1024
DEPTH = 2
GRID_W = 64
NH, NKV, HD = 8, 2, 128
GROUP = NH // NKV
ROPE_THETA = 10000.0
ATTN_SCALE = HD ** -0.5
Q_FOLD = ATTN_SCALE * 1.4426950408889634
P_HALO = 16
GH, GDK, GDV = 4, 128, 256
GLA_RANK = 16
GLA_TAU = 16.0
CH = 64
GLA_SCALE = GDK ** -0.5
EPS = 1e-6
NDEV = 8
LANE = 128
TM = 256
BIG_ROWS = 544
ATTN_HEADS_PER_STEP = 4
ATTN_FWD_KEY_CHUNK = 2176
ATTN_BWD_KEY_CHUNK = 256
KEY_ALIGN = LANE

ADAM_LR, ADAM_B1, ADAM_B2, ADAM_EPS, ADAM_WD, ADAM_STEP = 0.001, 0.9, 0.999, 1e-08, 0.01, 10

_SEGS = (("a_b", 0, 1024), ("a_z", 3072, 1024), ("a_c", 1024, 1024), ("a_x", 2048, 1024),
         ("z_attn", 5632, 1024), ("zg", 8736, 1024), ("gv", 7680, 1024), ("gq", 6656, 512), ("gk", 7168, 512),
         ("q", 4096, 1024), ("mg", 9760, 3072), ("k", 5120, 256), ("v", 5376, 256), ("r", 8704, 32))
DP_BLOCKS = {"conv_a": ("a_b", 2048), "conv_b": ("a_c", 2048), "branch": ("z_attn", 2048), "gla": ("gv", 2048),
             "q": ("q", 1024), "merge": ("mg", 3072), "tail": ("k", 1024)}
IN_WIDTH = 12832
NP = 13312
OFF = {}
_o = 0
for _n, _s, _w in _SEGS:
    OFF[_n] = _o
    _o += _w
R_PAD = 128


def _cparams(ngrid, vmem_mb):
    return pltpu.CompilerParams(dimension_semantics=("arbitrary",) * ngrid, vmem_limit_bytes=vmem_mb << 20)


def _pick(n, cands):
    for c in cands:
        if n % c == 0:
            return c
    return n


def _sigmoid(x):
    return 1.0 / (1.0 + jnp.exp(-x))


ADAM_SRC_BYTES = 8 << 20
ADAM_ROW_BYTES = 1 << 20


def _all_gather(xs, name):
    return _comm_alone(_GatherRider(xs), name)


_HBM = pl.BlockSpec(memory_space=pl.ANY)


class _Rider:
    def __init__(self, xs, out_shapes, remote_copies=NDEV - 1):
        self.xs, self.n = list(xs), len(xs)
        self.out_shape = [jax.ShapeDtypeStruct(s, x.dtype) for s, x in zip(out_shapes, xs)]
        self.scratch = [pltpu.SemaphoreType.DMA((remote_copies * self.n,)),
                        pltpu.SemaphoreType.DMA((remote_copies * self.n,)), pltpu.SemaphoreType.DMA((self.n,))]


class _GatherRider(_Rider):
    def __init__(self, xs):
        super().__init__(xs, [(NDEV,) + x.shape for x in xs])

    def _parts(self, x_refs, out_refs, sems):
        n = self.n
        send_sems, recv_sems, local_sems = sems
        mx, my, mc = lax.axis_index("x"), lax.axis_index("y"), lax.axis_index("c")
        me, sibling = (mx, my, mc), (mx, my, 1 - mc)
        chips = [(1 - mx, my), (mx, 1 - my), (1 - mx, 1 - my)]

        def slot(a, px, py, pc):
            return out_refs[a].at[4 * px + 2 * py + pc]

        def copy(k, a, block, to, own=False):
            return pltpu.make_async_remote_copy(
                src_ref=x_refs[a] if own else slot(a, *block), dst_ref=slot(a, *block),
                send_sem=send_sems.at[k * n + a], recv_sem=recv_sems.at[k * n + a],
                device_id=to, device_id_type=pl.DeviceIdType.MESH)

        mine = [pltpu.make_async_copy(x_refs[a], slot(a, *me), local_sems.at[a]) for a in range(n)]
        first = [copy(0, a, me, sibling, own=True) for a in range(n)]
        first += [copy(1 + j, a, me, (*chip, mc), own=True) for a in range(n) for j, chip in enumerate(chips)]
        landed = [copy(1 + j, a, (*chip, mc), me) for a in range(n) for j, chip in enumerate(chips)]
        passed = [copy(4 + j, a, (*chip, mc), sibling) for a in range(n) for j, chip in enumerate(chips)]
        from_sibling = [copy(0, a, sibling, me) for a in range(n)]
        from_sibling += [copy(4 + j, a, (*chip, 1 - mc), me) for a in range(n) for j, chip in enumerate(chips)]
        return mine, first, landed, passed, from_sibling

    def start(self, x_refs, out_refs, sems):
        mine, first, _, _, _ = self._parts(x_refs, out_refs, sems)
        for cp in mine + first:
            cp.start()

    def middle(self, x_refs, out_refs, sems):
        _, _, landed, passed, _ = self._parts(x_refs, out_refs, sems)
        for got, fwd in zip(landed, passed):
            got.wait_recv()
            fwd.start()

    def finish(self, x_refs, out_refs, sems):
        mine, first, _, passed, from_sibling = self._parts(x_refs, out_refs, sems)
        for cp in from_sibling:
            cp.wait_recv()
        for cp in first + passed:
            cp.wait_send()
        for cp in mine:
            cp.wait()


class _ExchangeRider(_Rider):
    def __init__(self, xs, chips_only=False):
        self.chips_only = chips_only
        super().__init__(xs, [x.shape for x in xs], 3 if chips_only else NDEV - 1)

    def _parts(self, x_refs, out_refs, sems):
        n = self.n
        send_sems, recv_sems, local_sems = sems
        mx, my, mc = lax.axis_index("x"), lax.axis_index("y"), lax.axis_index("c")
        me = 2 * mx + my if self.chips_only else 4 * mx + 2 * my + mc
        mine = [pltpu.make_async_copy(x_refs[a].at[me], out_refs[a].at[me], local_sems.at[a]) for a in range(n)]
        copies = []
        for a in range(n):
            for rel in range(1, 4 if self.chips_only else NDEV):
                bits = rel << 1 if self.chips_only else rel
                px = (1 - mx) if bits & 4 else mx
                py = (1 - my) if bits & 2 else my
                pc = (1 - mc) if bits & 1 else mc
                peer = 2 * px + py if self.chips_only else 4 * px + 2 * py + pc
                k = (rel - 1) * n + a
                copies.append(pltpu.make_async_remote_copy(
                    src_ref=x_refs[a].at[peer], dst_ref=out_refs[a].at[me],
                    send_sem=send_sems.at[k], recv_sem=recv_sems.at[k],
                    device_id=(px, py, pc), device_id_type=pl.DeviceIdType.MESH))
        return mine, copies

    def start(self, x_refs, out_refs, sems):
        mine, copies = self._parts(x_refs, out_refs, sems)
        for cp in mine + copies:
            cp.start()

    def middle(self, x_refs, out_refs, sems):
        pass

    def finish(self, x_refs, out_refs, sems):
        mine, copies = self._parts(x_refs, out_refs, sems)
        for cp in copies:
            cp.wait_recv()
        for cp in copies:
            cp.wait_send()
        for cp in mine:
            cp.wait()


class _SwapRider(_Rider):
    def __init__(self, xs):
        super().__init__(xs, [x.shape for x in xs], 1)

    def _parts(self, x_refs, out_refs, sems):
        send_sems, recv_sems, _ = sems
        sibling = (lax.axis_index("x"), lax.axis_index("y"), 1 - lax.axis_index("c"))
        return [pltpu.make_async_remote_copy(
            src_ref=x_refs[a], dst_ref=out_refs[a], send_sem=send_sems.at[a], recv_sem=recv_sems.at[a],
            device_id=sibling, device_id_type=pl.DeviceIdType.MESH) for a in range(self.n)]

    def start(self, x_refs, out_refs, sems):
        for cp in self._parts(x_refs, out_refs, sems):
            cp.start()

    def middle(self, x_refs, out_refs, sems):
        pass

    def finish(self, x_refs, out_refs, sems):
        copies = self._parts(x_refs, out_refs, sems)
        for cp in copies:
            cp.wait_recv()
        for cp in copies:
            cp.wait_send()


class _Riders:
    def __init__(self, riders):
        self.riders = list(riders)
        self.xs = [x for r in self.riders for x in r.xs]
        self.n = len(self.xs)
        self.out_shape = [s for r in self.riders for s in r.out_shape]
        self.scratch = [s for r in self.riders for s in r.scratch]

    def _each(self, method, x_refs, out_refs, sems):
        a = b = 0
        for r in self.riders:
            getattr(r, method)(x_refs[a:a + r.n], out_refs[a:a + r.n], sems[b:b + len(r.scratch)])
            a, b = a + r.n, b + len(r.scratch)

    def start(self, *refs):
        self._each("start", *refs)

    def middle(self, *refs):
        self._each("middle", *refs)

    def finish(self, *refs):
        self._each("finish", *refs)


def _comm_alone(rider, name):
    n = rider.n

    def body(*refs):
        x_refs, out_refs, sems = refs[:n], refs[n:2 * n], refs[2 * n:]
        rider.start(x_refs, out_refs, sems)
        rider.middle(x_refs, out_refs, sems)
        rider.finish(x_refs, out_refs, sems)

    return pl.pallas_call(
        body, name=name, out_shape=tuple(rider.out_shape), in_specs=[_HBM] * n, out_specs=(_HBM,) * n,
        scratch_shapes=rider.scratch,
    )(*rider.xs)


def _with_rider(body, nin, nout, rider, first, mid, last):
    if rider is None:
        return body
    n = rider.n

    def wrapped(*refs):
        ins, x_refs = refs[:nin], refs[nin:nin + n]
        outs, out_refs = refs[nin + n:nin + n + nout], refs[nin + n + nout:nin + 2 * n + nout]
        ns = len(rider.scratch)
        scratch, sems = refs[nin + 2 * n + nout:len(refs) - ns], refs[len(refs) - ns:]

        @pl.when(first())
        def _():
            rider.start(x_refs, out_refs, sems)

        body(*ins, *outs, *scratch)

        @pl.when(mid())
        def _():
            rider.middle(x_refs, out_refs, sems)

        @pl.when(last())
        def _():
            rider.finish(x_refs, out_refs, sems)

    return wrapped


def _mm(a, b, name, ta=False, tb=False, out_dtype=F32, bias=None, precise=False, tm=None, tn=None, tk=None, rider=None):
    m, k = (a.shape[1], a.shape[0]) if ta else a.shape
    n = b.shape[0] if tb else b.shape[1]
    assert k == (b.shape[1] if tb else b.shape[0])
    tm = tm or _pick(m, (1088, 1024, 512, 256, 128))
    tn = tn or _pick(n, (1024, 512, 384, 256, 128))
    tk = tk or _pick(k, (1024, 1088, 512, 256, 128))
    nk = k // tk
    dn = (((0 if ta else 1,), (1 if tb else 0,)), ((), ()))

    def body(*refs):
        if bias is None:
            a_ref, b_ref, o_ref = refs[:3]
            bias_ref = None
        else:
            a_ref, b_ref, bias_ref, o_ref = refs[:4]
        x, y = a_ref[...], b_ref[...]
        if precise:
            p = lax.dot_general(x.astype(F32), y.astype(F32), dn, preferred_element_type=F32, precision=HIGHEST)
        else:
            p = lax.dot_general(x.astype(BF16), y.astype(BF16), dn, preferred_element_type=F32)

        def finish(acc):
            if bias_ref is not None:
                acc = acc + bias_ref[...]
            o_ref[...] = acc.astype(out_dtype)

        if nk == 1:
            finish(p)
        else:
            acc_ref = refs[-1]
            kk = pl.program_id(2)

            @pl.when(kk == 0)
            def _():
                acc_ref[...] = p

            @pl.when(kk > 0)
            def _():
                acc_ref[...] += p

            @pl.when(kk == nk - 1)
            def _():
                finish(acc_ref[...])

    a_spec = pl.BlockSpec((tk, tm), lambda i, j, kk: (kk, i)) if ta else pl.BlockSpec((tm, tk), lambda i, j, kk: (i, kk))
    b_spec = pl.BlockSpec((tn, tk), lambda i, j, kk: (j, kk)) if tb else pl.BlockSpec((tk, tn), lambda i, j, kk: (kk, j))
    in_specs = [a_spec, b_spec]
    args = [a, b]
    if bias is not None:
        in_specs.append(pl.BlockSpec((1, tn), lambda i, j, kk: (0, j)))
        args.append(bias)
    grid = (m // tm, n // tn, nk)
    out_spec = pl.BlockSpec((tm, tn), lambda i, j, kk: (i, j))
    scratch = [pltpu.VMEM((tm, tn), F32)] if nk > 1 else []
    if rider is None:
        return pl.pallas_call(
            body, name=name, grid=grid, in_specs=in_specs, out_specs=out_spec,
            out_shape=jax.ShapeDtypeStruct((m, n), out_dtype), scratch_shapes=scratch, compiler_params=_cparams(3, 56),
        )(*args)

    def at(step):
        return lambda: ((pl.program_id(0) == step[0]) & (pl.program_id(1) == step[1]) & (pl.program_id(2) == step[2]))

    end = tuple(g - 1 for g in grid)
    step = grid[0] * grid[1] * grid[2] * 7 // 8
    late = (step // (grid[1] * grid[2]), step // grid[2] % grid[1], step % grid[2])
    return pl.pallas_call(
        _with_rider(body, len(args), 1, rider, at((0, 0, 0)), at(late), at(end)),
        name=name, grid=grid, in_specs=in_specs + [_HBM] * rider.n, out_specs=(out_spec,) + (_HBM,) * rider.n,
        out_shape=(jax.ShapeDtypeStruct((m, n), out_dtype),) + tuple(rider.out_shape),
        scratch_shapes=scratch + rider.scratch, compiler_params=_cparams(3, 56),
    )(*args, *rider.xs)


def _ada_in(cc):
    def body(c_ref, s_ref, d_ref):
        x = c_ref[...]
        sg = _sigmoid(x)
        s_ref[...] = x * sg
        d_ref[...] = sg * (1.0 + x * (1.0 - sg))

    return pl.pallas_call(body, name="ada_in", out_shape=(jax.ShapeDtypeStruct(cc.shape, F32),) * 2)(cc)


def _cctx_grad(t0, t1, dsilu):
    def body(a_ref, b_ref, d_ref, o_ref):
        o_ref[...] = (a_ref[...] + b_ref[...]) * d_ref[...]

    return pl.pallas_call(body, name="cctx_grad", out_shape=jax.ShapeDtypeStruct(t0.shape, F32))(t0, t1, dsilu)


def _seg_spec(nct, rows=3):
    return pl.BlockSpec((None, rows, D), lambda i: (jnp.where(i >= nct, 1, 0), 0, 0))


def _prenorm_fwd(x, g_pre, mod3, nct, name):
    t = x.shape[0]

    def body(x_ref, g_ref, mod_ref, h_ref):
        xv = x_ref[...]
        r = lax.rsqrt(jnp.mean(xv * xv, axis=-1, keepdims=True) + EPS)
        y = xv * r * g_ref[...]
        h_ref[...] = (y * (1.0 + mod_ref[1:2, :]) + mod_ref[0:1, :]).astype(BF16)

    return pl.pallas_call(
        body, name=name, grid=(t // TM,),
        in_specs=[pl.BlockSpec((TM, D), lambda i: (i, 0)), pl.BlockSpec((1, D), lambda i: (0, 0)), _seg_spec(nct)],
        out_specs=pl.BlockSpec((TM, D), lambda i: (i, 0)),
        out_shape=jax.ShapeDtypeStruct((t, D), BF16), compiler_params=_cparams(1, 32),
    )(x, g_pre, mod3)


def _prenorm_bwd(dh, x, dxo, g_pre, mod3, nct, name):
    t = x.shape[0]

    def body(dh_ref, x_ref, dxo_ref, g_ref, mod_ref, dx_ref, dsh_ref, dsc_ref, dg_ref):
        i = pl.program_id(0)
        xv, dhv, g = x_ref[...], dh_ref[...], g_ref[...]
        r = lax.rsqrt(jnp.mean(xv * xv, axis=-1, keepdims=True) + EPS)
        xh = xv * r
        dy = dhv * (1.0 + mod_ref[1:2, :])
        dxh = dy * g
        dx_ref[...] = dxo_ref[...] + r * (dxh - xh * jnp.mean(dxh * xh, axis=-1, keepdims=True))

        @pl.when((i == 0) | (i == nct))
        def _():
            dsh_ref[...] = jnp.zeros_like(dsh_ref)
            dsc_ref[...] = jnp.zeros_like(dsc_ref)

        @pl.when(i == 0)
        def _():
            dg_ref[...] = jnp.zeros_like(dg_ref)

        dsh_ref[...] += jnp.sum(dhv, axis=0, keepdims=True)
        dsc_ref[...] += jnp.sum(dhv * (xh * g), axis=0, keepdims=True)
        dg_ref[...] += jnp.sum(dy * xh, axis=0, keepdims=True)

    row = pl.BlockSpec((TM, D), lambda i: (i, 0))
    seg8 = pl.BlockSpec((None, 8, D), lambda i: (jnp.where(i >= nct, 1, 0), 0, 0))
    return pl.pallas_call(
        body, name=name, grid=(t // TM,),
        in_specs=[row, row, row, pl.BlockSpec((1, D), lambda i: (0, 0)), _seg_spec(nct)],
        out_specs=(row, seg8, seg8, pl.BlockSpec((8, D), lambda i: (0, 0))),
        out_shape=(jax.ShapeDtypeStruct((t, D), F32), jax.ShapeDtypeStruct((2, 8, D), F32),
                   jax.ShapeDtypeStruct((2, 8, D), F32), jax.ShapeDtypeStruct((8, D), F32)),
        compiler_params=_cparams(1, 32),
    )(dh, x, dxo, g_pre, mod3)


def _post_fwd(x, out, g_post, mod3, nct, name):
    t = x.shape[0]

    def body(x_ref, o_ref, g_ref, mod_ref, y_ref):
        ov = o_ref[...]
        r = lax.rsqrt(jnp.mean(ov * ov, axis=-1, keepdims=True) + EPS)
        y_ref[...] = x_ref[...] + mod_ref[2:3, :] * (ov * r * g_ref[...])

    row = pl.BlockSpec((TM, D), lambda i: (i, 0))
    return pl.pallas_call(
        body, name=name, grid=(t // TM,),
        in_specs=[row, row, pl.BlockSpec((1, D), lambda i: (0, 0)), _seg_spec(nct)],
        out_specs=row, out_shape=jax.ShapeDtypeStruct((t, D), F32), compiler_params=_cparams(1, 32),
    )(x, out, g_post, mod3)


def _post_bwd(dxo, out, g_post, mod3, nct, name):
    t = out.shape[0]

    def body(dx_ref, o_ref, g_ref, mod_ref, do_ref, dgt_ref, dg_ref):
        i = pl.program_id(0)
        ov, dxv, g = o_ref[...], dx_ref[...], g_ref[...]
        r = lax.rsqrt(jnp.mean(ov * ov, axis=-1, keepdims=True) + EPS)
        nh = ov * r
        dn = dxv * mod_ref[2:3, :]
        dnh = dn * g
        do_ref[...] = (r * (dnh - nh * jnp.mean(dnh * nh, axis=-1, keepdims=True))).astype(BF16)

        @pl.when((i == 0) | (i == nct))
        def _():
            dgt_ref[...] = jnp.zeros_like(dgt_ref)

        @pl.when(i == 0)
        def _():
            dg_ref[...] = jnp.zeros_like(dg_ref)

        dgt_ref[...] += jnp.sum(dxv * (nh * g), axis=0, keepdims=True)
        dg_ref[...] += jnp.sum(dn * nh, axis=0, keepdims=True)

    row = pl.BlockSpec((TM, D), lambda i: (i, 0))
    seg8 = pl.BlockSpec((None, 8, D), lambda i: (jnp.where(i >= nct, 1, 0), 0, 0))
    return pl.pallas_call(
        body, name=name, grid=(t // TM,),
        in_specs=[row, row, pl.BlockSpec((1, D), lambda i: (0, 0)), _seg_spec(nct)],
        out_specs=(row, seg8, pl.BlockSpec((8, D), lambda i: (0, 0))),
        out_shape=(jax.ShapeDtypeStruct((t, D), BF16), jax.ShapeDtypeStruct((2, 8, D), F32),
                   jax.ShapeDtypeStruct((8, D), F32)),
        compiler_params=_cparams(1, 32),
    )(dxo, out, g_post, mod3)


def _loss_grad(y, target, nct, name):
    t = y.shape[0]

    def body(y_ref, t_ref, dy_ref, l_ref):
        i = pl.program_id(0)

        @pl.when(i == 0)
        def _():
            l_ref[...] = jnp.zeros_like(l_ref)

        @pl.when(i < nct)
        def _():
            dy_ref[...] = jnp.zeros_like(dy_ref)

        @pl.when(i >= nct)
        def _():
            err = y_ref[...] - t_ref[...]
            dy_ref[...] = err / D
            l_ref[...] += jnp.sum(jnp.sum(err * err, axis=1, keepdims=True), axis=0, keepdims=True)

    row = pl.BlockSpec((TM, D), lambda i: (i, 0))
    return pl.pallas_call(
        body, name=name, grid=(t // TM,),
        in_specs=[row, pl.BlockSpec((TM, D), lambda i: (jnp.maximum(i - nct, 0), 0))],
        out_specs=(row, pl.BlockSpec((8, LANE), lambda i: (0, 0))),
        out_shape=(jax.ShapeDtypeStruct((t, D), F32), jax.ShapeDtypeStruct((8, LANE), F32)),
        compiler_params=_cparams(1, 32),
    )(y, target)


def _pcol(name, width, rows=TM):
    assert OFF[name] % width == 0
    blk = OFF[name] // width
    return pl.BlockSpec((rows, width), lambda i: (i, blk))


def _big_rows(t):
    return max(r for r in range(16, BIG_ROWS + 1, 16) if t % r == 0)


def _shift_rows(u, prev_row, next_row):
    n = u.shape[0]
    row = lax.broadcasted_iota(jnp.int32, u.shape, 0)
    prev = jnp.where(row == 0, prev_row, pltpu.roll(u, 1, 0))
    nxt = jnp.where(row == n - 1, next_row, pltpu.roll(u, n - 1, 0))
    return prev, nxt


def _halo_specs(width, nt, blk=0, rows=8):
    per = TM // rows
    prev = pl.BlockSpec((rows, width), lambda i: (jnp.maximum(i * per - 1, 0), blk))
    nxt = pl.BlockSpec((rows, width), lambda i: (jnp.minimum((i + 1) * per, nt * per - 1), blk))
    return prev, nxt


def _conv_fwd(p, conv_w8, nct, name):
    t = p.shape[0]
    nt = t // TM

    def body(ab_ref, ac_ref, ax_ref, az_ref, acp_ref, axp_ref, acn_ref, axn_ref, w_ref, cv_ref, ya_ref):
        i = pl.program_id(0)
        def f(ref, rows=slice(None)):
            return ref[rows, :].astype(F32)

        u = f(ac_ref) * f(ax_ref)
        mp = jnp.where((i == 0) | (i == nct), 0.0, 1.0)
        mn = jnp.where((i == nct - 1) | (i == nt - 1), 0.0, 1.0)
        last, first = slice(P_HALO - 1, P_HALO), slice(0, 1)
        prev, nxt = _shift_rows(u, f(acp_ref, last) * f(axp_ref, last) * mp, f(acn_ref, first) * f(axn_ref, first) * mn)
        cv = w_ref[0:1, :] * prev + w_ref[1:2, :] * u + w_ref[2:3, :] * nxt
        az = f(az_ref)
        cv_ref[...] = cv.astype(BF16)
        ya_ref[...] = (f(ab_ref) * cv * (az * _sigmoid(az))).astype(BF16)

    acp, acn = _halo_specs(D, nt, OFF["a_c"] // D, P_HALO)
    axp, axn = _halo_specs(D, nt, OFF["a_x"] // D, P_HALO)
    row = pl.BlockSpec((TM, D), lambda i: (i, 0))
    return pl.pallas_call(
        body, name=name, grid=(nt,),
        in_specs=[_pcol("a_b", D), _pcol("a_c", D), _pcol("a_x", D), _pcol("a_z", D), acp, axp, acn, axn,
                  pl.BlockSpec((8, D), lambda i: (0, 0))],
        out_specs=(row, row),
        out_shape=(jax.ShapeDtypeStruct((t, D), BF16), jax.ShapeDtypeStruct((t, D), BF16)),
        compiler_params=_cparams(1, 40),
    )(p, p, p, p, p, p, p, p, conv_w8)


def _dp_spec(key, rows=TM):
    seg, width = DP_BLOCKS[key]
    assert OFF[seg] % width == 0
    blk = OFF[seg] // width
    return pl.BlockSpec((rows, width), lambda i: (i, blk))


def _conv_bwd_a(dya, p, cv, dp, name):
    t = p.shape[0]

    def body(dy_ref, ab_ref, az_ref, cv_ref, _, dcv_ref, dp_ref):
        dy, ab = dy_ref[...].astype(F32), ab_ref[...].astype(F32)
        az, c = az_ref[...].astype(F32), cv_ref[...].astype(F32)
        sg = _sigmoid(az)
        sz = az * sg
        dcv_ref[...] = dy * ab * sz
        dp_ref[:, 0:D] = (dy * c * sz).astype(BF16)
        dp_ref[:, D:2 * D] = (dy * ab * c * (sg * (1.0 + az * (1.0 - sg)))).astype(BF16)

    rt = _big_rows(t)
    row = pl.BlockSpec((rt, D), lambda i: (i, 0))
    return pl.pallas_call(
        body, name=name, grid=(t // rt,),
        in_specs=[row, _pcol("a_b", D, rt), _pcol("a_z", D, rt), row, _HBM], out_specs=(row, _dp_spec("conv_a", rt)),
        out_shape=(jax.ShapeDtypeStruct((t, D), F32), jax.ShapeDtypeStruct(dp.shape, dp.dtype)),
        input_output_aliases={4: 1}, compiler_params=_cparams(1, 40),
    )(dya, p, p, cv, dp)


def _conv_bwd_b(dcv, p, conv_w8, nct, dp, name):
    t = p.shape[0]
    nt = t // TM

    def body(dcv_ref, hp_ref, hn_ref, ac_ref, ax_ref, w_ref, _, dp_ref, dw_ref):
        i = pl.program_id(0)
        d, ac, ax = dcv_ref[...], ac_ref[...].astype(F32), ax_ref[...].astype(F32)
        u = ac * ax
        mp = jnp.where((i == 0) | (i == nct), 0.0, 1.0)
        mn = jnp.where((i == nct - 1) | (i == nt - 1), 0.0, 1.0)
        dprev, dnxt = _shift_rows(d, hp_ref[7:8, :] * mp, hn_ref[0:1, :] * mn)
        du = w_ref[0:1, :] * dnxt + w_ref[1:2, :] * d + w_ref[2:3, :] * dprev
        dp_ref[:, 0:D] = (du * ax).astype(BF16)
        dp_ref[:, D:2 * D] = (du * ac).astype(BF16)

        @pl.when(i == 0)
        def _():
            dw_ref[...] = jnp.zeros_like(dw_ref)

        dw0 = jnp.sum(u * dnxt, axis=0, keepdims=True)
        dw1 = jnp.sum(u * d, axis=0, keepdims=True)
        dw2 = jnp.sum(u * dprev, axis=0, keepdims=True)
        r8 = lax.broadcasted_iota(jnp.int32, (8, D), 0)
        dw_ref[...] += jnp.where(r8 == 0, dw0, jnp.where(r8 == 1, dw1, jnp.where(r8 == 2, dw2, 0.0)))

    hp, hn = _halo_specs(D, nt)
    row = pl.BlockSpec((TM, D), lambda i: (i, 0))
    return pl.pallas_call(
        body, name=name, grid=(nt,),
        in_specs=[row, hp, hn, _pcol("a_c", D), _pcol("a_x", D), pl.BlockSpec((8, D), lambda i: (0, 0)), _HBM],
        out_specs=(_dp_spec("conv_b"), pl.BlockSpec((8, D), lambda i: (0, 0))),
        out_shape=(jax.ShapeDtypeStruct(dp.shape, dp.dtype), jax.ShapeDtypeStruct((8, D), F32)),
        input_output_aliases={6: 0}, compiler_params=_cparams(1, 40),
    )(dcv, dcv, dcv, p, p, conv_w8, dp)


def _rot_half(x):
    lane = lax.broadcasted_iota(jnp.int32, x.shape, 1)
    return jnp.where((lane % 64) < 32, pltpu.roll(x, 96, 1), pltpu.roll(x, 32, 1))


def _qk_prep_fwd(p, qg, kg, cos_t, sin_t, name):
    t = p.shape[0]

    def body(q_ref, k_ref, qg_ref, kg_ref, c_ref, s_ref, qo_ref, ko_ref):
        c, s = c_ref[...], s_ref[...]

        def one(xv, g, scale):
            y = xv * lax.rsqrt(jnp.mean(xv * xv, axis=-1, keepdims=True) + EPS) * g
            return ((y * c + _rot_half(y) * s) * scale).astype(BF16)

        for h in range(NH):
            qo_ref[:, h * HD:(h + 1) * HD] = one(q_ref[:, h * HD:(h + 1) * HD].astype(F32), qg_ref[...], Q_FOLD)
        for h in range(NKV):
            ko_ref[:, h * HD:(h + 1) * HD] = one(k_ref[:, h * HD:(h + 1) * HD].astype(F32), kg_ref[...], 1.0)

    vec = pl.BlockSpec((1, HD), lambda i: (0, 0))
    rt = _big_rows(t)
    tab = pl.BlockSpec((rt, HD), lambda i: (i, 0))
    return pl.pallas_call(
        body, name=name, grid=(t // rt,),
        in_specs=[_pcol("q", NH * HD, rt), _pcol("k", NKV * HD, rt), vec, vec, tab, tab],
        out_specs=(pl.BlockSpec((rt, NH * HD), lambda i: (i, 0)), pl.BlockSpec((rt, NKV * HD), lambda i: (i, 0))),
        out_shape=(jax.ShapeDtypeStruct((t, NH * HD), BF16), jax.ShapeDtypeStruct((t, NKV * HD), BF16)),
        compiler_params=_cparams(1, 32),
    )(p, p, qg, kg, cos_t, sin_t)


def _qk_prep_bwd(dqr, dkr, p, qg, kg, cos_t, sin_t, dp, name):
    t = p.shape[0]

    def body(dq_ref, dk_ref, q_ref, k_ref, qg_ref, kg_ref, c_ref, s_ref, _, dqo_ref, dko_ref, dqg_ref, dkg_ref):
        i = pl.program_id(0)
        c, s = c_ref[...], s_ref[...]

        @pl.when(i == 0)
        def _():
            dqg_ref[...] = jnp.zeros_like(dqg_ref)
            dkg_ref[...] = jnp.zeros_like(dkg_ref)

        def one(dyr, xv, g):
            dy = dyr * c + _rot_half(dyr * s)
            r = lax.rsqrt(jnp.mean(xv * xv, axis=-1, keepdims=True) + EPS)
            xh = xv * r
            dxh = dy * g
            dx = r * (dxh - xh * jnp.mean(dxh * xh, axis=-1, keepdims=True))
            return dx.astype(BF16), jnp.sum(dy * xh, axis=0, keepdims=True)

        for h in range(NH):
            sl = slice(h * HD, (h + 1) * HD)
            dx, dg = one(dq_ref[:, sl] * ATTN_SCALE, q_ref[:, sl].astype(F32), qg_ref[...])
            dqo_ref[:, sl] = dx
            dqg_ref[...] += dg
        for h in range(NKV):
            sl = slice(h * HD, (h + 1) * HD)
            dx, dg = one(dk_ref[:, sl] * (ATTN_SCALE / Q_FOLD), k_ref[:, sl].astype(F32), kg_ref[...])
            dko_ref[:, sl] = dx
            dkg_ref[...] += dg

    vec = pl.BlockSpec((1, HD), lambda i: (0, 0))
    rt = _big_rows(t)
    tab = pl.BlockSpec((rt, HD), lambda i: (i, 0))
    acc = pl.BlockSpec((8, HD), lambda i: (0, 0))
    qrow = pl.BlockSpec((rt, NH * HD), lambda i: (i, 0))
    krow = pl.BlockSpec((rt, NKV * HD), lambda i: (i, 0))
    return pl.pallas_call(
        body, name=name, grid=(t // rt,),
        in_specs=[qrow, krow, _pcol("q", NH * HD, rt), _pcol("k", NKV * HD, rt), vec, vec, tab, tab, _HBM],
        out_specs=(_dp_spec("q", rt), krow, acc, acc),
        out_shape=(jax.ShapeDtypeStruct(dp.shape, dp.dtype), jax.ShapeDtypeStruct((t, NKV * HD), BF16),
                   jax.ShapeDtypeStruct((8, HD), F32), jax.ShapeDtypeStruct((8, HD), F32)),
        input_output_aliases={8: 0}, compiler_params=_cparams(1, 32),
    )(dqr, dkr, p, p, qg, kg, cos_t, sin_t, dp)


def _key_chunks(n, limit):
    c = max(c for c in range(KEY_ALIGN, min(n, limit) + 1, KEY_ALIGN) if n % c == 0)
    return [(lo, lo + c) for lo in range(0, n, c)]


def _attn_fwd(qr, kr, p, nct, name, rider=None):
    t = qr.shape[0]
    nt = t // TM
    ctx = nct * TM
    vblk = OFF["v"] // HD
    hps = ATTN_HEADS_PER_STEP
    nhp, per_kv = NH // hps, GROUP // hps

    def body(q_ref, k_ref, v_ref, o_ref, lse_ref):
        def tile(nkeys):
            sls = [slice(j * HD, (j + 1) * HD) for j in range(hps)]
            qs = [q_ref[:, sl] for sl in sls]
            m = l = acc = None
            for lo, hi in _key_chunks(nkeys, ATTN_FWD_KEY_CHUNK):
                k, vb = k_ref[lo:hi, :], v_ref[lo:hi, :].astype(BF16)
                ss = [lax.dot_general(q, k, _NT, preferred_element_type=F32) for q in qs]
                mcs = [jnp.max(s, axis=-1, keepdims=True) for s in ss]
                m_new = mcs if m is None else [jnp.maximum(a, b) for a, b in zip(m, mcs)]
                es = [jnp.exp2(s - mn) for s, mn in zip(ss, m_new)]
                lcs = [jnp.sum(e, axis=-1, keepdims=True) for e in es]
                pvs = [jnp.dot(e.astype(BF16), vb, preferred_element_type=F32) for e in es]
                if m is None:
                    l, acc = lcs, pvs
                else:
                    alphas = [jnp.exp2(a - b) for a, b in zip(m, m_new)]
                    l = [x * al + y for x, al, y in zip(l, alphas, lcs)]
                    acc = [x * al + y for x, al, y in zip(acc, alphas, pvs)]
                m = m_new
            for j, sl in enumerate(sls):
                o_ref[:, sl] = (acc[j] / l[j]).astype(BF16)
                lse_ref[:, j:j + 1] = m[j] + jnp.log2(l[j])

        pl.when(pl.program_id(1) < nct)(lambda: tile(ctx))
        pl.when(pl.program_id(1) >= nct)(lambda: tile(t))

    def at(h, i):
        return lambda: (pl.program_id(0) == h) & (pl.program_id(1) == i)

    rn = 0 if rider is None else rider.n
    qspec = pl.BlockSpec((TM, hps * HD), lambda h, i: (i, h))
    return pl.pallas_call(
        _with_rider(body, 3, 2, rider, at(0, 0), at(*divmod(nhp * nt * 7 // 8, nt)), at(nhp - 1, nt - 1)),
        name=name, grid=(nhp, nt),
        in_specs=[qspec, pl.BlockSpec((t, HD), lambda h, i: (0, h // per_kv)),
                  pl.BlockSpec((t, HD), lambda h, i: (0, vblk + h // per_kv))] + [_HBM] * rn,
        out_specs=(qspec, pl.BlockSpec((None, TM, hps), lambda h, i: (h, i, 0))) + (_HBM,) * rn,
        out_shape=(jax.ShapeDtypeStruct((t, NH * HD), BF16), jax.ShapeDtypeStruct((nhp, t, hps), F32))
        + (() if rider is None else tuple(rider.out_shape)),
        scratch_shapes=[] if rider is None else rider.scratch,
        compiler_params=_cparams(2, 48),
    )(qr, kr, p, *(() if rider is None else rider.xs))


def _attn_bwd(qr, kr, p, o, lse, do, nct, name, rider=None):
    t = qr.shape[0]
    nt = t // TM
    ctx = nct * TM
    vblk = OFF["v"] // HD
    hps = ATTN_HEADS_PER_STEP

    def body(q_ref, k_ref, v_ref, o_ref, lse_ref, do_ref, dq_ref, dk_ref, dv_ref):
        g, i = pl.program_id(1), pl.program_id(2)

        @pl.when((g == 0) & (i == 0))
        def _():
            dk_ref[...] = jnp.zeros_like(dk_ref)
            dv_ref[...] = jnp.zeros_like(dv_ref)

        def tile(nkeys):
            heads = []
            for j in range(hps):
                sl = slice(j * HD, (j + 1) * HD)
                dob = do_ref[:, sl]
                drow = jnp.sum(dob.astype(F32) * o_ref[:, sl].astype(F32), axis=-1, keepdims=True)
                heads.append((sl, q_ref[:, sl], dob, drow, lse_ref[:, j:j + 1]))
            dq = [None] * hps
            for lo, hi in _key_chunks(nkeys, ATTN_BWD_KEY_CHUNK):
                k = k_ref[lo:hi, :]
                vb = v_ref[lo:hi, :].astype(BF16)
                ss = [lax.dot_general(q, k, _NT, preferred_element_type=F32) for _, q, _, _, _ in heads]
                dps = [lax.dot_general(dob, vb, _NT, preferred_element_type=F32) for _, _, dob, _, _ in heads]
                prs = [jnp.exp2(s - h[4]) for s, h in zip(ss, heads)]
                dss = [(pr * (dp - h[3])).astype(BF16) for pr, dp, h in zip(prs, dps, heads)]
                pbs = [pr.astype(BF16) for pr in prs]
                dqs = [jnp.dot(ds, k, preferred_element_type=F32) for ds in dss]
                dks = [lax.dot_general(ds, h[1], _TN, preferred_element_type=F32) for ds, h in zip(dss, heads)]
                dvs = [lax.dot_general(pb, h[2], _TN, preferred_element_type=F32) for pb, h in zip(pbs, heads)]
                dq = [x if y is None else y + x for x, y in zip(dqs, dq)]
                dk_ref[lo:hi, :] += functools.reduce(lambda a, b: a + b, dks)
                dv_ref[lo:hi, :] += functools.reduce(lambda a, b: a + b, dvs)
            for j, (sl, *_) in enumerate(heads):
                dq_ref[:, sl] = dq[j]

        pl.when(i < nct)(lambda: tile(ctx))
        pl.when(i >= nct)(lambda: tile(t))

    def at(kv, g, i):
        return lambda: (pl.program_id(0) == kv) & (pl.program_id(1) == g) & (pl.program_id(2) == i)

    rn = 0 if rider is None else rider.n
    per_kv = GROUP // hps
    qspec = pl.BlockSpec((TM, hps * HD), lambda kv, g, i: (i, kv * per_kv + g))
    kvspec = pl.BlockSpec((t, HD), lambda kv, g, i: (0, kv))
    lspec = pl.BlockSpec((None, TM, hps), lambda kv, g, i: (kv * per_kv + g, i, 0))
    return pl.pallas_call(
        _with_rider(body, 6, 3, rider, at(0, 0, 0), at(NKV - 1, 0, 0), at(NKV - 1, per_kv - 1, nt - 1)),
        name=name, grid=(NKV, per_kv, nt),
        in_specs=[qspec, kvspec, pl.BlockSpec((t, HD), lambda kv, g, i: (0, vblk + kv)), qspec, lspec, qspec]
        + [_HBM] * rn,
        out_specs=(qspec, kvspec, kvspec) + (_HBM,) * rn,
        out_shape=(jax.ShapeDtypeStruct((t, NH * HD), F32), jax.ShapeDtypeStruct((t, NKV * HD), F32),
                   jax.ShapeDtypeStruct((t, NKV * HD), F32)) + (() if rider is None else tuple(rider.out_shape)),
        scratch_shapes=[] if rider is None else rider.scratch,
        compiler_params=_cparams(3, 48),
    )(qr, kr, p, o, lse, do, *(() if rider is None else rider.xs))


def _decay_fwd(p, wd, bd, name):
    t = p.shape[0]

    def body(r_ref, w_ref, b_ref, z_ref, bc_ref):
        z = jnp.dot(r_ref[...].astype(BF16), w_ref[...].astype(BF16), preferred_element_type=F32) + b_ref[...]
        z_ref[...] = z
        la = (jnp.minimum(z, 0.0) - jnp.log(1.0 + jnp.exp(-jnp.abs(z)))) / GLA_TAU
        half = GH * GDK
        bc_ref[:, 0:half] = _chunk_sums(la[:, 0:half], False)
        bc_ref[:, half:] = _chunk_sums(la[:, half:], True)

    row = pl.BlockSpec((TM, D), lambda i: (i, 0))
    return pl.pallas_call(
        body, name=name, grid=(t // TM,),
        in_specs=[_pcol("r", R_PAD), pl.BlockSpec((R_PAD, D), lambda i: (0, 0)), pl.BlockSpec((1, D), lambda i: (0, 0))],
        out_specs=(row, row),
        out_shape=(jax.ShapeDtypeStruct((t, D), F32), jax.ShapeDtypeStruct((t, D), F32)),
        compiler_params=_cparams(1, 32),
    )(p, wd, bd)


def _chunk_order(s, ncc, nc, rev):
    if not rev:
        return s
    return jnp.where(s < ncc, ncc - 1 - s, nc - 1 - (s - ncc))


GLA_CPS = TM // CH


class _Chain:
    def __init__(self, rev, d, h, sub, refs):
        self.rev, self.d, self.h, self.sub, self.refs = rev, d, h, sub, refs
        self.rows, self.k, self.v = slice(sub * CH, (sub + 1) * CH), _hk(h), _hv(h)
        self.last = sub * CH + (0 if rev else CH - 1)


def _gla_chains(dirs, step, backward):
    return [_Chain(rev, d, h, step if rev == backward else GLA_CPS - 1 - step, refs)
            for d, (rev, refs) in enumerate(dirs) for h in range(GH)]


def _hk(h):
    return slice(h * GDK, (h + 1) * GDK)


def _hv(h):
    return slice(h * GDV, (h + 1) * GDV)


def _chunk_sums(x, from_end):
    r = lax.broadcasted_iota(jnp.int32, (CH, CH), 0)
    c = lax.broadcasted_iota(jnp.int32, (CH, CH), 1)
    tri = ((c >= r) if from_end else (c <= r)).astype(F32)
    return jnp.concatenate([jnp.dot(tri, x[lo:lo + CH], preferred_element_type=F32, precision=HIGHEST)
                            for lo in range(0, x.shape[0], CH)], axis=0)


def _gla_factors(qs, ks, bcs, bls, revs):
    r = lax.broadcasted_iota(jnp.int32, (CH, CH), 0)
    c = lax.broadcasted_iota(jnp.int32, (CH, CH), 1)
    keeps = [(c >= r) if rev else (c <= r) for rev in revs]
    qs, ks = [q.astype(F32) for q in qs], [k.astype(F32) for k in ks]
    qts = [q * GLA_SCALE * jnp.exp(bc) for q, bc in zip(qs, bcs)]
    kts = [k * jnp.exp(-bc) for k, bc in zip(ks, bcs)]
    khs = [k * jnp.exp(bl - bc) for k, bl, bc in zip(ks, bls, bcs)]
    gls = [jnp.exp(bl) for bl in bls]
    return qts, kts, gls, khs, keeps


def _gla_loads(ch):
    qs = [c.refs[0][c.rows, c.k] for c in ch]
    ks = [c.refs[1][c.rows, c.k] for c in ch]
    bcs = [c.refs[3][c.rows, c.k] for c in ch]
    bls = [c.refs[3][c.last:c.last + 1, c.k] for c in ch]
    return qs, ks, bcs, bls


_NT = (((1,), (1,)), ((), ()))
_TN = (((0,), (0,)), ((), ()))


def _gla_specs(ncs, ns, rev, backward):
    def idx(s):
        return _chunk_order((ns - 1 - s) if backward else s, ncs, ns, rev)

    wk, wv = GH * GDK, GH * GDV
    qb, kb, vb = OFF["gq"] // wk, OFF["gk"] // wk, OFF["gv"] // wv
    lab = 1 if rev else 0
    q = pl.BlockSpec((TM, wk), lambda s: (idx(s), qb))
    k = pl.BlockSpec((TM, wk), lambda s: (idx(s), kb))
    v = pl.BlockSpec((TM, wv), lambda s: (idx(s), vb))
    la = pl.BlockSpec((TM, wk), lambda s: (idx(s), lab))
    o = pl.BlockSpec((TM, wv), lambda s: (idx(s), 0))
    dk = pl.BlockSpec((TM, wk), lambda s: (idx(s), 0))
    st = pl.BlockSpec((GLA_CPS, GH, GDV, GDK), lambda s: (idx(s), 0, 0, 0))
    return q, k, v, la, o, dk, st


def _gla_fwd(p, la, ncs, name):
    t = p.shape[0]
    nc, ns = t // CH, t // TM
    specs = [_gla_specs(ncs, ns, rev, False) for rev in (False, True)]

    def body(qf, kf, vf, laf, qb_, kb_, vb_, lab, of, stf, ob, stb, s_scr):
        @pl.when(pl.program_id(0) == 0)
        def _():
            s_scr[...] = jnp.zeros_like(s_scr)

        dirs = ((False, (qf, kf, vf, laf, of, stf)), (True, (qb_, kb_, vb_, lab, ob, stb)))
        for step in range(GLA_CPS):
            ch = _gla_chains(dirs, step, False)
            qts, kts, gls, khs, keeps = _gla_factors(*_gla_loads(ch), [c.rev for c in ch])
            sts = [s_scr[c.d, c.h] for c in ch]
            for c, st in zip(ch, sts):
                c.refs[5][c.sub, c.h] = st.astype(BF16)
            vbs = [c.refs[2][c.rows, c.v].astype(BF16) for c in ch]
            qbs = [qt.astype(BF16) for qt in qts]
            a_s = [jnp.where(keep, lax.dot_general(qb, kt.astype(BF16), _NT, preferred_element_type=F32), 0.0)
                   for keep, qb, kt in zip(keeps, qbs, kts)]
            inter = [lax.dot_general(qb, st.astype(BF16), _NT, preferred_element_type=F32) for qb, st in zip(qbs, sts)]
            intra = [jnp.dot(a.astype(BF16), vb, preferred_element_type=F32) for a, vb in zip(a_s, vbs)]
            for c, x, y in zip(ch, inter, intra):
                c.refs[4][c.rows, c.v] = (x + y).astype(BF16)
            upd = [lax.dot_general(vb, kh.astype(BF16), _TN, preferred_element_type=F32) for vb, kh in zip(vbs, khs)]
            for c, st, gl, u in zip(ch, sts, gls, upd):
                s_scr[c.d, c.h] = st * gl + u

    o_shape = jax.ShapeDtypeStruct((t, GH * GDV), BF16)
    st_shape = jax.ShapeDtypeStruct((nc, GH, GDV, GDK), BF16)
    return pl.pallas_call(
        body, name=name, grid=(ns,),
        in_specs=[sp for s_ in specs for sp in s_[:4]],
        out_specs=tuple(sp for s_ in specs for sp in (s_[4], s_[6])),
        out_shape=(o_shape, st_shape, o_shape, st_shape),
        scratch_shapes=[pltpu.VMEM((2, GH, GDV, GDK), F32)], compiler_params=_cparams(1, 32),
    )(p, p, p, la, p, p, p, la)


def _gla_bwd(p, la, do, stf, stb, ncs, name):
    t = p.shape[0]
    ns = t // TM
    specs = [_gla_specs(ncs, ns, rev, True) for rev in (False, True)]

    def mm(xs, ys, dims=None):
        if dims is None:
            return [jnp.dot(x, y, preferred_element_type=F32) for x, y in zip(xs, ys)]
        return [lax.dot_general(x, y, dims, preferred_element_type=F32) for x, y in zip(xs, ys)]

    def body(*refs):
        ins_f, ins_b, outs_f, outs_b, ds_scr = refs[0:6], refs[6:12], refs[12:16], refs[16:20], refs[20]

        @pl.when(pl.program_id(0) == 0)
        def _():
            ds_scr[...] = jnp.zeros_like(ds_scr)

        dirs = ((False, (*ins_f, *outs_f)), (True, (*ins_b, *outs_b)))
        row = lax.broadcasted_iota(jnp.int32, (CH, GDK), 0)
        for step in range(GLA_CPS):
            ch = _gla_chains(dirs, step, True)
            revs = [c.rev for c in ch]
            loads = _gla_loads(ch)
            bcs = loads[2]
            qts, kts, gls, khs, keeps = _gla_factors(*loads, revs)
            stvs = [c.refs[5][c.sub, c.h] for c in ch]
            dsns = [ds_scr[c.d, c.h] for c in ch]
            dsbs = [x.astype(BF16) for x in dsns]
            vbs = [c.refs[2][c.rows, c.v].astype(BF16) for c in ch]
            dobs = [c.refs[4][c.rows, c.v].astype(BF16) for c in ch]
            qbs, kbs = [x.astype(BF16) for x in qts], [x.astype(BF16) for x in kts]
            a_s = [jnp.where(keep, x, 0.0).astype(BF16) for keep, x in zip(keeps, mm(qbs, kbs, _NT))]
            das = [jnp.where(keep, x, 0.0).astype(BF16) for keep, x in zip(keeps, mm(dobs, vbs, _NT))]
            dqts = [x + y for x, y in zip(mm(dobs, stvs), mm(das, kbs))]
            dkhs = mm(vbs, dsbs)
            dkts = [x + dkh * gl for x, dkh, gl in zip(mm(das, qbs, _TN), dkhs, gls)]
            for c, x, y in zip(ch, mm(a_s, dobs, _TN), mm([kh.astype(BF16) for kh in khs], dsbs, _NT)):
                c.refs[8][c.rows, c.v] = (x + y).astype(BF16)
            for c, x, dsn, gl in zip(ch, mm(dobs, qbs, _TN), dsns, gls):
                ds_scr[c.d, c.h] = x + dsn * gl
            dgls = [jnp.sum(st.astype(F32) * dsn, axis=0, keepdims=True) + jnp.sum(dkh * kt, axis=0, keepdims=True)
                    for st, dsn, dkh, kt in zip(stvs, dsns, dkhs, kts)]
            dbcs = [dqt * qt - dkt * kt + jnp.where(row == (0 if rev else CH - 1), dgl * gl, 0.0)
                    for rev, dqt, qt, dkt, kt, dgl, gl in zip(revs, dqts, qts, dkts, kts, dgls, gls)]
            for c, dbc, dqt, dkt, bc in zip(ch, dbcs, dqts, dkts, bcs):
                c.refs[9][c.rows, c.k] = dbc
                c.refs[6][c.rows, c.k] = (dqt * (GLA_SCALE * jnp.exp(bc))).astype(BF16)
                c.refs[7][c.rows, c.k] = (dkt * jnp.exp(-bc)).astype(BF16)

    k_shape = jax.ShapeDtypeStruct((t, GH * GDK), BF16)
    v_shape = jax.ShapeDtypeStruct((t, GH * GDV), BF16)
    c_shape = jax.ShapeDtypeStruct((t, GH * GDK), F32)
    res = pl.pallas_call(
        body, name=name, grid=(ns,),
        in_specs=[sp for q_s, k_s, v_s, la_s, o_s, _, st_s in specs for sp in (q_s, k_s, v_s, la_s, o_s, st_s)],
        out_specs=tuple(sp for _, _, _, _, o_s, dk_s, _ in specs for sp in (dk_s, dk_s, o_s, dk_s)),
        out_shape=(k_shape, k_shape, v_shape, c_shape) * 2,
        scratch_shapes=[pltpu.VMEM((2, GH, GDV, GDK), F32)], compiler_params=_cparams(1, 32),
    )(p, p, p, la, do, stf, p, p, p, la, do, stb)
    return res[:4], res[4:]


def _gla_merge_bwd(gf, gb, z, p, wd, dp, name):
    t = p.shape[0]
    w2 = GH * GDK

    def body(dqf, dkf, dvf, dlf, dqb, dkb, dvb, dlb, z_ref, r_ref, w_ref, _, dp_ref, dr_ref, db_ref, dw_ref):
        i = pl.program_id(0)
        dp_ref[:, 0:D] = (dvf[...].astype(F32) + dvb[...].astype(F32)).astype(BF16)
        dp_ref[:, D:D + w2] = (dqf[...].astype(F32) + dqb[...].astype(F32)).astype(BF16)
        dp_ref[:, D + w2:D + 2 * w2] = (dkf[...].astype(F32) + dkb[...].astype(F32)).astype(BF16)
        zv = z_ref[...]
        dlf_, dlb_ = _chunk_sums(dlf[...], True), _chunk_sums(dlb[...], False)
        dz = jnp.concatenate([dlf_, dlb_], axis=1) * (_sigmoid(-zv) / GLA_TAU)
        dzb = dz.astype(BF16)
        dr_ref[...] = lax.dot_general(dzb, w_ref[...].astype(BF16), _NT, preferred_element_type=F32).astype(BF16)

        @pl.when(i == 0)
        def _():
            db_ref[...] = jnp.zeros_like(db_ref)
            dw_ref[...] = jnp.zeros_like(dw_ref)

        db_ref[...] += jnp.sum(dz, axis=0, keepdims=True)
        dw_ref[...] += lax.dot_general(r_ref[...].astype(BF16), dzb, _TN, preferred_element_type=F32)

    half = pl.BlockSpec((TM, w2), lambda i: (i, 0))
    row = pl.BlockSpec((TM, D), lambda i: (i, 0))
    wspec = pl.BlockSpec((R_PAD, D), lambda i: (0, 0))
    return pl.pallas_call(
        body, name=name, grid=(t // TM,),
        in_specs=[half, half, row, half, half, half, row, half, row, _pcol("r", R_PAD), wspec, _HBM],
        out_specs=(_dp_spec("gla"), pl.BlockSpec((TM, R_PAD), lambda i: (i, 0)),
                   pl.BlockSpec((8, D), lambda i: (0, 0)), wspec),
        out_shape=(jax.ShapeDtypeStruct(dp.shape, dp.dtype), jax.ShapeDtypeStruct((t, R_PAD), BF16),
                   jax.ShapeDtypeStruct((8, D), F32), jax.ShapeDtypeStruct((R_PAD, D), F32)),
        input_output_aliases={11: 0}, compiler_params=_cparams(1, 40),
    )(*gf, *gb, z, p, wd, dp)


def _dp_tail(dk, dv, dr, dp, name):
    t = dk.shape[0]
    wk = NKV * HD

    def body(dk_ref, dv_ref, dr_ref, _, dp_ref):
        dp_ref[:, 0:wk] = dk_ref[...]
        dp_ref[:, wk:2 * wk] = dv_ref[...].astype(BF16)
        dp_ref[:, 2 * wk:2 * wk + R_PAD] = dr_ref[...]
        dp_ref[:, 2 * wk + R_PAD:] = jnp.zeros((rt, DP_BLOCKS["tail"][1] - 2 * wk - R_PAD), BF16)

    rt = _big_rows(t)
    kv = pl.BlockSpec((rt, wk), lambda i: (i, 0))
    return pl.pallas_call(
        body, name=name, grid=(t // rt,),
        in_specs=[kv, kv, pl.BlockSpec((rt, R_PAD), lambda i: (i, 0)), _HBM], out_specs=_dp_spec("tail", rt),
        out_shape=jax.ShapeDtypeStruct(dp.shape, dp.dtype), input_output_aliases={3: 0},
        compiler_params=_cparams(1, 32),
    )(dk, dv, dr, dp)


def _branch_fwd(att, of, ob, p, gla_g, name):
    t = p.shape[0]

    def body(att_ref, of_ref, ob_ref, za_ref, zg_ref, g_ref, yb_ref, yc_ref):
        za = za_ref[...].astype(F32)
        yb_ref[...] = (att_ref[...].astype(F32) * (za * _sigmoid(za))).astype(BF16)
        for h in range(GH):
            sl = slice(h * GDV, (h + 1) * GDV)
            o = of_ref[:, sl].astype(F32) + ob_ref[:, sl].astype(F32)
            n = o * lax.rsqrt(jnp.mean(o * o, axis=-1, keepdims=True) + EPS) * g_ref[...]
            zh = zg_ref[:, sl].astype(F32)
            yc_ref[:, sl] = (n * (zh * _sigmoid(zh))).astype(BF16)

    rt = _big_rows(t)
    row = pl.BlockSpec((rt, D), lambda i: (i, 0))
    return pl.pallas_call(
        body, name=name, grid=(t // rt,),
        in_specs=[row, row, row, _pcol("z_attn", D, rt), _pcol("zg", D, rt), pl.BlockSpec((1, GDV), lambda i: (0, 0))],
        out_specs=(row, row),
        out_shape=(jax.ShapeDtypeStruct((t, D), BF16), jax.ShapeDtypeStruct((t, D), BF16)),
        compiler_params=_cparams(1, 40),
    )(att, of, ob, p, p, gla_g)


def _branch_bwd(dyb, dyc, att, of, ob, p, gla_g, dp, name):
    t = p.shape[0]

    def body(dyb_ref, dyc_ref, att_ref, of_ref, ob_ref, za_ref, zg_ref, g_ref, _, datt_ref, do_ref, dp_ref, dg_ref):
        i = pl.program_id(0)

        @pl.when(i == 0)
        def _():
            dg_ref[...] = jnp.zeros_like(dg_ref)

        za, dyb = za_ref[...].astype(F32), dyb_ref[...].astype(F32)
        sa = _sigmoid(za)
        datt_ref[...] = (dyb * (za * sa)).astype(BF16)
        dp_ref[:, 0:D] = (dyb * att_ref[...].astype(F32) * (sa * (1.0 + za * (1.0 - sa)))).astype(BF16)
        g = g_ref[...]
        for h in range(GH):
            sl = slice(h * GDV, (h + 1) * GDV)
            o = of_ref[:, sl].astype(F32) + ob_ref[:, sl].astype(F32)
            r = lax.rsqrt(jnp.mean(o * o, axis=-1, keepdims=True) + EPS)
            oh = o * r
            zh, dyc = zg_ref[:, sl].astype(F32), dyc_ref[:, sl].astype(F32)
            sg = _sigmoid(zh)
            dn = dyc * (zh * sg)
            dp_ref[:, D + h * GDV:D + (h + 1) * GDV] = (dyc * (oh * g) * (sg * (1.0 + zh * (1.0 - sg)))).astype(BF16)
            doh = dn * g
            do_ref[:, sl] = (r * (doh - oh * jnp.mean(doh * oh, axis=-1, keepdims=True))).astype(BF16)
            dg_ref[...] += jnp.sum(dn * oh, axis=0, keepdims=True)

    rt = _big_rows(t)
    row = pl.BlockSpec((rt, D), lambda i: (i, 0))
    return pl.pallas_call(
        body, name=name, grid=(t // rt,),
        in_specs=[row, row, row, row, row, _pcol("z_attn", D, rt), _pcol("zg", D, rt),
                  pl.BlockSpec((1, GDV), lambda i: (0, 0)), _HBM],
        out_specs=(row, row, _dp_spec("branch", rt), pl.BlockSpec((8, GDV), lambda i: (0, 0))),
        out_shape=(jax.ShapeDtypeStruct((t, D), BF16), jax.ShapeDtypeStruct((t, D), BF16),
                   jax.ShapeDtypeStruct(dp.shape, dp.dtype), jax.ShapeDtypeStruct((8, GDV), F32)),
        input_output_aliases={8: 2}, compiler_params=_cparams(1, 56),
    )(dyb, dyc, att, of, ob, p, p, gla_g, dp)


def _merge_fwd(bra, brb, brc, p, b_gate, name):
    t = p.shape[0]
    mgb = OFF["mg"] // D

    def body(a_ref, b_ref, c_ref, ga_ref, gb_ref, gc_ref, bg_ref, m_ref):
        m_ref[...] = (_sigmoid(ga_ref[...].astype(F32) + bg_ref[:, 0:D]) * a_ref[...].astype(F32)
                      + _sigmoid(gb_ref[...].astype(F32) + bg_ref[:, D:2 * D]) * b_ref[...].astype(F32)
                      + _sigmoid(gc_ref[...].astype(F32) + bg_ref[:, 2 * D:3 * D]) * c_ref[...].astype(F32)).astype(BF16)

    rt = _big_rows(t)
    row = pl.BlockSpec((rt, D), lambda i: (i, 0))
    gates = [pl.BlockSpec((rt, D), functools.partial(lambda i, b: (i, b), b=mgb + j)) for j in range(3)]
    return pl.pallas_call(
        body, name=name, grid=(t // rt,),
        in_specs=[row, row, row, *gates, pl.BlockSpec((1, 3 * D), lambda i: (0, 0))],
        out_specs=row, out_shape=jax.ShapeDtypeStruct((t, D), BF16), compiler_params=_cparams(1, 40),
    )(bra, brb, brc, p, p, p, b_gate)


def _merge_bwd(dm, bra, brb, brc, p, b_gate, name):
    t = p.shape[0]
    mgb = OFF["mg"] // D

    def body(dm_ref, a_ref, b_ref, c_ref, ga_ref, gb_ref, gc_ref, bg_ref, da_ref, db_ref, dc_ref, dmg_ref, dbg_ref):
        i = pl.program_id(0)

        @pl.when(i == 0)
        def _():
            dbg_ref[...] = jnp.zeros_like(dbg_ref)

        dm = dm_ref[...].astype(F32)
        for j, (br_ref, g_ref, d_ref) in enumerate(((a_ref, ga_ref, da_ref), (b_ref, gb_ref, db_ref), (c_ref, gc_ref, dc_ref))):
            sl = slice(j * D, (j + 1) * D)
            g = _sigmoid(g_ref[...].astype(F32) + bg_ref[:, sl])
            d_ref[...] = (dm * g).astype(BF16)
            dmg = dm * br_ref[...].astype(F32) * (g * (1.0 - g))
            dmg_ref[:, sl] = dmg.astype(BF16)
            dbg_ref[:, sl] += jnp.sum(dmg, axis=0, keepdims=True)

    rt = _big_rows(t)
    row = pl.BlockSpec((rt, D), lambda i: (i, 0))
    gates = [pl.BlockSpec((rt, D), functools.partial(lambda i, b: (i, b), b=mgb + j)) for j in range(3)]
    return pl.pallas_call(
        body, name=name, grid=(t // rt,),
        in_specs=[row, row, row, row, *gates, pl.BlockSpec((1, 3 * D), lambda i: (0, 0))],
        out_specs=(row, row, row, _dp_spec("merge", rt), pl.BlockSpec((8, 3 * D), lambda i: (0, 0))),
        out_shape=(jax.ShapeDtypeStruct((t, D), BF16),) * 3 + (jax.ShapeDtypeStruct((t, NP), BF16),
                                                                jax.ShapeDtypeStruct((8, 3 * D), F32)),
        compiler_params=_cparams(1, 56),
    )(dm, bra, brb, brc, p, p, p, b_gate)


def _adam_update(ns, g_ref, w_ref, m_ref, v_ref, go_ref, d_ref, mo_ref, vo_ref):
    g = g_ref[0].astype(F32)
    for s in range(1, ns):
        g = g + g_ref[s].astype(F32)
    mn = ADAM_B1 * m_ref[...] + (1.0 - ADAM_B1) * g
    vn = ADAM_B2 * v_ref[...] + (1.0 - ADAM_B2) * jnp.square(g)
    m_hat = mn / (1.0 - ADAM_B1 ** ADAM_STEP)
    v_hat = vn / (1.0 - ADAM_B2 ** ADAM_STEP)
    go_ref[...] = g
    d_ref[...] = -ADAM_LR * (m_hat / (jnp.sqrt(v_hat) + ADAM_EPS) + ADAM_WD * w_ref[...])
    mo_ref[...] = mn
    vo_ref[...] = vn


def _adamw(gsrc, w, m, v, name):
    ns, nl, r, c = gsrc.shape
    gb = gsrc.dtype.itemsize

    def fits(rows, cols):
        lanes = -(-cols // LANE) * LANE
        return ns * rows * lanes * gb <= ADAM_SRC_BYTES and rows * lanes * 4 <= ADAM_ROW_BYTES

    tr, tc = r, c
    if not fits(r, c):
        rows = [cand for cand in range(16, r, 16) if r % cand == 0 and fits(cand, c)]
        cols = [cand for cand in range(LANE, c, LANE) if c % cand == 0 and fits(r, cand)]
        if rows:
            tr = rows[-1]
        else:
            tc = cols[-1]

    def body(*refs):
        _adam_update(ns, *refs)

    row = pl.BlockSpec((None, tr, tc), lambda l, i, j: (l, i, j))
    return pl.pallas_call(
        body, name=name, grid=(nl, r // tr, c // tc),
        in_specs=[pl.BlockSpec((ns, None, tr, tc), lambda l, i, j: (0, l, i, j)), row, row, row],
        out_specs=(row,) * 4, out_shape=(jax.ShapeDtypeStruct((nl, r, c), F32),) * 4,
        compiler_params=_cparams(3, 48),
    )(gsrc, w, m, v)


def _pair_sum(a, b, name):
    s, r, c = a.shape
    tc = _pick(c, (256, 128))

    def body(a_ref, b_ref, o_ref):
        o_ref[...] = (a_ref[...].astype(F32) + b_ref[...].astype(F32)).astype(BF16)

    blk = pl.BlockSpec((None, r, tc), lambda i, j: (i, 0, j))
    return pl.pallas_call(
        body, name=name, grid=(s, c // tc), in_specs=[blk, blk], out_specs=blk,
        out_shape=jax.ShapeDtypeStruct(a.shape, BF16), compiler_params=_cparams(2, 32),
    )(a, b)


def _adamw_small(items, name):
    k = len(items)

    def body(*refs):
        for j in range(k):
            _adam_update(items[j][0].shape[0], *refs[4 * j:4 * j + 4], *refs[4 * k + 4 * j:4 * k + 4 * j + 4])

    out = pl.pallas_call(
        body, name=name,
        out_shape=tuple(jax.ShapeDtypeStruct(w.shape, F32) for _, w, _, _ in items for _ in range(4)),
    )(*[a for item in items for a in item])
    return [out[4 * j:4 * j + 4] for j in range(k)]


def _rope_tables(ctx, seq):
    n_rows = seq // GRID_W
    pairs = HD // 4
    row = jnp.repeat(jnp.arange(n_rows, dtype=F32), GRID_W)
    col = jnp.tile(jnp.arange(GRID_W, dtype=F32), n_rows)
    freqs = ROPE_THETA ** (-jnp.arange(pairs, dtype=F32) * 2.0 / (HD // 2))
    ar, ac = row[:, None] * freqs, col[:, None] * freqs
    cos_l = jnp.concatenate([jnp.cos(ar), jnp.cos(ar), jnp.cos(ac), jnp.cos(ac)], axis=1)
    sin_l = jnp.concatenate([-jnp.sin(ar), jnp.sin(ar), -jnp.sin(ac), jnp.sin(ac)], axis=1)
    cos_t = jnp.concatenate([jnp.ones((ctx, HD), F32), cos_l], axis=0)
    sin_t = jnp.concatenate([jnp.zeros((ctx, HD), F32), sin_l], axis=0)
    return cos_t, sin_t


def _to_proj_layout(wt):
    parts = [wt[s:s + wd] for _, s, wd in _SEGS]
    used = sum(wd for _, _, wd in _SEGS)
    parts.append(jnp.zeros((NP - used, wt.shape[1]), wt.dtype))
    return jnp.concatenate(parts, axis=0)


def _from_proj_layout(g):
    order = sorted(_SEGS, key=lambda sg: sg[1])
    return jnp.concatenate([g[OFF[n]:OFF[n] + wd] for n, _, wd in order], axis=0)


def _row0(a):
    return a[..., 0, :]


def kernel(x, c, ctx, c_ctx, w_ada, b_ada, g_pre, g_post, w_in, conv_w, q_norm_g, k_norm_g, w_decay_fwd, b_decay_fwd, w_decay_bwd, b_decay_bwd, gla_norm_g, w_br_conv, w_br_attn, w_br_gla, b_gate, w_out, loss_target, m_c_ctx, m_w_ada, m_b_ada, m_g_pre, m_g_post, m_w_in, m_conv_w, m_q_norm_g, m_k_norm_g, m_w_decay_fwd, m_b_decay_fwd, m_w_decay_bwd, m_b_decay_bwd, m_gla_norm_g, m_w_br_conv, m_w_br_attn, m_w_br_gla, m_b_gate, m_w_out, v_c_ctx, v_w_ada, v_b_ada, v_g_pre, v_g_post, v_w_in, v_conv_w, v_q_norm_g, v_k_norm_g, v_w_decay_fwd, v_b_decay_fwd, v_w_decay_bwd, v_b_decay_bwd, v_gla_norm_g, v_w_br_conv, v_w_br_attn, v_w_br_gla, v_b_gate, v_w_out):
    seq, n_ctx = x.shape[1], ctx.shape[1]
    assert n_ctx % TM == 0 and seq % TM == 0 and seq % GRID_W == 0
    t = n_ctx + seq
    nct = n_ctx // TM
    dev = 4 * lax.axis_index("x") + 2 * lax.axis_index("y") + lax.axis_index("c")
    ada_w = w_ada.shape[2]
    in_w = w_in.shape[2]
    br_r = w_br_conv.shape[1]

    def in_t(a, l):
        return a.transpose(2, 0, 1)[:, l, :]

    wb = [w.astype(BF16) for w in (w_ada, w_br_conv, w_br_attn, w_br_gla, w_out)]
    wall = _all_gather([wb[0][0], in_t(w_in, 0).astype(BF16), conv_w, w_decay_fwd, w_decay_bwd],
                       "gather_first")
    later_square = _GatherRider([wb[1], wb[2], wb[3], wb[4]])
    later_in = _GatherRider([in_t(w_in, 1).astype(BF16), wb[0][1]])

    def full_small(g):
        return g.transpose(1, 2, 0, 3).reshape(DEPTH, g.shape[2], NDEV * g.shape[3])

    def full_in(g):
        return _to_proj_layout(g.reshape(IN_WIDTH, D))

    def full_ada(g):
        return g.transpose(1, 0, 2).reshape(D, 3 * D)

    w_ada_f = [full_ada(wall[0]), None]
    wp = [full_in(wall[1]), None]
    conv_f, wdf_f, wdb_f = full_small(wall[2]), full_small(wall[3]), full_small(wall[4])

    cos_t, sin_t = _rope_tables(n_ctx, seq)
    cc = jnp.concatenate([c_ctx[None, :], c.reshape(1, D), jnp.zeros((6, D), F32)], axis=0)
    silu_cc, dsilu_cc = _ada_in(cc)

    conv8, wd_pad, bd = [], [], []
    for l in range(DEPTH):
        conv8.append(jnp.concatenate([conv_f[l], jnp.zeros((5, D), F32)], axis=0))
        zr = jnp.zeros((GLA_RANK, GH * GDK), F32)
        wd_pad.append(jnp.concatenate([jnp.concatenate([wdf_f[l], zr], axis=1), jnp.concatenate([zr, wdb_f[l]], axis=1),
                                       jnp.zeros((R_PAD - 2 * GLA_RANK, D), F32)], axis=0))
        bd.append(jnp.concatenate([b_decay_fwd[l], b_decay_bwd[l]])[None, :])

    xs = jnp.concatenate([ctx[0], x[0]], axis=0)
    saved = []
    for l in range(DEPTH):
        n = f"l{l}_"
        mod = _mm(silu_cc, w_ada_f[l], n + "mod", bias=b_ada[l][None, :])
        mod3 = mod[0:2].reshape(2, 3, D)
        h = _prenorm_fwd(xs, g_pre[l][None, :], mod3, nct, n + "prenorm")
        if l == 0:
            p, *got = _mm(h, wp[l], n + "proj", tb=True, out_dtype=BF16, tm=t // 2, rider=later_square)
            w_brs_f = [g.transpose(1, 0, 2, 3).reshape(DEPTH, D, D) for g in got]
        else:
            p = _mm(h, wp[l], n + "proj", tb=True, out_dtype=BF16, tm=t // 2)
        cv, ya = _conv_fwd(p, conv8[l], nct, n + "conv")
        qr, kr = _qk_prep_fwd(p, q_norm_g[l][None, :], k_norm_g[l][None, :], cos_t, sin_t, n + "qk_prep")
        att, lse, *got = _attn_fwd(qr, kr, p, nct, n + "attn", rider=later_in if l == 0 else None)
        if l == 0:
            wp[1], w_ada_f[1] = full_in(got[0]), full_ada(got[1])
        z, la = _decay_fwd(p, wd_pad[l], bd[l], n + "decay")
        of, stf, ob, stb = _gla_fwd(p, la, nct, n + "gla")
        yb, yc = _branch_fwd(att, of, ob, p, gla_norm_g[l][None, :], n + "branch")
        bra = _mm(ya, w_brs_f[0][l], n + "br_conv", out_dtype=BF16)
        brb = _mm(yb, w_brs_f[1][l], n + "br_attn", out_dtype=BF16)
        brc = _mm(yc, w_brs_f[2][l], n + "br_gla", out_dtype=BF16)
        mm_ = _merge_fwd(bra, brb, brc, p, b_gate[l][None, :], n + "merge")
        out = _mm(mm_, w_brs_f[3][l], n + "out")
        x_new = _post_fwd(xs, out, g_post[l][None, :], mod3, nct, n + "post")
        saved.append(dict(x=xs, mod3=mod3, h=h, p=p, cv=cv, ya=ya, qr=qr, kr=kr, att=att, lse=lse, z=z, la=la, of=of, ob=ob,
                          stf=stf, stb=stb, yb=yb, yc=yc, bra=bra, brb=brb, brc=brc, m=mm_, out=out))
        xs = x_new

    dx, sq = _loss_grad(xs, loss_target[0], nct, "loss")
    loss = lax.psum(0.5 * sq[0, 0] / D, ("x", "y", "c"))

    gw = {k: [None] * DEPTH for k in ("w_in", "br_conv", "br_attn", "br_gla", "out", "b_gate", "g_pre", "g_post",
                                      "conv_w", "qg", "kg", "wd", "bdec", "gla_g", "dmod")}
    dctx = []

    def in_slots(l):
        return _from_proj_layout(gw["w_in"][l]).reshape(NDEV, in_w, D)

    def br_slots(l):
        return [gw[k][l].reshape(NDEV, br_r, D) for k in ("br_conv", "br_attn", "br_gla", "out")]

    for l in reversed(range(DEPTH)):
        n = f"l{l}_b_"
        s = saved[l]
        p = s["p"]
        d_out, dgt, gw["g_post"][l] = _post_bwd(dx, s["out"], g_post[l][None, :], s["mod3"], nct, n + "post")
        dm = _mm(d_out, w_brs_f[3][l], n + "dm", tb=True, out_dtype=BF16)
        gw["out"][l] = _mm(s["m"], d_out, n + "dw_out", ta=True, out_dtype=BF16)
        dbra, dbrb, dbrc, dp, gw["b_gate"][l] = _merge_bwd(dm, s["bra"], s["brb"], s["brc"], p, b_gate[l][None, :], n + "merge")
        dya = _mm(dbra, w_brs_f[0][l], n + "dya", tb=True, out_dtype=BF16)
        dyb = _mm(dbrb, w_brs_f[1][l], n + "dyb", tb=True, out_dtype=BF16)
        dyc = _mm(dbrc, w_brs_f[2][l], n + "dyc", tb=True, out_dtype=BF16)
        gw["br_conv"][l] = _mm(s["ya"], dbra, n + "dw_conv", ta=True, out_dtype=BF16)
        gw["br_attn"][l] = _mm(s["yb"], dbrb, n + "dw_attn", ta=True, out_dtype=BF16)
        gw["br_gla"][l] = _mm(s["yc"], dbrc, n + "dw_gla", ta=True, out_dtype=BF16)
        dcv, dp = _conv_bwd_a(dya, p, s["cv"], dp, n + "conv_a")
        dp, gw["conv_w"][l] = _conv_bwd_b(dcv, p, conv8[l], nct, dp, n + "conv_b")
        datt, dgo, dp, gw["gla_g"][l] = _branch_bwd(dyb, dyc, s["att"], s["of"], s["ob"], p, gla_norm_g[l][None, :], dp, n + "branch")
        ex1 = _ExchangeRider([in_slots(DEPTH - 1)] + br_slots(DEPTH - 1)) if l == 0 else None
        dqr, dkr, dv, *got = _attn_bwd(s["qr"], s["kr"], p, s["att"], s["lse"], datt, nct, n + "attn", rider=ex1)
        if l == 0:
            recv_in1, recv_br1 = got[0], got[1:]
        dp, dk, gw["qg"][l], gw["kg"][l] = _qk_prep_bwd(dqr, dkr, p, q_norm_g[l][None, :], k_norm_g[l][None, :], cos_t, sin_t, dp, n + "qk_prep")
        gf, gb = _gla_bwd(p, s["la"], dgo, s["stf"], s["stb"], nct, n + "gla")
        dp, dr, gw["bdec"][l], gw["wd"][l] = _gla_merge_bwd(gf, gb, s["z"], p, wd_pad[l], dp, n + "gla_merge")
        dp = _dp_tail(dk, dv, dr, dp, n + "dp_tail")
        tk_in = t // 2 if t % 32 == 0 else None
        if l == 0:
            gw["w_in"][l], *recv_br0 = _mm(dp, s["h"], n + "dw_in", ta=True, out_dtype=BF16, tk=tk_in,
                                           rider=_ExchangeRider(br_slots(0)))
        else:
            gw["w_in"][l] = _mm(dp, s["h"], n + "dw_in", ta=True, out_dtype=BF16, tk=tk_in)
        if l == 0:
            core = lax.axis_index("c")
            halves = in_slots(0).reshape(NDEV // 2, 2, in_w, D)
            kept = lax.dynamic_index_in_dim(halves, core, axis=1, keepdims=False)
            sent = lax.dynamic_index_in_dim(halves, 1 - core, axis=1, keepdims=False)
            from_sibling, = _comm_alone(_SwapRider([sent]), n + "swap_dw_in")
            chip_sum = _pair_sum(kept, from_sibling, n + "chip_sum_dw_in")
            dh, recv_in0 = _mm(dp, wp[l], n + "dh", tk=NP // 4, rider=_ExchangeRider([chip_sum], chips_only=True))
        else:
            dh = _mm(dp, wp[l], n + "dh", tk=NP // 4)
        dx, dsh, dsc, gw["g_pre"][l] = _prenorm_bwd(dh, s["x"], dx, g_pre[l][None, :], s["mod3"], nct, n + "prenorm")
        dmod = jnp.stack([_row0(dsh), _row0(dsc), _row0(dgt)], axis=1).reshape(2, 3 * D)
        gw["dmod"][l] = dmod
        dmod8 = jnp.concatenate([dmod, jnp.zeros((6, 3 * D), F32)], axis=0)
        dctx.append(_mm(dmod8, w_ada_f[l], n + "dsilu", tb=True))
    grad_x = dx[n_ctx:][None]
    g_cctx = _cctx_grad(dctx[0], dctx[1], dsilu_cc)[0]

    def st2(name):
        return jnp.stack(gw[name])

    g_b_ada = jnp.stack([gw["dmod"][l][0] + gw["dmod"][l][1] for l in range(DEPTH)])
    g_bdf = jnp.stack([gw["bdec"][l][0, :GH * GDK] for l in range(DEPTH)])
    g_bdb = jnp.stack([gw["bdec"][l][0, GH * GDK:] for l in range(DEPTH)])
    g_wdf = jnp.stack([gw["wd"][l][0:GLA_RANK, :GH * GDK] for l in range(DEPTH)])
    g_wdb = jnp.stack([gw["wd"][l][GLA_RANK:2 * GLA_RANK, GH * GDK:] for l in range(DEPTH)])
    rep_grads = [g_cctx, g_b_ada, st2("g_pre")[:, 0], st2("g_post")[:, 0], st2("qg")[:, 0], st2("kg")[:, 0], g_bdf, g_bdb,
                 st2("gla_g")[:, 0], st2("b_gate")[:, 0]]
    rep_w = [c_ctx, b_ada, g_pre, g_post, q_norm_g, k_norm_g, b_decay_fwd, b_decay_bwd, gla_norm_g, b_gate]
    rep_m = [m_c_ctx, m_b_ada, m_g_pre, m_g_post, m_q_norm_g, m_k_norm_g, m_b_decay_fwd, m_b_decay_bwd, m_gla_norm_g, m_b_gate]
    rep_v = [v_c_ctx, v_b_ada, v_g_pre, v_g_post, v_q_norm_g, v_k_norm_g, v_b_decay_fwd, v_b_decay_bwd, v_gla_norm_g, v_b_gate]
    def two_d(a):
        return a.reshape(1, -1) if a.ndim == 1 else a

    def owner_slots(g):
        return g.reshape(DEPTH, g.shape[1], NDEV, g.shape[2] // NDEV).transpose(2, 0, 1, 3)

    n_rep = len(rep_grads)
    small = _comm_alone(_Riders([
        _GatherRider([two_d(g) for g in rep_grads] + [silu_cc[0:2], jnp.stack(gw["dmod"])]),
        _ExchangeRider([owner_slots(st2("conv_w")[:, 0:3]), owner_slots(g_wdf), owner_slots(g_wdb)])]),
        "exchange_small_grads")
    rep_src, (a_all, d_all), sh_src = small[:n_rep], small[n_rep:n_rep + 2], small[n_rep + 2:]
    sh_w = [conv_w, w_decay_fwd, w_decay_bwd]
    sh_m = [m_conv_w, m_w_decay_fwd, m_w_decay_bwd]
    sh_v = [v_conv_w, v_w_decay_fwd, v_w_decay_bwd]
    small_out = _adamw_small(
        [(g, two_d(w), two_d(m), two_d(v)) for g, w, m, v in zip(rep_src, rep_w, rep_m, rep_v)]
        + list(zip(sh_src, sh_w, sh_m, sh_v)), "adam_small")
    rep_g, rep_d, rep_nm, rep_nv = [[small_out[j][k].reshape(rep_w[j].shape) for j in range(n_rep)] for k in range(4)]
    sh_gr, sh_d, sh_nm, sh_nv = [[small_out[n_rep + j][k] for j in range(len(sh_w))] for k in range(4)]

    a_all = a_all.reshape(NDEV * 2, D)
    d_all = d_all.transpose(1, 0, 2, 3).reshape(DEPTH, NDEV * 2, 3 * D)
    g_ada = jnp.stack([_mm(a_all, lax.dynamic_slice_in_dim(d_all[l], dev * ada_w, ada_w, axis=1), f"dw_ada{l}",
                           ta=True, precise=True, tk=NDEV * 2) for l in range(DEPTH)])
    ada_g, ada_d, ada_nm, ada_nv = _adamw(g_ada[None], w_ada, m_w_ada, v_w_ada, "adam_ada")

    big_w = [w_br_conv, w_br_attn, w_br_gla, w_out]
    big_m = [m_w_br_conv, m_w_br_attn, m_w_br_gla, m_w_out]
    big_v = [v_w_br_conv, v_w_br_attn, v_w_br_gla, v_w_out]
    big_out = [_adamw(jnp.stack([recv_br0[j], recv_br1[j]], axis=1), big_w[j], big_m[j], big_v[j], f"adam_big{j}")
               for j in range(len(big_w))]
    in_out = [_adamw(r_[:, None], in_t(w_in, l)[None], in_t(m_w_in, l)[None], in_t(v_w_in, l)[None], f"adam_in{l}")
              for l, r_ in enumerate((recv_in0, recv_in1))]
    in_res = [jnp.stack([in_out[l][k][0] for l in range(DEPTH)], axis=1).transpose(1, 2, 0) for k in range(4)]
    big_g, big_d, big_nm, big_nv = [[in_res[k]] + [o[k] for o in big_out] for k in range(4)]

    def ordered(rep, ada, big, sh):
        c_ctx_, b_ada_, g_pre_, g_post_, qg_, kg_, bdf_, bdb_, glag_, bgate_ = rep
        w_in_, brc_, bra_, brg_, wout_ = big
        conv_, wdf_, wdb_ = sh
        return [c_ctx_, ada, b_ada_, g_pre_, g_post_, w_in_, conv_, qg_, kg_, wdf_, bdf_, wdb_, bdb_, glag_,
                brc_, bra_, brg_, bgate_, wout_]

    return (loss, grad_x,
            *ordered(rep_g, ada_g, big_g, sh_gr), *ordered(rep_d, ada_d, big_d, sh_d),
            *ordered(rep_nm, ada_nm, big_nm, sh_nm), *ordered(rep_nv, ada_nv, big_nv, sh_nv))
```

```python
import functools

import numpy as np
import jax
import jax.numpy as jnp
from jax import lax
from jax.experimental import pallas as pl
from jax.experimental.pallas import tpu as pltpu

F32, BF16 = jnp.float32, jnp.bfloat16
HIGHEST = lax.Precision.HIGHEST

D = 1024
DEPTH = 2
GRID_W = 64
NH, NKV, HD = 8, 2, 128
GROUP = NH // NKV
ROPE_THETA = 10000.0
ATTN_SCALE = HD ** -0.5
Q_FOLD = ATTN_SCALE * 1.4426950408889634
P_HALO = 16
GH, GDK, GDV = 4, 128, 256
GLA_RANK = 16
GLA_TAU = 16.0
CH = 64
GLA_SCALE = GDK ** -0.5
EPS = 1e-6
NDEV = 8
LANE = 128
TM = 256
BIG_ROWS = 544
ATTN_HEADS_PER_STEP = 4
ATTN_FWD_KEY_CHUNK = 2176
ATTN_BWD_KEY_CHUNK = 256
KEY_ALIGN = LANE

ADAM_LR, ADAM_B1, ADAM_B2, ADAM_EPS, ADAM_WD, ADAM_STEP = 0.001, 0.9, 0.999, 1e-08, 0.01, 10

_SEGS = (("a_b", 0, 1024), ("a_z", 3072, 1024), ("a_c", 1024, 1024), ("a_x", 2048, 1024),
         ("z_attn", 5632, 1024), ("zg", 8736, 1024), ("gv", 7680, 1024), ("gq", 6656, 512), ("gk", 7168, 512),
         ("q", 4096, 1024), ("mg", 9760, 3072), ("k", 5120, 256), ("v", 5376, 256), ("r", 8704, 32))
DP_BLOCKS = {"conv_a": ("a_b", 2048), "conv_b": ("a_c", 2048), "branch": ("z_attn", 2048), "gla": ("gv", 2048),
             "q": ("q", 1024), "merge": ("mg", 3072), "tail": ("k", 1024)}
IN_WIDTH = 12832
NP = 13312
OFF = {}
_o = 0
for _n, _s, _w in _SEGS:
    OFF[_n] = _o
    _o += _w
R_PAD = 128


def _cparams(ngrid, vmem_mb):
    return pltpu.CompilerParams(dimension_semantics=("arbitrary",) * ngrid, vmem_limit_bytes=vmem_mb << 20)


def _pick(n, cands):
    for c in cands:
        if n % c == 0:
            return c
    return n


def _sigmoid(x):
    return 1.0 / (1.0 + jnp.exp(-x))


ADAM_SRC_BYTES = 8 << 20
ADAM_ROW_BYTES = 2 << 20


def _all_gather(xs, name):
    return _comm_alone(_GatherRider(xs), name)


_HBM = pl.BlockSpec(memory_space=pl.ANY)


class _Rider:
    def __init__(self, xs, out_shapes, remote_copies=NDEV - 1):
        self.xs, self.n = list(xs), len(xs)
        self.out_shape = [jax.ShapeDtypeStruct(s, x.dtype) for s, x in zip(out_shapes, xs)]
        self.scratch = [pltpu.SemaphoreType.DMA((remote_copies * self.n,)),
                        pltpu.SemaphoreType.DMA((remote_copies * self.n,)), pltpu.SemaphoreType.DMA((self.n,))]


class _GatherRider(_Rider):
    def __init__(self, xs):
        super().__init__(xs, [(NDEV,) + x.shape for x in xs])

    def _parts(self, x_refs, out_refs, sems):
        n = self.n
        send_sems, recv_sems, local_sems = sems
        mx, my, mc = lax.axis_index("x"), lax.axis_index("y"), lax.axis_index("c")
        me, sibling = (mx, my, mc), (mx, my, 1 - mc)
        chips = [(1 - mx, my), (mx, 1 - my), (1 - mx, 1 - my)]

        def slot(a, px, py, pc):
            return out_refs[a].at[4 * px + 2 * py + pc]

        def copy(k, a, block, to, own=False):
            return pltpu.make_async_remote_copy(
                src_ref=x_refs[a] if own else slot(a, *block), dst_ref=slot(a, *block),
                send_sem=send_sems.at[k * n + a], recv_sem=recv_sems.at[k * n + a],
                device_id=to, device_id_type=pl.DeviceIdType.MESH)

        mine = [pltpu.make_async_copy(x_refs[a], slot(a, *me), local_sems.at[a]) for a in range(n)]
        first = [copy(0, a, me, sibling, own=True) for a in range(n)]
        first += [copy(1 + j, a, me, (*chip, mc), own=True) for a in range(n) for j, chip in enumerate(chips)]
        landed = [copy(1 + j, a, (*chip, mc), me) for a in range(n) for j, chip in enumerate(chips)]
        passed = [copy(4 + j, a, (*chip, mc), sibling) for a in range(n) for j, chip in enumerate(chips)]
        from_sibling = [copy(0, a, sibling, me) for a in range(n)]
        from_sibling += [copy(4 + j, a, (*chip, 1 - mc), me) for a in range(n) for j, chip in enumerate(chips)]
        return mine, first, landed, passed, from_sibling

    def start(self, x_refs, out_refs, sems):
        mine, first, _, _, _ = self._parts(x_refs, out_refs, sems)
        for cp in mine + first:
            cp.start()

    def middle(self, x_refs, out_refs, sems):
        _, _, landed, passed, _ = self._parts(x_refs, out_refs, sems)
        for got, fwd in zip(landed, passed):
            got.wait_recv()
            fwd.start()

    def finish(self, x_refs, out_refs, sems):
        mine, first, _, passed, from_sibling = self._parts(x_refs, out_refs, sems)
        for cp in from_sibling:
            cp.wait_recv()
        for cp in first + passed:
            cp.wait_send()
        for cp in mine:
            cp.wait()


class _ExchangeRider(_Rider):
    def __init__(self, xs, chips_only=False):
        self.chips_only = chips_only
        super().__init__(xs, [x.shape for x in xs], 3 if chips_only else NDEV - 1)

    def _parts(self, x_refs, out_refs, sems):
        n = self.n
        send_sems, recv_sems, local_sems = sems
        mx, my, mc = lax.axis_index("x"), lax.axis_index("y"), lax.axis_index("c")
        me = 2 * mx + my if self.chips_only else 4 * mx + 2 * my + mc
        mine = [pltpu.make_async_copy(x_refs[a].at[me], out_refs[a].at[me], local_sems.at[a]) for a in range(n)]
        copies = []
        for a in range(n):
            for rel in range(1, 4 if self.chips_only else NDEV):
                bits = rel << 1 if self.chips_only else rel
                px = (1 - mx) if bits & 4 else mx
                py = (1 - my) if bits & 2 else my
                pc = (1 - mc) if bits & 1 else mc
                peer = 2 * px + py if self.chips_only else 4 * px + 2 * py + pc
                k = (rel - 1) * n + a
                copies.append(pltpu.make_async_remote_copy(
                    src_ref=x_refs[a].at[peer], dst_ref=out_refs[a].at[me],
                    send_sem=send_sems.at[k], recv_sem=recv_sems.at[k],
                    device_id=(px, py, pc), device_id_type=pl.DeviceIdType.MESH))
        return mine, copies

    def start(self, x_refs, out_refs, sems):
        mine, copies = self._parts(x_refs, out_refs, sems)
        for cp in mine + copies:
            cp.start()

    def middle(self, x_refs, out_refs, sems):
        pass

    def finish(self, x_refs, out_refs, sems):
        mine, copies = self._parts(x_refs, out_refs, sems)
        for cp in copies:
            cp.wait_recv()
        for cp in copies:
            cp.wait_send()
        for cp in mine:
            cp.wait()


class _SwapRider(_Rider):
    def __init__(self, xs):
        super().__init__(xs, [x.shape for x in xs], 1)

    def _parts(self, x_refs, out_refs, sems):
        send_sems, recv_sems, _ = sems
        sibling = (lax.axis_index("x"), lax.axis_index("y"), 1 - lax.axis_index("c"))
        return [pltpu.make_async_remote_copy(
            src_ref=x_refs[a], dst_ref=out_refs[a], send_sem=send_sems.at[a], recv_sem=recv_sems.at[a],
            device_id=sibling, device_id_type=pl.DeviceIdType.MESH) for a in range(self.n)]

    def start(self, x_refs, out_refs, sems):
        for cp in self._parts(x_refs, out_refs, sems):
            cp.start()

    def middle(self, x_refs, out_refs, sems):
        pass

    def finish(self, x_refs, out_refs, sems):
        copies = self._parts(x_refs, out_refs, sems)
        for cp in copies:
            cp.wait_recv()
        for cp in copies:
            cp.wait_send()


class _Riders:
    def __init__(self, riders):
        self.riders = list(riders)
        self.xs = [x for r in self.riders for x in r.xs]
        self.n = len(self.xs)
        self.out_shape = [s for r in self.riders for s in r.out_shape]
        self.scratch = [s for r in self.riders for s in r.scratch]

    def _each(self, method, x_refs, out_refs, sems):
        a = b = 0
        for r in self.riders:
            getattr(r, method)(x_refs[a:a + r.n], out_refs[a:a + r.n], sems[b:b + len(r.scratch)])
            a, b = a + r.n, b + len(r.scratch)

    def start(self, *refs):
        self._each("start", *refs)

    def middle(self, *refs):
        self._each("middle", *refs)

    def finish(self, *refs):
        self._each("finish", *refs)


def _comm_alone(rider, name):
    n = rider.n

    def body(*refs):
        x_refs, out_refs, sems = refs[:n], refs[n:2 * n], refs[2 * n:]
        rider.start(x_refs, out_refs, sems)
        rider.middle(x_refs, out_refs, sems)
        rider.finish(x_refs, out_refs, sems)

    return pl.pallas_call(
        body, name=name, out_shape=tuple(rider.out_shape), in_specs=[_HBM] * n, out_specs=(_HBM,) * n,
        scratch_shapes=rider.scratch,
    )(*rider.xs)


def _with_rider(body, nin, nout, rider, first, mid, last):
    if rider is None:
        return body
    n = rider.n

    def wrapped(*refs):
        ins, x_refs = refs[:nin], refs[nin:nin + n]
        outs, out_refs = refs[nin + n:nin + n + nout], refs[nin + n + nout:nin + 2 * n + nout]
        ns = len(rider.scratch)
        scratch, sems = refs[nin + 2 * n + nout:len(refs) - ns], refs[len(refs) - ns:]

        @pl.when(first())
        def _():
            rider.start(x_refs, out_refs, sems)

        body(*ins, *outs, *scratch)

        @pl.when(mid())
        def _():
            rider.middle(x_refs, out_refs, sems)

        @pl.when(last())
        def _():
            rider.finish(x_refs, out_refs, sems)

    return wrapped


def _mm(a, b, name, ta=False, tb=False, out_dtype=F32, bias=None, precise=False, tm=None, tn=None, tk=None, rider=None):
    m, k = (a.shape[1], a.shape[0]) if ta else a.shape
    n = b.shape[0] if tb else b.shape[1]
    assert k == (b.shape[1] if tb else b.shape[0])
    tm = tm or _pick(m, (1088, 1024, 512, 256, 128))
    tn = tn or _pick(n, (1024, 512, 384, 256, 128))
    tk = tk or _pick(k, (1024, 1088, 512, 256, 128))
    nk = k // tk
    dn = (((0 if ta else 1,), (1 if tb else 0,)), ((), ()))

    def body(*refs):
        if bias is None:
            a_ref, b_ref, o_ref = refs[:3]
            bias_ref = None
        else:
            a_ref, b_ref, bias_ref, o_ref = refs[:4]
        x, y = a_ref[...], b_ref[...]
        if precise:
            p = lax.dot_general(x.astype(F32), y.astype(F32), dn, preferred_element_type=F32, precision=HIGHEST)
        else:
            p = lax.dot_general(x.astype(BF16), y.astype(BF16), dn, preferred_element_type=F32)

        def finish(acc):
            if bias_ref is not None:
                acc = acc + bias_ref[...]
            o_ref[...] = acc.astype(out_dtype)

        if nk == 1:
            finish(p)
        else:
            acc_ref = refs[-1]
            kk = pl.program_id(2)

            @pl.when(kk == 0)
            def _():
                acc_ref[...] = p

            @pl.when(kk > 0)
            def _():
                acc_ref[...] += p

            @pl.when(kk == nk - 1)
            def _():
                finish(acc_ref[...])

    a_spec = pl.BlockSpec((tk, tm), lambda i, j, kk: (kk, i)) if ta else pl.BlockSpec((tm, tk), lambda i, j, kk: (i, kk))
    b_spec = pl.BlockSpec((tn, tk), lambda i, j, kk: (j, kk)) if tb else pl.BlockSpec((tk, tn), lambda i, j, kk: (kk, j))
    in_specs = [a_spec, b_spec]
    args = [a, b]
    if bias is not None:
        in_specs.append(pl.BlockSpec((1, tn), lambda i, j, kk: (0, j)))
        args.append(bias)
    grid = (m // tm, n // tn, nk)
    out_spec = pl.BlockSpec((tm, tn), lambda i, j, kk: (i, j))
    scratch = [pltpu.VMEM((tm, tn), F32)] if nk > 1 else []
    if rider is None:
        return pl.pallas_call(
            body, name=name, grid=grid, in_specs=in_specs, out_specs=out_spec,
            out_shape=jax.ShapeDtypeStruct((m, n), out_dtype), scratch_shapes=scratch, compiler_params=_cparams(3, 56),
        )(*args)

    def at(step):
        return lambda: ((pl.program_id(0) == step[0]) & (pl.program_id(1) == step[1]) & (pl.program_id(2) == step[2]))

    end = tuple(g - 1 for g in grid)
    step = grid[0] * grid[1] * grid[2] * 7 // 8
    late = (step // (grid[1] * grid[2]), step // grid[2] % grid[1], step % grid[2])
    return pl.pallas_call(
        _with_rider(body, len(args), 1, rider, at((0, 0, 0)), at(late), at(end)),
        name=name, grid=grid, in_specs=in_specs + [_HBM] * rider.n, out_specs=(out_spec,) + (_HBM,) * rider.n,
        out_shape=(jax.ShapeDtypeStruct((m, n), out_dtype),) + tuple(rider.out_shape),
        scratch_shapes=scratch + rider.scratch, compiler_params=_cparams(3, 56),
    )(*args, *rider.xs)


def _ada_in(cc):
    def body(c_ref, s_ref, d_ref):
        x = c_ref[...]
        sg = _sigmoid(x)
        s_ref[...] = x * sg
        d_ref[...] = sg * (1.0 + x * (1.0 - sg))

    return pl.pallas_call(body, name="ada_in", out_shape=(jax.ShapeDtypeStruct(cc.shape, F32),) * 2)(cc)


def _cctx_grad(t0, t1, dsilu):
    def body(a_ref, b_ref, d_ref, o_ref):
        o_ref[...] = (a_ref[...] + b_ref[...]) * d_ref[...]

    return pl.pallas_call(body, name="cctx_grad", out_shape=jax.ShapeDtypeStruct(t0.shape, F32))(t0, t1, dsilu)


def _seg_spec(nct, rows=3):
    return pl.BlockSpec((None, rows, D), lambda i: (jnp.where(i >= nct, 1, 0), 0, 0))


def _prenorm_fwd(x, g_pre, mod3, nct, name):
    t = x.shape[0]

    def body(x_ref, g_ref, mod_ref, h_ref):
        xv = x_ref[...]
        r = lax.rsqrt(jnp.mean(xv * xv, axis=-1, keepdims=True) + EPS)
        y = xv * r * g_ref[...]
        h_ref[...] = (y * (1.0 + mod_ref[1:2, :]) + mod_ref[0:1, :]).astype(BF16)

    return pl.pallas_call(
        body, name=name, grid=(t // TM,),
        in_specs=[pl.BlockSpec((TM, D), lambda i: (i, 0)), pl.BlockSpec((1, D), lambda i: (0, 0)), _seg_spec(nct)],
        out_specs=pl.BlockSpec((TM, D), lambda i: (i, 0)),
        out_shape=jax.ShapeDtypeStruct((t, D), BF16), compiler_params=_cparams(1, 32),
    )(x, g_pre, mod3)


def _prenorm_bwd(dh, x, dxo, g_pre, mod3, nct, name):
    t = x.shape[0]

    def body(dh_ref, x_ref, dxo_ref, g_ref, mod_ref, dx_ref, dsh_ref, dsc_ref, dg_ref):
        i = pl.program_id(0)
        xv, dhv, g = x_ref[...], dh_ref[...], g_ref[...]
        r = lax.rsqrt(jnp.mean(xv * xv, axis=-1, keepdims=True) + EPS)
        xh = xv * r
        dy = dhv * (1.0 + mod_ref[1:2, :])
        dxh = dy * g
        dx_ref[...] = dxo_ref[...] + r * (dxh - xh * jnp.mean(dxh * xh, axis=-1, keepdims=True))

        @pl.when((i == 0) | (i == nct))
        def _():
            dsh_ref[...] = jnp.zeros_like(dsh_ref)
            dsc_ref[...] = jnp.zeros_like(dsc_ref)

        @pl.when(i == 0)
        def _():
            dg_ref[...] = jnp.zeros_like(dg_ref)

        dsh_ref[...] += jnp.sum(dhv, axis=0, keepdims=True)
        dsc_ref[...] += jnp.sum(dhv * (xh * g), axis=0, keepdims=True)
        dg_ref[...] += jnp.sum(dy * xh, axis=0, keepdims=True)

    row = pl.BlockSpec((TM, D), lambda i: (i, 0))
    seg8 = pl.BlockSpec((None, 8, D), lambda i: (jnp.where(i >= nct, 1, 0), 0, 0))
    return pl.pallas_call(
        body, name=name, grid=(t // TM,),
        in_specs=[row, row, row, pl.BlockSpec((1, D), lambda i: (0, 0)), _seg_spec(nct)],
        out_specs=(row, seg8, seg8, pl.BlockSpec((8, D), lambda i: (0, 0))),
        out_shape=(jax.ShapeDtypeStruct((t, D), F32), jax.ShapeDtypeStruct((2, 8, D), F32),
                   jax.ShapeDtypeStruct((2, 8, D), F32), jax.ShapeDtypeStruct((8, D), F32)),
        compiler_params=_cparams(1, 32),
    )(dh, x, dxo, g_pre, mod3)


def _post_fwd(x, out, g_post, mod3, nct, name):
    t = x.shape[0]

    def body(x_ref, o_ref, g_ref, mod_ref, y_ref):
        ov = o_ref[...]
        r = lax.rsqrt(jnp.mean(ov * ov, axis=-1, keepdims=True) + EPS)
        y_ref[...] = x_ref[...] + mod_ref[2:3, :] * (ov * r * g_ref[...])

    row = pl.BlockSpec((TM, D), lambda i: (i, 0))
    return pl.pallas_call(
        body, name=name, grid=(t // TM,),
        in_specs=[row, row, pl.BlockSpec((1, D), lambda i: (0, 0)), _seg_spec(nct)],
        out_specs=row, out_shape=jax.ShapeDtypeStruct((t, D), F32), compiler_params=_cparams(1, 32),
    )(x, out, g_post, mod3)


def _post_bwd(dxo, out, g_post, mod3, nct, name):
    t = out.shape[0]

    def body(dx_ref, o_ref, g_ref, mod_ref, do_ref, dgt_ref, dg_ref):
        i = pl.program_id(0)
        ov, dxv, g = o_ref[...], dx_ref[...], g_ref[...]
        r = lax.rsqrt(jnp.mean(ov * ov, axis=-1, keepdims=True) + EPS)
        nh = ov * r
        dn = dxv * mod_ref[2:3, :]
        dnh = dn * g
        do_ref[...] = (r * (dnh - nh * jnp.mean(dnh * nh, axis=-1, keepdims=True))).astype(BF16)

        @pl.when((i == 0) | (i == nct))
        def _():
            dgt_ref[...] = jnp.zeros_like(dgt_ref)

        @pl.when(i == 0)
        def _():
            dg_ref[...] = jnp.zeros_like(dg_ref)

        dgt_ref[...] += jnp.sum(dxv * (nh * g), axis=0, keepdims=True)
        dg_ref[...] += jnp.sum(dn * nh, axis=0, keepdims=True)

    row = pl.BlockSpec((TM, D), lambda i: (i, 0))
    seg8 = pl.BlockSpec((None, 8, D), lambda i: (jnp.where(i >= nct, 1, 0), 0, 0))
    return pl.pallas_call(
        body, name=name, grid=(t // TM,),
        in_specs=[row, row, pl.BlockSpec((1, D), lambda i: (0, 0)), _seg_spec(nct)],
        out_specs=(row, seg8, pl.BlockSpec((8, D), lambda i: (0, 0))),
        out_shape=(jax.ShapeDtypeStruct((t, D), BF16), jax.ShapeDtypeStruct((2, 8, D), F32),
                   jax.ShapeDtypeStruct((8, D), F32)),
        compiler_params=_cparams(1, 32),
    )(dxo, out, g_post, mod3)


def _loss_grad(y, target, nct, name):
    t = y.shape[0]

    def body(y_ref, t_ref, dy_ref, l_ref):
        i = pl.program_id(0)

        @pl.when(i == 0)
        def _():
            l_ref[...] = jnp.zeros_like(l_ref)

        @pl.when(i < nct)
        def _():
            dy_ref[...] = jnp.zeros_like(dy_ref)

        @pl.when(i >= nct)
        def _():
            err = y_ref[...] - t_ref[...]
            dy_ref[...] = err / D
            l_ref[...] += jnp.sum(jnp.sum(err * err, axis=1, keepdims=True), axis=0, keepdims=True)

    row = pl.BlockSpec((TM, D), lambda i: (i, 0))
    return pl.pallas_call(
        body, name=name, grid=(t // TM,),
        in_specs=[row, pl.BlockSpec((TM, D), lambda i: (jnp.maximum(i - nct, 0), 0))],
        out_specs=(row, pl.BlockSpec((8, LANE), lambda i: (0, 0))),
        out_shape=(jax.ShapeDtypeStruct((t, D), F32), jax.ShapeDtypeStruct((8, LANE), F32)),
        compiler_params=_cparams(1, 32),
    )(y, target)


def _pcol(name, width, rows=TM):
    assert OFF[name] % width == 0
    blk = OFF[name] // width
    return pl.BlockSpec((rows, width), lambda i: (i, blk))


def _big_rows(t):
    return max(r for r in range(16, BIG_ROWS + 1, 16) if t % r == 0)


def _shift_rows(u, prev_row, next_row):
    n = u.shape[0]
    row = lax.broadcasted_iota(jnp.int32, u.shape, 0)
    prev = jnp.where(row == 0, prev_row, pltpu.roll(u, 1, 0))
    nxt = jnp.where(row == n - 1, next_row, pltpu.roll(u, n - 1, 0))
    return prev, nxt


def _halo_specs(width, nt, blk=0, rows=8):
    per = TM // rows
    prev = pl.BlockSpec((rows, width), lambda i: (jnp.maximum(i * per - 1, 0), blk))
    nxt = pl.BlockSpec((rows, width), lambda i: (jnp.minimum((i + 1) * per, nt * per - 1), blk))
    return prev, nxt


def _conv_fwd(p, conv_w8, nct, name):
    t = p.shape[0]
    nt = t // TM

    def body(ab_ref, ac_ref, ax_ref, az_ref, acp_ref, axp_ref, acn_ref, axn_ref, w_ref, cv_ref, ya_ref):
        i = pl.program_id(0)
        def f(ref, rows=slice(None)):
            return ref[rows, :].astype(F32)

        u = f(ac_ref) * f(ax_ref)
        mp = jnp.where((i == 0) | (i == nct), 0.0, 1.0)
        mn = jnp.where((i == nct - 1) | (i == nt - 1), 0.0, 1.0)
        last, first = slice(P_HALO - 1, P_HALO), slice(0, 1)
        prev, nxt = _shift_rows(u, f(acp_ref, last) * f(axp_ref, last) * mp, f(acn_ref, first) * f(axn_ref, first) * mn)
        cv = w_ref[0:1, :] * prev + w_ref[1:2, :] * u + w_ref[2:3, :] * nxt
        az = f(az_ref)
        cv_ref[...] = cv.astype(BF16)
        ya_ref[...] = (f(ab_ref) * cv * (az * _sigmoid(az))).astype(BF16)

    acp, acn = _halo_specs(D, nt, OFF["a_c"] // D, P_HALO)
    axp, axn = _halo_specs(D, nt, OFF["a_x"] // D, P_HALO)
    row = pl.BlockSpec((TM, D), lambda i: (i, 0))
    return pl.pallas_call(
        body, name=name, grid=(nt,),
        in_specs=[_pcol("a_b", D), _pcol("a_c", D), _pcol("a_x", D), _pcol("a_z", D), acp, axp, acn, axn,
                  pl.BlockSpec((8, D), lambda i: (0, 0))],
        out_specs=(row, row),
        out_shape=(jax.ShapeDtypeStruct((t, D), BF16), jax.ShapeDtypeStruct((t, D), BF16)),
        compiler_params=_cparams(1, 40),
    )(p, p, p, p, p, p, p, p, conv_w8)


def _dp_spec(key, rows=TM):
    seg, width = DP_BLOCKS[key]
    assert OFF[seg] % width == 0
    blk = OFF[seg] // width
    return pl.BlockSpec((rows, width), lambda i: (i, blk))


def _conv_bwd_a(dya, p, cv, dp, name):
    t = p.shape[0]

    def body(dy_ref, ab_ref, az_ref, cv_ref, _, dcv_ref, dp_ref):
        dy, ab = dy_ref[...].astype(F32), ab_ref[...].astype(F32)
        az, c = az_ref[...].astype(F32), cv_ref[...].astype(F32)
        sg = _sigmoid(az)
        sz = az * sg
        dcv_ref[...] = dy * ab * sz
        dp_ref[:, 0:D] = (dy * c * sz).astype(BF16)
        dp_ref[:, D:2 * D] = (dy * ab * c * (sg * (1.0 + az * (1.0 - sg)))).astype(BF16)

    rt = _big_rows(t)
    row = pl.BlockSpec((rt, D), lambda i: (i, 0))
    return pl.pallas_call(
        body, name=name, grid=(t // rt,),
        in_specs=[row, _pcol("a_b", D, rt), _pcol("a_z", D, rt), row, _HBM], out_specs=(row, _dp_spec("conv_a", rt)),
        out_shape=(jax.ShapeDtypeStruct((t, D), F32), jax.ShapeDtypeStruct(dp.shape, dp.dtype)),
        input_output_aliases={4: 1}, compiler_params=_cparams(1, 40),
    )(dya, p, p, cv, dp)


def _conv_bwd_b(dcv, p, conv_w8, nct, dp, name):
    t = p.shape[0]
    nt = t // TM

    def body(dcv_ref, hp_ref, hn_ref, ac_ref, ax_ref, w_ref, _, dp_ref, dw_ref):
        i = pl.program_id(0)
        d, ac, ax = dcv_ref[...], ac_ref[...].astype(F32), ax_ref[...].astype(F32)
        u = ac * ax
        mp = jnp.where((i == 0) | (i == nct), 0.0, 1.0)
        mn = jnp.where((i == nct - 1) | (i == nt - 1), 0.0, 1.0)
        dprev, dnxt = _shift_rows(d, hp_ref[7:8, :] * mp, hn_ref[0:1, :] * mn)
        du = w_ref[0:1, :] * dnxt + w_ref[1:2, :] * d + w_ref[2:3, :] * dprev
        dp_ref[:, 0:D] = (du * ax).astype(BF16)
        dp_ref[:, D:2 * D] = (du * ac).astype(BF16)

        @pl.when(i == 0)
        def _():
            dw_ref[...] = jnp.zeros_like(dw_ref)

        dw0 = jnp.sum(u * dnxt, axis=0, keepdims=True)
        dw1 = jnp.sum(u * d, axis=0, keepdims=True)
        dw2 = jnp.sum(u * dprev, axis=0, keepdims=True)
        r8 = lax.broadcasted_iota(jnp.int32, (8, D), 0)
        dw_ref[...] += jnp.where(r8 == 0, dw0, jnp.where(r8 == 1, dw1, jnp.where(r8 == 2, dw2, 0.0)))

    hp, hn = _halo_specs(D, nt)
    row = pl.BlockSpec((TM, D), lambda i: (i, 0))
    return pl.pallas_call(
        body, name=name, grid=(nt,),
        in_specs=[row, hp, hn, _pcol("a_c", D), _pcol("a_x", D), pl.BlockSpec((8, D), lambda i: (0, 0)), _HBM],
        out_specs=(_dp_spec("conv_b"), pl.BlockSpec((8, D), lambda i: (0, 0))),
        out_shape=(jax.ShapeDtypeStruct(dp.shape, dp.dtype), jax.ShapeDtypeStruct((8, D), F32)),
        input_output_aliases={6: 0}, compiler_params=_cparams(1, 40),
    )(dcv, dcv, dcv, p, p, conv_w8, dp)


def _rot_half(x):
    lane = lax.broadcasted_iota(jnp.int32, x.shape, 1)
    return jnp.where((lane % 64) < 32, pltpu.roll(x, 96, 1), pltpu.roll(x, 32, 1))


def _qk_prep_fwd(p, qg, kg, cos_t, sin_t, name):
    t = p.shape[0]

    def body(q_ref, k_ref, qg_ref, kg_ref, c_ref, s_ref, qo_ref, ko_ref):
        c, s = c_ref[...], s_ref[...]

        def one(xv, g, scale):
            y = xv * lax.rsqrt(jnp.mean(xv * xv, axis=-1, keepdims=True) + EPS) * g
            return ((y * c + _rot_half(y) * s) * scale).astype(BF16)

        for h in range(NH):
            qo_ref[:, h * HD:(h + 1) * HD] = one(q_ref[:, h * HD:(h + 1) * HD].astype(F32), qg_ref[...], Q_FOLD)
        for h in range(NKV):
            ko_ref[:, h * HD:(h + 1) * HD] = one(k_ref[:, h * HD:(h + 1) * HD].astype(F32), kg_ref[...], 1.0)

    vec = pl.BlockSpec((1, HD), lambda i: (0, 0))
    rt = _big_rows(t)
    tab = pl.BlockSpec((rt, HD), lambda i: (i, 0))
    return pl.pallas_call(
        body, name=name, grid=(t // rt,),
        in_specs=[_pcol("q", NH * HD, rt), _pcol("k", NKV * HD, rt), vec, vec, tab, tab],
        out_specs=(pl.BlockSpec((rt, NH * HD), lambda i: (i, 0)), pl.BlockSpec((rt, NKV * HD), lambda i: (i, 0))),
        out_shape=(jax.ShapeDtypeStruct((t, NH * HD), BF16), jax.ShapeDtypeStruct((t, NKV * HD), BF16)),
        compiler_params=_cparams(1, 32),
    )(p, p, qg, kg, cos_t, sin_t)


def _qk_prep_bwd(dqr, dkr, p, qg, kg, cos_t, sin_t, dp, name):
    t = p.shape[0]

    def body(dq_ref, dk_ref, q_ref, k_ref, qg_ref, kg_ref, c_ref, s_ref, _, dqo_ref, dko_ref, dqg_ref, dkg_ref):
        i = pl.program_id(0)
        c, s = c_ref[...], s_ref[...]

        @pl.when(i == 0)
        def _():
            dqg_ref[...] = jnp.zeros_like(dqg_ref)
            dkg_ref[...] = jnp.zeros_like(dkg_ref)

        def one(dyr, xv, g):
            dy = dyr * c + _rot_half(dyr * s)
            r = lax.rsqrt(jnp.mean(xv * xv, axis=-1, keepdims=True) + EPS)
            xh = xv * r
            dxh = dy * g
            dx = r * (dxh - xh * jnp.mean(dxh * xh, axis=-1, keepdims=True))
            return dx.astype(BF16), jnp.sum(dy * xh, axis=0, keepdims=True)

        for h in range(NH):
            sl = slice(h * HD, (h + 1) * HD)
            dx, dg = one(dq_ref[:, sl] * ATTN_SCALE, q_ref[:, sl].astype(F32), qg_ref[...])
            dqo_ref[:, sl] = dx
            dqg_ref[...] += dg
        for h in range(NKV):
            sl = slice(h * HD, (h + 1) * HD)
            dx, dg = one(dk_ref[:, sl] * (ATTN_SCALE / Q_FOLD), k_ref[:, sl].astype(F32), kg_ref[...])
            dko_ref[:, sl] = dx
            dkg_ref[...] += dg

    vec = pl.BlockSpec((1, HD), lambda i: (0, 0))
    rt = _big_rows(t)
    tab = pl.BlockSpec((rt, HD), lambda i: (i, 0))
    acc = pl.BlockSpec((8, HD), lambda i: (0, 0))
    qrow = pl.BlockSpec((rt, NH * HD), lambda i: (i, 0))
    krow = pl.BlockSpec((rt, NKV * HD), lambda i: (i, 0))
    return pl.pallas_call(
        body, name=name, grid=(t // rt,),
        in_specs=[qrow, krow, _pcol("q", NH * HD, rt), _pcol("k", NKV * HD, rt), vec, vec, tab, tab, _HBM],
        out_specs=(_dp_spec("q", rt), krow, acc, acc),
        out_shape=(jax.ShapeDtypeStruct(dp.shape, dp.dtype), jax.ShapeDtypeStruct((t, NKV * HD), BF16),
                   jax.ShapeDtypeStruct((8, HD), F32), jax.ShapeDtypeStruct((8, HD), F32)),
        input_output_aliases={8: 0}, compiler_params=_cparams(1, 32),
    )(dqr, dkr, p, p, qg, kg, cos_t, sin_t, dp)


def _key_chunks(n, limit):
    c = max(c for c in range(KEY_ALIGN, min(n, limit) + 1, KEY_ALIGN) if n % c == 0)
    return [(lo, lo + c) for lo in range(0, n, c)]


def _attn_fwd(qr, kr, p, nct, name, rider=None):
    t = qr.shape[0]
    nt = t // TM
    ctx = nct * TM
    vblk = OFF["v"] // HD
    hps = ATTN_HEADS_PER_STEP
    nhp, per_kv = NH // hps, GROUP // hps

    def body(q_ref, k_ref, v_ref, o_ref, lse_ref):
        def tile(nkeys):
            sls = [slice(j * HD, (j + 1) * HD) for j in range(hps)]
            qs = [q_ref[:, sl] for sl in sls]
            m = l = acc = None
            for lo, hi in _key_chunks(nkeys, ATTN_FWD_KEY_CHUNK):
                k, vb = k_ref[lo:hi, :], v_ref[lo:hi, :].astype(BF16)
                ss = [lax.dot_general(q, k, _NT, preferred_element_type=F32) for q in qs]
                mcs = [jnp.max(s, axis=-1, keepdims=True) for s in ss]
                m_new = mcs if m is None else [jnp.maximum(a, b) for a, b in zip(m, mcs)]
                es = [jnp.exp2(s - mn) for s, mn in zip(ss, m_new)]
                lcs = [jnp.sum(e, axis=-1, keepdims=True) for e in es]
                pvs = [jnp.dot(e.astype(BF16), vb, preferred_element_type=F32) for e in es]
                if m is None:
                    l, acc = lcs, pvs
                else:
                    alphas = [jnp.exp2(a - b) for a, b in zip(m, m_new)]
                    l = [x * al + y for x, al, y in zip(l, alphas, lcs)]
                    acc = [x * al + y for x, al, y in zip(acc, alphas, pvs)]
                m = m_new
            for j, sl in enumerate(sls):
                o_ref[:, sl] = (acc[j] / l[j]).astype(BF16)
                lse_ref[:, j:j + 1] = m[j] + jnp.log2(l[j])

        pl.when(pl.program_id(1) < nct)(lambda: tile(ctx))
        pl.when(pl.program_id(1) >= nct)(lambda: tile(t))

    def at(h, i):
        return lambda: (pl.program_id(0) == h) & (pl.program_id(1) == i)

    rn = 0 if rider is None else rider.n
    qspec = pl.BlockSpec((TM, hps * HD), lambda h, i: (i, h))
    return pl.pallas_call(
        _with_rider(body, 3, 2, rider, at(0, 0), at(*divmod(nhp * nt * 7 // 8, nt)), at(nhp - 1, nt - 1)),
        name=name, grid=(nhp, nt),
        in_specs=[qspec, pl.BlockSpec((t, HD), lambda h, i: (0, h // per_kv)),
                  pl.BlockSpec((t, HD), lambda h, i: (0, vblk + h // per_kv))] + [_HBM] * rn,
        out_specs=(qspec, pl.BlockSpec((None, TM, hps), lambda h, i: (h, i, 0))) + (_HBM,) * rn,
        out_shape=(jax.ShapeDtypeStruct((t, NH * HD), BF16), jax.ShapeDtypeStruct((nhp, t, hps), F32))
        + (() if rider is None else tuple(rider.out_shape)),
        scratch_shapes=[] if rider is None else rider.scratch,
        compiler_params=_cparams(2, 48),
    )(qr, kr, p, *(() if rider is None else rider.xs))


def _attn_bwd(qr, kr, p, o, lse, do, nct, name, rider=None):
    t = qr.shape[0]
    nt = t // TM
    ctx = nct * TM
    vblk = OFF["v"] // HD
    hps = ATTN_HEADS_PER_STEP

    def body(q_ref, k_ref, v_ref, o_ref, lse_ref, do_ref, dq_ref, dk_ref, dv_ref):
        g, i = pl.program_id(1), pl.program_id(2)

        @pl.when((g == 0) & (i == 0))
        def _():
            dk_ref[...] = jnp.zeros_like(dk_ref)
            dv_ref[...] = jnp.zeros_like(dv_ref)

        def tile(nkeys):
            heads = []
            for j in range(hps):
                sl = slice(j * HD, (j + 1) * HD)
                dob = do_ref[:, sl]
                drow = jnp.sum(dob.astype(F32) * o_ref[:, sl].astype(F32), axis=-1, keepdims=True)
                heads.append((sl, q_ref[:, sl], dob, drow, lse_ref[:, j:j + 1]))
            dq = [None] * hps
            for lo, hi in _key_chunks(nkeys, ATTN_BWD_KEY_CHUNK):
                k = k_ref[lo:hi, :]
                vb = v_ref[lo:hi, :].astype(BF16)
                ss = [lax.dot_general(q, k, _NT, preferred_element_type=F32) for _, q, _, _, _ in heads]
                dps = [lax.dot_general(dob, vb, _NT, preferred_element_type=F32) for _, _, dob, _, _ in heads]
                prs = [jnp.exp2(s - h[4]) for s, h in zip(ss, heads)]
                dss = [(pr * (dp - h[3])).astype(BF16) for pr, dp, h in zip(prs, dps, heads)]
                pbs = [pr.astype(BF16) for pr in prs]
                dqs = [jnp.dot(ds, k, preferred_element_type=F32) for ds in dss]
                dks = [lax.dot_general(ds, h[1], _TN, preferred_element_type=F32) for ds, h in zip(dss, heads)]
                dvs = [lax.dot_general(pb, h[2], _TN, preferred_element_type=F32) for pb, h in zip(pbs, heads)]
                dq = [x if y is None else y + x for x, y in zip(dqs, dq)]
                dk_ref[lo:hi, :] += functools.reduce(lambda a, b: a + b, dks)
                dv_ref[lo:hi, :] += functools.reduce(lambda a, b: a + b, dvs)
            for j, (sl, *_) in enumerate(heads):
                dq_ref[:, sl] = dq[j]

        pl.when(i < nct)(lambda: tile(ctx))
        pl.when(i >= nct)(lambda: tile(t))

    def at(kv, g, i):
        return lambda: (pl.program_id(0) == kv) & (pl.program_id(1) == g) & (pl.program_id(2) == i)

    rn = 0 if rider is None else rider.n
    per_kv = GROUP // hps
    qspec = pl.BlockSpec((TM, hps * HD), lambda kv, g, i: (i, kv * per_kv + g))
    kvspec = pl.BlockSpec((t, HD), lambda kv, g, i: (0, kv))
    lspec = pl.BlockSpec((None, TM, hps), lambda kv, g, i: (kv * per_kv + g, i, 0))
    return pl.pallas_call(
        _with_rider(body, 6, 3, rider, at(0, 0, 0), at(NKV - 1, 0, 0), at(NKV - 1, per_kv - 1, nt - 1)),
        name=name, grid=(NKV, per_kv, nt),
        in_specs=[qspec, kvspec, pl.BlockSpec((t, HD), lambda kv, g, i: (0, vblk + kv)), qspec, lspec, qspec]
        + [_HBM] * rn,
        out_specs=(qspec, kvspec, kvspec) + (_HBM,) * rn,
        out_shape=(jax.ShapeDtypeStruct((t, NH * HD), F32), jax.ShapeDtypeStruct((t, NKV * HD), F32),
                   jax.ShapeDtypeStruct((t, NKV * HD), F32)) + (() if rider is None else tuple(rider.out_shape)),
        scratch_shapes=[] if rider is None else rider.scratch,
        compiler_params=_cparams(3, 48),
    )(qr, kr, p, o, lse, do, *(() if rider is None else rider.xs))


def _decay_fwd(p, wd, bd, name):
    t = p.shape[0]

    def body(r_ref, w_ref, b_ref, z_ref, bc_ref):
        z = jnp.dot(r_ref[...].astype(BF16), w_ref[...].astype(BF16), preferred_element_type=F32) + b_ref[...]
        z_ref[...] = z
        la = (jnp.minimum(z, 0.0) - jnp.log(1.0 + jnp.exp(-jnp.abs(z)))) / GLA_TAU
        half = GH * GDK
        bc_ref[:, 0:half] = _chunk_sums(la[:, 0:half], False)
        bc_ref[:, half:] = _chunk_sums(la[:, half:], True)

    row = pl.BlockSpec((TM, D), lambda i: (i, 0))
    return pl.pallas_call(
        body, name=name, grid=(t // TM,),
        in_specs=[_pcol("r", R_PAD), pl.BlockSpec((R_PAD, D), lambda i: (0, 0)), pl.BlockSpec((1, D), lambda i: (0, 0))],
        out_specs=(row, row),
        out_shape=(jax.ShapeDtypeStruct((t, D), F32), jax.ShapeDtypeStruct((t, D), F32)),
        compiler_params=_cparams(1, 32),
    )(p, wd, bd)


def _chunk_order(s, ncc, nc, rev):
    if not rev:
        return s
    return jnp.where(s < ncc, ncc - 1 - s, nc - 1 - (s - ncc))


GLA_CPS = TM // CH


class _Chain:
    def __init__(self, rev, d, h, sub, refs):
        self.rev, self.d, self.h, self.sub, self.refs = rev, d, h, sub, refs
        self.rows, self.k, self.v = slice(sub * CH, (sub + 1) * CH), _hk(h), _hv(h)
        self.last = sub * CH + (0 if rev else CH - 1)


def _gla_chains(dirs, step, backward):
    return [_Chain(rev, d, h, step if rev == backward else GLA_CPS - 1 - step, refs)
            for d, (rev, refs) in enumerate(dirs) for h in range(GH)]


def _hk(h):
    return slice(h * GDK, (h + 1) * GDK)


def _hv(h):
    return slice(h * GDV, (h + 1) * GDV)


def _chunk_sums(x, from_end):
    r = lax.broadcasted_iota(jnp.int32, (CH, CH), 0)
    c = lax.broadcasted_iota(jnp.int32, (CH, CH), 1)
    tri = ((c >= r) if from_end else (c <= r)).astype(F32)
    return jnp.concatenate([jnp.dot(tri, x[lo:lo + CH], preferred_element_type=F32, precision=HIGHEST)
                            for lo in range(0, x.shape[0], CH)], axis=0)


def _gla_factors(qs, ks, bcs, bls, revs):
    r = lax.broadcasted_iota(jnp.int32, (CH, CH), 0)
    c = lax.broadcasted_iota(jnp.int32, (CH, CH), 1)
    keeps = [(c >= r) if rev else (c <= r) for rev in revs]
    qs, ks = [q.astype(F32) for q in qs], [k.astype(F32) for k in ks]
    qts = [q * GLA_SCALE * jnp.exp(bc) for q, bc in zip(qs, bcs)]
    kts = [k * jnp.exp(-bc) for k, bc in zip(ks, bcs)]
    khs = [k * jnp.exp(bl - bc) for k, bl, bc in zip(ks, bls, bcs)]
    gls = [jnp.exp(bl) for bl in bls]
    return qts, kts, gls, khs, keeps


def _gla_loads(ch):
    qs = [c.refs[0][c.rows, c.k] for c in ch]
    ks = [c.refs[1][c.rows, c.k] for c in ch]
    bcs = [c.refs[3][c.rows, c.k] for c in ch]
    bls = [c.refs[3][c.last:c.last + 1, c.k] for c in ch]
    return qs, ks, bcs, bls


_NT = (((1,), (1,)), ((), ()))
_TN = (((0,), (0,)), ((), ()))


def _gla_specs(ncs, ns, rev, backward):
    def idx(s):
        return _chunk_order((ns - 1 - s) if backward else s, ncs, ns, rev)

    wk, wv = GH * GDK, GH * GDV
    qb, kb, vb = OFF["gq"] // wk, OFF["gk"] // wk, OFF["gv"] // wv
    lab = 1 if rev else 0
    q = pl.BlockSpec((TM, wk), lambda s: (idx(s), qb))
    k = pl.BlockSpec((TM, wk), lambda s: (idx(s), kb))
    v = pl.BlockSpec((TM, wv), lambda s: (idx(s), vb))
    la = pl.BlockSpec((TM, wk), lambda s: (idx(s), lab))
    o = pl.BlockSpec((TM, wv), lambda s: (idx(s), 0))
    dk = pl.BlockSpec((TM, wk), lambda s: (idx(s), 0))
    st = pl.BlockSpec((GLA_CPS, GH, GDV, GDK), lambda s: (idx(s), 0, 0, 0))
    return q, k, v, la, o, dk, st


def _gla_fwd(p, la, ncs, name):
    t = p.shape[0]
    nc, ns = t // CH, t // TM
    specs = [_gla_specs(ncs, ns, rev, False) for rev in (False, True)]

    def body(qf, kf, vf, laf, qb_, kb_, vb_, lab, of, stf, ob, stb, s_scr):
        @pl.when(pl.program_id(0) == 0)
        def _():
            s_scr[...] = jnp.zeros_like(s_scr)

        dirs = ((False, (qf, kf, vf, laf, of, stf)), (True, (qb_, kb_, vb_, lab, ob, stb)))
        for step in range(GLA_CPS):
            ch = _gla_chains(dirs, step, False)
            qts, kts, gls, khs, keeps = _gla_factors(*_gla_loads(ch), [c.rev for c in ch])
            sts = [s_scr[c.d, c.h] for c in ch]
            for c, st in zip(ch, sts):
                c.refs[5][c.sub, c.h] = st.astype(BF16)
            vbs = [c.refs[2][c.rows, c.v].astype(BF16) for c in ch]
            qbs = [qt.astype(BF16) for qt in qts]
            a_s = [jnp.where(keep, lax.dot_general(qb, kt.astype(BF16), _NT, preferred_element_type=F32), 0.0)
                   for keep, qb, kt in zip(keeps, qbs, kts)]
            inter = [lax.dot_general(qb, st.astype(BF16), _NT, preferred_element_type=F32) for qb, st in zip(qbs, sts)]
            intra = [jnp.dot(a.astype(BF16), vb, preferred_element_type=F32) for a, vb in zip(a_s, vbs)]
            for c, x, y in zip(ch, inter, intra):
                c.refs[4][c.rows, c.v] = (x + y).astype(BF16)
            upd = [lax.dot_general(vb, kh.astype(BF16), _TN, preferred_element_type=F32) for vb, kh in zip(vbs, khs)]
            for c, st, gl, u in zip(ch, sts, gls, upd):
                s_scr[c.d, c.h] = st * gl + u

    o_shape = jax.ShapeDtypeStruct((t, GH * GDV), BF16)
    st_shape = jax.ShapeDtypeStruct((nc, GH, GDV, GDK), BF16)
    return pl.pallas_call(
        body, name=name, grid=(ns,),
        in_specs=[sp for s_ in specs for sp in s_[:4]],
        out_specs=tuple(sp for s_ in specs for sp in (s_[4], s_[6])),
        out_shape=(o_shape, st_shape, o_shape, st_shape),
        scratch_shapes=[pltpu.VMEM((2, GH, GDV, GDK), F32)], compiler_params=_cparams(1, 32),
    )(p, p, p, la, p, p, p, la)


def _gla_bwd(p, la, do, stf, stb, ncs, name):
    t = p.shape[0]
    ns = t // TM
    specs = [_gla_specs(ncs, ns, rev, True) for rev in (False, True)]

    def mm(xs, ys, dims=None):
        if dims is None:
            return [jnp.dot(x, y, preferred_element_type=F32) for x, y in zip(xs, ys)]
        return [lax.dot_general(x, y, dims, preferred_element_type=F32) for x, y in zip(xs, ys)]

    def body(*refs):
        ins_f, ins_b, outs_f, outs_b, ds_scr = refs[0:6], refs[6:12], refs[12:16], refs[16:20], refs[20]

        @pl.when(pl.program_id(0) == 0)
        def _():
            ds_scr[...] = jnp.zeros_like(ds_scr)

        dirs = ((False, (*ins_f, *outs_f)), (True, (*ins_b, *outs_b)))
        row = lax.broadcasted_iota(jnp.int32, (CH, GDK), 0)
        for step in range(GLA_CPS):
            ch = _gla_chains(dirs, step, True)
            revs = [c.rev for c in ch]
            loads = _gla_loads(ch)
            bcs = loads[2]
            qts, kts, gls, khs, keeps = _gla_factors(*loads, revs)
            stvs = [c.refs[5][c.sub, c.h] for c in ch]
            dsns = [ds_scr[c.d, c.h] for c in ch]
            dsbs = [x.astype(BF16) for x in dsns]
            vbs = [c.refs[2][c.rows, c.v].astype(BF16) for c in ch]
            dobs = [c.refs[4][c.rows, c.v].astype(BF16) for c in ch]
            qbs, kbs = [x.astype(BF16) for x in qts], [x.astype(BF16) for x in kts]
            a_s = [jnp.where(keep, x, 0.0).astype(BF16) for keep, x in zip(keeps, mm(qbs, kbs, _NT))]
            das = [jnp.where(keep, x, 0.0).astype(BF16) for keep, x in zip(keeps, mm(dobs, vbs, _NT))]
            dqts = [x + y for x, y in zip(mm(dobs, stvs), mm(das, kbs))]
            dkhs = mm(vbs, dsbs)
            dkts = [x + dkh * gl for x, dkh, gl in zip(mm(das, qbs, _TN), dkhs, gls)]
            for c, x, y in zip(ch, mm(a_s, dobs, _TN), mm([kh.astype(BF16) for kh in khs], dsbs, _NT)):
                c.refs[8][c.rows, c.v] = (x + y).astype(BF16)
            for c, x, dsn, gl in zip(ch, mm(dobs, qbs, _TN), dsns, gls):
                ds_scr[c.d, c.h] = x + dsn * gl
            dgls = [jnp.sum(st.astype(F32) * dsn, axis=0, keepdims=True) + jnp.sum(dkh * kt, axis=0, keepdims=True)
                    for st, dsn, dkh, kt in zip(stvs, dsns, dkhs, kts)]
            dbcs = [dqt * qt - dkt * kt + jnp.where(row == (0 if rev else CH - 1), dgl * gl, 0.0)
                    for rev, dqt, qt, dkt, kt, dgl, gl in zip(revs, dqts, qts, dkts, kts, dgls, gls)]
            for c, dbc, dqt, dkt, bc in zip(ch, dbcs, dqts, dkts, bcs):
                c.refs[9][c.rows, c.k] = dbc
                c.refs[6][c.rows, c.k] = (dqt * (GLA_SCALE * jnp.exp(bc))).astype(BF16)
                c.refs[7][c.rows, c.k] = (dkt * jnp.exp(-bc)).astype(BF16)

    k_shape = jax.ShapeDtypeStruct((t, GH * GDK), BF16)
    v_shape = jax.ShapeDtypeStruct((t, GH * GDV), BF16)
    c_shape = jax.ShapeDtypeStruct((t, GH * GDK), F32)
    res = pl.pallas_call(
        body, name=name, grid=(ns,),
        in_specs=[sp for q_s, k_s, v_s, la_s, o_s, _, st_s in specs for sp in (q_s, k_s, v_s, la_s, o_s, st_s)],
        out_specs=tuple(sp for _, _, _, _, o_s, dk_s, _ in specs for sp in (dk_s, dk_s, o_s, dk_s)),
        out_shape=(k_shape, k_shape, v_shape, c_shape) * 2,
        scratch_shapes=[pltpu.VMEM((2, GH, GDV, GDK), F32)], compiler_params=_cparams(1, 32),
    )(p, p, p, la, do, stf, p, p, p, la, do, stb)
    return res[:4], res[4:]


def _gla_merge_bwd(gf, gb, z, p, wd, dp, name):
    t = p.shape[0]
    w2 = GH * GDK

    def body(dqf, dkf, dvf, dlf, dqb, dkb, dvb, dlb, z_ref, r_ref, w_ref, _, dp_ref, dr_ref, db_ref, dw_ref):
        i = pl.program_id(0)
        dp_ref[:, 0:D] = (dvf[...].astype(F32) + dvb[...].astype(F32)).astype(BF16)
        dp_ref[:, D:D + w2] = (dqf[...].astype(F32) + dqb[...].astype(F32)).astype(BF16)
        dp_ref[:, D + w2:D + 2 * w2] = (dkf[...].astype(F32) + dkb[...].astype(F32)).astype(BF16)
        zv = z_ref[...]
        dlf_, dlb_ = _chunk_sums(dlf[...], True), _chunk_sums(dlb[...], False)
        dz = jnp.concatenate([dlf_, dlb_], axis=1) * (_sigmoid(-zv) / GLA_TAU)
        dzb = dz.astype(BF16)
        dr_ref[...] = lax.dot_general(dzb, w_ref[...].astype(BF16), _NT, preferred_element_type=F32).astype(BF16)

        @pl.when(i == 0)
        def _():
            db_ref[...] = jnp.zeros_like(db_ref)
            dw_ref[...] = jnp.zeros_like(dw_ref)

        db_ref[...] += jnp.sum(dz, axis=0, keepdims=True)
        dw_ref[...] += lax.dot_general(r_ref[...].astype(BF16), dzb, _TN, preferred_element_type=F32)

    half = pl.BlockSpec((TM, w2), lambda i: (i, 0))
    row = pl.BlockSpec((TM, D), lambda i: (i, 0))
    wspec = pl.BlockSpec((R_PAD, D), lambda i: (0, 0))
    return pl.pallas_call(
        body, name=name, grid=(t // TM,),
        in_specs=[half, half, row, half, half, half, row, half, row, _pcol("r", R_PAD), wspec, _HBM],
        out_specs=(_dp_spec("gla"), pl.BlockSpec((TM, R_PAD), lambda i: (i, 0)),
                   pl.BlockSpec((8, D), lambda i: (0, 0)), wspec),
        out_shape=(jax.ShapeDtypeStruct(dp.shape, dp.dtype), jax.ShapeDtypeStruct((t, R_PAD), BF16),
                   jax.ShapeDtypeStruct((8, D), F32), jax.ShapeDtypeStruct((R_PAD, D), F32)),
        input_output_aliases={11: 0}, compiler_params=_cparams(1, 40),
    )(*gf, *gb, z, p, wd, dp)


def _dp_tail(dk, dv, dr, dp, name):
    t = dk.shape[0]
    wk = NKV * HD

    def body(dk_ref, dv_ref, dr_ref, _, dp_ref):
        dp_ref[:, 0:wk] = dk_ref[...]
        dp_ref[:, wk:2 * wk] = dv_ref[...].astype(BF16)
        dp_ref[:, 2 * wk:2 * wk + R_PAD] = dr_ref[...]
        dp_ref[:, 2 * wk + R_PAD:] = jnp.zeros((rt, DP_BLOCKS["tail"][1] - 2 * wk - R_PAD), BF16)

    rt = _big_rows(t)
    kv = pl.BlockSpec((rt, wk), lambda i: (i, 0))
    return pl.pallas_call(
        body, name=name, grid=(t // rt,),
        in_specs=[kv, kv, pl.BlockSpec((rt, R_PAD), lambda i: (i, 0)), _HBM], out_specs=_dp_spec("tail", rt),
        out_shape=jax.ShapeDtypeStruct(dp.shape, dp.dtype), input_output_aliases={3: 0},
        compiler_params=_cparams(1, 32),
    )(dk, dv, dr, dp)


def _branch_fwd(att, of, ob, p, gla_g, name):
    t = p.shape[0]

    def body(att_ref, of_ref, ob_ref, za_ref, zg_ref, g_ref, yb_ref, yc_ref):
        za = za_ref[...].astype(F32)
        yb_ref[...] = (att_ref[...].astype(F32) * (za * _sigmoid(za))).astype(BF16)
        for h in range(GH):
            sl = slice(h * GDV, (h + 1) * GDV)
            o = of_ref[:, sl].astype(F32) + ob_ref[:, sl].astype(F32)
            n = o * lax.rsqrt(jnp.mean(o * o, axis=-1, keepdims=True) + EPS) * g_ref[...]
            zh = zg_ref[:, sl].astype(F32)
            yc_ref[:, sl] = (n * (zh * _sigmoid(zh))).astype(BF16)

    rt = _big_rows(t)
    row = pl.BlockSpec((rt, D), lambda i: (i, 0))
    return pl.pallas_call(
        body, name=name, grid=(t // rt,),
        in_specs=[row, row, row, _pcol("z_attn", D, rt), _pcol("zg", D, rt), pl.BlockSpec((1, GDV), lambda i: (0, 0))],
        out_specs=(row, row),
        out_shape=(jax.ShapeDtypeStruct((t, D), BF16), jax.ShapeDtypeStruct((t, D), BF16)),
        compiler_params=_cparams(1, 40),
    )(att, of, ob, p, p, gla_g)


def _branch_bwd(dyb, dyc, att, of, ob, p, gla_g, dp, name):
    t = p.shape[0]

    def body(dyb_ref, dyc_ref, att_ref, of_ref, ob_ref, za_ref, zg_ref, g_ref, _, datt_ref, do_ref, dp_ref, dg_ref):
        i = pl.program_id(0)

        @pl.when(i == 0)
        def _():
            dg_ref[...] = jnp.zeros_like(dg_ref)

        za, dyb = za_ref[...].astype(F32), dyb_ref[...].astype(F32)
        sa = _sigmoid(za)
        datt_ref[...] = (dyb * (za * sa)).astype(BF16)
        dp_ref[:, 0:D] = (dyb * att_ref[...].astype(F32) * (sa * (1.0 + za * (1.0 - sa)))).astype(BF16)
        g = g_ref[...]
        for h in range(GH):
            sl = slice(h * GDV, (h + 1) * GDV)
            o = of_ref[:, sl].astype(F32) + ob_ref[:, sl].astype(F32)
            r = lax.rsqrt(jnp.mean(o * o, axis=-1, keepdims=True) + EPS)
            oh = o * r
            zh, dyc = zg_ref[:, sl].astype(F32), dyc_ref[:, sl].astype(F32)
            sg = _sigmoid(zh)
            dn = dyc * (zh * sg)
            dp_ref[:, D + h * GDV:D + (h + 1) * GDV] = (dyc * (oh * g) * (sg * (1.0 + zh * (1.0 - sg)))).astype(BF16)
            doh = dn * g
            do_ref[:, sl] = (r * (doh - oh * jnp.mean(doh * oh, axis=-1, keepdims=True))).astype(BF16)
            dg_ref[...] += jnp.sum(dn * oh, axis=0, keepdims=True)

    row = pl.BlockSpec((TM, D), lambda i: (i, 0))
    return pl.pallas_call(
        body, name=name, grid=(t // TM,),
        in_specs=[row, row, row, row, row, _pcol("z_attn", D), _pcol("zg", D), pl.BlockSpec((1, GDV), lambda i: (0, 0)),
                  _HBM],
        out_specs=(row, row, _dp_spec("branch"), pl.BlockSpec((8, GDV), lambda i: (0, 0))),
        out_shape=(jax.ShapeDtypeStruct((t, D), BF16), jax.ShapeDtypeStruct((t, D), BF16),
                   jax.ShapeDtypeStruct(dp.shape, dp.dtype), jax.ShapeDtypeStruct((8, GDV), F32)),
        input_output_aliases={8: 2}, compiler_params=_cparams(1, 48),
    )(dyb, dyc, att, of, ob, p, p, gla_g, dp)


def _merge_fwd(bra, brb, brc, p, b_gate, name):
    t = p.shape[0]
    mgb = OFF["mg"] // D

    def body(a_ref, b_ref, c_ref, ga_ref, gb_ref, gc_ref, bg_ref, m_ref):
        m_ref[...] = (_sigmoid(ga_ref[...].astype(F32) + bg_ref[:, 0:D]) * a_ref[...].astype(F32)
                      + _sigmoid(gb_ref[...].astype(F32) + bg_ref[:, D:2 * D]) * b_ref[...].astype(F32)
                      + _sigmoid(gc_ref[...].astype(F32) + bg_ref[:, 2 * D:3 * D]) * c_ref[...].astype(F32)).astype(BF16)

    rt = _big_rows(t)
    row = pl.BlockSpec((rt, D), lambda i: (i, 0))
    gates = [pl.BlockSpec((rt, D), functools.partial(lambda i, b: (i, b), b=mgb + j)) for j in range(3)]
    return pl.pallas_call(
        body, name=name, grid=(t // rt,),
        in_specs=[row, row, row, *gates, pl.BlockSpec((1, 3 * D), lambda i: (0, 0))],
        out_specs=row, out_shape=jax.ShapeDtypeStruct((t, D), BF16), compiler_params=_cparams(1, 40),
    )(bra, brb, brc, p, p, p, b_gate)


def _merge_bwd(dm, bra, brb, brc, p, b_gate, name):
    t = p.shape[0]
    mgb = OFF["mg"] // D

    def body(dm_ref, a_ref, b_ref, c_ref, ga_ref, gb_ref, gc_ref, bg_ref, da_ref, db_ref, dc_ref, dmg_ref, dbg_ref):
        i = pl.program_id(0)

        @pl.when(i == 0)
        def _():
            dbg_ref[...] = jnp.zeros_like(dbg_ref)

        dm = dm_ref[...].astype(F32)
        for j, (br_ref, g_ref, d_ref) in enumerate(((a_ref, ga_ref, da_ref), (b_ref, gb_ref, db_ref), (c_ref, gc_ref, dc_ref))):
            sl = slice(j * D, (j + 1) * D)
            g = _sigmoid(g_ref[...].astype(F32) + bg_ref[:, sl])
            d_ref[...] = (dm * g).astype(BF16)
            dmg = dm * br_ref[...].astype(F32) * (g * (1.0 - g))
            dmg_ref[:, sl] = dmg.astype(BF16)
            dbg_ref[:, sl] += jnp.sum(dmg, axis=0, keepdims=True)

    rt = _big_rows(t)
    row = pl.BlockSpec((rt, D), lambda i: (i, 0))
    gates = [pl.BlockSpec((rt, D), functools.partial(lambda i, b: (i, b), b=mgb + j)) for j in range(3)]
    return pl.pallas_call(
        body, name=name, grid=(t // rt,),
        in_specs=[row, row, row, row, *gates, pl.BlockSpec((1, 3 * D), lambda i: (0, 0))],
        out_specs=(row, row, row, _dp_spec("merge", rt), pl.BlockSpec((8, 3 * D), lambda i: (0, 0))),
        out_shape=(jax.ShapeDtypeStruct((t, D), BF16),) * 3 + (jax.ShapeDtypeStruct((t, NP), BF16),
                                                                jax.ShapeDtypeStruct((8, 3 * D), F32)),
        compiler_params=_cparams(1, 56),
    )(dm, bra, brb, brc, p, p, p, b_gate)


def _adam_update(ns, g_ref, w_ref, m_ref, v_ref, go_ref, d_ref, mo_ref, vo_ref):
    g = g_ref[0].astype(F32)
    for s in range(1, ns):
        g = g + g_ref[s].astype(F32)
    mn = ADAM_B1 * m_ref[...] + (1.0 - ADAM_B1) * g
    vn = ADAM_B2 * v_ref[...] + (1.0 - ADAM_B2) * jnp.square(g)
    m_hat = mn / (1.0 - ADAM_B1 ** ADAM_STEP)
    v_hat = vn / (1.0 - ADAM_B2 ** ADAM_STEP)
    go_ref[...] = g
    d_ref[...] = -ADAM_LR * (m_hat / (jnp.sqrt(v_hat) + ADAM_EPS) + ADAM_WD * w_ref[...])
    mo_ref[...] = mn
    vo_ref[...] = vn


def _adamw(gsrc, w, m, v, name):
    ns, nl, r, c = gsrc.shape
    gb = gsrc.dtype.itemsize

    def fits(rows, cols):
        lanes = -(-cols // LANE) * LANE
        return ns * rows * lanes * gb <= ADAM_SRC_BYTES and rows * lanes * 4 <= ADAM_ROW_BYTES

    tr, tc = r, c
    if not fits(r, c):
        rows = [cand for cand in range(16, r, 16) if r % cand == 0 and fits(cand, c)]
        cols = [cand for cand in range(LANE, c, LANE) if c % cand == 0 and fits(r, cand)]
        if rows:
            tr = rows[-1]
        else:
            tc = cols[-1]

    def body(*refs):
        _adam_update(ns, *refs)

    row = pl.BlockSpec((None, tr, tc), lambda l, i, j: (l, i, j))
    return pl.pallas_call(
        body, name=name, grid=(nl, r // tr, c // tc),
        in_specs=[pl.BlockSpec((ns, None, tr, tc), lambda l, i, j: (0, l, i, j)), row, row, row],
        out_specs=(row,) * 4, out_shape=(jax.ShapeDtypeStruct((nl, r, c), F32),) * 4,
        compiler_params=_cparams(3, 48),
    )(gsrc, w, m, v)


def _pair_sum(a, b, name):
    s, r, c = a.shape
    tc = _pick(c, (256, 128))

    def body(a_ref, b_ref, o_ref):
        o_ref[...] = (a_ref[...].astype(F32) + b_ref[...].astype(F32)).astype(BF16)

    blk = pl.BlockSpec((None, r, tc), lambda i, j: (i, 0, j))
    return pl.pallas_call(
        body, name=name, grid=(s, c // tc), in_specs=[blk, blk], out_specs=blk,
        out_shape=jax.ShapeDtypeStruct(a.shape, BF16), compiler_params=_cparams(2, 32),
    )(a, b)


def _adamw_small(items, name):
    k = len(items)

    def body(*refs):
        for j in range(k):
            _adam_update(items[j][0].shape[0], *refs[4 * j:4 * j + 4], *refs[4 * k + 4 * j:4 * k + 4 * j + 4])

    out = pl.pallas_call(
        body, name=name,
        out_shape=tuple(jax.ShapeDtypeStruct(w.shape, F32) for _, w, _, _ in items for _ in range(4)),
    )(*[a for item in items for a in item])
    return [out[4 * j:4 * j + 4] for j in range(k)]


def _rope_tables(ctx, seq):
    n_rows = seq // GRID_W
    pairs = HD // 4
    row = jnp.repeat(jnp.arange(n_rows, dtype=F32), GRID_W)
    col = jnp.tile(jnp.arange(GRID_W, dtype=F32), n_rows)
    freqs = ROPE_THETA ** (-jnp.arange(pairs, dtype=F32) * 2.0 / (HD // 2))
    ar, ac = row[:, None] * freqs, col[:, None] * freqs
    cos_l = jnp.concatenate([jnp.cos(ar), jnp.cos(ar), jnp.cos(ac), jnp.cos(ac)], axis=1)
    sin_l = jnp.concatenate([-jnp.sin(ar), jnp.sin(ar), -jnp.sin(ac), jnp.sin(ac)], axis=1)
    cos_t = jnp.concatenate([jnp.ones((ctx, HD), F32), cos_l], axis=0)
    sin_t = jnp.concatenate([jnp.zeros((ctx, HD), F32), sin_l], axis=0)
    return cos_t, sin_t


def _to_proj_layout(wt):
    parts = [wt[s:s + wd] for _, s, wd in _SEGS]
    used = sum(wd for _, _, wd in _SEGS)
    parts.append(jnp.zeros((NP - used, wt.shape[1]), wt.dtype))
    return jnp.concatenate(parts, axis=0)


def _from_proj_layout(g):
    order = sorted(_SEGS, key=lambda sg: sg[1])
    return jnp.concatenate([g[OFF[n]:OFF[n] + wd] for n, _, wd in order], axis=0)


def _row0(a):
    return a[..., 0, :]


def kernel(x, c, ctx, c_ctx, w_ada, b_ada, g_pre, g_post, w_in, conv_w, q_norm_g, k_norm_g, w_decay_fwd, b_decay_fwd, w_decay_bwd, b_decay_bwd, gla_norm_g, w_br_conv, w_br_attn, w_br_gla, b_gate, w_out, loss_target, m_c_ctx, m_w_ada, m_b_ada, m_g_pre, m_g_post, m_w_in, m_conv_w, m_q_norm_g, m_k_norm_g, m_w_decay_fwd, m_b_decay_fwd, m_w_decay_bwd, m_b_decay_bwd, m_gla_norm_g, m_w_br_conv, m_w_br_attn, m_w_br_gla, m_b_gate, m_w_out, v_c_ctx, v_w_ada, v_b_ada, v_g_pre, v_g_post, v_w_in, v_conv_w, v_q_norm_g, v_k_norm_g, v_w_decay_fwd, v_b_decay_fwd, v_w_decay_bwd, v_b_decay_bwd, v_gla_norm_g, v_w_br_conv, v_w_br_attn, v_w_br_gla, v_b_gate, v_w_out):
    seq, n_ctx = x.shape[1], ctx.shape[1]
    assert n_ctx % TM == 0 and seq % TM == 0 and seq % GRID_W == 0
    t = n_ctx + seq
    nct = n_ctx // TM
    dev = 4 * lax.axis_index("x") + 2 * lax.axis_index("y") + lax.axis_index("c")
    ada_w = w_ada.shape[2]
    in_w = w_in.shape[2]
    br_r = w_br_conv.shape[1]

    def in_t(a, l):
        return a.transpose(2, 0, 1)[:, l, :]

    wb = [w.astype(BF16) for w in (w_ada, w_br_conv, w_br_attn, w_br_gla, w_out)]
    wall = _all_gather([wb[0][0], in_t(w_in, 0).astype(BF16), conv_w, w_decay_fwd, w_decay_bwd],
                       "gather_first")
    later_square = _GatherRider([wb[1], wb[2], wb[3], wb[4]])
    later_in = _GatherRider([in_t(w_in, 1).astype(BF16), wb[0][1]])

    def full_small(g):
        return g.transpose(1, 2, 0, 3).reshape(DEPTH, g.shape[2], NDEV * g.shape[3])

    def full_in(g):
        return _to_proj_layout(g.reshape(IN_WIDTH, D))

    def full_ada(g):
        return g.transpose(1, 0, 2).reshape(D, 3 * D)

    w_ada_f = [full_ada(wall[0]), None]
    wp = [full_in(wall[1]), None]
    conv_f, wdf_f, wdb_f = full_small(wall[2]), full_small(wall[3]), full_small(wall[4])

    cos_t, sin_t = _rope_tables(n_ctx, seq)
    cc = jnp.concatenate([c_ctx[None, :], c.reshape(1, D), jnp.zeros((6, D), F32)], axis=0)
    silu_cc, dsilu_cc = _ada_in(cc)

    conv8, wd_pad, bd = [], [], []
    for l in range(DEPTH):
        conv8.append(jnp.concatenate([conv_f[l], jnp.zeros((5, D), F32)], axis=0))
        zr = jnp.zeros((GLA_RANK, GH * GDK), F32)
        wd_pad.append(jnp.concatenate([jnp.concatenate([wdf_f[l], zr], axis=1), jnp.concatenate([zr, wdb_f[l]], axis=1),
                                       jnp.zeros((R_PAD - 2 * GLA_RANK, D), F32)], axis=0))
        bd.append(jnp.concatenate([b_decay_fwd[l], b_decay_bwd[l]])[None, :])

    xs = jnp.concatenate([ctx[0], x[0]], axis=0)
    saved = []
    for l in range(DEPTH):
        n = f"l{l}_"
        mod = _mm(silu_cc, w_ada_f[l], n + "mod", bias=b_ada[l][None, :])
        mod3 = mod[0:2].reshape(2, 3, D)
        h = _prenorm_fwd(xs, g_pre[l][None, :], mod3, nct, n + "prenorm")
        if l == 0:
            p, *got = _mm(h, wp[l], n + "proj", tb=True, out_dtype=BF16, tm=t // 2, rider=later_square)
            w_brs_f = [g.transpose(1, 0, 2, 3).reshape(DEPTH, D, D) for g in got]
        else:
            p = _mm(h, wp[l], n + "proj", tb=True, out_dtype=BF16, tm=t // 2)
        cv, ya = _conv_fwd(p, conv8[l], nct, n + "conv")
        qr, kr = _qk_prep_fwd(p, q_norm_g[l][None, :], k_norm_g[l][None, :], cos_t, sin_t, n + "qk_prep")
        att, lse, *got = _attn_fwd(qr, kr, p, nct, n + "attn", rider=later_in if l == 0 else None)
        if l == 0:
            wp[1], w_ada_f[1] = full_in(got[0]), full_ada(got[1])
        z, la = _decay_fwd(p, wd_pad[l], bd[l], n + "decay")
        of, stf, ob, stb = _gla_fwd(p, la, nct, n + "gla")
        yb, yc = _branch_fwd(att, of, ob, p, gla_norm_g[l][None, :], n + "branch")
        bra = _mm(ya, w_brs_f[0][l], n + "br_conv", out_dtype=BF16)
        brb = _mm(yb, w_brs_f[1][l], n + "br_attn", out_dtype=BF16)
        brc = _mm(yc, w_brs_f[2][l], n + "br_gla", out_dtype=BF16)
        mm_ = _merge_fwd(bra, brb, brc, p, b_gate[l][None, :], n + "merge")
        out = _mm(mm_, w_brs_f[3][l], n + "out")
        x_new = _post_fwd(xs, out, g_post[l][None, :], mod3, nct, n + "post")
        saved.append(dict(x=xs, mod3=mod3, h=h, p=p, cv=cv, ya=ya, qr=qr, kr=kr, att=att, lse=lse, z=z, la=la, of=of, ob=ob,
                          stf=stf, stb=stb, yb=yb, yc=yc, bra=bra, brb=brb, brc=brc, m=mm_, out=out))
        xs = x_new

    dx, sq = _loss_grad(xs, loss_target[0], nct, "loss")
    loss = lax.psum(0.5 * sq[0, 0] / D, ("x", "y", "c"))

    gw = {k: [None] * DEPTH for k in ("w_in", "br_conv", "br_attn", "br_gla", "out", "b_gate", "g_pre", "g_post",
                                      "conv_w", "qg", "kg", "wd", "bdec", "gla_g", "dmod")}
    dctx = []

    def in_slots(l):
        return _from_proj_layout(gw["w_in"][l]).reshape(NDEV, in_w, D)

    def br_slots(l):
        return [gw[k][l].reshape(NDEV, br_r, D) for k in ("br_conv", "br_attn", "br_gla", "out")]

    for l in reversed(range(DEPTH)):
        n = f"l{l}_b_"
        s = saved[l]
        p = s["p"]
        d_out, dgt, gw["g_post"][l] = _post_bwd(dx, s["out"], g_post[l][None, :], s["mod3"], nct, n + "post")
        dm = _mm(d_out, w_brs_f[3][l], n + "dm", tb=True, out_dtype=BF16)
        gw["out"][l] = _mm(s["m"], d_out, n + "dw_out", ta=True, out_dtype=BF16)
        dbra, dbrb, dbrc, dp, gw["b_gate"][l] = _merge_bwd(dm, s["bra"], s["brb"], s["brc"], p, b_gate[l][None, :], n + "merge")
        dya = _mm(dbra, w_brs_f[0][l], n + "dya", tb=True, out_dtype=BF16)
        dyb = _mm(dbrb, w_brs_f[1][l], n + "dyb", tb=True, out_dtype=BF16)
        dyc = _mm(dbrc, w_brs_f[2][l], n + "dyc", tb=True, out_dtype=BF16)
        gw["br_conv"][l] = _mm(s["ya"], dbra, n + "dw_conv", ta=True, out_dtype=BF16)
        gw["br_attn"][l] = _mm(s["yb"], dbrb, n + "dw_attn", ta=True, out_dtype=BF16)
        gw["br_gla"][l] = _mm(s["yc"], dbrc, n + "dw_gla", ta=True, out_dtype=BF16)
        dcv, dp = _conv_bwd_a(dya, p, s["cv"], dp, n + "conv_a")
        dp, gw["conv_w"][l] = _conv_bwd_b(dcv, p, conv8[l], nct, dp, n + "conv_b")
        datt, dgo, dp, gw["gla_g"][l] = _branch_bwd(dyb, dyc, s["att"], s["of"], s["ob"], p, gla_norm_g[l][None, :], dp, n + "branch")
        ex1 = _ExchangeRider([in_slots(DEPTH - 1)] + br_slots(DEPTH - 1)) if l == 0 else None
        dqr, dkr, dv, *got = _attn_bwd(s["qr"], s["kr"], p, s["att"], s["lse"], datt, nct, n + "attn", rider=ex1)
        if l == 0:
            recv_in1, recv_br1 = got[0], got[1:]
        dp, dk, gw["qg"][l], gw["kg"][l] = _qk_prep_bwd(dqr, dkr, p, q_norm_g[l][None, :], k_norm_g[l][None, :], cos_t, sin_t, dp, n + "qk_prep")
        gf, gb = _gla_bwd(p, s["la"], dgo, s["stf"], s["stb"], nct, n + "gla")
        dp, dr, gw["bdec"][l], gw["wd"][l] = _gla_merge_bwd(gf, gb, s["z"], p, wd_pad[l], dp, n + "gla_merge")
        dp = _dp_tail(dk, dv, dr, dp, n + "dp_tail")
        tk_in = t // 2 if t % 32 == 0 else None
        if l == 0:
            gw["w_in"][l], *recv_br0 = _mm(dp, s["h"], n + "dw_in", ta=True, out_dtype=BF16, tk=tk_in,
                                           rider=_ExchangeRider(br_slots(0)))
        else:
            gw["w_in"][l] = _mm(dp, s["h"], n + "dw_in", ta=True, out_dtype=BF16, tk=tk_in)
        if l == 0:
            core = lax.axis_index("c")
            halves = in_slots(0).reshape(NDEV // 2, 2, in_w, D)
            kept = lax.dynamic_index_in_dim(halves, core, axis=1, keepdims=False)
            sent = lax.dynamic_index_in_dim(halves, 1 - core, axis=1, keepdims=False)
            from_sibling, = _comm_alone(_SwapRider([sent]), n + "swap_dw_in")
            chip_sum = _pair_sum(kept, from_sibling, n + "chip_sum_dw_in")
            dh, recv_in0 = _mm(dp, wp[l], n + "dh", tk=NP // 4, rider=_ExchangeRider([chip_sum], chips_only=True))
        else:
            dh = _mm(dp, wp[l], n + "dh", tk=NP // 4)
        dx, dsh, dsc, gw["g_pre"][l] = _prenorm_bwd(dh, s["x"], dx, g_pre[l][None, :], s["mod3"], nct, n + "prenorm")
        dmod = jnp.stack([_row0(dsh), _row0(dsc), _row0(dgt)], axis=1).reshape(2, 3 * D)
        gw["dmod"][l] = dmod
        dmod8 = jnp.concatenate([dmod, jnp.zeros((6, 3 * D), F32)], axis=0)
        dctx.append(_mm(dmod8, w_ada_f[l], n + "dsilu", tb=True))
    grad_x = dx[n_ctx:][None]
    g_cctx = _cctx_grad(dctx[0], dctx[1], dsilu_cc)[0]

    def st2(name):
        return jnp.stack(gw[name])

    g_b_ada = jnp.stack([gw["dmod"][l][0] + gw["dmod"][l][1] for l in range(DEPTH)])
    g_bdf = jnp.stack([gw["bdec"][l][0, :GH * GDK] for l in range(DEPTH)])
    g_bdb = jnp.stack([gw["bdec"][l][0, GH * GDK:] for l in range(DEPTH)])
    g_wdf = jnp.stack([gw["wd"][l][0:GLA_RANK, :GH * GDK] for l in range(DEPTH)])
    g_wdb = jnp.stack([gw["wd"][l][GLA_RANK:2 * GLA_RANK, GH * GDK:] for l in range(DEPTH)])
    rep_grads = [g_cctx, g_b_ada, st2("g_pre")[:, 0], st2("g_post")[:, 0], st2("qg")[:, 0], st2("kg")[:, 0], g_bdf, g_bdb,
                 st2("gla_g")[:, 0], st2("b_gate")[:, 0]]
    rep_w = [c_ctx, b_ada, g_pre, g_post, q_norm_g, k_norm_g, b_decay_fwd, b_decay_bwd, gla_norm_g, b_gate]
    rep_m = [m_c_ctx, m_b_ada, m_g_pre, m_g_post, m_q_norm_g, m_k_norm_g, m_b_decay_fwd, m_b_decay_bwd, m_gla_norm_g, m_b_gate]
    rep_v = [v_c_ctx, v_b_ada, v_g_pre, v_g_post, v_q_norm_g, v_k_norm_g, v_b_decay_fwd, v_b_decay_bwd, v_gla_norm_g, v_b_gate]
    def two_d(a):
        return a.reshape(1, -1) if a.ndim == 1 else a

    def owner_slots(g):
        return g.reshape(DEPTH, g.shape[1], NDEV, g.shape[2] // NDEV).transpose(2, 0, 1, 3)

    n_rep = len(rep_grads)
    small = _comm_alone(_Riders([
        _GatherRider([two_d(g) for g in rep_grads] + [silu_cc[0:2], jnp.stack(gw["dmod"])]),
        _ExchangeRider([owner_slots(st2("conv_w")[:, 0:3]), owner_slots(g_wdf), owner_slots(g_wdb)])]),
        "exchange_small_grads")
    rep_src, (a_all, d_all), sh_src = small[:n_rep], small[n_rep:n_rep + 2], small[n_rep + 2:]
    sh_w = [conv_w, w_decay_fwd, w_decay_bwd]
    sh_m = [m_conv_w, m_w_decay_fwd, m_w_decay_bwd]
    sh_v = [v_conv_w, v_w_decay_fwd, v_w_decay_bwd]
    small_out = _adamw_small(
        [(g, two_d(w), two_d(m), two_d(v)) for g, w, m, v in zip(rep_src, rep_w, rep_m, rep_v)]
        + list(zip(sh_src, sh_w, sh_m, sh_v)), "adam_small")
    rep_g, rep_d, rep_nm, rep_nv = [[small_out[j][k].reshape(rep_w[j].shape) for j in range(n_rep)] for k in range(4)]
    sh_gr, sh_d, sh_nm, sh_nv = [[small_out[n_rep + j][k] for j in range(len(sh_w))] for k in range(4)]

    a_all = a_all.reshape(NDEV * 2, D)
    d_all = d_all.transpose(1, 0, 2, 3).reshape(DEPTH, NDEV * 2, 3 * D)
    g_ada = jnp.stack([_mm(a_all, lax.dynamic_slice_in_dim(d_all[l], dev * ada_w, ada_w, axis=1), f"dw_ada{l}",
                           ta=True, precise=True, tk=NDEV * 2) for l in range(DEPTH)])
    ada_g, ada_d, ada_nm, ada_nv = _adamw(g_ada[None], w_ada, m_w_ada, v_w_ada, "adam_ada")

    big_w = [w_br_conv, w_br_attn, w_br_gla, w_out]
    big_m = [m_w_br_conv, m_w_br_attn, m_w_br_gla, m_w_out]
    big_v = [v_w_br_conv, v_w_br_attn, v_w_br_gla, v_w_out]
    big_out = [_adamw(jnp.stack([recv_br0[j], recv_br1[j]], axis=1), big_w[j], big_m[j], big_v[j], f"adam_big{j}")
               for j in range(len(big_w))]
    in_out = [_adamw(r_[:, None], in_t(w_in, l)[None], in_t(m_w_in, l)[None], in_t(v_w_in, l)[None], f"adam_in{l}")
              for l, r_ in enumerate((recv_in0, recv_in1))]
    in_res = [jnp.stack([in_out[l][k][0] for l in range(DEPTH)], axis=1).transpose(1, 2, 0) for k in range(4)]
    big_g, big_d, big_nm, big_nv = [[in_res[k]] + [o[k] for o in big_out] for k in range(4)]

    def ordered(rep, ada, big, sh):
        c_ctx_, b_ada_, g_pre_, g_post_, qg_, kg_, bdf_, bdb_, glag_, bgate_ = rep
        w_in_, brc_, bra_, brg_, wout_ = big
        conv_, wdf_, wdb_ = sh
        return [c_ctx_, ada, b_ada_, g_pre_, g_post_, w_in_, conv_, qg_, kg_, wdf_, bdf_, wdb_, bdb_, glag_,
                brc_, bra_, brg_, bgate_, wout_]

    return (loss, grad_x,
            *ordered(rep_g, ada_g, big_g, sh_gr), *ordered(rep_d, ada_d, big_d, sh_d),
            *ordered(rep_nm, ada_nm, big_nm, sh_nm), *ordered(rep_nv, ada_nv, big_nv, sh_nv))
```
